```python
import jax, jax.numpy as jnp
from jax import lax
import numpy as np

D_MODEL = 1024
BATCH = 32
SEQ = 256
DEPTH = 2
DEC_BATCH = 4
DEC_SEQ = 1024
PAST_LEN = 256

GRID_W = 64
N_EVEN = (DEPTH + 1) // 2
N_ODD = DEPTH // 2
HEADS_A = 4
KEY_A = 128
VAL_A = (D_MODEL // 2) // HEADS_A
HGRN_CHUNK = 16
HEAD_DIM_B = 128
HEADS_B = (D_MODEL // 2) // HEAD_DIM_B
KV_HEADS_B = HEADS_B // 2
Q_BLOCK = 128
ROPE_BASE = 10000.0
HEADS_C = 4
KEY_C = D_MODEL // HEADS_C
VAL_C = 2 * KEY_C
RET_CHUNK = 128
N_GROUPS = 4
EXPERTS_PER_GROUP = 8
N_EXPERTS = N_GROUPS * EXPERTS_PER_GROUP
TOP_K_INNER = 2
D_EXPERT = D_MODEL // 4
EPS = 1e-6

EVEN_SPLITS = (HEADS_A * KEY_A, HEADS_A * VAL_A, HEADS_A * KEY_A, HEADS_A * KEY_A, HEADS_A * VAL_A,
               HEADS_B * HEAD_DIM_B, KV_HEADS_B * HEAD_DIM_B, KV_HEADS_B * HEAD_DIM_B)
EVEN_MIX = HEADS_A * VAL_A + HEADS_B * HEAD_DIM_B
ODD_SPLITS = (HEADS_C * KEY_C, HEADS_C * KEY_C, HEADS_C * VAL_C, HEADS_C * VAL_C)
ODD_MIX = HEADS_C * VAL_C

kernel_name = 'hybrid_flow_backbone_step'

F32 = jnp.float32


def split_cols(p, sizes):
    return jnp.split(p, [int(s) for s in np.cumsum(sizes)[:-1]], axis=-1)


def rms_norm(x, g):
    xf = x.astype(F32)
    y = xf * lax.rsqrt(jnp.mean(xf * xf, axis=-1, keepdims=True) + EPS)
    return (y * g.astype(F32)).astype(x.dtype)


def head_layer_norm(x, g):
    xf = x.astype(F32)
    mu = jnp.mean(xf, axis=-1, keepdims=True)
    var = jnp.mean(jnp.square(xf - mu), axis=-1, keepdims=True)
    return (xf - mu) * lax.rsqrt(var + EPS) * g.astype(F32)


def ada_mod(cvec, w, b):
    m = jax.nn.silu(cvec.astype(F32)) @ w.astype(F32) + b.astype(F32)
    return tuple(jnp.split(m[:, None, :], 6, axis=-1))


def modulate(h, shift, scale):
    return (h.astype(F32) * (1.0 + scale) + shift).astype(h.dtype)


def flip(t):
    return jnp.flip(t, axis=1)


def axial_rope(x):
    b, s, h, dh = x.shape
    rows = s // GRID_W
    row = jnp.repeat(jnp.arange(rows), GRID_W).astype(F32)
    col = jnp.tile(jnp.arange(GRID_W), rows).astype(F32)
    half = dh // 2
    n_freq = half // 2
    inv = ROPE_BASE ** (-jnp.arange(n_freq, dtype=F32) / n_freq)
    ang = jnp.concatenate([row[:, None] * inv, col[:, None] * inv], axis=-1)[None, :, None, :]
    cos, sin = jnp.cos(ang), jnp.sin(ang)
    xf = x.astype(F32)
    x1, x2 = xf[..., :half], xf[..., half:]
    return jnp.concatenate([x1 * cos - x2 * sin, x2 * cos + x1 * sin], axis=-1).astype(x.dtype)


def gqa_blocked(q, k, v):
    b, s, h, d = q.shape
    kvh = k.shape[2]
    grp = h // kvh
    qb = jnp.moveaxis(q.astype(F32).reshape(b, s // Q_BLOCK, Q_BLOCK, kvh, grp, d), 1, 0)
    kf, vf = k.astype(F32), v.astype(F32)
    scale = d ** -0.5

    def one_block(qblk):
        p = jax.nn.softmax(jnp.einsum('bqkgd,btkd->bkgqt', qblk, kf) * scale, axis=-1)
        return jnp.einsum('bkgqt,btkd->bqkgd', p, vf)

    o = lax.map(one_block, qb)
    return jnp.moveaxis(o, 0, 1).reshape(b, s, h * d).astype(q.dtype)


def hgrn_gates(z, lb):
    z = z.astype(F32)
    logf = jnp.log(lb + (1.0 - lb) * jax.nn.sigmoid(z))
    key = (1.0 - lb) * jax.nn.sigmoid(-z)
    return logf, key


def hgrn2_chunked(q, k, v, logf, s0):
    b, t, h, kd = q.shape
    vd = v.shape[-1]
    L = HGRN_CHUNK
    n = t // L
    q, k, logf = (a.reshape(b, n, L, h, kd) for a in (q, k, logf))
    v = v.reshape(b, n, L, h, vd)
    cum = jnp.cumsum(logf, axis=2)
    causal = (jnp.arange(L)[:, None] >= jnp.arange(L)[None, :])[None, None, :, :, None, None]
    decay = jnp.exp(jnp.where(causal, cum[:, :, :, None] - cum[:, :, None, :], -jnp.inf))
    scores = jnp.einsum('bnthk,bnshk,bntshk->bnhts', q, k, decay)
    o = jnp.einsum('bnhts,bnshv->bnthv', scores, v)
    last = cum[:, :, -1]
    kv = jnp.einsum('bnshk,bnshv->bnhkv', k * jnp.exp(last[:, :, None] - cum), v)

    def step(state, inp):
        dec, kvc = inp
        return dec[..., None] * state + kvc, state

    s_fin, s_start = lax.scan(step, s0, (jnp.moveaxis(jnp.exp(last), 1, 0), jnp.moveaxis(kv, 1, 0)))
    o = o + jnp.einsum('bnthk,bnhkv->bnthv', q * jnp.exp(cum), jnp.moveaxis(s_start, 0, 1))
    return o.reshape(b, t, h, vd), s_fin


def retention_chunked(q, k, v, log_gamma, s0):
    b, t, h, kd = q.shape
    vd = v.shape[-1]
    L = RET_CHUNK
    n = t // L
    q = q.reshape(b, n, L, h, kd)
    k = k.reshape(b, n, L, h, kd)
    v = v.reshape(b, n, L, h, vd)
    j = jnp.arange(L, dtype=F32)
    rel = (j[:, None] - j[None, :])[None]
    dmat = jnp.exp(jnp.where(rel >= 0, rel * log_gamma[:, None, None], -jnp.inf))
    o = jnp.einsum('bnhts,bnshv->bnthv', jnp.einsum('bnthk,bnshk->bnhts', q, k) * dmat, v)
    k_dec = k * jnp.exp((L - 1.0 - j)[:, None] * log_gamma)[:, :, None]
    kv = jnp.einsum('bnshk,bnshv->bnhkv', k_dec, v)
    chunk_dec = jnp.exp(L * log_gamma)[:, None, None]

    def step(state, kvc):
        return chunk_dec * state + kvc, state

    s_fin, s_start = lax.scan(step, s0, jnp.moveaxis(kv, 1, 0))
    q_dec = q * jnp.exp((j + 1.0)[:, None] * log_gamma)[:, :, None]
    o = o + jnp.einsum('bnthk,bnhkv->bnthv', q_dec, jnp.moveaxis(s_start, 0, 1))
    return o.reshape(b, t, h, vd), s_fin


def even_mixer(h, w_in, w_out, lb, hgrn_g, qn_g, kn_g, ctx=None):
    b, s, _ = h.shape
    qa, ia, zf, zb, ga, qb, kb, vb = split_cols(h @ w_in, EVEN_SPLITS)
    qa = jax.nn.silu(qa.astype(F32)).reshape(b, s, HEADS_A, KEY_A) * KEY_A ** -0.5
    ia = ia.astype(F32).reshape(b, s, HEADS_A, VAL_A)
    lb = lb.reshape(2, HEADS_A, KEY_A)
    logf_f, key_f = hgrn_gates(zf.reshape(b, s, HEADS_A, KEY_A), lb[0])
    logf_b, key_b = hgrn_gates(zb.reshape(b, s, HEADS_A, KEY_A), lb[1])
    if ctx is None:
        s0 = jnp.zeros((b, 2, HEADS_A, KEY_A, VAL_A), F32)
    else:
        s0 = ctx[2].astype(F32)
    o_f, s_f = hgrn2_chunked(qa, key_f, ia, logf_f, s0[:, 0])
    o_b, s_b = hgrn2_chunked(flip(qa), flip(key_b), flip(ia), flip(logf_b), s0[:, 1])
    oa = rms_norm(o_f + flip(o_b), hgrn_g).reshape(b, s, HEADS_A * VAL_A) * jax.nn.silu(ga.astype(F32))
    qb = rms_norm(qb.reshape(b, s, HEADS_B, HEAD_DIM_B), qn_g)
    kb = rms_norm(kb.reshape(b, s, KV_HEADS_B, HEAD_DIM_B), kn_g)
    vb = vb.reshape(b, s, KV_HEADS_B, HEAD_DIM_B)
    if ctx is None:
        ob = gqa_blocked(qb, kb, vb)
    else:
        k_all = jnp.concatenate([ctx[0].astype(kb.dtype), axial_rope(kb)], axis=1)
        v_all = jnp.concatenate([ctx[1].astype(vb.dtype), vb], axis=1)
        ob = gqa_blocked(axial_rope(qb), k_all, v_all)
    out = jnp.concatenate([oa.astype(h.dtype), ob.astype(h.dtype)], axis=-1) @ w_out
    if ctx is None:
        return out, (kb, vb, jnp.stack([s_f, s_b], axis=1).astype(h.dtype))
    return out, None


def odd_mixer(h, w_in, w_out, decay_logit, gn_g, ctx=None):
    b, s, _ = h.shape
    q, k, v, g = split_cols(h @ w_in, ODD_SPLITS)
    q = q.reshape(b, s, HEADS_C, KEY_C)
    k = k.reshape(b, s, HEADS_C, KEY_C) * KEY_C ** -0.5
    if ctx is not None:
        q, k = axial_rope(q), axial_rope(k)
    q, k = q.astype(F32), k.astype(F32)
    v = v.astype(F32).reshape(b, s, HEADS_C, VAL_C)
    log_gamma = jax.nn.log_sigmoid(decay_logit.astype(F32))
    if ctx is None:
        s0 = jnp.zeros((b, 2, HEADS_C, KEY_C, VAL_C), F32)
    else:
        s0 = ctx.astype(F32)
    o_f, s_f = retention_chunked(q, k, v, log_gamma[0], s0[:, 0])
    o_b, s_b = retention_chunked(flip(q), flip(k), flip(v), log_gamma[1], s0[:, 1])
    y = head_layer_norm(o_f + flip(o_b), gn_g).reshape(b, s, ODD_MIX)
    out = (jax.nn.silu(g.astype(F32)) * y).astype(h.dtype) @ w_out
    if ctx is None:
        return out, jnp.stack([s_f, s_b], axis=1).astype(h.dtype)
    return out, None


def hier_moe(h, w_group, b_group, w_expert, b_expert, w_gate, w_up, w_down):
    b, s, d = h.shape
    x = h.reshape(b * s, d)
    xf = x.astype(F32)
    g_prob = jax.nn.softmax(xf @ w_group.astype(F32) + b_group.astype(F32), axis=-1)
    p_grp, g_idx = lax.top_k(g_prob, 1)
    e_logit = (xf @ w_expert.astype(F32) + b_expert.astype(F32)).reshape(-1, N_GROUPS, EXPERTS_PER_GROUP)
    e_logit = jnp.einsum('tge,tg->te', e_logit, jax.nn.one_hot(g_idx[:, 0], N_GROUPS, dtype=F32))
    e_w, e_idx = lax.top_k(jax.nn.softmax(e_logit, axis=-1), TOP_K_INNER)
    e_w = e_w / jnp.sum(e_w, axis=-1, keepdims=True) * p_grp
    combine = jnp.sum(jax.nn.one_hot(g_idx * EXPERTS_PER_GROUP + e_idx, N_EXPERTS, dtype=F32)
                      * e_w[..., None], axis=1)
    act = jax.nn.silu(jnp.einsum('td,edf->tef', x, w_gate)) * jnp.einsum('td,edf->tef', x, w_up)
    act = act * combine[..., None].astype(act.dtype)
    y = jnp.einsum('tef,efd->td', act, w_down)
    return y.reshape(b, s, d).astype(h.dtype)


def setup_inputs(seed: int = 0) -> dict:
    key = jax.random.key(seed)
    ks = iter(jax.random.split(key, 32))

    def nrm(shape, scale):
        return jax.random.normal(next(ks), shape, F32) * scale

    D = D_MODEL
    decay_base = jnp.log(2.0 ** (5.0 + jnp.arange(HEADS_C, dtype=F32)) - 1.0)
    return {
        'x_prompt': nrm((BATCH, SEQ, D), 1.0),
        'x_sample': nrm((DEC_BATCH, DEC_SEQ, D), 1.0),
        'cache_attn_k': nrm((DEC_BATCH, N_EVEN, PAST_LEN, KV_HEADS_B, HEAD_DIM_B), 1.0),
        'cache_attn_v': nrm((DEC_BATCH, N_EVEN, PAST_LEN, KV_HEADS_B, HEAD_DIM_B), 1.0),
        'state_hgrn': nrm((DEC_BATCH, N_EVEN, 2, HEADS_A, KEY_A, VAL_A), 0.5),
        'state_ret': nrm((DEC_BATCH, N_ODD, 2, HEADS_C, KEY_C, VAL_C), 0.5),
        'c': nrm((DEC_BATCH, D), 1.0),
        'c_ctx': nrm((D,), 1.0),
        'mod_w': nrm((DEPTH, D, 6 * D), 0.5 * D ** -0.5),
        'mod_b': nrm((DEPTH, 6 * D), 0.02),
        'norm1_g': 1.0 + nrm((DEPTH, D), 0.05),
        'norm2_g': 1.0 + nrm((DEPTH, D), 0.05),
        'even_w_in': nrm((N_EVEN, D, sum(EVEN_SPLITS)), D ** -0.5),
        'even_w_out': nrm((N_EVEN, EVEN_MIX, D), EVEN_MIX ** -0.5),
        'hgrn_lb_logits': nrm((DEPTH + 1, 2, HEADS_A * KEY_A), 0.5),
        'hgrn_norm_g': 1.0 + nrm((N_EVEN, HEADS_A, VAL_A), 0.05),
        'attn_qn_g': 1.0 + nrm((N_EVEN, HEAD_DIM_B), 0.05),
        'attn_kn_g': 1.0 + nrm((N_EVEN, HEAD_DIM_B), 0.05),
        'odd_w_in': nrm((N_ODD, D, sum(ODD_SPLITS)), D ** -0.5),
        'odd_w_out': nrm((N_ODD, ODD_MIX, D), ODD_MIX ** -0.5),
        'ret_decay_logit': decay_base[None, None, :] + nrm((N_ODD, 2, HEADS_C), 0.1),
        'ret_gn_g': 1.0 + nrm((N_ODD, HEADS_C, VAL_C), 0.05),
        'moe_w_group': nrm((DEPTH, D, N_GROUPS), D ** -0.5),
        'moe_b_group': nrm((DEPTH, N_GROUPS), 0.01),
        'moe_w_expert': nrm((DEPTH, D, N_EXPERTS), D ** -0.5),
        'moe_b_expert': nrm((DEPTH, N_EXPERTS), 0.01),
        'moe_w_gate': nrm((DEPTH, N_EXPERTS, D, D_EXPERT), D ** -0.5),
        'moe_w_up': nrm((DEPTH, N_EXPERTS, D, D_EXPERT), D ** -0.5),
        'moe_w_down': nrm((DEPTH, N_EXPERTS, D_EXPERT, D), D_EXPERT ** -0.5),
        'final_norm_g': 1.0 + nrm((D,), 0.05),
    }


def reference(x_prompt, x_sample, cache_attn_k, cache_attn_v, state_hgrn, state_ret, c, c_ctx,
              mod_w, mod_b, norm1_g, norm2_g, even_w_in, even_w_out, hgrn_lb_logits, hgrn_norm_g,
              attn_qn_g, attn_kn_g, odd_w_in, odd_w_out, ret_decay_logit, ret_gn_g,
              moe_w_group, moe_b_group, moe_w_expert, moe_b_expert, moe_w_gate, moe_w_up, moe_w_down,
              final_norm_g):
    lower_bounds = jnp.cumsum(jax.nn.softmax(hgrn_lb_logits.astype(F32), axis=0), axis=0)
    xc, xs = x_prompt, x_sample
    new_k, new_v, new_sh, new_sr = [], [], [], []
    for l in range(DEPTH):
        mc = ada_mod(c_ctx[None], mod_w[l], mod_b[l])
        ms = ada_mod(c, mod_w[l], mod_b[l])
        hc = modulate(rms_norm(xc, norm1_g[l]), mc[0], mc[1])
        hs = modulate(rms_norm(xs, norm1_g[l]), ms[0], ms[1])
        if l % 2 == 0:
            e = l // 2
            args = (even_w_in[e], even_w_out[e], lower_bounds[l], hgrn_norm_g[e], attn_qn_g[e], attn_kn_g[e])
            oc, (kc, vc, sc) = even_mixer(hc, *args)
            os_, _ = even_mixer(hs, *args, ctx=(cache_attn_k[:, e], cache_attn_v[:, e], state_hgrn[:, e]))
            new_k.append(kc)
            new_v.append(vc)
            new_sh.append(sc)
        else:
            o = l // 2
            args = (odd_w_in[o], odd_w_out[o], ret_decay_logit[o], ret_gn_g[o])
            oc, sc = odd_mixer(hc, *args)
            os_, _ = odd_mixer(hs, *args, ctx=state_ret[:, o])
            new_sr.append(sc)
        xc = xc + (mc[2] * oc.astype(F32)).astype(xc.dtype)
        xs = xs + (ms[2] * os_.astype(F32)).astype(xs.dtype)
        moe_args = (moe_w_group[l], moe_b_group[l], moe_w_expert[l], moe_b_expert[l],
                    moe_w_gate[l], moe_w_up[l], moe_w_down[l])
        hc = modulate(rms_norm(xc, norm2_g[l]), mc[3], mc[4])
        hs = modulate(rms_norm(xs, norm2_g[l]), ms[3], ms[4])
        xc = xc + (mc[5] * hier_moe(hc, *moe_args).astype(F32)).astype(xc.dtype)
        xs = xs + (ms[5] * hier_moe(hs, *moe_args).astype(F32)).astype(xs.dtype)
    y_prompt = rms_norm(xc, final_norm_g)
    y_sample = rms_norm(xs, final_norm_g)
    new_attn_k = jnp.stack(new_k, axis=1)
    new_attn_v = jnp.stack(new_v, axis=1)
    new_state_hgrn = jnp.stack(new_sh, axis=1)
    new_state_ret = jnp.stack(new_sr, axis=1)
    return (y_prompt, y_sample, new_attn_k, new_attn_v, new_state_hgrn, new_state_ret)
```

```python
import functools

import numpy as np
import jax
import jax.numpy as jnp
from jax import lax
from jax.experimental import pallas as pl
from jax.experimental.pallas import tpu as pltpu

F32 = jnp.float32
BF16 = jnp.bfloat16
EPS = 1e-6
ROPE_BASE = 10000.0
GRID_W = 64
N_GROUPS = 4
TOP_K_INNER = 2
HGRN_BLOCK = 128
HGRN_DIAG = 8
RET_CHUNK = 256
TOKEN_TILE = 256
EXPERT_TILE = 256
V7X_VMEM_LIMIT_BYTES = 56 * 1024 * 1024
HIGHEST = lax.Precision.HIGHEST

_NT = (((1,), (1,)), ((), ()))


def _cparams(*sem):
    return pltpu.CompilerParams(dimension_semantics=sem, vmem_limit_bytes=V7X_VMEM_LIMIT_BYTES)


def _rms(x, g):
    return x * lax.rsqrt(jnp.mean(x * x, axis=-1, keepdims=True) + EPS) * g


def _silu(x):
    return x * jax.nn.sigmoid(x)


def _bdot(a, b):
    return jnp.dot(a.astype(BF16), b.astype(BF16), preferred_element_type=F32)


def _bdot_nt(a, b):
    return lax.dot_general(a.astype(BF16), b.astype(BF16), _NT, preferred_element_type=F32)


def _mod_kernel(c_ref, w_ref, b_ref, o_ref):
    a = _silu(c_ref[...])
    o_ref[0] = jnp.dot(a, w_ref[0], precision=HIGHEST, preferred_element_type=F32) + b_ref[0]


def _ada_mod(cvecs, mod_w, mod_b):
    depth, d, n = mod_w.shape
    tn = n // 4
    rows = cvecs.shape[0]
    return pl.pallas_call(
        _mod_kernel,
        grid=(depth, n // tn),
        in_specs=[pl.BlockSpec((rows, d), lambda l, j: (0, 0)),
                  pl.BlockSpec((1, d, tn), lambda l, j: (l, 0, j)),
                  pl.BlockSpec((1, 1, tn), lambda l, j: (l, 0, j))],
        out_specs=pl.BlockSpec((1, rows, tn), lambda l, j: (l, 0, j)),
        out_shape=jax.ShapeDtypeStruct((depth, rows, n), F32),
        compiler_params=_cparams("arbitrary", "arbitrary"),
        name="ada_mod",
    )(cvecs, mod_w, mod_b.reshape(depth, 1, n))


def _tile_mod_row(i, n_ctx_tiles, tiles_per_req):
    return jnp.where(i < n_ctx_tiles, 0, 1 + jnp.maximum(i - n_ctx_tiles, 0) // tiles_per_req)


def _inproj_kernel(x_ref, g_ref, mod_ref, w_ref, o_ref):
    h = _rms(x_ref[...], g_ref[...]) * (1.0 + mod_ref[0, 1:2, :]) + mod_ref[0, 0:1, :]
    o_ref[...] = jnp.dot(h.astype(BF16), w_ref[...], preferred_element_type=F32)


def _in_proj(x, g, mod, w_bf, n_ctx_tiles, tiles_per_req, tn):
    t, d = x.shape
    n = w_bf.shape[1]
    tm = TOKEN_TILE
    mrow = functools.partial(_tile_mod_row, n_ctx_tiles=n_ctx_tiles, tiles_per_req=tiles_per_req)
    return pl.pallas_call(
        _inproj_kernel,
        grid=(n // tn, t // tm),
        in_specs=[pl.BlockSpec((tm, d), lambda j, i: (i, 0)),
                  pl.BlockSpec((1, d), lambda j, i: (0, 0)),
                  pl.BlockSpec((1, 6, d), lambda j, i: (mrow(i), 0, 0)),
                  pl.BlockSpec((d, tn), lambda j, i: (0, j))],
        out_specs=pl.BlockSpec((tm, tn), lambda j, i: (i, j)),
        out_shape=jax.ShapeDtypeStruct((t, n), F32),
        compiler_params=_cparams("arbitrary", "arbitrary"),
        name="in_proj",
    )(x, g.reshape(1, d), mod, w_bf)


def _hgrn_block(q, k, v, logf, st_t, reverse):
    nb = HGRN_BLOCK
    row = lax.broadcasted_iota(jnp.int32, (nb, nb), 0)
    col = lax.broadcasted_iota(jnp.int32, (nb, nb), 1)
    tri = jnp.where((col >= row) if reverse else (col <= row), 1.0, 0.0).astype(BF16)
    hi = logf.astype(BF16)
    lo = (logf - hi.astype(F32)).astype(BF16)
    c = (jnp.dot(tri, hi, preferred_element_type=F32)
         + jnp.dot(tri, lo, preferred_element_type=F32))
    c_edge = c[0:1, :] if reverse else c[nb - 1:nb, :]

    o = _bdot_nt(q * jnp.exp(c), st_t)
    kd = k * jnp.exp(c_edge - c)
    st_new = st_t * jnp.exp(c_edge) + _bdot(v.T, kd)

    a = jnp.zeros((nb, nb), F32)
    m = nb // 2
    while m >= HGRN_DIAG:
        pieces = []
        for grp in range(nb // (2 * m)):
            idx = grp * 2 * m + (m if reverse else m - 1)
            pieces.append(jnp.broadcast_to(c[idx:idx + 1, :], (2 * m, nb)))
        bnd = pieces[0] if len(pieces) == 1 else jnp.concatenate(pieces, axis=0)
        e = jnp.exp(-jnp.abs(c - bnd))
        am = _bdot_nt(q * e, k * e)
        same = (row & ~(2 * m - 1)) == (col & ~(2 * m - 1))
        r_hi = (row & m) != 0
        c_hi = (col & m) != 0
        if reverse:
            mask = same & jnp.logical_not(r_hi) & c_hi
        else:
            mask = same & r_hi & jnp.logical_not(c_hi)
        a = jnp.where(mask, am, a)
        m //= 2

    dg = HGRN_DIAG
    rowi = lax.broadcasted_iota(jnp.int32, (dg, nb), 0)
    tiles = []
    for j in range(nb // dg):
        qj = q[j * dg:(j + 1) * dg, :]
        cj = c[j * dg:(j + 1) * dg, :]
        for s in range(dg):
            r = j * dg + s
            cs = jnp.broadcast_to(c[r:r + 1, :], (dg, nb))
            ks = jnp.broadcast_to(k[r:r + 1, :], (dg, nb))
            keep = (rowi <= s) if reverse else (rowi >= s)
            tiles.append(qj * ks * jnp.exp(jnp.where(keep, cj - cs, -jnp.inf)))
    p = jnp.concatenate(tiles, axis=0)
    red = jnp.dot(p.astype(BF16), jnp.ones((nb, nb), BF16), preferred_element_type=F32)
    coli = lax.broadcasted_iota(jnp.int32, (dg, nb), 1)
    rows_out = []
    for j in range(nb // dg):
        aj = a[j * dg:(j + 1) * dg, :]
        for s in range(dg):
            r = j * dg + s
            aj = jnp.where(coli == r, red[r * dg:(r + 1) * dg, :], aj)
        rows_out.append(aj)
    a = jnp.concatenate(rows_out, axis=0)
    o = o + _bdot(a, v)
    return o, st_new


def _hgrn_kernel(*refs, nblk, has_state, qscale):
    qa_ref, ia_ref, zf_ref, zb_ref, ga_ref, lb_ref, ng_ref = refs[:7]
    s0_ref = refs[7] if has_state else None
    o_ref, sfin_ref, oacc_ref = refs[-3:]
    nb = HGRN_BLOCK

    for direction in (0, 1):
        reverse = direction == 1
        z_ref = zb_ref if reverse else zf_ref
        lb = lb_ref[direction:direction + 1, :]
        if has_state:
            st0 = s0_ref[direction].T
        else:
            st0 = jnp.zeros((ia_ref.shape[1], qa_ref.shape[1]), F32)

        def body(i, st_t, reverse=reverse, z_ref=z_ref, lb=lb):
            blk = (nblk - 1 - i) if reverse else i
            r0 = pl.multiple_of(blk * nb, nb)
            q = _silu(qa_ref[pl.ds(r0, nb), :]) * qscale
            z = z_ref[pl.ds(r0, nb), :]
            logf = jnp.log(lb + (1.0 - lb) * jax.nn.sigmoid(z))
            k = (1.0 - lb) * jax.nn.sigmoid(-z)
            v = ia_ref[pl.ds(r0, nb), :]
            o, st_new = _hgrn_block(q, k, v, logf, st_t, reverse)
            if reverse:
                oacc_ref[pl.ds(r0, nb), :] += o
            else:
                oacc_ref[pl.ds(r0, nb), :] = o
            return st_new

        st_fin = lax.fori_loop(0, nblk, body, st0)
        sfin_ref[direction] = st_fin.T

    o = oacc_ref[...]
    o_ref[...] = _rms(o, ng_ref[0]) * _silu(ga_ref[...])


def _hgrn(proj, lb, norm_g, s0, layer_idx, out_prev, n_seq, seq, row_blk0, heads, key_d, val_d):
    t = proj.shape[0]
    has_state = s0 is not None
    hk = heads * key_d // 128
    off_qa, off_ia, off_zf, off_zb, off_ga = 0, hk, 2 * hk, 3 * hk, 4 * hk

    def col(off):
        return pl.BlockSpec((seq, 128), lambda b, h: (row_blk0 + b, off + h))

    in_specs = [col(off_qa), col(off_ia), col(off_zf), col(off_zb), col(off_ga),
                pl.BlockSpec((2, key_d), lambda b, h: (0, h)),
                pl.BlockSpec((1, 1, val_d), lambda b, h: (h, 0, 0))]
    args = [proj, proj, proj, proj, proj, lb, norm_g.reshape(heads, 1, val_d)]
    if has_state:
        in_specs.append(pl.BlockSpec((None, None, 2, None, key_d, val_d),
                                     lambda b, h: (b, layer_idx, 0, h, 0, 0)))
        args.append(s0)
    aliases = {}
    if out_prev is not None:
        in_specs.append(pl.BlockSpec(memory_space=pl.ANY))
        args.append(out_prev)
        aliases = {len(args) - 1: 0}

    kern = functools.partial(_hgrn_kernel, nblk=seq // HGRN_BLOCK, has_state=has_state,
                             qscale=float(key_d) ** -0.5)
    return pl.pallas_call(
        kern,
        grid=(n_seq, heads),
        in_specs=in_specs,
        out_specs=[pl.BlockSpec((seq, val_d), lambda b, h: (row_blk0 + b, h)),
                   pl.BlockSpec((None, 2, None, key_d, val_d), lambda b, h: (b, 0, h, 0, 0))],
        out_shape=[jax.ShapeDtypeStruct((t, heads * val_d), F32),
                   jax.ShapeDtypeStruct((n_seq, 2, heads, key_d, val_d), F32)],
        scratch_shapes=[pltpu.VMEM((seq, val_d), F32)],
        input_output_aliases=aliases,
        compiler_params=_cparams("arbitrary", "arbitrary"),
        name="hgrn2",
    )(*args)


def _rope(x, cos, sin_signed):
    return x * cos + pltpu.roll(x, x.shape[-1] // 2, 1) * sin_signed


def _attn_kernel(*refs, has_ctx, grp, qblk, scale):
    if has_ctx:
        (q_ref, k_ref, v_ref, qn_ref, kn_ref, ck_ref, cv_ref, cos_ref, sin_ref, _prev, o_ref) = refs
    else:
        (q_ref, k_ref, v_ref, qn_ref, kn_ref, o_ref, ko_ref, vo_ref) = refs
    seq, hd = k_ref.shape
    k = _rms(k_ref[...], kn_ref[...])
    v = v_ref[...]
    if has_ctx:
        k = _rope(k, cos_ref[...], sin_ref[...])
        ck = ck_ref[...].astype(BF16)
        cv = cv_ref[...].astype(BF16)
    else:
        ko_ref[...] = k
        vo_ref[...] = v
    kb = k.astype(BF16)
    vb = v.astype(BF16)

    for g in range(grp):
        def body(i, carry, g=g):
            r0 = pl.multiple_of(i * qblk, qblk)
            q = _rms(q_ref[pl.ds(r0, qblk), g * hd:(g + 1) * hd], qn_ref[...])
            if has_ctx:
                q = _rope(q, cos_ref[pl.ds(r0, qblk), :], sin_ref[pl.ds(r0, qblk), :])
            qb = q.astype(BF16)
            s2 = lax.dot_general(qb, kb, _NT, preferred_element_type=F32) * scale
            m = jnp.max(s2, axis=-1, keepdims=True)
            if has_ctx:
                s1 = lax.dot_general(qb, ck, _NT, preferred_element_type=F32) * scale
                m = jnp.maximum(m, jnp.max(s1, axis=-1, keepdims=True))
                p1 = jnp.exp(s1 - m)
            p2 = jnp.exp(s2 - m)
            den = jnp.sum(p2, axis=-1, keepdims=True)
            acc = jnp.dot(p2.astype(BF16), vb, preferred_element_type=F32)
            if has_ctx:
                den = den + jnp.sum(p1, axis=-1, keepdims=True)
                acc = acc + jnp.dot(p1.astype(BF16), cv, preferred_element_type=F32)
            o_ref[pl.ds(r0, qblk), g * hd:(g + 1) * hd] = acc / den
            return carry

        lax.fori_loop(0, seq // qblk, body, 0)


def _attn(proj, qn_g, kn_g, ctx, layer_idx, out_prev, n_seq, seq, row_blk0, heads, kv_heads, hd, col0):
    t = proj.shape[0]
    grp = heads // kv_heads
    has_ctx = ctx is not None
    cq = col0 // grp
    ck0 = col0 + heads
    cv0 = ck0 + kv_heads
    in_specs = [pl.BlockSpec((seq, grp * hd), lambda b, h: (row_blk0 + b, cq + h)),
                pl.BlockSpec((seq, hd), lambda b, h: (row_blk0 + b, ck0 + h)),
                pl.BlockSpec((seq, hd), lambda b, h: (row_blk0 + b, cv0 + h)),
                pl.BlockSpec((1, hd), lambda b, h: (0, 0)),
                pl.BlockSpec((1, hd), lambda b, h: (0, 0))]
    args = [proj, proj, proj, qn_g.reshape(1, hd), kn_g.reshape(1, hd)]
    out_specs = [pl.BlockSpec((seq, grp * hd), lambda b, h: (row_blk0 + b, h))]
    out_shape = [jax.ShapeDtypeStruct((t, heads * hd), F32)]
    aliases = {}
    if has_ctx:
        cache_k, cache_v, cos, sin_signed = ctx
        nreq, nlay, past = cache_k.shape[:3]
        cache_k = cache_k.reshape(nreq, nlay, past, kv_heads * hd)
        cache_v = cache_v.reshape(nreq, nlay, past, kv_heads * hd)
        spec_c = pl.BlockSpec((None, None, past, hd), lambda b, h: (b, layer_idx, 0, h))
        in_specs += [spec_c, spec_c,
                     pl.BlockSpec((seq, hd), lambda b, h: (0, 0)),
                     pl.BlockSpec((seq, hd), lambda b, h: (0, 0)),
                     pl.BlockSpec(memory_space=pl.ANY)]
        args += [cache_k, cache_v, cos, sin_signed, out_prev]
        aliases = {len(args) - 1: 0}
    else:
        out_specs += [pl.BlockSpec((seq, hd), lambda b, h: (b, h)),
                      pl.BlockSpec((seq, hd), lambda b, h: (b, h))]
        out_shape += [jax.ShapeDtypeStruct((n_seq * seq, kv_heads * hd), F32)] * 2
    return pl.pallas_call(
        functools.partial(_attn_kernel, has_ctx=has_ctx, grp=grp, qblk=min(seq, 256), scale=float(hd) ** -0.5),
        grid=(n_seq, kv_heads),
        in_specs=in_specs,
        out_specs=out_specs,
        out_shape=out_shape,
        input_output_aliases=aliases,
        compiler_params=_cparams("arbitrary", "arbitrary"),
        name="gqa",
    )(*args)


def _ret_kernel(*refs, nchunk, has_ctx, kscale):
    if has_ctx:
        (lg_ref, q_ref, k_ref, v_ref, g_ref, gn_ref, s0_ref, cos_ref, sin_ref, _prev,
         o_ref, sfin_ref, sb_ref) = refs
    else:
        (lg_ref, q_ref, k_ref, v_ref, g_ref, gn_ref, o_ref, sfin_ref, sb_ref) = refs
    cs = RET_CHUNK
    kd_dim = q_ref.shape[1]
    h = pl.program_id(1)
    lgf = lg_ref[0, h]
    lgb = lg_ref[1, h]
    row = lax.broadcasted_iota(jnp.int32, (cs, cs), 0)
    col = lax.broadcasted_iota(jnp.int32, (cs, cs), 1)
    rel = (row - col).astype(F32)
    dmat = (jnp.exp(jnp.where(rel >= 0, rel * lgf, -jnp.inf))
            + jnp.exp(jnp.where(rel <= 0, -rel * lgb, -jnp.inf)))
    j = lax.broadcasted_iota(jnp.int32, (cs, 1), 0).astype(F32)
    one = jnp.ones((1, 1), F32)
    dec_f = jnp.exp(one * (cs * lgf))
    dec_b = jnp.exp(one * (cs * lgb))

    def load_qk(ref, r0, mul):
        x = ref[pl.ds(r0, cs), :] * mul
        if has_ctx:
            half = kd_dim // 2
            x1, x2 = x[:, :half], x[:, half:]
            cos = cos_ref[pl.ds(r0, cs), :]
            sin = sin_ref[pl.ds(r0, cs), :]
            x = jnp.concatenate([x1 * cos - x2 * sin, x2 * cos + x1 * sin], axis=1)
        return x

    def bwd_body(i, st):
        n = nchunk - 1 - i
        r0 = pl.multiple_of(n * cs, cs)
        sb_ref[n] = st
        k = load_qk(k_ref, r0, kscale)
        v = v_ref[pl.ds(r0, cs), :]
        return st * dec_b + _bdot((k * jnp.exp(j * lgb)).T, v)

    st_b0 = s0_ref[1] if has_ctx else jnp.zeros(sfin_ref.shape[1:], F32)
    st_b = lax.fori_loop(0, nchunk, bwd_body, st_b0)
    sfin_ref[1] = st_b

    def fwd_body(n, st):
        r0 = pl.multiple_of(n * cs, cs)
        q = load_qk(q_ref, r0, 1.0)
        k = load_qk(k_ref, r0, kscale)
        v = v_ref[pl.ds(r0, cs), :]
        a = _bdot_nt(q, k) * dmat
        o = _bdot(a, v)
        o = o + _bdot(q * jnp.exp((j + 1.0) * lgf), st)
        o = o + _bdot(q * jnp.exp((cs - j) * lgb), sb_ref[n])
        mu = jnp.mean(o, axis=-1, keepdims=True)
        var = jnp.mean(jnp.square(o - mu), axis=-1, keepdims=True)
        y = (o - mu) * lax.rsqrt(var + EPS) * gn_ref[0]
        o_ref[pl.ds(r0, cs), :] = _silu(g_ref[pl.ds(r0, cs), :]) * y
        return st * dec_f + _bdot((k * jnp.exp((cs - 1.0 - j) * lgf)).T, v)

    st_f0 = s0_ref[0] if has_ctx else jnp.zeros(sfin_ref.shape[1:], F32)
    sfin_ref[0] = lax.fori_loop(0, nchunk, fwd_body, st_f0)


def _retention(proj, log_gamma, gn_g, ctx, layer_idx, out_prev, n_seq, seq, row_blk0, heads, key_d, val_d):
    t = proj.shape[0]
    has_ctx = ctx is not None
    nq = heads * key_d
    cq, ck, cv, cg = 0, nq // key_d, 2 * nq // val_d, (2 * nq + heads * val_d) // val_d
    in_specs = [pl.BlockSpec(memory_space=pltpu.SMEM),
                pl.BlockSpec((seq, key_d), lambda b, h: (row_blk0 + b, cq + h)),
                pl.BlockSpec((seq, key_d), lambda b, h: (row_blk0 + b, ck + h)),
                pl.BlockSpec((seq, val_d), lambda b, h: (row_blk0 + b, cv + h)),
                pl.BlockSpec((seq, val_d), lambda b, h: (row_blk0 + b, cg + h)),
                pl.BlockSpec((1, 1, val_d), lambda b, h: (h, 0, 0))]
    args = [log_gamma, proj, proj, proj, proj, gn_g.reshape(heads, 1, val_d)]
    aliases = {}
    if has_ctx:
        s0, cos, sin = ctx
        in_specs += [pl.BlockSpec((None, None, 2, None, key_d, val_d), lambda b, h: (b, layer_idx, 0, h, 0, 0)),
                     pl.BlockSpec((seq, key_d // 2), lambda b, h: (0, 0)),
                     pl.BlockSpec((seq, key_d // 2), lambda b, h: (0, 0)),
                     pl.BlockSpec(memory_space=pl.ANY)]
        args += [s0, cos, sin, out_prev]
        aliases = {len(args) - 1: 0}
    nchunk = seq // RET_CHUNK
    return pl.pallas_call(
        functools.partial(_ret_kernel, nchunk=nchunk, has_ctx=has_ctx, kscale=float(key_d) ** -0.5),
        grid=(n_seq, heads),
        in_specs=in_specs,
        out_specs=[pl.BlockSpec((seq, val_d), lambda b, h: (row_blk0 + b, h)),
                   pl.BlockSpec((None, 2, None, key_d, val_d), lambda b, h: (b, 0, h, 0, 0))],
        out_shape=[jax.ShapeDtypeStruct((t, heads * val_d), F32),
                   jax.ShapeDtypeStruct((n_seq, 2, heads, key_d, val_d), F32)],
        scratch_shapes=[pltpu.VMEM((nchunk, key_d, val_d), F32)],
        input_output_aliases=aliases,
        compiler_params=_cparams("arbitrary", "arbitrary"),
        name="retention",
    )(*args)


def _outproj_kernel(*refs, n_in, n_groups, n_experts):
    mix_refs = refs[:n_in]
    w_refs = refs[n_in:2 * n_in]
    x_ref, mod_ref, g2_ref, wr_ref, br_ref, xn_ref, h2_ref, eidx_ref, ew_ref = refs[2 * n_in:]
    acc = _bdot(mix_refs[0][...], w_refs[0][...])
    for mref, wref in zip(mix_refs[1:], w_refs[1:]):
        acc = acc + _bdot(mref[...], wref[...])
    xn = x_ref[...] + mod_ref[0, 2:3, :] * acc
    xn_ref[...] = xn
    h2 = _rms(xn, g2_ref[...]) * (1.0 + mod_ref[0, 4:5, :]) + mod_ref[0, 3:4, :]
    h2_ref[...] = h2
    lg = jnp.dot(h2, wr_ref[...], precision=HIGHEST, preferred_element_type=F32) + br_ref[...]

    per_grp = n_experts // n_groups
    lane = lax.broadcasted_iota(jnp.int32, lg.shape, 1).astype(F32)
    big = float(lg.shape[1])
    ninf = -jnp.inf
    gl = jnp.where(lane < n_groups, lg, ninf)
    gmax = jnp.max(gl, axis=-1, keepdims=True)
    p_grp = 1.0 / jnp.sum(jnp.exp(gl - gmax), axis=-1, keepdims=True)
    gidx = jnp.min(jnp.where(gl == gmax, lane, big), axis=-1, keepdims=True)
    in_grp = ((lane >= n_groups) & (lane < n_groups + n_experts)
              & (jnp.floor((lane - n_groups) / per_grp) == gidx))
    el = jnp.where(in_grp, lg, ninf)
    emax = jnp.max(el, axis=-1, keepdims=True)
    esum = jnp.sum(jnp.exp(el - emax), axis=-1, keepdims=True)
    i1 = jnp.min(jnp.where(el == emax, lane, big), axis=-1, keepdims=True)
    el2 = jnp.where(lane == i1, ninf, el)
    emax2 = jnp.max(el2, axis=-1, keepdims=True)
    i2 = jnp.min(jnp.where(el2 == emax2, lane, big), axis=-1, keepdims=True)
    p1 = 1.0 / esum
    p2 = jnp.exp(emax2 - emax) / esum
    w1 = p1 / (p1 + p2) * p_grp
    w2 = p2 / (p1 + p2) * p_grp
    eidx_ref[...] = jnp.where(lane == 0.0, i1 - n_groups, jnp.where(lane == 1.0, i2 - n_groups, 0.0)).astype(jnp.int32)
    ew_ref[...] = jnp.where(lane == 0.0, w1, jnp.where(lane == 1.0, w2, 0.0))


def _out_proj(mixes, w_parts, x, mod, g2, w_router, b_router, n_ctx_tiles, tiles_per_req, n_groups, n_experts):
    t, d = x.shape
    tm = TOKEN_TILE
    n_in = len(mixes)
    mrow = functools.partial(_tile_mod_row, n_ctx_tiles=n_ctx_tiles, tiles_per_req=tiles_per_req)
    in_specs = ([pl.BlockSpec((tm, m.shape[1]), lambda i: (i, 0)) for m in mixes]
                + [pl.BlockSpec(w.shape, lambda i: (0, 0)) for w in w_parts]
                + [pl.BlockSpec((tm, d), lambda i: (i, 0)),
                   pl.BlockSpec((1, 6, d), lambda i: (mrow(i), 0, 0)),
                   pl.BlockSpec((1, d), lambda i: (0, 0)),
                   pl.BlockSpec(w_router.shape, lambda i: (0, 0)),
                   pl.BlockSpec((1, 128), lambda i: (0, 0))])
    return pl.pallas_call(
        functools.partial(_outproj_kernel, n_in=n_in, n_groups=n_groups, n_experts=n_experts),
        grid=(t // tm,),
        in_specs=in_specs,
        out_specs=[pl.BlockSpec((tm, d), lambda i: (i, 0)),
                   pl.BlockSpec((tm, d), lambda i: (i, 0)),
                   pl.BlockSpec((tm, 128), lambda i: (i, 0)),
                   pl.BlockSpec((tm, 128), lambda i: (i, 0))],
        out_shape=[jax.ShapeDtypeStruct((t, d), F32),
                   jax.ShapeDtypeStruct((t, d), F32),
                   jax.ShapeDtypeStruct((t, 128), jnp.int32),
                   jax.ShapeDtypeStruct((t, 128), F32)],
        compiler_params=_cparams("arbitrary"),
        name="out_proj_router",
    )(*mixes, *w_parts, x, mod, g2.reshape(1, d), w_router, b_router)


def _dispatch_plan(e1, e2, n_experts, tile):
    t = e1.shape[0]
    e = jnp.concatenate([e1, e2])
    onehot = (e[:, None] == jnp.arange(n_experts, dtype=jnp.int32)[None, :]).astype(jnp.int32)
    csum = jnp.cumsum(onehot, axis=0)
    rank = jnp.take_along_axis(csum, e[:, None], axis=1)[:, 0] - 1
    counts = csum[-1]
    padded = ((counts + tile - 1) // tile) * tile
    ends = jnp.cumsum(padded)
    starts = ends - padded
    pos = starts[e] + rank
    n_tiles = (2 * t) // tile + n_experts
    n_rows = n_tiles * tile
    src_tok = jnp.zeros((n_rows,), jnp.int32).at[pos].set(jnp.tile(jnp.arange(t, dtype=jnp.int32), 2))
    n_used = (ends[-1] // tile).astype(jnp.int32)
    tile_start = jnp.arange(n_tiles, dtype=jnp.int32) * tile
    tile_expert = jnp.searchsorted(ends, jnp.minimum(tile_start, ends[-1] - 1), side="right").astype(jnp.int32)
    return src_tok.reshape(n_tiles, 1, tile), tile_expert, n_used.reshape(1), pos[:t], pos[t:]


def _moe_kernel(te_ref, nu_ref, tok_ref, h_hbm, wg_ref, wu_ref, wd_ref, o_ref, buf, sem, *, tm):
    i = pl.program_id(0)

    def row_copy(r, src_row):
        return pltpu.make_async_copy(h_hbm.at[pl.ds(src_row, 1)], buf.at[pl.ds(r, 1)], sem.at[0])

    @pl.when(i < nu_ref[0])
    def _():
        def issue(r, c):
            row_copy(r, tok_ref[0, 0, r]).start()
            return c

        lax.fori_loop(0, tm, issue, 0)

        def drain(r, c):
            row_copy(r, 0).wait()
            return c

        lax.fori_loop(0, tm, drain, 0)
        x = buf[...].astype(BF16)
        g = jnp.dot(x, wg_ref[...].astype(BF16), preferred_element_type=F32)
        u = jnp.dot(x, wu_ref[...].astype(BF16), preferred_element_type=F32)
        act = _silu(g) * u
        o_ref[...] = jnp.dot(act.astype(BF16), wd_ref[...].astype(BF16), preferred_element_type=F32)

    @pl.when(i >= nu_ref[0])
    def _():
        o_ref[...] = jnp.zeros_like(o_ref)


def _moe_experts(h2, src_tok, tile_expert, n_used, w_gate, w_up, w_down, layer_idx):
    t, d = h2.shape
    n_tiles, _, tm = src_tok.shape
    f = w_gate.shape[-1]
    return pl.pallas_call(
        functools.partial(_moe_kernel, tm=tm),
        grid_spec=pltpu.PrefetchScalarGridSpec(
            num_scalar_prefetch=2,
            grid=(n_tiles,),
            in_specs=[pl.BlockSpec((1, 1, tm), lambda i, te, nu: (i, 0, 0), memory_space=pltpu.SMEM),
                      pl.BlockSpec(memory_space=pl.ANY),
                      pl.BlockSpec((None, None, d, f), lambda i, te, nu: (layer_idx, te[i], 0, 0)),
                      pl.BlockSpec((None, None, d, f), lambda i, te, nu: (layer_idx, te[i], 0, 0)),
                      pl.BlockSpec((None, None, f, d), lambda i, te, nu: (layer_idx, te[i], 0, 0))],
            out_specs=pl.BlockSpec((tm, d), lambda i, te, nu: (i, 0)),
            scratch_shapes=[pltpu.VMEM((tm, d), F32), pltpu.SemaphoreType.DMA((1,))]),
        out_shape=jax.ShapeDtypeStruct((n_tiles * tm, d), F32),
        compiler_params=_cparams("arbitrary"),
        name="moe_experts",
    )(tile_expert, n_used, src_tok, h2, w_gate, w_up, w_down)


def _combine_kernel(p1_ref, p2_ref, y_hbm, x_ref, ew_ref, mod_ref, gf_ref, o_ref, buf1, buf2, sem, *, tm, final_norm):
    def row_copy(buf, k, r, src_row):
        return pltpu.make_async_copy(y_hbm.at[pl.ds(src_row, 1)], buf.at[pl.ds(r, 1)], sem.at[k])

    def issue(r, c):
        row_copy(buf1, 0, r, p1_ref[0, 0, r]).start()
        row_copy(buf2, 1, r, p2_ref[0, 0, r]).start()
        return c

    lax.fori_loop(0, tm, issue, 0)

    def drain(r, c):
        row_copy(buf1, 0, r, 0).wait()
        row_copy(buf2, 1, r, 0).wait()
        return c

    lax.fori_loop(0, tm, drain, 0)
    ew = ew_ref[...]
    y = ew[:, 0:1] * buf1[...] + ew[:, 1:2] * buf2[...]
    x = x_ref[...] + mod_ref[0, 5:6, :] * y
    if final_norm:
        x = _rms(x, gf_ref[...])
    o_ref[...] = x


def _moe_combine(y_sorted, pos1, pos2, x, ew, mod, final_g, n_ctx_tiles, tiles_per_req, final_norm):
    t, d = x.shape
    tm = TOKEN_TILE
    mrow = functools.partial(_tile_mod_row, n_ctx_tiles=n_ctx_tiles, tiles_per_req=tiles_per_req)
    smem_rows = pl.BlockSpec((1, 1, tm), lambda i: (i, 0, 0), memory_space=pltpu.SMEM)
    return pl.pallas_call(
        functools.partial(_combine_kernel, tm=tm, final_norm=final_norm),
        grid=(t // tm,),
        in_specs=[smem_rows, smem_rows,
                  pl.BlockSpec(memory_space=pl.ANY),
                  pl.BlockSpec((tm, d), lambda i: (i, 0)),
                  pl.BlockSpec((tm, 128), lambda i: (i, 0)),
                  pl.BlockSpec((1, 6, d), lambda i: (mrow(i), 0, 0)),
                  pl.BlockSpec((1, d), lambda i: (0, 0))],
        out_specs=pl.BlockSpec((tm, d), lambda i: (i, 0)),
        out_shape=jax.ShapeDtypeStruct((t, d), F32),
        scratch_shapes=[pltpu.VMEM((tm, d), F32), pltpu.VMEM((tm, d), F32), pltpu.SemaphoreType.DMA((2,))],
        compiler_params=_cparams("arbitrary"),
        name="moe_combine",
    )(pos1.reshape(t // tm, 1, tm), pos2.reshape(t // tm, 1, tm), y_sorted, x, ew, mod, final_g.reshape(1, d))


def _axial_angles(seq, head_dim):
    rows = seq // GRID_W
    row = jnp.repeat(jnp.arange(rows), GRID_W).astype(F32)
    col = jnp.tile(jnp.arange(GRID_W), rows).astype(F32)
    n_freq = head_dim // 4
    inv = ROPE_BASE ** (-jnp.arange(n_freq, dtype=F32) / n_freq)
    return jnp.concatenate([row[:, None] * inv, col[:, None] * inv], axis=-1)


def kernel(x_prompt, x_sample, cache_attn_k, cache_attn_v, state_hgrn, state_ret, c, c_ctx, mod_w, mod_b, norm1_g, norm2_g, even_w_in, even_w_out, hgrn_lb_logits, hgrn_norm_g, attn_qn_g, attn_kn_g, odd_w_in, odd_w_out, ret_decay_logit, ret_gn_g, moe_w_group, moe_b_group, moe_w_expert, moe_b_expert, moe_w_gate, moe_w_up, moe_w_down, final_norm_g):
    batch, seq, d = x_prompt.shape
    dec_batch, dec_seq, _ = x_sample.shape
    depth = mod_w.shape[0]
    heads_a, val_a = hgrn_norm_g.shape[1:]
    key_a = hgrn_lb_logits.shape[2] // heads_a
    hd_b = attn_qn_g.shape[1]
    kv_b = cache_attn_k.shape[3]
    heads_b = (even_w_out.shape[1] - heads_a * val_a) // hd_b
    heads_c, val_c = ret_gn_g.shape[1:]
    key_c = state_ret.shape[4]
    n_experts = moe_w_expert.shape[2]
    n_groups = moe_w_group.shape[2]
    tm = TOKEN_TILE
    n_ctx = batch * seq
    n_ctx_tiles = n_ctx // tm
    tiles_per_req = dec_seq // tm
    assert seq % tm == 0 and dec_seq % tm == 0 and n_ctx % dec_seq == 0
    assert key_a == 128 and val_a == 128 and hd_b == 128 and n_groups + n_experts <= 128

    x = jnp.concatenate([x_prompt.reshape(n_ctx, d), x_sample.reshape(dec_batch * dec_seq, d)], axis=0)
    t = x.shape[0]

    n_vec = 1 + dec_batch
    rows = -(-n_vec // 8) * 8
    cvecs = jnp.zeros((rows, d), F32).at[0].set(c_ctx).at[1:n_vec].set(c)
    mods = _ada_mod(cvecs, mod_w, mod_b).reshape(depth, rows, 6, d)

    lower_bounds = jnp.cumsum(jax.nn.softmax(hgrn_lb_logits.astype(F32), axis=0), axis=0)
    log_gamma = jax.nn.log_sigmoid(ret_decay_logit.astype(F32))

    ang_b = _axial_angles(dec_seq, hd_b)
    cos_b = jnp.concatenate([jnp.cos(ang_b), jnp.cos(ang_b)], axis=-1)
    sin_b = jnp.concatenate([-jnp.sin(ang_b), jnp.sin(ang_b)], axis=-1)
    ang_c = _axial_angles(dec_seq, key_c)
    cos_c, sin_c = jnp.cos(ang_c), jnp.sin(ang_c)

    pad = 128 - n_groups - n_experts
    new_k = new_v = new_sh = new_sr = None
    assert depth == 2, "one HGRN2/GQA layer followed by one retention layer"
    for l in range(depth):
        mod = mods[l]
        if l % 2 == 0:
            e = l // 2
            w_in = even_w_in[e].astype(BF16)
            proj = _in_proj(x, norm1_g[l], mod, w_in, n_ctx_tiles, tiles_per_req, w_in.shape[1])
            lb = lower_bounds[l]
            oa, s_ctx = _hgrn(proj, lb, hgrn_norm_g[e], None, e, None, batch, seq, 0, heads_a, key_a, val_a)
            oa, _ = _hgrn(proj, lb, hgrn_norm_g[e], state_hgrn, e, oa, dec_batch, dec_seq, n_ctx // dec_seq,
                          heads_a, key_a, val_a)
            col0 = (3 * heads_a * key_a + 2 * heads_a * val_a) // 128
            ob, kc, vc = _attn(proj, attn_qn_g[e], attn_kn_g[e], None, e, None, batch, seq, 0,
                               heads_b, kv_b, hd_b, col0)
            (ob,) = _attn(proj, attn_qn_g[e], attn_kn_g[e], (cache_attn_k, cache_attn_v, cos_b, sin_b), e, ob,
                          dec_batch, dec_seq, n_ctx // dec_seq, heads_b, kv_b, hd_b, col0)
            new_k = kc.reshape(batch, 1, seq, kv_b, hd_b)
            new_v = vc.reshape(batch, 1, seq, kv_b, hd_b)
            new_sh = s_ctx[:, None]
            w_out = even_w_out[e].astype(BF16)
            mixes = [oa, ob]
            w_parts = [w_out[:heads_a * val_a], w_out[heads_a * val_a:]]
        else:
            o = l // 2
            w_in = odd_w_in[o].astype(BF16)
            proj = _in_proj(x, norm1_g[l], mod, w_in, n_ctx_tiles, tiles_per_req, w_in.shape[1] // 2)
            y, s_ctx = _retention(proj, log_gamma[o], ret_gn_g[o], None, o, None, batch, seq, 0,
                                  heads_c, key_c, val_c)
            y, _ = _retention(proj, log_gamma[o], ret_gn_g[o], (state_ret, cos_c, sin_c), o, y,
                              dec_batch, dec_seq, n_ctx // dec_seq, heads_c, key_c, val_c)
            new_sr = s_ctx[:, None]
            mixes = [y]
            w_parts = [odd_w_out[o].astype(BF16)]

        w_router = jnp.concatenate([moe_w_group[l], moe_w_expert[l], jnp.zeros((d, pad), F32)], axis=1)
        b_router = jnp.concatenate([moe_b_group[l], moe_b_expert[l], jnp.zeros((pad,), F32)]).reshape(1, 128)
        xn, h2, eidx, ew = _out_proj(mixes, w_parts, x, mod, norm2_g[l], w_router, b_router,
                                     n_ctx_tiles, tiles_per_req, n_groups, n_experts)
        src_tok, tile_expert, n_used, pos1, pos2 = _dispatch_plan(eidx[:, 0], eidx[:, 1], n_experts, EXPERT_TILE)
        y_sorted = _moe_experts(h2, src_tok, tile_expert, n_used, moe_w_gate, moe_w_up, moe_w_down, l)
        x = _moe_combine(y_sorted, pos1, pos2, xn, ew, mod, final_norm_g, n_ctx_tiles, tiles_per_req,
                         final_norm=(l == depth - 1))

    y_prompt = x[:n_ctx].reshape(batch, seq, d)
    y_sample = x[n_ctx:].reshape(dec_batch, dec_seq, d)
    return (y_prompt, y_sample, new_k, new_v, new_sh, new_sr)
```

```python
import functools

import numpy as np
import jax
import jax.numpy as jnp
from jax import lax
from jax.experimental import pallas as pl
from jax.experimental.pallas import tpu as pltpu

F32 = jnp.float32
BF16 = jnp.bfloat16
EPS = 1e-6
ROPE_BASE = 10000.0
GRID_W = 64
N_GROUPS = 4
TOP_K_INNER = 2
HGRN_BLOCK = 128
HGRN_DIAG = 8
RET_CHUNK = 256
TOKEN_TILE = 256
EXPERT_TILE = 256
V7X_VMEM_LIMIT_BYTES = 56 * 1024 * 1024
HIGHEST = lax.Precision.HIGHEST

_NT = (((1,), (1,)), ((), ()))


def _cparams(*sem):
    return pltpu.CompilerParams(dimension_semantics=sem, vmem_limit_bytes=V7X_VMEM_LIMIT_BYTES)


def _rms(x, g):
    return x * lax.rsqrt(jnp.mean(x * x, axis=-1, keepdims=True) + EPS) * g


def _silu(x):
    return x * jax.nn.sigmoid(x)


def _bdot(a, b):
    return jnp.dot(a.astype(BF16), b.astype(BF16), preferred_element_type=F32)


def _bdot_nt(a, b):
    return lax.dot_general(a.astype(BF16), b.astype(BF16), _NT, preferred_element_type=F32)


def _mod_kernel(c_ref, w_ref, b_ref, o_ref):
    a = _silu(c_ref[...])
    o_ref[0] = jnp.dot(a, w_ref[0], precision=HIGHEST, preferred_element_type=F32) + b_ref[0]


def _ada_mod(cvecs, mod_w, mod_b):
    depth, d, n = mod_w.shape
    tn = n // 4
    rows = cvecs.shape[0]
    return pl.pallas_call(
        _mod_kernel,
        grid=(depth, n // tn),
        in_specs=[pl.BlockSpec((rows, d), lambda l, j: (0, 0)),
                  pl.BlockSpec((1, d, tn), lambda l, j: (l, 0, j)),
                  pl.BlockSpec((1, 1, tn), lambda l, j: (l, 0, j))],
        out_specs=pl.BlockSpec((1, rows, tn), lambda l, j: (l, 0, j)),
        out_shape=jax.ShapeDtypeStruct((depth, rows, n), F32),
        compiler_params=_cparams("arbitrary", "arbitrary"),
        name="ada_mod",
    )(cvecs, mod_w, mod_b.reshape(depth, 1, n))


def _tile_mod_row(i, n_ctx_tiles, tiles_per_req):
    return jnp.where(i < n_ctx_tiles, 0, 1 + jnp.maximum(i - n_ctx_tiles, 0) // tiles_per_req)


def _inproj_kernel(x_ref, g_ref, mod_ref, w_ref, o_ref):
    h = _rms(x_ref[...], g_ref[...]) * (1.0 + mod_ref[0, 1:2, :]) + mod_ref[0, 0:1, :]
    o_ref[...] = jnp.dot(h.astype(BF16), w_ref[...], preferred_element_type=F32)


def _in_proj(x, g, mod, w_bf, n_ctx_tiles, tiles_per_req, tn):
    t, d = x.shape
    n = w_bf.shape[1]
    tm = TOKEN_TILE
    mrow = functools.partial(_tile_mod_row, n_ctx_tiles=n_ctx_tiles, tiles_per_req=tiles_per_req)
    return pl.pallas_call(
        _inproj_kernel,
        grid=(n // tn, t // tm),
        in_specs=[pl.BlockSpec((tm, d), lambda j, i: (i, 0)),
                  pl.BlockSpec((1, d), lambda j, i: (0, 0)),
                  pl.BlockSpec((1, 6, d), lambda j, i: (mrow(i), 0, 0)),
                  pl.BlockSpec((d, tn), lambda j, i: (0, j))],
        out_specs=pl.BlockSpec((tm, tn), lambda j, i: (i, j)),
        out_shape=jax.ShapeDtypeStruct((t, n), F32),
        compiler_params=_cparams("arbitrary", "arbitrary"),
        name="in_proj",
    )(x, g.reshape(1, d), mod, w_bf)


def _hgrn_block(q, k, v, logf, st_t, reverse):
    nb = HGRN_BLOCK
    row = lax.broadcasted_iota(jnp.int32, (nb, nb), 0)
    col = lax.broadcasted_iota(jnp.int32, (nb, nb), 1)
    tri = jnp.where((col >= row) if reverse else (col <= row), 1.0, 0.0).astype(BF16)
    hi = logf.astype(BF16)
    lo = (logf - hi.astype(F32)).astype(BF16)
    c = (jnp.dot(tri, hi, preferred_element_type=F32)
         + jnp.dot(tri, lo, preferred_element_type=F32))
    c_edge = c[0:1, :] if reverse else c[nb - 1:nb, :]

    o = _bdot_nt(q * jnp.exp(c), st_t)
    kd = k * jnp.exp(c_edge - c)
    st_new = st_t * jnp.exp(c_edge) + _bdot(v.T, kd)

    a = jnp.zeros((nb, nb), F32)
    m = nb // 2
    while m >= HGRN_DIAG:
        pieces = []
        for grp in range(nb // (2 * m)):
            idx = grp * 2 * m + (m if reverse else m - 1)
            pieces.append(jnp.broadcast_to(c[idx:idx + 1, :], (2 * m, nb)))
        bnd = pieces[0] if len(pieces) == 1 else jnp.concatenate(pieces, axis=0)
        e = jnp.exp(-jnp.abs(c - bnd))
        am = _bdot_nt(q * e, k * e)
        same = (row & ~(2 * m - 1)) == (col & ~(2 * m - 1))
        r_hi = (row & m) != 0
        c_hi = (col & m) != 0
        if reverse:
            mask = same & jnp.logical_not(r_hi) & c_hi
        else:
            mask = same & r_hi & jnp.logical_not(c_hi)
        a = jnp.where(mask, am, a)
        m //= 2

    dg = HGRN_DIAG
    rowi = lax.broadcasted_iota(jnp.int32, (dg, nb), 0)
    tiles = []
    for j in range(nb // dg):
        qj = q[j * dg:(j + 1) * dg, :]
        cj = c[j * dg:(j + 1) * dg, :]
        for s in range(dg):
            r = j * dg + s
            cs = jnp.broadcast_to(c[r:r + 1, :], (dg, nb))
            ks = jnp.broadcast_to(k[r:r + 1, :], (dg, nb))
            keep = (rowi <= s) if reverse else (rowi >= s)
            tiles.append(qj * ks * jnp.exp(jnp.where(keep, cj - cs, -jnp.inf)))
    p = jnp.concatenate(tiles, axis=0)
    red = jnp.dot(p.astype(BF16), jnp.ones((nb, nb), BF16), preferred_element_type=F32)
    coli = lax.broadcasted_iota(jnp.int32, (dg, nb), 1)
    rows_out = []
    for j in range(nb // dg):
        aj = a[j * dg:(j + 1) * dg, :]
        for s in range(dg):
            r = j * dg + s
            aj = jnp.where(coli == r, red[r * dg:(r + 1) * dg, :], aj)
        rows_out.append(aj)
    a = jnp.concatenate(rows_out, axis=0)
    o = o + _bdot(a, v)
    return o, st_new


def _hgrn_kernel(*refs, nblk, has_state, qscale):
    qa_ref, ia_ref, zf_ref, zb_ref, ga_ref, lb_ref, ng_ref = refs[:7]
    s0_ref = refs[7] if has_state else None
    o_ref, sfin_ref, oacc_ref = refs[-3:]
    nb = HGRN_BLOCK

    for direction in (0, 1):
        reverse = direction == 1
        z_ref = zb_ref if reverse else zf_ref
        lb = lb_ref[direction:direction + 1, :]
        if has_state:
            st0 = s0_ref[direction].T
        else:
            st0 = jnp.zeros((ia_ref.shape[1], qa_ref.shape[1]), F32)

        def body(i, st_t, reverse=reverse, z_ref=z_ref, lb=lb):
            blk = (nblk - 1 - i) if reverse else i
            r0 = pl.multiple_of(blk * nb, nb)
            q = _silu(qa_ref[pl.ds(r0, nb), :]) * qscale
            z = z_ref[pl.ds(r0, nb), :]
            logf = jnp.log(lb + (1.0 - lb) * jax.nn.sigmoid(z))
            k = (1.0 - lb) * jax.nn.sigmoid(-z)
            v = ia_ref[pl.ds(r0, nb), :]
            o, st_new = _hgrn_block(q, k, v, logf, st_t, reverse)
            if reverse:
                oacc_ref[pl.ds(r0, nb), :] += o
            else:
                oacc_ref[pl.ds(r0, nb), :] = o
            return st_new

        st_fin = lax.fori_loop(0, nblk, body, st0)
        sfin_ref[direction] = st_fin.T

    o = oacc_ref[...]
    o_ref[...] = _rms(o, ng_ref[0]) * _silu(ga_ref[...])


def _hgrn(proj, lb, norm_g, s0, layer_idx, out_prev, n_seq, seq, row_blk0, heads, key_d, val_d):
    t = proj.shape[0]
    has_state = s0 is not None
    hk = heads * key_d // 128
    off_qa, off_ia, off_zf, off_zb, off_ga = 0, hk, 2 * hk, 3 * hk, 4 * hk

    def col(off):
        return pl.BlockSpec((seq, 128), lambda b, h: (row_blk0 + b, off + h))

    in_specs = [col(off_qa), col(off_ia), col(off_zf), col(off_zb), col(off_ga),
                pl.BlockSpec((2, key_d), lambda b, h: (0, h)),
                pl.BlockSpec((1, 1, val_d), lambda b, h: (h, 0, 0))]
    args = [proj, proj, proj, proj, proj, lb, norm_g.reshape(heads, 1, val_d)]
    if has_state:
        in_specs.append(pl.BlockSpec((None, None, 2, None, key_d, val_d),
                                     lambda b, h: (b, layer_idx, 0, h, 0, 0)))
        args.append(s0)
    aliases = {}
    if out_prev is not None:
        in_specs.append(pl.BlockSpec(memory_space=pl.ANY))
        args.append(out_prev)
        aliases = {len(args) - 1: 0}

    kern = functools.partial(_hgrn_kernel, nblk=seq // HGRN_BLOCK, has_state=has_state,
                             qscale=float(key_d) ** -0.5)
    return pl.pallas_call(
        kern,
        grid=(n_seq, heads),
        in_specs=in_specs,
        out_specs=[pl.BlockSpec((seq, val_d), lambda b, h: (row_blk0 + b, h)),
                   pl.BlockSpec((None, 2, None, key_d, val_d), lambda b, h: (b, 0, h, 0, 0))],
        out_shape=[jax.ShapeDtypeStruct((t, heads * val_d), F32),
                   jax.ShapeDtypeStruct((n_seq, 2, heads, key_d, val_d), F32)],
        scratch_shapes=[pltpu.VMEM((seq, val_d), F32)],
        input_output_aliases=aliases,
        compiler_params=_cparams("arbitrary", "arbitrary"),
        name="hgrn2",
    )(*args)


def _rope(x, cos, sin_signed):
    return x * cos + pltpu.roll(x, x.shape[-1] // 2, 1) * sin_signed


def _attn_kernel(*refs, has_ctx, grp, qblk, scale):
    if has_ctx:
        (q_ref, k_ref, v_ref, qn_ref, kn_ref, ck_ref, cv_ref, cos_ref, sin_ref, _prev, o_ref) = refs
    else:
        (q_ref, k_ref, v_ref, qn_ref, kn_ref, o_ref, ko_ref, vo_ref) = refs
    seq, hd = k_ref.shape
    k = _rms(k_ref[...], kn_ref[...])
    v = v_ref[...]
    if has_ctx:
        k = _rope(k, cos_ref[...], sin_ref[...])
        ck = ck_ref[...].astype(BF16)
        cv = cv_ref[...].astype(BF16)
    else:
        ko_ref[...] = k
        vo_ref[...] = v
    kb = k.astype(BF16)
    vb = v.astype(BF16)

    for g in range(grp):
        def body(i, carry, g=g):
            r0 = pl.multiple_of(i * qblk, qblk)
            q = _rms(q_ref[pl.ds(r0, qblk), g * hd:(g + 1) * hd], qn_ref[...])
            if has_ctx:
                q = _rope(q, cos_ref[pl.ds(r0, qblk), :], sin_ref[pl.ds(r0, qblk), :])
            qb = q.astype(BF16)
            s2 = lax.dot_general(qb, kb, _NT, preferred_element_type=F32) * scale
            m = jnp.max(s2, axis=-1, keepdims=True)
            if has_ctx:
                s1 = lax.dot_general(qb, ck, _NT, preferred_element_type=F32) * scale
                m = jnp.maximum(m, jnp.max(s1, axis=-1, keepdims=True))
                p1 = jnp.exp(s1 - m)
            p2 = jnp.exp(s2 - m)
            den = jnp.sum(p2, axis=-1, keepdims=True)
            acc = jnp.dot(p2.astype(BF16), vb, preferred_element_type=F32)
            if has_ctx:
                den = den + jnp.sum(p1, axis=-1, keepdims=True)
                acc = acc + jnp.dot(p1.astype(BF16), cv, preferred_element_type=F32)
            o_ref[pl.ds(r0, qblk), g * hd:(g + 1) * hd] = acc / den
            return carry

        lax.fori_loop(0, seq // qblk, body, 0)


def _attn(proj, qn_g, kn_g, ctx, layer_idx, out_prev, n_seq, seq, row_blk0, heads, kv_heads, hd, col0):
    t = proj.shape[0]
    grp = heads // kv_heads
    has_ctx = ctx is not None
    cq = col0 // grp
    ck0 = col0 + heads
    cv0 = ck0 + kv_heads
    in_specs = [pl.BlockSpec((seq, grp * hd), lambda b, h: (row_blk0 + b, cq + h)),
                pl.BlockSpec((seq, hd), lambda b, h: (row_blk0 + b, ck0 + h)),
                pl.BlockSpec((seq, hd), lambda b, h: (row_blk0 + b, cv0 + h)),
                pl.BlockSpec((1, hd), lambda b, h: (0, 0)),
                pl.BlockSpec((1, hd), lambda b, h: (0, 0))]
    args = [proj, proj, proj, qn_g.reshape(1, hd), kn_g.reshape(1, hd)]
    out_specs = [pl.BlockSpec((seq, grp * hd), lambda b, h: (row_blk0 + b, h))]
    out_shape = [jax.ShapeDtypeStruct((t, heads * hd), F32)]
    aliases = {}
    if has_ctx:
        cache_k, cache_v, cos, sin_signed = ctx
        nreq, nlay, past = cache_k.shape[:3]
        cache_k = cache_k.reshape(nreq, nlay, past, kv_heads * hd)
        cache_v = cache_v.reshape(nreq, nlay, past, kv_heads * hd)
        spec_c = pl.BlockSpec((None, None, past, hd), lambda b, h: (b, layer_idx, 0, h))
        in_specs += [spec_c, spec_c,
                     pl.BlockSpec((seq, hd), lambda b, h: (0, 0)),
                     pl.BlockSpec((seq, hd), lambda b, h: (0, 0)),
                     pl.BlockSpec(memory_space=pl.ANY)]
        args += [cache_k, cache_v, cos, sin_signed, out_prev]
        aliases = {len(args) - 1: 0}
    else:
        out_specs += [pl.BlockSpec((seq, hd), lambda b, h: (b, h)),
                      pl.BlockSpec((seq, hd), lambda b, h: (b, h))]
        out_shape += [jax.ShapeDtypeStruct((n_seq * seq, kv_heads * hd), F32)] * 2
    return pl.pallas_call(
        functools.partial(_attn_kernel, has_ctx=has_ctx, grp=grp, qblk=min(seq, 256), scale=float(hd) ** -0.5),
        grid=(n_seq, kv_heads),
        in_specs=in_specs,
        out_specs=out_specs,
        out_shape=out_shape,
        input_output_aliases=aliases,
        compiler_params=_cparams("arbitrary", "arbitrary"),
        name="gqa",
    )(*args)


def _ret_kernel(*refs, nchunk, has_ctx, kscale):
    if has_ctx:
        (lg_ref, q_ref, k_ref, v_ref, g_ref, gn_ref, s0_ref, cos_ref, sin_ref, _prev,
         o_ref, sfin_ref, sb_ref) = refs
    else:
        (lg_ref, q_ref, k_ref, v_ref, g_ref, gn_ref, o_ref, sfin_ref, sb_ref) = refs
    cs = RET_CHUNK
    kd_dim = q_ref.shape[1]
    h = pl.program_id(1)
    lgf = lg_ref[0, h]
    lgb = lg_ref[1, h]
    row = lax.broadcasted_iota(jnp.int32, (cs, cs), 0)
    col = lax.broadcasted_iota(jnp.int32, (cs, cs), 1)
    rel = (row - col).astype(F32)
    dmat = (jnp.exp(jnp.where(rel >= 0, rel * lgf, -jnp.inf))
            + jnp.exp(jnp.where(rel <= 0, -rel * lgb, -jnp.inf)))
    j = lax.broadcasted_iota(jnp.int32, (cs, 1), 0).astype(F32)
    one = jnp.ones((1, 1), F32)
    dec_f = jnp.exp(one * (cs * lgf))
    dec_b = jnp.exp(one * (cs * lgb))

    def load_qk(ref, r0, mul):
        x = ref[pl.ds(r0, cs), :] * mul
        if has_ctx:
            half = kd_dim // 2
            x1, x2 = x[:, :half], x[:, half:]
            cos = cos_ref[pl.ds(r0, cs), :]
            sin = sin_ref[pl.ds(r0, cs), :]
            x = jnp.concatenate([x1 * cos - x2 * sin, x2 * cos + x1 * sin], axis=1)
        return x

    def bwd_body(i, st):
        n = nchunk - 1 - i
        r0 = pl.multiple_of(n * cs, cs)
        sb_ref[n] = st
        k = load_qk(k_ref, r0, kscale)
        v = v_ref[pl.ds(r0, cs), :]
        return st * dec_b + _bdot((k * jnp.exp(j * lgb)).T, v)

    st_b0 = s0_ref[1] if has_ctx else jnp.zeros(sfin_ref.shape[1:], F32)
    st_b = lax.fori_loop(0, nchunk, bwd_body, st_b0)
    sfin_ref[1] = st_b

    def fwd_body(n, st):
        r0 = pl.multiple_of(n * cs, cs)
        q = load_qk(q_ref, r0, 1.0)
        k = load_qk(k_ref, r0, kscale)
        v = v_ref[pl.ds(r0, cs), :]
        a = _bdot_nt(q, k) * dmat
        o = _bdot(a, v)
        o = o + _bdot(q * jnp.exp((j + 1.0) * lgf), st)
        o = o + _bdot(q * jnp.exp((cs - j) * lgb), sb_ref[n])
        mu = jnp.mean(o, axis=-1, keepdims=True)
        var = jnp.mean(jnp.square(o - mu), axis=-1, keepdims=True)
        y = (o - mu) * lax.rsqrt(var + EPS) * gn_ref[0]
        o_ref[pl.ds(r0, cs), :] = _silu(g_ref[pl.ds(r0, cs), :]) * y
        return st * dec_f + _bdot((k * jnp.exp((cs - 1.0 - j) * lgf)).T, v)

    st_f0 = s0_ref[0] if has_ctx else jnp.zeros(sfin_ref.shape[1:], F32)
    sfin_ref[0] = lax.fori_loop(0, nchunk, fwd_body, st_f0)


def _retention(proj, log_gamma, gn_g, ctx, layer_idx, out_prev, n_seq, seq, row_blk0, heads, key_d, val_d):
    t = proj.shape[0]
    has_ctx = ctx is not None
    nq = heads * key_d
    cq, ck, cv, cg = 0, nq // key_d, 2 * nq // val_d, (2 * nq + heads * val_d) // val_d
    in_specs = [pl.BlockSpec(memory_space=pltpu.SMEM),
                pl.BlockSpec((seq, key_d), lambda b, h: (row_blk0 + b, cq + h)),
                pl.BlockSpec((seq, key_d), lambda b, h: (row_blk0 + b, ck + h)),
                pl.BlockSpec((seq, val_d), lambda b, h: (row_blk0 + b, cv + h)),
                pl.BlockSpec((seq, val_d), lambda b, h: (row_blk0 + b, cg + h)),
                pl.BlockSpec((1, 1, val_d), lambda b, h: (h, 0, 0))]
    args = [log_gamma, proj, proj, proj, proj, gn_g.reshape(heads, 1, val_d)]
    aliases = {}
    if has_ctx:
        s0, cos, sin = ctx
        in_specs += [pl.BlockSpec((None, None, 2, None, key_d, val_d), lambda b, h: (b, layer_idx, 0, h, 0, 0)),
                     pl.BlockSpec((seq, key_d // 2), lambda b, h: (0, 0)),
                     pl.BlockSpec((seq, key_d // 2), lambda b, h: (0, 0)),
                     pl.BlockSpec(memory_space=pl.ANY)]
        args += [s0, cos, sin, out_prev]
        aliases = {len(args) - 1: 0}
    nchunk = seq // RET_CHUNK
    return pl.pallas_call(
        functools.partial(_ret_kernel, nchunk=nchunk, has_ctx=has_ctx, kscale=float(key_d) ** -0.5),
        grid=(n_seq, heads),
        in_specs=in_specs,
        out_specs=[pl.BlockSpec((seq, val_d), lambda b, h: (row_blk0 + b, h)),
                   pl.BlockSpec((None, 2, None, key_d, val_d), lambda b, h: (b, 0, h, 0, 0))],
        out_shape=[jax.ShapeDtypeStruct((t, heads * val_d), F32),
                   jax.ShapeDtypeStruct((n_seq, 2, heads, key_d, val_d), F32)],
        scratch_shapes=[pltpu.VMEM((nchunk, key_d, val_d), F32)],
        input_output_aliases=aliases,
        compiler_params=_cparams("arbitrary", "arbitrary"),
        name="retention",
    )(*args)


def _outproj_kernel(*refs, n_in, n_groups, n_experts):
    mix_refs = refs[:n_in]
    w_refs = refs[n_in:2 * n_in]
    x_ref, mod_ref, g2_ref, wr_ref, br_ref, xn_ref, h2_ref, eidx_ref, ew_ref = refs[2 * n_in:]
    acc = _bdot(mix_refs[0][...], w_refs[0][...])
    for mref, wref in zip(mix_refs[1:], w_refs[1:]):
        acc = acc + _bdot(mref[...], wref[...])
    xn = x_ref[...] + mod_ref[0, 2:3, :] * acc
    xn_ref[...] = xn
    h2 = _rms(xn, g2_ref[...]) * (1.0 + mod_ref[0, 4:5, :]) + mod_ref[0, 3:4, :]
    h2_ref[...] = h2
    lg = jnp.dot(h2, wr_ref[...], precision=HIGHEST, preferred_element_type=F32) + br_ref[...]

    per_grp = n_experts // n_groups
    lane = lax.broadcasted_iota(jnp.int32, lg.shape, 1).astype(F32)
    big = float(lg.shape[1])
    ninf = -jnp.inf
    gl = jnp.where(lane < n_groups, lg, ninf)
    gmax = jnp.max(gl, axis=-1, keepdims=True)
    p_grp = 1.0 / jnp.sum(jnp.exp(gl - gmax), axis=-1, keepdims=True)
    gidx = jnp.min(jnp.where(gl == gmax, lane, big), axis=-1, keepdims=True)
    in_grp = ((lane >= n_groups) & (lane < n_groups + n_experts)
              & (jnp.floor((lane - n_groups) / per_grp) == gidx))
    el = jnp.where(in_grp, lg, ninf)
    emax = jnp.max(el, axis=-1, keepdims=True)
    esum = jnp.sum(jnp.exp(el - emax), axis=-1, keepdims=True)
    i1 = jnp.min(jnp.where(el == emax, lane, big), axis=-1, keepdims=True)
    el2 = jnp.where(lane == i1, ninf, el)
    emax2 = jnp.max(el2, axis=-1, keepdims=True)
    i2 = jnp.min(jnp.where(el2 == emax2, lane, big), axis=-1, keepdims=True)
    p1 = 1.0 / esum
    p2 = jnp.exp(emax2 - emax) / esum
    w1 = p1 / (p1 + p2) * p_grp
    w2 = p2 / (p1 + p2) * p_grp
    eidx_ref[...] = jnp.where(lane == 0.0, i1 - n_groups, jnp.where(lane == 1.0, i2 - n_groups, 0.0)).astype(jnp.int32)
    ew_ref[...] = jnp.where(lane == 0.0, w1, jnp.where(lane == 1.0, w2, 0.0))


def _out_proj(mixes, w_parts, x, mod, g2, w_router, b_router, n_ctx_tiles, tiles_per_req, n_groups, n_experts):
    t, d = x.shape
    tm = TOKEN_TILE
    n_in = len(mixes)
    mrow = functools.partial(_tile_mod_row, n_ctx_tiles=n_ctx_tiles, tiles_per_req=tiles_per_req)
    in_specs = ([pl.BlockSpec((tm, m.shape[1]), lambda i: (i, 0)) for m in mixes]
                + [pl.BlockSpec(w.shape, lambda i: (0, 0)) for w in w_parts]
                + [pl.BlockSpec((tm, d), lambda i: (i, 0)),
                   pl.BlockSpec((1, 6, d), lambda i: (mrow(i), 0, 0)),
                   pl.BlockSpec((1, d), lambda i: (0, 0)),
                   pl.BlockSpec(w_router.shape, lambda i: (0, 0)),
                   pl.BlockSpec((1, 128), lambda i: (0, 0))])
    return pl.pallas_call(
        functools.partial(_outproj_kernel, n_in=n_in, n_groups=n_groups, n_experts=n_experts),
        grid=(t // tm,),
        in_specs=in_specs,
        out_specs=[pl.BlockSpec((tm, d), lambda i: (i, 0)),
                   pl.BlockSpec((tm, d), lambda i: (i, 0)),
                   pl.BlockSpec((tm, 128), lambda i: (i, 0)),
                   pl.BlockSpec((tm, 128), lambda i: (i, 0))],
        out_shape=[jax.ShapeDtypeStruct((t, d), F32),
                   jax.ShapeDtypeStruct((t, d), F32),
                   jax.ShapeDtypeStruct((t, 128), jnp.int32),
                   jax.ShapeDtypeStruct((t, 128), F32)],
        compiler_params=_cparams("arbitrary"),
        name="out_proj_router",
    )(*mixes, *w_parts, x, mod, g2.reshape(1, d), w_router, b_router)


def _dispatch_plan(e1, e2, n_experts, tile):
    t = e1.shape[0]
    e = jnp.concatenate([e1, e2])
    onehot = (e[:, None] == jnp.arange(n_experts, dtype=jnp.int32)[None, :]).astype(jnp.int32)
    csum = jnp.cumsum(onehot, axis=0)
    rank = jnp.take_along_axis(csum, e[:, None], axis=1)[:, 0] - 1
    counts = csum[-1]
    padded = ((counts + tile - 1) // tile) * tile
    ends = jnp.cumsum(padded)
    starts = ends - padded
    pos = starts[e] + rank
    n_tiles = (2 * t) // tile + n_experts
    n_used = (ends[-1] // tile).astype(jnp.int32)
    tile_start = jnp.arange(n_tiles, dtype=jnp.int32) * tile
    probe = jnp.minimum(tile_start, ends[-1] - 1)
    tile_expert = jnp.sum((ends[None, :] <= probe[:, None]).astype(jnp.int32), axis=1)
    empty = counts == 0
    empty_rank = jnp.cumsum(empty.astype(jnp.int32)) - 1
    clear_row = jnp.where(empty, (n_used + empty_rank) * tile, ends - tile).astype(jnp.int32)
    return tile_expert, n_used.reshape(1), clear_row, pos[:t], pos[t:], n_tiles


def _dispatch_kernel(clear_ref, p1_ref, p2_ref, h_hbm, xs_hbm, zbuf, sem, *, tm, tile, n_experts):
    i = pl.program_id(0)

    def clear_copy(e):
        return pltpu.make_async_copy(zbuf, xs_hbm.at[pl.ds(pl.multiple_of(clear_ref[e], tile), tile)], sem.at[2])

    @pl.when(i == 0)
    def _():
        zbuf[...] = jnp.zeros_like(zbuf)
        for e in range(n_experts):
            clear_copy(e).start()
        for e in range(n_experts):
            clear_copy(e).wait()

    def row_copy(k, r, dst_row):
        return pltpu.make_async_copy(h_hbm.at[pl.ds(i * tm + r, 1)], xs_hbm.at[pl.ds(dst_row, 1)], sem.at[k])

    def issue(r, c):
        row_copy(0, r, p1_ref[0, 0, r]).start()
        row_copy(1, r, p2_ref[0, 0, r]).start()
        return c

    lax.fori_loop(0, tm, issue, 0, unroll=8)

    def drain(r, c):
        row_copy(0, r, 0).wait()
        row_copy(1, r, 0).wait()
        return c

    lax.fori_loop(0, tm, drain, 0, unroll=8)


def _moe_dispatch(h2, pos1, pos2, clear_row, n_tiles, tile):
    t, d = h2.shape
    tm = TOKEN_TILE
    n_experts = clear_row.shape[0]
    smem_rows = pl.BlockSpec((1, 1, tm), lambda i, cr: (i, 0, 0), memory_space=pltpu.SMEM)
    return pl.pallas_call(
        functools.partial(_dispatch_kernel, tm=tm, tile=tile, n_experts=n_experts),
        grid_spec=pltpu.PrefetchScalarGridSpec(
            num_scalar_prefetch=1,
            grid=(t // tm,),
            in_specs=[smem_rows, smem_rows, pl.BlockSpec(memory_space=pl.ANY)],
            out_specs=pl.BlockSpec(memory_space=pl.ANY),
            scratch_shapes=[pltpu.VMEM((tile, d), F32), pltpu.SemaphoreType.DMA((3,))]),
        out_shape=jax.ShapeDtypeStruct((n_tiles * tile, d), F32),
        compiler_params=_cparams("arbitrary"),
        name="moe_dispatch",
    )(clear_row, pos1.reshape(t // tm, 1, tm), pos2.reshape(t // tm, 1, tm), h2)


def _moe_kernel(te_ref, nu_ref, x_ref, wg_ref, wu_ref, wd_ref, o_ref, wg_bf, wu_bf, wd_bf):
    i = pl.program_id(0)
    new_expert = jnp.logical_or(i == 0, te_ref[i] != te_ref[jnp.maximum(i - 1, 0)])

    @pl.when(new_expert)
    def _():
        wg_bf[...] = wg_ref[...].astype(BF16)
        wu_bf[...] = wu_ref[...].astype(BF16)
        wd_bf[...] = wd_ref[...].astype(BF16)

    @pl.when(i < nu_ref[0])
    def _():
        x = x_ref[...].astype(BF16)
        g = jnp.dot(x, wg_bf[...], preferred_element_type=F32)
        u = jnp.dot(x, wu_bf[...], preferred_element_type=F32)
        act = _silu(g) * u
        o_ref[...] = jnp.dot(act.astype(BF16), wd_bf[...], preferred_element_type=F32)

    @pl.when(i >= nu_ref[0])
    def _():
        o_ref[...] = jnp.zeros_like(o_ref)


def _moe_experts(xs, tile_expert, n_used, w_gate, w_up, w_down, layer_idx, tm):
    rows, d = xs.shape
    f = w_gate.shape[-1]

    def used(i, nu):
        return jnp.minimum(i, nu[0] - 1)

    return pl.pallas_call(
        _moe_kernel,
        grid_spec=pltpu.PrefetchScalarGridSpec(
            num_scalar_prefetch=2,
            grid=(rows // tm,),
            in_specs=[pl.BlockSpec((tm, d), lambda i, te, nu: (used(i, nu), 0)),
                      pl.BlockSpec((None, None, d, f), lambda i, te, nu: (layer_idx, te[i], 0, 0)),
                      pl.BlockSpec((None, None, d, f), lambda i, te, nu: (layer_idx, te[i], 0, 0)),
                      pl.BlockSpec((None, None, f, d), lambda i, te, nu: (layer_idx, te[i], 0, 0))],
            out_specs=pl.BlockSpec((tm, d), lambda i, te, nu: (i, 0)),
            scratch_shapes=[pltpu.VMEM((d, f), BF16), pltpu.VMEM((d, f), BF16), pltpu.VMEM((f, d), BF16)]),
        out_shape=jax.ShapeDtypeStruct((rows, d), F32),
        compiler_params=_cparams("arbitrary"),
        name="moe_experts",
    )(tile_expert, n_used, xs, w_gate, w_up, w_down)


def _combine_kernel(p1_ref, p2_ref, p1n_ref, p2n_ref, y_hbm, x_ref, ew_ref, mod_ref, gf_ref, o_ref, buf, sem,
                    *, tm, n_steps, final_norm):
    i = pl.program_id(0)
    slot = lax.rem(i, 2)

    def row_copy(s, k, r, src_row):
        return pltpu.make_async_copy(y_hbm.at[pl.ds(src_row, 1)], buf.at[s, k, pl.ds(r, 1)], sem.at[s, k])

    def issue(s, pa_ref, pb_ref):
        def body(r, c):
            row_copy(s, 0, r, pa_ref[0, 0, r]).start()
            row_copy(s, 1, r, pb_ref[0, 0, r]).start()
            return c

        lax.fori_loop(0, tm, body, 0, unroll=8)

    @pl.when(i == 0)
    def _():
        issue(0, p1_ref, p2_ref)

    @pl.when(i + 1 < n_steps)
    def _():
        issue(1 - slot, p1n_ref, p2n_ref)

    def drain(r, c):
        row_copy(slot, 0, r, 0).wait()
        row_copy(slot, 1, r, 0).wait()
        return c

    lax.fori_loop(0, tm, drain, 0, unroll=8)
    ew = ew_ref[...]
    y = ew[:, 0:1] * buf[slot, 0] + ew[:, 1:2] * buf[slot, 1]
    x = x_ref[...] + mod_ref[0, 5:6, :] * y
    if final_norm:
        x = _rms(x, gf_ref[...])
    o_ref[...] = x


def _moe_combine(y_sorted, pos1, pos2, x, ew, mod, final_g, n_ctx_tiles, tiles_per_req, final_norm):
    t, d = x.shape
    tm = TOKEN_TILE
    n_steps = t // tm
    mrow = functools.partial(_tile_mod_row, n_ctx_tiles=n_ctx_tiles, tiles_per_req=tiles_per_req)
    smem_rows = pl.BlockSpec((1, 1, tm), lambda i: (i, 0, 0), memory_space=pltpu.SMEM)
    smem_next = pl.BlockSpec((1, 1, tm), lambda i: (jnp.minimum(i + 1, n_steps - 1), 0, 0),
                             memory_space=pltpu.SMEM)
    p1 = pos1.reshape(n_steps, 1, tm)
    p2 = pos2.reshape(n_steps, 1, tm)
    return pl.pallas_call(
        functools.partial(_combine_kernel, tm=tm, n_steps=n_steps, final_norm=final_norm),
        grid=(n_steps,),
        in_specs=[smem_rows, smem_rows, smem_next, smem_next,
                  pl.BlockSpec(memory_space=pl.ANY),
                  pl.BlockSpec((tm, d), lambda i: (i, 0)),
                  pl.BlockSpec((tm, 128), lambda i: (i, 0)),
                  pl.BlockSpec((1, 6, d), lambda i: (mrow(i), 0, 0)),
                  pl.BlockSpec((1, d), lambda i: (0, 0))],
        out_specs=pl.BlockSpec((tm, d), lambda i: (i, 0)),
        out_shape=jax.ShapeDtypeStruct((t, d), F32),
        scratch_shapes=[pltpu.VMEM((2, 2, tm, d), F32), pltpu.SemaphoreType.DMA((2, 2))],
        compiler_params=_cparams("arbitrary"),
        name="moe_combine",
    )(p1, p2, p1, p2, y_sorted, x, ew, mod, final_g.reshape(1, d))


def _axial_angles(seq, head_dim):
    rows = seq // GRID_W
    row = jnp.repeat(jnp.arange(rows), GRID_W).astype(F32)
    col = jnp.tile(jnp.arange(GRID_W), rows).astype(F32)
    n_freq = head_dim // 4
    inv = ROPE_BASE ** (-jnp.arange(n_freq, dtype=F32) / n_freq)
    return jnp.concatenate([row[:, None] * inv, col[:, None] * inv], axis=-1)


def kernel(x_prompt, x_sample, cache_attn_k, cache_attn_v, state_hgrn, state_ret, c, c_ctx, mod_w, mod_b, norm1_g, norm2_g, even_w_in, even_w_out, hgrn_lb_logits, hgrn_norm_g, attn_qn_g, attn_kn_g, odd_w_in, odd_w_out, ret_decay_logit, ret_gn_g, moe_w_group, moe_b_group, moe_w_expert, moe_b_expert, moe_w_gate, moe_w_up, moe_w_down, final_norm_g):
    batch, seq, d = x_prompt.shape
    dec_batch, dec_seq, _ = x_sample.shape
    depth = mod_w.shape[0]
    heads_a, val_a = hgrn_norm_g.shape[1:]
    key_a = hgrn_lb_logits.shape[2] // heads_a
    hd_b = attn_qn_g.shape[1]
    kv_b = cache_attn_k.shape[3]
    heads_b = (even_w_out.shape[1] - heads_a * val_a) // hd_b
    heads_c, val_c = ret_gn_g.shape[1:]
    key_c = state_ret.shape[4]
    n_experts = moe_w_expert.shape[2]
    n_groups = moe_w_group.shape[2]
    tm = TOKEN_TILE
    n_ctx = batch * seq
    n_ctx_tiles = n_ctx // tm
    tiles_per_req = dec_seq // tm
    assert seq % tm == 0 and dec_seq % tm == 0 and n_ctx % dec_seq == 0
    assert key_a == 128 and val_a == 128 and hd_b == 128 and n_groups + n_experts <= 128

    x = jnp.concatenate([x_prompt.reshape(n_ctx, d), x_sample.reshape(dec_batch * dec_seq, d)], axis=0)
    t = x.shape[0]

    n_vec = 1 + dec_batch
    rows = -(-n_vec // 8) * 8
    cvecs = jnp.zeros((rows, d), F32).at[0].set(c_ctx).at[1:n_vec].set(c)
    mods = _ada_mod(cvecs, mod_w, mod_b).reshape(depth, rows, 6, d)

    lower_bounds = jnp.cumsum(jax.nn.softmax(hgrn_lb_logits.astype(F32), axis=0), axis=0)
    log_gamma = jax.nn.log_sigmoid(ret_decay_logit.astype(F32))

    ang_b = _axial_angles(dec_seq, hd_b)
    cos_b = jnp.concatenate([jnp.cos(ang_b), jnp.cos(ang_b)], axis=-1)
    sin_b = jnp.concatenate([-jnp.sin(ang_b), jnp.sin(ang_b)], axis=-1)
    ang_c = _axial_angles(dec_seq, key_c)
    cos_c, sin_c = jnp.cos(ang_c), jnp.sin(ang_c)

    pad = 128 - n_groups - n_experts
    new_k = new_v = new_sh = new_sr = None
    assert depth == 2, "one HGRN2/GQA layer followed by one retention layer"
    for l in range(depth):
        mod = mods[l]
        if l % 2 == 0:
            e = l // 2
            w_in = even_w_in[e].astype(BF16)
            proj = _in_proj(x, norm1_g[l], mod, w_in, n_ctx_tiles, tiles_per_req, w_in.shape[1])
            lb = lower_bounds[l]
            oa, s_ctx = _hgrn(proj, lb, hgrn_norm_g[e], None, e, None, batch, seq, 0, heads_a, key_a, val_a)
            oa, _ = _hgrn(proj, lb, hgrn_norm_g[e], state_hgrn, e, oa, dec_batch, dec_seq, n_ctx // dec_seq,
                          heads_a, key_a, val_a)
            col0 = (3 * heads_a * key_a + 2 * heads_a * val_a) // 128
            ob, kc, vc = _attn(proj, attn_qn_g[e], attn_kn_g[e], None, e, None, batch, seq, 0,
                               heads_b, kv_b, hd_b, col0)
            (ob,) = _attn(proj, attn_qn_g[e], attn_kn_g[e], (cache_attn_k, cache_attn_v, cos_b, sin_b), e, ob,
                          dec_batch, dec_seq, n_ctx // dec_seq, heads_b, kv_b, hd_b, col0)
            new_k = kc.reshape(batch, 1, seq, kv_b, hd_b)
            new_v = vc.reshape(batch, 1, seq, kv_b, hd_b)
            new_sh = s_ctx[:, None]
            w_out = even_w_out[e].astype(BF16)
            mixes = [oa, ob]
            w_parts = [w_out[:heads_a * val_a], w_out[heads_a * val_a:]]
        else:
            o = l // 2
            w_in = odd_w_in[o].astype(BF16)
            proj = _in_proj(x, norm1_g[l], mod, w_in, n_ctx_tiles, tiles_per_req, w_in.shape[1] // 2)
            y, s_ctx = _retention(proj, log_gamma[o], ret_gn_g[o], None, o, None, batch, seq, 0,
                                  heads_c, key_c, val_c)
            y, _ = _retention(proj, log_gamma[o], ret_gn_g[o], (state_ret, cos_c, sin_c), o, y,
                              dec_batch, dec_seq, n_ctx // dec_seq, heads_c, key_c, val_c)
            new_sr = s_ctx[:, None]
            mixes = [y]
            w_parts = [odd_w_out[o].astype(BF16)]

        w_router = jnp.concatenate([moe_w_group[l], moe_w_expert[l], jnp.zeros((d, pad), F32)], axis=1)
        b_router = jnp.concatenate([moe_b_group[l], moe_b_expert[l], jnp.zeros((pad,), F32)]).reshape(1, 128)
        xn, h2, eidx, ew = _out_proj(mixes, w_parts, x, mod, norm2_g[l], w_router, b_router,
                                     n_ctx_tiles, tiles_per_req, n_groups, n_experts)
        tile_expert, n_used, clear_row, pos1, pos2, n_tiles = _dispatch_plan(
            eidx[:, 0], eidx[:, 1], n_experts, EXPERT_TILE)
        xs = _moe_dispatch(h2, pos1, pos2, clear_row, n_tiles, EXPERT_TILE)
        y_sorted = _moe_experts(xs, tile_expert, n_used, moe_w_gate, moe_w_up, moe_w_down, l, EXPERT_TILE)
        x = _moe_combine(y_sorted, pos1, pos2, xn, ew, mod, final_norm_g, n_ctx_tiles, tiles_per_req,
                         final_norm=(l == depth - 1))

    y_prompt = x[:n_ctx].reshape(batch, seq, d)
    y_sample = x[n_ctx:].reshape(dec_batch, dec_seq, d)
    return (y_prompt, y_sample, new_k, new_v, new_sh, new_sr)
```

```python
import functools

import numpy as np
import jax
import jax.numpy as jnp
from jax import lax
from jax.experimental import pallas as pl
from jax.experimental.pallas import tpu as pltpu

F32 = jnp.float32
BF16 = jnp.bfloat16
EPS = 1e-6
ROPE_BASE = 10000.0
GRID_W = 64
N_GROUPS = 4
TOP_K_INNER = 2
HGRN_BLOCK = 128
HGRN_DIAG = 8
RET_CHUNK = 256
TOKEN_TILE = 256
EXPERT_TILE = 256
V7X_VMEM_LIMIT_BYTES = 56 * 1024 * 1024
HIGHEST = lax.Precision.HIGHEST

_NT = (((1,), (1,)), ((), ()))


def _cparams(*sem):
    return pltpu.CompilerParams(dimension_semantics=sem, vmem_limit_bytes=V7X_VMEM_LIMIT_BYTES)


def _rms(x, g):
    return x * lax.rsqrt(jnp.mean(x * x, axis=-1, keepdims=True) + EPS) * g


def _silu(x):
    return x * jax.nn.sigmoid(x)


def _bdot(a, b):
    return jnp.dot(a.astype(BF16), b.astype(BF16), preferred_element_type=F32)


def _bdot_nt(a, b):
    return lax.dot_general(a.astype(BF16), b.astype(BF16), _NT, preferred_element_type=F32)


def _mod_kernel(c_ref, w_ref, b_ref, o_ref):
    a = _silu(c_ref[...])
    o_ref[0] = jnp.dot(a, w_ref[0], precision=HIGHEST, preferred_element_type=F32) + b_ref[0]


def _ada_mod(cvecs, mod_w, mod_b):
    depth, d, n = mod_w.shape
    tn = n // 4
    rows = cvecs.shape[0]
    return pl.pallas_call(
        _mod_kernel,
        grid=(depth, n // tn),
        in_specs=[pl.BlockSpec((rows, d), lambda l, j: (0, 0)),
                  pl.BlockSpec((1, d, tn), lambda l, j: (l, 0, j)),
                  pl.BlockSpec((1, 1, tn), lambda l, j: (l, 0, j))],
        out_specs=pl.BlockSpec((1, rows, tn), lambda l, j: (l, 0, j)),
        out_shape=jax.ShapeDtypeStruct((depth, rows, n), F32),
        compiler_params=_cparams("arbitrary", "arbitrary"),
        name="ada_mod",
    )(cvecs, mod_w, mod_b.reshape(depth, 1, n))


def _tile_mod_row(i, n_ctx_tiles, tiles_per_req):
    return jnp.where(i < n_ctx_tiles, 0, 1 + jnp.maximum(i - n_ctx_tiles, 0) // tiles_per_req)


def _inproj_kernel(x_ref, g_ref, mod_ref, w_ref, o_ref):
    h = _rms(x_ref[...], g_ref[...]) * (1.0 + mod_ref[0, 1:2, :]) + mod_ref[0, 0:1, :]
    o_ref[...] = jnp.dot(h.astype(BF16), w_ref[...], preferred_element_type=F32)


def _in_proj(x, g, mod, w_bf, n_ctx_tiles, tiles_per_req, tn):
    t, d = x.shape
    n = w_bf.shape[1]
    tm = TOKEN_TILE
    mrow = functools.partial(_tile_mod_row, n_ctx_tiles=n_ctx_tiles, tiles_per_req=tiles_per_req)
    return pl.pallas_call(
        _inproj_kernel,
        grid=(n // tn, t // tm),
        in_specs=[pl.BlockSpec((tm, d), lambda j, i: (i, 0)),
                  pl.BlockSpec((1, d), lambda j, i: (0, 0)),
                  pl.BlockSpec((1, 6, d), lambda j, i: (mrow(i), 0, 0)),
                  pl.BlockSpec((d, tn), lambda j, i: (0, j))],
        out_specs=pl.BlockSpec((tm, tn), lambda j, i: (i, j)),
        out_shape=jax.ShapeDtypeStruct((t, n), F32),
        compiler_params=_cparams("arbitrary", "arbitrary"),
        name="in_proj",
    )(x, g.reshape(1, d), mod, w_bf)


def _hgrn_block(q, k, v, logf, st_t, reverse):
    nb = HGRN_BLOCK
    row = lax.broadcasted_iota(jnp.int32, (nb, nb), 0)
    col = lax.broadcasted_iota(jnp.int32, (nb, nb), 1)
    tri = jnp.where((col >= row) if reverse else (col <= row), 1.0, 0.0).astype(BF16)
    hi = logf.astype(BF16)
    lo = (logf - hi.astype(F32)).astype(BF16)
    c = (jnp.dot(tri, hi, preferred_element_type=F32)
         + jnp.dot(tri, lo, preferred_element_type=F32))
    c_edge = c[0:1, :] if reverse else c[nb - 1:nb, :]

    o = _bdot_nt(q * jnp.exp(c), st_t)
    kd = k * jnp.exp(c_edge - c)
    st_new = st_t * jnp.exp(c_edge) + _bdot(v.T, kd)

    a = jnp.zeros((nb, nb), F32)
    m = nb // 2
    while m >= HGRN_DIAG:
        pieces = []
        for grp in range(nb // (2 * m)):
            idx = grp * 2 * m + (m if reverse else m - 1)
            pieces.append(jnp.broadcast_to(c[idx:idx + 1, :], (2 * m, nb)))
        bnd = pieces[0] if len(pieces) == 1 else jnp.concatenate(pieces, axis=0)
        e = jnp.exp(-jnp.abs(c - bnd))
        am = _bdot_nt(q * e, k * e)
        same = (row & ~(2 * m - 1)) == (col & ~(2 * m - 1))
        r_hi = (row & m) != 0
        c_hi = (col & m) != 0
        if reverse:
            mask = same & jnp.logical_not(r_hi) & c_hi
        else:
            mask = same & r_hi & jnp.logical_not(c_hi)
        a = jnp.where(mask, am, a)
        m //= 2

    dg = HGRN_DIAG
    rowi = lax.broadcasted_iota(jnp.int32, (dg, nb), 0)
    tiles = []
    for j in range(nb // dg):
        qj = q[j * dg:(j + 1) * dg, :]
        cj = c[j * dg:(j + 1) * dg, :]
        for s in range(dg):
            r = j * dg + s
            cs = jnp.broadcast_to(c[r:r + 1, :], (dg, nb))
            ks = jnp.broadcast_to(k[r:r + 1, :], (dg, nb))
            keep = (rowi <= s) if reverse else (rowi >= s)
            tiles.append(qj * ks * jnp.exp(jnp.where(keep, cj - cs, -jnp.inf)))
    p = jnp.concatenate(tiles, axis=0)
    red = jnp.dot(p.astype(BF16), jnp.ones((nb, nb), BF16), preferred_element_type=F32)
    coli = lax.broadcasted_iota(jnp.int32, (dg, nb), 1)
    rows_out = []
    for j in range(nb // dg):
        aj = a[j * dg:(j + 1) * dg, :]
        for s in range(dg):
            r = j * dg + s
            aj = jnp.where(coli == r, red[r * dg:(r + 1) * dg, :], aj)
        rows_out.append(aj)
    a = jnp.concatenate(rows_out, axis=0)
    o = o + _bdot(a, v)
    return o, st_new


def _hgrn_kernel(*refs, nblk, has_state, qscale):
    qa_ref, ia_ref, zf_ref, zb_ref, ga_ref, lb_ref, ng_ref = refs[:7]
    s0_ref = refs[7] if has_state else None
    o_ref, sfin_ref, oacc_ref = refs[-3:]
    nb = HGRN_BLOCK

    for direction in (0, 1):
        reverse = direction == 1
        z_ref = zb_ref if reverse else zf_ref
        lb = lb_ref[direction:direction + 1, :]
        if has_state:
            st0 = s0_ref[direction].T
        else:
            st0 = jnp.zeros((ia_ref.shape[1], qa_ref.shape[1]), F32)

        def body(i, st_t, reverse=reverse, z_ref=z_ref, lb=lb):
            blk = (nblk - 1 - i) if reverse else i
            r0 = pl.multiple_of(blk * nb, nb)
            q = _silu(qa_ref[pl.ds(r0, nb), :]) * qscale
            z = z_ref[pl.ds(r0, nb), :]
            logf = jnp.log(lb + (1.0 - lb) * jax.nn.sigmoid(z))
            k = (1.0 - lb) * jax.nn.sigmoid(-z)
            v = ia_ref[pl.ds(r0, nb), :]
            o, st_new = _hgrn_block(q, k, v, logf, st_t, reverse)
            if reverse:
                oacc_ref[pl.ds(r0, nb), :] += o
            else:
                oacc_ref[pl.ds(r0, nb), :] = o
            return st_new

        st_fin = lax.fori_loop(0, nblk, body, st0)
        sfin_ref[direction] = st_fin.T

    o = oacc_ref[...]
    o_ref[...] = _rms(o, ng_ref[0]) * _silu(ga_ref[...])


def _hgrn(proj, lb, norm_g, s0, layer_idx, out_prev, n_seq, seq, row_blk0, heads, key_d, val_d):
    t = proj.shape[0]
    has_state = s0 is not None
    hk = heads * key_d // 128
    off_qa, off_ia, off_zf, off_zb, off_ga = 0, hk, 2 * hk, 3 * hk, 4 * hk

    def col(off):
        return pl.BlockSpec((seq, 128), lambda b, h: (row_blk0 + b, off + h))

    in_specs = [col(off_qa), col(off_ia), col(off_zf), col(off_zb), col(off_ga),
                pl.BlockSpec((2, key_d), lambda b, h: (0, h)),
                pl.BlockSpec((1, 1, val_d), lambda b, h: (h, 0, 0))]
    args = [proj, proj, proj, proj, proj, lb, norm_g.reshape(heads, 1, val_d)]
    if has_state:
        in_specs.append(pl.BlockSpec((None, None, 2, None, key_d, val_d),
                                     lambda b, h: (b, layer_idx, 0, h, 0, 0)))
        args.append(s0)
    aliases = {}
    if out_prev is not None:
        in_specs.append(pl.BlockSpec(memory_space=pl.ANY))
        args.append(out_prev)
        aliases = {len(args) - 1: 0}

    kern = functools.partial(_hgrn_kernel, nblk=seq // HGRN_BLOCK, has_state=has_state,
                             qscale=float(key_d) ** -0.5)
    return pl.pallas_call(
        kern,
        grid=(n_seq, heads),
        in_specs=in_specs,
        out_specs=[pl.BlockSpec((seq, val_d), lambda b, h: (row_blk0 + b, h)),
                   pl.BlockSpec((None, 2, None, key_d, val_d), lambda b, h: (b, 0, h, 0, 0))],
        out_shape=[jax.ShapeDtypeStruct((t, heads * val_d), F32),
                   jax.ShapeDtypeStruct((n_seq, 2, heads, key_d, val_d), F32)],
        scratch_shapes=[pltpu.VMEM((seq, val_d), F32)],
        input_output_aliases=aliases,
        compiler_params=_cparams("arbitrary", "arbitrary"),
        name="hgrn2",
    )(*args)


def _rope(x, cos, sin_signed):
    return x * cos + pltpu.roll(x, x.shape[-1] // 2, 1) * sin_signed


def _attn_kernel(*refs, has_ctx, grp, qblk, scale):
    if has_ctx:
        (q_ref, k_ref, v_ref, qn_ref, kn_ref, ck_ref, cv_ref, cos_ref, sin_ref, _prev, o_ref) = refs
    else:
        (q_ref, k_ref, v_ref, qn_ref, kn_ref, o_ref, ko_ref, vo_ref) = refs
    seq, hd = k_ref.shape
    k = _rms(k_ref[...], kn_ref[...])
    v = v_ref[...]
    if has_ctx:
        k = _rope(k, cos_ref[...], sin_ref[...])
        ck = ck_ref[...].astype(BF16)
        cv = cv_ref[...].astype(BF16)
    else:
        ko_ref[...] = k
        vo_ref[...] = v
    kb = k.astype(BF16)
    vb = v.astype(BF16)

    for g in range(grp):
        def body(i, carry, g=g):
            r0 = pl.multiple_of(i * qblk, qblk)
            q = _rms(q_ref[pl.ds(r0, qblk), g * hd:(g + 1) * hd], qn_ref[...])
            if has_ctx:
                q = _rope(q, cos_ref[pl.ds(r0, qblk), :], sin_ref[pl.ds(r0, qblk), :])
            qb = q.astype(BF16)
            s2 = lax.dot_general(qb, kb, _NT, preferred_element_type=F32) * scale
            m = jnp.max(s2, axis=-1, keepdims=True)
            if has_ctx:
                s1 = lax.dot_general(qb, ck, _NT, preferred_element_type=F32) * scale
                m = jnp.maximum(m, jnp.max(s1, axis=-1, keepdims=True))
                p1 = jnp.exp(s1 - m)
            p2 = jnp.exp(s2 - m)
            den = jnp.sum(p2, axis=-1, keepdims=True)
            acc = jnp.dot(p2.astype(BF16), vb, preferred_element_type=F32)
            if has_ctx:
                den = den + jnp.sum(p1, axis=-1, keepdims=True)
                acc = acc + jnp.dot(p1.astype(BF16), cv, preferred_element_type=F32)
            o_ref[pl.ds(r0, qblk), g * hd:(g + 1) * hd] = acc / den
            return carry

        lax.fori_loop(0, seq // qblk, body, 0)


def _attn(proj, qn_g, kn_g, ctx, layer_idx, out_prev, n_seq, seq, row_blk0, heads, kv_heads, hd, col0):
    t = proj.shape[0]
    grp = heads // kv_heads
    has_ctx = ctx is not None
    cq = col0 // grp
    ck0 = col0 + heads
    cv0 = ck0 + kv_heads
    in_specs = [pl.BlockSpec((seq, grp * hd), lambda b, h: (row_blk0 + b, cq + h)),
                pl.BlockSpec((seq, hd), lambda b, h: (row_blk0 + b, ck0 + h)),
                pl.BlockSpec((seq, hd), lambda b, h: (row_blk0 + b, cv0 + h)),
                pl.BlockSpec((1, hd), lambda b, h: (0, 0)),
                pl.BlockSpec((1, hd), lambda b, h: (0, 0))]
    args = [proj, proj, proj, qn_g.reshape(1, hd), kn_g.reshape(1, hd)]
    out_specs = [pl.BlockSpec((seq, grp * hd), lambda b, h: (row_blk0 + b, h))]
    out_shape = [jax.ShapeDtypeStruct((t, heads * hd), F32)]
    aliases = {}
    if has_ctx:
        cache_k, cache_v, cos, sin_signed = ctx
        nreq, nlay, past = cache_k.shape[:3]
        cache_k = cache_k.reshape(nreq, nlay, past, kv_heads * hd)
        cache_v = cache_v.reshape(nreq, nlay, past, kv_heads * hd)
        spec_c = pl.BlockSpec((None, None, past, hd), lambda b, h: (b, layer_idx, 0, h))
        in_specs += [spec_c, spec_c,
                     pl.BlockSpec((seq, hd), lambda b, h: (0, 0)),
                     pl.BlockSpec((seq, hd), lambda b, h: (0, 0)),
                     pl.BlockSpec(memory_space=pl.ANY)]
        args += [cache_k, cache_v, cos, sin_signed, out_prev]
        aliases = {len(args) - 1: 0}
    else:
        out_specs += [pl.BlockSpec((seq, hd), lambda b, h: (b, h)),
                      pl.BlockSpec((seq, hd), lambda b, h: (b, h))]
        out_shape += [jax.ShapeDtypeStruct((n_seq * seq, kv_heads * hd), F32)] * 2
    return pl.pallas_call(
        functools.partial(_attn_kernel, has_ctx=has_ctx, grp=grp, qblk=min(seq, 256), scale=float(hd) ** -0.5),
        grid=(n_seq, kv_heads),
        in_specs=in_specs,
        out_specs=out_specs,
        out_shape=out_shape,
        input_output_aliases=aliases,
        compiler_params=_cparams("arbitrary", "arbitrary"),
        name="gqa",
    )(*args)


def _ret_kernel(*refs, nchunk, has_ctx, kscale):
    if has_ctx:
        (lg_ref, q_ref, k_ref, v_ref, g_ref, gn_ref, s0_ref, cos_ref, sin_ref, _prev,
         o_ref, sfin_ref, sb_ref) = refs
    else:
        (lg_ref, q_ref, k_ref, v_ref, g_ref, gn_ref, o_ref, sfin_ref, sb_ref) = refs
    cs = RET_CHUNK
    kd_dim = q_ref.shape[1]
    h = pl.program_id(1)
    lgf = lg_ref[0, h]
    lgb = lg_ref[1, h]
    row = lax.broadcasted_iota(jnp.int32, (cs, cs), 0)
    col = lax.broadcasted_iota(jnp.int32, (cs, cs), 1)
    rel = (row - col).astype(F32)
    dmat = (jnp.exp(jnp.where(rel >= 0, rel * lgf, -jnp.inf))
            + jnp.exp(jnp.where(rel <= 0, -rel * lgb, -jnp.inf)))
    j = lax.broadcasted_iota(jnp.int32, (cs, 1), 0).astype(F32)
    one = jnp.ones((1, 1), F32)
    dec_f = jnp.exp(one * (cs * lgf))
    dec_b = jnp.exp(one * (cs * lgb))

    def load_qk(ref, r0, mul):
        x = ref[pl.ds(r0, cs), :] * mul
        if has_ctx:
            half = kd_dim // 2
            x1, x2 = x[:, :half], x[:, half:]
            cos = cos_ref[pl.ds(r0, cs), :]
            sin = sin_ref[pl.ds(r0, cs), :]
            x = jnp.concatenate([x1 * cos - x2 * sin, x2 * cos + x1 * sin], axis=1)
        return x

    def bwd_body(i, st):
        n = nchunk - 1 - i
        r0 = pl.multiple_of(n * cs, cs)
        sb_ref[n] = st
        k = load_qk(k_ref, r0, kscale)
        v = v_ref[pl.ds(r0, cs), :]
        return st * dec_b + _bdot((k * jnp.exp(j * lgb)).T, v)

    st_b0 = s0_ref[1] if has_ctx else jnp.zeros(sfin_ref.shape[1:], F32)
    st_b = lax.fori_loop(0, nchunk, bwd_body, st_b0)
    sfin_ref[1] = st_b

    def fwd_body(n, st):
        r0 = pl.multiple_of(n * cs, cs)
        q = load_qk(q_ref, r0, 1.0)
        k = load_qk(k_ref, r0, kscale)
        v = v_ref[pl.ds(r0, cs), :]
        a = _bdot_nt(q, k) * dmat
        o = _bdot(a, v)
        o = o + _bdot(q * jnp.exp((j + 1.0) * lgf), st)
        o = o + _bdot(q * jnp.exp((cs - j) * lgb), sb_ref[n])
        mu = jnp.mean(o, axis=-1, keepdims=True)
        var = jnp.mean(jnp.square(o - mu), axis=-1, keepdims=True)
        y = (o - mu) * lax.rsqrt(var + EPS) * gn_ref[0]
        o_ref[pl.ds(r0, cs), :] = _silu(g_ref[pl.ds(r0, cs), :]) * y
        return st * dec_f + _bdot((k * jnp.exp((cs - 1.0 - j) * lgf)).T, v)

    st_f0 = s0_ref[0] if has_ctx else jnp.zeros(sfin_ref.shape[1:], F32)
    sfin_ref[0] = lax.fori_loop(0, nchunk, fwd_body, st_f0)


def _retention(proj, log_gamma, gn_g, ctx, layer_idx, out_prev, n_seq, seq, row_blk0, heads, key_d, val_d):
    t = proj.shape[0]
    has_ctx = ctx is not None
    nq = heads * key_d
    cq, ck, cv, cg = 0, nq // key_d, 2 * nq // val_d, (2 * nq + heads * val_d) // val_d
    in_specs = [pl.BlockSpec(memory_space=pltpu.SMEM),
                pl.BlockSpec((seq, key_d), lambda b, h: (row_blk0 + b, cq + h)),
                pl.BlockSpec((seq, key_d), lambda b, h: (row_blk0 + b, ck + h)),
                pl.BlockSpec((seq, val_d), lambda b, h: (row_blk0 + b, cv + h)),
                pl.BlockSpec((seq, val_d), lambda b, h: (row_blk0 + b, cg + h)),
                pl.BlockSpec((1, 1, val_d), lambda b, h: (h, 0, 0))]
    args = [log_gamma, proj, proj, proj, proj, gn_g.reshape(heads, 1, val_d)]
    aliases = {}
    if has_ctx:
        s0, cos, sin = ctx
        in_specs += [pl.BlockSpec((None, None, 2, None, key_d, val_d), lambda b, h: (b, layer_idx, 0, h, 0, 0)),
                     pl.BlockSpec((seq, key_d // 2), lambda b, h: (0, 0)),
                     pl.BlockSpec((seq, key_d // 2), lambda b, h: (0, 0)),
                     pl.BlockSpec(memory_space=pl.ANY)]
        args += [s0, cos, sin, out_prev]
        aliases = {len(args) - 1: 0}
    nchunk = seq // RET_CHUNK
    return pl.pallas_call(
        functools.partial(_ret_kernel, nchunk=nchunk, has_ctx=has_ctx, kscale=float(key_d) ** -0.5),
        grid=(n_seq, heads),
        in_specs=in_specs,
        out_specs=[pl.BlockSpec((seq, val_d), lambda b, h: (row_blk0 + b, h)),
                   pl.BlockSpec((None, 2, None, key_d, val_d), lambda b, h: (b, 0, h, 0, 0))],
        out_shape=[jax.ShapeDtypeStruct((t, heads * val_d), F32),
                   jax.ShapeDtypeStruct((n_seq, 2, heads, key_d, val_d), F32)],
        scratch_shapes=[pltpu.VMEM((nchunk, key_d, val_d), F32)],
        input_output_aliases=aliases,
        compiler_params=_cparams("arbitrary", "arbitrary"),
        name="retention",
    )(*args)


def _outproj_kernel(*refs, n_in, n_groups, n_experts):
    mix_refs = refs[:n_in]
    w_refs = refs[n_in:2 * n_in]
    x_ref, mod_ref, g2_ref, wr_ref, br_ref, xn_ref, h2_ref, eidx_ref, ew_ref = refs[2 * n_in:]
    acc = _bdot(mix_refs[0][...], w_refs[0][...])
    for mref, wref in zip(mix_refs[1:], w_refs[1:]):
        acc = acc + _bdot(mref[...], wref[...])
    xn = x_ref[...] + mod_ref[0, 2:3, :] * acc
    xn_ref[...] = xn
    h2 = _rms(xn, g2_ref[...]) * (1.0 + mod_ref[0, 4:5, :]) + mod_ref[0, 3:4, :]
    h2_ref[...] = h2.reshape(h2_ref.shape)
    lg = jnp.dot(h2, wr_ref[...], precision=HIGHEST, preferred_element_type=F32) + br_ref[...]

    per_grp = n_experts // n_groups
    lane = lax.broadcasted_iota(jnp.int32, lg.shape, 1).astype(F32)
    big = float(lg.shape[1])
    ninf = -jnp.inf
    gl = jnp.where(lane < n_groups, lg, ninf)
    gmax = jnp.max(gl, axis=-1, keepdims=True)
    p_grp = 1.0 / jnp.sum(jnp.exp(gl - gmax), axis=-1, keepdims=True)
    gidx = jnp.min(jnp.where(gl == gmax, lane, big), axis=-1, keepdims=True)
    in_grp = ((lane >= n_groups) & (lane < n_groups + n_experts)
              & (jnp.floor((lane - n_groups) / per_grp) == gidx))
    el = jnp.where(in_grp, lg, ninf)
    emax = jnp.max(el, axis=-1, keepdims=True)
    esum = jnp.sum(jnp.exp(el - emax), axis=-1, keepdims=True)
    i1 = jnp.min(jnp.where(el == emax, lane, big), axis=-1, keepdims=True)
    el2 = jnp.where(lane == i1, ninf, el)
    emax2 = jnp.max(el2, axis=-1, keepdims=True)
    i2 = jnp.min(jnp.where(el2 == emax2, lane, big), axis=-1, keepdims=True)
    p1 = 1.0 / esum
    p2 = jnp.exp(emax2 - emax) / esum
    w1 = p1 / (p1 + p2) * p_grp
    w2 = p2 / (p1 + p2) * p_grp
    eidx_ref[...] = jnp.where(lane == 0.0, i1 - n_groups, jnp.where(lane == 1.0, i2 - n_groups, 0.0)).astype(jnp.int32)
    ew_ref[...] = jnp.where(lane == 0.0, w1, jnp.where(lane == 1.0, w2, 0.0))


def _out_proj(mixes, w_parts, x, mod, g2, w_router, b_router, n_ctx_tiles, tiles_per_req, n_groups, n_experts):
    t, d = x.shape
    tm = TOKEN_TILE
    n_in = len(mixes)
    mrow = functools.partial(_tile_mod_row, n_ctx_tiles=n_ctx_tiles, tiles_per_req=tiles_per_req)
    in_specs = ([pl.BlockSpec((tm, m.shape[1]), lambda i: (i, 0)) for m in mixes]
                + [pl.BlockSpec(w.shape, lambda i: (0, 0)) for w in w_parts]
                + [pl.BlockSpec((tm, d), lambda i: (i, 0)),
                   pl.BlockSpec((1, 6, d), lambda i: (mrow(i), 0, 0)),
                   pl.BlockSpec((1, d), lambda i: (0, 0)),
                   pl.BlockSpec(w_router.shape, lambda i: (0, 0)),
                   pl.BlockSpec((1, 128), lambda i: (0, 0))])
    return pl.pallas_call(
        functools.partial(_outproj_kernel, n_in=n_in, n_groups=n_groups, n_experts=n_experts),
        grid=(t // tm,),
        in_specs=in_specs,
        out_specs=[pl.BlockSpec((tm, d), lambda i: (i, 0)),
                   pl.BlockSpec((tm, d // 128, 128), lambda i: (i, 0, 0)),
                   pl.BlockSpec((tm, 128), lambda i: (i, 0)),
                   pl.BlockSpec((tm, 128), lambda i: (i, 0))],
        out_shape=[jax.ShapeDtypeStruct((t, d), F32),
                   jax.ShapeDtypeStruct((t, d // 128, 128), F32),
                   jax.ShapeDtypeStruct((t, 128), jnp.int32),
                   jax.ShapeDtypeStruct((t, 128), F32)],
        compiler_params=_cparams("arbitrary"),
        name="out_proj_router",
    )(*mixes, *w_parts, x, mod, g2.reshape(1, d), w_router, b_router)


def _dispatch_plan(e1, e2, n_experts, tile):
    t = e1.shape[0]
    e = jnp.concatenate([e1, e2])
    onehot = (e[:, None] == jnp.arange(n_experts, dtype=jnp.int32)[None, :]).astype(jnp.int32)
    csum = jnp.cumsum(onehot, axis=0)
    rank = jnp.take_along_axis(csum, e[:, None], axis=1)[:, 0] - 1
    counts = csum[-1]
    padded = ((counts + tile - 1) // tile) * tile
    ends = jnp.cumsum(padded)
    starts = ends - padded
    pos = starts[e] + rank
    n_tiles = (2 * t) // tile + n_experts
    n_used = (ends[-1] // tile).astype(jnp.int32)
    tile_start = jnp.arange(n_tiles, dtype=jnp.int32) * tile
    probe = jnp.minimum(tile_start, ends[-1] - 1)
    tile_expert = jnp.sum((ends[None, :] <= probe[:, None]).astype(jnp.int32), axis=1)
    return tile_expert, n_used.reshape(1), pos[:t], pos[t:], n_tiles


def _row_source_kernel(p1_ref, p2_ref, o_ref, *, n_tok, n_rows):
    def clear(r, c):
        o_ref[r] = 0
        return c

    lax.fori_loop(0, n_rows, clear, 0, unroll=8)

    def place(t, c):
        o_ref[p1_ref[t]] = t
        o_ref[p2_ref[t]] = t
        return c

    lax.fori_loop(0, n_tok, place, 0, unroll=8)


def _row_source(pos1, pos2, n_rows):
    n_tok = pos1.shape[0]
    return pl.pallas_call(
        functools.partial(_row_source_kernel, n_tok=n_tok, n_rows=n_rows),
        in_specs=[pl.BlockSpec(memory_space=pltpu.SMEM), pl.BlockSpec(memory_space=pltpu.SMEM)],
        out_specs=pl.BlockSpec(memory_space=pltpu.SMEM),
        out_shape=jax.ShapeDtypeStruct((n_rows,), jnp.int32),
        name="moe_row_source",
    )(pos1, pos2)


def _moe_kernel(te_ref, nu_ref, tok_ref, tokn_ref, h_hbm, wg_ref, wu_ref, wd_ref, o_ref,
                wg_bf, wu_bf, wd_bf, buf, sem, *, tm):
    i = pl.program_id(0)
    slot = lax.rem(i, 2)
    n_used = nu_ref[0]

    def issue(s, idx_ref):
        def body(r, c):
            pltpu.make_async_copy(h_hbm.at[idx_ref[0, 0, r]], buf.at[s, r], sem.at[s]).start()
            return c

        lax.fori_loop(0, tm, body, 0, unroll=8)

    @pl.when(i == 0)
    def _():
        issue(0, tok_ref)

    @pl.when(i + 1 < n_used)
    def _():
        issue(1 - slot, tokn_ref)

    new_expert = jnp.logical_or(i == 0, te_ref[i] != te_ref[jnp.maximum(i - 1, 0)])

    @pl.when(new_expert)
    def _():
        wg_bf[...] = wg_ref[...].astype(BF16)
        wu_bf[...] = wu_ref[...].astype(BF16)
        wd_bf[...] = wd_ref[...].astype(BF16)

    @pl.when(i < n_used)
    def _():
        pltpu.make_async_copy(h_hbm.at[pl.ds(0, tm)], buf.at[slot], sem.at[slot]).wait()
        x = buf[slot].reshape(tm, -1).astype(BF16)
        g = jnp.dot(x, wg_bf[...], preferred_element_type=F32)
        u = jnp.dot(x, wu_bf[...], preferred_element_type=F32)
        act = _silu(g) * u
        y = jnp.dot(act.astype(BF16), wd_bf[...], preferred_element_type=F32)
        o_ref[...] = y.reshape(o_ref.shape)

    @pl.when(i >= n_used)
    def _():
        o_ref[...] = jnp.zeros_like(o_ref)


def _moe_experts(h3, src_tok, tile_expert, n_used, w_gate, w_up, w_down, layer_idx, tm):
    t, sub, lanes = h3.shape
    d = sub * lanes
    n_tiles = src_tok.shape[0] // tm
    f = w_gate.shape[-1]
    tok = src_tok.reshape(n_tiles, 1, tm)
    return pl.pallas_call(
        functools.partial(_moe_kernel, tm=tm),
        grid_spec=pltpu.PrefetchScalarGridSpec(
            num_scalar_prefetch=2,
            grid=(n_tiles,),
            in_specs=[pl.BlockSpec((1, 1, tm), lambda i, te, nu: (i, 0, 0), memory_space=pltpu.SMEM),
                      pl.BlockSpec((1, 1, tm), lambda i, te, nu: (jnp.minimum(i + 1, n_tiles - 1), 0, 0),
                                   memory_space=pltpu.SMEM),
                      pl.BlockSpec(memory_space=pl.ANY),
                      pl.BlockSpec((None, None, d, f), lambda i, te, nu: (layer_idx, te[i], 0, 0)),
                      pl.BlockSpec((None, None, d, f), lambda i, te, nu: (layer_idx, te[i], 0, 0)),
                      pl.BlockSpec((None, None, f, d), lambda i, te, nu: (layer_idx, te[i], 0, 0))],
            out_specs=pl.BlockSpec((tm, sub, lanes), lambda i, te, nu: (i, 0, 0)),
            scratch_shapes=[pltpu.VMEM((d, f), BF16), pltpu.VMEM((d, f), BF16), pltpu.VMEM((f, d), BF16),
                            pltpu.VMEM((2, tm, sub, lanes), F32), pltpu.SemaphoreType.DMA((2,))]),
        out_shape=jax.ShapeDtypeStruct((n_tiles * tm, sub, lanes), F32),
        compiler_params=_cparams("arbitrary"),
        name="moe_experts",
    )(tile_expert, n_used, tok, tok, h3, w_gate, w_up, w_down)


def _combine_kernel(p1_ref, p2_ref, p1n_ref, p2n_ref, y_hbm, x_ref, ew_ref, mod_ref, gf_ref, o_ref, buf, sem,
                    *, tm, n_steps, final_norm):
    i = pl.program_id(0)
    slot = lax.rem(i, 2)

    def issue(s, pa_ref, pb_ref):
        def body(r, c):
            pltpu.make_async_copy(y_hbm.at[pa_ref[0, 0, r]], buf.at[s, 0, r], sem.at[s, 0]).start()
            pltpu.make_async_copy(y_hbm.at[pb_ref[0, 0, r]], buf.at[s, 1, r], sem.at[s, 1]).start()
            return c

        lax.fori_loop(0, tm, body, 0, unroll=8)

    @pl.when(i == 0)
    def _():
        issue(0, p1_ref, p2_ref)

    @pl.when(i + 1 < n_steps)
    def _():
        issue(1 - slot, p1n_ref, p2n_ref)

    for k in range(2):
        pltpu.make_async_copy(y_hbm.at[pl.ds(0, tm)], buf.at[slot, k], sem.at[slot, k]).wait()
    ew = ew_ref[...]
    y = ew[:, 0:1] * buf[slot, 0].reshape(tm, -1) + ew[:, 1:2] * buf[slot, 1].reshape(tm, -1)
    x = x_ref[...] + mod_ref[0, 5:6, :] * y
    if final_norm:
        x = _rms(x, gf_ref[...])
    o_ref[...] = x


def _moe_combine(y_sorted, pos1, pos2, x, ew, mod, final_g, n_ctx_tiles, tiles_per_req, final_norm):
    t, d = x.shape
    tm = TOKEN_TILE
    n_steps = t // tm
    mrow = functools.partial(_tile_mod_row, n_ctx_tiles=n_ctx_tiles, tiles_per_req=tiles_per_req)
    smem_rows = pl.BlockSpec((1, 1, tm), lambda i: (i, 0, 0), memory_space=pltpu.SMEM)
    smem_next = pl.BlockSpec((1, 1, tm), lambda i: (jnp.minimum(i + 1, n_steps - 1), 0, 0),
                             memory_space=pltpu.SMEM)
    p1 = pos1.reshape(n_steps, 1, tm)
    p2 = pos2.reshape(n_steps, 1, tm)
    return pl.pallas_call(
        functools.partial(_combine_kernel, tm=tm, n_steps=n_steps, final_norm=final_norm),
        grid=(n_steps,),
        in_specs=[smem_rows, smem_rows, smem_next, smem_next,
                  pl.BlockSpec(memory_space=pl.ANY),
                  pl.BlockSpec((tm, d), lambda i: (i, 0)),
                  pl.BlockSpec((tm, 128), lambda i: (i, 0)),
                  pl.BlockSpec((1, 6, d), lambda i: (mrow(i), 0, 0)),
                  pl.BlockSpec((1, d), lambda i: (0, 0))],
        out_specs=pl.BlockSpec((tm, d), lambda i: (i, 0)),
        out_shape=jax.ShapeDtypeStruct((t, d), F32),
        scratch_shapes=[pltpu.VMEM((2, 2, tm) + y_sorted.shape[1:], F32), pltpu.SemaphoreType.DMA((2, 2))],
        compiler_params=_cparams("arbitrary"),
        name="moe_combine",
    )(p1, p2, p1, p2, y_sorted, x, ew, mod, final_g.reshape(1, d))


def _axial_angles(seq, head_dim):
    rows = seq // GRID_W
    row = jnp.repeat(jnp.arange(rows), GRID_W).astype(F32)
    col = jnp.tile(jnp.arange(GRID_W), rows).astype(F32)
    n_freq = head_dim // 4
    inv = ROPE_BASE ** (-jnp.arange(n_freq, dtype=F32) / n_freq)
    return jnp.concatenate([row[:, None] * inv, col[:, None] * inv], axis=-1)


def kernel(x_prompt, x_sample, cache_attn_k, cache_attn_v, state_hgrn, state_ret, c, c_ctx, mod_w, mod_b, norm1_g, norm2_g, even_w_in, even_w_out, hgrn_lb_logits, hgrn_norm_g, attn_qn_g, attn_kn_g, odd_w_in, odd_w_out, ret_decay_logit, ret_gn_g, moe_w_group, moe_b_group, moe_w_expert, moe_b_expert, moe_w_gate, moe_w_up, moe_w_down, final_norm_g):
    batch, seq, d = x_prompt.shape
    dec_batch, dec_seq, _ = x_sample.shape
    depth = mod_w.shape[0]
    heads_a, val_a = hgrn_norm_g.shape[1:]
    key_a = hgrn_lb_logits.shape[2] // heads_a
    hd_b = attn_qn_g.shape[1]
    kv_b = cache_attn_k.shape[3]
    heads_b = (even_w_out.shape[1] - heads_a * val_a) // hd_b
    heads_c, val_c = ret_gn_g.shape[1:]
    key_c = state_ret.shape[4]
    n_experts = moe_w_expert.shape[2]
    n_groups = moe_w_group.shape[2]
    tm = TOKEN_TILE
    n_ctx = batch * seq
    n_ctx_tiles = n_ctx // tm
    tiles_per_req = dec_seq // tm
    assert seq % tm == 0 and dec_seq % tm == 0 and n_ctx % dec_seq == 0
    assert key_a == 128 and val_a == 128 and hd_b == 128 and n_groups + n_experts <= 128

    x = jnp.concatenate([x_prompt.reshape(n_ctx, d), x_sample.reshape(dec_batch * dec_seq, d)], axis=0)
    t = x.shape[0]

    n_vec = 1 + dec_batch
    rows = -(-n_vec // 8) * 8
    cvecs = jnp.zeros((rows, d), F32).at[0].set(c_ctx).at[1:n_vec].set(c)
    mods = _ada_mod(cvecs, mod_w, mod_b).reshape(depth, rows, 6, d)

    lower_bounds = jnp.cumsum(jax.nn.softmax(hgrn_lb_logits.astype(F32), axis=0), axis=0)
    log_gamma = jax.nn.log_sigmoid(ret_decay_logit.astype(F32))

    ang_b = _axial_angles(dec_seq, hd_b)
    cos_b = jnp.concatenate([jnp.cos(ang_b), jnp.cos(ang_b)], axis=-1)
    sin_b = jnp.concatenate([-jnp.sin(ang_b), jnp.sin(ang_b)], axis=-1)
    ang_c = _axial_angles(dec_seq, key_c)
    cos_c, sin_c = jnp.cos(ang_c), jnp.sin(ang_c)

    pad = 128 - n_groups - n_experts
    new_k = new_v = new_sh = new_sr = None
    assert depth == 2, "one HGRN2/GQA layer followed by one retention layer"
    for l in range(depth):
        mod = mods[l]
        if l % 2 == 0:
            e = l // 2
            w_in = even_w_in[e].astype(BF16)
            proj = _in_proj(x, norm1_g[l], mod, w_in, n_ctx_tiles, tiles_per_req, w_in.shape[1])
            lb = lower_bounds[l]
            oa, s_ctx = _hgrn(proj, lb, hgrn_norm_g[e], None, e, None, batch, seq, 0, heads_a, key_a, val_a)
            oa, _ = _hgrn(proj, lb, hgrn_norm_g[e], state_hgrn, e, oa, dec_batch, dec_seq, n_ctx // dec_seq,
                          heads_a, key_a, val_a)
            col0 = (3 * heads_a * key_a + 2 * heads_a * val_a) // 128
            ob, kc, vc = _attn(proj, attn_qn_g[e], attn_kn_g[e], None, e, None, batch, seq, 0,
                               heads_b, kv_b, hd_b, col0)
            (ob,) = _attn(proj, attn_qn_g[e], attn_kn_g[e], (cache_attn_k, cache_attn_v, cos_b, sin_b), e, ob,
                          dec_batch, dec_seq, n_ctx // dec_seq, heads_b, kv_b, hd_b, col0)
            new_k = kc.reshape(batch, 1, seq, kv_b, hd_b)
            new_v = vc.reshape(batch, 1, seq, kv_b, hd_b)
            new_sh = s_ctx[:, None]
            w_out = even_w_out[e].astype(BF16)
            mixes = [oa, ob]
            w_parts = [w_out[:heads_a * val_a], w_out[heads_a * val_a:]]
        else:
            o = l // 2
            w_in = odd_w_in[o].astype(BF16)
            proj = _in_proj(x, norm1_g[l], mod, w_in, n_ctx_tiles, tiles_per_req, w_in.shape[1] // 2)
            y, s_ctx = _retention(proj, log_gamma[o], ret_gn_g[o], None, o, None, batch, seq, 0,
                                  heads_c, key_c, val_c)
            y, _ = _retention(proj, log_gamma[o], ret_gn_g[o], (state_ret, cos_c, sin_c), o, y,
                              dec_batch, dec_seq, n_ctx // dec_seq, heads_c, key_c, val_c)
            new_sr = s_ctx[:, None]
            mixes = [y]
            w_parts = [odd_w_out[o].astype(BF16)]

        w_router = jnp.concatenate([moe_w_group[l], moe_w_expert[l], jnp.zeros((d, pad), F32)], axis=1)
        b_router = jnp.concatenate([moe_b_group[l], moe_b_expert[l], jnp.zeros((pad,), F32)]).reshape(1, 128)
        xn, h2, eidx, ew = _out_proj(mixes, w_parts, x, mod, norm2_g[l], w_router, b_router,
                                     n_ctx_tiles, tiles_per_req, n_groups, n_experts)
        tile_expert, n_used, pos1, pos2, n_tiles = _dispatch_plan(eidx[:, 0], eidx[:, 1], n_experts, EXPERT_TILE)
        src_tok = _row_source(pos1, pos2, n_tiles * EXPERT_TILE)
        y_sorted = _moe_experts(h2, src_tok, tile_expert, n_used, moe_w_gate, moe_w_up, moe_w_down, l, EXPERT_TILE)
        x = _moe_combine(y_sorted, pos1, pos2, xn, ew, mod, final_norm_g, n_ctx_tiles, tiles_per_req,
                         final_norm=(l == depth - 1))

    y_prompt = x[:n_ctx].reshape(batch, seq, d)
    y_sample = x[n_ctx:].reshape(dec_batch, dec_seq, d)
    return (y_prompt, y_sample, new_k, new_v, new_sh, new_sr)
```

```python
import functools

import numpy as np
import jax
import jax.numpy as jnp
from jax import lax
from jax.experimental import pallas as pl
from jax.experimental.pallas import tpu as pltpu

F32 = jnp.float32
BF16 = jnp.bfloat16
EPS = 1e-6
ROPE_BASE = 10000.0
GRID_W = 64
N_GROUPS = 4
TOP_K_INNER = 2
HGRN_BLOCK = 128
HGRN_DIAG = 8
RET_CHUNK = 256
TOKEN_TILE = 256
EXPERT_TILE = 256
MOE_GATHER_AHEAD = 2
V7X_VMEM_LIMIT_BYTES = 56 * 1024 * 1024
HIGHEST = lax.Precision.HIGHEST

_NT = (((1,), (1,)), ((), ()))


def _cparams(*sem):
    return pltpu.CompilerParams(dimension_semantics=sem, vmem_limit_bytes=V7X_VMEM_LIMIT_BYTES)


def _rms(x, g):
    return x * lax.rsqrt(jnp.mean(x * x, axis=-1, keepdims=True) + EPS) * g


def _silu(x):
    return x * jax.nn.sigmoid(x)


def _bdot(a, b):
    return jnp.dot(a.astype(BF16), b.astype(BF16), preferred_element_type=F32)


def _bdot_nt(a, b):
    return lax.dot_general(a.astype(BF16), b.astype(BF16), _NT, preferred_element_type=F32)


def _mod_kernel(c_ref, w_ref, b_ref, o_ref):
    a = _silu(c_ref[...])
    o_ref[0] = jnp.dot(a, w_ref[0], precision=HIGHEST, preferred_element_type=F32) + b_ref[0]


def _ada_mod(cvecs, mod_w, mod_b):
    depth, d, n = mod_w.shape
    tn = n // 4
    rows = cvecs.shape[0]
    return pl.pallas_call(
        _mod_kernel,
        grid=(depth, n // tn),
        in_specs=[pl.BlockSpec((rows, d), lambda l, j: (0, 0)),
                  pl.BlockSpec((1, d, tn), lambda l, j: (l, 0, j)),
                  pl.BlockSpec((1, 1, tn), lambda l, j: (l, 0, j))],
        out_specs=pl.BlockSpec((1, rows, tn), lambda l, j: (l, 0, j)),
        out_shape=jax.ShapeDtypeStruct((depth, rows, n), F32),
        compiler_params=_cparams("arbitrary", "arbitrary"),
        name="ada_mod",
    )(cvecs, mod_w, mod_b.reshape(depth, 1, n))


def _tile_mod_row(i, n_ctx_tiles, tiles_per_req):
    return jnp.where(i < n_ctx_tiles, 0, 1 + jnp.maximum(i - n_ctx_tiles, 0) // tiles_per_req)


def _inproj_kernel(x_ref, g_ref, mod_ref, w_ref, o_ref):
    h = _rms(x_ref[...], g_ref[...]) * (1.0 + mod_ref[0, 1:2, :]) + mod_ref[0, 0:1, :]
    o_ref[...] = jnp.dot(h.astype(BF16), w_ref[...], preferred_element_type=F32)


def _in_proj(x, g, mod, w_bf, n_ctx_tiles, tiles_per_req, tn):
    t, d = x.shape
    n = w_bf.shape[1]
    tm = TOKEN_TILE
    mrow = functools.partial(_tile_mod_row, n_ctx_tiles=n_ctx_tiles, tiles_per_req=tiles_per_req)
    return pl.pallas_call(
        _inproj_kernel,
        grid=(n // tn, t // tm),
        in_specs=[pl.BlockSpec((tm, d), lambda j, i: (i, 0)),
                  pl.BlockSpec((1, d), lambda j, i: (0, 0)),
                  pl.BlockSpec((1, 6, d), lambda j, i: (mrow(i), 0, 0)),
                  pl.BlockSpec((d, tn), lambda j, i: (0, j))],
        out_specs=pl.BlockSpec((tm, tn), lambda j, i: (i, j)),
        out_shape=jax.ShapeDtypeStruct((t, n), F32),
        compiler_params=_cparams("arbitrary", "arbitrary"),
        name="in_proj",
    )(x, g.reshape(1, d), mod, w_bf)


def _hgrn_block(q, k, v, logf, st_t, reverse):
    nb = HGRN_BLOCK
    row = lax.broadcasted_iota(jnp.int32, (nb, nb), 0)
    col = lax.broadcasted_iota(jnp.int32, (nb, nb), 1)
    tri = jnp.where((col >= row) if reverse else (col <= row), 1.0, 0.0).astype(BF16)
    hi = logf.astype(BF16)
    lo = (logf - hi.astype(F32)).astype(BF16)
    c = (jnp.dot(tri, hi, preferred_element_type=F32)
         + jnp.dot(tri, lo, preferred_element_type=F32))
    c_edge = c[0:1, :] if reverse else c[nb - 1:nb, :]

    o = _bdot_nt(q * jnp.exp(c), st_t)
    kd = k * jnp.exp(c_edge - c)
    st_new = st_t * jnp.exp(c_edge) + _bdot(v.T, kd)

    a = jnp.zeros((nb, nb), F32)
    m = nb // 2
    while m >= HGRN_DIAG:
        pieces = []
        for grp in range(nb // (2 * m)):
            idx = grp * 2 * m + (m if reverse else m - 1)
            pieces.append(jnp.broadcast_to(c[idx:idx + 1, :], (2 * m, nb)))
        bnd = pieces[0] if len(pieces) == 1 else jnp.concatenate(pieces, axis=0)
        e = jnp.exp(-jnp.abs(c - bnd))
        am = _bdot_nt(q * e, k * e)
        same = (row & ~(2 * m - 1)) == (col & ~(2 * m - 1))
        r_hi = (row & m) != 0
        c_hi = (col & m) != 0
        if reverse:
            mask = same & jnp.logical_not(r_hi) & c_hi
        else:
            mask = same & r_hi & jnp.logical_not(c_hi)
        a = jnp.where(mask, am, a)
        m //= 2

    dg = HGRN_DIAG
    rowi = lax.broadcasted_iota(jnp.int32, (dg, nb), 0)
    tiles = []
    for j in range(nb // dg):
        qj = q[j * dg:(j + 1) * dg, :]
        cj = c[j * dg:(j + 1) * dg, :]
        for s in range(dg):
            r = j * dg + s
            cs = jnp.broadcast_to(c[r:r + 1, :], (dg, nb))
            ks = jnp.broadcast_to(k[r:r + 1, :], (dg, nb))
            keep = (rowi <= s) if reverse else (rowi >= s)
            tiles.append(qj * ks * jnp.exp(jnp.where(keep, cj - cs, -jnp.inf)))
    p = jnp.concatenate(tiles, axis=0)
    red = jnp.dot(p.astype(BF16), jnp.ones((nb, nb), BF16), preferred_element_type=F32)
    coli = lax.broadcasted_iota(jnp.int32, (dg, nb), 1)
    rows_out = []
    for j in range(nb // dg):
        aj = a[j * dg:(j + 1) * dg, :]
        for s in range(dg):
            r = j * dg + s
            aj = jnp.where(coli == r, red[r * dg:(r + 1) * dg, :], aj)
        rows_out.append(aj)
    a = jnp.concatenate(rows_out, axis=0)
    o = o + _bdot(a, v)
    return o, st_new


def _hgrn_kernel(*refs, nblk, has_state, qscale):
    qa_ref, ia_ref, zf_ref, zb_ref, ga_ref, lb_ref, ng_ref = refs[:7]
    s0_ref = refs[7] if has_state else None
    o_ref, sfin_ref, oacc_ref = refs[-3:]
    nb = HGRN_BLOCK

    for direction in (0, 1):
        reverse = direction == 1
        z_ref = zb_ref if reverse else zf_ref
        lb = lb_ref[direction:direction + 1, :]
        if has_state:
            st0 = s0_ref[direction].T
        else:
            st0 = jnp.zeros((ia_ref.shape[1], qa_ref.shape[1]), F32)

        def body(i, st_t, reverse=reverse, z_ref=z_ref, lb=lb):
            blk = (nblk - 1 - i) if reverse else i
            r0 = pl.multiple_of(blk * nb, nb)
            q = _silu(qa_ref[pl.ds(r0, nb), :]) * qscale
            z = z_ref[pl.ds(r0, nb), :]
            logf = jnp.log(lb + (1.0 - lb) * jax.nn.sigmoid(z))
            k = (1.0 - lb) * jax.nn.sigmoid(-z)
            v = ia_ref[pl.ds(r0, nb), :]
            o, st_new = _hgrn_block(q, k, v, logf, st_t, reverse)
            if reverse:
                oacc_ref[pl.ds(r0, nb), :] += o
            else:
                oacc_ref[pl.ds(r0, nb), :] = o
            return st_new

        st_fin = lax.fori_loop(0, nblk, body, st0)
        sfin_ref[direction] = st_fin.T

    o = oacc_ref[...]
    o_ref[...] = _rms(o, ng_ref[0]) * _silu(ga_ref[...])


def _hgrn(proj, lb, norm_g, s0, layer_idx, out_prev, n_seq, seq, row_blk0, heads, key_d, val_d):
    t = proj.shape[0]
    has_state = s0 is not None
    hk = heads * key_d // 128
    off_qa, off_ia, off_zf, off_zb, off_ga = 0, hk, 2 * hk, 3 * hk, 4 * hk

    def col(off):
        return pl.BlockSpec((seq, 128), lambda b, h: (row_blk0 + b, off + h))

    in_specs = [col(off_qa), col(off_ia), col(off_zf), col(off_zb), col(off_ga),
                pl.BlockSpec((2, key_d), lambda b, h: (0, h)),
                pl.BlockSpec((1, 1, val_d), lambda b, h: (h, 0, 0))]
    args = [proj, proj, proj, proj, proj, lb, norm_g.reshape(heads, 1, val_d)]
    if has_state:
        in_specs.append(pl.BlockSpec((None, None, 2, None, key_d, val_d),
                                     lambda b, h: (b, layer_idx, 0, h, 0, 0)))
        args.append(s0)
    aliases = {}
    if out_prev is not None:
        in_specs.append(pl.BlockSpec(memory_space=pl.ANY))
        args.append(out_prev)
        aliases = {len(args) - 1: 0}

    kern = functools.partial(_hgrn_kernel, nblk=seq // HGRN_BLOCK, has_state=has_state,
                             qscale=float(key_d) ** -0.5)
    return pl.pallas_call(
        kern,
        grid=(n_seq, heads),
        in_specs=in_specs,
        out_specs=[pl.BlockSpec((seq, val_d), lambda b, h: (row_blk0 + b, h)),
                   pl.BlockSpec((None, 2, None, key_d, val_d), lambda b, h: (b, 0, h, 0, 0))],
        out_shape=[jax.ShapeDtypeStruct((t, heads * val_d), F32),
                   jax.ShapeDtypeStruct((n_seq, 2, heads, key_d, val_d), F32)],
        scratch_shapes=[pltpu.VMEM((seq, val_d), F32)],
        input_output_aliases=aliases,
        compiler_params=_cparams("arbitrary", "arbitrary"),
        name="hgrn2",
    )(*args)


def _rope(x, cos, sin_signed):
    return x * cos + pltpu.roll(x, x.shape[-1] // 2, 1) * sin_signed


def _attn_kernel(*refs, has_ctx, grp, qblk, scale):
    if has_ctx:
        (q_ref, k_ref, v_ref, qn_ref, kn_ref, ck_ref, cv_ref, cos_ref, sin_ref, _prev, o_ref) = refs
    else:
        (q_ref, k_ref, v_ref, qn_ref, kn_ref, o_ref, ko_ref, vo_ref) = refs
    seq, hd = k_ref.shape
    k = _rms(k_ref[...], kn_ref[...])
    v = v_ref[...]
    if has_ctx:
        k = _rope(k, cos_ref[...], sin_ref[...])
        ck = ck_ref[...].astype(BF16)
        cv = cv_ref[...].astype(BF16)
    else:
        ko_ref[...] = k
        vo_ref[...] = v
    kb = k.astype(BF16)
    vb = v.astype(BF16)

    for g in range(grp):
        def body(i, carry, g=g):
            r0 = pl.multiple_of(i * qblk, qblk)
            q = _rms(q_ref[pl.ds(r0, qblk), g * hd:(g + 1) * hd], qn_ref[...])
            if has_ctx:
                q = _rope(q, cos_ref[pl.ds(r0, qblk), :], sin_ref[pl.ds(r0, qblk), :])
            qb = q.astype(BF16)
            s2 = lax.dot_general(qb, kb, _NT, preferred_element_type=F32) * scale
            m = jnp.max(s2, axis=-1, keepdims=True)
            if has_ctx:
                s1 = lax.dot_general(qb, ck, _NT, preferred_element_type=F32) * scale
                m = jnp.maximum(m, jnp.max(s1, axis=-1, keepdims=True))
                p1 = jnp.exp(s1 - m)
            p2 = jnp.exp(s2 - m)
            den = jnp.sum(p2, axis=-1, keepdims=True)
            acc = jnp.dot(p2.astype(BF16), vb, preferred_element_type=F32)
            if has_ctx:
                den = den + jnp.sum(p1, axis=-1, keepdims=True)
                acc = acc + jnp.dot(p1.astype(BF16), cv, preferred_element_type=F32)
            o_ref[pl.ds(r0, qblk), g * hd:(g + 1) * hd] = acc / den
            return carry

        lax.fori_loop(0, seq // qblk, body, 0)


def _attn(proj, qn_g, kn_g, ctx, layer_idx, out_prev, n_seq, seq, row_blk0, heads, kv_heads, hd, col0):
    t = proj.shape[0]
    grp = heads // kv_heads
    has_ctx = ctx is not None
    cq = col0 // grp
    ck0 = col0 + heads
    cv0 = ck0 + kv_heads
    in_specs = [pl.BlockSpec((seq, grp * hd), lambda b, h: (row_blk0 + b, cq + h)),
                pl.BlockSpec((seq, hd), lambda b, h: (row_blk0 + b, ck0 + h)),
                pl.BlockSpec((seq, hd), lambda b, h: (row_blk0 + b, cv0 + h)),
                pl.BlockSpec((1, hd), lambda b, h: (0, 0)),
                pl.BlockSpec((1, hd), lambda b, h: (0, 0))]
    args = [proj, proj, proj, qn_g.reshape(1, hd), kn_g.reshape(1, hd)]
    out_specs = [pl.BlockSpec((seq, grp * hd), lambda b, h: (row_blk0 + b, h))]
    out_shape = [jax.ShapeDtypeStruct((t, heads * hd), F32)]
    aliases = {}
    if has_ctx:
        cache_k, cache_v, cos, sin_signed = ctx
        nreq, nlay, past = cache_k.shape[:3]
        cache_k = cache_k.reshape(nreq, nlay, past, kv_heads * hd)
        cache_v = cache_v.reshape(nreq, nlay, past, kv_heads * hd)
        spec_c = pl.BlockSpec((None, None, past, hd), lambda b, h: (b, layer_idx, 0, h))
        in_specs += [spec_c, spec_c,
                     pl.BlockSpec((seq, hd), lambda b, h: (0, 0)),
                     pl.BlockSpec((seq, hd), lambda b, h: (0, 0)),
                     pl.BlockSpec(memory_space=pl.ANY)]
        args += [cache_k, cache_v, cos, sin_signed, out_prev]
        aliases = {len(args) - 1: 0}
    else:
        out_specs += [pl.BlockSpec((seq, hd), lambda b, h: (b, h)),
                      pl.BlockSpec((seq, hd), lambda b, h: (b, h))]
        out_shape += [jax.ShapeDtypeStruct((n_seq * seq, kv_heads * hd), F32)] * 2
    return pl.pallas_call(
        functools.partial(_attn_kernel, has_ctx=has_ctx, grp=grp, qblk=min(seq, 256), scale=float(hd) ** -0.5),
        grid=(n_seq, kv_heads),
        in_specs=in_specs,
        out_specs=out_specs,
        out_shape=out_shape,
        input_output_aliases=aliases,
        compiler_params=_cparams("arbitrary", "arbitrary"),
        name="gqa",
    )(*args)


def _ret_kernel(*refs, nchunk, has_ctx, kscale):
    if has_ctx:
        (lg_ref, q_ref, k_ref, v_ref, g_ref, gn_ref, s0_ref, cos_ref, sin_ref, _prev,
         o_ref, sfin_ref, sb_ref) = refs
    else:
        (lg_ref, q_ref, k_ref, v_ref, g_ref, gn_ref, o_ref, sfin_ref, sb_ref) = refs
    cs = RET_CHUNK
    kd_dim = q_ref.shape[1]
    h = pl.program_id(1)
    lgf = lg_ref[0, h]
    lgb = lg_ref[1, h]
    row = lax.broadcasted_iota(jnp.int32, (cs, cs), 0)
    col = lax.broadcasted_iota(jnp.int32, (cs, cs), 1)
    rel = (row - col).astype(F32)
    dmat = (jnp.exp(jnp.where(rel >= 0, rel * lgf, -jnp.inf))
            + jnp.exp(jnp.where(rel <= 0, -rel * lgb, -jnp.inf)))
    j = lax.broadcasted_iota(jnp.int32, (cs, 1), 0).astype(F32)
    one = jnp.ones((1, 1), F32)
    dec_f = jnp.exp(one * (cs * lgf))
    dec_b = jnp.exp(one * (cs * lgb))

    def load_qk(ref, r0, mul):
        x = ref[pl.ds(r0, cs), :] * mul
        if has_ctx:
            half = kd_dim // 2
            x1, x2 = x[:, :half], x[:, half:]
            cos = cos_ref[pl.ds(r0, cs), :]
            sin = sin_ref[pl.ds(r0, cs), :]
            x = jnp.concatenate([x1 * cos - x2 * sin, x2 * cos + x1 * sin], axis=1)
        return x

    def bwd_body(i, st):
        n = nchunk - 1 - i
        r0 = pl.multiple_of(n * cs, cs)
        sb_ref[n] = st
        k = load_qk(k_ref, r0, kscale)
        v = v_ref[pl.ds(r0, cs), :]
        return st * dec_b + _bdot((k * jnp.exp(j * lgb)).T, v)

    st_b0 = s0_ref[1] if has_ctx else jnp.zeros(sfin_ref.shape[1:], F32)
    st_b = lax.fori_loop(0, nchunk, bwd_body, st_b0)
    sfin_ref[1] = st_b

    def fwd_body(n, st):
        r0 = pl.multiple_of(n * cs, cs)
        q = load_qk(q_ref, r0, 1.0)
        k = load_qk(k_ref, r0, kscale)
        v = v_ref[pl.ds(r0, cs), :]
        a = _bdot_nt(q, k) * dmat
        o = _bdot(a, v)
        o = o + _bdot(q * jnp.exp((j + 1.0) * lgf), st)
        o = o + _bdot(q * jnp.exp((cs - j) * lgb), sb_ref[n])
        mu = jnp.mean(o, axis=-1, keepdims=True)
        var = jnp.mean(jnp.square(o - mu), axis=-1, keepdims=True)
        y = (o - mu) * lax.rsqrt(var + EPS) * gn_ref[0]
        o_ref[pl.ds(r0, cs), :] = _silu(g_ref[pl.ds(r0, cs), :]) * y
        return st * dec_f + _bdot((k * jnp.exp((cs - 1.0 - j) * lgf)).T, v)

    st_f0 = s0_ref[0] if has_ctx else jnp.zeros(sfin_ref.shape[1:], F32)
    sfin_ref[0] = lax.fori_loop(0, nchunk, fwd_body, st_f0)


def _retention(proj, log_gamma, gn_g, ctx, layer_idx, out_prev, n_seq, seq, row_blk0, heads, key_d, val_d):
    t = proj.shape[0]
    has_ctx = ctx is not None
    nq = heads * key_d
    cq, ck, cv, cg = 0, nq // key_d, 2 * nq // val_d, (2 * nq + heads * val_d) // val_d
    in_specs = [pl.BlockSpec(memory_space=pltpu.SMEM),
                pl.BlockSpec((seq, key_d), lambda b, h: (row_blk0 + b, cq + h)),
                pl.BlockSpec((seq, key_d), lambda b, h: (row_blk0 + b, ck + h)),
                pl.BlockSpec((seq, val_d), lambda b, h: (row_blk0 + b, cv + h)),
                pl.BlockSpec((seq, val_d), lambda b, h: (row_blk0 + b, cg + h)),
                pl.BlockSpec((1, 1, val_d), lambda b, h: (h, 0, 0))]
    args = [log_gamma, proj, proj, proj, proj, gn_g.reshape(heads, 1, val_d)]
    aliases = {}
    if has_ctx:
        s0, cos, sin = ctx
        in_specs += [pl.BlockSpec((None, None, 2, None, key_d, val_d), lambda b, h: (b, layer_idx, 0, h, 0, 0)),
                     pl.BlockSpec((seq, key_d // 2), lambda b, h: (0, 0)),
                     pl.BlockSpec((seq, key_d // 2), lambda b, h: (0, 0)),
                     pl.BlockSpec(memory_space=pl.ANY)]
        args += [s0, cos, sin, out_prev]
        aliases = {len(args) - 1: 0}
    nchunk = seq // RET_CHUNK
    return pl.pallas_call(
        functools.partial(_ret_kernel, nchunk=nchunk, has_ctx=has_ctx, kscale=float(key_d) ** -0.5),
        grid=(n_seq, heads),
        in_specs=in_specs,
        out_specs=[pl.BlockSpec((seq, val_d), lambda b, h: (row_blk0 + b, h)),
                   pl.BlockSpec((None, 2, None, key_d, val_d), lambda b, h: (b, 0, h, 0, 0))],
        out_shape=[jax.ShapeDtypeStruct((t, heads * val_d), F32),
                   jax.ShapeDtypeStruct((n_seq, 2, heads, key_d, val_d), F32)],
        scratch_shapes=[pltpu.VMEM((nchunk, key_d, val_d), F32)],
        input_output_aliases=aliases,
        compiler_params=_cparams("arbitrary", "arbitrary"),
        name="retention",
    )(*args)


def _outproj_kernel(*refs, n_in, n_groups, n_experts):
    mix_refs = refs[:n_in]
    w_refs = refs[n_in:2 * n_in]
    x_ref, mod_ref, g2_ref, wr_ref, br_ref, xn_ref, h2_ref, eidx_ref, ew_ref = refs[2 * n_in:]
    acc = _bdot(mix_refs[0][...], w_refs[0][...])
    for mref, wref in zip(mix_refs[1:], w_refs[1:]):
        acc = acc + _bdot(mref[...], wref[...])
    xn = x_ref[...] + mod_ref[0, 2:3, :] * acc
    xn_ref[...] = xn
    h2 = _rms(xn, g2_ref[...]) * (1.0 + mod_ref[0, 4:5, :]) + mod_ref[0, 3:4, :]
    h2_ref[...] = h2.reshape(h2_ref.shape)
    h_hi = h2.astype(BF16)
    h_lo = (h2 - h_hi.astype(F32)).astype(BF16)
    lg = (jnp.dot(h_hi, wr_ref[0], preferred_element_type=F32)
          + jnp.dot(h_hi, wr_ref[1], preferred_element_type=F32)
          + jnp.dot(h_lo, wr_ref[0], preferred_element_type=F32)) + br_ref[...]

    per_grp = n_experts // n_groups
    lane = lax.broadcasted_iota(jnp.int32, lg.shape, 1).astype(F32)
    big = float(lg.shape[1])
    ninf = -jnp.inf
    gl = jnp.where(lane < n_groups, lg, ninf)
    gmax = jnp.max(gl, axis=-1, keepdims=True)
    p_grp = 1.0 / jnp.sum(jnp.exp(gl - gmax), axis=-1, keepdims=True)
    gidx = jnp.min(jnp.where(gl == gmax, lane, big), axis=-1, keepdims=True)
    in_grp = ((lane >= n_groups) & (lane < n_groups + n_experts)
              & (jnp.floor((lane - n_groups) / per_grp) == gidx))
    el = jnp.where(in_grp, lg, ninf)
    emax = jnp.max(el, axis=-1, keepdims=True)
    esum = jnp.sum(jnp.exp(el - emax), axis=-1, keepdims=True)
    i1 = jnp.min(jnp.where(el == emax, lane, big), axis=-1, keepdims=True)
    el2 = jnp.where(lane == i1, ninf, el)
    emax2 = jnp.max(el2, axis=-1, keepdims=True)
    i2 = jnp.min(jnp.where(el2 == emax2, lane, big), axis=-1, keepdims=True)
    p1 = 1.0 / esum
    p2 = jnp.exp(emax2 - emax) / esum
    w1 = p1 / (p1 + p2) * p_grp
    w2 = p2 / (p1 + p2) * p_grp
    eidx_ref[...] = jnp.where(lane == 0.0, i1 - n_groups, jnp.where(lane == 1.0, i2 - n_groups, 0.0)).astype(jnp.int32)
    ew_ref[...] = jnp.where(lane == 0.0, w1, jnp.where(lane == 1.0, w2, 0.0))


def _out_proj(mixes, w_parts, x, mod, g2, w_router, b_router, n_ctx_tiles, tiles_per_req, n_groups, n_experts):
    t, d = x.shape
    tm = TOKEN_TILE
    n_in = len(mixes)
    mrow = functools.partial(_tile_mod_row, n_ctx_tiles=n_ctx_tiles, tiles_per_req=tiles_per_req)
    in_specs = ([pl.BlockSpec((tm, m.shape[1]), lambda i: (i, 0)) for m in mixes]
                + [pl.BlockSpec(w.shape, lambda i: (0, 0)) for w in w_parts]
                + [pl.BlockSpec((tm, d), lambda i: (i, 0)),
                   pl.BlockSpec((1, 6, d), lambda i: (mrow(i), 0, 0)),
                   pl.BlockSpec((1, d), lambda i: (0, 0)),
                   pl.BlockSpec(w_router.shape, lambda i: (0, 0, 0)),
                   pl.BlockSpec((1, 128), lambda i: (0, 0))])
    return pl.pallas_call(
        functools.partial(_outproj_kernel, n_in=n_in, n_groups=n_groups, n_experts=n_experts),
        grid=(t // tm,),
        in_specs=in_specs,
        out_specs=[pl.BlockSpec((tm, d), lambda i: (i, 0)),
                   pl.BlockSpec((tm, d // 128, 128), lambda i: (i, 0, 0)),
                   pl.BlockSpec((tm, 128), lambda i: (i, 0)),
                   pl.BlockSpec((tm, 128), lambda i: (i, 0))],
        out_shape=[jax.ShapeDtypeStruct((t, d), F32),
                   jax.ShapeDtypeStruct((t, d // 128, 128), F32),
                   jax.ShapeDtypeStruct((t, 128), jnp.int32),
                   jax.ShapeDtypeStruct((t, 128), F32)],
        compiler_params=_cparams("arbitrary"),
        name="out_proj_router",
    )(*mixes, *w_parts, x, mod, g2.reshape(1, d), w_router, b_router)


def _dispatch_plan(e1, e2, n_experts, tile):
    t = e1.shape[0]
    e = jnp.concatenate([e1, e2])
    onehot = (e[:, None] == jnp.arange(n_experts, dtype=jnp.int32)[None, :]).astype(jnp.int32)
    csum = jnp.cumsum(onehot, axis=0)
    rank = jnp.take_along_axis(csum, e[:, None], axis=1)[:, 0] - 1
    counts = csum[-1]
    padded = ((counts + tile - 1) // tile) * tile
    ends = jnp.cumsum(padded)
    starts = ends - padded
    pos = starts[e] + rank
    n_tiles = (2 * t) // tile + n_experts
    n_used = (ends[-1] // tile).astype(jnp.int32)
    tile_start = jnp.arange(n_tiles, dtype=jnp.int32) * tile
    probe = jnp.minimum(tile_start, ends[-1] - 1)
    tile_expert = jnp.sum((ends[None, :] <= probe[:, None]).astype(jnp.int32), axis=1)
    return tile_expert, n_used.reshape(1), ends.astype(jnp.int32), pos[:t], pos[t:], n_tiles


def _row_source_kernel(ends_ref, p1_ref, p2_ref, o_ref, *, n_tok, n_rows, tile, n_experts):
    def clear8(j, c):
        for u in range(8):
            o_ref[j * 8 + u] = 0
        return c

    def per_expert(e, c):
        first = lax.shift_right_logical(jnp.maximum(ends_ref[e] - tile, 0), 3)
        lax.fori_loop(first, first + tile // 8, clear8, 0)
        return c

    lax.fori_loop(0, n_experts, per_expert, 0)
    lax.fori_loop(lax.shift_right_logical(ends_ref[n_experts - 1], 3), n_rows // 8, clear8, 0)

    def place(t, c):
        o_ref[p1_ref[t]] = t
        o_ref[p2_ref[t]] = t
        return c

    lax.fori_loop(0, n_tok, place, 0, unroll=8)


def _row_source(ends, pos1, pos2, n_rows, tile):
    n_tok = pos1.shape[0]
    smem = pl.BlockSpec(memory_space=pltpu.SMEM)
    return pl.pallas_call(
        functools.partial(_row_source_kernel, n_tok=n_tok, n_rows=n_rows, tile=tile, n_experts=ends.shape[0]),
        in_specs=[smem, smem, smem],
        out_specs=smem,
        out_shape=jax.ShapeDtypeStruct((n_rows,), jnp.int32),
        name="moe_row_source",
    )(ends, pos1, pos2)


def _moe_kernel(te_ref, nu_ref, tok_ref, h_hbm, wg_ref, wu_ref, wd_ref, o_ref,
                wg_bf, wu_bf, wd_bf, buf, sem, *, tm, ahead):
    i = pl.program_id(0)
    nslot = ahead + 1
    slot = lax.rem(i, nslot)
    n_used = nu_ref[0]

    def issue(tile):
        s = lax.rem(tile, nslot)
        base = tile * tm

        def body(r, c):
            pltpu.make_async_copy(h_hbm.at[tok_ref[base + r]], buf.at[s, r], sem.at[s]).start(priority=1)
            return c

        lax.fori_loop(0, tm, body, 0, unroll=8)

    @pl.when(i == 0)
    def _():
        for j in range(ahead):
            @pl.when(j < n_used)
            def _():
                issue(jnp.int32(j))

    @pl.when(i + ahead < n_used)
    def _():
        issue(i + ahead)

    new_expert = jnp.logical_or(i == 0, te_ref[i] != te_ref[jnp.maximum(i - 1, 0)])

    @pl.when(new_expert)
    def _():
        wg_bf[...] = wg_ref[...].astype(BF16)
        wu_bf[...] = wu_ref[...].astype(BF16)
        wd_bf[...] = wd_ref[...].astype(BF16)

    @pl.when(i < n_used)
    def _():
        pltpu.make_async_copy(h_hbm.at[pl.ds(0, tm)], buf.at[slot], sem.at[slot]).wait()
        x = buf[slot].reshape(tm, -1).astype(BF16)
        g = jnp.dot(x, wg_bf[...], preferred_element_type=F32)
        u = jnp.dot(x, wu_bf[...], preferred_element_type=F32)
        act = _silu(g) * u
        y = jnp.dot(act.astype(BF16), wd_bf[...], preferred_element_type=F32)
        o_ref[...] = y.reshape(o_ref.shape)

    @pl.when(i >= n_used)
    def _():
        o_ref[...] = jnp.zeros_like(o_ref)


def _moe_experts(h3, src_tok, tile_expert, n_used, w_gate, w_up, w_down, layer_idx, tm):
    t, sub, lanes = h3.shape
    d = sub * lanes
    n_tiles = src_tok.shape[0] // tm
    f = w_gate.shape[-1]
    ahead = MOE_GATHER_AHEAD
    return pl.pallas_call(
        functools.partial(_moe_kernel, tm=tm, ahead=ahead),
        grid_spec=pltpu.PrefetchScalarGridSpec(
            num_scalar_prefetch=3,
            grid=(n_tiles,),
            in_specs=[pl.BlockSpec(memory_space=pl.ANY),
                      pl.BlockSpec((None, None, d, f), lambda i, te, nu, tok: (layer_idx, te[i], 0, 0)),
                      pl.BlockSpec((None, None, d, f), lambda i, te, nu, tok: (layer_idx, te[i], 0, 0)),
                      pl.BlockSpec((None, None, f, d), lambda i, te, nu, tok: (layer_idx, te[i], 0, 0))],
            out_specs=pl.BlockSpec((tm, sub, lanes), lambda i, te, nu, tok: (i, 0, 0)),
            scratch_shapes=[pltpu.VMEM((d, f), BF16), pltpu.VMEM((d, f), BF16), pltpu.VMEM((f, d), BF16),
                            pltpu.VMEM((ahead + 1, tm, sub, lanes), F32),
                            pltpu.SemaphoreType.DMA((ahead + 1,))]),
        out_shape=jax.ShapeDtypeStruct((n_tiles * tm, sub, lanes), F32),
        compiler_params=_cparams("arbitrary"),
        name="moe_experts",
    )(tile_expert, n_used, src_tok, h3, w_gate, w_up, w_down)


def _combine_kernel(p1_ref, p2_ref, p1n_ref, p2n_ref, y_hbm, x_ref, ew_ref, mod_ref, gf_ref, o_ref, buf, sem,
                    *, tm, n_steps, final_norm):
    i = pl.program_id(0)
    slot = lax.rem(i, 2)

    def issue(s, pa_ref, pb_ref):
        def body(r, c):
            pltpu.make_async_copy(y_hbm.at[pa_ref[0, 0, r]], buf.at[s, 0, r], sem.at[s, 0]).start()
            pltpu.make_async_copy(y_hbm.at[pb_ref[0, 0, r]], buf.at[s, 1, r], sem.at[s, 1]).start()
            return c

        lax.fori_loop(0, tm, body, 0, unroll=8)

    @pl.when(i == 0)
    def _():
        issue(0, p1_ref, p2_ref)

    @pl.when(i + 1 < n_steps)
    def _():
        issue(1 - slot, p1n_ref, p2n_ref)

    for k in range(2):
        pltpu.make_async_copy(y_hbm.at[pl.ds(0, tm)], buf.at[slot, k], sem.at[slot, k]).wait()
    ew = ew_ref[...]
    y = ew[:, 0:1] * buf[slot, 0].reshape(tm, -1) + ew[:, 1:2] * buf[slot, 1].reshape(tm, -1)
    x = x_ref[...] + mod_ref[0, 5:6, :] * y
    if final_norm:
        x = _rms(x, gf_ref[...])
    o_ref[...] = x


def _moe_combine(y_sorted, pos1, pos2, x, ew, mod, final_g, n_ctx_tiles, tiles_per_req, final_norm):
    t, d = x.shape
    tm = TOKEN_TILE
    n_steps = t // tm
    mrow = functools.partial(_tile_mod_row, n_ctx_tiles=n_ctx_tiles, tiles_per_req=tiles_per_req)
    smem_rows = pl.BlockSpec((1, 1, tm), lambda i: (i, 0, 0), memory_space=pltpu.SMEM)
    smem_next = pl.BlockSpec((1, 1, tm), lambda i: (jnp.minimum(i + 1, n_steps - 1), 0, 0),
                             memory_space=pltpu.SMEM)
    p1 = pos1.reshape(n_steps, 1, tm)
    p2 = pos2.reshape(n_steps, 1, tm)
    return pl.pallas_call(
        functools.partial(_combine_kernel, tm=tm, n_steps=n_steps, final_norm=final_norm),
        grid=(n_steps,),
        in_specs=[smem_rows, smem_rows, smem_next, smem_next,
                  pl.BlockSpec(memory_space=pl.ANY),
                  pl.BlockSpec((tm, d), lambda i: (i, 0)),
                  pl.BlockSpec((tm, 128), lambda i: (i, 0)),
                  pl.BlockSpec((1, 6, d), lambda i: (mrow(i), 0, 0)),
                  pl.BlockSpec((1, d), lambda i: (0, 0))],
        out_specs=pl.BlockSpec((tm, d), lambda i: (i, 0)),
        out_shape=jax.ShapeDtypeStruct((t, d), F32),
        scratch_shapes=[pltpu.VMEM((2, 2, tm) + y_sorted.shape[1:], F32), pltpu.SemaphoreType.DMA((2, 2))],
        compiler_params=_cparams("arbitrary"),
        name="moe_combine",
    )(p1, p2, p1, p2, y_sorted, x, ew, mod, final_g.reshape(1, d))


def _axial_angles(seq, head_dim):
    rows = seq // GRID_W
    row = jnp.repeat(jnp.arange(rows), GRID_W).astype(F32)
    col = jnp.tile(jnp.arange(GRID_W), rows).astype(F32)
    n_freq = head_dim // 4
    inv = ROPE_BASE ** (-jnp.arange(n_freq, dtype=F32) / n_freq)
    return jnp.concatenate([row[:, None] * inv, col[:, None] * inv], axis=-1)


def kernel(x_prompt, x_sample, cache_attn_k, cache_attn_v, state_hgrn, state_ret, c, c_ctx, mod_w, mod_b, norm1_g, norm2_g, even_w_in, even_w_out, hgrn_lb_logits, hgrn_norm_g, attn_qn_g, attn_kn_g, odd_w_in, odd_w_out, ret_decay_logit, ret_gn_g, moe_w_group, moe_b_group, moe_w_expert, moe_b_expert, moe_w_gate, moe_w_up, moe_w_down, final_norm_g):
    batch, seq, d = x_prompt.shape
    dec_batch, dec_seq, _ = x_sample.shape
    depth = mod_w.shape[0]
    heads_a, val_a = hgrn_norm_g.shape[1:]
    key_a = hgrn_lb_logits.shape[2] // heads_a
    hd_b = attn_qn_g.shape[1]
    kv_b = cache_attn_k.shape[3]
    heads_b = (even_w_out.shape[1] - heads_a * val_a) // hd_b
    heads_c, val_c = ret_gn_g.shape[1:]
    key_c = state_ret.shape[4]
    n_experts = moe_w_expert.shape[2]
    n_groups = moe_w_group.shape[2]
    tm = TOKEN_TILE
    n_ctx = batch * seq
    n_ctx_tiles = n_ctx // tm
    tiles_per_req = dec_seq // tm
    assert seq % tm == 0 and dec_seq % tm == 0 and n_ctx % dec_seq == 0
    assert key_a == 128 and val_a == 128 and hd_b == 128 and n_groups + n_experts <= 128

    x = jnp.concatenate([x_prompt.reshape(n_ctx, d), x_sample.reshape(dec_batch * dec_seq, d)], axis=0)
    t = x.shape[0]

    n_vec = 1 + dec_batch
    rows = -(-n_vec // 8) * 8
    cvecs = jnp.zeros((rows, d), F32).at[0].set(c_ctx).at[1:n_vec].set(c)
    mods = _ada_mod(cvecs, mod_w, mod_b).reshape(depth, rows, 6, d)

    lower_bounds = jnp.cumsum(jax.nn.softmax(hgrn_lb_logits.astype(F32), axis=0), axis=0)
    log_gamma = jax.nn.log_sigmoid(ret_decay_logit.astype(F32))

    ang_b = _axial_angles(dec_seq, hd_b)
    cos_b = jnp.concatenate([jnp.cos(ang_b), jnp.cos(ang_b)], axis=-1)
    sin_b = jnp.concatenate([-jnp.sin(ang_b), jnp.sin(ang_b)], axis=-1)
    ang_c = _axial_angles(dec_seq, key_c)
    cos_c, sin_c = jnp.cos(ang_c), jnp.sin(ang_c)

    pad = 128 - n_groups - n_experts
    new_k = new_v = new_sh = new_sr = None
    assert depth == 2, "one HGRN2/GQA layer followed by one retention layer"
    for l in range(depth):
        mod = mods[l]
        if l % 2 == 0:
            e = l // 2
            w_in = even_w_in[e].astype(BF16)
            proj = _in_proj(x, norm1_g[l], mod, w_in, n_ctx_tiles, tiles_per_req, w_in.shape[1])
            lb = lower_bounds[l]
            oa, s_ctx = _hgrn(proj, lb, hgrn_norm_g[e], None, e, None, batch, seq, 0, heads_a, key_a, val_a)
            oa, _ = _hgrn(proj, lb, hgrn_norm_g[e], state_hgrn, e, oa, dec_batch, dec_seq, n_ctx // dec_seq,
                          heads_a, key_a, val_a)
            col0 = (3 * heads_a * key_a + 2 * heads_a * val_a) // 128
            ob, kc, vc = _attn(proj, attn_qn_g[e], attn_kn_g[e], None, e, None, batch, seq, 0,
                               heads_b, kv_b, hd_b, col0)
            (ob,) = _attn(proj, attn_qn_g[e], attn_kn_g[e], (cache_attn_k, cache_attn_v, cos_b, sin_b), e, ob,
                          dec_batch, dec_seq, n_ctx // dec_seq, heads_b, kv_b, hd_b, col0)
            new_k = kc.reshape(batch, 1, seq, kv_b, hd_b)
            new_v = vc.reshape(batch, 1, seq, kv_b, hd_b)
            new_sh = s_ctx[:, None]
            w_out = even_w_out[e].astype(BF16)
            mixes = [oa, ob]
            w_parts = [w_out[:heads_a * val_a], w_out[heads_a * val_a:]]
        else:
            o = l // 2
            w_in = odd_w_in[o].astype(BF16)
            proj = _in_proj(x, norm1_g[l], mod, w_in, n_ctx_tiles, tiles_per_req, w_in.shape[1] // 2)
            y, s_ctx = _retention(proj, log_gamma[o], ret_gn_g[o], None, o, None, batch, seq, 0,
                                  heads_c, key_c, val_c)
            y, _ = _retention(proj, log_gamma[o], ret_gn_g[o], (state_ret, cos_c, sin_c), o, y,
                              dec_batch, dec_seq, n_ctx // dec_seq, heads_c, key_c, val_c)
            new_sr = s_ctx[:, None]
            mixes = [y]
            w_parts = [odd_w_out[o].astype(BF16)]

        w_router = jnp.concatenate([moe_w_group[l], moe_w_expert[l], jnp.zeros((d, pad), F32)], axis=1)
        w_router_hi = w_router.astype(BF16)
        w_router = jnp.stack([w_router_hi, (w_router - w_router_hi.astype(F32)).astype(BF16)])
        b_router = jnp.concatenate([moe_b_group[l], moe_b_expert[l], jnp.zeros((pad,), F32)]).reshape(1, 128)
        xn, h2, eidx, ew = _out_proj(mixes, w_parts, x, mod, norm2_g[l], w_router, b_router,
                                     n_ctx_tiles, tiles_per_req, n_groups, n_experts)
        tile_expert, n_used, ends, pos1, pos2, n_tiles = _dispatch_plan(
            eidx[:, 0], eidx[:, 1], n_experts, EXPERT_TILE)
        src_tok = _row_source(ends, pos1, pos2, n_tiles * EXPERT_TILE, EXPERT_TILE)
        y_sorted = _moe_experts(h2, src_tok, tile_expert, n_used, moe_w_gate, moe_w_up, moe_w_down, l, EXPERT_TILE)
        x = _moe_combine(y_sorted, pos1, pos2, xn, ew, mod, final_norm_g, n_ctx_tiles, tiles_per_req,
                         final_norm=(l == depth - 1))

    y_prompt = x[:n_ctx].reshape(batch, seq, d)
    y_sample = x[n_ctx:].reshape(dec_batch, dec_seq, d)
    return (y_prompt, y_sample, new_k, new_v, new_sh, new_sr)
```

```python
import functools

import numpy as np
import jax
import jax.numpy as jnp
from jax import lax
from jax.experimental import pallas as pl
from jax.experimental.pallas import tpu as pltpu

F32 = jnp.float32
BF16 = jnp.bfloat16
EPS = 1e-6
ROPE_BASE = 10000.0
GRID_W = 64
N_GROUPS = 4
TOP_K_INNER = 2
HGRN_BLOCK = 128
HGRN_DIAG = 8
RET_CHUNK = 256
TOKEN_TILE = 256
EXPERT_TILE = 256
MOE_GATHER_AHEAD = 2
V7X_VMEM_LIMIT_BYTES = 56 * 1024 * 1024
HIGHEST = lax.Precision.HIGHEST
LOG2_E = 1.4426950408889634

_NT = (((1,), (1,)), ((), ()))


def _cparams(*sem):
    return pltpu.CompilerParams(dimension_semantics=sem, vmem_limit_bytes=V7X_VMEM_LIMIT_BYTES)


def _rms(x, g):
    return x * lax.rsqrt(jnp.mean(x * x, axis=-1, keepdims=True) + EPS) * g


def _silu(x):
    return x * jax.nn.sigmoid(x)


def _bdot(a, b):
    return jnp.dot(a.astype(BF16), b.astype(BF16), preferred_element_type=F32)


def _bdot_nt(a, b):
    return lax.dot_general(a.astype(BF16), b.astype(BF16), _NT, preferred_element_type=F32)


def _mod_kernel(c_ref, w_ref, b_ref, o_ref):
    a = _silu(c_ref[...])
    o_ref[0] = jnp.dot(a, w_ref[0], precision=HIGHEST, preferred_element_type=F32) + b_ref[0]


def _ada_mod(cvecs, mod_w, mod_b):
    depth, d, n = mod_w.shape
    tn = n // 4
    rows = cvecs.shape[0]
    return pl.pallas_call(
        _mod_kernel,
        grid=(depth, n // tn),
        in_specs=[pl.BlockSpec((rows, d), lambda l, j: (0, 0)),
                  pl.BlockSpec((1, d, tn), lambda l, j: (l, 0, j)),
                  pl.BlockSpec((1, 1, tn), lambda l, j: (l, 0, j))],
        out_specs=pl.BlockSpec((1, rows, tn), lambda l, j: (l, 0, j)),
        out_shape=jax.ShapeDtypeStruct((depth, rows, n), F32),
        compiler_params=_cparams("arbitrary", "arbitrary"),
        name="ada_mod",
    )(cvecs, mod_w, mod_b.reshape(depth, 1, n))


def _tile_mod_row(i, n_ctx_tiles, tiles_per_req):
    return jnp.where(i < n_ctx_tiles, 0, 1 + jnp.maximum(i - n_ctx_tiles, 0) // tiles_per_req)


def _inproj_kernel(x_ref, g_ref, mod_ref, w_ref, o_ref):
    h = _rms(x_ref[...], g_ref[...]) * (1.0 + mod_ref[0, 1:2, :]) + mod_ref[0, 0:1, :]
    o_ref[...] = jnp.dot(h.astype(BF16), w_ref[...], preferred_element_type=F32)


def _in_proj(x, g, mod, w_bf, n_ctx_tiles, tiles_per_req, tn):
    t, d = x.shape
    n = w_bf.shape[1]
    tm = TOKEN_TILE
    mrow = functools.partial(_tile_mod_row, n_ctx_tiles=n_ctx_tiles, tiles_per_req=tiles_per_req)
    return pl.pallas_call(
        _inproj_kernel,
        grid=(n // tn, t // tm),
        in_specs=[pl.BlockSpec((tm, d), lambda j, i: (i, 0)),
                  pl.BlockSpec((1, d), lambda j, i: (0, 0)),
                  pl.BlockSpec((1, 6, d), lambda j, i: (mrow(i), 0, 0)),
                  pl.BlockSpec((d, tn), lambda j, i: (0, j))],
        out_specs=pl.BlockSpec((tm, tn), lambda j, i: (i, j)),
        out_shape=jax.ShapeDtypeStruct((t, n), F32),
        compiler_params=_cparams("arbitrary", "arbitrary"),
        name="in_proj",
    )(x, g.reshape(1, d), mod, w_bf)


def _hgrn_decay(z, lb, reverse):
    nb = HGRN_BLOCK
    row = lax.broadcasted_iota(jnp.int32, (nb, nb), 0)
    col = lax.broadcasted_iota(jnp.int32, (nb, nb), 1)
    tri = jnp.where((col >= row) if reverse else (col <= row), 1.0, 0.0).astype(BF16)
    logf = jnp.log(lb + (1.0 - lb) * jax.nn.sigmoid(z))
    k = (1.0 - lb) * jax.nn.sigmoid(-z)
    l2 = logf * LOG2_E
    hi = l2.astype(BF16)
    lo = (l2 - hi.astype(F32)).astype(BF16)
    c = jnp.dot(tri, hi, preferred_element_type=F32) + jnp.dot(tri, lo, preferred_element_type=F32)
    return c, k


def _hgrn_block(q, v, st_t, reverse, c_ref, k_ref, ecat):
    nb = HGRN_BLOCK
    row = lax.broadcasted_iota(jnp.int32, (nb, nb), 0)
    col = lax.broadcasted_iota(jnp.int32, (nb, nb), 1)
    c = c_ref[...]
    k = k_ref[...]
    edge = 0 if reverse else nb - 1
    c_edge = c_ref[edge:edge + 1, :]

    o = _bdot_nt(q * jnp.exp2(c), st_t)
    kd = k * jnp.exp2(c_edge - c)
    st_new = st_t * jnp.exp2(c_edge) + _bdot(v.T, kd)

    a = jnp.zeros((nb, nb), F32)
    m = nb // 2
    while m >= HGRN_DIAG:
        pieces = []
        for grp in range(nb // (2 * m)):
            idx = grp * 2 * m + (m if reverse else m - 1)
            pieces.append(jnp.broadcast_to(c_ref[idx:idx + 1, :], (2 * m, nb)))
        bnd = pieces[0] if len(pieces) == 1 else jnp.concatenate(pieces, axis=0)
        e = jnp.exp2(-jnp.abs(c - bnd))
        am = _bdot_nt(q * e, k * e)
        same = (row & ~(2 * m - 1)) == (col & ~(2 * m - 1))
        r_hi = (row & m) != 0
        c_hi = (col & m) != 0
        if reverse:
            mask = same & jnp.logical_not(r_hi) & c_hi
        else:
            mask = same & r_hi & jnp.logical_not(c_hi)
        a = jnp.where(mask, am, a)
        m //= 2

    dg = HGRN_DIAG
    rowi = lax.broadcasted_iota(jnp.int32, (dg, nb), 0)
    cols = []
    for s in range(dg):
        keep = (rowi <= s) if reverse else (rowi >= s)
        tiles = []
        for j in range(nb // dg):
            r = j * dg + s
            qj = q[j * dg:(j + 1) * dg, :]
            cj = c[j * dg:(j + 1) * dg, :]
            cs = jnp.broadcast_to(c_ref[r:r + 1, :], (dg, nb))
            ks = jnp.broadcast_to(k_ref[r:r + 1, :], (dg, nb))
            tiles.append(qj * ks * jnp.exp2(jnp.where(keep, cj - cs, -jnp.inf)))
        cols.append(jnp.concatenate(tiles, axis=0).astype(BF16))
    p = jnp.concatenate(cols, axis=1)
    red = jnp.dot(p, ecat, preferred_element_type=F32)
    a = jnp.where((row >> 3) == (col >> 3), red, a)
    o = o + _bdot(a, v)
    return o, st_new


def _hgrn_kernel(*refs, nblk, has_state, qscale):
    qa_ref, ia_ref, zf_ref, zb_ref, ga_ref, lb_ref, ng_ref, ecat_ref = refs[:8]
    s0_ref = refs[8] if has_state else None
    o_ref, sfin_ref, oacc_ref, c_scr, k_scr = refs[-5:]
    nb = HGRN_BLOCK

    def gates(direction, blk, slot):
        z_ref = zb_ref if direction == 1 else zf_ref
        r0 = pl.multiple_of(blk * nb, nb)
        c, k = _hgrn_decay(z_ref[pl.ds(r0, nb), :], lb_ref[direction:direction + 1, :], direction == 1)
        c_scr[direction, slot] = c
        k_scr[direction, slot] = k

    def block(direction, blk, slot, st_t):
        r0 = pl.multiple_of(blk * nb, nb)
        q = _silu(qa_ref[pl.ds(r0, nb), :]) * qscale
        v = ia_ref[pl.ds(r0, nb), :]
        o, st_new = _hgrn_block(q, v, st_t, direction == 1, c_scr.at[direction, slot],
                                k_scr.at[direction, slot], ecat_ref[...])
        oacc_ref[direction, pl.ds(r0, nb), :] = o
        return st_new

    gates(0, 0, 0)
    gates(1, nblk - 1, 0)

    def body(i, carry):
        st_f, st_b = carry
        slot = lax.rem(i, 2)
        st_f = block(0, i, slot, st_f)
        st_b = block(1, nblk - 1 - i, slot, st_b)
        nxt = jnp.minimum(i + 1, nblk - 1)
        gates(0, nxt, 1 - slot)
        gates(1, nblk - 1 - nxt, 1 - slot)
        return st_f, st_b

    if has_state:
        st0 = (s0_ref[0].T, s0_ref[1].T)
    else:
        zero = jnp.zeros((ia_ref.shape[1], qa_ref.shape[1]), F32)
        st0 = (zero, zero)
    st_f, st_b = lax.fori_loop(0, nblk, body, st0)
    sfin_ref[0] = st_f.T
    sfin_ref[1] = st_b.T

    o = oacc_ref[0] + oacc_ref[1]
    o_ref[...] = _rms(o, ng_ref[0]) * _silu(ga_ref[...])


def _hgrn(proj, lb, norm_g, s0, layer_idx, out_prev, n_seq, seq, row_blk0, heads, key_d, val_d):
    t = proj.shape[0]
    has_state = s0 is not None
    hk = heads * key_d // 128
    off_qa, off_ia, off_zf, off_zb, off_ga = 0, hk, 2 * hk, 3 * hk, 4 * hk

    def col(off):
        return pl.BlockSpec((seq, 128), lambda b, h: (row_blk0 + b, off + h))

    lane = np.arange(key_d)[None, :]
    srow = np.repeat(np.arange(HGRN_DIAG), key_d)[:, None]
    ecat = jnp.asarray((lane % HGRN_DIAG == srow).astype(np.float32), dtype=BF16)
    in_specs = [col(off_qa), col(off_ia), col(off_zf), col(off_zb), col(off_ga),
                pl.BlockSpec((2, key_d), lambda b, h: (0, h)),
                pl.BlockSpec((1, 1, val_d), lambda b, h: (h, 0, 0)),
                pl.BlockSpec((HGRN_DIAG * key_d, key_d), lambda b, h: (0, 0))]
    args = [proj, proj, proj, proj, proj, lb, norm_g.reshape(heads, 1, val_d), ecat]
    if has_state:
        in_specs.append(pl.BlockSpec((None, None, 2, None, key_d, val_d),
                                     lambda b, h: (b, layer_idx, 0, h, 0, 0)))
        args.append(s0)
    aliases = {}
    if out_prev is not None:
        in_specs.append(pl.BlockSpec(memory_space=pl.ANY))
        args.append(out_prev)
        aliases = {len(args) - 1: 0}

    kern = functools.partial(_hgrn_kernel, nblk=seq // HGRN_BLOCK, has_state=has_state,
                             qscale=float(key_d) ** -0.5)
    return pl.pallas_call(
        kern,
        grid=(n_seq, heads),
        in_specs=in_specs,
        out_specs=[pl.BlockSpec((seq, val_d), lambda b, h: (row_blk0 + b, h)),
                   pl.BlockSpec((None, 2, None, key_d, val_d), lambda b, h: (b, 0, h, 0, 0))],
        out_shape=[jax.ShapeDtypeStruct((t, heads * val_d), F32),
                   jax.ShapeDtypeStruct((n_seq, 2, heads, key_d, val_d), F32)],
        scratch_shapes=[pltpu.VMEM((2, seq, val_d), F32),
                        pltpu.VMEM((2, 2, HGRN_BLOCK, key_d), F32),
                        pltpu.VMEM((2, 2, HGRN_BLOCK, key_d), F32)],
        input_output_aliases=aliases,
        compiler_params=_cparams("arbitrary", "arbitrary"),
        name="hgrn2",
    )(*args)


def _rope(x, cos, sin_signed):
    return x * cos + pltpu.roll(x, x.shape[-1] // 2, 1) * sin_signed


def _attn_kernel(*refs, has_ctx, grp, qblk, scale):
    if has_ctx:
        (q_ref, k_ref, v_ref, qn_ref, kn_ref, ck_ref, cv_ref, cos_ref, sin_ref, _prev, o_ref) = refs
    else:
        (q_ref, k_ref, v_ref, qn_ref, kn_ref, o_ref, ko_ref, vo_ref) = refs
    seq, hd = k_ref.shape
    k = _rms(k_ref[...], kn_ref[...])
    v = v_ref[...]
    if has_ctx:
        k = _rope(k, cos_ref[...], sin_ref[...])
        ck = ck_ref[...].astype(BF16)
        cv = cv_ref[...].astype(BF16)
    else:
        ko_ref[...] = k
        vo_ref[...] = v
    kb = k.astype(BF16)
    vb = v.astype(BF16)

    for g in range(grp):
        def body(i, carry, g=g):
            r0 = pl.multiple_of(i * qblk, qblk)
            q = _rms(q_ref[pl.ds(r0, qblk), g * hd:(g + 1) * hd], qn_ref[...])
            if has_ctx:
                q = _rope(q, cos_ref[pl.ds(r0, qblk), :], sin_ref[pl.ds(r0, qblk), :])
            qb = q.astype(BF16)
            s2 = lax.dot_general(qb, kb, _NT, preferred_element_type=F32) * scale
            m = jnp.max(s2, axis=-1, keepdims=True)
            if has_ctx:
                s1 = lax.dot_general(qb, ck, _NT, preferred_element_type=F32) * scale
                m = jnp.maximum(m, jnp.max(s1, axis=-1, keepdims=True))
                p1 = jnp.exp(s1 - m)
            p2 = jnp.exp(s2 - m)
            den = jnp.sum(p2, axis=-1, keepdims=True)
            acc = jnp.dot(p2.astype(BF16), vb, preferred_element_type=F32)
            if has_ctx:
                den = den + jnp.sum(p1, axis=-1, keepdims=True)
                acc = acc + jnp.dot(p1.astype(BF16), cv, preferred_element_type=F32)
            o_ref[pl.ds(r0, qblk), g * hd:(g + 1) * hd] = acc / den
            return carry

        lax.fori_loop(0, seq // qblk, body, 0)


def _attn(proj, qn_g, kn_g, ctx, layer_idx, out_prev, n_seq, seq, row_blk0, heads, kv_heads, hd, col0):
    t = proj.shape[0]
    grp = heads // kv_heads
    has_ctx = ctx is not None
    cq = col0 // grp
    ck0 = col0 + heads
    cv0 = ck0 + kv_heads
    in_specs = [pl.BlockSpec((seq, grp * hd), lambda b, h: (row_blk0 + b, cq + h)),
                pl.BlockSpec((seq, hd), lambda b, h: (row_blk0 + b, ck0 + h)),
                pl.BlockSpec((seq, hd), lambda b, h: (row_blk0 + b, cv0 + h)),
                pl.BlockSpec((1, hd), lambda b, h: (0, 0)),
                pl.BlockSpec((1, hd), lambda b, h: (0, 0))]
    args = [proj, proj, proj, qn_g.reshape(1, hd), kn_g.reshape(1, hd)]
    out_specs = [pl.BlockSpec((seq, grp * hd), lambda b, h: (row_blk0 + b, h))]
    out_shape = [jax.ShapeDtypeStruct((t, heads * hd), F32)]
    aliases = {}
    if has_ctx:
        cache_k, cache_v, cos, sin_signed = ctx
        nreq, nlay, past = cache_k.shape[:3]
        cache_k = cache_k.reshape(nreq, nlay, past, kv_heads * hd)
        cache_v = cache_v.reshape(nreq, nlay, past, kv_heads * hd)
        spec_c = pl.BlockSpec((None, None, past, hd), lambda b, h: (b, layer_idx, 0, h))
        in_specs += [spec_c, spec_c,
                     pl.BlockSpec((seq, hd), lambda b, h: (0, 0)),
                     pl.BlockSpec((seq, hd), lambda b, h: (0, 0)),
                     pl.BlockSpec(memory_space=pl.ANY)]
        args += [cache_k, cache_v, cos, sin_signed, out_prev]
        aliases = {len(args) - 1: 0}
    else:
        out_specs += [pl.BlockSpec((seq, hd), lambda b, h: (b, h)),
                      pl.BlockSpec((seq, hd), lambda b, h: (b, h))]
        out_shape += [jax.ShapeDtypeStruct((n_seq * seq, kv_heads * hd), F32)] * 2
    return pl.pallas_call(
        functools.partial(_attn_kernel, has_ctx=has_ctx, grp=grp, qblk=min(seq, 256), scale=float(hd) ** -0.5),
        grid=(n_seq, kv_heads),
        in_specs=in_specs,
        out_specs=out_specs,
        out_shape=out_shape,
        input_output_aliases=aliases,
        compiler_params=_cparams("arbitrary", "arbitrary"),
        name="gqa",
    )(*args)


def _ret_kernel(*refs, nchunk, has_ctx, kscale):
    if has_ctx:
        (lg_ref, q_ref, k_ref, v_ref, g_ref, gn_ref, s0_ref, cos_ref, sin_ref, _prev,
         o_ref, sfin_ref, sb_ref) = refs
    else:
        (lg_ref, q_ref, k_ref, v_ref, g_ref, gn_ref, o_ref, sfin_ref, sb_ref) = refs
    cs = RET_CHUNK
    kd_dim = q_ref.shape[1]
    h = pl.program_id(1)
    lgf = lg_ref[0, h]
    lgb = lg_ref[1, h]
    row = lax.broadcasted_iota(jnp.int32, (cs, cs), 0)
    col = lax.broadcasted_iota(jnp.int32, (cs, cs), 1)
    rel = (row - col).astype(F32)
    dmat = (jnp.exp(jnp.where(rel >= 0, rel * lgf, -jnp.inf))
            + jnp.exp(jnp.where(rel <= 0, -rel * lgb, -jnp.inf)))
    j = lax.broadcasted_iota(jnp.int32, (cs, 1), 0).astype(F32)
    one = jnp.ones((1, 1), F32)
    dec_f = jnp.exp(one * (cs * lgf))
    dec_b = jnp.exp(one * (cs * lgb))

    def load_qk(ref, r0, mul):
        x = ref[pl.ds(r0, cs), :] * mul
        if has_ctx:
            half = kd_dim // 2
            x1, x2 = x[:, :half], x[:, half:]
            cos = cos_ref[pl.ds(r0, cs), :]
            sin = sin_ref[pl.ds(r0, cs), :]
            x = jnp.concatenate([x1 * cos - x2 * sin, x2 * cos + x1 * sin], axis=1)
        return x

    def bwd_body(i, st):
        n = nchunk - 1 - i
        r0 = pl.multiple_of(n * cs, cs)
        sb_ref[n] = st
        k = load_qk(k_ref, r0, kscale)
        v = v_ref[pl.ds(r0, cs), :]
        return st * dec_b + _bdot((k * jnp.exp(j * lgb)).T, v)

    st_b0 = s0_ref[1] if has_ctx else jnp.zeros(sfin_ref.shape[1:], F32)
    st_b = lax.fori_loop(0, nchunk, bwd_body, st_b0)
    sfin_ref[1] = st_b

    def fwd_body(n, st):
        r0 = pl.multiple_of(n * cs, cs)
        q = load_qk(q_ref, r0, 1.0)
        k = load_qk(k_ref, r0, kscale)
        v = v_ref[pl.ds(r0, cs), :]
        a = _bdot_nt(q, k) * dmat
        o = _bdot(a, v)
        o = o + _bdot(q * jnp.exp((j + 1.0) * lgf), st)
        o = o + _bdot(q * jnp.exp((cs - j) * lgb), sb_ref[n])
        mu = jnp.mean(o, axis=-1, keepdims=True)
        var = jnp.mean(jnp.square(o - mu), axis=-1, keepdims=True)
        y = (o - mu) * lax.rsqrt(var + EPS) * gn_ref[0]
        o_ref[pl.ds(r0, cs), :] = _silu(g_ref[pl.ds(r0, cs), :]) * y
        return st * dec_f + _bdot((k * jnp.exp((cs - 1.0 - j) * lgf)).T, v)

    st_f0 = s0_ref[0] if has_ctx else jnp.zeros(sfin_ref.shape[1:], F32)
    sfin_ref[0] = lax.fori_loop(0, nchunk, fwd_body, st_f0)


def _retention(proj, log_gamma, gn_g, ctx, layer_idx, out_prev, n_seq, seq, row_blk0, heads, key_d, val_d):
    t = proj.shape[0]
    has_ctx = ctx is not None
    nq = heads * key_d
    cq, ck, cv, cg = 0, nq // key_d, 2 * nq // val_d, (2 * nq + heads * val_d) // val_d
    in_specs = [pl.BlockSpec(memory_space=pltpu.SMEM),
                pl.BlockSpec((seq, key_d), lambda b, h: (row_blk0 + b, cq + h)),
                pl.BlockSpec((seq, key_d), lambda b, h: (row_blk0 + b, ck + h)),
                pl.BlockSpec((seq, val_d), lambda b, h: (row_blk0 + b, cv + h)),
                pl.BlockSpec((seq, val_d), lambda b, h: (row_blk0 + b, cg + h)),
                pl.BlockSpec((1, 1, val_d), lambda b, h: (h, 0, 0))]
    args = [log_gamma, proj, proj, proj, proj, gn_g.reshape(heads, 1, val_d)]
    aliases = {}
    if has_ctx:
        s0, cos, sin = ctx
        in_specs += [pl.BlockSpec((None, None, 2, None, key_d, val_d), lambda b, h: (b, layer_idx, 0, h, 0, 0)),
                     pl.BlockSpec((seq, key_d // 2), lambda b, h: (0, 0)),
                     pl.BlockSpec((seq, key_d // 2), lambda b, h: (0, 0)),
                     pl.BlockSpec(memory_space=pl.ANY)]
        args += [s0, cos, sin, out_prev]
        aliases = {len(args) - 1: 0}
    nchunk = seq // RET_CHUNK
    return pl.pallas_call(
        functools.partial(_ret_kernel, nchunk=nchunk, has_ctx=has_ctx, kscale=float(key_d) ** -0.5),
        grid=(n_seq, heads),
        in_specs=in_specs,
        out_specs=[pl.BlockSpec((seq, val_d), lambda b, h: (row_blk0 + b, h)),
                   pl.BlockSpec((None, 2, None, key_d, val_d), lambda b, h: (b, 0, h, 0, 0))],
        out_shape=[jax.ShapeDtypeStruct((t, heads * val_d), F32),
                   jax.ShapeDtypeStruct((n_seq, 2, heads, key_d, val_d), F32)],
        scratch_shapes=[pltpu.VMEM((nchunk, key_d, val_d), F32)],
        input_output_aliases=aliases,
        compiler_params=_cparams("arbitrary", "arbitrary"),
        name="retention",
    )(*args)


def _outproj_kernel(*refs, n_in, n_groups, n_experts):
    mix_refs = refs[:n_in]
    w_refs = refs[n_in:2 * n_in]
    x_ref, mod_ref, g2_ref, wr_ref, br_ref, xn_ref, h2_ref, eidx_ref, ew_ref = refs[2 * n_in:]
    acc = _bdot(mix_refs[0][...], w_refs[0][...])
    for mref, wref in zip(mix_refs[1:], w_refs[1:]):
        acc = acc + _bdot(mref[...], wref[...])
    xn = x_ref[...] + mod_ref[0, 2:3, :] * acc
    xn_ref[...] = xn
    h2 = _rms(xn, g2_ref[...]) * (1.0 + mod_ref[0, 4:5, :]) + mod_ref[0, 3:4, :]
    h2_ref[...] = h2.reshape(h2_ref.shape)
    h_hi = h2.astype(BF16)
    h_lo = (h2 - h_hi.astype(F32)).astype(BF16)
    lg = (jnp.dot(h_hi, wr_ref[0], preferred_element_type=F32)
          + jnp.dot(h_hi, wr_ref[1], preferred_element_type=F32)
          + jnp.dot(h_lo, wr_ref[0], preferred_element_type=F32)) + br_ref[...]

    per_grp = n_experts // n_groups
    lane = lax.broadcasted_iota(jnp.int32, lg.shape, 1).astype(F32)
    big = float(lg.shape[1])
    ninf = -jnp.inf
    gl = jnp.where(lane < n_groups, lg, ninf)
    gmax = jnp.max(gl, axis=-1, keepdims=True)
    p_grp = 1.0 / jnp.sum(jnp.exp(gl - gmax), axis=-1, keepdims=True)
    gidx = jnp.min(jnp.where(gl == gmax, lane, big), axis=-1, keepdims=True)
    in_grp = ((lane >= n_groups) & (lane < n_groups + n_experts)
              & (jnp.floor((lane - n_groups) / per_grp) == gidx))
    el = jnp.where(in_grp, lg, ninf)
    emax = jnp.max(el, axis=-1, keepdims=True)
    esum = jnp.sum(jnp.exp(el - emax), axis=-1, keepdims=True)
    i1 = jnp.min(jnp.where(el == emax, lane, big), axis=-1, keepdims=True)
    el2 = jnp.where(lane == i1, ninf, el)
    emax2 = jnp.max(el2, axis=-1, keepdims=True)
    i2 = jnp.min(jnp.where(el2 == emax2, lane, big), axis=-1, keepdims=True)
    p1 = 1.0 / esum
    p2 = jnp.exp(emax2 - emax) / esum
    w1 = p1 / (p1 + p2) * p_grp
    w2 = p2 / (p1 + p2) * p_grp
    eidx_ref[...] = jnp.where(lane == 0.0, i1 - n_groups, jnp.where(lane == 1.0, i2 - n_groups, 0.0)).astype(jnp.int32)
    ew_ref[...] = jnp.where(lane == 0.0, w1, jnp.where(lane == 1.0, w2, 0.0))


def _out_proj(mixes, w_parts, x, mod, g2, w_router, b_router, n_ctx_tiles, tiles_per_req, n_groups, n_experts):
    t, d = x.shape
    tm = TOKEN_TILE
    n_in = len(mixes)
    mrow = functools.partial(_tile_mod_row, n_ctx_tiles=n_ctx_tiles, tiles_per_req=tiles_per_req)
    in_specs = ([pl.BlockSpec((tm, m.shape[1]), lambda i: (i, 0)) for m in mixes]
                + [pl.BlockSpec(w.shape, lambda i: (0, 0)) for w in w_parts]
                + [pl.BlockSpec((tm, d), lambda i: (i, 0)),
                   pl.BlockSpec((1, 6, d), lambda i: (mrow(i), 0, 0)),
                   pl.BlockSpec((1, d), lambda i: (0, 0)),
                   pl.BlockSpec(w_router.shape, lambda i: (0, 0, 0)),
                   pl.BlockSpec((1, 128), lambda i: (0, 0))])
    return pl.pallas_call(
        functools.partial(_outproj_kernel, n_in=n_in, n_groups=n_groups, n_experts=n_experts),
        grid=(t // tm,),
        in_specs=in_specs,
        out_specs=[pl.BlockSpec((tm, d), lambda i: (i, 0)),
                   pl.BlockSpec((tm, d // 128, 128), lambda i: (i, 0, 0)),
                   pl.BlockSpec((tm, 128), lambda i: (i, 0)),
                   pl.BlockSpec((tm, 128), lambda i: (i, 0))],
        out_shape=[jax.ShapeDtypeStruct((t, d), F32),
                   jax.ShapeDtypeStruct((t, d // 128, 128), F32),
                   jax.ShapeDtypeStruct((t, 128), jnp.int32),
                   jax.ShapeDtypeStruct((t, 128), F32)],
        compiler_params=_cparams("arbitrary"),
        name="out_proj_router",
    )(*mixes, *w_parts, x, mod, g2.reshape(1, d), w_router, b_router)


def _dispatch_plan(e1, e2, n_experts, tile):
    t = e1.shape[0]
    e = jnp.concatenate([e1, e2])
    onehot = (e[:, None] == jnp.arange(n_experts, dtype=jnp.int32)[None, :]).astype(jnp.int32)
    csum = jnp.cumsum(onehot, axis=0)
    rank = jnp.take_along_axis(csum, e[:, None], axis=1)[:, 0] - 1
    counts = csum[-1]
    padded = ((counts + tile - 1) // tile) * tile
    ends = jnp.cumsum(padded)
    starts = ends - padded
    pos = starts[e] + rank
    n_tiles = (2 * t) // tile + n_experts
    n_used = (ends[-1] // tile).astype(jnp.int32)
    tile_start = jnp.arange(n_tiles, dtype=jnp.int32) * tile
    probe = jnp.minimum(tile_start, ends[-1] - 1)
    tile_expert = jnp.sum((ends[None, :] <= probe[:, None]).astype(jnp.int32), axis=1)
    return tile_expert, n_used.reshape(1), ends.astype(jnp.int32), pos[:t], pos[t:], n_tiles


def _row_source_kernel(ends_ref, p1_ref, p2_ref, o_ref, *, n_tok, n_rows, tile, n_experts):
    def clear8(j, c):
        for u in range(8):
            o_ref[j * 8 + u] = 0
        return c

    def per_expert(e, c):
        first = lax.shift_right_logical(jnp.maximum(ends_ref[e] - tile, 0), 3)
        lax.fori_loop(first, first + tile // 8, clear8, 0)
        return c

    lax.fori_loop(0, n_experts, per_expert, 0)
    lax.fori_loop(lax.shift_right_logical(ends_ref[n_experts - 1], 3), n_rows // 8, clear8, 0)

    def place(t, c):
        o_ref[p1_ref[t]] = t
        o_ref[p2_ref[t]] = t
        return c

    lax.fori_loop(0, n_tok, place, 0, unroll=8)


def _row_source(ends, pos1, pos2, n_rows, tile):
    n_tok = pos1.shape[0]
    smem = pl.BlockSpec(memory_space=pltpu.SMEM)
    return pl.pallas_call(
        functools.partial(_row_source_kernel, n_tok=n_tok, n_rows=n_rows, tile=tile, n_experts=ends.shape[0]),
        in_specs=[smem, smem, smem],
        out_specs=smem,
        out_shape=jax.ShapeDtypeStruct((n_rows,), jnp.int32),
        name="moe_row_source",
    )(ends, pos1, pos2)


def _moe_kernel(te_ref, nu_ref, tok_ref, h_hbm, wg_ref, wu_ref, wd_ref, o_ref,
                wg_bf, wu_bf, wd_bf, buf, sem, *, tm, ahead):
    i = pl.program_id(0)
    nslot = ahead + 1
    slot = lax.rem(i, nslot)
    n_used = nu_ref[0]

    def issue(tile):
        s = lax.rem(tile, nslot)
        base = tile * tm

        def body(r, c):
            pltpu.make_async_copy(h_hbm.at[tok_ref[base + r]], buf.at[s, r], sem.at[s]).start(priority=1)
            return c

        lax.fori_loop(0, tm, body, 0, unroll=8)

    @pl.when(i == 0)
    def _():
        for j in range(ahead):
            @pl.when(j < n_used)
            def _():
                issue(jnp.int32(j))

    @pl.when(i + ahead < n_used)
    def _():
        issue(i + ahead)

    new_expert = jnp.logical_or(i == 0, te_ref[i] != te_ref[jnp.maximum(i - 1, 0)])

    @pl.when(new_expert)
    def _():
        wg_bf[...] = wg_ref[...].astype(BF16)
        wu_bf[...] = wu_ref[...].astype(BF16)
        wd_bf[...] = wd_ref[...].astype(BF16)

    @pl.when(i < n_used)
    def _():
        pltpu.make_async_copy(h_hbm.at[pl.ds(0, tm)], buf.at[slot], sem.at[slot]).wait()
        x = buf[slot].reshape(tm, -1).astype(BF16)
        g = jnp.dot(x, wg_bf[...], preferred_element_type=F32)
        u = jnp.dot(x, wu_bf[...], preferred_element_type=F32)
        act = _silu(g) * u
        y = jnp.dot(act.astype(BF16), wd_bf[...], preferred_element_type=F32)
        o_ref[...] = y.reshape(o_ref.shape)

    @pl.when(i >= n_used)
    def _():
        o_ref[...] = jnp.zeros_like(o_ref)


def _moe_experts(h3, src_tok, tile_expert, n_used, w_gate, w_up, w_down, layer_idx, tm):
    t, sub, lanes = h3.shape
    d = sub * lanes
    n_tiles = src_tok.shape[0] // tm
    f = w_gate.shape[-1]
    ahead = MOE_GATHER_AHEAD
    return pl.pallas_call(
        functools.partial(_moe_kernel, tm=tm, ahead=ahead),
        grid_spec=pltpu.PrefetchScalarGridSpec(
            num_scalar_prefetch=3,
            grid=(n_tiles,),
            in_specs=[pl.BlockSpec(memory_space=pl.ANY),
                      pl.BlockSpec((None, None, d, f), lambda i, te, nu, tok: (layer_idx, te[i], 0, 0)),
                      pl.BlockSpec((None, None, d, f), lambda i, te, nu, tok: (layer_idx, te[i], 0, 0)),
                      pl.BlockSpec((None, None, f, d), lambda i, te, nu, tok: (layer_idx, te[i], 0, 0))],
            out_specs=pl.BlockSpec((tm, sub, lanes), lambda i, te, nu, tok: (i, 0, 0)),
            scratch_shapes=[pltpu.VMEM((d, f), BF16), pltpu.VMEM((d, f), BF16), pltpu.VMEM((f, d), BF16),
                            pltpu.VMEM((ahead + 1, tm, sub, lanes), F32),
                            pltpu.SemaphoreType.DMA((ahead + 1,))]),
        out_shape=jax.ShapeDtypeStruct((n_tiles * tm, sub, lanes), F32),
        compiler_params=_cparams("arbitrary"),
        name="moe_experts",
    )(tile_expert, n_used, src_tok, h3, w_gate, w_up, w_down)


def _combine_kernel(p1_ref, p2_ref, p1n_ref, p2n_ref, y_hbm, x_ref, ew_ref, mod_ref, gf_ref, o_ref, buf, sem,
                    *, tm, n_steps, final_norm):
    i = pl.program_id(0)
    slot = lax.rem(i, 2)

    def issue(s, pa_ref, pb_ref):
        def body(r, c):
            pltpu.make_async_copy(y_hbm.at[pa_ref[0, 0, r]], buf.at[s, 0, r], sem.at[s, 0]).start()
            pltpu.make_async_copy(y_hbm.at[pb_ref[0, 0, r]], buf.at[s, 1, r], sem.at[s, 1]).start()
            return c

        lax.fori_loop(0, tm, body, 0, unroll=8)

    @pl.when(i == 0)
    def _():
        issue(0, p1_ref, p2_ref)

    @pl.when(i + 1 < n_steps)
    def _():
        issue(1 - slot, p1n_ref, p2n_ref)

    for k in range(2):
        pltpu.make_async_copy(y_hbm.at[pl.ds(0, tm)], buf.at[slot, k], sem.at[slot, k]).wait()
    ew = ew_ref[...]
    y = ew[:, 0:1] * buf[slot, 0].reshape(tm, -1) + ew[:, 1:2] * buf[slot, 1].reshape(tm, -1)
    x = x_ref[...] + mod_ref[0, 5:6, :] * y
    if final_norm:
        x = _rms(x, gf_ref[...])
    o_ref[...] = x


def _moe_combine(y_sorted, pos1, pos2, x, ew, mod, final_g, n_ctx_tiles, tiles_per_req, final_norm):
    t, d = x.shape
    tm = TOKEN_TILE
    n_steps = t // tm
    mrow = functools.partial(_tile_mod_row, n_ctx_tiles=n_ctx_tiles, tiles_per_req=tiles_per_req)
    smem_rows = pl.BlockSpec((1, 1, tm), lambda i: (i, 0, 0), memory_space=pltpu.SMEM)
    smem_next = pl.BlockSpec((1, 1, tm), lambda i: (jnp.minimum(i + 1, n_steps - 1), 0, 0),
                             memory_space=pltpu.SMEM)
    p1 = pos1.reshape(n_steps, 1, tm)
    p2 = pos2.reshape(n_steps, 1, tm)
    return pl.pallas_call(
        functools.partial(_combine_kernel, tm=tm, n_steps=n_steps, final_norm=final_norm),
        grid=(n_steps,),
        in_specs=[smem_rows, smem_rows, smem_next, smem_next,
                  pl.BlockSpec(memory_space=pl.ANY),
                  pl.BlockSpec((tm, d), lambda i: (i, 0)),
                  pl.BlockSpec((tm, 128), lambda i: (i, 0)),
                  pl.BlockSpec((1, 6, d), lambda i: (mrow(i), 0, 0)),
                  pl.BlockSpec((1, d), lambda i: (0, 0))],
        out_specs=pl.BlockSpec((tm, d), lambda i: (i, 0)),
        out_shape=jax.ShapeDtypeStruct((t, d), F32),
        scratch_shapes=[pltpu.VMEM((2, 2, tm) + y_sorted.shape[1:], F32), pltpu.SemaphoreType.DMA((2, 2))],
        compiler_params=_cparams("arbitrary"),
        name="moe_combine",
    )(p1, p2, p1, p2, y_sorted, x, ew, mod, final_g.reshape(1, d))


def _axial_angles(seq, head_dim):
    rows = seq // GRID_W
    row = jnp.repeat(jnp.arange(rows), GRID_W).astype(F32)
    col = jnp.tile(jnp.arange(GRID_W), rows).astype(F32)
    n_freq = head_dim // 4
    inv = ROPE_BASE ** (-jnp.arange(n_freq, dtype=F32) / n_freq)
    return jnp.concatenate([row[:, None] * inv, col[:, None] * inv], axis=-1)


def kernel(x_prompt, x_sample, cache_attn_k, cache_attn_v, state_hgrn, state_ret, c, c_ctx, mod_w, mod_b, norm1_g, norm2_g, even_w_in, even_w_out, hgrn_lb_logits, hgrn_norm_g, attn_qn_g, attn_kn_g, odd_w_in, odd_w_out, ret_decay_logit, ret_gn_g, moe_w_group, moe_b_group, moe_w_expert, moe_b_expert, moe_w_gate, moe_w_up, moe_w_down, final_norm_g):
    batch, seq, d = x_prompt.shape
    dec_batch, dec_seq, _ = x_sample.shape
    depth = mod_w.shape[0]
    heads_a, val_a = hgrn_norm_g.shape[1:]
    key_a = hgrn_lb_logits.shape[2] // heads_a
    hd_b = attn_qn_g.shape[1]
    kv_b = cache_attn_k.shape[3]
    heads_b = (even_w_out.shape[1] - heads_a * val_a) // hd_b
    heads_c, val_c = ret_gn_g.shape[1:]
    key_c = state_ret.shape[4]
    n_experts = moe_w_expert.shape[2]
    n_groups = moe_w_group.shape[2]
    tm = TOKEN_TILE
    n_ctx = batch * seq
    n_ctx_tiles = n_ctx // tm
    tiles_per_req = dec_seq // tm
    assert seq % tm == 0 and dec_seq % tm == 0 and n_ctx % dec_seq == 0
    assert key_a == 128 and val_a == 128 and hd_b == 128 and n_groups + n_experts <= 128

    x = jnp.concatenate([x_prompt.reshape(n_ctx, d), x_sample.reshape(dec_batch * dec_seq, d)], axis=0)
    t = x.shape[0]

    n_vec = 1 + dec_batch
    rows = -(-n_vec // 8) * 8
    cvecs = jnp.zeros((rows, d), F32).at[0].set(c_ctx).at[1:n_vec].set(c)
    mods = _ada_mod(cvecs, mod_w, mod_b).reshape(depth, rows, 6, d)

    lower_bounds = jnp.cumsum(jax.nn.softmax(hgrn_lb_logits.astype(F32), axis=0), axis=0)
    log_gamma = jax.nn.log_sigmoid(ret_decay_logit.astype(F32))

    ang_b = _axial_angles(dec_seq, hd_b)
    cos_b = jnp.concatenate([jnp.cos(ang_b), jnp.cos(ang_b)], axis=-1)
    sin_b = jnp.concatenate([-jnp.sin(ang_b), jnp.sin(ang_b)], axis=-1)
    ang_c = _axial_angles(dec_seq, key_c)
    cos_c, sin_c = jnp.cos(ang_c), jnp.sin(ang_c)

    pad = 128 - n_groups - n_experts
    new_k = new_v = new_sh = new_sr = None
    assert depth == 2, "one HGRN2/GQA layer followed by one retention layer"
    for l in range(depth):
        mod = mods[l]
        if l % 2 == 0:
            e = l // 2
            w_in = even_w_in[e].astype(BF16)
            proj = _in_proj(x, norm1_g[l], mod, w_in, n_ctx_tiles, tiles_per_req, w_in.shape[1])
            lb = lower_bounds[l]
            oa, s_ctx = _hgrn(proj, lb, hgrn_norm_g[e], None, e, None, batch, seq, 0, heads_a, key_a, val_a)
            oa, _ = _hgrn(proj, lb, hgrn_norm_g[e], state_hgrn, e, oa, dec_batch, dec_seq, n_ctx // dec_seq,
                          heads_a, key_a, val_a)
            col0 = (3 * heads_a * key_a + 2 * heads_a * val_a) // 128
            ob, kc, vc = _attn(proj, attn_qn_g[e], attn_kn_g[e], None, e, None, batch, seq, 0,
                               heads_b, kv_b, hd_b, col0)
            (ob,) = _attn(proj, attn_qn_g[e], attn_kn_g[e], (cache_attn_k, cache_attn_v, cos_b, sin_b), e, ob,
                          dec_batch, dec_seq, n_ctx // dec_seq, heads_b, kv_b, hd_b, col0)
            new_k = kc.reshape(batch, 1, seq, kv_b, hd_b)
            new_v = vc.reshape(batch, 1, seq, kv_b, hd_b)
            new_sh = s_ctx[:, None]
            w_out = even_w_out[e].astype(BF16)
            mixes = [oa, ob]
            w_parts = [w_out[:heads_a * val_a], w_out[heads_a * val_a:]]
        else:
            o = l // 2
            w_in = odd_w_in[o].astype(BF16)
            proj = _in_proj(x, norm1_g[l], mod, w_in, n_ctx_tiles, tiles_per_req, w_in.shape[1] // 2)
            y, s_ctx = _retention(proj, log_gamma[o], ret_gn_g[o], None, o, None, batch, seq, 0,
                                  heads_c, key_c, val_c)
            y, _ = _retention(proj, log_gamma[o], ret_gn_g[o], (state_ret, cos_c, sin_c), o, y,
                              dec_batch, dec_seq, n_ctx // dec_seq, heads_c, key_c, val_c)
            new_sr = s_ctx[:, None]
            mixes = [y]
            w_parts = [odd_w_out[o].astype(BF16)]

        w_router = jnp.concatenate([moe_w_group[l], moe_w_expert[l], jnp.zeros((d, pad), F32)], axis=1)
        w_router_hi = w_router.astype(BF16)
        w_router = jnp.stack([w_router_hi, (w_router - w_router_hi.astype(F32)).astype(BF16)])
        b_router = jnp.concatenate([moe_b_group[l], moe_b_expert[l], jnp.zeros((pad,), F32)]).reshape(1, 128)
        xn, h2, eidx, ew = _out_proj(mixes, w_parts, x, mod, norm2_g[l], w_router, b_router,
                                     n_ctx_tiles, tiles_per_req, n_groups, n_experts)
        tile_expert, n_used, ends, pos1, pos2, n_tiles = _dispatch_plan(
            eidx[:, 0], eidx[:, 1], n_experts, EXPERT_TILE)
        src_tok = _row_source(ends, pos1, pos2, n_tiles * EXPERT_TILE, EXPERT_TILE)
        y_sorted = _moe_experts(h2, src_tok, tile_expert, n_used, moe_w_gate, moe_w_up, moe_w_down, l, EXPERT_TILE)
        x = _moe_combine(y_sorted, pos1, pos2, xn, ew, mod, final_norm_g, n_ctx_tiles, tiles_per_req,
                         final_norm=(l == depth - 1))

    y_prompt = x[:n_ctx].reshape(batch, seq, d)
    y_sample = x[n_ctx:].reshape(dec_batch, dec_seq, d)
    return (y_prompt, y_sample, new_k, new_v, new_sh, new_sr)
```

```python
import functools

import numpy as np
import jax
import jax.numpy as jnp
from jax import lax
from jax.experimental import pallas as pl
from jax.experimental.pallas import tpu as pltpu

F32 = jnp.float32
BF16 = jnp.bfloat16
EPS = 1e-6
ROPE_BASE = 10000.0
GRID_W = 64
N_GROUPS = 4
TOP_K_INNER = 2
HGRN_BLOCK = 128
HGRN_DIAG = 8
RET_CHUNK = 256
TOKEN_TILE = 256
EXPERT_TILE = 256
MOE_GATHER_AHEAD = 2
V7X_VMEM_LIMIT_BYTES = 56 * 1024 * 1024
HIGHEST = lax.Precision.HIGHEST
LOG2_E = 1.4426950408889634

_NT = (((1,), (1,)), ((), ()))


def _cparams(*sem):
    return pltpu.CompilerParams(dimension_semantics=sem, vmem_limit_bytes=V7X_VMEM_LIMIT_BYTES)


def _rms(x, g):
    return x * lax.rsqrt(jnp.mean(x * x, axis=-1, keepdims=True) + EPS) * g


def _silu(x):
    return x * jax.nn.sigmoid(x)


def _bdot(a, b):
    return jnp.dot(a.astype(BF16), b.astype(BF16), preferred_element_type=F32)


def _bdot_nt(a, b):
    return lax.dot_general(a.astype(BF16), b.astype(BF16), _NT, preferred_element_type=F32)


def _mod_kernel(c_ref, w_ref, b_ref, o_ref):
    a = _silu(c_ref[...])
    o_ref[0] = jnp.dot(a, w_ref[0], precision=HIGHEST, preferred_element_type=F32) + b_ref[0]


def _ada_mod(cvecs, mod_w, mod_b):
    depth, d, n = mod_w.shape
    tn = n // 4
    rows = cvecs.shape[0]
    return pl.pallas_call(
        _mod_kernel,
        grid=(depth, n // tn),
        in_specs=[pl.BlockSpec((rows, d), lambda l, j: (0, 0)),
                  pl.BlockSpec((1, d, tn), lambda l, j: (l, 0, j)),
                  pl.BlockSpec((1, 1, tn), lambda l, j: (l, 0, j))],
        out_specs=pl.BlockSpec((1, rows, tn), lambda l, j: (l, 0, j)),
        out_shape=jax.ShapeDtypeStruct((depth, rows, n), F32),
        compiler_params=_cparams("arbitrary", "arbitrary"),
        name="ada_mod",
    )(cvecs, mod_w, mod_b.reshape(depth, 1, n))


def _tile_mod_row(i, n_ctx_tiles, tiles_per_req):
    return jnp.where(i < n_ctx_tiles, 0, 1 + jnp.maximum(i - n_ctx_tiles, 0) // tiles_per_req)


def _token_specs(x, tm, n_ctx_tiles, step_of):
    if not isinstance(x, tuple):
        return [pl.BlockSpec((tm, x.shape[1]), lambda *g: (step_of(*g), 0))], (x,)
    d = x[0].shape[1]
    return ([pl.BlockSpec((tm, d), lambda *g: (jnp.minimum(step_of(*g), n_ctx_tiles - 1), 0)),
             pl.BlockSpec((tm, d), lambda *g: (jnp.maximum(step_of(*g) - n_ctx_tiles, 0), 0))], x)


def _token_rows(x_refs, i, n_ctx_tiles):
    if len(x_refs) == 1:
        return x_refs[0][...]
    return jnp.where(i < n_ctx_tiles, x_refs[0][...], x_refs[1][...])


def _inproj_kernel(*refs, n_x, n_ctx_tiles):
    x_refs = refs[:n_x]
    g_ref, mod_ref, w_ref, o_ref = refs[n_x:]
    x = _token_rows(x_refs, pl.program_id(1), n_ctx_tiles)
    h = _rms(x, g_ref[...]) * (1.0 + mod_ref[0, 1:2, :]) + mod_ref[0, 0:1, :]
    o_ref[...] = jnp.dot(h.astype(BF16), w_ref[...], preferred_element_type=F32)


def _in_proj(x, g, mod, w_bf, t, n_ctx_tiles, tiles_per_req, tn):
    d, n = w_bf.shape
    tm = TOKEN_TILE
    mrow = functools.partial(_tile_mod_row, n_ctx_tiles=n_ctx_tiles, tiles_per_req=tiles_per_req)
    x_specs, x_args = _token_specs(x, tm, n_ctx_tiles, lambda j, i: i)
    return pl.pallas_call(
        functools.partial(_inproj_kernel, n_x=len(x_args), n_ctx_tiles=n_ctx_tiles),
        grid=(n // tn, t // tm),
        in_specs=x_specs + [pl.BlockSpec((1, d), lambda j, i: (0, 0)),
                            pl.BlockSpec((1, 6, d), lambda j, i: (mrow(i), 0, 0)),
                            pl.BlockSpec((d, tn), lambda j, i: (0, j))],
        out_specs=pl.BlockSpec((tm, tn), lambda j, i: (i, j)),
        out_shape=jax.ShapeDtypeStruct((t, n), F32),
        compiler_params=_cparams("arbitrary", "arbitrary"),
        name="in_proj",
    )(*x_args, g.reshape(1, d), mod, w_bf)


def _hgrn_decay(z, lb, reverse):
    nb = HGRN_BLOCK
    row = lax.broadcasted_iota(jnp.int32, (nb, nb), 0)
    col = lax.broadcasted_iota(jnp.int32, (nb, nb), 1)
    tri = jnp.where((col >= row) if reverse else (col <= row), 1.0, 0.0).astype(BF16)
    logf = jnp.log(lb + (1.0 - lb) * jax.nn.sigmoid(z))
    k = (1.0 - lb) * jax.nn.sigmoid(-z)
    l2 = logf * LOG2_E
    hi = l2.astype(BF16)
    lo = (l2 - hi.astype(F32)).astype(BF16)
    c = jnp.dot(tri, hi, preferred_element_type=F32) + jnp.dot(tri, lo, preferred_element_type=F32)
    return c, k


def _hgrn_block(q, v, st_t, reverse, c_ref, k_ref, ecat):
    nb = HGRN_BLOCK
    row = lax.broadcasted_iota(jnp.int32, (nb, nb), 0)
    col = lax.broadcasted_iota(jnp.int32, (nb, nb), 1)
    c = c_ref[...]
    k = k_ref[...]
    edge = 0 if reverse else nb - 1
    c_edge = c_ref[edge:edge + 1, :]

    o = _bdot_nt(q * jnp.exp2(c), st_t)
    kd = k * jnp.exp2(c_edge - c)
    st_new = st_t * jnp.exp2(c_edge) + _bdot(v.T, kd)

    a = jnp.zeros((nb, nb), F32)
    m = nb // 2
    while m >= HGRN_DIAG:
        pieces = []
        for grp in range(nb // (2 * m)):
            idx = grp * 2 * m + (m if reverse else m - 1)
            pieces.append(jnp.broadcast_to(c_ref[idx:idx + 1, :], (2 * m, nb)))
        bnd = pieces[0] if len(pieces) == 1 else jnp.concatenate(pieces, axis=0)
        e = jnp.exp2(-jnp.abs(c - bnd))
        am = _bdot_nt(q * e, k * e)
        same = (row & ~(2 * m - 1)) == (col & ~(2 * m - 1))
        r_hi = (row & m) != 0
        c_hi = (col & m) != 0
        if reverse:
            mask = same & jnp.logical_not(r_hi) & c_hi
        else:
            mask = same & r_hi & jnp.logical_not(c_hi)
        a = jnp.where(mask, am, a)
        m //= 2

    dg = HGRN_DIAG
    rowi = lax.broadcasted_iota(jnp.int32, (dg, nb), 0)
    cols = []
    for s in range(dg):
        keep = (rowi <= s) if reverse else (rowi >= s)
        tiles = []
        for j in range(nb // dg):
            r = j * dg + s
            qj = q[j * dg:(j + 1) * dg, :]
            cj = c[j * dg:(j + 1) * dg, :]
            cs = jnp.broadcast_to(c_ref[r:r + 1, :], (dg, nb))
            ks = jnp.broadcast_to(k_ref[r:r + 1, :], (dg, nb))
            tiles.append(qj * ks * jnp.exp2(jnp.where(keep, cj - cs, -jnp.inf)))
        cols.append(jnp.concatenate(tiles, axis=0).astype(BF16))
    p = jnp.concatenate(cols, axis=1)
    red = jnp.dot(p, ecat, preferred_element_type=F32)
    a = jnp.where((row >> 3) == (col >> 3), red, a)
    o = o + _bdot(a, v)
    return o, st_new


def _hgrn_kernel(*refs, nblk, has_state, qscale):
    qa_ref, ia_ref, zf_ref, zb_ref, ga_ref, lb_ref, ng_ref, ecat_ref = refs[:8]
    s0_ref = refs[8] if has_state else None
    o_ref, sfin_ref, oacc_ref, c_scr, k_scr = refs[-5:]
    nb = HGRN_BLOCK

    def gates(direction, blk, slot):
        z_ref = zb_ref if direction == 1 else zf_ref
        r0 = pl.multiple_of(blk * nb, nb)
        c, k = _hgrn_decay(z_ref[pl.ds(r0, nb), :], lb_ref[direction:direction + 1, :], direction == 1)
        c_scr[direction, slot] = c
        k_scr[direction, slot] = k

    def block(direction, blk, slot, st_t):
        r0 = pl.multiple_of(blk * nb, nb)
        q = _silu(qa_ref[pl.ds(r0, nb), :]) * qscale
        v = ia_ref[pl.ds(r0, nb), :]
        o, st_new = _hgrn_block(q, v, st_t, direction == 1, c_scr.at[direction, slot],
                                k_scr.at[direction, slot], ecat_ref[...])
        oacc_ref[direction, pl.ds(r0, nb), :] = o
        return st_new

    gates(0, 0, 0)
    gates(1, nblk - 1, 0)

    def body(i, carry):
        st_f, st_b = carry
        slot = lax.rem(i, 2)
        st_f = block(0, i, slot, st_f)
        st_b = block(1, nblk - 1 - i, slot, st_b)
        nxt = jnp.minimum(i + 1, nblk - 1)
        gates(0, nxt, 1 - slot)
        gates(1, nblk - 1 - nxt, 1 - slot)
        return st_f, st_b

    if has_state:
        st0 = (s0_ref[0].T, s0_ref[1].T)
    else:
        zero = jnp.zeros((ia_ref.shape[1], qa_ref.shape[1]), F32)
        st0 = (zero, zero)
    st_f, st_b = lax.fori_loop(0, nblk, body, st0)
    sfin_ref[0] = st_f.T
    sfin_ref[1] = st_b.T

    o = oacc_ref[0] + oacc_ref[1]
    o_ref[...] = _rms(o, ng_ref[0]) * _silu(ga_ref[...])


def _hgrn(proj, lb, norm_g, s0, layer_idx, out_prev, n_seq, seq, row_blk0, heads, key_d, val_d):
    t = proj.shape[0]
    has_state = s0 is not None
    hk = heads * key_d // 128
    off_qa, off_ia, off_zf, off_zb, off_ga = 0, hk, 2 * hk, 3 * hk, 4 * hk

    def col(off):
        return pl.BlockSpec((seq, 128), lambda b, h: (row_blk0 + b, off + h))

    lane = np.arange(key_d)[None, :]
    srow = np.repeat(np.arange(HGRN_DIAG), key_d)[:, None]
    ecat = jnp.asarray((lane % HGRN_DIAG == srow).astype(np.float32), dtype=BF16)
    in_specs = [col(off_qa), col(off_ia), col(off_zf), col(off_zb), col(off_ga),
                pl.BlockSpec((2, key_d), lambda b, h: (0, h)),
                pl.BlockSpec((1, 1, val_d), lambda b, h: (h, 0, 0)),
                pl.BlockSpec((HGRN_DIAG * key_d, key_d), lambda b, h: (0, 0))]
    args = [proj, proj, proj, proj, proj, lb, norm_g.reshape(heads, 1, val_d), ecat]
    if has_state:
        in_specs.append(pl.BlockSpec((None, None, 2, None, key_d, val_d),
                                     lambda b, h: (b, layer_idx, 0, h, 0, 0)))
        args.append(s0)
    aliases = {}
    if out_prev is not None:
        in_specs.append(pl.BlockSpec(memory_space=pl.ANY))
        args.append(out_prev)
        aliases = {len(args) - 1: 0}

    kern = functools.partial(_hgrn_kernel, nblk=seq // HGRN_BLOCK, has_state=has_state,
                             qscale=float(key_d) ** -0.5)
    return pl.pallas_call(
        kern,
        grid=(n_seq, heads),
        in_specs=in_specs,
        out_specs=[pl.BlockSpec((seq, val_d), lambda b, h: (row_blk0 + b, h)),
                   pl.BlockSpec((None, 2, None, key_d, val_d), lambda b, h: (b, 0, h, 0, 0))],
        out_shape=[jax.ShapeDtypeStruct((t, heads * val_d), F32),
                   jax.ShapeDtypeStruct((n_seq, 2, heads, key_d, val_d), F32)],
        scratch_shapes=[pltpu.VMEM((2, seq, val_d), F32),
                        pltpu.VMEM((2, 2, HGRN_BLOCK, key_d), F32),
                        pltpu.VMEM((2, 2, HGRN_BLOCK, key_d), F32)],
        input_output_aliases=aliases,
        compiler_params=_cparams("arbitrary", "arbitrary"),
        name="hgrn2",
    )(*args)


def _rope(x, cos, sin_signed):
    return x * cos + pltpu.roll(x, x.shape[-1] // 2, 1) * sin_signed


def _attn_kernel(*refs, has_ctx, grp, qblk, scale):
    if has_ctx:
        (q_ref, k_ref, v_ref, qn_ref, kn_ref, ck_ref, cv_ref, cos_ref, sin_ref, _prev, o_ref) = refs
    else:
        (q_ref, k_ref, v_ref, qn_ref, kn_ref, o_ref, ko_ref, vo_ref) = refs
    seq, hd = k_ref.shape
    k = _rms(k_ref[...], kn_ref[...])
    v = v_ref[...]
    if has_ctx:
        k = _rope(k, cos_ref[...], sin_ref[...])
        ck = ck_ref[...].astype(BF16)
        cv = cv_ref[...].astype(BF16)
    else:
        ko_ref[...] = k
        vo_ref[...] = v
    kb = k.astype(BF16)
    vb = v.astype(BF16)

    for g in range(grp):
        def body(i, carry, g=g):
            r0 = pl.multiple_of(i * qblk, qblk)
            q = _rms(q_ref[pl.ds(r0, qblk), g * hd:(g + 1) * hd], qn_ref[...])
            if has_ctx:
                q = _rope(q, cos_ref[pl.ds(r0, qblk), :], sin_ref[pl.ds(r0, qblk), :])
            qb = q.astype(BF16)
            s2 = lax.dot_general(qb, kb, _NT, preferred_element_type=F32) * scale
            m = jnp.max(s2, axis=-1, keepdims=True)
            if has_ctx:
                s1 = lax.dot_general(qb, ck, _NT, preferred_element_type=F32) * scale
                m = jnp.maximum(m, jnp.max(s1, axis=-1, keepdims=True))
                p1 = jnp.exp(s1 - m)
            p2 = jnp.exp(s2 - m)
            den = jnp.sum(p2, axis=-1, keepdims=True)
            acc = jnp.dot(p2.astype(BF16), vb, preferred_element_type=F32)
            if has_ctx:
                den = den + jnp.sum(p1, axis=-1, keepdims=True)
                acc = acc + jnp.dot(p1.astype(BF16), cv, preferred_element_type=F32)
            o_ref[pl.ds(r0, qblk), g * hd:(g + 1) * hd] = acc / den
            return carry

        lax.fori_loop(0, seq // qblk, body, 0)


def _attn(proj, qn_g, kn_g, ctx, layer_idx, out_prev, n_seq, seq, row_blk0, heads, kv_heads, hd, col0):
    t = proj.shape[0]
    grp = heads // kv_heads
    has_ctx = ctx is not None
    cq = col0 // grp
    ck0 = col0 + heads
    cv0 = ck0 + kv_heads
    in_specs = [pl.BlockSpec((seq, grp * hd), lambda b, h: (row_blk0 + b, cq + h)),
                pl.BlockSpec((seq, hd), lambda b, h: (row_blk0 + b, ck0 + h)),
                pl.BlockSpec((seq, hd), lambda b, h: (row_blk0 + b, cv0 + h)),
                pl.BlockSpec((1, hd), lambda b, h: (0, 0)),
                pl.BlockSpec((1, hd), lambda b, h: (0, 0))]
    args = [proj, proj, proj, qn_g.reshape(1, hd), kn_g.reshape(1, hd)]
    out_specs = [pl.BlockSpec((seq, grp * hd), lambda b, h: (row_blk0 + b, h))]
    out_shape = [jax.ShapeDtypeStruct((t, heads * hd), F32)]
    aliases = {}
    if has_ctx:
        cache_k, cache_v, cos, sin_signed = ctx
        nreq, nlay, past = cache_k.shape[:3]
        cache_k = cache_k.reshape(nreq, nlay, past, kv_heads * hd)
        cache_v = cache_v.reshape(nreq, nlay, past, kv_heads * hd)
        spec_c = pl.BlockSpec((None, None, past, hd), lambda b, h: (b, layer_idx, 0, h))
        in_specs += [spec_c, spec_c,
                     pl.BlockSpec((seq, hd), lambda b, h: (0, 0)),
                     pl.BlockSpec((seq, hd), lambda b, h: (0, 0)),
                     pl.BlockSpec(memory_space=pl.ANY)]
        args += [cache_k, cache_v, cos, sin_signed, out_prev]
        aliases = {len(args) - 1: 0}
    else:
        out_specs += [pl.BlockSpec((seq, hd), lambda b, h: (b, h)),
                      pl.BlockSpec((seq, hd), lambda b, h: (b, h))]
        out_shape += [jax.ShapeDtypeStruct((n_seq * seq, kv_heads * hd), F32)] * 2
    return pl.pallas_call(
        functools.partial(_attn_kernel, has_ctx=has_ctx, grp=grp, qblk=min(seq, 256), scale=float(hd) ** -0.5),
        grid=(n_seq, kv_heads),
        in_specs=in_specs,
        out_specs=out_specs,
        out_shape=out_shape,
        input_output_aliases=aliases,
        compiler_params=_cparams("arbitrary", "arbitrary"),
        name="gqa",
    )(*args)


def _ret_kernel(*refs, nchunk, has_ctx, kscale):
    if has_ctx:
        (lg_ref, q_ref, k_ref, v_ref, g_ref, gn_ref, s0_ref, cos_ref, sin_ref, _prev,
         o_ref, sfin_ref, sb_ref) = refs
    else:
        (lg_ref, q_ref, k_ref, v_ref, g_ref, gn_ref, o_ref, sfin_ref, sb_ref) = refs
    cs = RET_CHUNK
    kd_dim = q_ref.shape[1]
    h = pl.program_id(1)
    lgf = lg_ref[0, h]
    lgb = lg_ref[1, h]
    row = lax.broadcasted_iota(jnp.int32, (cs, cs), 0)
    col = lax.broadcasted_iota(jnp.int32, (cs, cs), 1)
    rel = (row - col).astype(F32)
    dmat = (jnp.exp(jnp.where(rel >= 0, rel * lgf, -jnp.inf))
            + jnp.exp(jnp.where(rel <= 0, -rel * lgb, -jnp.inf)))
    j = lax.broadcasted_iota(jnp.int32, (cs, 1), 0).astype(F32)
    one = jnp.ones((1, 1), F32)
    dec_f = jnp.exp(one * (cs * lgf))
    dec_b = jnp.exp(one * (cs * lgb))

    def load_qk(ref, r0, mul):
        x = ref[pl.ds(r0, cs), :] * mul
        if has_ctx:
            half = kd_dim // 2
            x1, x2 = x[:, :half], x[:, half:]
            cos = cos_ref[pl.ds(r0, cs), :]
            sin = sin_ref[pl.ds(r0, cs), :]
            x = jnp.concatenate([x1 * cos - x2 * sin, x2 * cos + x1 * sin], axis=1)
        return x

    def bwd_body(i, st):
        n = nchunk - 1 - i
        r0 = pl.multiple_of(n * cs, cs)
        sb_ref[n] = st
        k = load_qk(k_ref, r0, kscale)
        v = v_ref[pl.ds(r0, cs), :]
        return st * dec_b + _bdot((k * jnp.exp(j * lgb)).T, v)

    st_b0 = s0_ref[1] if has_ctx else jnp.zeros(sfin_ref.shape[1:], F32)
    st_b = lax.fori_loop(0, nchunk, bwd_body, st_b0)
    sfin_ref[1] = st_b

    def fwd_body(n, st):
        r0 = pl.multiple_of(n * cs, cs)
        q = load_qk(q_ref, r0, 1.0)
        k = load_qk(k_ref, r0, kscale)
        v = v_ref[pl.ds(r0, cs), :]
        a = _bdot_nt(q, k) * dmat
        o = _bdot(a, v)
        if has_ctx or nchunk > 1:
            o = o + _bdot(q * jnp.exp((j + 1.0) * lgf), st)
            o = o + _bdot(q * jnp.exp((cs - j) * lgb), sb_ref[n])
        mu = jnp.mean(o, axis=-1, keepdims=True)
        var = jnp.mean(jnp.square(o - mu), axis=-1, keepdims=True)
        y = (o - mu) * lax.rsqrt(var + EPS) * gn_ref[0]
        o_ref[pl.ds(r0, cs), :] = _silu(g_ref[pl.ds(r0, cs), :]) * y
        return st * dec_f + _bdot((k * jnp.exp((cs - 1.0 - j) * lgf)).T, v)

    st_f0 = s0_ref[0] if has_ctx else jnp.zeros(sfin_ref.shape[1:], F32)
    sfin_ref[0] = lax.fori_loop(0, nchunk, fwd_body, st_f0)


def _retention(proj, log_gamma, gn_g, ctx, layer_idx, out_prev, n_seq, seq, row_blk0, heads, key_d, val_d):
    t = proj.shape[0]
    has_ctx = ctx is not None
    nq = heads * key_d
    cq, ck, cv, cg = 0, nq // key_d, 2 * nq // val_d, (2 * nq + heads * val_d) // val_d
    in_specs = [pl.BlockSpec(memory_space=pltpu.SMEM),
                pl.BlockSpec((seq, key_d), lambda b, h: (row_blk0 + b, cq + h)),
                pl.BlockSpec((seq, key_d), lambda b, h: (row_blk0 + b, ck + h)),
                pl.BlockSpec((seq, val_d), lambda b, h: (row_blk0 + b, cv + h)),
                pl.BlockSpec((seq, val_d), lambda b, h: (row_blk0 + b, cg + h)),
                pl.BlockSpec((1, 1, val_d), lambda b, h: (h, 0, 0))]
    args = [log_gamma, proj, proj, proj, proj, gn_g.reshape(heads, 1, val_d)]
    aliases = {}
    if has_ctx:
        s0, cos, sin = ctx
        in_specs += [pl.BlockSpec((None, None, 2, None, key_d, val_d), lambda b, h: (b, layer_idx, 0, h, 0, 0)),
                     pl.BlockSpec((seq, key_d // 2), lambda b, h: (0, 0)),
                     pl.BlockSpec((seq, key_d // 2), lambda b, h: (0, 0)),
                     pl.BlockSpec(memory_space=pl.ANY)]
        args += [s0, cos, sin, out_prev]
        aliases = {len(args) - 1: 0}
    nchunk = seq // RET_CHUNK
    return pl.pallas_call(
        functools.partial(_ret_kernel, nchunk=nchunk, has_ctx=has_ctx, kscale=float(key_d) ** -0.5),
        grid=(n_seq, heads),
        in_specs=in_specs,
        out_specs=[pl.BlockSpec((seq, val_d), lambda b, h: (row_blk0 + b, h)),
                   pl.BlockSpec((None, 2, None, key_d, val_d), lambda b, h: (b, 0, h, 0, 0))],
        out_shape=[jax.ShapeDtypeStruct((t, heads * val_d), F32),
                   jax.ShapeDtypeStruct((n_seq, 2, heads, key_d, val_d), F32)],
        scratch_shapes=[pltpu.VMEM((nchunk, key_d, val_d), F32)],
        input_output_aliases=aliases,
        compiler_params=_cparams("arbitrary", "arbitrary"),
        name="retention",
    )(*args)


def _outproj_kernel(*refs, n_in, n_x, n_ctx_tiles, n_groups, n_experts):
    mix_refs = refs[:n_in]
    w_refs = refs[n_in:2 * n_in]
    x_refs = refs[2 * n_in:2 * n_in + n_x]
    mod_ref, g2_ref, wr_ref, br_ref, xn_ref, h2_ref, eidx_ref, ew_ref = refs[2 * n_in + n_x:]
    acc = _bdot(mix_refs[0][...], w_refs[0][...])
    for mref, wref in zip(mix_refs[1:], w_refs[1:]):
        acc = acc + _bdot(mref[...], wref[...])
    xn = _token_rows(x_refs, pl.program_id(0), n_ctx_tiles) + mod_ref[0, 2:3, :] * acc
    xn_ref[...] = xn
    h2 = _rms(xn, g2_ref[...]) * (1.0 + mod_ref[0, 4:5, :]) + mod_ref[0, 3:4, :]
    h2_ref[...] = h2.reshape(h2_ref.shape)
    h_hi = h2.astype(BF16)
    h_lo = (h2 - h_hi.astype(F32)).astype(BF16)
    lg = (jnp.dot(h_hi, wr_ref[0], preferred_element_type=F32)
          + jnp.dot(h_hi, wr_ref[1], preferred_element_type=F32)
          + jnp.dot(h_lo, wr_ref[0], preferred_element_type=F32)) + br_ref[...]

    per_grp = n_experts // n_groups
    lane = lax.broadcasted_iota(jnp.int32, lg.shape, 1).astype(F32)
    big = float(lg.shape[1])
    ninf = -jnp.inf
    gl = jnp.where(lane < n_groups, lg, ninf)
    gmax = jnp.max(gl, axis=-1, keepdims=True)
    p_grp = 1.0 / jnp.sum(jnp.exp(gl - gmax), axis=-1, keepdims=True)
    gidx = jnp.min(jnp.where(gl == gmax, lane, big), axis=-1, keepdims=True)
    in_grp = ((lane >= n_groups) & (lane < n_groups + n_experts)
              & (jnp.floor((lane - n_groups) / per_grp) == gidx))
    el = jnp.where(in_grp, lg, ninf)
    emax = jnp.max(el, axis=-1, keepdims=True)
    esum = jnp.sum(jnp.exp(el - emax), axis=-1, keepdims=True)
    i1 = jnp.min(jnp.where(el == emax, lane, big), axis=-1, keepdims=True)
    el2 = jnp.where(lane == i1, ninf, el)
    emax2 = jnp.max(el2, axis=-1, keepdims=True)
    i2 = jnp.min(jnp.where(el2 == emax2, lane, big), axis=-1, keepdims=True)
    p1 = 1.0 / esum
    p2 = jnp.exp(emax2 - emax) / esum
    w1 = p1 / (p1 + p2) * p_grp
    w2 = p2 / (p1 + p2) * p_grp
    eidx_ref[...] = jnp.where(lane == 0.0, i1 - n_groups, jnp.where(lane == 1.0, i2 - n_groups, 0.0)).astype(jnp.int32)
    ew_ref[...] = jnp.where(lane == 0.0, w1, jnp.where(lane == 1.0, w2, 0.0))


def _out_proj(mixes, w_parts, x, mod, g2, w_router, b_router, n_ctx_tiles, tiles_per_req, n_groups, n_experts):
    t = mixes[0].shape[0]
    d = w_parts[0].shape[1]
    tm = TOKEN_TILE
    n_in = len(mixes)
    mrow = functools.partial(_tile_mod_row, n_ctx_tiles=n_ctx_tiles, tiles_per_req=tiles_per_req)
    x_specs, x_args = _token_specs(x, tm, n_ctx_tiles, lambda i: i)
    in_specs = ([pl.BlockSpec((tm, m.shape[1]), lambda i: (i, 0)) for m in mixes]
                + [pl.BlockSpec(w.shape, lambda i: (0, 0)) for w in w_parts]
                + x_specs
                + [pl.BlockSpec((1, 6, d), lambda i: (mrow(i), 0, 0)),
                   pl.BlockSpec((1, d), lambda i: (0, 0)),
                   pl.BlockSpec(w_router.shape, lambda i: (0, 0, 0)),
                   pl.BlockSpec((1, 128), lambda i: (0, 0))])
    return pl.pallas_call(
        functools.partial(_outproj_kernel, n_in=n_in, n_x=len(x_args), n_ctx_tiles=n_ctx_tiles,
                          n_groups=n_groups, n_experts=n_experts),
        grid=(t // tm,),
        in_specs=in_specs,
        out_specs=[pl.BlockSpec((tm, d), lambda i: (i, 0)),
                   pl.BlockSpec((tm, d // 128, 128), lambda i: (i, 0, 0)),
                   pl.BlockSpec((tm, 128), lambda i: (i, 0)),
                   pl.BlockSpec((tm, 128), lambda i: (i, 0))],
        out_shape=[jax.ShapeDtypeStruct((t, d), F32),
                   jax.ShapeDtypeStruct((t, d // 128, 128), F32),
                   jax.ShapeDtypeStruct((t, 128), jnp.int32),
                   jax.ShapeDtypeStruct((t, 128), F32)],
        compiler_params=_cparams("arbitrary"),
        name="out_proj_router",
    )(*mixes, *w_parts, *x_args, mod, g2.reshape(1, d), w_router, b_router)


def _dispatch_plan(e1, e2, n_experts, tile):
    t = e1.shape[0]
    e = jnp.concatenate([e1, e2])
    onehot = (e[:, None] == jnp.arange(n_experts, dtype=jnp.int32)[None, :]).astype(jnp.int32)
    csum = jnp.cumsum(onehot, axis=0)
    rank = jnp.take_along_axis(csum, e[:, None], axis=1)[:, 0] - 1
    counts = csum[-1]
    padded = ((counts + tile - 1) // tile) * tile
    ends = jnp.cumsum(padded)
    starts = ends - padded
    pos = starts[e] + rank
    n_tiles = (2 * t) // tile + n_experts
    n_used = (ends[-1] // tile).astype(jnp.int32)
    tile_start = jnp.arange(n_tiles, dtype=jnp.int32) * tile
    probe = jnp.minimum(tile_start, ends[-1] - 1)
    tile_expert = jnp.sum((ends[None, :] <= probe[:, None]).astype(jnp.int32), axis=1)
    return tile_expert, n_used.reshape(1), ends.astype(jnp.int32), pos[:t], pos[t:], n_tiles


def _row_source_kernel(ends_ref, p1_ref, p2_ref, o_ref, *, n_tok, n_rows, tile, n_experts):
    def clear8(j, c):
        for u in range(8):
            o_ref[j * 8 + u] = 0
        return c

    def per_expert(e, c):
        first = lax.shift_right_logical(jnp.maximum(ends_ref[e] - tile, 0), 3)
        lax.fori_loop(first, first + tile // 8, clear8, 0)
        return c

    lax.fori_loop(0, n_experts, per_expert, 0)
    lax.fori_loop(lax.shift_right_logical(ends_ref[n_experts - 1], 3), n_rows // 8, clear8, 0)

    def place(t, c):
        o_ref[p1_ref[t]] = t
        o_ref[p2_ref[t]] = t
        return c

    lax.fori_loop(0, n_tok, place, 0, unroll=8)


def _row_source(ends, pos1, pos2, n_rows, tile):
    n_tok = pos1.shape[0]
    smem = pl.BlockSpec(memory_space=pltpu.SMEM)
    return pl.pallas_call(
        functools.partial(_row_source_kernel, n_tok=n_tok, n_rows=n_rows, tile=tile, n_experts=ends.shape[0]),
        in_specs=[smem, smem, smem],
        out_specs=smem,
        out_shape=jax.ShapeDtypeStruct((n_rows,), jnp.int32),
        name="moe_row_source",
    )(ends, pos1, pos2)


def _moe_kernel(te_ref, nu_ref, tok_ref, h_hbm, wg_ref, wu_ref, wd_ref, o_ref,
                wg_bf, wu_bf, wd_bf, buf, sem, *, tm, ahead):
    i = pl.program_id(0)
    nslot = ahead + 1
    slot = lax.rem(i, nslot)
    n_used = nu_ref[0]

    def issue(tile):
        s = lax.rem(tile, nslot)
        base = tile * tm

        def body(j, c):
            for u in range(2):
                r = 2 * j + u
                pltpu.make_async_copy(h_hbm.at[tok_ref[base + r]], buf.at[s, r], sem.at[s]).start(priority=u)
            return c

        lax.fori_loop(0, tm // 2, body, 0, unroll=4)

    @pl.when(i == 0)
    def _():
        for j in range(ahead):
            @pl.when(j < n_used)
            def _():
                issue(jnp.int32(j))

    @pl.when(i + ahead < n_used)
    def _():
        issue(i + ahead)

    new_expert = jnp.logical_or(i == 0, te_ref[i] != te_ref[jnp.maximum(i - 1, 0)])

    @pl.when(new_expert)
    def _():
        wg_bf[...] = wg_ref[...].astype(BF16)
        wu_bf[...] = wu_ref[...].astype(BF16)
        wd_bf[...] = wd_ref[...].astype(BF16)

    @pl.when(i < n_used)
    def _():
        pltpu.make_async_copy(h_hbm.at[pl.ds(0, tm)], buf.at[slot], sem.at[slot]).wait()
        x = buf[slot].reshape(tm, -1).astype(BF16)
        g = jnp.dot(x, wg_bf[...], preferred_element_type=F32)
        u = jnp.dot(x, wu_bf[...], preferred_element_type=F32)
        act = _silu(g) * u
        y = jnp.dot(act.astype(BF16), wd_bf[...], preferred_element_type=F32)
        o_ref[...] = y.reshape(o_ref.shape)

    @pl.when(i >= n_used)
    def _():
        o_ref[...] = jnp.zeros_like(o_ref)


def _moe_experts(h3, src_tok, tile_expert, n_used, w_gate, w_up, w_down, layer_idx, tm):
    t, sub, lanes = h3.shape
    d = sub * lanes
    n_tiles = src_tok.shape[0] // tm
    f = w_gate.shape[-1]
    ahead = MOE_GATHER_AHEAD
    return pl.pallas_call(
        functools.partial(_moe_kernel, tm=tm, ahead=ahead),
        grid_spec=pltpu.PrefetchScalarGridSpec(
            num_scalar_prefetch=3,
            grid=(n_tiles,),
            in_specs=[pl.BlockSpec(memory_space=pl.ANY),
                      pl.BlockSpec((None, None, d, f), lambda i, te, nu, tok: (layer_idx, te[i], 0, 0)),
                      pl.BlockSpec((None, None, d, f), lambda i, te, nu, tok: (layer_idx, te[i], 0, 0)),
                      pl.BlockSpec((None, None, f, d), lambda i, te, nu, tok: (layer_idx, te[i], 0, 0))],
            out_specs=pl.BlockSpec((tm, sub, lanes), lambda i, te, nu, tok: (i, 0, 0)),
            scratch_shapes=[pltpu.VMEM((d, f), BF16), pltpu.VMEM((d, f), BF16), pltpu.VMEM((f, d), BF16),
                            pltpu.VMEM((ahead + 1, tm, sub, lanes), F32),
                            pltpu.SemaphoreType.DMA((ahead + 1,))]),
        out_shape=jax.ShapeDtypeStruct((n_tiles * tm, sub, lanes), F32),
        compiler_params=_cparams("arbitrary"),
        name="moe_experts",
    )(tile_expert, n_used, src_tok, h3, w_gate, w_up, w_down)


def _combine_kernel(p1_ref, p2_ref, p1n_ref, p2n_ref, y_hbm, x_ref, ew_ref, mod_ref, gf_ref, *rest,
                    tm, n_steps, n_ctx_tiles, final_norm):
    out_refs, (buf, sem) = rest[:-2], rest[-2:]
    i = pl.program_id(0)
    slot = lax.rem(i, 2)

    def issue(s, pa_ref, pb_ref):
        def body(r, c):
            pltpu.make_async_copy(y_hbm.at[pa_ref[0, 0, r]], buf.at[s, 0, r], sem.at[s, 0]).start(priority=0)
            pltpu.make_async_copy(y_hbm.at[pb_ref[0, 0, r]], buf.at[s, 1, r], sem.at[s, 1]).start(priority=1)
            return c

        lax.fori_loop(0, tm, body, 0, unroll=8)

    @pl.when(i == 0)
    def _():
        issue(0, p1_ref, p2_ref)

    @pl.when(i + 1 < n_steps)
    def _():
        issue(1 - slot, p1n_ref, p2n_ref)

    for k in range(2):
        pltpu.make_async_copy(y_hbm.at[pl.ds(0, tm)], buf.at[slot, k], sem.at[slot, k]).wait()
    ew = ew_ref[...]
    y = ew[:, 0:1] * buf[slot, 0].reshape(tm, -1) + ew[:, 1:2] * buf[slot, 1].reshape(tm, -1)
    x = x_ref[...] + mod_ref[0, 5:6, :] * y
    if not final_norm:
        out_refs[0][...] = x
    else:
        x = _rms(x, gf_ref[...])
        ctx_ref, lat_ref = out_refs

        @pl.when(i < n_ctx_tiles)
        def _():
            ctx_ref[...] = x

        @pl.when(i >= n_ctx_tiles)
        def _():
            lat_ref[...] = x


def _moe_combine(y_sorted, pos1, pos2, x, ew, mod, final_g, n_ctx_tiles, tiles_per_req, final_norm):
    t, d = x.shape
    tm = TOKEN_TILE
    n_steps = t // tm
    mrow = functools.partial(_tile_mod_row, n_ctx_tiles=n_ctx_tiles, tiles_per_req=tiles_per_req)
    smem_rows = pl.BlockSpec((1, 1, tm), lambda i: (i, 0, 0), memory_space=pltpu.SMEM)
    smem_next = pl.BlockSpec((1, 1, tm), lambda i: (jnp.minimum(i + 1, n_steps - 1), 0, 0),
                             memory_space=pltpu.SMEM)
    p1 = pos1.reshape(n_steps, 1, tm)
    p2 = pos2.reshape(n_steps, 1, tm)
    if final_norm:
        out_specs = [pl.BlockSpec((tm, d), lambda i: (jnp.minimum(i, n_ctx_tiles - 1), 0)),
                     pl.BlockSpec((tm, d), lambda i: (jnp.maximum(i - n_ctx_tiles, 0), 0))]
        out_shape = [jax.ShapeDtypeStruct((n_ctx_tiles * tm, d), F32),
                     jax.ShapeDtypeStruct((t - n_ctx_tiles * tm, d), F32)]
    else:
        out_specs = pl.BlockSpec((tm, d), lambda i: (i, 0))
        out_shape = jax.ShapeDtypeStruct((t, d), F32)
    return pl.pallas_call(
        functools.partial(_combine_kernel, tm=tm, n_steps=n_steps, n_ctx_tiles=n_ctx_tiles,
                          final_norm=final_norm),
        grid=(n_steps,),
        in_specs=[smem_rows, smem_rows, smem_next, smem_next,
                  pl.BlockSpec(memory_space=pl.ANY),
                  pl.BlockSpec((tm, d), lambda i: (i, 0)),
                  pl.BlockSpec((tm, 128), lambda i: (i, 0)),
                  pl.BlockSpec((1, 6, d), lambda i: (mrow(i), 0, 0)),
                  pl.BlockSpec((1, d), lambda i: (0, 0))],
        out_specs=out_specs,
        out_shape=out_shape,
        scratch_shapes=[pltpu.VMEM((2, 2, tm) + y_sorted.shape[1:], F32), pltpu.SemaphoreType.DMA((2, 2))],
        compiler_params=_cparams("arbitrary"),
        name="moe_combine",
    )(p1, p2, p1, p2, y_sorted, x, ew, mod, final_g.reshape(1, d))


def _axial_angles(seq, head_dim):
    rows = seq // GRID_W
    row = jnp.repeat(jnp.arange(rows), GRID_W).astype(F32)
    col = jnp.tile(jnp.arange(GRID_W), rows).astype(F32)
    n_freq = head_dim // 4
    inv = ROPE_BASE ** (-jnp.arange(n_freq, dtype=F32) / n_freq)
    return jnp.concatenate([row[:, None] * inv, col[:, None] * inv], axis=-1)


def kernel(x_prompt, x_sample, cache_attn_k, cache_attn_v, state_hgrn, state_ret, c, c_ctx, mod_w, mod_b, norm1_g, norm2_g, even_w_in, even_w_out, hgrn_lb_logits, hgrn_norm_g, attn_qn_g, attn_kn_g, odd_w_in, odd_w_out, ret_decay_logit, ret_gn_g, moe_w_group, moe_b_group, moe_w_expert, moe_b_expert, moe_w_gate, moe_w_up, moe_w_down, final_norm_g):
    batch, seq, d = x_prompt.shape
    dec_batch, dec_seq, _ = x_sample.shape
    depth = mod_w.shape[0]
    heads_a, val_a = hgrn_norm_g.shape[1:]
    key_a = hgrn_lb_logits.shape[2] // heads_a
    hd_b = attn_qn_g.shape[1]
    kv_b = cache_attn_k.shape[3]
    heads_b = (even_w_out.shape[1] - heads_a * val_a) // hd_b
    heads_c, val_c = ret_gn_g.shape[1:]
    key_c = state_ret.shape[4]
    n_experts = moe_w_expert.shape[2]
    n_groups = moe_w_group.shape[2]
    tm = TOKEN_TILE
    n_ctx = batch * seq
    n_ctx_tiles = n_ctx // tm
    tiles_per_req = dec_seq // tm
    assert seq % tm == 0 and dec_seq % tm == 0 and n_ctx % dec_seq == 0
    assert key_a == 128 and val_a == 128 and hd_b == 128 and n_groups + n_experts <= 128

    x = (x_prompt.reshape(n_ctx, d), x_sample.reshape(dec_batch * dec_seq, d))
    t = n_ctx + dec_batch * dec_seq

    n_vec = 1 + dec_batch
    rows = -(-n_vec // 8) * 8
    cvecs = jnp.zeros((rows, d), F32).at[0].set(c_ctx).at[1:n_vec].set(c)
    mods = _ada_mod(cvecs, mod_w, mod_b).reshape(depth, rows, 6, d)

    lower_bounds = jnp.cumsum(jax.nn.softmax(hgrn_lb_logits.astype(F32), axis=0), axis=0)
    log_gamma = jax.nn.log_sigmoid(ret_decay_logit.astype(F32))

    ang_b = _axial_angles(dec_seq, hd_b)
    cos_b = jnp.concatenate([jnp.cos(ang_b), jnp.cos(ang_b)], axis=-1)
    sin_b = jnp.concatenate([-jnp.sin(ang_b), jnp.sin(ang_b)], axis=-1)
    ang_c = _axial_angles(dec_seq, key_c)
    cos_c, sin_c = jnp.cos(ang_c), jnp.sin(ang_c)

    pad = 128 - n_groups - n_experts
    new_k = new_v = new_sh = new_sr = None
    assert depth == 2, "one HGRN2/GQA layer followed by one retention layer"
    for l in range(depth):
        mod = mods[l]
        if l % 2 == 0:
            e = l // 2
            w_in = even_w_in[e].astype(BF16)
            proj = _in_proj(x, norm1_g[l], mod, w_in, t, n_ctx_tiles, tiles_per_req, w_in.shape[1])
            lb = lower_bounds[l]
            oa, s_ctx = _hgrn(proj, lb, hgrn_norm_g[e], None, e, None, batch, seq, 0, heads_a, key_a, val_a)
            oa, _ = _hgrn(proj, lb, hgrn_norm_g[e], state_hgrn, e, oa, dec_batch, dec_seq, n_ctx // dec_seq,
                          heads_a, key_a, val_a)
            col0 = (3 * heads_a * key_a + 2 * heads_a * val_a) // 128
            ob, kc, vc = _attn(proj, attn_qn_g[e], attn_kn_g[e], None, e, None, batch, seq, 0,
                               heads_b, kv_b, hd_b, col0)
            (ob,) = _attn(proj, attn_qn_g[e], attn_kn_g[e], (cache_attn_k, cache_attn_v, cos_b, sin_b), e, ob,
                          dec_batch, dec_seq, n_ctx // dec_seq, heads_b, kv_b, hd_b, col0)
            new_k = kc.reshape(batch, 1, seq, kv_b, hd_b)
            new_v = vc.reshape(batch, 1, seq, kv_b, hd_b)
            new_sh = s_ctx[:, None]
            w_out = even_w_out[e].astype(BF16)
            mixes = [oa, ob]
            w_parts = [w_out[:heads_a * val_a], w_out[heads_a * val_a:]]
        else:
            o = l // 2
            w_in = odd_w_in[o].astype(BF16)
            proj = _in_proj(x, norm1_g[l], mod, w_in, t, n_ctx_tiles, tiles_per_req, w_in.shape[1] // 2)
            y, s_ctx = _retention(proj, log_gamma[o], ret_gn_g[o], None, o, None, batch, seq, 0,
                                  heads_c, key_c, val_c)
            y, _ = _retention(proj, log_gamma[o], ret_gn_g[o], (state_ret, cos_c, sin_c), o, y,
                              dec_batch, dec_seq, n_ctx // dec_seq, heads_c, key_c, val_c)
            new_sr = s_ctx[:, None]
            mixes = [y]
            w_parts = [odd_w_out[o].astype(BF16)]

        w_router = jnp.concatenate([moe_w_group[l], moe_w_expert[l], jnp.zeros((d, pad), F32)], axis=1)
        w_router_hi = w_router.astype(BF16)
        w_router = jnp.stack([w_router_hi, (w_router - w_router_hi.astype(F32)).astype(BF16)])
        b_router = jnp.concatenate([moe_b_group[l], moe_b_expert[l], jnp.zeros((pad,), F32)]).reshape(1, 128)
        xn, h2, eidx, ew = _out_proj(mixes, w_parts, x, mod, norm2_g[l], w_router, b_router,
                                     n_ctx_tiles, tiles_per_req, n_groups, n_experts)
        tile_expert, n_used, ends, pos1, pos2, n_tiles = _dispatch_plan(
            eidx[:, 0], eidx[:, 1], n_experts, EXPERT_TILE)
        src_tok = _row_source(ends, pos1, pos2, n_tiles * EXPERT_TILE, EXPERT_TILE)
        y_sorted = _moe_experts(h2, src_tok, tile_expert, n_used, moe_w_gate, moe_w_up, moe_w_down, l, EXPERT_TILE)
        x = _moe_combine(y_sorted, pos1, pos2, xn, ew, mod, final_norm_g, n_ctx_tiles, tiles_per_req,
                         final_norm=(l == depth - 1))

    y_prompt = x[0].reshape(batch, seq, d)
    y_sample = x[1].reshape(dec_batch, dec_seq, d)
    return (y_prompt, y_sample, new_k, new_v, new_sh, new_sr)
```

```python
import functools

import numpy as np
import jax
import jax.numpy as jnp
from jax import lax
from jax.experimental import pallas as pl
from jax.experimental.pallas import tpu as pltpu

F32 = jnp.float32
BF16 = jnp.bfloat16
EPS = 1e-6
ROPE_BASE = 10000.0
GRID_W = 64
N_GROUPS = 4
TOP_K_INNER = 2
HGRN_BLOCK = 128
HGRN_DIAG = 8
RET_CHUNK = 256
TOKEN_TILE = 256
EXPERT_TILE = 256
MOE_GATHER_AHEAD = 2
V7X_VMEM_LIMIT_BYTES = 56 * 1024 * 1024
HIGHEST = lax.Precision.HIGHEST
LOG2_E = 1.4426950408889634

_NT = (((1,), (1,)), ((), ()))


def _cparams(*sem):
    return pltpu.CompilerParams(dimension_semantics=sem, vmem_limit_bytes=V7X_VMEM_LIMIT_BYTES)


def _rms(x, g):
    return x * lax.rsqrt(jnp.mean(x * x, axis=-1, keepdims=True) + EPS) * g


def _silu(x):
    return x * jax.nn.sigmoid(x)


def _bdot(a, b):
    return jnp.dot(a.astype(BF16), b.astype(BF16), preferred_element_type=F32)


def _bdot_nt(a, b):
    return lax.dot_general(a.astype(BF16), b.astype(BF16), _NT, preferred_element_type=F32)


def _mod_kernel(c_ref, w_ref, b_ref, o_ref):
    a = _silu(c_ref[...])
    o_ref[0] = jnp.dot(a, w_ref[0], precision=HIGHEST, preferred_element_type=F32) + b_ref[0]


def _ada_mod(cvecs, mod_w, mod_b):
    depth, d, n = mod_w.shape
    tn = n // 4
    rows = cvecs.shape[0]
    return pl.pallas_call(
        _mod_kernel,
        grid=(depth, n // tn),
        in_specs=[pl.BlockSpec((rows, d), lambda l, j: (0, 0)),
                  pl.BlockSpec((1, d, tn), lambda l, j: (l, 0, j)),
                  pl.BlockSpec((1, 1, tn), lambda l, j: (l, 0, j))],
        out_specs=pl.BlockSpec((1, rows, tn), lambda l, j: (l, 0, j)),
        out_shape=jax.ShapeDtypeStruct((depth, rows, n), F32),
        compiler_params=_cparams("arbitrary", "arbitrary"),
        name="ada_mod",
    )(cvecs, mod_w, mod_b.reshape(depth, 1, n))


def _tile_mod_row(i, n_ctx_tiles, tiles_per_req):
    return jnp.where(i < n_ctx_tiles, 0, 1 + jnp.maximum(i - n_ctx_tiles, 0) // tiles_per_req)


def _token_specs(x, tm, n_ctx_tiles, step_of):
    if not isinstance(x, tuple):
        return [pl.BlockSpec((tm, x.shape[1]), lambda *g: (step_of(*g), 0))], (x,)
    d = x[0].shape[1]
    return ([pl.BlockSpec((tm, d), lambda *g: (jnp.minimum(step_of(*g), n_ctx_tiles - 1), 0)),
             pl.BlockSpec((tm, d), lambda *g: (jnp.maximum(step_of(*g) - n_ctx_tiles, 0), 0))], x)


def _token_rows(x_refs, i, n_ctx_tiles):
    if len(x_refs) == 1:
        return x_refs[0][...]
    return jnp.where(i < n_ctx_tiles, x_refs[0][...], x_refs[1][...])


def _inproj_kernel(*refs, n_x, n_ctx_tiles):
    x_refs = refs[:n_x]
    g_ref, mod_ref, w_ref, o_ref = refs[n_x:]
    x = _token_rows(x_refs, pl.program_id(1), n_ctx_tiles)
    h = _rms(x, g_ref[...]) * (1.0 + mod_ref[0, 1:2, :]) + mod_ref[0, 0:1, :]
    o_ref[...] = jnp.dot(h.astype(BF16), w_ref[...], preferred_element_type=F32)


def _in_proj(x, g, mod, w_bf, t, n_ctx_tiles, tiles_per_req, tn):
    d, n = w_bf.shape
    tm = TOKEN_TILE
    mrow = functools.partial(_tile_mod_row, n_ctx_tiles=n_ctx_tiles, tiles_per_req=tiles_per_req)
    x_specs, x_args = _token_specs(x, tm, n_ctx_tiles, lambda j, i: i)
    return pl.pallas_call(
        functools.partial(_inproj_kernel, n_x=len(x_args), n_ctx_tiles=n_ctx_tiles),
        grid=(n // tn, t // tm),
        in_specs=x_specs + [pl.BlockSpec((1, d), lambda j, i: (0, 0)),
                            pl.BlockSpec((1, 6, d), lambda j, i: (mrow(i), 0, 0)),
                            pl.BlockSpec((d, tn), lambda j, i: (0, j))],
        out_specs=pl.BlockSpec((tm, tn), lambda j, i: (i, j)),
        out_shape=jax.ShapeDtypeStruct((t, n), F32),
        compiler_params=_cparams("arbitrary", "arbitrary"),
        name="in_proj",
    )(*x_args, g.reshape(1, d), mod, w_bf)


def _hgrn_decay(z, lb, reverse):
    nb = HGRN_BLOCK
    row = lax.broadcasted_iota(jnp.int32, (nb, nb), 0)
    col = lax.broadcasted_iota(jnp.int32, (nb, nb), 1)
    tri = jnp.where((col >= row) if reverse else (col <= row), 1.0, 0.0).astype(BF16)
    logf = jnp.log(lb + (1.0 - lb) * jax.nn.sigmoid(z))
    k = (1.0 - lb) * jax.nn.sigmoid(-z)
    l2 = logf * LOG2_E
    hi = l2.astype(BF16)
    lo = (l2 - hi.astype(F32)).astype(BF16)
    c = jnp.dot(tri, hi, preferred_element_type=F32) + jnp.dot(tri, lo, preferred_element_type=F32)
    return c, k


def _hgrn_block(q, v, st_t, reverse, c_ref, k_ref, ecat):
    nb = HGRN_BLOCK
    row = lax.broadcasted_iota(jnp.int32, (nb, nb), 0)
    col = lax.broadcasted_iota(jnp.int32, (nb, nb), 1)
    c = c_ref[...]
    k = k_ref[...]
    edge = 0 if reverse else nb - 1
    c_edge = c_ref[edge:edge + 1, :]

    o = _bdot_nt(q * jnp.exp2(c), st_t)
    kd = k * jnp.exp2(c_edge - c)
    st_new = st_t * jnp.exp2(c_edge) + _bdot(v.T, kd)

    a = jnp.zeros((nb, nb), F32)
    m = nb // 2
    while m >= HGRN_DIAG:
        pieces = []
        for grp in range(nb // (2 * m)):
            idx = grp * 2 * m + (m if reverse else m - 1)
            pieces.append(jnp.broadcast_to(c_ref[idx:idx + 1, :], (2 * m, nb)))
        bnd = pieces[0] if len(pieces) == 1 else jnp.concatenate(pieces, axis=0)
        e = jnp.exp2(-jnp.abs(c - bnd))
        am = _bdot_nt(q * e, k * e)
        same = (row & ~(2 * m - 1)) == (col & ~(2 * m - 1))
        r_hi = (row & m) != 0
        c_hi = (col & m) != 0
        if reverse:
            mask = same & jnp.logical_not(r_hi) & c_hi
        else:
            mask = same & r_hi & jnp.logical_not(c_hi)
        a = jnp.where(mask, am, a)
        m //= 2

    dg = HGRN_DIAG
    rowi = lax.broadcasted_iota(jnp.int32, (dg, nb), 0)
    cols = []
    for s in range(dg):
        keep = (rowi <= s) if reverse else (rowi >= s)
        tiles = []
        for j in range(nb // dg):
            r = j * dg + s
            qj = q[j * dg:(j + 1) * dg, :]
            cj = c[j * dg:(j + 1) * dg, :]
            cs = jnp.broadcast_to(c_ref[r:r + 1, :], (dg, nb))
            ks = jnp.broadcast_to(k_ref[r:r + 1, :], (dg, nb))
            tiles.append(qj * ks * jnp.exp2(jnp.where(keep, cj - cs, -jnp.inf)))
        cols.append(jnp.concatenate(tiles, axis=0).astype(BF16))
    p = jnp.concatenate(cols, axis=1)
    red = jnp.dot(p, ecat, preferred_element_type=F32)
    a = jnp.where((row >> 3) == (col >> 3), red, a)
    o = o + _bdot(a, v)
    return o, st_new


def _hgrn_kernel(*refs, nblk, has_state, qscale):
    qa_ref, ia_ref, zf_ref, zb_ref, ga_ref, lb_ref, ng_ref, ecat_ref = refs[:8]
    s0_ref = refs[8] if has_state else None
    o_ref, sfin_ref, oacc_ref, c_scr, k_scr = refs[-5:]
    nb = HGRN_BLOCK

    def gates(direction, blk, slot):
        z_ref = zb_ref if direction == 1 else zf_ref
        r0 = pl.multiple_of(blk * nb, nb)
        c, k = _hgrn_decay(z_ref[pl.ds(r0, nb), :], lb_ref[direction:direction + 1, :], direction == 1)
        c_scr[direction, slot] = c
        k_scr[direction, slot] = k

    def block(direction, blk, slot, st_t):
        r0 = pl.multiple_of(blk * nb, nb)
        q = _silu(qa_ref[pl.ds(r0, nb), :]) * qscale
        v = ia_ref[pl.ds(r0, nb), :]
        o, st_new = _hgrn_block(q, v, st_t, direction == 1, c_scr.at[direction, slot],
                                k_scr.at[direction, slot], ecat_ref[...])
        oacc_ref[direction, pl.ds(r0, nb), :] = o
        return st_new

    gates(0, 0, 0)
    gates(1, nblk - 1, 0)

    def body(i, carry):
        st_f, st_b = carry
        slot = lax.rem(i, 2)
        st_f = block(0, i, slot, st_f)
        st_b = block(1, nblk - 1 - i, slot, st_b)
        nxt = jnp.minimum(i + 1, nblk - 1)
        gates(0, nxt, 1 - slot)
        gates(1, nblk - 1 - nxt, 1 - slot)
        return st_f, st_b

    if has_state:
        st0 = (s0_ref[0].T, s0_ref[1].T)
    else:
        zero = jnp.zeros((ia_ref.shape[1], qa_ref.shape[1]), F32)
        st0 = (zero, zero)
    st_f, st_b = lax.fori_loop(0, nblk, body, st0)
    sfin_ref[0] = st_f.T
    sfin_ref[1] = st_b.T

    o = oacc_ref[0] + oacc_ref[1]
    o_ref[...] = _rms(o, ng_ref[0]) * _silu(ga_ref[...])


def _hgrn(proj, lb, norm_g, s0, layer_idx, out_prev, n_seq, seq, row_blk0, heads, key_d, val_d):
    t = proj.shape[0]
    has_state = s0 is not None
    hk = heads * key_d // 128
    off_qa, off_ia, off_zf, off_zb, off_ga = 0, hk, 2 * hk, 3 * hk, 4 * hk

    def col(off):
        return pl.BlockSpec((seq, 128), lambda b, h: (row_blk0 + b, off + h))

    lane = np.arange(key_d)[None, :]
    srow = np.repeat(np.arange(HGRN_DIAG), key_d)[:, None]
    ecat = jnp.asarray((lane % HGRN_DIAG == srow).astype(np.float32), dtype=BF16)
    in_specs = [col(off_qa), col(off_ia), col(off_zf), col(off_zb), col(off_ga),
                pl.BlockSpec((2, key_d), lambda b, h: (0, h)),
                pl.BlockSpec((1, 1, val_d), lambda b, h: (h, 0, 0)),
                pl.BlockSpec((HGRN_DIAG * key_d, key_d), lambda b, h: (0, 0))]
    args = [proj, proj, proj, proj, proj, lb, norm_g.reshape(heads, 1, val_d), ecat]
    if has_state:
        in_specs.append(pl.BlockSpec((None, None, 2, None, key_d, val_d),
                                     lambda b, h: (b, layer_idx, 0, h, 0, 0)))
        args.append(s0)
    aliases = {}
    if out_prev is not None:
        in_specs.append(pl.BlockSpec(memory_space=pl.ANY))
        args.append(out_prev)
        aliases = {len(args) - 1: 0}

    kern = functools.partial(_hgrn_kernel, nblk=seq // HGRN_BLOCK, has_state=has_state,
                             qscale=float(key_d) ** -0.5)
    return pl.pallas_call(
        kern,
        grid=(n_seq, heads),
        in_specs=in_specs,
        out_specs=[pl.BlockSpec((seq, val_d), lambda b, h: (row_blk0 + b, h)),
                   pl.BlockSpec((None, 2, None, key_d, val_d), lambda b, h: (b, 0, h, 0, 0))],
        out_shape=[jax.ShapeDtypeStruct((t, heads * val_d), F32),
                   jax.ShapeDtypeStruct((n_seq, 2, heads, key_d, val_d), F32)],
        scratch_shapes=[pltpu.VMEM((2, seq, val_d), F32),
                        pltpu.VMEM((2, 2, HGRN_BLOCK, key_d), F32),
                        pltpu.VMEM((2, 2, HGRN_BLOCK, key_d), F32)],
        input_output_aliases=aliases,
        compiler_params=_cparams("arbitrary", "arbitrary"),
        name="hgrn2",
    )(*args)


def _rope(x, cos, sin_signed):
    return x * cos + pltpu.roll(x, x.shape[-1] // 2, 1) * sin_signed


def _attn_kernel(*refs, has_ctx, grp, qblk, scale):
    if has_ctx:
        (q_ref, k_ref, v_ref, qn_ref, kn_ref, ck_ref, cv_ref, cos_ref, sin_ref, _prev, o_ref) = refs
    else:
        (q_ref, k_ref, v_ref, qn_ref, kn_ref, o_ref, ko_ref, vo_ref) = refs
    seq, hd = k_ref.shape
    k = _rms(k_ref[...], kn_ref[...])
    v = v_ref[...]
    if has_ctx:
        k = _rope(k, cos_ref[...], sin_ref[...])
        ck = ck_ref[...].astype(BF16)
        cv = cv_ref[...].astype(BF16)
    else:
        ko_ref[...] = k
        vo_ref[...] = v
    kb = k.astype(BF16)
    vb = v.astype(BF16)

    for g in range(grp):
        def body(i, carry, g=g):
            r0 = pl.multiple_of(i * qblk, qblk)
            q = _rms(q_ref[pl.ds(r0, qblk), g * hd:(g + 1) * hd], qn_ref[...])
            if has_ctx:
                q = _rope(q, cos_ref[pl.ds(r0, qblk), :], sin_ref[pl.ds(r0, qblk), :])
            qb = q.astype(BF16)
            s2 = lax.dot_general(qb, kb, _NT, preferred_element_type=F32) * scale
            m = jnp.max(s2, axis=-1, keepdims=True)
            if has_ctx:
                s1 = lax.dot_general(qb, ck, _NT, preferred_element_type=F32) * scale
                m = jnp.maximum(m, jnp.max(s1, axis=-1, keepdims=True))
                p1 = jnp.exp(s1 - m)
            p2 = jnp.exp(s2 - m)
            den = jnp.sum(p2, axis=-1, keepdims=True)
            acc = jnp.dot(p2.astype(BF16), vb, preferred_element_type=F32)
            if has_ctx:
                den = den + jnp.sum(p1, axis=-1, keepdims=True)
                acc = acc + jnp.dot(p1.astype(BF16), cv, preferred_element_type=F32)
            o_ref[pl.ds(r0, qblk), g * hd:(g + 1) * hd] = acc / den
            return carry

        lax.fori_loop(0, seq // qblk, body, 0)


def _attn(proj, qn_g, kn_g, ctx, layer_idx, out_prev, n_seq, seq, row_blk0, heads, kv_heads, hd, col0):
    t = proj.shape[0]
    grp = heads // kv_heads
    has_ctx = ctx is not None
    cq = col0 // grp
    ck0 = col0 + heads
    cv0 = ck0 + kv_heads
    in_specs = [pl.BlockSpec((seq, grp * hd), lambda b, h: (row_blk0 + b, cq + h)),
                pl.BlockSpec((seq, hd), lambda b, h: (row_blk0 + b, ck0 + h)),
                pl.BlockSpec((seq, hd), lambda b, h: (row_blk0 + b, cv0 + h)),
                pl.BlockSpec((1, hd), lambda b, h: (0, 0)),
                pl.BlockSpec((1, hd), lambda b, h: (0, 0))]
    args = [proj, proj, proj, qn_g.reshape(1, hd), kn_g.reshape(1, hd)]
    out_specs = [pl.BlockSpec((seq, grp * hd), lambda b, h: (row_blk0 + b, h))]
    out_shape = [jax.ShapeDtypeStruct((t, heads * hd), F32)]
    aliases = {}
    if has_ctx:
        cache_k, cache_v, cos, sin_signed = ctx
        nreq, nlay, past = cache_k.shape[:3]
        cache_k = cache_k.reshape(nreq, nlay, past, kv_heads * hd)
        cache_v = cache_v.reshape(nreq, nlay, past, kv_heads * hd)
        spec_c = pl.BlockSpec((None, None, past, hd), lambda b, h: (b, layer_idx, 0, h))
        in_specs += [spec_c, spec_c,
                     pl.BlockSpec((seq, hd), lambda b, h: (0, 0)),
                     pl.BlockSpec((seq, hd), lambda b, h: (0, 0)),
                     pl.BlockSpec(memory_space=pl.ANY)]
        args += [cache_k, cache_v, cos, sin_signed, out_prev]
        aliases = {len(args) - 1: 0}
    else:
        out_specs += [pl.BlockSpec((seq, hd), lambda b, h: (b, h)),
                      pl.BlockSpec((seq, hd), lambda b, h: (b, h))]
        out_shape += [jax.ShapeDtypeStruct((n_seq * seq, kv_heads * hd), F32)] * 2
    return pl.pallas_call(
        functools.partial(_attn_kernel, has_ctx=has_ctx, grp=grp, qblk=min(seq, 256), scale=float(hd) ** -0.5),
        grid=(n_seq, kv_heads),
        in_specs=in_specs,
        out_specs=out_specs,
        out_shape=out_shape,
        input_output_aliases=aliases,
        compiler_params=_cparams("arbitrary", "arbitrary"),
        name="gqa",
    )(*args)


def _ret_kernel(*refs, nchunk, has_ctx, kscale):
    if has_ctx:
        (lg_ref, q_ref, k_ref, v_ref, g_ref, gn_ref, s0_ref, cos_ref, sin_ref, _prev,
         o_ref, sfin_ref, sb_ref) = refs
    else:
        (lg_ref, q_ref, k_ref, v_ref, g_ref, gn_ref, o_ref, sfin_ref, sb_ref) = refs
    cs = RET_CHUNK
    kd_dim = q_ref.shape[1]
    h = pl.program_id(1)
    lgf = lg_ref[0, h]
    lgb = lg_ref[1, h]
    row = lax.broadcasted_iota(jnp.int32, (cs, cs), 0)
    col = lax.broadcasted_iota(jnp.int32, (cs, cs), 1)
    rel = (row - col).astype(F32)
    dmat = (jnp.exp(jnp.where(rel >= 0, rel * lgf, -jnp.inf))
            + jnp.exp(jnp.where(rel <= 0, -rel * lgb, -jnp.inf)))
    j = lax.broadcasted_iota(jnp.int32, (cs, 1), 0).astype(F32)
    one = jnp.ones((1, 1), F32)
    dec_f = jnp.exp(one * (cs * lgf))
    dec_b = jnp.exp(one * (cs * lgb))

    def load_qk(ref, r0, mul):
        x = ref[pl.ds(r0, cs), :] * mul
        if has_ctx:
            half = kd_dim // 2
            x1, x2 = x[:, :half], x[:, half:]
            cos = cos_ref[pl.ds(r0, cs), :]
            sin = sin_ref[pl.ds(r0, cs), :]
            x = jnp.concatenate([x1 * cos - x2 * sin, x2 * cos + x1 * sin], axis=1)
        return x

    def bwd_body(i, st):
        n = nchunk - 1 - i
        r0 = pl.multiple_of(n * cs, cs)
        sb_ref[n] = st
        k = load_qk(k_ref, r0, kscale)
        v = v_ref[pl.ds(r0, cs), :]
        return st * dec_b + _bdot((k * jnp.exp(j * lgb)).T, v)

    st_b0 = s0_ref[1] if has_ctx else jnp.zeros(sfin_ref.shape[1:], F32)
    st_b = lax.fori_loop(0, nchunk, bwd_body, st_b0)
    sfin_ref[1] = st_b

    def fwd_body(n, st):
        r0 = pl.multiple_of(n * cs, cs)
        q = load_qk(q_ref, r0, 1.0)
        k = load_qk(k_ref, r0, kscale)
        v = v_ref[pl.ds(r0, cs), :]
        a = _bdot_nt(q, k) * dmat
        o = _bdot(a, v)
        if has_ctx or nchunk > 1:
            o = o + _bdot(q * jnp.exp((j + 1.0) * lgf), st)
            o = o + _bdot(q * jnp.exp((cs - j) * lgb), sb_ref[n])
        mu = jnp.mean(o, axis=-1, keepdims=True)
        var = jnp.mean(jnp.square(o - mu), axis=-1, keepdims=True)
        y = (o - mu) * lax.rsqrt(var + EPS) * gn_ref[0]
        o_ref[pl.ds(r0, cs), :] = _silu(g_ref[pl.ds(r0, cs), :]) * y
        return st * dec_f + _bdot((k * jnp.exp((cs - 1.0 - j) * lgf)).T, v)

    st_f0 = s0_ref[0] if has_ctx else jnp.zeros(sfin_ref.shape[1:], F32)
    sfin_ref[0] = lax.fori_loop(0, nchunk, fwd_body, st_f0)


def _retention(proj, log_gamma, gn_g, ctx, layer_idx, out_prev, n_seq, seq, row_blk0, heads, key_d, val_d):
    t = proj.shape[0]
    has_ctx = ctx is not None
    nq = heads * key_d
    cq, ck, cv, cg = 0, nq // key_d, 2 * nq // val_d, (2 * nq + heads * val_d) // val_d
    in_specs = [pl.BlockSpec(memory_space=pltpu.SMEM),
                pl.BlockSpec((seq, key_d), lambda b, h: (row_blk0 + b, cq + h)),
                pl.BlockSpec((seq, key_d), lambda b, h: (row_blk0 + b, ck + h)),
                pl.BlockSpec((seq, val_d), lambda b, h: (row_blk0 + b, cv + h)),
                pl.BlockSpec((seq, val_d), lambda b, h: (row_blk0 + b, cg + h)),
                pl.BlockSpec((1, 1, val_d), lambda b, h: (h, 0, 0))]
    args = [log_gamma, proj, proj, proj, proj, gn_g.reshape(heads, 1, val_d)]
    aliases = {}
    if has_ctx:
        s0, cos, sin = ctx
        in_specs += [pl.BlockSpec((None, None, 2, None, key_d, val_d), lambda b, h: (b, layer_idx, 0, h, 0, 0)),
                     pl.BlockSpec((seq, key_d // 2), lambda b, h: (0, 0)),
                     pl.BlockSpec((seq, key_d // 2), lambda b, h: (0, 0)),
                     pl.BlockSpec(memory_space=pl.ANY)]
        args += [s0, cos, sin, out_prev]
        aliases = {len(args) - 1: 0}
    nchunk = seq // RET_CHUNK
    return pl.pallas_call(
        functools.partial(_ret_kernel, nchunk=nchunk, has_ctx=has_ctx, kscale=float(key_d) ** -0.5),
        grid=(n_seq, heads),
        in_specs=in_specs,
        out_specs=[pl.BlockSpec((seq, val_d), lambda b, h: (row_blk0 + b, h)),
                   pl.BlockSpec((None, 2, None, key_d, val_d), lambda b, h: (b, 0, h, 0, 0))],
        out_shape=[jax.ShapeDtypeStruct((t, heads * val_d), F32),
                   jax.ShapeDtypeStruct((n_seq, 2, heads, key_d, val_d), F32)],
        scratch_shapes=[pltpu.VMEM((nchunk, key_d, val_d), F32)],
        input_output_aliases=aliases,
        compiler_params=_cparams("arbitrary", "arbitrary"),
        name="retention",
    )(*args)


def _outproj_kernel(*refs, n_in, n_x, n_ctx_tiles, n_groups, n_experts):
    mix_refs = refs[:n_in]
    w_refs = refs[n_in:2 * n_in]
    x_refs = refs[2 * n_in:2 * n_in + n_x]
    mod_ref, g2_ref, wr_ref, br_ref, xn_ref, h2_ref, eidx_ref, ew_ref = refs[2 * n_in + n_x:]
    acc = _bdot(mix_refs[0][...], w_refs[0][...])
    for mref, wref in zip(mix_refs[1:], w_refs[1:]):
        acc = acc + _bdot(mref[...], wref[...])
    xn = _token_rows(x_refs, pl.program_id(0), n_ctx_tiles) + mod_ref[0, 2:3, :] * acc
    xn_ref[...] = xn
    h2 = _rms(xn, g2_ref[...]) * (1.0 + mod_ref[0, 4:5, :]) + mod_ref[0, 3:4, :]
    h2_ref[...] = h2.reshape(h2_ref.shape)
    h_hi = h2.astype(BF16)
    h_lo = (h2 - h_hi.astype(F32)).astype(BF16)
    lg = (jnp.dot(h_hi, wr_ref[0], preferred_element_type=F32)
          + jnp.dot(h_hi, wr_ref[1], preferred_element_type=F32)
          + jnp.dot(h_lo, wr_ref[0], preferred_element_type=F32)) + br_ref[...]

    per_grp = n_experts // n_groups
    lane = lax.broadcasted_iota(jnp.int32, lg.shape, 1).astype(F32)
    big = float(lg.shape[1])
    ninf = -jnp.inf
    gl = jnp.where(lane < n_groups, lg, ninf)
    gmax = jnp.max(gl, axis=-1, keepdims=True)
    p_grp = 1.0 / jnp.sum(jnp.exp(gl - gmax), axis=-1, keepdims=True)
    gidx = jnp.min(jnp.where(gl == gmax, lane, big), axis=-1, keepdims=True)
    in_grp = ((lane >= n_groups) & (lane < n_groups + n_experts)
              & (jnp.floor((lane - n_groups) / per_grp) == gidx))
    el = jnp.where(in_grp, lg, ninf)
    emax = jnp.max(el, axis=-1, keepdims=True)
    esum = jnp.sum(jnp.exp(el - emax), axis=-1, keepdims=True)
    i1 = jnp.min(jnp.where(el == emax, lane, big), axis=-1, keepdims=True)
    el2 = jnp.where(lane == i1, ninf, el)
    emax2 = jnp.max(el2, axis=-1, keepdims=True)
    i2 = jnp.min(jnp.where(el2 == emax2, lane, big), axis=-1, keepdims=True)
    p1 = 1.0 / esum
    p2 = jnp.exp(emax2 - emax) / esum
    w1 = p1 / (p1 + p2) * p_grp
    w2 = p2 / (p1 + p2) * p_grp
    eidx_ref[...] = jnp.where(lane == 0.0, i1 - n_groups, jnp.where(lane == 1.0, i2 - n_groups, 0.0)).astype(jnp.int32)
    ew_ref[...] = jnp.where(lane == 0.0, w1, jnp.where(lane == 1.0, w2, 0.0))


def _out_proj(mixes, w_parts, x, mod, g2, w_router, b_router, n_ctx_tiles, tiles_per_req, n_groups, n_experts):
    t = mixes[0].shape[0]
    d = w_parts[0].shape[1]
    tm = TOKEN_TILE
    n_in = len(mixes)
    mrow = functools.partial(_tile_mod_row, n_ctx_tiles=n_ctx_tiles, tiles_per_req=tiles_per_req)
    x_specs, x_args = _token_specs(x, tm, n_ctx_tiles, lambda i: i)
    in_specs = ([pl.BlockSpec((tm, m.shape[1]), lambda i: (i, 0)) for m in mixes]
                + [pl.BlockSpec(w.shape, lambda i: (0, 0)) for w in w_parts]
                + x_specs
                + [pl.BlockSpec((1, 6, d), lambda i: (mrow(i), 0, 0)),
                   pl.BlockSpec((1, d), lambda i: (0, 0)),
                   pl.BlockSpec(w_router.shape, lambda i: (0, 0, 0)),
                   pl.BlockSpec((1, 128), lambda i: (0, 0))])
    return pl.pallas_call(
        functools.partial(_outproj_kernel, n_in=n_in, n_x=len(x_args), n_ctx_tiles=n_ctx_tiles,
                          n_groups=n_groups, n_experts=n_experts),
        grid=(t // tm,),
        in_specs=in_specs,
        out_specs=[pl.BlockSpec((tm, d), lambda i: (i, 0)),
                   pl.BlockSpec((tm, d // 128, 128), lambda i: (i, 0, 0)),
                   pl.BlockSpec((tm, 128), lambda i: (i, 0)),
                   pl.BlockSpec((tm, 128), lambda i: (i, 0))],
        out_shape=[jax.ShapeDtypeStruct((t, d), F32),
                   jax.ShapeDtypeStruct((t, d // 128, 128), F32),
                   jax.ShapeDtypeStruct((t, 128), jnp.int32),
                   jax.ShapeDtypeStruct((t, 128), F32)],
        compiler_params=_cparams("arbitrary"),
        name="out_proj_router",
    )(*mixes, *w_parts, *x_args, mod, g2.reshape(1, d), w_router, b_router)


def _dispatch_plan(e1, e2, n_experts, tile):
    t = e1.shape[0]
    e = jnp.concatenate([e1, e2])
    onehot = (e[:, None] == jnp.arange(n_experts, dtype=jnp.int32)[None, :]).astype(jnp.int32)
    csum = jnp.cumsum(onehot, axis=0)
    rank = jnp.take_along_axis(csum, e[:, None], axis=1)[:, 0] - 1
    counts = csum[-1]
    padded = ((counts + tile - 1) // tile) * tile
    ends = jnp.cumsum(padded)
    starts = ends - padded
    pos = starts[e] + rank
    n_tiles = (2 * t) // tile + n_experts
    n_used = (ends[-1] // tile).astype(jnp.int32)
    tile_start = jnp.arange(n_tiles, dtype=jnp.int32) * tile
    probe = jnp.minimum(tile_start, ends[-1] - 1)
    tile_expert = jnp.sum((ends[None, :] <= probe[:, None]).astype(jnp.int32), axis=1)
    return tile_expert, n_used.reshape(1), ends.astype(jnp.int32), pos[:t], pos[t:], n_tiles


def _row_source_kernel(ends_ref, p1_ref, p2_ref, o_ref, *, n_tok, n_rows, tile, n_experts):
    def clear8(j, c):
        base = lax.rem(j * 8, n_tok)
        for u in range(8):
            o_ref[j * 8 + u] = base + u
        return c

    def per_expert(e, c):
        first = lax.shift_right_logical(jnp.maximum(ends_ref[e] - tile, 0), 3)
        lax.fori_loop(first, first + tile // 8, clear8, 0)
        return c

    lax.fori_loop(0, n_experts, per_expert, 0)
    lax.fori_loop(lax.shift_right_logical(ends_ref[n_experts - 1], 3), n_rows // 8, clear8, 0)

    def place(t, c):
        o_ref[p1_ref[t]] = t
        o_ref[p2_ref[t]] = t
        return c

    lax.fori_loop(0, n_tok, place, 0, unroll=8)


def _row_source(ends, pos1, pos2, n_rows, tile):
    n_tok = pos1.shape[0]
    smem = pl.BlockSpec(memory_space=pltpu.SMEM)
    return pl.pallas_call(
        functools.partial(_row_source_kernel, n_tok=n_tok, n_rows=n_rows, tile=tile, n_experts=ends.shape[0]),
        in_specs=[smem, smem, smem],
        out_specs=smem,
        out_shape=jax.ShapeDtypeStruct((n_rows,), jnp.int32),
        name="moe_row_source",
    )(ends, pos1, pos2)


def _moe_kernel(te_ref, nu_ref, *refs, tm, ahead):
    tok_refs = refs[:ahead + 1]
    h_hbm, wg_ref, wu_ref, wd_ref, o_ref, wg_bf, wu_bf, wd_bf, buf, sem = refs[ahead + 1:]
    i = pl.program_id(0)
    nslot = ahead + 1
    slot = lax.rem(i, nslot)
    n_used = nu_ref[0]

    def issue(tile, tok_ref):
        s = lax.rem(tile, nslot)

        def body(j, c):
            for u in range(2):
                r = 2 * j + u
                pltpu.make_async_copy(h_hbm.at[tok_ref[0, 0, r]], buf.at[s, r], sem.at[s]).start(priority=u)
            return c

        lax.fori_loop(0, tm // 2, body, 0, unroll=4)

    @pl.when(i == 0)
    def _():
        for j in range(ahead):
            @pl.when(j < n_used)
            def _():
                issue(jnp.int32(j), tok_refs[j])

    @pl.when(i + ahead < n_used)
    def _():
        issue(i + ahead, tok_refs[ahead])

    new_expert = jnp.logical_or(i == 0, te_ref[i] != te_ref[jnp.maximum(i - 1, 0)])

    @pl.when(new_expert)
    def _():
        wg_bf[...] = wg_ref[...].astype(BF16)
        wu_bf[...] = wu_ref[...].astype(BF16)
        wd_bf[...] = wd_ref[...].astype(BF16)

    @pl.when(i < n_used)
    def _():
        pltpu.make_async_copy(h_hbm.at[pl.ds(0, tm)], buf.at[slot], sem.at[slot]).wait()
        x = buf[slot].reshape(tm, -1).astype(BF16)
        g = jnp.dot(x, wg_bf[...], preferred_element_type=F32)
        u = jnp.dot(x, wu_bf[...], preferred_element_type=F32)
        act = _silu(g) * u
        y = jnp.dot(act.astype(BF16), wd_bf[...], preferred_element_type=F32)
        o_ref[...] = y.reshape(o_ref.shape)

    @pl.when(i >= n_used)
    def _():
        o_ref[...] = jnp.zeros_like(o_ref)


def _moe_experts(h3, src_tok, tile_expert, n_used, w_gate, w_up, w_down, layer_idx, tm):
    t, sub, lanes = h3.shape
    d = sub * lanes
    n_tiles = src_tok.shape[0] // tm
    f = w_gate.shape[-1]
    ahead = MOE_GATHER_AHEAD
    tok = src_tok.reshape(n_tiles, 1, tm)

    def tok_spec(k):
        return pl.BlockSpec((1, 1, tm), lambda i, te, nu: (jnp.minimum(i + k, n_tiles - 1), 0, 0),
                            memory_space=pltpu.SMEM)

    return pl.pallas_call(
        functools.partial(_moe_kernel, tm=tm, ahead=ahead),
        grid_spec=pltpu.PrefetchScalarGridSpec(
            num_scalar_prefetch=2,
            grid=(n_tiles,),
            in_specs=[tok_spec(k) for k in range(ahead + 1)] + [
                      pl.BlockSpec(memory_space=pl.ANY),
                      pl.BlockSpec((None, None, d, f), lambda i, te, nu: (layer_idx, te[i], 0, 0)),
                      pl.BlockSpec((None, None, d, f), lambda i, te, nu: (layer_idx, te[i], 0, 0)),
                      pl.BlockSpec((None, None, f, d), lambda i, te, nu: (layer_idx, te[i], 0, 0))],
            out_specs=pl.BlockSpec((tm, sub, lanes), lambda i, te, nu: (i, 0, 0)),
            scratch_shapes=[pltpu.VMEM((d, f), BF16), pltpu.VMEM((d, f), BF16), pltpu.VMEM((f, d), BF16),
                            pltpu.VMEM((ahead + 1, tm, sub, lanes), F32),
                            pltpu.SemaphoreType.DMA((ahead + 1,))]),
        out_shape=jax.ShapeDtypeStruct((n_tiles * tm, sub, lanes), F32),
        compiler_params=_cparams("arbitrary"),
        name="moe_experts",
    )(tile_expert, n_used, *([tok] * (ahead + 1)), h3, w_gate, w_up, w_down)


def _combine_kernel(p1_ref, p2_ref, p1n_ref, p2n_ref, y_hbm, x_ref, ew_ref, mod_ref, gf_ref, *rest,
                    tm, n_steps, n_ctx_tiles, final_norm):
    out_refs, (buf, sem) = rest[:-2], rest[-2:]
    i = pl.program_id(0)
    slot = lax.rem(i, 2)

    def issue(s, pa_ref, pb_ref):
        def body(r, c):
            pltpu.make_async_copy(y_hbm.at[pa_ref[0, 0, r]], buf.at[s, 0, r], sem.at[s, 0]).start(priority=0)
            pltpu.make_async_copy(y_hbm.at[pb_ref[0, 0, r]], buf.at[s, 1, r], sem.at[s, 1]).start(priority=1)
            return c

        lax.fori_loop(0, tm, body, 0, unroll=8)

    @pl.when(i == 0)
    def _():
        issue(0, p1_ref, p2_ref)

    @pl.when(i + 1 < n_steps)
    def _():
        issue(1 - slot, p1n_ref, p2n_ref)

    for k in range(2):
        pltpu.make_async_copy(y_hbm.at[pl.ds(0, tm)], buf.at[slot, k], sem.at[slot, k]).wait()
    ew = ew_ref[...]
    y = ew[:, 0:1] * buf[slot, 0].reshape(tm, -1) + ew[:, 1:2] * buf[slot, 1].reshape(tm, -1)
    x = x_ref[...] + mod_ref[0, 5:6, :] * y
    if not final_norm:
        out_refs[0][...] = x
    else:
        x = _rms(x, gf_ref[...])
        ctx_ref, lat_ref = out_refs

        @pl.when(i < n_ctx_tiles)
        def _():
            ctx_ref[...] = x

        @pl.when(i >= n_ctx_tiles)
        def _():
            lat_ref[...] = x


def _moe_combine(y_sorted, pos1, pos2, x, ew, mod, final_g, n_ctx_tiles, tiles_per_req, final_norm):
    t, d = x.shape
    tm = TOKEN_TILE
    n_steps = t // tm
    mrow = functools.partial(_tile_mod_row, n_ctx_tiles=n_ctx_tiles, tiles_per_req=tiles_per_req)
    smem_rows = pl.BlockSpec((1, 1, tm), lambda i: (i, 0, 0), memory_space=pltpu.SMEM)
    smem_next = pl.BlockSpec((1, 1, tm), lambda i: (jnp.minimum(i + 1, n_steps - 1), 0, 0),
                             memory_space=pltpu.SMEM)
    p1 = pos1.reshape(n_steps, 1, tm)
    p2 = pos2.reshape(n_steps, 1, tm)
    if final_norm:
        out_specs = [pl.BlockSpec((tm, d), lambda i: (jnp.minimum(i, n_ctx_tiles - 1), 0)),
                     pl.BlockSpec((tm, d), lambda i: (jnp.maximum(i - n_ctx_tiles, 0), 0))]
        out_shape = [jax.ShapeDtypeStruct((n_ctx_tiles * tm, d), F32),
                     jax.ShapeDtypeStruct((t - n_ctx_tiles * tm, d), F32)]
    else:
        out_specs = pl.BlockSpec((tm, d), lambda i: (i, 0))
        out_shape = jax.ShapeDtypeStruct((t, d), F32)
    return pl.pallas_call(
        functools.partial(_combine_kernel, tm=tm, n_steps=n_steps, n_ctx_tiles=n_ctx_tiles,
                          final_norm=final_norm),
        grid=(n_steps,),
        in_specs=[smem_rows, smem_rows, smem_next, smem_next,
                  pl.BlockSpec(memory_space=pl.ANY),
                  pl.BlockSpec((tm, d), lambda i: (i, 0)),
                  pl.BlockSpec((tm, 128), lambda i: (i, 0)),
                  pl.BlockSpec((1, 6, d), lambda i: (mrow(i), 0, 0)),
                  pl.BlockSpec((1, d), lambda i: (0, 0))],
        out_specs=out_specs,
        out_shape=out_shape,
        scratch_shapes=[pltpu.VMEM((2, 2, tm) + y_sorted.shape[1:], F32), pltpu.SemaphoreType.DMA((2, 2))],
        compiler_params=_cparams("arbitrary"),
        name="moe_combine",
    )(p1, p2, p1, p2, y_sorted, x, ew, mod, final_g.reshape(1, d))


def _axial_angles(seq, head_dim):
    rows = seq // GRID_W
    row = jnp.repeat(jnp.arange(rows), GRID_W).astype(F32)
    col = jnp.tile(jnp.arange(GRID_W), rows).astype(F32)
    n_freq = head_dim // 4
    inv = ROPE_BASE ** (-jnp.arange(n_freq, dtype=F32) / n_freq)
    return jnp.concatenate([row[:, None] * inv, col[:, None] * inv], axis=-1)


def kernel(x_prompt, x_sample, cache_attn_k, cache_attn_v, state_hgrn, state_ret, c, c_ctx, mod_w, mod_b, norm1_g, norm2_g, even_w_in, even_w_out, hgrn_lb_logits, hgrn_norm_g, attn_qn_g, attn_kn_g, odd_w_in, odd_w_out, ret_decay_logit, ret_gn_g, moe_w_group, moe_b_group, moe_w_expert, moe_b_expert, moe_w_gate, moe_w_up, moe_w_down, final_norm_g):
    batch, seq, d = x_prompt.shape
    dec_batch, dec_seq, _ = x_sample.shape
    depth = mod_w.shape[0]
    heads_a, val_a = hgrn_norm_g.shape[1:]
    key_a = hgrn_lb_logits.shape[2] // heads_a
    hd_b = attn_qn_g.shape[1]
    kv_b = cache_attn_k.shape[3]
    heads_b = (even_w_out.shape[1] - heads_a * val_a) // hd_b
    heads_c, val_c = ret_gn_g.shape[1:]
    key_c = state_ret.shape[4]
    n_experts = moe_w_expert.shape[2]
    n_groups = moe_w_group.shape[2]
    tm = TOKEN_TILE
    n_ctx = batch * seq
    n_ctx_tiles = n_ctx // tm
    tiles_per_req = dec_seq // tm
    assert seq % tm == 0 and dec_seq % tm == 0 and n_ctx % dec_seq == 0
    assert key_a == 128 and val_a == 128 and hd_b == 128 and n_groups + n_experts <= 128

    x = (x_prompt.reshape(n_ctx, d), x_sample.reshape(dec_batch * dec_seq, d))
    t = n_ctx + dec_batch * dec_seq

    n_vec = 1 + dec_batch
    rows = -(-n_vec // 8) * 8
    cvecs = jnp.zeros((rows, d), F32).at[0].set(c_ctx).at[1:n_vec].set(c)
    mods = _ada_mod(cvecs, mod_w, mod_b).reshape(depth, rows, 6, d)

    lower_bounds = jnp.cumsum(jax.nn.softmax(hgrn_lb_logits.astype(F32), axis=0), axis=0)
    log_gamma = jax.nn.log_sigmoid(ret_decay_logit.astype(F32))

    ang_b = _axial_angles(dec_seq, hd_b)
    cos_b = jnp.concatenate([jnp.cos(ang_b), jnp.cos(ang_b)], axis=-1)
    sin_b = jnp.concatenate([-jnp.sin(ang_b), jnp.sin(ang_b)], axis=-1)
    ang_c = _axial_angles(dec_seq, key_c)
    cos_c, sin_c = jnp.cos(ang_c), jnp.sin(ang_c)

    pad = 128 - n_groups - n_experts
    new_k = new_v = new_sh = new_sr = None
    assert depth == 2, "one HGRN2/GQA layer followed by one retention layer"
    for l in range(depth):
        mod = mods[l]
        if l % 2 == 0:
            e = l // 2
            w_in = even_w_in[e].astype(BF16)
            proj = _in_proj(x, norm1_g[l], mod, w_in, t, n_ctx_tiles, tiles_per_req, w_in.shape[1])
            lb = lower_bounds[l]
            oa, s_ctx = _hgrn(proj, lb, hgrn_norm_g[e], None, e, None, batch, seq, 0, heads_a, key_a, val_a)
            oa, _ = _hgrn(proj, lb, hgrn_norm_g[e], state_hgrn, e, oa, dec_batch, dec_seq, n_ctx // dec_seq,
                          heads_a, key_a, val_a)
            col0 = (3 * heads_a * key_a + 2 * heads_a * val_a) // 128
            ob, kc, vc = _attn(proj, attn_qn_g[e], attn_kn_g[e], None, e, None, batch, seq, 0,
                               heads_b, kv_b, hd_b, col0)
            (ob,) = _attn(proj, attn_qn_g[e], attn_kn_g[e], (cache_attn_k, cache_attn_v, cos_b, sin_b), e, ob,
                          dec_batch, dec_seq, n_ctx // dec_seq, heads_b, kv_b, hd_b, col0)
            new_k = kc.reshape(batch, 1, seq, kv_b, hd_b)
            new_v = vc.reshape(batch, 1, seq, kv_b, hd_b)
            new_sh = s_ctx[:, None]
            w_out = even_w_out[e].astype(BF16)
            mixes = [oa, ob]
            w_parts = [w_out[:heads_a * val_a], w_out[heads_a * val_a:]]
        else:
            o = l // 2
            w_in = odd_w_in[o].astype(BF16)
            proj = _in_proj(x, norm1_g[l], mod, w_in, t, n_ctx_tiles, tiles_per_req, w_in.shape[1] // 2)
            y, s_ctx = _retention(proj, log_gamma[o], ret_gn_g[o], None, o, None, batch, seq, 0,
                                  heads_c, key_c, val_c)
            y, _ = _retention(proj, log_gamma[o], ret_gn_g[o], (state_ret, cos_c, sin_c), o, y,
                              dec_batch, dec_seq, n_ctx // dec_seq, heads_c, key_c, val_c)
            new_sr = s_ctx[:, None]
            mixes = [y]
            w_parts = [odd_w_out[o].astype(BF16)]

        w_router = jnp.concatenate([moe_w_group[l], moe_w_expert[l], jnp.zeros((d, pad), F32)], axis=1)
        w_router_hi = w_router.astype(BF16)
        w_router = jnp.stack([w_router_hi, (w_router - w_router_hi.astype(F32)).astype(BF16)])
        b_router = jnp.concatenate([moe_b_group[l], moe_b_expert[l], jnp.zeros((pad,), F32)]).reshape(1, 128)
        xn, h2, eidx, ew = _out_proj(mixes, w_parts, x, mod, norm2_g[l], w_router, b_router,
                                     n_ctx_tiles, tiles_per_req, n_groups, n_experts)
        tile_expert, n_used, ends, pos1, pos2, n_tiles = _dispatch_plan(
            eidx[:, 0], eidx[:, 1], n_experts, EXPERT_TILE)
        src_tok = _row_source(ends, pos1, pos2, n_tiles * EXPERT_TILE, EXPERT_TILE)
        y_sorted = _moe_experts(h2, src_tok, tile_expert, n_used, moe_w_gate, moe_w_up, moe_w_down, l, EXPERT_TILE)
        x = _moe_combine(y_sorted, pos1, pos2, xn, ew, mod, final_norm_g, n_ctx_tiles, tiles_per_req,
                         final_norm=(l == depth - 1))

    y_prompt = x[0].reshape(batch, seq, d)
    y_sample = x[1].reshape(dec_batch, dec_seq, d)
    return (y_prompt, y_sample, new_k, new_v, new_sh, new_sr)
```

```python
import functools

import numpy as np
import jax
import jax.numpy as jnp
from jax import lax
from jax.experimental import pallas as pl
from jax.experimental.pallas import tpu as pltpu

F32 = jnp.float32
BF16 = jnp.bfloat16
EPS = 1e-6
ROPE_BASE = 10000.0
GRID_W = 64
N_GROUPS = 4
TOP_K_INNER = 2
HGRN_BLOCK = 128
HGRN_DIAG = 8
RET_CHUNK = 256
TOKEN_TILE = 256
EXPERT_TILE = 256
MOE_GATHER_AHEAD = 2
V7X_VMEM_LIMIT_BYTES = 56 * 1024 * 1024
HIGHEST = lax.Precision.HIGHEST
LOG2_E = 1.4426950408889634

_NT = (((1,), (1,)), ((), ()))


def _cparams(*sem):
    return pltpu.CompilerParams(dimension_semantics=sem, vmem_limit_bytes=V7X_VMEM_LIMIT_BYTES)


def _rms(x, g):
    return x * lax.rsqrt(jnp.mean(x * x, axis=-1, keepdims=True) + EPS) * g


def _silu(x):
    return x * jax.nn.sigmoid(x)


def _bdot(a, b):
    return jnp.dot(a.astype(BF16), b.astype(BF16), preferred_element_type=F32)


def _bdot_nt(a, b):
    return lax.dot_general(a.astype(BF16), b.astype(BF16), _NT, preferred_element_type=F32)


def _mod_kernel(c_ref, w_ref, b_ref, o_ref):
    a = _silu(c_ref[...])
    o_ref[0] = jnp.dot(a, w_ref[0], precision=HIGHEST, preferred_element_type=F32) + b_ref[0]


def _ada_mod(cvecs, mod_w, mod_b):
    depth, d, n = mod_w.shape
    tn = n // 4
    rows = cvecs.shape[0]
    return pl.pallas_call(
        _mod_kernel,
        grid=(depth, n // tn),
        in_specs=[pl.BlockSpec((rows, d), lambda l, j: (0, 0)),
                  pl.BlockSpec((1, d, tn), lambda l, j: (l, 0, j)),
                  pl.BlockSpec((1, 1, tn), lambda l, j: (l, 0, j))],
        out_specs=pl.BlockSpec((1, rows, tn), lambda l, j: (l, 0, j)),
        out_shape=jax.ShapeDtypeStruct((depth, rows, n), F32),
        compiler_params=_cparams("arbitrary", "arbitrary"),
        name="ada_mod",
    )(cvecs, mod_w, mod_b.reshape(depth, 1, n))


def _tile_mod_row(i, n_ctx_tiles, tiles_per_req):
    return jnp.where(i < n_ctx_tiles, 0, 1 + jnp.maximum(i - n_ctx_tiles, 0) // tiles_per_req)


def _token_specs(x, tm, n_ctx_tiles, step_of):
    if not isinstance(x, tuple):
        return [pl.BlockSpec((tm, x.shape[1]), lambda *g: (step_of(*g), 0))], (x,)
    d = x[0].shape[1]
    return ([pl.BlockSpec((tm, d), lambda *g: (jnp.minimum(step_of(*g), n_ctx_tiles - 1), 0)),
             pl.BlockSpec((tm, d), lambda *g: (jnp.maximum(step_of(*g) - n_ctx_tiles, 0), 0))], x)


def _token_rows(x_refs, i, n_ctx_tiles):
    if len(x_refs) == 1:
        return x_refs[0][...]
    return jnp.where(i < n_ctx_tiles, x_refs[0][...], x_refs[1][...])


def _inproj_kernel(*refs, n_x, n_ctx_tiles, n16):
    x_refs = refs[:n_x]
    g_ref, mod_ref, w_ref = refs[n_x:n_x + 3]
    out_refs = refs[n_x + 3:]
    x = _token_rows(x_refs, pl.program_id(1), n_ctx_tiles)
    h = _rms(x, g_ref[...]) * (1.0 + mod_ref[0, 1:2, :]) + mod_ref[0, 0:1, :]
    acc = jnp.dot(h.astype(BF16), w_ref[...], preferred_element_type=F32)
    if len(out_refs) == 1:
        out_refs[0][...] = acc.astype(out_refs[0].dtype)
    else:
        out_refs[0][...] = acc[:, :n16].astype(BF16)
        out_refs[1][...] = acc[:, n16:]


def _in_proj(x, g, mod, w_bf, t, n_ctx_tiles, tiles_per_req, tn, n16):
    d, n = w_bf.shape
    tm = TOKEN_TILE
    mrow = functools.partial(_tile_mod_row, n_ctx_tiles=n_ctx_tiles, tiles_per_req=tiles_per_req)
    x_specs, x_args = _token_specs(x, tm, n_ctx_tiles, lambda j, i: i)
    if n16 == n:
        out_specs = pl.BlockSpec((tm, tn), lambda j, i: (i, j))
        out_shape = jax.ShapeDtypeStruct((t, n), BF16)
    else:
        assert tn == n
        out_specs = [pl.BlockSpec((tm, n16), lambda j, i: (i, 0)),
                     pl.BlockSpec((tm, n - n16), lambda j, i: (i, 0))]
        out_shape = [jax.ShapeDtypeStruct((t, n16), BF16), jax.ShapeDtypeStruct((t, n - n16), F32)]
    return pl.pallas_call(
        functools.partial(_inproj_kernel, n_x=len(x_args), n_ctx_tiles=n_ctx_tiles, n16=n16),
        grid=(n // tn, t // tm),
        in_specs=x_specs + [pl.BlockSpec((1, d), lambda j, i: (0, 0)),
                            pl.BlockSpec((1, 6, d), lambda j, i: (mrow(i), 0, 0)),
                            pl.BlockSpec((d, tn), lambda j, i: (0, j))],
        out_specs=out_specs,
        out_shape=out_shape,
        compiler_params=_cparams("arbitrary", "arbitrary"),
        name="in_proj",
    )(*x_args, g.reshape(1, d), mod, w_bf)


def _hgrn_decay(z, lb, reverse):
    nb = HGRN_BLOCK
    row = lax.broadcasted_iota(jnp.int32, (nb, nb), 0)
    col = lax.broadcasted_iota(jnp.int32, (nb, nb), 1)
    tri = jnp.where((col >= row) if reverse else (col <= row), 1.0, 0.0).astype(BF16)
    logf = jnp.log(lb + (1.0 - lb) * jax.nn.sigmoid(z))
    k = (1.0 - lb) * jax.nn.sigmoid(-z)
    l2 = logf * LOG2_E
    hi = l2.astype(BF16)
    lo = (l2 - hi.astype(F32)).astype(BF16)
    c = jnp.dot(tri, hi, preferred_element_type=F32) + jnp.dot(tri, lo, preferred_element_type=F32)
    return c, k


def _hgrn_block(q, v, st_t, reverse, c_ref, k_ref, ecat):
    nb = HGRN_BLOCK
    row = lax.broadcasted_iota(jnp.int32, (nb, nb), 0)
    col = lax.broadcasted_iota(jnp.int32, (nb, nb), 1)
    c = c_ref[...]
    k = k_ref[...]
    edge = 0 if reverse else nb - 1
    c_edge = c_ref[edge:edge + 1, :]

    o = _bdot_nt(q * jnp.exp2(c), st_t)
    kd = k * jnp.exp2(c_edge - c)
    st_new = st_t * jnp.exp2(c_edge) + _bdot(v.T, kd)

    a = jnp.zeros((nb, nb), F32)
    m = nb // 2
    while m >= HGRN_DIAG:
        pieces = []
        for grp in range(nb // (2 * m)):
            idx = grp * 2 * m + (m if reverse else m - 1)
            pieces.append(jnp.broadcast_to(c_ref[idx:idx + 1, :], (2 * m, nb)))
        bnd = pieces[0] if len(pieces) == 1 else jnp.concatenate(pieces, axis=0)
        e = jnp.exp2(-jnp.abs(c - bnd))
        am = _bdot_nt(q * e, k * e)
        same = (row & ~(2 * m - 1)) == (col & ~(2 * m - 1))
        r_hi = (row & m) != 0
        c_hi = (col & m) != 0
        if reverse:
            mask = same & jnp.logical_not(r_hi) & c_hi
        else:
            mask = same & r_hi & jnp.logical_not(c_hi)
        a = jnp.where(mask, am, a)
        m //= 2

    dg = HGRN_DIAG
    rowi = lax.broadcasted_iota(jnp.int32, (dg, nb), 0)
    cols = []
    for s in range(dg):
        keep = (rowi <= s) if reverse else (rowi >= s)
        tiles = []
        for j in range(nb // dg):
            r = j * dg + s
            qj = q[j * dg:(j + 1) * dg, :]
            cj = c[j * dg:(j + 1) * dg, :]
            cs = jnp.broadcast_to(c_ref[r:r + 1, :], (dg, nb))
            ks = jnp.broadcast_to(k_ref[r:r + 1, :], (dg, nb))
            tiles.append(qj * ks * jnp.exp2(jnp.where(keep, cj - cs, -jnp.inf)))
        cols.append(jnp.concatenate(tiles, axis=0).astype(BF16))
    p = jnp.concatenate(cols, axis=1)
    red = jnp.dot(p, ecat, preferred_element_type=F32)
    a = jnp.where((row >> 3) == (col >> 3), red, a)
    o = o + _bdot(a, v)
    return o, st_new


def _hgrn_kernel(*refs, nblk, has_state, qscale):
    qa_ref, ia_ref, zf_ref, zb_ref, ga_ref, lb_ref, ng_ref, ecat_ref = refs[:8]
    s0_ref = refs[8] if has_state else None
    o_ref, sfin_ref, oacc_ref, c_scr, k_scr = refs[-5:]
    nb = HGRN_BLOCK

    def gates(direction, blk, slot):
        z_ref = zb_ref if direction == 1 else zf_ref
        r0 = pl.multiple_of(blk * nb, nb)
        c, k = _hgrn_decay(z_ref[pl.ds(r0, nb), :], lb_ref[direction:direction + 1, :], direction == 1)
        c_scr[direction, slot] = c
        k_scr[direction, slot] = k

    def block(direction, blk, slot, st_t):
        r0 = pl.multiple_of(blk * nb, nb)
        q = _silu(qa_ref[pl.ds(r0, nb), :].astype(F32)) * qscale
        v = ia_ref[pl.ds(r0, nb), :].astype(F32)
        o, st_new = _hgrn_block(q, v, st_t, direction == 1, c_scr.at[direction, slot],
                                k_scr.at[direction, slot], ecat_ref[...])
        oacc_ref[direction, pl.ds(r0, nb), :] = o
        return st_new

    gates(0, 0, 0)
    gates(1, nblk - 1, 0)

    def body(i, carry):
        st_f, st_b = carry
        slot = lax.rem(i, 2)
        st_f = block(0, i, slot, st_f)
        st_b = block(1, nblk - 1 - i, slot, st_b)
        nxt = jnp.minimum(i + 1, nblk - 1)
        gates(0, nxt, 1 - slot)
        gates(1, nblk - 1 - nxt, 1 - slot)
        return st_f, st_b

    if has_state:
        st0 = (s0_ref[0].T, s0_ref[1].T)
    else:
        zero = jnp.zeros((ia_ref.shape[1], qa_ref.shape[1]), F32)
        st0 = (zero, zero)
    st_f, st_b = lax.fori_loop(0, nblk, body, st0)
    sfin_ref[0] = st_f.T
    sfin_ref[1] = st_b.T

    o = oacc_ref[0] + oacc_ref[1]
    o_ref[...] = (_rms(o, ng_ref[0]) * _silu(ga_ref[...].astype(F32))).astype(o_ref.dtype)


def _hgrn(proj16, proj32, lb, norm_g, s0, layer_idx, out_prev, n_seq, seq, row_blk0, heads, key_d, val_d):
    t = proj16.shape[0]
    has_state = s0 is not None
    hk = heads * key_d // 128
    off_qa, off_ia, off_ga = 0, hk, 2 * hk
    off_zf, off_zb = 0, hk

    def col(off):
        return pl.BlockSpec((seq, 128), lambda b, h: (row_blk0 + b, off + h))

    lane = np.arange(key_d)[None, :]
    srow = np.repeat(np.arange(HGRN_DIAG), key_d)[:, None]
    ecat = jnp.asarray((lane % HGRN_DIAG == srow).astype(np.float32), dtype=BF16)
    in_specs = [col(off_qa), col(off_ia), col(off_zf), col(off_zb), col(off_ga),
                pl.BlockSpec((2, key_d), lambda b, h: (0, h)),
                pl.BlockSpec((1, 1, val_d), lambda b, h: (h, 0, 0)),
                pl.BlockSpec((HGRN_DIAG * key_d, key_d), lambda b, h: (0, 0))]
    args = [proj16, proj16, proj32, proj32, proj16, lb, norm_g.reshape(heads, 1, val_d), ecat]
    if has_state:
        in_specs.append(pl.BlockSpec((None, None, 2, None, key_d, val_d),
                                     lambda b, h: (b, layer_idx, 0, h, 0, 0)))
        args.append(s0)
    aliases = {}
    if out_prev is not None:
        in_specs.append(pl.BlockSpec(memory_space=pl.ANY))
        args.append(out_prev)
        aliases = {len(args) - 1: 0}

    kern = functools.partial(_hgrn_kernel, nblk=seq // HGRN_BLOCK, has_state=has_state,
                             qscale=float(key_d) ** -0.5)
    return pl.pallas_call(
        kern,
        grid=(n_seq, heads),
        in_specs=in_specs,
        out_specs=[pl.BlockSpec((seq, val_d), lambda b, h: (row_blk0 + b, h)),
                   pl.BlockSpec((None, 2, None, key_d, val_d), lambda b, h: (b, 0, h, 0, 0))],
        out_shape=[jax.ShapeDtypeStruct((t, heads * val_d), BF16),
                   jax.ShapeDtypeStruct((n_seq, 2, heads, key_d, val_d), F32)],
        scratch_shapes=[pltpu.VMEM((2, seq, val_d), F32),
                        pltpu.VMEM((2, 2, HGRN_BLOCK, key_d), F32),
                        pltpu.VMEM((2, 2, HGRN_BLOCK, key_d), F32)],
        input_output_aliases=aliases,
        compiler_params=_cparams("arbitrary", "arbitrary"),
        name="hgrn2",
    )(*args)


def _rope(x, cos, sin_signed):
    return x * cos + pltpu.roll(x, x.shape[-1] // 2, 1) * sin_signed


def _attn_kernel(*refs, has_ctx, grp, qblk, scale):
    if has_ctx:
        (q_ref, k_ref, v_ref, qn_ref, kn_ref, ck_ref, cv_ref, cos_ref, sin_ref, _prev, o_ref) = refs
    else:
        (q_ref, k_ref, v_ref, qn_ref, kn_ref, o_ref, ko_ref, vo_ref) = refs
    seq, hd = k_ref.shape
    k = _rms(k_ref[...], kn_ref[...])
    v = v_ref[...]
    if has_ctx:
        k = _rope(k, cos_ref[...], sin_ref[...])
        ck = ck_ref[...].astype(BF16)
        cv = cv_ref[...].astype(BF16)
    else:
        ko_ref[...] = k
        vo_ref[...] = v
    kb = k.astype(BF16)
    vb = v.astype(BF16)

    for g in range(grp):
        def body(i, carry, g=g):
            r0 = pl.multiple_of(i * qblk, qblk)
            q = _rms(q_ref[pl.ds(r0, qblk), g * hd:(g + 1) * hd].astype(F32), qn_ref[...])
            if has_ctx:
                q = _rope(q, cos_ref[pl.ds(r0, qblk), :], sin_ref[pl.ds(r0, qblk), :])
            qb = q.astype(BF16)
            s2 = lax.dot_general(qb, kb, _NT, preferred_element_type=F32) * scale
            m = jnp.max(s2, axis=-1, keepdims=True)
            if has_ctx:
                s1 = lax.dot_general(qb, ck, _NT, preferred_element_type=F32) * scale
                m = jnp.maximum(m, jnp.max(s1, axis=-1, keepdims=True))
                p1 = jnp.exp(s1 - m)
            p2 = jnp.exp(s2 - m)
            den = jnp.sum(p2, axis=-1, keepdims=True)
            acc = jnp.dot(p2.astype(BF16), vb, preferred_element_type=F32)
            if has_ctx:
                den = den + jnp.sum(p1, axis=-1, keepdims=True)
                acc = acc + jnp.dot(p1.astype(BF16), cv, preferred_element_type=F32)
            o_ref[pl.ds(r0, qblk), g * hd:(g + 1) * hd] = (acc / den).astype(o_ref.dtype)
            return carry

        lax.fori_loop(0, seq // qblk, body, 0)


def _attn(proj16, proj32, qn_g, kn_g, ctx, layer_idx, out_prev, n_seq, seq, row_blk0, heads, kv_heads, hd,
          col_q, col_kv):
    t = proj16.shape[0]
    grp = heads // kv_heads
    has_ctx = ctx is not None
    cq = col_q // grp
    ck0 = col_kv
    cv0 = ck0 + kv_heads
    in_specs = [pl.BlockSpec((seq, grp * hd), lambda b, h: (row_blk0 + b, cq + h)),
                pl.BlockSpec((seq, hd), lambda b, h: (row_blk0 + b, ck0 + h)),
                pl.BlockSpec((seq, hd), lambda b, h: (row_blk0 + b, cv0 + h)),
                pl.BlockSpec((1, hd), lambda b, h: (0, 0)),
                pl.BlockSpec((1, hd), lambda b, h: (0, 0))]
    args = [proj16, proj32, proj32, qn_g.reshape(1, hd), kn_g.reshape(1, hd)]
    out_specs = [pl.BlockSpec((seq, grp * hd), lambda b, h: (row_blk0 + b, h))]
    out_shape = [jax.ShapeDtypeStruct((t, heads * hd), BF16)]
    aliases = {}
    if has_ctx:
        cache_k, cache_v, cos, sin_signed = ctx
        nreq, nlay, past = cache_k.shape[:3]
        cache_k = cache_k.reshape(nreq, nlay, past, kv_heads * hd)
        cache_v = cache_v.reshape(nreq, nlay, past, kv_heads * hd)
        spec_c = pl.BlockSpec((None, None, past, hd), lambda b, h: (b, layer_idx, 0, h))
        in_specs += [spec_c, spec_c,
                     pl.BlockSpec((seq, hd), lambda b, h: (0, 0)),
                     pl.BlockSpec((seq, hd), lambda b, h: (0, 0)),
                     pl.BlockSpec(memory_space=pl.ANY)]
        args += [cache_k, cache_v, cos, sin_signed, out_prev]
        aliases = {len(args) - 1: 0}
    else:
        out_specs += [pl.BlockSpec((seq, hd), lambda b, h: (b, h)),
                      pl.BlockSpec((seq, hd), lambda b, h: (b, h))]
        out_shape += [jax.ShapeDtypeStruct((n_seq * seq, kv_heads * hd), F32)] * 2
    return pl.pallas_call(
        functools.partial(_attn_kernel, has_ctx=has_ctx, grp=grp, qblk=min(seq, 256), scale=float(hd) ** -0.5),
        grid=(n_seq, kv_heads),
        in_specs=in_specs,
        out_specs=out_specs,
        out_shape=out_shape,
        input_output_aliases=aliases,
        compiler_params=_cparams("arbitrary", "arbitrary"),
        name="gqa",
    )(*args)


def _ret_kernel(*refs, nchunk, has_ctx, kscale):
    if has_ctx:
        (lg_ref, q_ref, k_ref, v_ref, g_ref, gn_ref, s0_ref, cos_ref, sin_ref, _prev,
         o_ref, sfin_ref, sb_ref) = refs
    else:
        (lg_ref, q_ref, k_ref, v_ref, g_ref, gn_ref, o_ref, sfin_ref, sb_ref) = refs
    cs = RET_CHUNK
    kd_dim = q_ref.shape[1]
    h = pl.program_id(1)
    lgf = lg_ref[0, h]
    lgb = lg_ref[1, h]
    row = lax.broadcasted_iota(jnp.int32, (cs, cs), 0)
    col = lax.broadcasted_iota(jnp.int32, (cs, cs), 1)
    rel = (row - col).astype(F32)
    dmat = (jnp.exp(jnp.where(rel >= 0, rel * lgf, -jnp.inf))
            + jnp.exp(jnp.where(rel <= 0, -rel * lgb, -jnp.inf)))
    j = lax.broadcasted_iota(jnp.int32, (cs, 1), 0).astype(F32)
    one = jnp.ones((1, 1), F32)
    dec_f = jnp.exp(one * (cs * lgf))
    dec_b = jnp.exp(one * (cs * lgb))

    def load_qk(ref, r0, mul):
        x = ref[pl.ds(r0, cs), :].astype(F32) * mul
        if has_ctx:
            half = kd_dim // 2
            x1, x2 = x[:, :half], x[:, half:]
            cos = cos_ref[pl.ds(r0, cs), :]
            sin = sin_ref[pl.ds(r0, cs), :]
            x = jnp.concatenate([x1 * cos - x2 * sin, x2 * cos + x1 * sin], axis=1)
        return x

    def bwd_body(i, st):
        n = nchunk - 1 - i
        r0 = pl.multiple_of(n * cs, cs)
        sb_ref[n] = st
        k = load_qk(k_ref, r0, kscale)
        v = v_ref[pl.ds(r0, cs), :]
        return st * dec_b + _bdot((k * jnp.exp(j * lgb)).T, v)

    st_b0 = s0_ref[1] if has_ctx else jnp.zeros(sfin_ref.shape[1:], F32)
    st_b = lax.fori_loop(0, nchunk, bwd_body, st_b0)
    sfin_ref[1] = st_b

    def fwd_body(n, st):
        r0 = pl.multiple_of(n * cs, cs)
        q = load_qk(q_ref, r0, 1.0)
        k = load_qk(k_ref, r0, kscale)
        v = v_ref[pl.ds(r0, cs), :]
        a = _bdot_nt(q, k) * dmat
        o = _bdot(a, v)
        if has_ctx or nchunk > 1:
            o = o + _bdot(q * jnp.exp((j + 1.0) * lgf), st)
            o = o + _bdot(q * jnp.exp((cs - j) * lgb), sb_ref[n])
        mu = jnp.mean(o, axis=-1, keepdims=True)
        var = jnp.mean(jnp.square(o - mu), axis=-1, keepdims=True)
        y = (o - mu) * lax.rsqrt(var + EPS) * gn_ref[0]
        o_ref[pl.ds(r0, cs), :] = (_silu(g_ref[pl.ds(r0, cs), :].astype(F32)) * y).astype(o_ref.dtype)
        return st * dec_f + _bdot((k * jnp.exp((cs - 1.0 - j) * lgf)).T, v)

    st_f0 = s0_ref[0] if has_ctx else jnp.zeros(sfin_ref.shape[1:], F32)
    sfin_ref[0] = lax.fori_loop(0, nchunk, fwd_body, st_f0)


def _retention(proj, log_gamma, gn_g, ctx, layer_idx, out_prev, n_seq, seq, row_blk0, heads, key_d, val_d):
    t = proj.shape[0]
    has_ctx = ctx is not None
    nq = heads * key_d
    cq, ck, cv, cg = 0, nq // key_d, 2 * nq // val_d, (2 * nq + heads * val_d) // val_d
    in_specs = [pl.BlockSpec(memory_space=pltpu.SMEM),
                pl.BlockSpec((seq, key_d), lambda b, h: (row_blk0 + b, cq + h)),
                pl.BlockSpec((seq, key_d), lambda b, h: (row_blk0 + b, ck + h)),
                pl.BlockSpec((seq, val_d), lambda b, h: (row_blk0 + b, cv + h)),
                pl.BlockSpec((seq, val_d), lambda b, h: (row_blk0 + b, cg + h)),
                pl.BlockSpec((1, 1, val_d), lambda b, h: (h, 0, 0))]
    args = [log_gamma, proj, proj, proj, proj, gn_g.reshape(heads, 1, val_d)]
    aliases = {}
    if has_ctx:
        s0, cos, sin = ctx
        in_specs += [pl.BlockSpec((None, None, 2, None, key_d, val_d), lambda b, h: (b, layer_idx, 0, h, 0, 0)),
                     pl.BlockSpec((seq, key_d // 2), lambda b, h: (0, 0)),
                     pl.BlockSpec((seq, key_d // 2), lambda b, h: (0, 0)),
                     pl.BlockSpec(memory_space=pl.ANY)]
        args += [s0, cos, sin, out_prev]
        aliases = {len(args) - 1: 0}
    nchunk = seq // RET_CHUNK
    return pl.pallas_call(
        functools.partial(_ret_kernel, nchunk=nchunk, has_ctx=has_ctx, kscale=float(key_d) ** -0.5),
        grid=(n_seq, heads),
        in_specs=in_specs,
        out_specs=[pl.BlockSpec((seq, val_d), lambda b, h: (row_blk0 + b, h)),
                   pl.BlockSpec((None, 2, None, key_d, val_d), lambda b, h: (b, 0, h, 0, 0))],
        out_shape=[jax.ShapeDtypeStruct((t, heads * val_d), BF16),
                   jax.ShapeDtypeStruct((n_seq, 2, heads, key_d, val_d), F32)],
        scratch_shapes=[pltpu.VMEM((nchunk, key_d, val_d), F32)],
        input_output_aliases=aliases,
        compiler_params=_cparams("arbitrary", "arbitrary"),
        name="retention",
    )(*args)


def _outproj_kernel(*refs, n_in, n_x, n_ctx_tiles, n_groups, n_experts):
    mix_refs = refs[:n_in]
    w_refs = refs[n_in:2 * n_in]
    x_refs = refs[2 * n_in:2 * n_in + n_x]
    mod_ref, g2_ref, wr_ref, br_ref, xn_ref, h2_ref, eidx_ref, ew_ref = refs[2 * n_in + n_x:]
    acc = _bdot(mix_refs[0][...], w_refs[0][...])
    for mref, wref in zip(mix_refs[1:], w_refs[1:]):
        acc = acc + _bdot(mref[...], wref[...])
    xn = _token_rows(x_refs, pl.program_id(0), n_ctx_tiles) + mod_ref[0, 2:3, :] * acc
    xn_ref[...] = xn
    h2 = _rms(xn, g2_ref[...]) * (1.0 + mod_ref[0, 4:5, :]) + mod_ref[0, 3:4, :]
    h2_ref[...] = h2.reshape(h2_ref.shape)
    h_hi = h2.astype(BF16)
    h_lo = (h2 - h_hi.astype(F32)).astype(BF16)
    lg = (jnp.dot(h_hi, wr_ref[0], preferred_element_type=F32)
          + jnp.dot(h_hi, wr_ref[1], preferred_element_type=F32)
          + jnp.dot(h_lo, wr_ref[0], preferred_element_type=F32)) + br_ref[...]

    per_grp = n_experts // n_groups
    lane = lax.broadcasted_iota(jnp.int32, lg.shape, 1).astype(F32)
    big = float(lg.shape[1])
    ninf = -jnp.inf
    gl = jnp.where(lane < n_groups, lg, ninf)
    gmax = jnp.max(gl, axis=-1, keepdims=True)
    p_grp = 1.0 / jnp.sum(jnp.exp(gl - gmax), axis=-1, keepdims=True)
    gidx = jnp.min(jnp.where(gl == gmax, lane, big), axis=-1, keepdims=True)
    in_grp = ((lane >= n_groups) & (lane < n_groups + n_experts)
              & (jnp.floor((lane - n_groups) / per_grp) == gidx))
    el = jnp.where(in_grp, lg, ninf)
    emax = jnp.max(el, axis=-1, keepdims=True)
    esum = jnp.sum(jnp.exp(el - emax), axis=-1, keepdims=True)
    i1 = jnp.min(jnp.where(el == emax, lane, big), axis=-1, keepdims=True)
    el2 = jnp.where(lane == i1, ninf, el)
    emax2 = jnp.max(el2, axis=-1, keepdims=True)
    i2 = jnp.min(jnp.where(el2 == emax2, lane, big), axis=-1, keepdims=True)
    p1 = 1.0 / esum
    p2 = jnp.exp(emax2 - emax) / esum
    w1 = p1 / (p1 + p2) * p_grp
    w2 = p2 / (p1 + p2) * p_grp
    eidx_ref[...] = jnp.where(lane == 0.0, i1 - n_groups, jnp.where(lane == 1.0, i2 - n_groups, 0.0)).astype(jnp.int32)
    ew_ref[...] = jnp.where(lane == 0.0, w1, jnp.where(lane == 1.0, w2, 0.0))


def _out_proj(mixes, w_parts, x, mod, g2, w_router, b_router, n_ctx_tiles, tiles_per_req, n_groups, n_experts):
    t = mixes[0].shape[0]
    d = w_parts[0].shape[1]
    tm = TOKEN_TILE
    n_in = len(mixes)
    mrow = functools.partial(_tile_mod_row, n_ctx_tiles=n_ctx_tiles, tiles_per_req=tiles_per_req)
    x_specs, x_args = _token_specs(x, tm, n_ctx_tiles, lambda i: i)
    in_specs = ([pl.BlockSpec((tm, m.shape[1]), lambda i: (i, 0)) for m in mixes]
                + [pl.BlockSpec(w.shape, lambda i: (0, 0)) for w in w_parts]
                + x_specs
                + [pl.BlockSpec((1, 6, d), lambda i: (mrow(i), 0, 0)),
                   pl.BlockSpec((1, d), lambda i: (0, 0)),
                   pl.BlockSpec(w_router.shape, lambda i: (0, 0, 0)),
                   pl.BlockSpec((1, 128), lambda i: (0, 0))])
    return pl.pallas_call(
        functools.partial(_outproj_kernel, n_in=n_in, n_x=len(x_args), n_ctx_tiles=n_ctx_tiles,
                          n_groups=n_groups, n_experts=n_experts),
        grid=(t // tm,),
        in_specs=in_specs,
        out_specs=[pl.BlockSpec((tm, d), lambda i: (i, 0)),
                   pl.BlockSpec((tm, d // 128, 128), lambda i: (i, 0, 0)),
                   pl.BlockSpec((tm, 128), lambda i: (i, 0)),
                   pl.BlockSpec((tm, 128), lambda i: (i, 0))],
        out_shape=[jax.ShapeDtypeStruct((t, d), F32),
                   jax.ShapeDtypeStruct((t, d // 128, 128), F32),
                   jax.ShapeDtypeStruct((t, 128), jnp.int32),
                   jax.ShapeDtypeStruct((t, 128), F32)],
        compiler_params=_cparams("arbitrary"),
        name="out_proj_router",
    )(*mixes, *w_parts, *x_args, mod, g2.reshape(1, d), w_router, b_router)


def _dispatch_plan(e1, e2, n_experts, tile):
    t = e1.shape[0]
    e = jnp.concatenate([e1, e2])
    onehot = (e[:, None] == jnp.arange(n_experts, dtype=jnp.int32)[None, :]).astype(jnp.int32)
    csum = jnp.cumsum(onehot, axis=0)
    rank = jnp.take_along_axis(csum, e[:, None], axis=1)[:, 0] - 1
    counts = csum[-1]
    padded = ((counts + tile - 1) // tile) * tile
    ends = jnp.cumsum(padded)
    starts = ends - padded
    pos = starts[e] + rank
    n_tiles = (2 * t) // tile + n_experts
    n_used = (ends[-1] // tile).astype(jnp.int32)
    tile_start = jnp.arange(n_tiles, dtype=jnp.int32) * tile
    probe = jnp.minimum(tile_start, ends[-1] - 1)
    tile_expert = jnp.sum((ends[None, :] <= probe[:, None]).astype(jnp.int32), axis=1)
    return tile_expert, n_used.reshape(1), ends.astype(jnp.int32), pos[:t], pos[t:], n_tiles


def _row_source_kernel(ends_ref, p1_ref, p2_ref, o_ref, *, n_tok, n_rows, tile, n_experts):
    def clear8(j, c):
        base = lax.rem(j * 8, n_tok)
        for u in range(8):
            o_ref[j * 8 + u] = base + u
        return c

    def per_expert(e, c):
        first = lax.shift_right_logical(jnp.maximum(ends_ref[e] - tile, 0), 3)
        lax.fori_loop(first, first + tile // 8, clear8, 0)
        return c

    lax.fori_loop(0, n_experts, per_expert, 0)
    lax.fori_loop(lax.shift_right_logical(ends_ref[n_experts - 1], 3), n_rows // 8, clear8, 0)

    def place(t, c):
        o_ref[p1_ref[t]] = t
        o_ref[p2_ref[t]] = t
        return c

    lax.fori_loop(0, n_tok, place, 0, unroll=8)


def _row_source(ends, pos1, pos2, n_rows, tile):
    n_tok = pos1.shape[0]
    smem = pl.BlockSpec(memory_space=pltpu.SMEM)
    return pl.pallas_call(
        functools.partial(_row_source_kernel, n_tok=n_tok, n_rows=n_rows, tile=tile, n_experts=ends.shape[0]),
        in_specs=[smem, smem, smem],
        out_specs=smem,
        out_shape=jax.ShapeDtypeStruct((n_rows,), jnp.int32),
        name="moe_row_source",
    )(ends, pos1, pos2)


def _moe_kernel(te_ref, nu_ref, *refs, tm, ahead):
    tok_refs = refs[:ahead + 1]
    h_hbm, wg_ref, wu_ref, wd_ref, o_ref, wg_bf, wu_bf, wd_bf, buf, sem = refs[ahead + 1:]
    i = pl.program_id(0)
    nslot = ahead + 1
    slot = lax.rem(i, nslot)
    n_used = nu_ref[0]

    def issue(tile, tok_ref):
        s = lax.rem(tile, nslot)

        def body(j, c):
            for u in range(2):
                r = 2 * j + u
                pltpu.make_async_copy(h_hbm.at[tok_ref[0, 0, r]], buf.at[s, r], sem.at[s]).start(priority=u)
            return c

        lax.fori_loop(0, tm // 2, body, 0, unroll=4)

    @pl.when(i == 0)
    def _():
        for j in range(ahead):
            @pl.when(j < n_used)
            def _():
                issue(jnp.int32(j), tok_refs[j])

    @pl.when(i + ahead < n_used)
    def _():
        issue(i + ahead, tok_refs[ahead])

    new_expert = jnp.logical_or(i == 0, te_ref[i] != te_ref[jnp.maximum(i - 1, 0)])

    @pl.when(new_expert)
    def _():
        wg_bf[...] = wg_ref[...].astype(BF16)
        wu_bf[...] = wu_ref[...].astype(BF16)
        wd_bf[...] = wd_ref[...].astype(BF16)

    @pl.when(i < n_used)
    def _():
        pltpu.make_async_copy(h_hbm.at[pl.ds(0, tm)], buf.at[slot], sem.at[slot]).wait()
        x = buf[slot].reshape(tm, -1).astype(BF16)
        g = jnp.dot(x, wg_bf[...], preferred_element_type=F32)
        u = jnp.dot(x, wu_bf[...], preferred_element_type=F32)
        act = _silu(g) * u
        y = jnp.dot(act.astype(BF16), wd_bf[...], preferred_element_type=F32)
        o_ref[...] = y.reshape(o_ref.shape)

    @pl.when(i >= n_used)
    def _():
        o_ref[...] = jnp.zeros_like(o_ref)


def _moe_experts(h3, src_tok, tile_expert, n_used, w_gate, w_up, w_down, layer_idx, tm):
    t, sub, lanes = h3.shape
    d = sub * lanes
    n_tiles = src_tok.shape[0] // tm
    f = w_gate.shape[-1]
    ahead = MOE_GATHER_AHEAD
    tok = src_tok.reshape(n_tiles, 1, tm)

    def tok_spec(k):
        return pl.BlockSpec((1, 1, tm), lambda i, te, nu: (jnp.minimum(i + k, n_tiles - 1), 0, 0),
                            memory_space=pltpu.SMEM)

    return pl.pallas_call(
        functools.partial(_moe_kernel, tm=tm, ahead=ahead),
        grid_spec=pltpu.PrefetchScalarGridSpec(
            num_scalar_prefetch=2,
            grid=(n_tiles,),
            in_specs=[tok_spec(k) for k in range(ahead + 1)] + [
                      pl.BlockSpec(memory_space=pl.ANY),
                      pl.BlockSpec((None, None, d, f), lambda i, te, nu: (layer_idx, te[i], 0, 0)),
                      pl.BlockSpec((None, None, d, f), lambda i, te, nu: (layer_idx, te[i], 0, 0)),
                      pl.BlockSpec((None, None, f, d), lambda i, te, nu: (layer_idx, te[i], 0, 0))],
            out_specs=pl.BlockSpec((tm, sub, lanes), lambda i, te, nu: (i, 0, 0)),
            scratch_shapes=[pltpu.VMEM((d, f), BF16), pltpu.VMEM((d, f), BF16), pltpu.VMEM((f, d), BF16),
                            pltpu.VMEM((ahead + 1, tm, sub, lanes), F32),
                            pltpu.SemaphoreType.DMA((ahead + 1,))]),
        out_shape=jax.ShapeDtypeStruct((n_tiles * tm, sub, lanes), F32),
        compiler_params=_cparams("arbitrary"),
        name="moe_experts",
    )(tile_expert, n_used, *([tok] * (ahead + 1)), h3, w_gate, w_up, w_down)


def _combine_kernel(p1_ref, p2_ref, p1n_ref, p2n_ref, y_hbm, x_ref, ew_ref, mod_ref, gf_ref, *rest,
                    tm, n_steps, n_ctx_tiles, final_norm):
    out_refs, (buf, sem) = rest[:-2], rest[-2:]
    i = pl.program_id(0)
    slot = lax.rem(i, 2)

    def issue(s, pa_ref, pb_ref):
        def body(r, c):
            pltpu.make_async_copy(y_hbm.at[pa_ref[0, 0, r]], buf.at[s, 0, r], sem.at[s, 0]).start(priority=0)
            pltpu.make_async_copy(y_hbm.at[pb_ref[0, 0, r]], buf.at[s, 1, r], sem.at[s, 1]).start(priority=1)
            return c

        lax.fori_loop(0, tm, body, 0, unroll=8)

    @pl.when(i == 0)
    def _():
        issue(0, p1_ref, p2_ref)

    @pl.when(i + 1 < n_steps)
    def _():
        issue(1 - slot, p1n_ref, p2n_ref)

    for k in range(2):
        pltpu.make_async_copy(y_hbm.at[pl.ds(0, tm)], buf.at[slot, k], sem.at[slot, k]).wait()
    ew = ew_ref[...]
    y = ew[:, 0:1] * buf[slot, 0].reshape(tm, -1) + ew[:, 1:2] * buf[slot, 1].reshape(tm, -1)
    x = x_ref[...] + mod_ref[0, 5:6, :] * y
    if not final_norm:
        out_refs[0][...] = x
    else:
        x = _rms(x, gf_ref[...])
        ctx_ref, lat_ref = out_refs

        @pl.when(i < n_ctx_tiles)
        def _():
            ctx_ref[...] = x

        @pl.when(i >= n_ctx_tiles)
        def _():
            lat_ref[...] = x


def _moe_combine(y_sorted, pos1, pos2, x, ew, mod, final_g, n_ctx_tiles, tiles_per_req, final_norm):
    t, d = x.shape
    tm = TOKEN_TILE
    n_steps = t // tm
    mrow = functools.partial(_tile_mod_row, n_ctx_tiles=n_ctx_tiles, tiles_per_req=tiles_per_req)
    smem_rows = pl.BlockSpec((1, 1, tm), lambda i: (i, 0, 0), memory_space=pltpu.SMEM)
    smem_next = pl.BlockSpec((1, 1, tm), lambda i: (jnp.minimum(i + 1, n_steps - 1), 0, 0),
                             memory_space=pltpu.SMEM)
    p1 = pos1.reshape(n_steps, 1, tm)
    p2 = pos2.reshape(n_steps, 1, tm)
    if final_norm:
        out_specs = [pl.BlockSpec((tm, d), lambda i: (jnp.minimum(i, n_ctx_tiles - 1), 0)),
                     pl.BlockSpec((tm, d), lambda i: (jnp.maximum(i - n_ctx_tiles, 0), 0))]
        out_shape = [jax.ShapeDtypeStruct((n_ctx_tiles * tm, d), F32),
                     jax.ShapeDtypeStruct((t - n_ctx_tiles * tm, d), F32)]
    else:
        out_specs = pl.BlockSpec((tm, d), lambda i: (i, 0))
        out_shape = jax.ShapeDtypeStruct((t, d), F32)
    return pl.pallas_call(
        functools.partial(_combine_kernel, tm=tm, n_steps=n_steps, n_ctx_tiles=n_ctx_tiles,
                          final_norm=final_norm),
        grid=(n_steps,),
        in_specs=[smem_rows, smem_rows, smem_next, smem_next,
                  pl.BlockSpec(memory_space=pl.ANY),
                  pl.BlockSpec((tm, d), lambda i: (i, 0)),
                  pl.BlockSpec((tm, 128), lambda i: (i, 0)),
                  pl.BlockSpec((1, 6, d), lambda i: (mrow(i), 0, 0)),
                  pl.BlockSpec((1, d), lambda i: (0, 0))],
        out_specs=out_specs,
        out_shape=out_shape,
        scratch_shapes=[pltpu.VMEM((2, 2, tm) + y_sorted.shape[1:], F32), pltpu.SemaphoreType.DMA((2, 2))],
        compiler_params=_cparams("arbitrary"),
        name="moe_combine",
    )(p1, p2, p1, p2, y_sorted, x, ew, mod, final_g.reshape(1, d))


def _axial_angles(seq, head_dim):
    rows = seq // GRID_W
    row = jnp.repeat(jnp.arange(rows), GRID_W).astype(F32)
    col = jnp.tile(jnp.arange(GRID_W), rows).astype(F32)
    n_freq = head_dim // 4
    inv = ROPE_BASE ** (-jnp.arange(n_freq, dtype=F32) / n_freq)
    return jnp.concatenate([row[:, None] * inv, col[:, None] * inv], axis=-1)


def kernel(x_prompt, x_sample, cache_attn_k, cache_attn_v, state_hgrn, state_ret, c, c_ctx, mod_w, mod_b, norm1_g, norm2_g, even_w_in, even_w_out, hgrn_lb_logits, hgrn_norm_g, attn_qn_g, attn_kn_g, odd_w_in, odd_w_out, ret_decay_logit, ret_gn_g, moe_w_group, moe_b_group, moe_w_expert, moe_b_expert, moe_w_gate, moe_w_up, moe_w_down, final_norm_g):
    batch, seq, d = x_prompt.shape
    dec_batch, dec_seq, _ = x_sample.shape
    depth = mod_w.shape[0]
    heads_a, val_a = hgrn_norm_g.shape[1:]
    key_a = hgrn_lb_logits.shape[2] // heads_a
    hd_b = attn_qn_g.shape[1]
    kv_b = cache_attn_k.shape[3]
    heads_b = (even_w_out.shape[1] - heads_a * val_a) // hd_b
    heads_c, val_c = ret_gn_g.shape[1:]
    key_c = state_ret.shape[4]
    n_experts = moe_w_expert.shape[2]
    n_groups = moe_w_group.shape[2]
    tm = TOKEN_TILE
    n_ctx = batch * seq
    n_ctx_tiles = n_ctx // tm
    tiles_per_req = dec_seq // tm
    assert seq % tm == 0 and dec_seq % tm == 0 and n_ctx % dec_seq == 0
    assert key_a == 128 and val_a == 128 and hd_b == 128 and n_groups + n_experts <= 128

    x = (x_prompt.reshape(n_ctx, d), x_sample.reshape(dec_batch * dec_seq, d))
    t = n_ctx + dec_batch * dec_seq

    n_vec = 1 + dec_batch
    rows = -(-n_vec // 8) * 8
    cvecs = jnp.zeros((rows, d), F32).at[0].set(c_ctx).at[1:n_vec].set(c)
    mods = _ada_mod(cvecs, mod_w, mod_b).reshape(depth, rows, 6, d)

    lower_bounds = jnp.cumsum(jax.nn.softmax(hgrn_lb_logits.astype(F32), axis=0), axis=0)
    log_gamma = jax.nn.log_sigmoid(ret_decay_logit.astype(F32))

    ang_b = _axial_angles(dec_seq, hd_b)
    cos_b = jnp.concatenate([jnp.cos(ang_b), jnp.cos(ang_b)], axis=-1)
    sin_b = jnp.concatenate([-jnp.sin(ang_b), jnp.sin(ang_b)], axis=-1)
    ang_c = _axial_angles(dec_seq, key_c)
    cos_c, sin_c = jnp.cos(ang_c), jnp.sin(ang_c)

    pad = 128 - n_groups - n_experts
    new_k = new_v = new_sh = new_sr = None
    assert depth == 2, "one HGRN2/GQA layer followed by one retention layer"
    for l in range(depth):
        mod = mods[l]
        if l % 2 == 0:
            e = l // 2
            na, nva, nqb, nkv = heads_a * key_a, heads_a * val_a, heads_b * hd_b, kv_b * hd_b
            o_ia, o_zf, o_zb, o_ga, o_qb = na, na + nva, 2 * na + nva, 3 * na + nva, 3 * na + 2 * nva
            order = np.concatenate([np.arange(0, na), np.arange(o_ia, o_ia + nva), np.arange(o_ga, o_ga + nva),
                                    np.arange(o_qb, o_qb + nqb), np.arange(o_zf, o_zf + 2 * na),
                                    np.arange(o_qb + nqb, o_qb + nqb + 2 * nkv)])
            n16 = na + 2 * nva + nqb
            w_in = even_w_in[e][:, order].astype(BF16)
            proj16, proj32 = _in_proj(x, norm1_g[l], mod, w_in, t, n_ctx_tiles, tiles_per_req, w_in.shape[1], n16)
            lb = lower_bounds[l]
            oa, s_ctx = _hgrn(proj16, proj32, lb, hgrn_norm_g[e], None, e, None, batch, seq, 0,
                              heads_a, key_a, val_a)
            oa, _ = _hgrn(proj16, proj32, lb, hgrn_norm_g[e], state_hgrn, e, oa, dec_batch, dec_seq,
                          n_ctx // dec_seq, heads_a, key_a, val_a)
            col_q, col_kv = (na + 2 * nva) // hd_b, (2 * na) // hd_b
            ob, kc, vc = _attn(proj16, proj32, attn_qn_g[e], attn_kn_g[e], None, e, None, batch, seq, 0,
                               heads_b, kv_b, hd_b, col_q, col_kv)
            (ob,) = _attn(proj16, proj32, attn_qn_g[e], attn_kn_g[e],
                          (cache_attn_k, cache_attn_v, cos_b, sin_b), e, ob,
                          dec_batch, dec_seq, n_ctx // dec_seq, heads_b, kv_b, hd_b, col_q, col_kv)
            new_k = kc.reshape(batch, 1, seq, kv_b, hd_b)
            new_v = vc.reshape(batch, 1, seq, kv_b, hd_b)
            new_sh = s_ctx[:, None]
            w_out = even_w_out[e].astype(BF16)
            mixes = [oa, ob]
            w_parts = [w_out[:heads_a * val_a], w_out[heads_a * val_a:]]
        else:
            o = l // 2
            w_in = odd_w_in[o].astype(BF16)
            proj = _in_proj(x, norm1_g[l], mod, w_in, t, n_ctx_tiles, tiles_per_req, w_in.shape[1] // 2,
                            w_in.shape[1])
            y, s_ctx = _retention(proj, log_gamma[o], ret_gn_g[o], None, o, None, batch, seq, 0,
                                  heads_c, key_c, val_c)
            y, _ = _retention(proj, log_gamma[o], ret_gn_g[o], (state_ret, cos_c, sin_c), o, y,
                              dec_batch, dec_seq, n_ctx // dec_seq, heads_c, key_c, val_c)
            new_sr = s_ctx[:, None]
            mixes = [y]
            w_parts = [odd_w_out[o].astype(BF16)]

        w_router = jnp.concatenate([moe_w_group[l], moe_w_expert[l], jnp.zeros((d, pad), F32)], axis=1)
        w_router_hi = w_router.astype(BF16)
        w_router = jnp.stack([w_router_hi, (w_router - w_router_hi.astype(F32)).astype(BF16)])
        b_router = jnp.concatenate([moe_b_group[l], moe_b_expert[l], jnp.zeros((pad,), F32)]).reshape(1, 128)
        xn, h2, eidx, ew = _out_proj(mixes, w_parts, x, mod, norm2_g[l], w_router, b_router,
                                     n_ctx_tiles, tiles_per_req, n_groups, n_experts)
        tile_expert, n_used, ends, pos1, pos2, n_tiles = _dispatch_plan(
            eidx[:, 0], eidx[:, 1], n_experts, EXPERT_TILE)
        src_tok = _row_source(ends, pos1, pos2, n_tiles * EXPERT_TILE, EXPERT_TILE)
        y_sorted = _moe_experts(h2, src_tok, tile_expert, n_used, moe_w_gate, moe_w_up, moe_w_down, l, EXPERT_TILE)
        x = _moe_combine(y_sorted, pos1, pos2, xn, ew, mod, final_norm_g, n_ctx_tiles, tiles_per_req,
                         final_norm=(l == depth - 1))

    y_prompt = x[0].reshape(batch, seq, d)
    y_sample = x[1].reshape(dec_batch, dec_seq, d)
    return (y_prompt, y_sample, new_k, new_v, new_sh, new_sr)
```

```python
import functools

import numpy as np
import jax
import jax.numpy as jnp
from jax import lax
from jax.experimental import pallas as pl
from jax.experimental.pallas import tpu as pltpu

F32 = jnp.float32
BF16 = jnp.bfloat16
EPS = 1e-6
ROPE_BASE = 10000.0
GRID_W = 64
N_GROUPS = 4
TOP_K_INNER = 2
HGRN_BLOCK = 128
HGRN_DIAG = 8
RET_CHUNK = 256
TOKEN_TILE = 256
EXPERT_TILE = 256
MOE_GATHER_AHEAD = 2
ROW_SOURCE_CHUNK = 512
V7X_VMEM_LIMIT_BYTES = 56 * 1024 * 1024
HIGHEST = lax.Precision.HIGHEST
LOG2_E = 1.4426950408889634

_NT = (((1,), (1,)), ((), ()))


def _cparams(*sem):
    return pltpu.CompilerParams(dimension_semantics=sem, vmem_limit_bytes=V7X_VMEM_LIMIT_BYTES)


def _rms(x, g):
    return x * lax.rsqrt(jnp.mean(x * x, axis=-1, keepdims=True) + EPS) * g


def _silu(x):
    return x * jax.nn.sigmoid(x)


def _bdot(a, b):
    return jnp.dot(a.astype(BF16), b.astype(BF16), preferred_element_type=F32)


def _bdot_nt(a, b):
    return lax.dot_general(a.astype(BF16), b.astype(BF16), _NT, preferred_element_type=F32)


def _mod_kernel(c_ref, w_ref, b_ref, o_ref):
    a = _silu(c_ref[...])
    o_ref[0] = jnp.dot(a, w_ref[0], precision=HIGHEST, preferred_element_type=F32) + b_ref[0]


def _ada_mod(cvecs, mod_w, mod_b):
    depth, d, n = mod_w.shape
    tn = n // 4
    rows = cvecs.shape[0]
    return pl.pallas_call(
        _mod_kernel,
        grid=(depth, n // tn),
        in_specs=[pl.BlockSpec((rows, d), lambda l, j: (0, 0)),
                  pl.BlockSpec((1, d, tn), lambda l, j: (l, 0, j)),
                  pl.BlockSpec((1, 1, tn), lambda l, j: (l, 0, j))],
        out_specs=pl.BlockSpec((1, rows, tn), lambda l, j: (l, 0, j)),
        out_shape=jax.ShapeDtypeStruct((depth, rows, n), F32),
        compiler_params=_cparams("arbitrary", "arbitrary"),
        name="ada_mod",
    )(cvecs, mod_w, mod_b.reshape(depth, 1, n))


def _tile_mod_row(i, n_ctx_tiles, tiles_per_req):
    return jnp.where(i < n_ctx_tiles, 0, 1 + jnp.maximum(i - n_ctx_tiles, 0) // tiles_per_req)


def _token_specs(x, tm, n_ctx_tiles, step_of):
    if not isinstance(x, tuple):
        return [pl.BlockSpec((tm, x.shape[1]), lambda *g: (step_of(*g), 0))], (x,)
    d = x[0].shape[1]
    return ([pl.BlockSpec((tm, d), lambda *g: (jnp.minimum(step_of(*g), n_ctx_tiles - 1), 0)),
             pl.BlockSpec((tm, d), lambda *g: (jnp.maximum(step_of(*g) - n_ctx_tiles, 0), 0))], x)


def _token_rows(x_refs, i, n_ctx_tiles):
    if len(x_refs) == 1:
        return x_refs[0][...]
    return jnp.where(i < n_ctx_tiles, x_refs[0][...], x_refs[1][...])


def _inproj_kernel(*refs, n_x, n_ctx_tiles, f32_cols):
    x_refs = refs[:n_x]
    g_ref, mod_ref, w_ref = refs[n_x:n_x + 3]
    out_refs = refs[n_x + 3:]
    x = _token_rows(x_refs, pl.program_id(1), n_ctx_tiles)
    h = _rms(x, g_ref[...]) * (1.0 + mod_ref[0, 1:2, :]) + mod_ref[0, 0:1, :]
    acc = jnp.dot(h.astype(BF16), w_ref[...], preferred_element_type=F32)
    if not f32_cols:
        out_refs[0][...] = acc.astype(BF16)
        return
    n = acc.shape[1]
    at16 = at32 = start = 0
    for lo, hi in tuple(f32_cols) + ((n, n),):
        if lo > start:
            out_refs[0][:, at16:at16 + lo - start] = acc[:, start:lo].astype(BF16)
            at16 += lo - start
        if hi > lo:
            out_refs[1][:, at32:at32 + hi - lo] = acc[:, lo:hi]
            at32 += hi - lo
        start = hi


def _in_proj(x, g, mod, w_bf, t, n_ctx_tiles, tiles_per_req, tn, f32_cols=()):
    d, n = w_bf.shape
    tm = TOKEN_TILE
    mrow = functools.partial(_tile_mod_row, n_ctx_tiles=n_ctx_tiles, tiles_per_req=tiles_per_req)
    x_specs, x_args = _token_specs(x, tm, n_ctx_tiles, lambda j, i: i)
    if not f32_cols:
        out_specs = pl.BlockSpec((tm, tn), lambda j, i: (i, j))
        out_shape = jax.ShapeDtypeStruct((t, n), BF16)
    else:
        assert tn == n
        n32 = sum(hi - lo for lo, hi in f32_cols)
        out_specs = [pl.BlockSpec((tm, n - n32), lambda j, i: (i, 0)),
                     pl.BlockSpec((tm, n32), lambda j, i: (i, 0))]
        out_shape = [jax.ShapeDtypeStruct((t, n - n32), BF16), jax.ShapeDtypeStruct((t, n32), F32)]
    return pl.pallas_call(
        functools.partial(_inproj_kernel, n_x=len(x_args), n_ctx_tiles=n_ctx_tiles, f32_cols=tuple(f32_cols)),
        grid=(n // tn, t // tm),
        in_specs=x_specs + [pl.BlockSpec((1, d), lambda j, i: (0, 0)),
                            pl.BlockSpec((1, 6, d), lambda j, i: (mrow(i), 0, 0)),
                            pl.BlockSpec((d, tn), lambda j, i: (0, j))],
        out_specs=out_specs,
        out_shape=out_shape,
        compiler_params=_cparams("arbitrary", "arbitrary"),
        name="in_proj",
    )(*x_args, g.reshape(1, d), mod, w_bf)


def _hgrn_decay(z, lb, reverse):
    nb = HGRN_BLOCK
    row = lax.broadcasted_iota(jnp.int32, (nb, nb), 0)
    col = lax.broadcasted_iota(jnp.int32, (nb, nb), 1)
    tri = jnp.where((col >= row) if reverse else (col <= row), 1.0, 0.0).astype(BF16)
    logf = jnp.log(lb + (1.0 - lb) * jax.nn.sigmoid(z))
    k = (1.0 - lb) * jax.nn.sigmoid(-z)
    l2 = logf * LOG2_E
    hi = l2.astype(BF16)
    lo = (l2 - hi.astype(F32)).astype(BF16)
    c = jnp.dot(tri, hi, preferred_element_type=F32) + jnp.dot(tri, lo, preferred_element_type=F32)
    return c, k


def _hgrn_block(q, v, st_t, reverse, c_ref, k_ref, ecat):
    nb = HGRN_BLOCK
    row = lax.broadcasted_iota(jnp.int32, (nb, nb), 0)
    col = lax.broadcasted_iota(jnp.int32, (nb, nb), 1)
    c = c_ref[...]
    k = k_ref[...]
    edge = 0 if reverse else nb - 1
    c_edge = c_ref[edge:edge + 1, :]

    o = _bdot_nt(q * jnp.exp2(c), st_t)
    kd = k * jnp.exp2(c_edge - c)
    st_new = st_t * jnp.exp2(c_edge) + _bdot(v.T, kd)

    a = jnp.zeros((nb, nb), F32)
    m = nb // 2
    while m >= HGRN_DIAG:
        pieces = []
        for grp in range(nb // (2 * m)):
            idx = grp * 2 * m + (m if reverse else m - 1)
            pieces.append(jnp.broadcast_to(c_ref[idx:idx + 1, :], (2 * m, nb)))
        bnd = pieces[0] if len(pieces) == 1 else jnp.concatenate(pieces, axis=0)
        e = jnp.exp2(-jnp.abs(c - bnd))
        am = _bdot_nt(q * e, k * e)
        same = (row & ~(2 * m - 1)) == (col & ~(2 * m - 1))
        r_hi = (row & m) != 0
        c_hi = (col & m) != 0
        if reverse:
            mask = same & jnp.logical_not(r_hi) & c_hi
        else:
            mask = same & r_hi & jnp.logical_not(c_hi)
        a = jnp.where(mask, am, a)
        m //= 2

    dg = HGRN_DIAG
    rowi = lax.broadcasted_iota(jnp.int32, (dg, nb), 0)
    cols = []
    for s in range(dg):
        keep = (rowi <= s) if reverse else (rowi >= s)
        tiles = []
        for j in range(nb // dg):
            r = j * dg + s
            qj = q[j * dg:(j + 1) * dg, :]
            cj = c[j * dg:(j + 1) * dg, :]
            cs = jnp.broadcast_to(c_ref[r:r + 1, :], (dg, nb))
            ks = jnp.broadcast_to(k_ref[r:r + 1, :], (dg, nb))
            tiles.append(qj * ks * jnp.exp2(jnp.where(keep, cj - cs, -jnp.inf)))
        cols.append(jnp.concatenate(tiles, axis=0).astype(BF16))
    p = jnp.concatenate(cols, axis=1)
    red = jnp.dot(p, ecat, preferred_element_type=F32)
    a = jnp.where((row >> 3) == (col >> 3), red, a)
    o = o + _bdot(a, v)
    return o, st_new


def _hgrn_kernel(*refs, nblk, has_state, qscale):
    qa_ref, ia_ref, zf_ref, zb_ref, ga_ref, lb_ref, ng_ref, ecat_ref = refs[:8]
    s0_ref = refs[8] if has_state else None
    o_ref, sfin_ref, oacc_ref, c_scr, k_scr = refs[-5:]
    nb = HGRN_BLOCK

    def gates(direction, blk, slot):
        z_ref = zb_ref if direction == 1 else zf_ref
        r0 = pl.multiple_of(blk * nb, nb)
        c, k = _hgrn_decay(z_ref[pl.ds(r0, nb), :], lb_ref[direction:direction + 1, :], direction == 1)
        c_scr[direction, slot] = c
        k_scr[direction, slot] = k

    def block(direction, blk, slot, st_t):
        r0 = pl.multiple_of(blk * nb, nb)
        q = _silu(qa_ref[pl.ds(r0, nb), :].astype(F32)) * qscale
        v = ia_ref[pl.ds(r0, nb), :].astype(F32)
        o, st_new = _hgrn_block(q, v, st_t, direction == 1, c_scr.at[direction, slot],
                                k_scr.at[direction, slot], ecat_ref[...])
        oacc_ref[direction, pl.ds(r0, nb), :] = o
        return st_new

    gates(0, 0, 0)
    gates(1, nblk - 1, 0)

    def body(i, carry):
        st_f, st_b = carry
        slot = lax.rem(i, 2)
        st_f = block(0, i, slot, st_f)
        st_b = block(1, nblk - 1 - i, slot, st_b)
        nxt = jnp.minimum(i + 1, nblk - 1)
        gates(0, nxt, 1 - slot)
        gates(1, nblk - 1 - nxt, 1 - slot)
        return st_f, st_b

    if has_state:
        st0 = (s0_ref[0].T, s0_ref[1].T)
    else:
        zero = jnp.zeros((ia_ref.shape[1], qa_ref.shape[1]), F32)
        st0 = (zero, zero)
    st_f, st_b = lax.fori_loop(0, nblk, body, st0)
    sfin_ref[0] = st_f.T
    sfin_ref[1] = st_b.T

    o = oacc_ref[0] + oacc_ref[1]
    o_ref[...] = (_rms(o, ng_ref[0]) * _silu(ga_ref[...].astype(F32))).astype(o_ref.dtype)


def _hgrn(proj16, proj32, lb, norm_g, s0, layer_idx, out_prev, n_seq, seq, row_blk0, heads, key_d, val_d):
    t = proj16.shape[0]
    has_state = s0 is not None
    hk = heads * key_d // 128
    off_qa, off_ia, off_ga = 0, hk, 2 * hk
    off_zf, off_zb = 0, hk

    def col(off):
        return pl.BlockSpec((seq, 128), lambda b, h: (row_blk0 + b, off + h))

    lane = np.arange(key_d)[None, :]
    srow = np.repeat(np.arange(HGRN_DIAG), key_d)[:, None]
    ecat = jnp.asarray((lane % HGRN_DIAG == srow).astype(np.float32), dtype=BF16)
    in_specs = [col(off_qa), col(off_ia), col(off_zf), col(off_zb), col(off_ga),
                pl.BlockSpec((2, key_d), lambda b, h: (0, h)),
                pl.BlockSpec((1, 1, val_d), lambda b, h: (h, 0, 0)),
                pl.BlockSpec((HGRN_DIAG * key_d, key_d), lambda b, h: (0, 0))]
    args = [proj16, proj16, proj32, proj32, proj16, lb, norm_g.reshape(heads, 1, val_d), ecat]
    if has_state:
        in_specs.append(pl.BlockSpec((None, None, 2, None, key_d, val_d),
                                     lambda b, h: (b, layer_idx, 0, h, 0, 0)))
        args.append(s0)
    aliases = {}
    if out_prev is not None:
        in_specs.append(pl.BlockSpec(memory_space=pl.ANY))
        args.append(out_prev)
        aliases = {len(args) - 1: 0}

    kern = functools.partial(_hgrn_kernel, nblk=seq // HGRN_BLOCK, has_state=has_state,
                             qscale=float(key_d) ** -0.5)
    return pl.pallas_call(
        kern,
        grid=(n_seq, heads),
        in_specs=in_specs,
        out_specs=[pl.BlockSpec((seq, val_d), lambda b, h: (row_blk0 + b, h)),
                   pl.BlockSpec((None, 2, None, key_d, val_d), lambda b, h: (b, 0, h, 0, 0))],
        out_shape=[jax.ShapeDtypeStruct((t, heads * val_d), BF16),
                   jax.ShapeDtypeStruct((n_seq, 2, heads, key_d, val_d), F32)],
        scratch_shapes=[pltpu.VMEM((2, seq, val_d), F32),
                        pltpu.VMEM((2, 2, HGRN_BLOCK, key_d), F32),
                        pltpu.VMEM((2, 2, HGRN_BLOCK, key_d), F32)],
        input_output_aliases=aliases,
        compiler_params=_cparams("arbitrary", "arbitrary"),
        name="hgrn2",
    )(*args)


def _rope(x, cos, sin_signed):
    return x * cos + pltpu.roll(x, x.shape[-1] // 2, 1) * sin_signed


def _attn_kernel(*refs, has_ctx, grp, qblk, scale):
    if has_ctx:
        (q_ref, k_ref, v_ref, qn_ref, kn_ref, ck_ref, cv_ref, cos_ref, sin_ref, _prev, o_ref) = refs
    else:
        (q_ref, k_ref, v_ref, qn_ref, kn_ref, o_ref, ko_ref, vo_ref) = refs
    seq, hd = k_ref.shape
    k = _rms(k_ref[...], kn_ref[...])
    v = v_ref[...]
    if has_ctx:
        k = _rope(k, cos_ref[...], sin_ref[...])
        ck = ck_ref[...].astype(BF16)
        cv = cv_ref[...].astype(BF16)
    else:
        ko_ref[...] = k
        vo_ref[...] = v
    kb = k.astype(BF16)
    vb = v.astype(BF16)

    for g in range(grp):
        def body(i, carry, g=g):
            r0 = pl.multiple_of(i * qblk, qblk)
            q = _rms(q_ref[pl.ds(r0, qblk), g * hd:(g + 1) * hd].astype(F32), qn_ref[...])
            if has_ctx:
                q = _rope(q, cos_ref[pl.ds(r0, qblk), :], sin_ref[pl.ds(r0, qblk), :])
            qb = q.astype(BF16)
            s2 = lax.dot_general(qb, kb, _NT, preferred_element_type=F32) * scale
            m = jnp.max(s2, axis=-1, keepdims=True)
            if has_ctx:
                s1 = lax.dot_general(qb, ck, _NT, preferred_element_type=F32) * scale
                m = jnp.maximum(m, jnp.max(s1, axis=-1, keepdims=True))
                p1 = jnp.exp(s1 - m)
            p2 = jnp.exp(s2 - m)
            den = jnp.sum(p2, axis=-1, keepdims=True)
            acc = jnp.dot(p2.astype(BF16), vb, preferred_element_type=F32)
            if has_ctx:
                den = den + jnp.sum(p1, axis=-1, keepdims=True)
                acc = acc + jnp.dot(p1.astype(BF16), cv, preferred_element_type=F32)
            o_ref[pl.ds(r0, qblk), g * hd:(g + 1) * hd] = (acc / den).astype(o_ref.dtype)
            return carry

        lax.fori_loop(0, seq // qblk, body, 0)


def _attn(proj16, proj32, qn_g, kn_g, ctx, layer_idx, out_prev, n_seq, seq, row_blk0, heads, kv_heads, hd,
          col_q, col_kv):
    t = proj16.shape[0]
    grp = heads // kv_heads
    has_ctx = ctx is not None
    cq = col_q // grp
    ck0 = col_kv
    cv0 = ck0 + kv_heads
    in_specs = [pl.BlockSpec((seq, grp * hd), lambda b, h: (row_blk0 + b, cq + h)),
                pl.BlockSpec((seq, hd), lambda b, h: (row_blk0 + b, ck0 + h)),
                pl.BlockSpec((seq, hd), lambda b, h: (row_blk0 + b, cv0 + h)),
                pl.BlockSpec((1, hd), lambda b, h: (0, 0)),
                pl.BlockSpec((1, hd), lambda b, h: (0, 0))]
    args = [proj16, proj32, proj32, qn_g.reshape(1, hd), kn_g.reshape(1, hd)]
    out_specs = [pl.BlockSpec((seq, grp * hd), lambda b, h: (row_blk0 + b, h))]
    out_shape = [jax.ShapeDtypeStruct((t, heads * hd), BF16)]
    aliases = {}
    if has_ctx:
        cache_k, cache_v, cos, sin_signed = ctx
        nreq, nlay, past = cache_k.shape[:3]
        cache_k = cache_k.reshape(nreq, nlay, past, kv_heads * hd)
        cache_v = cache_v.reshape(nreq, nlay, past, kv_heads * hd)
        spec_c = pl.BlockSpec((None, None, past, hd), lambda b, h: (b, layer_idx, 0, h))
        in_specs += [spec_c, spec_c,
                     pl.BlockSpec((seq, hd), lambda b, h: (0, 0)),
                     pl.BlockSpec((seq, hd), lambda b, h: (0, 0)),
                     pl.BlockSpec(memory_space=pl.ANY)]
        args += [cache_k, cache_v, cos, sin_signed, out_prev]
        aliases = {len(args) - 1: 0}
    else:
        out_specs += [pl.BlockSpec((seq, hd), lambda b, h: (b, h)),
                      pl.BlockSpec((seq, hd), lambda b, h: (b, h))]
        out_shape += [jax.ShapeDtypeStruct((n_seq * seq, kv_heads * hd), F32)] * 2
    return pl.pallas_call(
        functools.partial(_attn_kernel, has_ctx=has_ctx, grp=grp, qblk=min(seq, 256), scale=float(hd) ** -0.5),
        grid=(n_seq, kv_heads),
        in_specs=in_specs,
        out_specs=out_specs,
        out_shape=out_shape,
        input_output_aliases=aliases,
        compiler_params=_cparams("arbitrary", "arbitrary"),
        name="gqa",
    )(*args)


def _ret_kernel(*refs, nchunk, has_ctx, kscale):
    if has_ctx:
        (lg_ref, q_ref, k_ref, v_ref, g_ref, gn_ref, s0_ref, cos_ref, sin_ref, _prev,
         o_ref, sfin_ref, sb_ref) = refs
    else:
        (lg_ref, q_ref, k_ref, v_ref, g_ref, gn_ref, o_ref, sfin_ref, sb_ref) = refs
    cs = RET_CHUNK
    kd_dim = q_ref.shape[1]
    h = pl.program_id(1)
    lgf = lg_ref[0, h]
    lgb = lg_ref[1, h]
    row = lax.broadcasted_iota(jnp.int32, (cs, cs), 0)
    col = lax.broadcasted_iota(jnp.int32, (cs, cs), 1)
    rel = (row - col).astype(F32)
    dmat = (jnp.exp(jnp.where(rel >= 0, rel * lgf, -jnp.inf))
            + jnp.exp(jnp.where(rel <= 0, -rel * lgb, -jnp.inf)))
    j = lax.broadcasted_iota(jnp.int32, (cs, 1), 0).astype(F32)
    one = jnp.ones((1, 1), F32)
    dec_f = jnp.exp(one * (cs * lgf))
    dec_b = jnp.exp(one * (cs * lgb))

    def load_qk(ref, r0, mul):
        x = ref[pl.ds(r0, cs), :].astype(F32) * mul
        if has_ctx:
            half = kd_dim // 2
            x1, x2 = x[:, :half], x[:, half:]
            cos = cos_ref[pl.ds(r0, cs), :]
            sin = sin_ref[pl.ds(r0, cs), :]
            x = jnp.concatenate([x1 * cos - x2 * sin, x2 * cos + x1 * sin], axis=1)
        return x

    def bwd_body(i, st):
        n = nchunk - 1 - i
        r0 = pl.multiple_of(n * cs, cs)
        sb_ref[n] = st
        k = load_qk(k_ref, r0, kscale)
        v = v_ref[pl.ds(r0, cs), :]
        return st * dec_b + _bdot((k * jnp.exp(j * lgb)).T, v)

    st_b0 = s0_ref[1] if has_ctx else jnp.zeros(sfin_ref.shape[1:], F32)
    st_b = lax.fori_loop(0, nchunk, bwd_body, st_b0)
    sfin_ref[1] = st_b

    def fwd_body(n, st):
        r0 = pl.multiple_of(n * cs, cs)
        q = load_qk(q_ref, r0, 1.0)
        k = load_qk(k_ref, r0, kscale)
        v = v_ref[pl.ds(r0, cs), :]
        a = _bdot_nt(q, k) * dmat
        o = _bdot(a, v)
        if has_ctx or nchunk > 1:
            o = o + _bdot(q * jnp.exp((j + 1.0) * lgf), st)
            o = o + _bdot(q * jnp.exp((cs - j) * lgb), sb_ref[n])
        mu = jnp.mean(o, axis=-1, keepdims=True)
        var = jnp.mean(jnp.square(o - mu), axis=-1, keepdims=True)
        y = (o - mu) * lax.rsqrt(var + EPS) * gn_ref[0]
        o_ref[pl.ds(r0, cs), :] = (_silu(g_ref[pl.ds(r0, cs), :].astype(F32)) * y).astype(o_ref.dtype)
        return st * dec_f + _bdot((k * jnp.exp((cs - 1.0 - j) * lgf)).T, v)

    st_f0 = s0_ref[0] if has_ctx else jnp.zeros(sfin_ref.shape[1:], F32)
    sfin_ref[0] = lax.fori_loop(0, nchunk, fwd_body, st_f0)


def _retention(proj, log_gamma, gn_g, ctx, layer_idx, out_prev, n_seq, seq, row_blk0, heads, key_d, val_d):
    t = proj.shape[0]
    has_ctx = ctx is not None
    nq = heads * key_d
    cq, ck, cv, cg = 0, nq // key_d, 2 * nq // val_d, (2 * nq + heads * val_d) // val_d
    in_specs = [pl.BlockSpec(memory_space=pltpu.SMEM),
                pl.BlockSpec((seq, key_d), lambda b, h: (row_blk0 + b, cq + h)),
                pl.BlockSpec((seq, key_d), lambda b, h: (row_blk0 + b, ck + h)),
                pl.BlockSpec((seq, val_d), lambda b, h: (row_blk0 + b, cv + h)),
                pl.BlockSpec((seq, val_d), lambda b, h: (row_blk0 + b, cg + h)),
                pl.BlockSpec((1, 1, val_d), lambda b, h: (h, 0, 0))]
    args = [log_gamma, proj, proj, proj, proj, gn_g.reshape(heads, 1, val_d)]
    aliases = {}
    if has_ctx:
        s0, cos, sin = ctx
        in_specs += [pl.BlockSpec((None, None, 2, None, key_d, val_d), lambda b, h: (b, layer_idx, 0, h, 0, 0)),
                     pl.BlockSpec((seq, key_d // 2), lambda b, h: (0, 0)),
                     pl.BlockSpec((seq, key_d // 2), lambda b, h: (0, 0)),
                     pl.BlockSpec(memory_space=pl.ANY)]
        args += [s0, cos, sin, out_prev]
        aliases = {len(args) - 1: 0}
    nchunk = seq // RET_CHUNK
    return pl.pallas_call(
        functools.partial(_ret_kernel, nchunk=nchunk, has_ctx=has_ctx, kscale=float(key_d) ** -0.5),
        grid=(n_seq, heads),
        in_specs=in_specs,
        out_specs=[pl.BlockSpec((seq, val_d), lambda b, h: (row_blk0 + b, h)),
                   pl.BlockSpec((None, 2, None, key_d, val_d), lambda b, h: (b, 0, h, 0, 0))],
        out_shape=[jax.ShapeDtypeStruct((t, heads * val_d), BF16),
                   jax.ShapeDtypeStruct((n_seq, 2, heads, key_d, val_d), F32)],
        scratch_shapes=[pltpu.VMEM((nchunk, key_d, val_d), F32)],
        input_output_aliases=aliases,
        compiler_params=_cparams("arbitrary", "arbitrary"),
        name="retention",
    )(*args)


def _outproj_kernel(*refs, n_in, n_x, n_ctx_tiles, n_groups, n_experts):
    mix_refs = refs[:n_in]
    w_refs = refs[n_in:2 * n_in]
    x_refs = refs[2 * n_in:2 * n_in + n_x]
    mod_ref, g2_ref, wr_ref, br_ref, xn_ref, h2_ref, eidx_ref, ew_ref = refs[2 * n_in + n_x:]
    acc = _bdot(mix_refs[0][...], w_refs[0][...])
    for mref, wref in zip(mix_refs[1:], w_refs[1:]):
        acc = acc + _bdot(mref[...], wref[...])
    xn = _token_rows(x_refs, pl.program_id(0), n_ctx_tiles) + mod_ref[0, 2:3, :] * acc
    xn_ref[...] = xn
    h2 = _rms(xn, g2_ref[...]) * (1.0 + mod_ref[0, 4:5, :]) + mod_ref[0, 3:4, :]
    h2_ref[...] = h2.astype(BF16).reshape(h2_ref.shape)
    h_hi = h2.astype(BF16)
    h_lo = (h2 - h_hi.astype(F32)).astype(BF16)
    lg = (jnp.dot(h_hi, wr_ref[0], preferred_element_type=F32)
          + jnp.dot(h_hi, wr_ref[1], preferred_element_type=F32)
          + jnp.dot(h_lo, wr_ref[0], preferred_element_type=F32)) + br_ref[...]

    per_grp = n_experts // n_groups
    lane = lax.broadcasted_iota(jnp.int32, lg.shape, 1).astype(F32)
    big = float(lg.shape[1])
    ninf = -jnp.inf
    gl = jnp.where(lane < n_groups, lg, ninf)
    gmax = jnp.max(gl, axis=-1, keepdims=True)
    p_grp = 1.0 / jnp.sum(jnp.exp(gl - gmax), axis=-1, keepdims=True)
    gidx = jnp.min(jnp.where(gl == gmax, lane, big), axis=-1, keepdims=True)
    in_grp = ((lane >= n_groups) & (lane < n_groups + n_experts)
              & (jnp.floor((lane - n_groups) / per_grp) == gidx))
    el = jnp.where(in_grp, lg, ninf)
    emax = jnp.max(el, axis=-1, keepdims=True)
    esum = jnp.sum(jnp.exp(el - emax), axis=-1, keepdims=True)
    i1 = jnp.min(jnp.where(el == emax, lane, big), axis=-1, keepdims=True)
    el2 = jnp.where(lane == i1, ninf, el)
    emax2 = jnp.max(el2, axis=-1, keepdims=True)
    i2 = jnp.min(jnp.where(el2 == emax2, lane, big), axis=-1, keepdims=True)
    p1 = 1.0 / esum
    p2 = jnp.exp(emax2 - emax) / esum
    w1 = p1 / (p1 + p2) * p_grp
    w2 = p2 / (p1 + p2) * p_grp
    eidx_ref[...] = jnp.where(lane == 0.0, i1 - n_groups, jnp.where(lane == 1.0, i2 - n_groups, 0.0)).astype(jnp.int32)
    ew_ref[...] = jnp.where(lane == 0.0, w1, jnp.where(lane == 1.0, w2, 0.0))


def _out_proj(mixes, w_parts, x, mod, g2, w_router, b_router, n_ctx_tiles, tiles_per_req, n_groups, n_experts):
    t = mixes[0].shape[0]
    d = w_parts[0].shape[1]
    tm = TOKEN_TILE
    n_in = len(mixes)
    mrow = functools.partial(_tile_mod_row, n_ctx_tiles=n_ctx_tiles, tiles_per_req=tiles_per_req)
    x_specs, x_args = _token_specs(x, tm, n_ctx_tiles, lambda i: i)
    in_specs = ([pl.BlockSpec((tm, m.shape[1]), lambda i: (i, 0)) for m in mixes]
                + [pl.BlockSpec(w.shape, lambda i: (0, 0)) for w in w_parts]
                + x_specs
                + [pl.BlockSpec((1, 6, d), lambda i: (mrow(i), 0, 0)),
                   pl.BlockSpec((1, d), lambda i: (0, 0)),
                   pl.BlockSpec(w_router.shape, lambda i: (0, 0, 0)),
                   pl.BlockSpec((1, 128), lambda i: (0, 0))])
    return pl.pallas_call(
        functools.partial(_outproj_kernel, n_in=n_in, n_x=len(x_args), n_ctx_tiles=n_ctx_tiles,
                          n_groups=n_groups, n_experts=n_experts),
        grid=(t // tm,),
        in_specs=in_specs,
        out_specs=[pl.BlockSpec((tm, d), lambda i: (i, 0)),
                   pl.BlockSpec((tm, d // 128, 128), lambda i: (i, 0, 0)),
                   pl.BlockSpec((tm, 128), lambda i: (i, 0)),
                   pl.BlockSpec((tm, 128), lambda i: (i, 0))],
        out_shape=[jax.ShapeDtypeStruct((t, d), F32),
                   jax.ShapeDtypeStruct((t, d // 128, 128), BF16),
                   jax.ShapeDtypeStruct((t, 128), jnp.int32),
                   jax.ShapeDtypeStruct((t, 128), F32)],
        compiler_params=_cparams("arbitrary"),
        name="out_proj_router",
    )(*mixes, *w_parts, *x_args, mod, g2.reshape(1, d), w_router, b_router)


def _dispatch_plan(e1, e2, n_experts, tile):
    t = e1.shape[0]
    e = jnp.concatenate([e1, e2])
    onehot = (e[:, None] == jnp.arange(n_experts, dtype=jnp.int32)[None, :]).astype(jnp.int32)
    csum = jnp.cumsum(onehot, axis=0)
    rank = jnp.take_along_axis(csum, e[:, None], axis=1)[:, 0] - 1
    counts = csum[-1]
    padded = ((counts + tile - 1) // tile) * tile
    ends = jnp.cumsum(padded)
    starts = ends - padded
    pos = starts[e] + rank
    n_tiles = (2 * t) // tile + n_experts
    n_used = (ends[-1] // tile).astype(jnp.int32)
    tile_start = jnp.arange(n_tiles, dtype=jnp.int32) * tile
    probe = jnp.minimum(tile_start, ends[-1] - 1)
    tile_expert = jnp.sum((ends[None, :] <= probe[:, None]).astype(jnp.int32), axis=1)
    return tile_expert, n_used.reshape(1), ends.astype(jnp.int32), pos[:t], pos[t:], n_tiles


def _row_source_kernel(ends_ref, p1_ref, p2_ref, o_ref, *, n_tok, n_rows, tile, n_experts):
    def clear8(j, c):
        base = lax.rem(j * 8, n_tok)
        for u in range(8):
            o_ref[j * 8 + u] = base + u
        return c

    def per_expert(e, c):
        first = lax.shift_right_logical(jnp.maximum(ends_ref[e] - tile, 0), 3)
        lax.fori_loop(first, first + tile // 8, clear8, 0)
        return c

    g = pl.program_id(0)

    @pl.when(g == 0)
    def _():
        lax.fori_loop(0, n_experts, per_expert, 0)
        lax.fori_loop(lax.shift_right_logical(ends_ref[n_experts - 1], 3), n_rows // 8, clear8, 0)

    chunk = p1_ref.shape[2]

    def place(r, c):
        t = g * chunk + r
        o_ref[p1_ref[0, 0, r]] = t
        o_ref[p2_ref[0, 0, r]] = t
        return c

    lax.fori_loop(0, chunk, place, 0, unroll=8)


def _row_source(ends, pos1, pos2, n_rows, tile):
    n_tok = pos1.shape[0]
    chunk = ROW_SOURCE_CHUNK
    smem = pl.BlockSpec(memory_space=pltpu.SMEM)
    pos_spec = pl.BlockSpec((1, 1, chunk), lambda g: (g, 0, 0), memory_space=pltpu.SMEM)
    return pl.pallas_call(
        functools.partial(_row_source_kernel, n_tok=n_tok, n_rows=n_rows, tile=tile, n_experts=ends.shape[0]),
        grid=(n_tok // chunk,),
        in_specs=[smem, pos_spec, pos_spec],
        out_specs=smem,
        out_shape=jax.ShapeDtypeStruct((n_rows,), jnp.int32),
        compiler_params=_cparams("arbitrary"),
        name="moe_row_source",
    )(ends, pos1.reshape(n_tok // chunk, 1, chunk), pos2.reshape(n_tok // chunk, 1, chunk))


def _moe_kernel(te_ref, nu_ref, *refs, tm, ahead):
    tok_refs = refs[:ahead + 1]
    h_hbm, wg_ref, wu_ref, wd_ref, o_ref, wg_bf, wu_bf, wd_bf, buf, sem = refs[ahead + 1:]
    i = pl.program_id(0)
    nslot = ahead + 1
    slot = lax.rem(i, nslot)
    n_used = nu_ref[0]

    def issue(tile, tok_ref):
        s = lax.rem(tile, nslot)

        def body(j, c):
            for u in range(2):
                r = 2 * j + u
                pltpu.make_async_copy(h_hbm.at[tok_ref[0, 0, r]], buf.at[s, r], sem.at[s]).start(priority=u)
            return c

        lax.fori_loop(0, tm // 2, body, 0, unroll=4)

    @pl.when(i == 0)
    def _():
        for j in range(ahead):
            @pl.when(j < n_used)
            def _():
                issue(jnp.int32(j), tok_refs[j])

    @pl.when(i + ahead < n_used)
    def _():
        issue(i + ahead, tok_refs[ahead])

    new_expert = jnp.logical_or(i == 0, te_ref[i] != te_ref[jnp.maximum(i - 1, 0)])

    @pl.when(new_expert)
    def _():
        wg_bf[...] = wg_ref[...].astype(BF16)
        wu_bf[...] = wu_ref[...].astype(BF16)
        wd_bf[...] = wd_ref[...].astype(BF16)

    @pl.when(i < n_used)
    def _():
        pltpu.make_async_copy(h_hbm.at[pl.ds(0, tm)], buf.at[slot], sem.at[slot]).wait()
        x = buf[slot].reshape(tm, -1)
        g = jnp.dot(x, wg_bf[...], preferred_element_type=F32)
        u = jnp.dot(x, wu_bf[...], preferred_element_type=F32)
        act = _silu(g) * u
        y = jnp.dot(act.astype(BF16), wd_bf[...], preferred_element_type=F32)
        o_ref[...] = y.astype(BF16).reshape(o_ref.shape)

    @pl.when(i >= n_used)
    def _():
        o_ref[...] = jnp.zeros_like(o_ref)


def _moe_experts(h3, src_tok, tile_expert, n_used, w_gate, w_up, w_down, layer_idx, tm):
    t, sub, lanes = h3.shape
    d = sub * lanes
    n_tiles = src_tok.shape[0] // tm
    f = w_gate.shape[-1]
    ahead = MOE_GATHER_AHEAD
    tok = src_tok.reshape(n_tiles, 1, tm)

    def tok_spec(k):
        return pl.BlockSpec((1, 1, tm), lambda i, te, nu: (jnp.minimum(i + k, n_tiles - 1), 0, 0),
                            memory_space=pltpu.SMEM)

    return pl.pallas_call(
        functools.partial(_moe_kernel, tm=tm, ahead=ahead),
        grid_spec=pltpu.PrefetchScalarGridSpec(
            num_scalar_prefetch=2,
            grid=(n_tiles,),
            in_specs=[tok_spec(k) for k in range(ahead + 1)] + [
                      pl.BlockSpec(memory_space=pl.ANY),
                      pl.BlockSpec((None, None, d, f), lambda i, te, nu: (layer_idx, te[i], 0, 0)),
                      pl.BlockSpec((None, None, d, f), lambda i, te, nu: (layer_idx, te[i], 0, 0)),
                      pl.BlockSpec((None, None, f, d), lambda i, te, nu: (layer_idx, te[i], 0, 0))],
            out_specs=pl.BlockSpec((tm, sub, lanes), lambda i, te, nu: (i, 0, 0)),
            scratch_shapes=[pltpu.VMEM((d, f), BF16), pltpu.VMEM((d, f), BF16), pltpu.VMEM((f, d), BF16),
                            pltpu.VMEM((ahead + 1, tm, sub, lanes), BF16),
                            pltpu.SemaphoreType.DMA((ahead + 1,))]),
        out_shape=jax.ShapeDtypeStruct((n_tiles * tm, sub, lanes), BF16),
        compiler_params=_cparams("arbitrary"),
        name="moe_experts",
    )(tile_expert, n_used, *([tok] * (ahead + 1)), h3, w_gate, w_up, w_down)


def _combine_kernel(p1_ref, p2_ref, p1n_ref, p2n_ref, y_hbm, x_ref, ew_ref, mod_ref, gf_ref, *rest,
                    tm, n_steps, n_ctx_tiles, final_norm):
    out_refs, (buf, sem) = rest[:-2], rest[-2:]
    i = pl.program_id(0)
    slot = lax.rem(i, 2)

    def issue(s, pa_ref, pb_ref):
        def body(r, c):
            pltpu.make_async_copy(y_hbm.at[pa_ref[0, 0, r]], buf.at[s, 0, r], sem.at[s, 0]).start(priority=0)
            pltpu.make_async_copy(y_hbm.at[pb_ref[0, 0, r]], buf.at[s, 1, r], sem.at[s, 1]).start(priority=1)
            return c

        lax.fori_loop(0, tm, body, 0, unroll=8)

    @pl.when(i == 0)
    def _():
        issue(0, p1_ref, p2_ref)

    @pl.when(i + 1 < n_steps)
    def _():
        issue(1 - slot, p1n_ref, p2n_ref)

    for k in range(2):
        pltpu.make_async_copy(y_hbm.at[pl.ds(0, tm)], buf.at[slot, k], sem.at[slot, k]).wait()
    ew = ew_ref[...]
    y = (ew[:, 0:1] * buf[slot, 0].reshape(tm, -1).astype(F32)
         + ew[:, 1:2] * buf[slot, 1].reshape(tm, -1).astype(F32))
    x = x_ref[...] + mod_ref[0, 5:6, :] * y
    if not final_norm:
        out_refs[0][...] = x
    else:
        x = _rms(x, gf_ref[...])
        ctx_ref, lat_ref = out_refs

        @pl.when(i < n_ctx_tiles)
        def _():
            ctx_ref[...] = x

        @pl.when(i >= n_ctx_tiles)
        def _():
            lat_ref[...] = x


def _moe_combine(y_sorted, pos1, pos2, x, ew, mod, final_g, n_ctx_tiles, tiles_per_req, final_norm):
    t, d = x.shape
    tm = TOKEN_TILE
    n_steps = t // tm
    mrow = functools.partial(_tile_mod_row, n_ctx_tiles=n_ctx_tiles, tiles_per_req=tiles_per_req)
    smem_rows = pl.BlockSpec((1, 1, tm), lambda i: (i, 0, 0), memory_space=pltpu.SMEM)
    smem_next = pl.BlockSpec((1, 1, tm), lambda i: (jnp.minimum(i + 1, n_steps - 1), 0, 0),
                             memory_space=pltpu.SMEM)
    p1 = pos1.reshape(n_steps, 1, tm)
    p2 = pos2.reshape(n_steps, 1, tm)
    if final_norm:
        out_specs = [pl.BlockSpec((tm, d), lambda i: (jnp.minimum(i, n_ctx_tiles - 1), 0)),
                     pl.BlockSpec((tm, d), lambda i: (jnp.maximum(i - n_ctx_tiles, 0), 0))]
        out_shape = [jax.ShapeDtypeStruct((n_ctx_tiles * tm, d), F32),
                     jax.ShapeDtypeStruct((t - n_ctx_tiles * tm, d), F32)]
    else:
        out_specs = pl.BlockSpec((tm, d), lambda i: (i, 0))
        out_shape = jax.ShapeDtypeStruct((t, d), F32)
    return pl.pallas_call(
        functools.partial(_combine_kernel, tm=tm, n_steps=n_steps, n_ctx_tiles=n_ctx_tiles,
                          final_norm=final_norm),
        grid=(n_steps,),
        in_specs=[smem_rows, smem_rows, smem_next, smem_next,
                  pl.BlockSpec(memory_space=pl.ANY),
                  pl.BlockSpec((tm, d), lambda i: (i, 0)),
                  pl.BlockSpec((tm, 128), lambda i: (i, 0)),
                  pl.BlockSpec((1, 6, d), lambda i: (mrow(i), 0, 0)),
                  pl.BlockSpec((1, d), lambda i: (0, 0))],
        out_specs=out_specs,
        out_shape=out_shape,
        scratch_shapes=[pltpu.VMEM((2, 2, tm) + y_sorted.shape[1:], y_sorted.dtype),
                        pltpu.SemaphoreType.DMA((2, 2))],
        compiler_params=_cparams("arbitrary"),
        name="moe_combine",
    )(p1, p2, p1, p2, y_sorted, x, ew, mod, final_g.reshape(1, d))


def _axial_angles(seq, head_dim):
    rows = seq // GRID_W
    row = jnp.repeat(jnp.arange(rows), GRID_W).astype(F32)
    col = jnp.tile(jnp.arange(GRID_W), rows).astype(F32)
    n_freq = head_dim // 4
    inv = ROPE_BASE ** (-jnp.arange(n_freq, dtype=F32) / n_freq)
    return jnp.concatenate([row[:, None] * inv, col[:, None] * inv], axis=-1)


def kernel(x_prompt, x_sample, cache_attn_k, cache_attn_v, state_hgrn, state_ret, c, c_ctx, mod_w, mod_b, norm1_g, norm2_g, even_w_in, even_w_out, hgrn_lb_logits, hgrn_norm_g, attn_qn_g, attn_kn_g, odd_w_in, odd_w_out, ret_decay_logit, ret_gn_g, moe_w_group, moe_b_group, moe_w_expert, moe_b_expert, moe_w_gate, moe_w_up, moe_w_down, final_norm_g):
    batch, seq, d = x_prompt.shape
    dec_batch, dec_seq, _ = x_sample.shape
    depth = mod_w.shape[0]
    heads_a, val_a = hgrn_norm_g.shape[1:]
    key_a = hgrn_lb_logits.shape[2] // heads_a
    hd_b = attn_qn_g.shape[1]
    kv_b = cache_attn_k.shape[3]
    heads_b = (even_w_out.shape[1] - heads_a * val_a) // hd_b
    heads_c, val_c = ret_gn_g.shape[1:]
    key_c = state_ret.shape[4]
    n_experts = moe_w_expert.shape[2]
    n_groups = moe_w_group.shape[2]
    tm = TOKEN_TILE
    n_ctx = batch * seq
    n_ctx_tiles = n_ctx // tm
    tiles_per_req = dec_seq // tm
    assert seq % tm == 0 and dec_seq % tm == 0 and n_ctx % dec_seq == 0
    assert key_a == 128 and val_a == 128 and hd_b == 128 and n_groups + n_experts <= 128

    x = (x_prompt.reshape(n_ctx, d), x_sample.reshape(dec_batch * dec_seq, d))
    t = n_ctx + dec_batch * dec_seq

    n_vec = 1 + dec_batch
    rows = -(-n_vec // 8) * 8
    cvecs = jnp.zeros((rows, d), F32).at[0].set(c_ctx).at[1:n_vec].set(c)
    mods = _ada_mod(cvecs, mod_w, mod_b).reshape(depth, rows, 6, d)

    lower_bounds = jnp.cumsum(jax.nn.softmax(hgrn_lb_logits.astype(F32), axis=0), axis=0)
    log_gamma = jax.nn.log_sigmoid(ret_decay_logit.astype(F32))

    ang_b = _axial_angles(dec_seq, hd_b)
    cos_b = jnp.concatenate([jnp.cos(ang_b), jnp.cos(ang_b)], axis=-1)
    sin_b = jnp.concatenate([-jnp.sin(ang_b), jnp.sin(ang_b)], axis=-1)
    ang_c = _axial_angles(dec_seq, key_c)
    cos_c, sin_c = jnp.cos(ang_c), jnp.sin(ang_c)

    pad = 128 - n_groups - n_experts
    new_k = new_v = new_sh = new_sr = None
    assert depth == 2, "one HGRN2/GQA layer followed by one retention layer"
    for l in range(depth):
        mod = mods[l]
        if l % 2 == 0:
            e = l // 2
            na, nva, nqb, nkv = heads_a * key_a, heads_a * val_a, heads_b * hd_b, kv_b * hd_b
            o_zf, o_kb = na + nva, 3 * na + 2 * nva + nqb
            w_in = even_w_in[e].astype(BF16)
            proj16, proj32 = _in_proj(x, norm1_g[l], mod, w_in, t, n_ctx_tiles, tiles_per_req, w_in.shape[1],
                                      f32_cols=((o_zf, o_zf + 2 * na), (o_kb, o_kb + 2 * nkv)))
            lb = lower_bounds[l]
            oa, s_ctx = _hgrn(proj16, proj32, lb, hgrn_norm_g[e], None, e, None, batch, seq, 0,
                              heads_a, key_a, val_a)
            oa, _ = _hgrn(proj16, proj32, lb, hgrn_norm_g[e], state_hgrn, e, oa, dec_batch, dec_seq,
                          n_ctx // dec_seq, heads_a, key_a, val_a)
            col_q, col_kv = (na + 2 * nva) // hd_b, (2 * na) // hd_b
            ob, kc, vc = _attn(proj16, proj32, attn_qn_g[e], attn_kn_g[e], None, e, None, batch, seq, 0,
                               heads_b, kv_b, hd_b, col_q, col_kv)
            (ob,) = _attn(proj16, proj32, attn_qn_g[e], attn_kn_g[e],
                          (cache_attn_k, cache_attn_v, cos_b, sin_b), e, ob,
                          dec_batch, dec_seq, n_ctx // dec_seq, heads_b, kv_b, hd_b, col_q, col_kv)
            new_k = kc.reshape(batch, 1, seq, kv_b, hd_b)
            new_v = vc.reshape(batch, 1, seq, kv_b, hd_b)
            new_sh = s_ctx[:, None]
            w_out = even_w_out[e].astype(BF16)
            mixes = [oa, ob]
            w_parts = [w_out[:heads_a * val_a], w_out[heads_a * val_a:]]
        else:
            o = l // 2
            w_in = odd_w_in[o].astype(BF16)
            proj = _in_proj(x, norm1_g[l], mod, w_in, t, n_ctx_tiles, tiles_per_req, w_in.shape[1] // 2)
            y, s_ctx = _retention(proj, log_gamma[o], ret_gn_g[o], None, o, None, batch, seq, 0,
                                  heads_c, key_c, val_c)
            y, _ = _retention(proj, log_gamma[o], ret_gn_g[o], (state_ret, cos_c, sin_c), o, y,
                              dec_batch, dec_seq, n_ctx // dec_seq, heads_c, key_c, val_c)
            new_sr = s_ctx[:, None]
            mixes = [y]
            w_parts = [odd_w_out[o].astype(BF16)]

        w_router = jnp.concatenate([moe_w_group[l], moe_w_expert[l], jnp.zeros((d, pad), F32)], axis=1)
        w_router_hi = w_router.astype(BF16)
        w_router = jnp.stack([w_router_hi, (w_router - w_router_hi.astype(F32)).astype(BF16)])
        b_router = jnp.concatenate([moe_b_group[l], moe_b_expert[l], jnp.zeros((pad,), F32)]).reshape(1, 128)
        xn, h2, eidx, ew = _out_proj(mixes, w_parts, x, mod, norm2_g[l], w_router, b_router,
                                     n_ctx_tiles, tiles_per_req, n_groups, n_experts)
        tile_expert, n_used, ends, pos1, pos2, n_tiles = _dispatch_plan(
            eidx[:, 0], eidx[:, 1], n_experts, EXPERT_TILE)
        src_tok = _row_source(ends, pos1, pos2, n_tiles * EXPERT_TILE, EXPERT_TILE)
        y_sorted = _moe_experts(h2, src_tok, tile_expert, n_used, moe_w_gate, moe_w_up, moe_w_down, l, EXPERT_TILE)
        x = _moe_combine(y_sorted, pos1, pos2, xn, ew, mod, final_norm_g, n_ctx_tiles, tiles_per_req,
                         final_norm=(l == depth - 1))

    y_prompt = x[0].reshape(batch, seq, d)
    y_sample = x[1].reshape(dec_batch, dec_seq, d)
    return (y_prompt, y_sample, new_k, new_v, new_sh, new_sr)
```

```python
import functools

import numpy as np
import jax
import jax.numpy as jnp
from jax import lax
from jax.experimental import pallas as pl
from jax.experimental.pallas import tpu as pltpu

F32 = jnp.float32
BF16 = jnp.bfloat16
EPS = 1e-6
ROPE_BASE = 10000.0
GRID_W = 64
N_GROUPS = 4
TOP_K_INNER = 2
HGRN_BLOCK = 128
HGRN_DIAG = 8
RET_CHUNK = 256
TOKEN_TILE = 256
EXPERT_TILE = 256
MOE_GATHER_AHEAD = 2
ROW_SOURCE_CHUNK = 512
V7X_VMEM_LIMIT_BYTES = 56 * 1024 * 1024
HIGHEST = lax.Precision.HIGHEST
LOG2_E = 1.4426950408889634

_NT = (((1,), (1,)), ((), ()))


def _cparams(*sem):
    return pltpu.CompilerParams(dimension_semantics=sem, vmem_limit_bytes=V7X_VMEM_LIMIT_BYTES)


def _rms(x, g):
    return x * lax.rsqrt(jnp.mean(x * x, axis=-1, keepdims=True) + EPS) * g


def _silu(x):
    return x * jax.nn.sigmoid(x)


def _bdot(a, b):
    return jnp.dot(a.astype(BF16), b.astype(BF16), preferred_element_type=F32)


def _bdot_nt(a, b):
    return lax.dot_general(a.astype(BF16), b.astype(BF16), _NT, preferred_element_type=F32)


def _mod_kernel(c_ref, w_ref, b_ref, o_ref):
    a = _silu(c_ref[...])
    o_ref[0] = jnp.dot(a, w_ref[0], precision=HIGHEST, preferred_element_type=F32) + b_ref[0]


def _ada_mod(cvecs, mod_w, mod_b):
    depth, d, n = mod_w.shape
    tn = n // 4
    rows = cvecs.shape[0]
    return pl.pallas_call(
        _mod_kernel,
        grid=(depth, n // tn),
        in_specs=[pl.BlockSpec((rows, d), lambda l, j: (0, 0)),
                  pl.BlockSpec((1, d, tn), lambda l, j: (l, 0, j)),
                  pl.BlockSpec((1, 1, tn), lambda l, j: (l, 0, j))],
        out_specs=pl.BlockSpec((1, rows, tn), lambda l, j: (l, 0, j)),
        out_shape=jax.ShapeDtypeStruct((depth, rows, n), F32),
        compiler_params=_cparams("arbitrary", "arbitrary"),
        name="ada_mod",
    )(cvecs, mod_w, mod_b.reshape(depth, 1, n))


def _tile_mod_row(i, n_ctx_tiles, tiles_per_req):
    return jnp.where(i < n_ctx_tiles, 0, 1 + jnp.maximum(i - n_ctx_tiles, 0) // tiles_per_req)


def _token_specs(x, tm, n_ctx_tiles, step_of):
    if not isinstance(x, tuple):
        return [pl.BlockSpec((tm, x.shape[1]), lambda *g: (step_of(*g), 0))], (x,)
    d = x[0].shape[1]
    return ([pl.BlockSpec((tm, d), lambda *g: (jnp.minimum(step_of(*g), n_ctx_tiles - 1), 0)),
             pl.BlockSpec((tm, d), lambda *g: (jnp.maximum(step_of(*g) - n_ctx_tiles, 0), 0))], x)


def _token_rows(x_refs, i, n_ctx_tiles):
    if len(x_refs) == 1:
        return x_refs[0][...]
    return jnp.where(i < n_ctx_tiles, x_refs[0][...], x_refs[1][...])


def _inproj_kernel(*refs, n_x, n_ctx_tiles, f32_cols):
    x_refs = refs[:n_x]
    g_ref, mod_ref, w_ref = refs[n_x:n_x + 3]
    out_refs = refs[n_x + 3:]
    x = _token_rows(x_refs, pl.program_id(1), n_ctx_tiles)
    h = _rms(x, g_ref[...]) * (1.0 + mod_ref[0, 1:2, :]) + mod_ref[0, 0:1, :]
    acc = jnp.dot(h.astype(BF16), w_ref[...], preferred_element_type=F32)
    if not f32_cols:
        out_refs[0][...] = acc.astype(BF16)
        return
    n = acc.shape[1]
    at16 = at32 = start = 0
    for lo, hi in tuple(f32_cols) + ((n, n),):
        if lo > start:
            out_refs[0][:, at16:at16 + lo - start] = acc[:, start:lo].astype(BF16)
            at16 += lo - start
        if hi > lo:
            out_refs[1][:, at32:at32 + hi - lo] = acc[:, lo:hi]
            at32 += hi - lo
        start = hi


def _in_proj(x, g, mod, w_bf, t, n_ctx_tiles, tiles_per_req, tn, f32_cols=()):
    d, n = w_bf.shape
    tm = TOKEN_TILE
    mrow = functools.partial(_tile_mod_row, n_ctx_tiles=n_ctx_tiles, tiles_per_req=tiles_per_req)
    x_specs, x_args = _token_specs(x, tm, n_ctx_tiles, lambda j, i: i)
    if not f32_cols:
        out_specs = pl.BlockSpec((tm, tn), lambda j, i: (i, j))
        out_shape = jax.ShapeDtypeStruct((t, n), BF16)
    else:
        assert tn == n
        n32 = sum(hi - lo for lo, hi in f32_cols)
        out_specs = [pl.BlockSpec((tm, n - n32), lambda j, i: (i, 0)),
                     pl.BlockSpec((tm, n32), lambda j, i: (i, 0))]
        out_shape = [jax.ShapeDtypeStruct((t, n - n32), BF16), jax.ShapeDtypeStruct((t, n32), F32)]
    return pl.pallas_call(
        functools.partial(_inproj_kernel, n_x=len(x_args), n_ctx_tiles=n_ctx_tiles, f32_cols=tuple(f32_cols)),
        grid=(n // tn, t // tm),
        in_specs=x_specs + [pl.BlockSpec((1, d), lambda j, i: (0, 0)),
                            pl.BlockSpec((1, 6, d), lambda j, i: (mrow(i), 0, 0)),
                            pl.BlockSpec((d, tn), lambda j, i: (0, j))],
        out_specs=out_specs,
        out_shape=out_shape,
        compiler_params=_cparams("arbitrary", "arbitrary"),
        name="in_proj",
    )(*x_args, g.reshape(1, d), mod, w_bf)


def _hgrn_decay(z, lb, reverse):
    nb = HGRN_BLOCK
    row = lax.broadcasted_iota(jnp.int32, (nb, nb), 0)
    col = lax.broadcasted_iota(jnp.int32, (nb, nb), 1)
    tri = jnp.where((col >= row) if reverse else (col <= row), 1.0, 0.0).astype(BF16)
    logf = jnp.log(lb + (1.0 - lb) * jax.nn.sigmoid(z))
    k = (1.0 - lb) * jax.nn.sigmoid(-z)
    l2 = logf * LOG2_E
    hi = l2.astype(BF16)
    lo = (l2 - hi.astype(F32)).astype(BF16)
    c = jnp.dot(tri, hi, preferred_element_type=F32) + jnp.dot(tri, lo, preferred_element_type=F32)
    return c, k


def _hgrn_block(q, v, st_t, reverse, c_ref, k_ref, ecat):
    nb = HGRN_BLOCK
    row = lax.broadcasted_iota(jnp.int32, (nb, nb), 0)
    col = lax.broadcasted_iota(jnp.int32, (nb, nb), 1)
    c = c_ref[...]
    k = k_ref[...]
    edge = 0 if reverse else nb - 1
    c_edge = c_ref[edge:edge + 1, :]

    o = _bdot_nt(q * jnp.exp2(c), st_t)
    kd = k * jnp.exp2(c_edge - c)
    st_new = st_t * jnp.exp2(c_edge) + _bdot(v.T, kd)

    a = jnp.zeros((nb, nb), F32)
    m = nb // 2
    while m >= HGRN_DIAG:
        pieces = []
        for grp in range(nb // (2 * m)):
            idx = grp * 2 * m + (m if reverse else m - 1)
            pieces.append(jnp.broadcast_to(c_ref[idx:idx + 1, :], (2 * m, nb)))
        bnd = pieces[0] if len(pieces) == 1 else jnp.concatenate(pieces, axis=0)
        e = jnp.exp2(-jnp.abs(c - bnd))
        am = _bdot_nt(q * e, k * e)
        same = (row & ~(2 * m - 1)) == (col & ~(2 * m - 1))
        r_hi = (row & m) != 0
        c_hi = (col & m) != 0
        if reverse:
            mask = same & jnp.logical_not(r_hi) & c_hi
        else:
            mask = same & r_hi & jnp.logical_not(c_hi)
        a = jnp.where(mask, am, a)
        m //= 2

    dg = HGRN_DIAG
    rowi = lax.broadcasted_iota(jnp.int32, (dg, nb), 0)
    cols = []
    for s in range(dg):
        keep = (rowi <= s) if reverse else (rowi >= s)
        tiles = []
        for j in range(nb // dg):
            r = j * dg + s
            qj = q[j * dg:(j + 1) * dg, :]
            cj = c[j * dg:(j + 1) * dg, :]
            cs = jnp.broadcast_to(c_ref[r:r + 1, :], (dg, nb))
            ks = jnp.broadcast_to(k_ref[r:r + 1, :], (dg, nb))
            tiles.append(qj * ks * jnp.exp2(jnp.where(keep, cj - cs, -jnp.inf)))
        cols.append(jnp.concatenate(tiles, axis=0).astype(BF16))
    p = jnp.concatenate(cols, axis=1)
    red = jnp.dot(p, ecat, preferred_element_type=F32)
    a = jnp.where((row >> 3) == (col >> 3), red, a)
    o = o + _bdot(a, v)
    return o, st_new


def _hgrn_kernel(*refs, nblk, has_state, qscale):
    qa_ref, ia_ref, zf_ref, zb_ref, ga_ref, lb_ref, ng_ref, ecat_ref = refs[:8]
    s0_ref = refs[8] if has_state else None
    o_ref, sfin_ref, oacc_ref, c_scr, k_scr = refs[-5:]
    nb = HGRN_BLOCK

    def gates(direction, blk, slot):
        z_ref = zb_ref if direction == 1 else zf_ref
        r0 = pl.multiple_of(blk * nb, nb)
        c, k = _hgrn_decay(z_ref[pl.ds(r0, nb), :], lb_ref[direction:direction + 1, :], direction == 1)
        c_scr[direction, slot] = c
        k_scr[direction, slot] = k

    def block(direction, blk, slot, st_t):
        r0 = pl.multiple_of(blk * nb, nb)
        q = _silu(qa_ref[pl.ds(r0, nb), :].astype(F32)) * qscale
        v = ia_ref[pl.ds(r0, nb), :].astype(F32)
        o, st_new = _hgrn_block(q, v, st_t, direction == 1, c_scr.at[direction, slot],
                                k_scr.at[direction, slot], ecat_ref[...])
        oacc_ref[direction, pl.ds(r0, nb), :] = o
        return st_new

    gates(0, 0, 0)
    gates(1, nblk - 1, 0)

    def body(i, carry):
        st_f, st_b = carry
        slot = lax.rem(i, 2)
        st_f = block(0, i, slot, st_f)
        st_b = block(1, nblk - 1 - i, slot, st_b)
        nxt = jnp.minimum(i + 1, nblk - 1)
        gates(0, nxt, 1 - slot)
        gates(1, nblk - 1 - nxt, 1 - slot)
        return st_f, st_b

    if has_state:
        st0 = (s0_ref[0].T, s0_ref[1].T)
    else:
        zero = jnp.zeros((ia_ref.shape[1], qa_ref.shape[1]), F32)
        st0 = (zero, zero)
    st_f, st_b = lax.fori_loop(0, nblk, body, st0)
    sfin_ref[0] = st_f.T
    sfin_ref[1] = st_b.T

    o = oacc_ref[0] + oacc_ref[1]
    o_ref[...] = (_rms(o, ng_ref[0]) * _silu(ga_ref[...].astype(F32))).astype(o_ref.dtype)


def _hgrn(proj16, proj32, lb, norm_g, s0, layer_idx, out_prev, n_seq, seq, row_blk0, heads, key_d, val_d):
    t = proj16.shape[0]
    has_state = s0 is not None
    hk = heads * key_d // 128
    off_qa, off_ia, off_ga = 0, hk, 2 * hk
    off_zf, off_zb = 0, hk

    def col(off):
        return pl.BlockSpec((seq, 128), lambda b, h: (row_blk0 + b, off + h))

    lane = np.arange(key_d)[None, :]
    srow = np.repeat(np.arange(HGRN_DIAG), key_d)[:, None]
    ecat = jnp.asarray((lane % HGRN_DIAG == srow).astype(np.float32), dtype=BF16)
    in_specs = [col(off_qa), col(off_ia), col(off_zf), col(off_zb), col(off_ga),
                pl.BlockSpec((2, key_d), lambda b, h: (0, h)),
                pl.BlockSpec((1, 1, val_d), lambda b, h: (h, 0, 0)),
                pl.BlockSpec((HGRN_DIAG * key_d, key_d), lambda b, h: (0, 0))]
    args = [proj16, proj16, proj32, proj32, proj16, lb, norm_g.reshape(heads, 1, val_d), ecat]
    if has_state:
        in_specs.append(pl.BlockSpec((None, None, 2, None, key_d, val_d),
                                     lambda b, h: (b, layer_idx, 0, h, 0, 0)))
        args.append(s0)
    aliases = {}
    if out_prev is not None:
        in_specs.append(pl.BlockSpec(memory_space=pl.ANY))
        args.append(out_prev)
        aliases = {len(args) - 1: 0}

    kern = functools.partial(_hgrn_kernel, nblk=seq // HGRN_BLOCK, has_state=has_state,
                             qscale=float(key_d) ** -0.5)
    return pl.pallas_call(
        kern,
        grid=(n_seq, heads),
        in_specs=in_specs,
        out_specs=[pl.BlockSpec((seq, val_d), lambda b, h: (row_blk0 + b, h)),
                   pl.BlockSpec((None, 2, None, key_d, val_d), lambda b, h: (b, 0, h, 0, 0))],
        out_shape=[jax.ShapeDtypeStruct((t, heads * val_d), BF16),
                   jax.ShapeDtypeStruct((n_seq, 2, heads, key_d, val_d), F32)],
        scratch_shapes=[pltpu.VMEM((2, seq, val_d), F32),
                        pltpu.VMEM((2, 2, HGRN_BLOCK, key_d), F32),
                        pltpu.VMEM((2, 2, HGRN_BLOCK, key_d), F32)],
        input_output_aliases=aliases,
        compiler_params=_cparams("arbitrary", "arbitrary"),
        name="hgrn2",
    )(*args)


def _rope(x, cos, sin_signed):
    return x * cos + pltpu.roll(x, x.shape[-1] // 2, 1) * sin_signed


def _attn_kernel(*refs, has_ctx, grp, qblk, scale):
    if has_ctx:
        (q_ref, k_ref, v_ref, qn_ref, kn_ref, ck_ref, cv_ref, cos_ref, sin_ref, _prev, o_ref) = refs
    else:
        (q_ref, k_ref, v_ref, qn_ref, kn_ref, o_ref, ko_ref, vo_ref) = refs
    seq, hd = k_ref.shape
    k = _rms(k_ref[...], kn_ref[...])
    v = v_ref[...]
    if has_ctx:
        k = _rope(k, cos_ref[...], sin_ref[...])
        ck = ck_ref[...].astype(BF16)
        cv = cv_ref[...].astype(BF16)
    else:
        ko_ref[...] = k
        vo_ref[...] = v
    kb = k.astype(BF16)
    vb = v.astype(BF16)

    for g in range(grp):
        def body(i, carry, g=g):
            r0 = pl.multiple_of(i * qblk, qblk)
            q = _rms(q_ref[pl.ds(r0, qblk), g * hd:(g + 1) * hd].astype(F32), qn_ref[...])
            if has_ctx:
                q = _rope(q, cos_ref[pl.ds(r0, qblk), :], sin_ref[pl.ds(r0, qblk), :])
            qb = q.astype(BF16)
            s2 = lax.dot_general(qb, kb, _NT, preferred_element_type=F32) * scale
            m = jnp.max(s2, axis=-1, keepdims=True)
            if has_ctx:
                s1 = lax.dot_general(qb, ck, _NT, preferred_element_type=F32) * scale
                m = jnp.maximum(m, jnp.max(s1, axis=-1, keepdims=True))
                p1 = jnp.exp(s1 - m)
            p2 = jnp.exp(s2 - m)
            den = jnp.sum(p2, axis=-1, keepdims=True)
            acc = jnp.dot(p2.astype(BF16), vb, preferred_element_type=F32)
            if has_ctx:
                den = den + jnp.sum(p1, axis=-1, keepdims=True)
                acc = acc + jnp.dot(p1.astype(BF16), cv, preferred_element_type=F32)
            o_ref[pl.ds(r0, qblk), g * hd:(g + 1) * hd] = (acc / den).astype(o_ref.dtype)
            return carry

        lax.fori_loop(0, seq // qblk, body, 0)


def _attn(proj16, proj32, qn_g, kn_g, ctx, layer_idx, out_prev, n_seq, seq, row_blk0, heads, kv_heads, hd,
          col_q, col_kv):
    t = proj16.shape[0]
    grp = heads // kv_heads
    has_ctx = ctx is not None
    cq = col_q // grp
    ck0 = col_kv
    cv0 = ck0 + kv_heads
    in_specs = [pl.BlockSpec((seq, grp * hd), lambda b, h: (row_blk0 + b, cq + h)),
                pl.BlockSpec((seq, hd), lambda b, h: (row_blk0 + b, ck0 + h)),
                pl.BlockSpec((seq, hd), lambda b, h: (row_blk0 + b, cv0 + h)),
                pl.BlockSpec((1, hd), lambda b, h: (0, 0)),
                pl.BlockSpec((1, hd), lambda b, h: (0, 0))]
    args = [proj16, proj32, proj32, qn_g.reshape(1, hd), kn_g.reshape(1, hd)]
    out_specs = [pl.BlockSpec((seq, grp * hd), lambda b, h: (row_blk0 + b, h))]
    out_shape = [jax.ShapeDtypeStruct((t, heads * hd), BF16)]
    aliases = {}
    if has_ctx:
        cache_k, cache_v, cos, sin_signed = ctx
        nreq, nlay, past = cache_k.shape[:3]
        cache_k = cache_k.reshape(nreq, nlay, past, kv_heads * hd)
        cache_v = cache_v.reshape(nreq, nlay, past, kv_heads * hd)
        spec_c = pl.BlockSpec((None, None, past, hd), lambda b, h: (b, layer_idx, 0, h))
        in_specs += [spec_c, spec_c,
                     pl.BlockSpec((seq, hd), lambda b, h: (0, 0)),
                     pl.BlockSpec((seq, hd), lambda b, h: (0, 0)),
                     pl.BlockSpec(memory_space=pl.ANY)]
        args += [cache_k, cache_v, cos, sin_signed, out_prev]
        aliases = {len(args) - 1: 0}
    else:
        out_specs += [pl.BlockSpec((seq, hd), lambda b, h: (b, h)),
                      pl.BlockSpec((seq, hd), lambda b, h: (b, h))]
        out_shape += [jax.ShapeDtypeStruct((n_seq * seq, kv_heads * hd), F32)] * 2
    return pl.pallas_call(
        functools.partial(_attn_kernel, has_ctx=has_ctx, grp=grp, qblk=min(seq, 256), scale=float(hd) ** -0.5),
        grid=(n_seq, kv_heads),
        in_specs=in_specs,
        out_specs=out_specs,
        out_shape=out_shape,
        input_output_aliases=aliases,
        compiler_params=_cparams("arbitrary", "arbitrary"),
        name="gqa",
    )(*args)


def _ret_kernel(*refs, nchunk, has_ctx, kscale):
    if has_ctx:
        (lg_ref, q_ref, k_ref, v_ref, g_ref, gn_ref, s0_ref, cos_ref, sin_ref, _prev,
         o_ref, sfin_ref, sb_ref) = refs
    else:
        (lg_ref, q_ref, k_ref, v_ref, g_ref, gn_ref, o_ref, sfin_ref, sb_ref) = refs
    cs = RET_CHUNK
    kd_dim = q_ref.shape[1]
    h = pl.program_id(1)
    lgf = lg_ref[0, h]
    lgb = lg_ref[1, h]
    row = lax.broadcasted_iota(jnp.int32, (cs, cs), 0)
    col = lax.broadcasted_iota(jnp.int32, (cs, cs), 1)
    rel = (row - col).astype(F32)
    dmat = (jnp.exp(jnp.where(rel >= 0, rel * lgf, -jnp.inf))
            + jnp.exp(jnp.where(rel <= 0, -rel * lgb, -jnp.inf)))
    j = lax.broadcasted_iota(jnp.int32, (cs, 1), 0).astype(F32)
    one = jnp.ones((1, 1), F32)
    dec_f = jnp.exp(one * (cs * lgf))
    dec_b = jnp.exp(one * (cs * lgb))

    def load_qk(ref, r0, mul):
        x = ref[pl.ds(r0, cs), :].astype(F32) * mul
        if has_ctx:
            half = kd_dim // 2
            x1, x2 = x[:, :half], x[:, half:]
            cos = cos_ref[pl.ds(r0, cs), :]
            sin = sin_ref[pl.ds(r0, cs), :]
            x = jnp.concatenate([x1 * cos - x2 * sin, x2 * cos + x1 * sin], axis=1)
        return x

    def bwd_body(i, st):
        n = nchunk - 1 - i
        r0 = pl.multiple_of(n * cs, cs)
        sb_ref[n] = st
        k = load_qk(k_ref, r0, kscale)
        v = v_ref[pl.ds(r0, cs), :]
        return st * dec_b + _bdot((k * jnp.exp(j * lgb)).T, v)

    st_b0 = s0_ref[1] if has_ctx else jnp.zeros(sfin_ref.shape[1:], F32)
    st_b = lax.fori_loop(0, nchunk, bwd_body, st_b0)
    sfin_ref[1] = st_b

    def fwd_body(n, st):
        r0 = pl.multiple_of(n * cs, cs)
        q = load_qk(q_ref, r0, 1.0)
        k = load_qk(k_ref, r0, kscale)
        v = v_ref[pl.ds(r0, cs), :]
        a = _bdot_nt(q, k) * dmat
        o = _bdot(a, v)
        if has_ctx or nchunk > 1:
            o = o + _bdot(q * jnp.exp((j + 1.0) * lgf), st)
            o = o + _bdot(q * jnp.exp((cs - j) * lgb), sb_ref[n])
        mu = jnp.mean(o, axis=-1, keepdims=True)
        var = jnp.mean(jnp.square(o - mu), axis=-1, keepdims=True)
        y = (o - mu) * lax.rsqrt(var + EPS) * gn_ref[0]
        o_ref[pl.ds(r0, cs), :] = (_silu(g_ref[pl.ds(r0, cs), :].astype(F32)) * y).astype(o_ref.dtype)
        return st * dec_f + _bdot((k * jnp.exp((cs - 1.0 - j) * lgf)).T, v)

    st_f0 = s0_ref[0] if has_ctx else jnp.zeros(sfin_ref.shape[1:], F32)
    sfin_ref[0] = lax.fori_loop(0, nchunk, fwd_body, st_f0)


def _retention(proj, log_gamma, gn_g, ctx, layer_idx, out_prev, n_seq, seq, row_blk0, heads, key_d, val_d):
    t = proj.shape[0]
    has_ctx = ctx is not None
    nq = heads * key_d
    cq, ck, cv, cg = 0, nq // key_d, 2 * nq // val_d, (2 * nq + heads * val_d) // val_d
    in_specs = [pl.BlockSpec(memory_space=pltpu.SMEM),
                pl.BlockSpec((seq, key_d), lambda b, h: (row_blk0 + b, cq + h)),
                pl.BlockSpec((seq, key_d), lambda b, h: (row_blk0 + b, ck + h)),
                pl.BlockSpec((seq, val_d), lambda b, h: (row_blk0 + b, cv + h)),
                pl.BlockSpec((seq, val_d), lambda b, h: (row_blk0 + b, cg + h)),
                pl.BlockSpec((1, 1, val_d), lambda b, h: (h, 0, 0))]
    args = [log_gamma, proj, proj, proj, proj, gn_g.reshape(heads, 1, val_d)]
    aliases = {}
    if has_ctx:
        s0, cos, sin = ctx
        in_specs += [pl.BlockSpec((None, None, 2, None, key_d, val_d), lambda b, h: (b, layer_idx, 0, h, 0, 0)),
                     pl.BlockSpec((seq, key_d // 2), lambda b, h: (0, 0)),
                     pl.BlockSpec((seq, key_d // 2), lambda b, h: (0, 0)),
                     pl.BlockSpec(memory_space=pl.ANY)]
        args += [s0, cos, sin, out_prev]
        aliases = {len(args) - 1: 0}
    nchunk = seq // RET_CHUNK
    return pl.pallas_call(
        functools.partial(_ret_kernel, nchunk=nchunk, has_ctx=has_ctx, kscale=float(key_d) ** -0.5),
        grid=(n_seq, heads),
        in_specs=in_specs,
        out_specs=[pl.BlockSpec((seq, val_d), lambda b, h: (row_blk0 + b, h)),
                   pl.BlockSpec((None, 2, None, key_d, val_d), lambda b, h: (b, 0, h, 0, 0))],
        out_shape=[jax.ShapeDtypeStruct((t, heads * val_d), BF16),
                   jax.ShapeDtypeStruct((n_seq, 2, heads, key_d, val_d), F32)],
        scratch_shapes=[pltpu.VMEM((nchunk, key_d, val_d), F32)],
        input_output_aliases=aliases,
        compiler_params=_cparams("arbitrary", "arbitrary"),
        name="retention",
    )(*args)


def _outproj_kernel(*refs, n_in, n_x, n_ctx_tiles, n_groups, n_experts):
    mix_refs = refs[:n_in]
    w_refs = refs[n_in:2 * n_in]
    x_refs = refs[2 * n_in:2 * n_in + n_x]
    mod_ref, g2_ref, wr_ref, br_ref, xn_ref, h2_ref, eidx_ref, ew_ref = refs[2 * n_in + n_x:]
    acc = _bdot(mix_refs[0][...], w_refs[0][...])
    for mref, wref in zip(mix_refs[1:], w_refs[1:]):
        acc = acc + _bdot(mref[...], wref[...])
    xn = _token_rows(x_refs, pl.program_id(0), n_ctx_tiles) + mod_ref[0, 2:3, :] * acc
    xn_ref[...] = xn
    h2 = _rms(xn, g2_ref[...]) * (1.0 + mod_ref[0, 4:5, :]) + mod_ref[0, 3:4, :]
    h2_ref[...] = h2.astype(BF16).reshape(h2_ref.shape)
    h_hi = h2.astype(BF16)
    h_lo = (h2 - h_hi.astype(F32)).astype(BF16)
    lg = (jnp.dot(h_hi, wr_ref[0], preferred_element_type=F32)
          + jnp.dot(h_hi, wr_ref[1], preferred_element_type=F32)
          + jnp.dot(h_lo, wr_ref[0], preferred_element_type=F32)) + br_ref[...]

    per_grp = n_experts // n_groups
    lane = lax.broadcasted_iota(jnp.int32, lg.shape, 1).astype(F32)
    big = float(lg.shape[1])
    ninf = -jnp.inf
    gl = jnp.where(lane < n_groups, lg, ninf)
    gmax = jnp.max(gl, axis=-1, keepdims=True)
    p_grp = 1.0 / jnp.sum(jnp.exp(gl - gmax), axis=-1, keepdims=True)
    gidx = jnp.min(jnp.where(gl == gmax, lane, big), axis=-1, keepdims=True)
    in_grp = ((lane >= n_groups) & (lane < n_groups + n_experts)
              & (jnp.floor((lane - n_groups) / per_grp) == gidx))
    el = jnp.where(in_grp, lg, ninf)
    emax = jnp.max(el, axis=-1, keepdims=True)
    esum = jnp.sum(jnp.exp(el - emax), axis=-1, keepdims=True)
    i1 = jnp.min(jnp.where(el == emax, lane, big), axis=-1, keepdims=True)
    el2 = jnp.where(lane == i1, ninf, el)
    emax2 = jnp.max(el2, axis=-1, keepdims=True)
    i2 = jnp.min(jnp.where(el2 == emax2, lane, big), axis=-1, keepdims=True)
    p1 = 1.0 / esum
    p2 = jnp.exp(emax2 - emax) / esum
    w1 = p1 / (p1 + p2) * p_grp
    w2 = p2 / (p1 + p2) * p_grp
    eidx_ref[...] = jnp.where(lane == 0.0, i1 - n_groups, jnp.where(lane == 1.0, i2 - n_groups, 0.0)).astype(jnp.int32)
    ew_ref[...] = jnp.where(lane == 0.0, w1, jnp.where(lane == 1.0, w2, 0.0))


def _out_proj(mixes, w_parts, x, mod, g2, w_router, b_router, n_ctx_tiles, tiles_per_req, n_groups, n_experts):
    t = mixes[0].shape[0]
    d = w_parts[0].shape[1]
    tm = TOKEN_TILE
    n_in = len(mixes)
    mrow = functools.partial(_tile_mod_row, n_ctx_tiles=n_ctx_tiles, tiles_per_req=tiles_per_req)
    x_specs, x_args = _token_specs(x, tm, n_ctx_tiles, lambda i: i)
    in_specs = ([pl.BlockSpec((tm, m.shape[1]), lambda i: (i, 0)) for m in mixes]
                + [pl.BlockSpec(w.shape, lambda i: (0, 0)) for w in w_parts]
                + x_specs
                + [pl.BlockSpec((1, 6, d), lambda i: (mrow(i), 0, 0)),
                   pl.BlockSpec((1, d), lambda i: (0, 0)),
                   pl.BlockSpec(w_router.shape, lambda i: (0, 0, 0)),
                   pl.BlockSpec((1, 128), lambda i: (0, 0))])
    return pl.pallas_call(
        functools.partial(_outproj_kernel, n_in=n_in, n_x=len(x_args), n_ctx_tiles=n_ctx_tiles,
                          n_groups=n_groups, n_experts=n_experts),
        grid=(t // tm,),
        in_specs=in_specs,
        out_specs=[pl.BlockSpec((tm, d), lambda i: (i, 0)),
                   pl.BlockSpec((tm, d // 128, 128), lambda i: (i, 0, 0)),
                   pl.BlockSpec((tm, 128), lambda i: (i, 0)),
                   pl.BlockSpec((tm, 128), lambda i: (i, 0))],
        out_shape=[jax.ShapeDtypeStruct((t, d), F32),
                   jax.ShapeDtypeStruct((t, d // 128, 128), BF16),
                   jax.ShapeDtypeStruct((t, 128), jnp.int32),
                   jax.ShapeDtypeStruct((t, 128), F32)],
        compiler_params=_cparams("arbitrary"),
        name="out_proj_router",
    )(*mixes, *w_parts, *x_args, mod, g2.reshape(1, d), w_router, b_router)


def _dispatch_plan(e1, e2, n_experts, tile):
    t = e1.shape[0]
    e = jnp.concatenate([e1, e2])
    onehot = (e[:, None] == jnp.arange(n_experts, dtype=jnp.int32)[None, :]).astype(jnp.int32)
    csum = jnp.cumsum(onehot, axis=0)
    rank = jnp.take_along_axis(csum, e[:, None], axis=1)[:, 0] - 1
    counts = csum[-1]
    padded = ((counts + tile - 1) // tile) * tile
    ends = jnp.cumsum(padded)
    starts = ends - padded
    pos = starts[e] + rank
    n_tiles = (2 * t) // tile + n_experts
    n_used = (ends[-1] // tile).astype(jnp.int32)
    tile_start = jnp.arange(n_tiles, dtype=jnp.int32) * tile
    probe = jnp.minimum(tile_start, ends[-1] - 1)
    tile_expert = jnp.sum((ends[None, :] <= probe[:, None]).astype(jnp.int32), axis=1)
    return tile_expert, n_used.reshape(1), ends.astype(jnp.int32), pos[:t], pos[t:], n_tiles


def _row_source_kernel(ends_ref, p1_ref, p2_ref, o_ref, *, n_tok, n_rows, tile, n_experts):
    def clear8(j, c):
        base = lax.rem(j * 8, n_tok)
        for u in range(8):
            o_ref[j * 8 + u] = base + u
        return c

    def per_expert(e, c):
        first = lax.shift_right_logical(jnp.maximum(ends_ref[e] - tile, 0), 3)
        lax.fori_loop(first, first + tile // 8, clear8, 0)
        return c

    g = pl.program_id(0)

    @pl.when(g == 0)
    def _():
        lax.fori_loop(0, n_experts, per_expert, 0)
        lax.fori_loop(lax.shift_right_logical(ends_ref[n_experts - 1], 3), n_rows // 8, clear8, 0)

    chunk = p1_ref.shape[2]

    def place(r, c):
        t = g * chunk + r
        o_ref[p1_ref[0, 0, r]] = t
        o_ref[p2_ref[0, 0, r]] = t
        return c

    lax.fori_loop(0, chunk, place, 0, unroll=8)


def _row_source(ends, pos1, pos2, n_rows, tile):
    n_tok = pos1.shape[0]
    chunk = ROW_SOURCE_CHUNK
    smem = pl.BlockSpec(memory_space=pltpu.SMEM)
    pos_spec = pl.BlockSpec((1, 1, chunk), lambda g: (g, 0, 0), memory_space=pltpu.SMEM)
    return pl.pallas_call(
        functools.partial(_row_source_kernel, n_tok=n_tok, n_rows=n_rows, tile=tile, n_experts=ends.shape[0]),
        grid=(n_tok // chunk,),
        in_specs=[smem, pos_spec, pos_spec],
        out_specs=smem,
        out_shape=jax.ShapeDtypeStruct((n_rows,), jnp.int32),
        compiler_params=_cparams("arbitrary"),
        name="moe_row_source",
    )(ends, pos1.reshape(n_tok // chunk, 1, chunk), pos2.reshape(n_tok // chunk, 1, chunk))


def _moe_kernel(te_ref, nu_ref, *refs, tm, ahead):
    tok_refs = refs[:ahead + 1]
    h_hbm, wg_ref, wu_ref, wd_ref, o_ref, wg_bf, wu_bf, wd_bf, buf, sem = refs[ahead + 1:]
    i = pl.program_id(0)
    nslot = ahead + 1
    slot = lax.rem(i, nslot)
    n_used = nu_ref[0]

    def issue(tile, tok_ref):
        s = lax.rem(tile, nslot)

        def body(j, c):
            for u in range(2):
                r = 2 * j + u
                pltpu.make_async_copy(h_hbm.at[tok_ref[0, 0, r]], buf.at[s, r], sem.at[s]).start(priority=u)
            return c

        lax.fori_loop(0, tm // 2, body, 0, unroll=4)

    @pl.when(i == 0)
    def _():
        for j in range(ahead):
            @pl.when(j < n_used)
            def _():
                issue(jnp.int32(j), tok_refs[j])

    new_expert = jnp.logical_or(i == 0, te_ref[i] != te_ref[jnp.maximum(i - 1, 0)])

    @pl.when(new_expert)
    def _():
        wg_bf[...] = wg_ref[...].astype(BF16)
        wu_bf[...] = wu_ref[...].astype(BF16)
        wd_bf[...] = wd_ref[...].astype(BF16)

    def compute(issue_next):
        s_next = lax.rem(i + ahead, nslot)
        quarter = tm // 4

        def start_rows(part):
            if issue_next:
                for r in range(part * quarter, (part + 1) * quarter):
                    pltpu.make_async_copy(h_hbm.at[tok_refs[ahead][0, 0, r]], buf.at[s_next, r],
                                          sem.at[s_next]).start(priority=r % 2)

        pltpu.make_async_copy(h_hbm.at[pl.ds(0, tm)], buf.at[slot], sem.at[slot]).wait()
        start_rows(0)
        x = buf[slot].reshape(tm, -1)
        g = jnp.dot(x, wg_bf[...], preferred_element_type=F32)
        start_rows(1)
        u = jnp.dot(x, wu_bf[...], preferred_element_type=F32)
        start_rows(2)
        act = _silu(g) * u
        y = jnp.dot(act.astype(BF16), wd_bf[...], preferred_element_type=F32)
        start_rows(3)
        o_ref[...] = y.astype(BF16).reshape(o_ref.shape)

    @pl.when(i + ahead < n_used)
    def _():
        compute(True)

    @pl.when(jnp.logical_and(i + ahead >= n_used, i < n_used))
    def _():
        compute(False)

    @pl.when(i >= n_used)
    def _():
        o_ref[...] = jnp.zeros_like(o_ref)


def _moe_experts(h3, src_tok, tile_expert, n_used, w_gate, w_up, w_down, layer_idx, tm):
    t, sub, lanes = h3.shape
    d = sub * lanes
    n_tiles = src_tok.shape[0] // tm
    f = w_gate.shape[-1]
    ahead = MOE_GATHER_AHEAD
    tok = src_tok.reshape(n_tiles, 1, tm)

    def tok_spec(k):
        return pl.BlockSpec((1, 1, tm), lambda i, te, nu: (jnp.minimum(i + k, n_tiles - 1), 0, 0),
                            memory_space=pltpu.SMEM)

    return pl.pallas_call(
        functools.partial(_moe_kernel, tm=tm, ahead=ahead),
        grid_spec=pltpu.PrefetchScalarGridSpec(
            num_scalar_prefetch=2,
            grid=(n_tiles,),
            in_specs=[tok_spec(k) for k in range(ahead + 1)] + [
                      pl.BlockSpec(memory_space=pl.ANY),
                      pl.BlockSpec((None, None, d, f), lambda i, te, nu: (layer_idx, te[i], 0, 0)),
                      pl.BlockSpec((None, None, d, f), lambda i, te, nu: (layer_idx, te[i], 0, 0)),
                      pl.BlockSpec((None, None, f, d), lambda i, te, nu: (layer_idx, te[i], 0, 0))],
            out_specs=pl.BlockSpec((tm, sub, lanes), lambda i, te, nu: (i, 0, 0)),
            scratch_shapes=[pltpu.VMEM((d, f), BF16), pltpu.VMEM((d, f), BF16), pltpu.VMEM((f, d), BF16),
                            pltpu.VMEM((ahead + 1, tm, sub, lanes), BF16),
                            pltpu.SemaphoreType.DMA((ahead + 1,))]),
        out_shape=jax.ShapeDtypeStruct((n_tiles * tm, sub, lanes), BF16),
        compiler_params=_cparams("arbitrary"),
        name="moe_experts",
    )(tile_expert, n_used, *([tok] * (ahead + 1)), h3, w_gate, w_up, w_down)


def _combine_kernel(p1_ref, p2_ref, p1n_ref, p2n_ref, y_hbm, x_ref, ew_ref, mod_ref, gf_ref, *rest,
                    tm, n_steps, n_ctx_tiles, final_norm):
    out_refs, (buf, sem) = rest[:-2], rest[-2:]
    i = pl.program_id(0)
    slot = lax.rem(i, 2)

    def issue(s, pa_ref, pb_ref):
        def body(r, c):
            pltpu.make_async_copy(y_hbm.at[pa_ref[0, 0, r]], buf.at[s, 0, r], sem.at[s, 0]).start(priority=0)
            pltpu.make_async_copy(y_hbm.at[pb_ref[0, 0, r]], buf.at[s, 1, r], sem.at[s, 1]).start(priority=1)
            return c

        lax.fori_loop(0, tm, body, 0, unroll=8)

    @pl.when(i == 0)
    def _():
        issue(0, p1_ref, p2_ref)

    @pl.when(i + 1 < n_steps)
    def _():
        issue(1 - slot, p1n_ref, p2n_ref)

    for k in range(2):
        pltpu.make_async_copy(y_hbm.at[pl.ds(0, tm)], buf.at[slot, k], sem.at[slot, k]).wait()
    ew = ew_ref[...]
    y = (ew[:, 0:1] * buf[slot, 0].reshape(tm, -1).astype(F32)
         + ew[:, 1:2] * buf[slot, 1].reshape(tm, -1).astype(F32))
    x = x_ref[...] + mod_ref[0, 5:6, :] * y
    if not final_norm:
        out_refs[0][...] = x
    else:
        x = _rms(x, gf_ref[...])
        ctx_ref, lat_ref = out_refs

        @pl.when(i < n_ctx_tiles)
        def _():
            ctx_ref[...] = x

        @pl.when(i >= n_ctx_tiles)
        def _():
            lat_ref[...] = x


def _moe_combine(y_sorted, pos1, pos2, x, ew, mod, final_g, n_ctx_tiles, tiles_per_req, final_norm):
    t, d = x.shape
    tm = TOKEN_TILE
    n_steps = t // tm
    mrow = functools.partial(_tile_mod_row, n_ctx_tiles=n_ctx_tiles, tiles_per_req=tiles_per_req)
    smem_rows = pl.BlockSpec((1, 1, tm), lambda i: (i, 0, 0), memory_space=pltpu.SMEM)
    smem_next = pl.BlockSpec((1, 1, tm), lambda i: (jnp.minimum(i + 1, n_steps - 1), 0, 0),
                             memory_space=pltpu.SMEM)
    p1 = pos1.reshape(n_steps, 1, tm)
    p2 = pos2.reshape(n_steps, 1, tm)
    if final_norm:
        out_specs = [pl.BlockSpec((tm, d), lambda i: (jnp.minimum(i, n_ctx_tiles - 1), 0)),
                     pl.BlockSpec((tm, d), lambda i: (jnp.maximum(i - n_ctx_tiles, 0), 0))]
        out_shape = [jax.ShapeDtypeStruct((n_ctx_tiles * tm, d), F32),
                     jax.ShapeDtypeStruct((t - n_ctx_tiles * tm, d), F32)]
    else:
        out_specs = pl.BlockSpec((tm, d), lambda i: (i, 0))
        out_shape = jax.ShapeDtypeStruct((t, d), F32)
    return pl.pallas_call(
        functools.partial(_combine_kernel, tm=tm, n_steps=n_steps, n_ctx_tiles=n_ctx_tiles,
                          final_norm=final_norm),
        grid=(n_steps,),
        in_specs=[smem_rows, smem_rows, smem_next, smem_next,
                  pl.BlockSpec(memory_space=pl.ANY),
                  pl.BlockSpec((tm, d), lambda i: (i, 0)),
                  pl.BlockSpec((tm, 128), lambda i: (i, 0)),
                  pl.BlockSpec((1, 6, d), lambda i: (mrow(i), 0, 0)),
                  pl.BlockSpec((1, d), lambda i: (0, 0))],
        out_specs=out_specs,
        out_shape=out_shape,
        scratch_shapes=[pltpu.VMEM((2, 2, tm) + y_sorted.shape[1:], y_sorted.dtype),
                        pltpu.SemaphoreType.DMA((2, 2))],
        compiler_params=_cparams("arbitrary"),
        name="moe_combine",
    )(p1, p2, p1, p2, y_sorted, x, ew, mod, final_g.reshape(1, d))


def _axial_angles(seq, head_dim):
    rows = seq // GRID_W
    row = jnp.repeat(jnp.arange(rows), GRID_W).astype(F32)
    col = jnp.tile(jnp.arange(GRID_W), rows).astype(F32)
    n_freq = head_dim // 4
    inv = ROPE_BASE ** (-jnp.arange(n_freq, dtype=F32) / n_freq)
    return jnp.concatenate([row[:, None] * inv, col[:, None] * inv], axis=-1)


def kernel(x_prompt, x_sample, cache_attn_k, cache_attn_v, state_hgrn, state_ret, c, c_ctx, mod_w, mod_b, norm1_g, norm2_g, even_w_in, even_w_out, hgrn_lb_logits, hgrn_norm_g, attn_qn_g, attn_kn_g, odd_w_in, odd_w_out, ret_decay_logit, ret_gn_g, moe_w_group, moe_b_group, moe_w_expert, moe_b_expert, moe_w_gate, moe_w_up, moe_w_down, final_norm_g):
    batch, seq, d = x_prompt.shape
    dec_batch, dec_seq, _ = x_sample.shape
    depth = mod_w.shape[0]
    heads_a, val_a = hgrn_norm_g.shape[1:]
    key_a = hgrn_lb_logits.shape[2] // heads_a
    hd_b = attn_qn_g.shape[1]
    kv_b = cache_attn_k.shape[3]
    heads_b = (even_w_out.shape[1] - heads_a * val_a) // hd_b
    heads_c, val_c = ret_gn_g.shape[1:]
    key_c = state_ret.shape[4]
    n_experts = moe_w_expert.shape[2]
    n_groups = moe_w_group.shape[2]
    tm = TOKEN_TILE
    n_ctx = batch * seq
    n_ctx_tiles = n_ctx // tm
    tiles_per_req = dec_seq // tm
    assert seq % tm == 0 and dec_seq % tm == 0 and n_ctx % dec_seq == 0
    assert key_a == 128 and val_a == 128 and hd_b == 128 and n_groups + n_experts <= 128

    x = (x_prompt.reshape(n_ctx, d), x_sample.reshape(dec_batch * dec_seq, d))
    t = n_ctx + dec_batch * dec_seq

    n_vec = 1 + dec_batch
    rows = -(-n_vec // 8) * 8
    cvecs = jnp.zeros((rows, d), F32).at[0].set(c_ctx).at[1:n_vec].set(c)
    mods = _ada_mod(cvecs, mod_w, mod_b).reshape(depth, rows, 6, d)

    lower_bounds = jnp.cumsum(jax.nn.softmax(hgrn_lb_logits.astype(F32), axis=0), axis=0)
    log_gamma = jax.nn.log_sigmoid(ret_decay_logit.astype(F32))

    ang_b = _axial_angles(dec_seq, hd_b)
    cos_b = jnp.concatenate([jnp.cos(ang_b), jnp.cos(ang_b)], axis=-1)
    sin_b = jnp.concatenate([-jnp.sin(ang_b), jnp.sin(ang_b)], axis=-1)
    ang_c = _axial_angles(dec_seq, key_c)
    cos_c, sin_c = jnp.cos(ang_c), jnp.sin(ang_c)

    pad = 128 - n_groups - n_experts
    new_k = new_v = new_sh = new_sr = None
    assert depth == 2, "one HGRN2/GQA layer followed by one retention layer"
    for l in range(depth):
        mod = mods[l]
        if l % 2 == 0:
            e = l // 2
            na, nva, nqb, nkv = heads_a * key_a, heads_a * val_a, heads_b * hd_b, kv_b * hd_b
            o_zf, o_kb = na + nva, 3 * na + 2 * nva + nqb
            w_in = even_w_in[e].astype(BF16)
            proj16, proj32 = _in_proj(x, norm1_g[l], mod, w_in, t, n_ctx_tiles, tiles_per_req, w_in.shape[1],
                                      f32_cols=((o_zf, o_zf + 2 * na), (o_kb, o_kb + 2 * nkv)))
            lb = lower_bounds[l]
            oa, s_ctx = _hgrn(proj16, proj32, lb, hgrn_norm_g[e], None, e, None, batch, seq, 0,
                              heads_a, key_a, val_a)
            oa, _ = _hgrn(proj16, proj32, lb, hgrn_norm_g[e], state_hgrn, e, oa, dec_batch, dec_seq,
                          n_ctx // dec_seq, heads_a, key_a, val_a)
            col_q, col_kv = (na + 2 * nva) // hd_b, (2 * na) // hd_b
            ob, kc, vc = _attn(proj16, proj32, attn_qn_g[e], attn_kn_g[e], None, e, None, batch, seq, 0,
                               heads_b, kv_b, hd_b, col_q, col_kv)
            (ob,) = _attn(proj16, proj32, attn_qn_g[e], attn_kn_g[e],
                          (cache_attn_k, cache_attn_v, cos_b, sin_b), e, ob,
                          dec_batch, dec_seq, n_ctx // dec_seq, heads_b, kv_b, hd_b, col_q, col_kv)
            new_k = kc.reshape(batch, 1, seq, kv_b, hd_b)
            new_v = vc.reshape(batch, 1, seq, kv_b, hd_b)
            new_sh = s_ctx[:, None]
            w_out = even_w_out[e].astype(BF16)
            mixes = [oa, ob]
            w_parts = [w_out[:heads_a * val_a], w_out[heads_a * val_a:]]
        else:
            o = l // 2
            w_in = odd_w_in[o].astype(BF16)
            proj = _in_proj(x, norm1_g[l], mod, w_in, t, n_ctx_tiles, tiles_per_req, w_in.shape[1] // 2)
            y, s_ctx = _retention(proj, log_gamma[o], ret_gn_g[o], None, o, None, batch, seq, 0,
                                  heads_c, key_c, val_c)
            y, _ = _retention(proj, log_gamma[o], ret_gn_g[o], (state_ret, cos_c, sin_c), o, y,
                              dec_batch, dec_seq, n_ctx // dec_seq, heads_c, key_c, val_c)
            new_sr = s_ctx[:, None]
            mixes = [y]
            w_parts = [odd_w_out[o].astype(BF16)]

        w_router = jnp.concatenate([moe_w_group[l], moe_w_expert[l], jnp.zeros((d, pad), F32)], axis=1)
        w_router_hi = w_router.astype(BF16)
        w_router = jnp.stack([w_router_hi, (w_router - w_router_hi.astype(F32)).astype(BF16)])
        b_router = jnp.concatenate([moe_b_group[l], moe_b_expert[l], jnp.zeros((pad,), F32)]).reshape(1, 128)
        xn, h2, eidx, ew = _out_proj(mixes, w_parts, x, mod, norm2_g[l], w_router, b_router,
                                     n_ctx_tiles, tiles_per_req, n_groups, n_experts)
        tile_expert, n_used, ends, pos1, pos2, n_tiles = _dispatch_plan(
            eidx[:, 0], eidx[:, 1], n_experts, EXPERT_TILE)
        src_tok = _row_source(ends, pos1, pos2, n_tiles * EXPERT_TILE, EXPERT_TILE)
        y_sorted = _moe_experts(h2, src_tok, tile_expert, n_used, moe_w_gate, moe_w_up, moe_w_down, l, EXPERT_TILE)
        x = _moe_combine(y_sorted, pos1, pos2, xn, ew, mod, final_norm_g, n_ctx_tiles, tiles_per_req,
                         final_norm=(l == depth - 1))

    y_prompt = x[0].reshape(batch, seq, d)
    y_sample = x[1].reshape(dec_batch, dec_seq, d)
    return (y_prompt, y_sample, new_k, new_v, new_sh, new_sr)
```

```python
import functools

import numpy as np
import jax
import jax.numpy as jnp
from jax import lax
from jax.experimental import pallas as pl
from jax.experimental.pallas import tpu as pltpu

F32 = jnp.float32
BF16 = jnp.bfloat16
EPS = 1e-6
ROPE_BASE = 10000.0
GRID_W = 64
N_GROUPS = 4
TOP_K_INNER = 2
HGRN_BLOCK = 128
HGRN_DIAG = 8
RET_CHUNK = 256
TOKEN_TILE = 256
EXPERT_TILE = 256
MOE_GATHER_AHEAD = 2
ROW_SOURCE_CHUNK = 512
V7X_VMEM_LIMIT_BYTES = 56 * 1024 * 1024
HIGHEST = lax.Precision.HIGHEST
LOG2_E = 1.4426950408889634

_NT = (((1,), (1,)), ((), ()))


def _cparams(*sem):
    return pltpu.CompilerParams(dimension_semantics=sem, vmem_limit_bytes=V7X_VMEM_LIMIT_BYTES)


def _rms(x, g):
    return x * lax.rsqrt(jnp.mean(x * x, axis=-1, keepdims=True) + EPS) * g


def _silu(x):
    return x * jax.nn.sigmoid(x)


def _bdot(a, b):
    return jnp.dot(a.astype(BF16), b.astype(BF16), preferred_element_type=F32)


def _bdot_nt(a, b):
    return lax.dot_general(a.astype(BF16), b.astype(BF16), _NT, preferred_element_type=F32)


def _mod_kernel(c_ref, w_ref, b_ref, o_ref):
    a = _silu(c_ref[...])
    o_ref[0] = jnp.dot(a, w_ref[0], precision=HIGHEST, preferred_element_type=F32) + b_ref[0]


def _ada_mod(cvecs, mod_w, mod_b):
    depth, d, n = mod_w.shape
    tn = n // 4
    rows = cvecs.shape[0]
    return pl.pallas_call(
        _mod_kernel,
        grid=(depth, n // tn),
        in_specs=[pl.BlockSpec((rows, d), lambda l, j: (0, 0)),
                  pl.BlockSpec((1, d, tn), lambda l, j: (l, 0, j)),
                  pl.BlockSpec((1, 1, tn), lambda l, j: (l, 0, j))],
        out_specs=pl.BlockSpec((1, rows, tn), lambda l, j: (l, 0, j)),
        out_shape=jax.ShapeDtypeStruct((depth, rows, n), F32),
        compiler_params=_cparams("arbitrary", "arbitrary"),
        name="ada_mod",
    )(cvecs, mod_w, mod_b.reshape(depth, 1, n))


def _tile_mod_row(i, n_ctx_tiles, tiles_per_req):
    return jnp.where(i < n_ctx_tiles, 0, 1 + jnp.maximum(i - n_ctx_tiles, 0) // tiles_per_req)


def _token_specs(x, tm, n_ctx_tiles, step_of):
    if not isinstance(x, tuple):
        return [pl.BlockSpec((tm, x.shape[1]), lambda *g: (step_of(*g), 0))], (x,)
    d = x[0].shape[1]
    return ([pl.BlockSpec((tm, d), lambda *g: (jnp.minimum(step_of(*g), n_ctx_tiles - 1), 0)),
             pl.BlockSpec((tm, d), lambda *g: (jnp.maximum(step_of(*g) - n_ctx_tiles, 0), 0))], x)


def _token_rows(x_refs, i, n_ctx_tiles):
    if len(x_refs) == 1:
        return x_refs[0][...]
    return jnp.where(i < n_ctx_tiles, x_refs[0][...], x_refs[1][...])


def _inproj_kernel(*refs, n_x, n_ctx_tiles, f32_cols):
    x_refs = refs[:n_x]
    g_ref, mod_ref, w_ref = refs[n_x:n_x + 3]
    out_refs = refs[n_x + 3:]
    x = _token_rows(x_refs, pl.program_id(1), n_ctx_tiles)
    h = _rms(x, g_ref[...]) * (1.0 + mod_ref[0, 1:2, :]) + mod_ref[0, 0:1, :]
    acc = jnp.dot(h.astype(BF16), w_ref[...], preferred_element_type=F32)
    if not f32_cols:
        out_refs[0][...] = acc.astype(BF16)
        return
    n = acc.shape[1]
    at16 = at32 = start = 0
    for lo, hi in tuple(f32_cols) + ((n, n),):
        if lo > start:
            out_refs[0][:, at16:at16 + lo - start] = acc[:, start:lo].astype(BF16)
            at16 += lo - start
        if hi > lo:
            out_refs[1][:, at32:at32 + hi - lo] = acc[:, lo:hi]
            at32 += hi - lo
        start = hi


def _in_proj(x, g, mod, w_bf, t, n_ctx_tiles, tiles_per_req, tn, f32_cols=()):
    d, n = w_bf.shape
    tm = TOKEN_TILE
    mrow = functools.partial(_tile_mod_row, n_ctx_tiles=n_ctx_tiles, tiles_per_req=tiles_per_req)
    x_specs, x_args = _token_specs(x, tm, n_ctx_tiles, lambda j, i: i)
    if not f32_cols:
        out_specs = pl.BlockSpec((tm, tn), lambda j, i: (i, j))
        out_shape = jax.ShapeDtypeStruct((t, n), BF16)
    else:
        assert tn == n
        n32 = sum(hi - lo for lo, hi in f32_cols)
        out_specs = [pl.BlockSpec((tm, n - n32), lambda j, i: (i, 0)),
                     pl.BlockSpec((tm, n32), lambda j, i: (i, 0))]
        out_shape = [jax.ShapeDtypeStruct((t, n - n32), BF16), jax.ShapeDtypeStruct((t, n32), F32)]
    return pl.pallas_call(
        functools.partial(_inproj_kernel, n_x=len(x_args), n_ctx_tiles=n_ctx_tiles, f32_cols=tuple(f32_cols)),
        grid=(n // tn, t // tm),
        in_specs=x_specs + [pl.BlockSpec((1, d), lambda j, i: (0, 0)),
                            pl.BlockSpec((1, 6, d), lambda j, i: (mrow(i), 0, 0)),
                            pl.BlockSpec((d, tn), lambda j, i: (0, j))],
        out_specs=out_specs,
        out_shape=out_shape,
        compiler_params=_cparams("arbitrary", "arbitrary"),
        name="in_proj",
    )(*x_args, g.reshape(1, d), mod, w_bf)


def _hgrn_decay(z, lb, reverse):
    nb = HGRN_BLOCK
    row = lax.broadcasted_iota(jnp.int32, (nb, nb), 0)
    col = lax.broadcasted_iota(jnp.int32, (nb, nb), 1)
    tri = jnp.where((col >= row) if reverse else (col <= row), 1.0, 0.0).astype(BF16)
    logf = jnp.log(lb + (1.0 - lb) * jax.nn.sigmoid(z))
    k = (1.0 - lb) * jax.nn.sigmoid(-z)
    l2 = logf * LOG2_E
    hi = l2.astype(BF16)
    lo = (l2 - hi.astype(F32)).astype(BF16)
    c = jnp.dot(tri, hi, preferred_element_type=F32) + jnp.dot(tri, lo, preferred_element_type=F32)
    return c, k


def _hgrn_block(q, v, st_t, reverse, c_ref, k_ref, ecat):
    nb = HGRN_BLOCK
    row = lax.broadcasted_iota(jnp.int32, (nb, nb), 0)
    col = lax.broadcasted_iota(jnp.int32, (nb, nb), 1)
    c = c_ref[...]
    k = k_ref[...]
    edge = 0 if reverse else nb - 1
    c_edge = c_ref[edge:edge + 1, :]

    o = _bdot_nt(q * jnp.exp2(c), st_t)
    kd = k * jnp.exp2(c_edge - c)
    st_new = st_t * jnp.exp2(c_edge) + _bdot(v.T, kd)

    a = jnp.zeros((nb, nb), F32)
    m = nb // 2
    while m >= HGRN_DIAG:
        pieces = []
        for grp in range(nb // (2 * m)):
            idx = grp * 2 * m + (m if reverse else m - 1)
            pieces.append(jnp.broadcast_to(c_ref[idx:idx + 1, :], (2 * m, nb)))
        bnd = pieces[0] if len(pieces) == 1 else jnp.concatenate(pieces, axis=0)
        e = jnp.exp2(-jnp.abs(c - bnd))
        am = _bdot_nt(q * e, k * e)
        same = (row & ~(2 * m - 1)) == (col & ~(2 * m - 1))
        r_hi = (row & m) != 0
        c_hi = (col & m) != 0
        if reverse:
            mask = same & jnp.logical_not(r_hi) & c_hi
        else:
            mask = same & r_hi & jnp.logical_not(c_hi)
        a = jnp.where(mask, am, a)
        m //= 2

    dg = HGRN_DIAG
    rowi = lax.broadcasted_iota(jnp.int32, (dg, nb), 0)
    cols = []
    for s in range(dg):
        keep = (rowi <= s) if reverse else (rowi >= s)
        tiles = []
        for j in range(nb // dg):
            r = j * dg + s
            qj = q[j * dg:(j + 1) * dg, :]
            cj = c[j * dg:(j + 1) * dg, :]
            cs = jnp.broadcast_to(c_ref[r:r + 1, :], (dg, nb))
            ks = jnp.broadcast_to(k_ref[r:r + 1, :], (dg, nb))
            tiles.append(qj * ks * jnp.exp2(jnp.where(keep, cj - cs, -jnp.inf)))
        cols.append(jnp.concatenate(tiles, axis=0).astype(BF16))
    p = jnp.concatenate(cols, axis=1)
    red = jnp.dot(p, ecat, preferred_element_type=F32)
    a = jnp.where((row >> 3) == (col >> 3), red, a)
    o = o + _bdot(a, v)
    return o, st_new


def _hgrn_kernel(*refs, nblk, has_state, qscale):
    qa_ref, ia_ref, zf_ref, zb_ref, ga_ref, lb_ref, ng_ref, ecat_ref = refs[:8]
    s0_ref = refs[8] if has_state else None
    o_ref, sfin_ref, oacc_ref, c_scr, k_scr = refs[-5:]
    nb = HGRN_BLOCK

    def gates(direction, blk, slot):
        z_ref = zb_ref if direction == 1 else zf_ref
        r0 = pl.multiple_of(blk * nb, nb)
        c, k = _hgrn_decay(z_ref[pl.ds(r0, nb), :], lb_ref[direction:direction + 1, :], direction == 1)
        c_scr[direction, slot] = c
        k_scr[direction, slot] = k

    def block(direction, blk, slot, st_t):
        r0 = pl.multiple_of(blk * nb, nb)
        q = _silu(qa_ref[pl.ds(r0, nb), :].astype(F32)) * qscale
        v = ia_ref[pl.ds(r0, nb), :].astype(F32)
        o, st_new = _hgrn_block(q, v, st_t, direction == 1, c_scr.at[direction, slot],
                                k_scr.at[direction, slot], ecat_ref[...])
        oacc_ref[direction, pl.ds(r0, nb), :] = o
        return st_new

    gates(0, 0, 0)
    gates(1, nblk - 1, 0)

    def body(i, carry):
        st_f, st_b = carry
        slot = lax.rem(i, 2)
        st_f = block(0, i, slot, st_f)
        st_b = block(1, nblk - 1 - i, slot, st_b)
        nxt = jnp.minimum(i + 1, nblk - 1)
        gates(0, nxt, 1 - slot)
        gates(1, nblk - 1 - nxt, 1 - slot)
        return st_f, st_b

    if has_state:
        st0 = (s0_ref[0].T, s0_ref[1].T)
    else:
        zero = jnp.zeros((ia_ref.shape[1], qa_ref.shape[1]), F32)
        st0 = (zero, zero)
    st_f, st_b = lax.fori_loop(0, nblk, body, st0)
    sfin_ref[0] = st_f.T
    sfin_ref[1] = st_b.T

    o = oacc_ref[0] + oacc_ref[1]
    o_ref[...] = (_rms(o, ng_ref[0]) * _silu(ga_ref[...].astype(F32))).astype(o_ref.dtype)


def _hgrn(proj16, proj32, lb, norm_g, s0, layer_idx, out_prev, n_seq, seq, row_blk0, heads, key_d, val_d):
    t = proj16.shape[0]
    has_state = s0 is not None
    hk = heads * key_d // 128
    off_qa, off_ia, off_ga = 0, hk, 2 * hk
    off_zf, off_zb = 0, hk

    def col(off):
        return pl.BlockSpec((seq, 128), lambda b, h: (row_blk0 + b, off + h))

    lane = np.arange(key_d)[None, :]
    srow = np.repeat(np.arange(HGRN_DIAG), key_d)[:, None]
    ecat = jnp.asarray((lane % HGRN_DIAG == srow).astype(np.float32), dtype=BF16)
    in_specs = [col(off_qa), col(off_ia), col(off_zf), col(off_zb), col(off_ga),
                pl.BlockSpec((2, key_d), lambda b, h: (0, h)),
                pl.BlockSpec((1, 1, val_d), lambda b, h: (h, 0, 0)),
                pl.BlockSpec((HGRN_DIAG * key_d, key_d), lambda b, h: (0, 0))]
    args = [proj16, proj16, proj32, proj32, proj16, lb, norm_g.reshape(heads, 1, val_d), ecat]
    if has_state:
        in_specs.append(pl.BlockSpec((None, None, 2, None, key_d, val_d),
                                     lambda b, h: (b, layer_idx, 0, h, 0, 0)))
        args.append(s0)
    aliases = {}
    if out_prev is not None:
        in_specs.append(pl.BlockSpec(memory_space=pl.ANY))
        args.append(out_prev)
        aliases = {len(args) - 1: 0}

    kern = functools.partial(_hgrn_kernel, nblk=seq // HGRN_BLOCK, has_state=has_state,
                             qscale=float(key_d) ** -0.5)
    return pl.pallas_call(
        kern,
        grid=(n_seq, heads),
        in_specs=in_specs,
        out_specs=[pl.BlockSpec((seq, val_d), lambda b, h: (row_blk0 + b, h)),
                   pl.BlockSpec((None, 2, None, key_d, val_d), lambda b, h: (b, 0, h, 0, 0))],
        out_shape=[jax.ShapeDtypeStruct((t, heads * val_d), BF16),
                   jax.ShapeDtypeStruct((n_seq, 2, heads, key_d, val_d), F32)],
        scratch_shapes=[pltpu.VMEM((2, seq, val_d), F32),
                        pltpu.VMEM((2, 2, HGRN_BLOCK, key_d), F32),
                        pltpu.VMEM((2, 2, HGRN_BLOCK, key_d), F32)],
        input_output_aliases=aliases,
        compiler_params=_cparams("arbitrary", "arbitrary"),
        name="hgrn2",
    )(*args)


def _rope(x, cos, sin_signed):
    return x * cos + pltpu.roll(x, x.shape[-1] // 2, 1) * sin_signed


def _attn_kernel(*refs, has_ctx, grp, qblk, scale):
    if has_ctx:
        (q_ref, k_ref, v_ref, qn_ref, kn_ref, ck_ref, cv_ref, cos_ref, sin_ref, _prev, o_ref) = refs
    else:
        (q_ref, k_ref, v_ref, qn_ref, kn_ref, o_ref, ko_ref, vo_ref) = refs
    seq, hd = k_ref.shape
    k = _rms(k_ref[...], kn_ref[...])
    v = v_ref[...]
    if has_ctx:
        k = _rope(k, cos_ref[...], sin_ref[...])
        ck = ck_ref[...].astype(BF16)
        cv = cv_ref[...].astype(BF16)
    else:
        ko_ref[...] = k
        vo_ref[...] = v
    kb = k.astype(BF16)
    vb = v.astype(BF16)

    for g in range(grp):
        def body(i, carry, g=g):
            r0 = pl.multiple_of(i * qblk, qblk)
            q = _rms(q_ref[pl.ds(r0, qblk), g * hd:(g + 1) * hd].astype(F32), qn_ref[...])
            if has_ctx:
                q = _rope(q, cos_ref[pl.ds(r0, qblk), :], sin_ref[pl.ds(r0, qblk), :])
            qb = q.astype(BF16)
            s2 = lax.dot_general(qb, kb, _NT, preferred_element_type=F32) * scale
            m = jnp.max(s2, axis=-1, keepdims=True)
            if has_ctx:
                s1 = lax.dot_general(qb, ck, _NT, preferred_element_type=F32) * scale
                m = jnp.maximum(m, jnp.max(s1, axis=-1, keepdims=True))
                p1 = jnp.exp(s1 - m)
            p2 = jnp.exp(s2 - m)
            den = jnp.sum(p2, axis=-1, keepdims=True)
            acc = jnp.dot(p2.astype(BF16), vb, preferred_element_type=F32)
            if has_ctx:
                den = den + jnp.sum(p1, axis=-1, keepdims=True)
                acc = acc + jnp.dot(p1.astype(BF16), cv, preferred_element_type=F32)
            o_ref[pl.ds(r0, qblk), g * hd:(g + 1) * hd] = (acc / den).astype(o_ref.dtype)
            return carry

        lax.fori_loop(0, seq // qblk, body, 0)


def _attn(proj16, proj32, qn_g, kn_g, ctx, layer_idx, out_prev, n_seq, seq, row_blk0, heads, kv_heads, hd,
          col_q, col_kv):
    t = proj16.shape[0]
    grp = heads // kv_heads
    has_ctx = ctx is not None
    cq = col_q // grp
    ck0 = col_kv
    cv0 = ck0 + kv_heads
    in_specs = [pl.BlockSpec((seq, grp * hd), lambda b, h: (row_blk0 + b, cq + h)),
                pl.BlockSpec((seq, hd), lambda b, h: (row_blk0 + b, ck0 + h)),
                pl.BlockSpec((seq, hd), lambda b, h: (row_blk0 + b, cv0 + h)),
                pl.BlockSpec((1, hd), lambda b, h: (0, 0)),
                pl.BlockSpec((1, hd), lambda b, h: (0, 0))]
    args = [proj16, proj32, proj32, qn_g.reshape(1, hd), kn_g.reshape(1, hd)]
    out_specs = [pl.BlockSpec((seq, grp * hd), lambda b, h: (row_blk0 + b, h))]
    out_shape = [jax.ShapeDtypeStruct((t, heads * hd), BF16)]
    aliases = {}
    if has_ctx:
        cache_k, cache_v, cos, sin_signed = ctx
        nreq, nlay, past = cache_k.shape[:3]
        cache_k = cache_k.reshape(nreq, nlay, past, kv_heads * hd)
        cache_v = cache_v.reshape(nreq, nlay, past, kv_heads * hd)
        spec_c = pl.BlockSpec((None, None, past, hd), lambda b, h: (b, layer_idx, 0, h))
        in_specs += [spec_c, spec_c,
                     pl.BlockSpec((seq, hd), lambda b, h: (0, 0)),
                     pl.BlockSpec((seq, hd), lambda b, h: (0, 0)),
                     pl.BlockSpec(memory_space=pl.ANY)]
        args += [cache_k, cache_v, cos, sin_signed, out_prev]
        aliases = {len(args) - 1: 0}
    else:
        out_specs += [pl.BlockSpec((seq, hd), lambda b, h: (b, h)),
                      pl.BlockSpec((seq, hd), lambda b, h: (b, h))]
        out_shape += [jax.ShapeDtypeStruct((n_seq * seq, kv_heads * hd), F32)] * 2
    return pl.pallas_call(
        functools.partial(_attn_kernel, has_ctx=has_ctx, grp=grp, qblk=min(seq, 256), scale=float(hd) ** -0.5),
        grid=(n_seq, kv_heads),
        in_specs=in_specs,
        out_specs=out_specs,
        out_shape=out_shape,
        input_output_aliases=aliases,
        compiler_params=_cparams("arbitrary", "arbitrary"),
        name="gqa",
    )(*args)


def _ret_kernel(*refs, nchunk, has_ctx, kscale):
    if has_ctx:
        (lg_ref, q_ref, k_ref, v_ref, g_ref, gn_ref, s0_ref, cos_ref, sin_ref, _prev,
         o_ref, sfin_ref, sb_ref) = refs
    else:
        (lg_ref, q_ref, k_ref, v_ref, g_ref, gn_ref, o_ref, sfin_ref, sb_ref) = refs
    cs = RET_CHUNK
    kd_dim = q_ref.shape[1]
    h = pl.program_id(1)
    lgf = lg_ref[0, h]
    lgb = lg_ref[1, h]
    row = lax.broadcasted_iota(jnp.int32, (cs, cs), 0)
    col = lax.broadcasted_iota(jnp.int32, (cs, cs), 1)
    rel = (row - col).astype(F32)
    dmat = (jnp.exp(jnp.where(rel >= 0, rel * lgf, -jnp.inf))
            + jnp.exp(jnp.where(rel <= 0, -rel * lgb, -jnp.inf)))
    j = lax.broadcasted_iota(jnp.int32, (cs, 1), 0).astype(F32)
    one = jnp.ones((1, 1), F32)
    dec_f = jnp.exp(one * (cs * lgf))
    dec_b = jnp.exp(one * (cs * lgb))

    def load_qk(ref, r0, mul):
        x = ref[pl.ds(r0, cs), :].astype(F32) * mul
        if has_ctx:
            half = kd_dim // 2
            x1, x2 = x[:, :half], x[:, half:]
            cos = cos_ref[pl.ds(r0, cs), :]
            sin = sin_ref[pl.ds(r0, cs), :]
            x = jnp.concatenate([x1 * cos - x2 * sin, x2 * cos + x1 * sin], axis=1)
        return x

    def bwd_body(i, st):
        n = nchunk - 1 - i
        r0 = pl.multiple_of(n * cs, cs)
        sb_ref[n] = st
        k = load_qk(k_ref, r0, kscale)
        v = v_ref[pl.ds(r0, cs), :]
        return st * dec_b + _bdot((k * jnp.exp(j * lgb)).T, v)

    st_b0 = s0_ref[1] if has_ctx else jnp.zeros(sfin_ref.shape[1:], F32)
    st_b = lax.fori_loop(0, nchunk, bwd_body, st_b0)
    sfin_ref[1] = st_b

    def fwd_body(n, st):
        r0 = pl.multiple_of(n * cs, cs)
        q = load_qk(q_ref, r0, 1.0)
        k = load_qk(k_ref, r0, kscale)
        v = v_ref[pl.ds(r0, cs), :]
        a = _bdot_nt(q, k) * dmat
        o = _bdot(a, v)
        if has_ctx or nchunk > 1:
            o = o + _bdot(q * jnp.exp((j + 1.0) * lgf), st)
            o = o + _bdot(q * jnp.exp((cs - j) * lgb), sb_ref[n])
        mu = jnp.mean(o, axis=-1, keepdims=True)
        var = jnp.mean(jnp.square(o - mu), axis=-1, keepdims=True)
        y = (o - mu) * lax.rsqrt(var + EPS) * gn_ref[0]
        o_ref[pl.ds(r0, cs), :] = (_silu(g_ref[pl.ds(r0, cs), :].astype(F32)) * y).astype(o_ref.dtype)
        return st * dec_f + _bdot((k * jnp.exp((cs - 1.0 - j) * lgf)).T, v)

    st_f0 = s0_ref[0] if has_ctx else jnp.zeros(sfin_ref.shape[1:], F32)
    sfin_ref[0] = lax.fori_loop(0, nchunk, fwd_body, st_f0)


def _retention(proj, log_gamma, gn_g, ctx, layer_idx, out_prev, n_seq, seq, row_blk0, heads, key_d, val_d):
    t = proj.shape[0]
    has_ctx = ctx is not None
    nq = heads * key_d
    cq, ck, cv, cg = 0, nq // key_d, 2 * nq // val_d, (2 * nq + heads * val_d) // val_d
    in_specs = [pl.BlockSpec(memory_space=pltpu.SMEM),
                pl.BlockSpec((seq, key_d), lambda b, h: (row_blk0 + b, cq + h)),
                pl.BlockSpec((seq, key_d), lambda b, h: (row_blk0 + b, ck + h)),
                pl.BlockSpec((seq, val_d), lambda b, h: (row_blk0 + b, cv + h)),
                pl.BlockSpec((seq, val_d), lambda b, h: (row_blk0 + b, cg + h)),
                pl.BlockSpec((1, 1, val_d), lambda b, h: (h, 0, 0))]
    args = [log_gamma, proj, proj, proj, proj, gn_g.reshape(heads, 1, val_d)]
    aliases = {}
    if has_ctx:
        s0, cos, sin = ctx
        in_specs += [pl.BlockSpec((None, None, 2, None, key_d, val_d), lambda b, h: (b, layer_idx, 0, h, 0, 0)),
                     pl.BlockSpec((seq, key_d // 2), lambda b, h: (0, 0)),
                     pl.BlockSpec((seq, key_d // 2), lambda b, h: (0, 0)),
                     pl.BlockSpec(memory_space=pl.ANY)]
        args += [s0, cos, sin, out_prev]
        aliases = {len(args) - 1: 0}
    nchunk = seq // RET_CHUNK
    return pl.pallas_call(
        functools.partial(_ret_kernel, nchunk=nchunk, has_ctx=has_ctx, kscale=float(key_d) ** -0.5),
        grid=(n_seq, heads),
        in_specs=in_specs,
        out_specs=[pl.BlockSpec((seq, val_d), lambda b, h: (row_blk0 + b, h)),
                   pl.BlockSpec((None, 2, None, key_d, val_d), lambda b, h: (b, 0, h, 0, 0))],
        out_shape=[jax.ShapeDtypeStruct((t, heads * val_d), BF16),
                   jax.ShapeDtypeStruct((n_seq, 2, heads, key_d, val_d), F32)],
        scratch_shapes=[pltpu.VMEM((nchunk, key_d, val_d), F32)],
        input_output_aliases=aliases,
        compiler_params=_cparams("arbitrary", "arbitrary"),
        name="retention",
    )(*args)


def _outproj_kernel(*refs, n_in, n_x, n_ctx_tiles, n_groups, n_experts):
    mix_refs = refs[:n_in]
    w_refs = refs[n_in:2 * n_in]
    x_refs = refs[2 * n_in:2 * n_in + n_x]
    (mod_ref, g2_ref, wr_ref, br_ref, xn_ref, h2_ref, meta_ref, ew_ref, cnt_out_ref,
     cnt_ref) = refs[2 * n_in + n_x:]
    acc = _bdot(mix_refs[0][...], w_refs[0][...])
    for mref, wref in zip(mix_refs[1:], w_refs[1:]):
        acc = acc + _bdot(mref[...], wref[...])
    xn = _token_rows(x_refs, pl.program_id(0), n_ctx_tiles) + mod_ref[0, 2:3, :] * acc
    xn_ref[...] = xn
    h2 = _rms(xn, g2_ref[...]) * (1.0 + mod_ref[0, 4:5, :]) + mod_ref[0, 3:4, :]
    h2_ref[...] = h2.astype(BF16).reshape(h2_ref.shape)
    h_hi = h2.astype(BF16)
    h_lo = (h2 - h_hi.astype(F32)).astype(BF16)
    lg = (jnp.dot(h_hi, wr_ref[0], preferred_element_type=F32)
          + jnp.dot(h_hi, wr_ref[1], preferred_element_type=F32)
          + jnp.dot(h_lo, wr_ref[0], preferred_element_type=F32)) + br_ref[...]

    per_grp = n_experts // n_groups
    lane = lax.broadcasted_iota(jnp.int32, lg.shape, 1).astype(F32)
    big = float(lg.shape[1])
    ninf = -jnp.inf
    gl = jnp.where(lane < n_groups, lg, ninf)
    gmax = jnp.max(gl, axis=-1, keepdims=True)
    p_grp = 1.0 / jnp.sum(jnp.exp(gl - gmax), axis=-1, keepdims=True)
    gidx = jnp.min(jnp.where(gl == gmax, lane, big), axis=-1, keepdims=True)
    in_grp = ((lane >= n_groups) & (lane < n_groups + n_experts)
              & (jnp.floor((lane - n_groups) / per_grp) == gidx))
    el = jnp.where(in_grp, lg, ninf)
    emax = jnp.max(el, axis=-1, keepdims=True)
    esum = jnp.sum(jnp.exp(el - emax), axis=-1, keepdims=True)
    i1 = jnp.min(jnp.where(el == emax, lane, big), axis=-1, keepdims=True)
    el2 = jnp.where(lane == i1, ninf, el)
    emax2 = jnp.max(el2, axis=-1, keepdims=True)
    i2 = jnp.min(jnp.where(el2 == emax2, lane, big), axis=-1, keepdims=True)
    p1 = 1.0 / esum
    p2 = jnp.exp(emax2 - emax) / esum
    w1 = p1 / (p1 + p2) * p_grp
    w2 = p2 / (p1 + p2) * p_grp
    ew_ref[...] = jnp.where(lane == 0.0, w1, jnp.where(lane == 1.0, w2, 0.0))

    step = pl.program_id(0)

    @pl.when(step == 0)
    def _():
        cnt_ref[...] = jnp.zeros_like(cnt_ref)

    tm = lg.shape[0]
    hit1 = lane == i1
    hit2 = lane == i2
    sel = jnp.where(hit1 | hit2, 1.0, 0.0)
    rr = lax.broadcasted_iota(jnp.int32, (tm, tm), 0)
    cc = lax.broadcasted_iota(jnp.int32, (tm, tm), 1)
    before = jnp.where(cc < rr, 1.0, 0.0).astype(BF16)
    prior = jnp.dot(before, sel.astype(BF16), preferred_element_type=F32) + cnt_ref[...]
    rank1 = jnp.sum(jnp.where(hit1, prior, 0.0), axis=-1, keepdims=True)
    rank2 = jnp.sum(jnp.where(hit2, prior, 0.0), axis=-1, keepdims=True)
    cnt_ref[...] += jnp.sum(sel, axis=0, keepdims=True)
    cnt_out_ref[...] = cnt_ref[...]
    meta = jnp.where(lane == 0.0, i1 - n_groups,
                     jnp.where(lane == 1.0, i2 - n_groups,
                               jnp.where(lane == 2.0, rank1, jnp.where(lane == 3.0, rank2, 0.0))))
    meta_ref[0] = meta.T[0:8, :].astype(jnp.int32)


def _out_proj(mixes, w_parts, x, mod, g2, w_router, b_router, n_ctx_tiles, tiles_per_req, n_groups, n_experts):
    t = mixes[0].shape[0]
    d = w_parts[0].shape[1]
    tm = TOKEN_TILE
    n_in = len(mixes)
    mrow = functools.partial(_tile_mod_row, n_ctx_tiles=n_ctx_tiles, tiles_per_req=tiles_per_req)
    x_specs, x_args = _token_specs(x, tm, n_ctx_tiles, lambda i: i)
    in_specs = ([pl.BlockSpec((tm, m.shape[1]), lambda i: (i, 0)) for m in mixes]
                + [pl.BlockSpec(w.shape, lambda i: (0, 0)) for w in w_parts]
                + x_specs
                + [pl.BlockSpec((1, 6, d), lambda i: (mrow(i), 0, 0)),
                   pl.BlockSpec((1, d), lambda i: (0, 0)),
                   pl.BlockSpec(w_router.shape, lambda i: (0, 0, 0)),
                   pl.BlockSpec((1, 128), lambda i: (0, 0))])
    return pl.pallas_call(
        functools.partial(_outproj_kernel, n_in=n_in, n_x=len(x_args), n_ctx_tiles=n_ctx_tiles,
                          n_groups=n_groups, n_experts=n_experts),
        grid=(t // tm,),
        in_specs=in_specs,
        out_specs=[pl.BlockSpec((tm, d), lambda i: (i, 0)),
                   pl.BlockSpec((tm, d // 128, 128), lambda i: (i, 0, 0)),
                   pl.BlockSpec((1, 8, tm), lambda i: (i, 0, 0)),
                   pl.BlockSpec((tm, 128), lambda i: (i, 0)),
                   pl.BlockSpec((1, 128), lambda i: (0, 0))],
        out_shape=[jax.ShapeDtypeStruct((t, d), F32),
                   jax.ShapeDtypeStruct((t, d // 128, 128), BF16),
                   jax.ShapeDtypeStruct((t // tm, 8, tm), jnp.int32),
                   jax.ShapeDtypeStruct((t, 128), F32),
                   jax.ShapeDtypeStruct((1, 128), F32)],
        scratch_shapes=[pltpu.VMEM((1, 128), F32)],
        compiler_params=_cparams("arbitrary"),
        name="out_proj_router",
    )(*mixes, *w_parts, *x_args, mod, g2.reshape(1, d), w_router, b_router)


def _dispatch_plan(meta, counts, n_experts, tile):
    t = meta.shape[0] * meta.shape[2]
    e1, e2, rank1, rank2 = (meta[:, k, :].reshape(t) for k in range(4))
    padded = ((counts + tile - 1) // tile) * tile
    ends = jnp.cumsum(padded)
    starts = ends - padded
    experts = jnp.arange(n_experts, dtype=jnp.int32)[None, :]
    pos = [jnp.sum(jnp.where(e[:, None] == experts, starts[None, :], 0), axis=1) + r
           for e, r in ((e1, rank1), (e2, rank2))]
    n_tiles = (2 * t) // tile + n_experts
    n_used = (ends[-1] // tile).astype(jnp.int32)
    tile_start = jnp.arange(n_tiles, dtype=jnp.int32) * tile
    probe = jnp.minimum(tile_start, ends[-1] - 1)
    tile_expert = jnp.sum((ends[None, :] <= probe[:, None]).astype(jnp.int32), axis=1)
    return tile_expert, n_used.reshape(1), ends.astype(jnp.int32), pos[0], pos[1], n_tiles


def _row_source_kernel(ends_ref, p1_ref, p2_ref, o_ref, *, n_tok, n_rows, tile, n_experts):
    def clear8(j, c):
        base = lax.rem(j * 8, n_tok)
        for u in range(8):
            o_ref[j * 8 + u] = base + u
        return c

    def per_expert(e, c):
        first = lax.shift_right_logical(jnp.maximum(ends_ref[e] - tile, 0), 3)
        lax.fori_loop(first, first + tile // 8, clear8, 0)
        return c

    g = pl.program_id(0)

    @pl.when(g == 0)
    def _():
        lax.fori_loop(0, n_experts, per_expert, 0)
        lax.fori_loop(lax.shift_right_logical(ends_ref[n_experts - 1], 3), n_rows // 8, clear8, 0)

    chunk = p1_ref.shape[2]

    def place(r, c):
        t = g * chunk + r
        o_ref[p1_ref[0, 0, r]] = t
        o_ref[p2_ref[0, 0, r]] = t
        return c

    lax.fori_loop(0, chunk, place, 0, unroll=8)


def _row_source(ends, pos1, pos2, n_rows, tile):
    n_tok = pos1.shape[0]
    chunk = ROW_SOURCE_CHUNK
    smem = pl.BlockSpec(memory_space=pltpu.SMEM)
    pos_spec = pl.BlockSpec((1, 1, chunk), lambda g: (g, 0, 0), memory_space=pltpu.SMEM)
    return pl.pallas_call(
        functools.partial(_row_source_kernel, n_tok=n_tok, n_rows=n_rows, tile=tile, n_experts=ends.shape[0]),
        grid=(n_tok // chunk,),
        in_specs=[smem, pos_spec, pos_spec],
        out_specs=smem,
        out_shape=jax.ShapeDtypeStruct((n_rows,), jnp.int32),
        compiler_params=_cparams("arbitrary"),
        name="moe_row_source",
    )(ends, pos1.reshape(n_tok // chunk, 1, chunk), pos2.reshape(n_tok // chunk, 1, chunk))


def _moe_kernel(te_ref, nu_ref, *refs, tm, ahead):
    tok_refs = refs[:ahead + 1]
    h_hbm, wg_ref, wu_ref, wd_ref, o_ref, wg_bf, wu_bf, wd_bf, buf, sem = refs[ahead + 1:]
    i = pl.program_id(0)
    nslot = ahead + 1
    slot = lax.rem(i, nslot)
    n_used = nu_ref[0]

    def issue(tile, tok_ref):
        s = lax.rem(tile, nslot)

        def body(j, c):
            for u in range(2):
                r = 2 * j + u
                pltpu.make_async_copy(h_hbm.at[tok_ref[0, 0, r]], buf.at[s, r], sem.at[s]).start(priority=u)
            return c

        lax.fori_loop(0, tm // 2, body, 0, unroll=4)

    @pl.when(i == 0)
    def _():
        for j in range(ahead):
            @pl.when(j < n_used)
            def _():
                issue(jnp.int32(j), tok_refs[j])

    new_expert = jnp.logical_or(i == 0, te_ref[i] != te_ref[jnp.maximum(i - 1, 0)])

    @pl.when(new_expert)
    def _():
        wg_bf[...] = wg_ref[...].astype(BF16)
        wu_bf[...] = wu_ref[...].astype(BF16)
        wd_bf[...] = wd_ref[...].astype(BF16)

    def compute(issue_next):
        s_next = lax.rem(i + ahead, nslot)
        quarter = tm // 4

        def start_rows(part):
            if issue_next:
                for r in range(part * quarter, (part + 1) * quarter):
                    pltpu.make_async_copy(h_hbm.at[tok_refs[ahead][0, 0, r]], buf.at[s_next, r],
                                          sem.at[s_next]).start(priority=r % 2)

        pltpu.make_async_copy(h_hbm.at[pl.ds(0, tm)], buf.at[slot], sem.at[slot]).wait()
        start_rows(0)
        x = buf[slot].reshape(tm, -1)
        g = jnp.dot(x, wg_bf[...], preferred_element_type=F32)
        start_rows(1)
        u = jnp.dot(x, wu_bf[...], preferred_element_type=F32)
        start_rows(2)
        act = _silu(g) * u
        y = jnp.dot(act.astype(BF16), wd_bf[...], preferred_element_type=F32)
        start_rows(3)
        o_ref[...] = y.astype(BF16).reshape(o_ref.shape)

    @pl.when(i + ahead < n_used)
    def _():
        compute(True)

    @pl.when(jnp.logical_and(i + ahead >= n_used, i < n_used))
    def _():
        compute(False)

    @pl.when(i >= n_used)
    def _():
        o_ref[...] = jnp.zeros_like(o_ref)


def _moe_experts(h3, src_tok, tile_expert, n_used, w_gate, w_up, w_down, layer_idx, tm):
    t, sub, lanes = h3.shape
    d = sub * lanes
    n_tiles = src_tok.shape[0] // tm
    f = w_gate.shape[-1]
    ahead = MOE_GATHER_AHEAD
    tok = src_tok.reshape(n_tiles, 1, tm)

    def tok_spec(k):
        return pl.BlockSpec((1, 1, tm), lambda i, te, nu: (jnp.minimum(i + k, n_tiles - 1), 0, 0),
                            memory_space=pltpu.SMEM)

    return pl.pallas_call(
        functools.partial(_moe_kernel, tm=tm, ahead=ahead),
        grid_spec=pltpu.PrefetchScalarGridSpec(
            num_scalar_prefetch=2,
            grid=(n_tiles,),
            in_specs=[tok_spec(k) for k in range(ahead + 1)] + [
                      pl.BlockSpec(memory_space=pl.ANY),
                      pl.BlockSpec((None, None, d, f), lambda i, te, nu: (layer_idx, te[i], 0, 0)),
                      pl.BlockSpec((None, None, d, f), lambda i, te, nu: (layer_idx, te[i], 0, 0)),
                      pl.BlockSpec((None, None, f, d), lambda i, te, nu: (layer_idx, te[i], 0, 0))],
            out_specs=pl.BlockSpec((tm, sub, lanes), lambda i, te, nu: (i, 0, 0)),
            scratch_shapes=[pltpu.VMEM((d, f), BF16), pltpu.VMEM((d, f), BF16), pltpu.VMEM((f, d), BF16),
                            pltpu.VMEM((ahead + 1, tm, sub, lanes), BF16),
                            pltpu.SemaphoreType.DMA((ahead + 1,))]),
        out_shape=jax.ShapeDtypeStruct((n_tiles * tm, sub, lanes), BF16),
        compiler_params=_cparams("arbitrary"),
        name="moe_experts",
    )(tile_expert, n_used, *([tok] * (ahead + 1)), h3, w_gate, w_up, w_down)


def _combine_kernel(p1_ref, p2_ref, p1n_ref, p2n_ref, y_hbm, x_ref, ew_ref, mod_ref, gf_ref, *rest,
                    tm, n_steps, n_ctx_tiles, final_norm):
    out_refs, (buf, sem) = rest[:-2], rest[-2:]
    i = pl.program_id(0)
    slot = lax.rem(i, 2)

    def issue(s, pa_ref, pb_ref):
        def body(r, c):
            pltpu.make_async_copy(y_hbm.at[pa_ref[0, 0, r]], buf.at[s, 0, r], sem.at[s, 0]).start(priority=0)
            pltpu.make_async_copy(y_hbm.at[pb_ref[0, 0, r]], buf.at[s, 1, r], sem.at[s, 1]).start(priority=1)
            return c

        lax.fori_loop(0, tm, body, 0, unroll=8)

    @pl.when(i == 0)
    def _():
        issue(0, p1_ref, p2_ref)

    @pl.when(i + 1 < n_steps)
    def _():
        issue(1 - slot, p1n_ref, p2n_ref)

    for k in range(2):
        pltpu.make_async_copy(y_hbm.at[pl.ds(0, tm)], buf.at[slot, k], sem.at[slot, k]).wait()
    ew = ew_ref[...]
    y = (ew[:, 0:1] * buf[slot, 0].reshape(tm, -1).astype(F32)
         + ew[:, 1:2] * buf[slot, 1].reshape(tm, -1).astype(F32))
    x = x_ref[...] + mod_ref[0, 5:6, :] * y
    if not final_norm:
        out_refs[0][...] = x
    else:
        x = _rms(x, gf_ref[...])
        ctx_ref, lat_ref = out_refs

        @pl.when(i < n_ctx_tiles)
        def _():
            ctx_ref[...] = x

        @pl.when(i >= n_ctx_tiles)
        def _():
            lat_ref[...] = x


def _moe_combine(y_sorted, pos1, pos2, x, ew, mod, final_g, n_ctx_tiles, tiles_per_req, final_norm):
    t, d = x.shape
    tm = TOKEN_TILE
    n_steps = t // tm
    mrow = functools.partial(_tile_mod_row, n_ctx_tiles=n_ctx_tiles, tiles_per_req=tiles_per_req)
    smem_rows = pl.BlockSpec((1, 1, tm), lambda i: (i, 0, 0), memory_space=pltpu.SMEM)
    smem_next = pl.BlockSpec((1, 1, tm), lambda i: (jnp.minimum(i + 1, n_steps - 1), 0, 0),
                             memory_space=pltpu.SMEM)
    p1 = pos1.reshape(n_steps, 1, tm)
    p2 = pos2.reshape(n_steps, 1, tm)
    if final_norm:
        out_specs = [pl.BlockSpec((tm, d), lambda i: (jnp.minimum(i, n_ctx_tiles - 1), 0)),
                     pl.BlockSpec((tm, d), lambda i: (jnp.maximum(i - n_ctx_tiles, 0), 0))]
        out_shape = [jax.ShapeDtypeStruct((n_ctx_tiles * tm, d), F32),
                     jax.ShapeDtypeStruct((t - n_ctx_tiles * tm, d), F32)]
    else:
        out_specs = pl.BlockSpec((tm, d), lambda i: (i, 0))
        out_shape = jax.ShapeDtypeStruct((t, d), F32)
    return pl.pallas_call(
        functools.partial(_combine_kernel, tm=tm, n_steps=n_steps, n_ctx_tiles=n_ctx_tiles,
                          final_norm=final_norm),
        grid=(n_steps,),
        in_specs=[smem_rows, smem_rows, smem_next, smem_next,
                  pl.BlockSpec(memory_space=pl.ANY),
                  pl.BlockSpec((tm, d), lambda i: (i, 0)),
                  pl.BlockSpec((tm, 128), lambda i: (i, 0)),
                  pl.BlockSpec((1, 6, d), lambda i: (mrow(i), 0, 0)),
                  pl.BlockSpec((1, d), lambda i: (0, 0))],
        out_specs=out_specs,
        out_shape=out_shape,
        scratch_shapes=[pltpu.VMEM((2, 2, tm) + y_sorted.shape[1:], y_sorted.dtype),
                        pltpu.SemaphoreType.DMA((2, 2))],
        compiler_params=_cparams("arbitrary"),
        name="moe_combine",
    )(p1, p2, p1, p2, y_sorted, x, ew, mod, final_g.reshape(1, d))


def _axial_angles(seq, head_dim):
    rows = seq // GRID_W
    row = jnp.repeat(jnp.arange(rows), GRID_W).astype(F32)
    col = jnp.tile(jnp.arange(GRID_W), rows).astype(F32)
    n_freq = head_dim // 4
    inv = ROPE_BASE ** (-jnp.arange(n_freq, dtype=F32) / n_freq)
    return jnp.concatenate([row[:, None] * inv, col[:, None] * inv], axis=-1)


def kernel(x_prompt, x_sample, cache_attn_k, cache_attn_v, state_hgrn, state_ret, c, c_ctx, mod_w, mod_b, norm1_g, norm2_g, even_w_in, even_w_out, hgrn_lb_logits, hgrn_norm_g, attn_qn_g, attn_kn_g, odd_w_in, odd_w_out, ret_decay_logit, ret_gn_g, moe_w_group, moe_b_group, moe_w_expert, moe_b_expert, moe_w_gate, moe_w_up, moe_w_down, final_norm_g):
    batch, seq, d = x_prompt.shape
    dec_batch, dec_seq, _ = x_sample.shape
    depth = mod_w.shape[0]
    heads_a, val_a = hgrn_norm_g.shape[1:]
    key_a = hgrn_lb_logits.shape[2] // heads_a
    hd_b = attn_qn_g.shape[1]
    kv_b = cache_attn_k.shape[3]
    heads_b = (even_w_out.shape[1] - heads_a * val_a) // hd_b
    heads_c, val_c = ret_gn_g.shape[1:]
    key_c = state_ret.shape[4]
    n_experts = moe_w_expert.shape[2]
    n_groups = moe_w_group.shape[2]
    tm = TOKEN_TILE
    n_ctx = batch * seq
    n_ctx_tiles = n_ctx // tm
    tiles_per_req = dec_seq // tm
    assert seq % tm == 0 and dec_seq % tm == 0 and n_ctx % dec_seq == 0
    assert key_a == 128 and val_a == 128 and hd_b == 128 and n_groups + n_experts <= 128

    x = (x_prompt.reshape(n_ctx, d), x_sample.reshape(dec_batch * dec_seq, d))
    t = n_ctx + dec_batch * dec_seq

    n_vec = 1 + dec_batch
    rows = -(-n_vec // 8) * 8
    cvecs = jnp.zeros((rows, d), F32).at[0].set(c_ctx).at[1:n_vec].set(c)
    mods = _ada_mod(cvecs, mod_w, mod_b).reshape(depth, rows, 6, d)

    lower_bounds = jnp.cumsum(jax.nn.softmax(hgrn_lb_logits.astype(F32), axis=0), axis=0)
    log_gamma = jax.nn.log_sigmoid(ret_decay_logit.astype(F32))

    ang_b = _axial_angles(dec_seq, hd_b)
    cos_b = jnp.concatenate([jnp.cos(ang_b), jnp.cos(ang_b)], axis=-1)
    sin_b = jnp.concatenate([-jnp.sin(ang_b), jnp.sin(ang_b)], axis=-1)
    ang_c = _axial_angles(dec_seq, key_c)
    cos_c, sin_c = jnp.cos(ang_c), jnp.sin(ang_c)

    pad = 128 - n_groups - n_experts
    new_k = new_v = new_sh = new_sr = None
    assert depth == 2, "one HGRN2/GQA layer followed by one retention layer"
    for l in range(depth):
        mod = mods[l]
        if l % 2 == 0:
            e = l // 2
            na, nva, nqb, nkv = heads_a * key_a, heads_a * val_a, heads_b * hd_b, kv_b * hd_b
            o_zf, o_kb = na + nva, 3 * na + 2 * nva + nqb
            w_in = even_w_in[e].astype(BF16)
            proj16, proj32 = _in_proj(x, norm1_g[l], mod, w_in, t, n_ctx_tiles, tiles_per_req, w_in.shape[1],
                                      f32_cols=((o_zf, o_zf + 2 * na), (o_kb, o_kb + 2 * nkv)))
            lb = lower_bounds[l]
            oa, s_ctx = _hgrn(proj16, proj32, lb, hgrn_norm_g[e], None, e, None, batch, seq, 0,
                              heads_a, key_a, val_a)
            oa, _ = _hgrn(proj16, proj32, lb, hgrn_norm_g[e], state_hgrn, e, oa, dec_batch, dec_seq,
                          n_ctx // dec_seq, heads_a, key_a, val_a)
            col_q, col_kv = (na + 2 * nva) // hd_b, (2 * na) // hd_b
            ob, kc, vc = _attn(proj16, proj32, attn_qn_g[e], attn_kn_g[e], None, e, None, batch, seq, 0,
                               heads_b, kv_b, hd_b, col_q, col_kv)
            (ob,) = _attn(proj16, proj32, attn_qn_g[e], attn_kn_g[e],
                          (cache_attn_k, cache_attn_v, cos_b, sin_b), e, ob,
                          dec_batch, dec_seq, n_ctx // dec_seq, heads_b, kv_b, hd_b, col_q, col_kv)
            new_k = kc.reshape(batch, 1, seq, kv_b, hd_b)
            new_v = vc.reshape(batch, 1, seq, kv_b, hd_b)
            new_sh = s_ctx[:, None]
            w_out = even_w_out[e].astype(BF16)
            mixes = [oa, ob]
            w_parts = [w_out[:heads_a * val_a], w_out[heads_a * val_a:]]
        else:
            o = l // 2
            w_in = odd_w_in[o].astype(BF16)
            proj = _in_proj(x, norm1_g[l], mod, w_in, t, n_ctx_tiles, tiles_per_req, w_in.shape[1] // 2)
            y, s_ctx = _retention(proj, log_gamma[o], ret_gn_g[o], None, o, None, batch, seq, 0,
                                  heads_c, key_c, val_c)
            y, _ = _retention(proj, log_gamma[o], ret_gn_g[o], (state_ret, cos_c, sin_c), o, y,
                              dec_batch, dec_seq, n_ctx // dec_seq, heads_c, key_c, val_c)
            new_sr = s_ctx[:, None]
            mixes = [y]
            w_parts = [odd_w_out[o].astype(BF16)]

        w_router = jnp.concatenate([moe_w_group[l], moe_w_expert[l], jnp.zeros((d, pad), F32)], axis=1)
        w_router_hi = w_router.astype(BF16)
        w_router = jnp.stack([w_router_hi, (w_router - w_router_hi.astype(F32)).astype(BF16)])
        b_router = jnp.concatenate([moe_b_group[l], moe_b_expert[l], jnp.zeros((pad,), F32)]).reshape(1, 128)
        xn, h2, meta, ew, cnt = _out_proj(mixes, w_parts, x, mod, norm2_g[l], w_router, b_router,
                                          n_ctx_tiles, tiles_per_req, n_groups, n_experts)
        counts = cnt[0, n_groups:n_groups + n_experts].astype(jnp.int32)
        tile_expert, n_used, ends, pos1, pos2, n_tiles = _dispatch_plan(meta, counts, n_experts, EXPERT_TILE)
        src_tok = _row_source(ends, pos1, pos2, n_tiles * EXPERT_TILE, EXPERT_TILE)
        y_sorted = _moe_experts(h2, src_tok, tile_expert, n_used, moe_w_gate, moe_w_up, moe_w_down, l, EXPERT_TILE)
        x = _moe_combine(y_sorted, pos1, pos2, xn, ew, mod, final_norm_g, n_ctx_tiles, tiles_per_req,
                         final_norm=(l == depth - 1))

    y_prompt = x[0].reshape(batch, seq, d)
    y_sample = x[1].reshape(dec_batch, dec_seq, d)
    return (y_prompt, y_sample, new_k, new_v, new_sh, new_sr)
```

```python
import functools

import numpy as np
import jax
import jax.numpy as jnp
from jax import lax
from jax.experimental import pallas as pl
from jax.experimental.pallas import tpu as pltpu

F32 = jnp.float32
BF16 = jnp.bfloat16
EPS = 1e-6
ROPE_BASE = 10000.0
GRID_W = 64
N_GROUPS = 4
TOP_K_INNER = 2
HGRN_BLOCK = 128
HGRN_DIAG = 8
RET_CHUNK = 256
TOKEN_TILE = 256
INPROJ_TILE = 512
EXPERT_TILE = 256
MOE_GATHER_AHEAD = 2
ROW_SOURCE_CHUNK = 512
V7X_VMEM_LIMIT_BYTES = 56 * 1024 * 1024
HIGHEST = lax.Precision.HIGHEST
LOG2_E = 1.4426950408889634

_NT = (((1,), (1,)), ((), ()))


def _cparams(*sem):
    return pltpu.CompilerParams(dimension_semantics=sem, vmem_limit_bytes=V7X_VMEM_LIMIT_BYTES)


def _rms(x, g):
    return x * lax.rsqrt(jnp.mean(x * x, axis=-1, keepdims=True) + EPS) * g


def _silu(x):
    return x * jax.nn.sigmoid(x)


def _bdot(a, b):
    return jnp.dot(a.astype(BF16), b.astype(BF16), preferred_element_type=F32)


def _bdot_nt(a, b):
    return lax.dot_general(a.astype(BF16), b.astype(BF16), _NT, preferred_element_type=F32)


def _mod_kernel(c_ref, w_ref, b_ref, o_ref):
    a = _silu(c_ref[...])
    o_ref[0] = jnp.dot(a, w_ref[0], precision=HIGHEST, preferred_element_type=F32) + b_ref[0]


def _ada_mod(cvecs, mod_w, mod_b):
    depth, d, n = mod_w.shape
    tn = n // 4
    rows = cvecs.shape[0]
    return pl.pallas_call(
        _mod_kernel,
        grid=(depth, n // tn),
        in_specs=[pl.BlockSpec((rows, d), lambda l, j: (0, 0)),
                  pl.BlockSpec((1, d, tn), lambda l, j: (l, 0, j)),
                  pl.BlockSpec((1, 1, tn), lambda l, j: (l, 0, j))],
        out_specs=pl.BlockSpec((1, rows, tn), lambda l, j: (l, 0, j)),
        out_shape=jax.ShapeDtypeStruct((depth, rows, n), F32),
        compiler_params=_cparams("arbitrary", "arbitrary"),
        name="ada_mod",
    )(cvecs, mod_w, mod_b.reshape(depth, 1, n))


def _tile_mod_row(i, n_ctx_tiles, tiles_per_req):
    return jnp.where(i < n_ctx_tiles, 0, 1 + jnp.maximum(i - n_ctx_tiles, 0) // tiles_per_req)


def _token_specs(x, tm, n_ctx_tiles, step_of):
    if not isinstance(x, tuple):
        return [pl.BlockSpec((tm, x.shape[1]), lambda *g: (step_of(*g), 0))], (x,)
    d = x[0].shape[1]
    return ([pl.BlockSpec((tm, d), lambda *g: (jnp.minimum(step_of(*g), n_ctx_tiles - 1), 0)),
             pl.BlockSpec((tm, d), lambda *g: (jnp.maximum(step_of(*g) - n_ctx_tiles, 0), 0))], x)


def _token_rows(x_refs, i, n_ctx_tiles):
    if len(x_refs) == 1:
        return x_refs[0][...]
    return jnp.where(i < n_ctx_tiles, x_refs[0][...], x_refs[1][...])


def _inproj_kernel(*refs, n_x, n_ctx_tiles, f32_cols):
    x_refs = refs[:n_x]
    g_ref, mod_ref, w_ref = refs[n_x:n_x + 3]
    out_refs = refs[n_x + 3:]
    x = _token_rows(x_refs, pl.program_id(1), n_ctx_tiles)
    h = _rms(x, g_ref[...]) * (1.0 + mod_ref[0, 1:2, :]) + mod_ref[0, 0:1, :]
    acc = jnp.dot(h.astype(BF16), w_ref[...], preferred_element_type=F32)
    if not f32_cols:
        out_refs[0][...] = acc.astype(BF16)
        return
    n = acc.shape[1]
    at16 = at32 = start = 0
    for lo, hi in tuple(f32_cols) + ((n, n),):
        if lo > start:
            out_refs[0][:, at16:at16 + lo - start] = acc[:, start:lo].astype(BF16)
            at16 += lo - start
        if hi > lo:
            out_refs[1][:, at32:at32 + hi - lo] = acc[:, lo:hi]
            at32 += hi - lo
        start = hi


def _in_proj(x, g, mod, w_bf, t, n_ctx_tiles, tiles_per_req, tn, f32_cols=()):
    d, n = w_bf.shape
    tm = INPROJ_TILE
    scale = INPROJ_TILE // TOKEN_TILE
    assert n_ctx_tiles % scale == 0 and tiles_per_req % scale == 0
    n_ctx_tiles, tiles_per_req = n_ctx_tiles // scale, tiles_per_req // scale
    mrow = functools.partial(_tile_mod_row, n_ctx_tiles=n_ctx_tiles, tiles_per_req=tiles_per_req)
    x_specs, x_args = _token_specs(x, tm, n_ctx_tiles, lambda j, i: i)
    if not f32_cols:
        out_specs = pl.BlockSpec((tm, tn), lambda j, i: (i, j))
        out_shape = jax.ShapeDtypeStruct((t, n), BF16)
    else:
        assert tn == n
        n32 = sum(hi - lo for lo, hi in f32_cols)
        out_specs = [pl.BlockSpec((tm, n - n32), lambda j, i: (i, 0)),
                     pl.BlockSpec((tm, n32), lambda j, i: (i, 0))]
        out_shape = [jax.ShapeDtypeStruct((t, n - n32), BF16), jax.ShapeDtypeStruct((t, n32), F32)]
    return pl.pallas_call(
        functools.partial(_inproj_kernel, n_x=len(x_args), n_ctx_tiles=n_ctx_tiles, f32_cols=tuple(f32_cols)),
        grid=(n // tn, t // tm),
        in_specs=x_specs + [pl.BlockSpec((1, d), lambda j, i: (0, 0)),
                            pl.BlockSpec((1, 6, d), lambda j, i: (mrow(i), 0, 0)),
                            pl.BlockSpec((d, tn), lambda j, i: (0, j))],
        out_specs=out_specs,
        out_shape=out_shape,
        compiler_params=_cparams("arbitrary", "arbitrary"),
        name="in_proj",
    )(*x_args, g.reshape(1, d), mod, w_bf)


def _hgrn_decay(z, lb, reverse):
    nb = HGRN_BLOCK
    row = lax.broadcasted_iota(jnp.int32, (nb, nb), 0)
    col = lax.broadcasted_iota(jnp.int32, (nb, nb), 1)
    tri = jnp.where((col >= row) if reverse else (col <= row), 1.0, 0.0).astype(BF16)
    logf = jnp.log(lb + (1.0 - lb) * jax.nn.sigmoid(z))
    k = (1.0 - lb) * jax.nn.sigmoid(-z)
    l2 = logf * LOG2_E
    hi = l2.astype(BF16)
    lo = (l2 - hi.astype(F32)).astype(BF16)
    c = jnp.dot(tri, hi, preferred_element_type=F32) + jnp.dot(tri, lo, preferred_element_type=F32)
    return c, k


def _hgrn_block(q, v, st_t, reverse, c_ref, k_ref, ecat):
    nb = HGRN_BLOCK
    row = lax.broadcasted_iota(jnp.int32, (nb, nb), 0)
    col = lax.broadcasted_iota(jnp.int32, (nb, nb), 1)
    c = c_ref[...]
    k = k_ref[...]
    edge = 0 if reverse else nb - 1
    c_edge = c_ref[edge:edge + 1, :]

    o = _bdot_nt(q * jnp.exp2(c), st_t)
    kd = k * jnp.exp2(c_edge - c)
    st_new = st_t * jnp.exp2(c_edge) + _bdot(v.T, kd)

    a = jnp.zeros((nb, nb), F32)
    m = nb // 2
    while m >= HGRN_DIAG:
        pieces = []
        for grp in range(nb // (2 * m)):
            idx = grp * 2 * m + (m if reverse else m - 1)
            pieces.append(jnp.broadcast_to(c_ref[idx:idx + 1, :], (2 * m, nb)))
        bnd = pieces[0] if len(pieces) == 1 else jnp.concatenate(pieces, axis=0)
        e = jnp.exp2(-jnp.abs(c - bnd))
        am = _bdot_nt(q * e, k * e)
        same = (row & ~(2 * m - 1)) == (col & ~(2 * m - 1))
        r_hi = (row & m) != 0
        c_hi = (col & m) != 0
        if reverse:
            mask = same & jnp.logical_not(r_hi) & c_hi
        else:
            mask = same & r_hi & jnp.logical_not(c_hi)
        a = jnp.where(mask, am, a)
        m //= 2

    dg = HGRN_DIAG
    rowi = lax.broadcasted_iota(jnp.int32, (dg, nb), 0)
    cols = []
    for s in range(dg):
        keep = (rowi <= s) if reverse else (rowi >= s)
        tiles = []
        for j in range(nb // dg):
            r = j * dg + s
            qj = q[j * dg:(j + 1) * dg, :]
            cj = c[j * dg:(j + 1) * dg, :]
            cs = jnp.broadcast_to(c_ref[r:r + 1, :], (dg, nb))
            ks = jnp.broadcast_to(k_ref[r:r + 1, :], (dg, nb))
            tiles.append(qj * ks * jnp.exp2(jnp.where(keep, cj - cs, -jnp.inf)))
        cols.append(jnp.concatenate(tiles, axis=0).astype(BF16))
    p = jnp.concatenate(cols, axis=1)
    red = jnp.dot(p, ecat, preferred_element_type=F32)
    a = jnp.where((row >> 3) == (col >> 3), red, a)
    o = o + _bdot(a, v)
    return o, st_new


def _hgrn_kernel(*refs, nblk, has_state, qscale):
    qa_ref, ia_ref, zf_ref, zb_ref, ga_ref, lb_ref, ng_ref, ecat_ref = refs[:8]
    s0_ref = refs[8] if has_state else None
    o_ref, sfin_ref, oacc_ref, c_scr, k_scr = refs[-5:]
    nb = HGRN_BLOCK

    def gates(direction, blk, slot):
        z_ref = zb_ref if direction == 1 else zf_ref
        r0 = blk * nb if isinstance(blk, int) else pl.multiple_of(blk * nb, nb)
        c, k = _hgrn_decay(z_ref[pl.ds(r0, nb), :], lb_ref[direction:direction + 1, :], direction == 1)
        c_scr[direction, slot] = c
        k_scr[direction, slot] = k

    def block(direction, blk, slot, st_t):
        r0 = blk * nb if isinstance(blk, int) else pl.multiple_of(blk * nb, nb)
        q = _silu(qa_ref[pl.ds(r0, nb), :].astype(F32)) * qscale
        v = ia_ref[pl.ds(r0, nb), :].astype(F32)
        o, st_new = _hgrn_block(q, v, st_t, direction == 1, c_scr.at[direction, slot],
                                k_scr.at[direction, slot], ecat_ref[...])
        oacc_ref[direction, pl.ds(r0, nb), :] = o
        return st_new

    gates(0, 0, 0)
    gates(1, nblk - 1, 0)

    def body(i, carry):
        st_f, st_b = carry
        slot = lax.rem(i, 2)
        st_f = block(0, i, slot, st_f)
        st_b = block(1, nblk - 1 - i, slot, st_b)
        gates(0, i + 1, 1 - slot)
        gates(1, nblk - 2 - i, 1 - slot)
        return st_f, st_b

    if has_state:
        st0 = (s0_ref[0].T, s0_ref[1].T)
    else:
        zero = jnp.zeros((ia_ref.shape[1], qa_ref.shape[1]), F32)
        st0 = (zero, zero)
    st_f, st_b = lax.fori_loop(0, nblk - 1, body, st0)
    last_slot = (nblk - 1) % 2
    st_f = block(0, nblk - 1, last_slot, st_f)
    st_b = block(1, 0, last_slot, st_b)
    sfin_ref[0] = st_f.T
    sfin_ref[1] = st_b.T

    o = oacc_ref[0] + oacc_ref[1]
    o_ref[...] = (_rms(o, ng_ref[0]) * _silu(ga_ref[...].astype(F32))).astype(o_ref.dtype)


def _hgrn(proj16, proj32, lb, norm_g, s0, layer_idx, out_prev, n_seq, seq, row_blk0, heads, key_d, val_d):
    t = proj16.shape[0]
    has_state = s0 is not None
    hk = heads * key_d // 128
    off_qa, off_ia, off_ga = 0, hk, 2 * hk
    off_zf, off_zb = 0, hk

    def col(off):
        return pl.BlockSpec((seq, 128), lambda b, h: (row_blk0 + b, off + h))

    lane = np.arange(key_d)[None, :]
    srow = np.repeat(np.arange(HGRN_DIAG), key_d)[:, None]
    ecat = jnp.asarray((lane % HGRN_DIAG == srow).astype(np.float32), dtype=BF16)
    in_specs = [col(off_qa), col(off_ia), col(off_zf), col(off_zb), col(off_ga),
                pl.BlockSpec((2, key_d), lambda b, h: (0, h)),
                pl.BlockSpec((1, 1, val_d), lambda b, h: (h, 0, 0)),
                pl.BlockSpec((HGRN_DIAG * key_d, key_d), lambda b, h: (0, 0))]
    args = [proj16, proj16, proj32, proj32, proj16, lb, norm_g.reshape(heads, 1, val_d), ecat]
    if has_state:
        in_specs.append(pl.BlockSpec((None, None, 2, None, key_d, val_d),
                                     lambda b, h: (b, layer_idx, 0, h, 0, 0)))
        args.append(s0)
    aliases = {}
    if out_prev is not None:
        in_specs.append(pl.BlockSpec(memory_space=pl.ANY))
        args.append(out_prev)
        aliases = {len(args) - 1: 0}

    kern = functools.partial(_hgrn_kernel, nblk=seq // HGRN_BLOCK, has_state=has_state,
                             qscale=float(key_d) ** -0.5)
    return pl.pallas_call(
        kern,
        grid=(n_seq, heads),
        in_specs=in_specs,
        out_specs=[pl.BlockSpec((seq, val_d), lambda b, h: (row_blk0 + b, h)),
                   pl.BlockSpec((None, 2, None, key_d, val_d), lambda b, h: (b, 0, h, 0, 0))],
        out_shape=[jax.ShapeDtypeStruct((t, heads * val_d), BF16),
                   jax.ShapeDtypeStruct((n_seq, 2, heads, key_d, val_d), F32)],
        scratch_shapes=[pltpu.VMEM((2, seq, val_d), F32),
                        pltpu.VMEM((2, 2, HGRN_BLOCK, key_d), F32),
                        pltpu.VMEM((2, 2, HGRN_BLOCK, key_d), F32)],
        input_output_aliases=aliases,
        compiler_params=_cparams("arbitrary", "arbitrary"),
        name="hgrn2",
    )(*args)


def _rope(x, cos, sin_signed):
    return x * cos + pltpu.roll(x, x.shape[-1] // 2, 1) * sin_signed


def _attn_kernel(*refs, has_ctx, grp, qblk, scale):
    if has_ctx:
        (q_ref, k_ref, v_ref, qn_ref, kn_ref, ck_ref, cv_ref, cos_ref, sin_ref, _prev, o_ref) = refs
    else:
        (q_ref, k_ref, v_ref, qn_ref, kn_ref, o_ref, ko_ref, vo_ref) = refs
    seq, hd = k_ref.shape
    k = _rms(k_ref[...], kn_ref[...])
    v = v_ref[...]
    if has_ctx:
        k = _rope(k, cos_ref[...], sin_ref[...])
        ck = ck_ref[...].astype(BF16)
        cv = cv_ref[...].astype(BF16)
    else:
        ko_ref[...] = k
        vo_ref[...] = v
    kb = k.astype(BF16)
    vb = v.astype(BF16)

    for g in range(grp):
        def body(i, carry, g=g):
            r0 = pl.multiple_of(i * qblk, qblk)
            q = _rms(q_ref[pl.ds(r0, qblk), g * hd:(g + 1) * hd].astype(F32), qn_ref[...])
            if has_ctx:
                q = _rope(q, cos_ref[pl.ds(r0, qblk), :], sin_ref[pl.ds(r0, qblk), :])
            qb = q.astype(BF16)
            s2 = lax.dot_general(qb, kb, _NT, preferred_element_type=F32) * scale
            m = jnp.max(s2, axis=-1, keepdims=True)
            if has_ctx:
                s1 = lax.dot_general(qb, ck, _NT, preferred_element_type=F32) * scale
                m = jnp.maximum(m, jnp.max(s1, axis=-1, keepdims=True))
                p1 = jnp.exp(s1 - m)
            p2 = jnp.exp(s2 - m)
            den = jnp.sum(p2, axis=-1, keepdims=True)
            acc = jnp.dot(p2.astype(BF16), vb, preferred_element_type=F32)
            if has_ctx:
                den = den + jnp.sum(p1, axis=-1, keepdims=True)
                acc = acc + jnp.dot(p1.astype(BF16), cv, preferred_element_type=F32)
            o_ref[pl.ds(r0, qblk), g * hd:(g + 1) * hd] = (acc / den).astype(o_ref.dtype)
            return carry

        lax.fori_loop(0, seq // qblk, body, 0)


def _attn(proj16, proj32, qn_g, kn_g, ctx, layer_idx, out_prev, n_seq, seq, row_blk0, heads, kv_heads, hd,
          col_q, col_kv):
    t = proj16.shape[0]
    grp = heads // kv_heads
    has_ctx = ctx is not None
    cq = col_q // grp
    ck0 = col_kv
    cv0 = ck0 + kv_heads
    in_specs = [pl.BlockSpec((seq, grp * hd), lambda b, h: (row_blk0 + b, cq + h)),
                pl.BlockSpec((seq, hd), lambda b, h: (row_blk0 + b, ck0 + h)),
                pl.BlockSpec((seq, hd), lambda b, h: (row_blk0 + b, cv0 + h)),
                pl.BlockSpec((1, hd), lambda b, h: (0, 0)),
                pl.BlockSpec((1, hd), lambda b, h: (0, 0))]
    args = [proj16, proj32, proj32, qn_g.reshape(1, hd), kn_g.reshape(1, hd)]
    out_specs = [pl.BlockSpec((seq, grp * hd), lambda b, h: (row_blk0 + b, h))]
    out_shape = [jax.ShapeDtypeStruct((t, heads * hd), BF16)]
    aliases = {}
    if has_ctx:
        cache_k, cache_v, cos, sin_signed = ctx
        nreq, nlay, past = cache_k.shape[:3]
        cache_k = cache_k.reshape(nreq, nlay, past, kv_heads * hd)
        cache_v = cache_v.reshape(nreq, nlay, past, kv_heads * hd)
        spec_c = pl.BlockSpec((None, None, past, hd), lambda b, h: (b, layer_idx, 0, h))
        in_specs += [spec_c, spec_c,
                     pl.BlockSpec((seq, hd), lambda b, h: (0, 0)),
                     pl.BlockSpec((seq, hd), lambda b, h: (0, 0)),
                     pl.BlockSpec(memory_space=pl.ANY)]
        args += [cache_k, cache_v, cos, sin_signed, out_prev]
        aliases = {len(args) - 1: 0}
    else:
        out_specs += [pl.BlockSpec((seq, hd), lambda b, h: (b, h)),
                      pl.BlockSpec((seq, hd), lambda b, h: (b, h))]
        out_shape += [jax.ShapeDtypeStruct((n_seq * seq, kv_heads * hd), F32)] * 2
    return pl.pallas_call(
        functools.partial(_attn_kernel, has_ctx=has_ctx, grp=grp, qblk=min(seq, 256), scale=float(hd) ** -0.5),
        grid=(n_seq, kv_heads),
        in_specs=in_specs,
        out_specs=out_specs,
        out_shape=out_shape,
        input_output_aliases=aliases,
        compiler_params=_cparams("arbitrary", "arbitrary"),
        name="gqa",
    )(*args)


def _ret_kernel(*refs, nchunk, has_ctx, kscale):
    if has_ctx:
        (lg_ref, q_ref, k_ref, v_ref, g_ref, gn_ref, s0_ref, cos_ref, sin_ref, _prev,
         o_ref, sfin_ref, sb_ref) = refs
    else:
        (lg_ref, q_ref, k_ref, v_ref, g_ref, gn_ref, o_ref, sfin_ref, sb_ref) = refs
    cs = RET_CHUNK
    kd_dim = q_ref.shape[1]
    h = pl.program_id(1)
    lgf = lg_ref[0, h]
    lgb = lg_ref[1, h]
    row = lax.broadcasted_iota(jnp.int32, (cs, cs), 0)
    col = lax.broadcasted_iota(jnp.int32, (cs, cs), 1)
    rel = (row - col).astype(F32)
    dmat = (jnp.exp(jnp.where(rel >= 0, rel * lgf, -jnp.inf))
            + jnp.exp(jnp.where(rel <= 0, -rel * lgb, -jnp.inf)))
    j = lax.broadcasted_iota(jnp.int32, (cs, 1), 0).astype(F32)
    one = jnp.ones((1, 1), F32)
    dec_f = jnp.exp(one * (cs * lgf))
    dec_b = jnp.exp(one * (cs * lgb))

    def load_qk(ref, r0, mul):
        x = ref[pl.ds(r0, cs), :].astype(F32) * mul
        if has_ctx:
            half = kd_dim // 2
            x1, x2 = x[:, :half], x[:, half:]
            cos = cos_ref[pl.ds(r0, cs), :]
            sin = sin_ref[pl.ds(r0, cs), :]
            x = jnp.concatenate([x1 * cos - x2 * sin, x2 * cos + x1 * sin], axis=1)
        return x

    def bwd_body(i, st):
        n = nchunk - 1 - i
        r0 = pl.multiple_of(n * cs, cs)
        sb_ref[n] = st
        k = load_qk(k_ref, r0, kscale)
        v = v_ref[pl.ds(r0, cs), :]
        return st * dec_b + _bdot((k * jnp.exp(j * lgb)).T, v)

    st_b0 = s0_ref[1] if has_ctx else jnp.zeros(sfin_ref.shape[1:], F32)
    st_b = lax.fori_loop(0, nchunk, bwd_body, st_b0)
    sfin_ref[1] = st_b

    def fwd_body(n, st):
        r0 = pl.multiple_of(n * cs, cs)
        q = load_qk(q_ref, r0, 1.0)
        k = load_qk(k_ref, r0, kscale)
        v = v_ref[pl.ds(r0, cs), :]
        a = _bdot_nt(q, k) * dmat
        o = _bdot(a, v)
        if has_ctx or nchunk > 1:
            o = o + _bdot(q * jnp.exp((j + 1.0) * lgf), st)
            o = o + _bdot(q * jnp.exp((cs - j) * lgb), sb_ref[n])
        mu = jnp.mean(o, axis=-1, keepdims=True)
        var = jnp.mean(jnp.square(o - mu), axis=-1, keepdims=True)
        y = (o - mu) * lax.rsqrt(var + EPS) * gn_ref[0]
        o_ref[pl.ds(r0, cs), :] = (_silu(g_ref[pl.ds(r0, cs), :].astype(F32)) * y).astype(o_ref.dtype)
        return st * dec_f + _bdot((k * jnp.exp((cs - 1.0 - j) * lgf)).T, v)

    st_f0 = s0_ref[0] if has_ctx else jnp.zeros(sfin_ref.shape[1:], F32)
    sfin_ref[0] = lax.fori_loop(0, nchunk, fwd_body, st_f0)


def _retention(proj, log_gamma, gn_g, ctx, layer_idx, out_prev, n_seq, seq, row_blk0, heads, key_d, val_d):
    t = proj.shape[0]
    has_ctx = ctx is not None
    nq = heads * key_d
    cq, ck, cv, cg = 0, nq // key_d, 2 * nq // val_d, (2 * nq + heads * val_d) // val_d
    in_specs = [pl.BlockSpec(memory_space=pltpu.SMEM),
                pl.BlockSpec((seq, key_d), lambda b, h: (row_blk0 + b, cq + h)),
                pl.BlockSpec((seq, key_d), lambda b, h: (row_blk0 + b, ck + h)),
                pl.BlockSpec((seq, val_d), lambda b, h: (row_blk0 + b, cv + h)),
                pl.BlockSpec((seq, val_d), lambda b, h: (row_blk0 + b, cg + h)),
                pl.BlockSpec((1, 1, val_d), lambda b, h: (h, 0, 0))]
    args = [log_gamma, proj, proj, proj, proj, gn_g.reshape(heads, 1, val_d)]
    aliases = {}
    if has_ctx:
        s0, cos, sin = ctx
        in_specs += [pl.BlockSpec((None, None, 2, None, key_d, val_d), lambda b, h: (b, layer_idx, 0, h, 0, 0)),
                     pl.BlockSpec((seq, key_d // 2), lambda b, h: (0, 0)),
                     pl.BlockSpec((seq, key_d // 2), lambda b, h: (0, 0)),
                     pl.BlockSpec(memory_space=pl.ANY)]
        args += [s0, cos, sin, out_prev]
        aliases = {len(args) - 1: 0}
    nchunk = seq // RET_CHUNK
    return pl.pallas_call(
        functools.partial(_ret_kernel, nchunk=nchunk, has_ctx=has_ctx, kscale=float(key_d) ** -0.5),
        grid=(n_seq, heads),
        in_specs=in_specs,
        out_specs=[pl.BlockSpec((seq, val_d), lambda b, h: (row_blk0 + b, h)),
                   pl.BlockSpec((None, 2, None, key_d, val_d), lambda b, h: (b, 0, h, 0, 0))],
        out_shape=[jax.ShapeDtypeStruct((t, heads * val_d), BF16),
                   jax.ShapeDtypeStruct((n_seq, 2, heads, key_d, val_d), F32)],
        scratch_shapes=[pltpu.VMEM((nchunk, key_d, val_d), F32)],
        input_output_aliases=aliases,
        compiler_params=_cparams("arbitrary", "arbitrary"),
        name="retention",
    )(*args)


def _outproj_kernel(*refs, n_in, n_x, n_ctx_tiles, n_groups, n_experts):
    mix_refs = refs[:n_in]
    w_refs = refs[n_in:2 * n_in]
    x_refs = refs[2 * n_in:2 * n_in + n_x]
    (mod_ref, g2_ref, wr_ref, br_ref, xn_ref, h2_ref, meta_ref, ew_ref, cnt_out_ref,
     cnt_ref) = refs[2 * n_in + n_x:]
    acc = _bdot(mix_refs[0][...], w_refs[0][...])
    for mref, wref in zip(mix_refs[1:], w_refs[1:]):
        acc = acc + _bdot(mref[...], wref[...])
    xn = _token_rows(x_refs, pl.program_id(0), n_ctx_tiles) + mod_ref[0, 2:3, :] * acc
    xn_ref[...] = xn
    h2 = _rms(xn, g2_ref[...]) * (1.0 + mod_ref[0, 4:5, :]) + mod_ref[0, 3:4, :]
    h2_ref[...] = h2.astype(BF16).reshape(h2_ref.shape)
    h_hi = h2.astype(BF16)
    h_lo = (h2 - h_hi.astype(F32)).astype(BF16)
    lg = (jnp.dot(h_hi, wr_ref[0], preferred_element_type=F32)
          + jnp.dot(h_hi, wr_ref[1], preferred_element_type=F32)
          + jnp.dot(h_lo, wr_ref[0], preferred_element_type=F32)) + br_ref[...]

    per_grp = n_experts // n_groups
    lane = lax.broadcasted_iota(jnp.int32, lg.shape, 1).astype(F32)
    big = float(lg.shape[1])
    ninf = -jnp.inf
    gl = jnp.where(lane < n_groups, lg, ninf)
    gmax = jnp.max(gl, axis=-1, keepdims=True)
    p_grp = 1.0 / jnp.sum(jnp.exp(gl - gmax), axis=-1, keepdims=True)
    gidx = jnp.min(jnp.where(gl == gmax, lane, big), axis=-1, keepdims=True)
    in_grp = ((lane >= n_groups) & (lane < n_groups + n_experts)
              & (jnp.floor((lane - n_groups) / per_grp) == gidx))
    el = jnp.where(in_grp, lg, ninf)
    emax = jnp.max(el, axis=-1, keepdims=True)
    esum = jnp.sum(jnp.exp(el - emax), axis=-1, keepdims=True)
    i1 = jnp.min(jnp.where(el == emax, lane, big), axis=-1, keepdims=True)
    el2 = jnp.where(lane == i1, ninf, el)
    emax2 = jnp.max(el2, axis=-1, keepdims=True)
    i2 = jnp.min(jnp.where(el2 == emax2, lane, big), axis=-1, keepdims=True)
    p1 = 1.0 / esum
    p2 = jnp.exp(emax2 - emax) / esum
    w1 = p1 / (p1 + p2) * p_grp
    w2 = p2 / (p1 + p2) * p_grp
    ew_ref[...] = jnp.where(lane == 0.0, w1, jnp.where(lane == 1.0, w2, 0.0))

    step = pl.program_id(0)

    @pl.when(step == 0)
    def _():
        cnt_ref[...] = jnp.zeros_like(cnt_ref)

    tm = lg.shape[0]
    hit1 = lane == i1
    hit2 = lane == i2
    sel = jnp.where(hit1 | hit2, 1.0, 0.0)
    rr = lax.broadcasted_iota(jnp.int32, (tm, tm), 0)
    cc = lax.broadcasted_iota(jnp.int32, (tm, tm), 1)
    before = jnp.where(cc < rr, 1.0, 0.0).astype(BF16)
    prior = jnp.dot(before, sel.astype(BF16), preferred_element_type=F32) + cnt_ref[...]
    rank1 = jnp.sum(jnp.where(hit1, prior, 0.0), axis=-1, keepdims=True)
    rank2 = jnp.sum(jnp.where(hit2, prior, 0.0), axis=-1, keepdims=True)
    cnt_ref[...] += jnp.sum(sel, axis=0, keepdims=True)
    cnt_out_ref[...] = cnt_ref[...]
    meta = jnp.where(lane == 0.0, i1 - n_groups,
                     jnp.where(lane == 1.0, i2 - n_groups,
                               jnp.where(lane == 2.0, rank1, jnp.where(lane == 3.0, rank2, 0.0))))
    meta_ref[0] = meta.T[0:8, :].astype(jnp.int32)


def _out_proj(mixes, w_parts, x, mod, g2, w_router, b_router, n_ctx_tiles, tiles_per_req, n_groups, n_experts):
    t = mixes[0].shape[0]
    d = w_parts[0].shape[1]
    tm = TOKEN_TILE
    n_in = len(mixes)
    mrow = functools.partial(_tile_mod_row, n_ctx_tiles=n_ctx_tiles, tiles_per_req=tiles_per_req)
    x_specs, x_args = _token_specs(x, tm, n_ctx_tiles, lambda i: i)
    in_specs = ([pl.BlockSpec((tm, m.shape[1]), lambda i: (i, 0)) for m in mixes]
                + [pl.BlockSpec(w.shape, lambda i: (0, 0)) for w in w_parts]
                + x_specs
                + [pl.BlockSpec((1, 6, d), lambda i: (mrow(i), 0, 0)),
                   pl.BlockSpec((1, d), lambda i: (0, 0)),
                   pl.BlockSpec(w_router.shape, lambda i: (0, 0, 0)),
                   pl.BlockSpec((1, 128), lambda i: (0, 0))])
    return pl.pallas_call(
        functools.partial(_outproj_kernel, n_in=n_in, n_x=len(x_args), n_ctx_tiles=n_ctx_tiles,
                          n_groups=n_groups, n_experts=n_experts),
        grid=(t // tm,),
        in_specs=in_specs,
        out_specs=[pl.BlockSpec((tm, d), lambda i: (i, 0)),
                   pl.BlockSpec((tm, d // 128, 128), lambda i: (i, 0, 0)),
                   pl.BlockSpec((1, 8, tm), lambda i: (i, 0, 0)),
                   pl.BlockSpec((tm, 128), lambda i: (i, 0)),
                   pl.BlockSpec((1, 128), lambda i: (0, 0))],
        out_shape=[jax.ShapeDtypeStruct((t, d), F32),
                   jax.ShapeDtypeStruct((t, d // 128, 128), BF16),
                   jax.ShapeDtypeStruct((t // tm, 8, tm), jnp.int32),
                   jax.ShapeDtypeStruct((t, 128), F32),
                   jax.ShapeDtypeStruct((1, 128), F32)],
        scratch_shapes=[pltpu.VMEM((1, 128), F32)],
        compiler_params=_cparams("arbitrary"),
        name="out_proj_router",
    )(*mixes, *w_parts, *x_args, mod, g2.reshape(1, d), w_router, b_router)


def _dispatch_plan(meta, counts, n_experts, tile):
    t = meta.shape[0] * meta.shape[2]
    e1, e2, rank1, rank2 = (meta[:, k, :].reshape(t) for k in range(4))
    padded = ((counts + tile - 1) // tile) * tile
    ends = jnp.cumsum(padded)
    starts = ends - padded
    experts = jnp.arange(n_experts, dtype=jnp.int32)[None, :]
    pos = [jnp.sum(jnp.where(e[:, None] == experts, starts[None, :], 0), axis=1) + r
           for e, r in ((e1, rank1), (e2, rank2))]
    n_tiles = (2 * t) // tile + n_experts
    n_used = (ends[-1] // tile).astype(jnp.int32)
    tile_start = jnp.arange(n_tiles, dtype=jnp.int32) * tile
    probe = jnp.minimum(tile_start, ends[-1] - 1)
    tile_expert = jnp.sum((ends[None, :] <= probe[:, None]).astype(jnp.int32), axis=1)
    return tile_expert, n_used.reshape(1), ends.astype(jnp.int32), pos[0], pos[1], n_tiles


def _row_source_kernel(ends_ref, p1_ref, p2_ref, o_ref, *, n_tok, n_rows, tile, n_experts):
    def clear8(j, c):
        base = lax.rem(j * 8, n_tok)
        for u in range(8):
            o_ref[j * 8 + u] = base + u
        return c

    def per_expert(e, c):
        first = lax.shift_right_logical(jnp.maximum(ends_ref[e] - tile, 0), 3)
        lax.fori_loop(first, first + tile // 8, clear8, 0)
        return c

    g = pl.program_id(0)

    @pl.when(g == 0)
    def _():
        lax.fori_loop(0, n_experts, per_expert, 0)
        lax.fori_loop(lax.shift_right_logical(ends_ref[n_experts - 1], 3), n_rows // 8, clear8, 0)

    chunk = p1_ref.shape[2]

    def place(r, c):
        t = g * chunk + r
        o_ref[p1_ref[0, 0, r]] = t
        o_ref[p2_ref[0, 0, r]] = t
        return c

    lax.fori_loop(0, chunk, place, 0, unroll=8)


def _row_source(ends, pos1, pos2, n_rows, tile):
    n_tok = pos1.shape[0]
    chunk = ROW_SOURCE_CHUNK
    smem = pl.BlockSpec(memory_space=pltpu.SMEM)
    pos_spec = pl.BlockSpec((1, 1, chunk), lambda g: (g, 0, 0), memory_space=pltpu.SMEM)
    return pl.pallas_call(
        functools.partial(_row_source_kernel, n_tok=n_tok, n_rows=n_rows, tile=tile, n_experts=ends.shape[0]),
        grid=(n_tok // chunk,),
        in_specs=[smem, pos_spec, pos_spec],
        out_specs=smem,
        out_shape=jax.ShapeDtypeStruct((n_rows,), jnp.int32),
        compiler_params=_cparams("arbitrary"),
        name="moe_row_source",
    )(ends, pos1.reshape(n_tok // chunk, 1, chunk), pos2.reshape(n_tok // chunk, 1, chunk))


def _moe_kernel(te_ref, nu_ref, *refs, tm, ahead):
    tok_refs = refs[:ahead + 1]
    h_hbm, wg_ref, wu_ref, wd_ref, o_ref, wg_bf, wu_bf, wd_bf, buf, sem = refs[ahead + 1:]
    i = pl.program_id(0)
    nslot = ahead + 1
    slot = lax.rem(i, nslot)
    n_used = nu_ref[0]

    def issue(tile, tok_ref):
        s = lax.rem(tile, nslot)

        def body(j, c):
            for u in range(2):
                r = 2 * j + u
                pltpu.make_async_copy(h_hbm.at[tok_ref[0, 0, r]], buf.at[s, r], sem.at[s]).start(priority=u)
            return c

        lax.fori_loop(0, tm // 2, body, 0, unroll=4)

    @pl.when(i == 0)
    def _():
        for j in range(ahead):
            @pl.when(j < n_used)
            def _():
                issue(jnp.int32(j), tok_refs[j])

    new_expert = jnp.logical_or(i == 0, te_ref[i] != te_ref[jnp.maximum(i - 1, 0)])

    @pl.when(new_expert)
    def _():
        wg_bf[...] = wg_ref[...].astype(BF16)
        wu_bf[...] = wu_ref[...].astype(BF16)
        wd_bf[...] = wd_ref[...].astype(BF16)

    def compute(issue_next):
        s_next = lax.rem(i + ahead, nslot)
        quarter = tm // 4

        def start_rows(part):
            if issue_next:
                for r in range(part * quarter, (part + 1) * quarter):
                    pltpu.make_async_copy(h_hbm.at[tok_refs[ahead][0, 0, r]], buf.at[s_next, r],
                                          sem.at[s_next]).start(priority=r % 2)

        pltpu.make_async_copy(h_hbm.at[pl.ds(0, tm)], buf.at[slot], sem.at[slot]).wait()
        start_rows(0)
        x = buf[slot].reshape(tm, -1)
        g = jnp.dot(x, wg_bf[...], preferred_element_type=F32)
        start_rows(1)
        u = jnp.dot(x, wu_bf[...], preferred_element_type=F32)
        start_rows(2)
        act = _silu(g) * u
        y = jnp.dot(act.astype(BF16), wd_bf[...], preferred_element_type=F32)
        start_rows(3)
        o_ref[...] = y.astype(BF16).reshape(o_ref.shape)

    @pl.when(i + ahead < n_used)
    def _():
        compute(True)

    @pl.when(jnp.logical_and(i + ahead >= n_used, i < n_used))
    def _():
        compute(False)

    @pl.when(i >= n_used)
    def _():
        o_ref[...] = jnp.zeros_like(o_ref)


def _moe_experts(h3, src_tok, tile_expert, n_used, w_gate, w_up, w_down, layer_idx, tm):
    t, sub, lanes = h3.shape
    d = sub * lanes
    n_tiles = src_tok.shape[0] // tm
    f = w_gate.shape[-1]
    ahead = MOE_GATHER_AHEAD
    tok = src_tok.reshape(n_tiles, 1, tm)

    def tok_spec(k):
        return pl.BlockSpec((1, 1, tm), lambda i, te, nu: (jnp.minimum(i + k, n_tiles - 1), 0, 0),
                            memory_space=pltpu.SMEM)

    return pl.pallas_call(
        functools.partial(_moe_kernel, tm=tm, ahead=ahead),
        grid_spec=pltpu.PrefetchScalarGridSpec(
            num_scalar_prefetch=2,
            grid=(n_tiles,),
            in_specs=[tok_spec(k) for k in range(ahead + 1)] + [
                      pl.BlockSpec(memory_space=pl.ANY),
                      pl.BlockSpec((None, None, d, f), lambda i, te, nu: (layer_idx, te[i], 0, 0)),
                      pl.BlockSpec((None, None, d, f), lambda i, te, nu: (layer_idx, te[i], 0, 0)),
                      pl.BlockSpec((None, None, f, d), lambda i, te, nu: (layer_idx, te[i], 0, 0))],
            out_specs=pl.BlockSpec((tm, sub, lanes), lambda i, te, nu: (i, 0, 0)),
            scratch_shapes=[pltpu.VMEM((d, f), BF16), pltpu.VMEM((d, f), BF16), pltpu.VMEM((f, d), BF16),
                            pltpu.VMEM((ahead + 1, tm, sub, lanes), BF16),
                            pltpu.SemaphoreType.DMA((ahead + 1,))]),
        out_shape=jax.ShapeDtypeStruct((n_tiles * tm, sub, lanes), BF16),
        compiler_params=_cparams("arbitrary"),
        name="moe_experts",
    )(tile_expert, n_used, *([tok] * (ahead + 1)), h3, w_gate, w_up, w_down)


def _combine_kernel(p1_ref, p2_ref, p1n_ref, p2n_ref, y_hbm, x_ref, ew_ref, mod_ref, gf_ref, *rest,
                    tm, n_steps, n_ctx_tiles, final_norm):
    out_refs, (buf, sem) = rest[:-2], rest[-2:]
    i = pl.program_id(0)
    slot = lax.rem(i, 2)

    def issue(s, pa_ref, pb_ref):
        def body(r, c):
            pltpu.make_async_copy(y_hbm.at[pa_ref[0, 0, r]], buf.at[s, 0, r], sem.at[s, 0]).start(priority=0)
            pltpu.make_async_copy(y_hbm.at[pb_ref[0, 0, r]], buf.at[s, 1, r], sem.at[s, 1]).start(priority=1)
            return c

        lax.fori_loop(0, tm, body, 0, unroll=8)

    @pl.when(i == 0)
    def _():
        issue(0, p1_ref, p2_ref)

    @pl.when(i + 1 < n_steps)
    def _():
        issue(1 - slot, p1n_ref, p2n_ref)

    for k in range(2):
        pltpu.make_async_copy(y_hbm.at[pl.ds(0, tm)], buf.at[slot, k], sem.at[slot, k]).wait()
    ew = ew_ref[...]
    y = (ew[:, 0:1] * buf[slot, 0].reshape(tm, -1).astype(F32)
         + ew[:, 1:2] * buf[slot, 1].reshape(tm, -1).astype(F32))
    x = x_ref[...] + mod_ref[0, 5:6, :] * y
    if not final_norm:
        out_refs[0][...] = x
    else:
        x = _rms(x, gf_ref[...])
        ctx_ref, lat_ref = out_refs

        @pl.when(i < n_ctx_tiles)
        def _():
            ctx_ref[...] = x

        @pl.when(i >= n_ctx_tiles)
        def _():
            lat_ref[...] = x


def _moe_combine(y_sorted, pos1, pos2, x, ew, mod, final_g, n_ctx_tiles, tiles_per_req, final_norm):
    t, d = x.shape
    tm = TOKEN_TILE
    n_steps = t // tm
    mrow = functools.partial(_tile_mod_row, n_ctx_tiles=n_ctx_tiles, tiles_per_req=tiles_per_req)
    smem_rows = pl.BlockSpec((1, 1, tm), lambda i: (i, 0, 0), memory_space=pltpu.SMEM)
    smem_next = pl.BlockSpec((1, 1, tm), lambda i: (jnp.minimum(i + 1, n_steps - 1), 0, 0),
                             memory_space=pltpu.SMEM)
    p1 = pos1.reshape(n_steps, 1, tm)
    p2 = pos2.reshape(n_steps, 1, tm)
    if final_norm:
        out_specs = [pl.BlockSpec((tm, d), lambda i: (jnp.minimum(i, n_ctx_tiles - 1), 0)),
                     pl.BlockSpec((tm, d), lambda i: (jnp.maximum(i - n_ctx_tiles, 0), 0))]
        out_shape = [jax.ShapeDtypeStruct((n_ctx_tiles * tm, d), F32),
                     jax.ShapeDtypeStruct((t - n_ctx_tiles * tm, d), F32)]
    else:
        out_specs = pl.BlockSpec((tm, d), lambda i: (i, 0))
        out_shape = jax.ShapeDtypeStruct((t, d), F32)
    return pl.pallas_call(
        functools.partial(_combine_kernel, tm=tm, n_steps=n_steps, n_ctx_tiles=n_ctx_tiles,
                          final_norm=final_norm),
        grid=(n_steps,),
        in_specs=[smem_rows, smem_rows, smem_next, smem_next,
                  pl.BlockSpec(memory_space=pl.ANY),
                  pl.BlockSpec((tm, d), lambda i: (i, 0)),
                  pl.BlockSpec((tm, 128), lambda i: (i, 0)),
                  pl.BlockSpec((1, 6, d), lambda i: (mrow(i), 0, 0)),
                  pl.BlockSpec((1, d), lambda i: (0, 0))],
        out_specs=out_specs,
        out_shape=out_shape,
        scratch_shapes=[pltpu.VMEM((2, 2, tm) + y_sorted.shape[1:], y_sorted.dtype),
                        pltpu.SemaphoreType.DMA((2, 2))],
        compiler_params=_cparams("arbitrary"),
        name="moe_combine",
    )(p1, p2, p1, p2, y_sorted, x, ew, mod, final_g.reshape(1, d))


def _axial_angles(seq, head_dim):
    rows = seq // GRID_W
    row = jnp.repeat(jnp.arange(rows), GRID_W).astype(F32)
    col = jnp.tile(jnp.arange(GRID_W), rows).astype(F32)
    n_freq = head_dim // 4
    inv = ROPE_BASE ** (-jnp.arange(n_freq, dtype=F32) / n_freq)
    return jnp.concatenate([row[:, None] * inv, col[:, None] * inv], axis=-1)


def kernel(x_prompt, x_sample, cache_attn_k, cache_attn_v, state_hgrn, state_ret, c, c_ctx, mod_w, mod_b, norm1_g, norm2_g, even_w_in, even_w_out, hgrn_lb_logits, hgrn_norm_g, attn_qn_g, attn_kn_g, odd_w_in, odd_w_out, ret_decay_logit, ret_gn_g, moe_w_group, moe_b_group, moe_w_expert, moe_b_expert, moe_w_gate, moe_w_up, moe_w_down, final_norm_g):
    batch, seq, d = x_prompt.shape
    dec_batch, dec_seq, _ = x_sample.shape
    depth = mod_w.shape[0]
    heads_a, val_a = hgrn_norm_g.shape[1:]
    key_a = hgrn_lb_logits.shape[2] // heads_a
    hd_b = attn_qn_g.shape[1]
    kv_b = cache_attn_k.shape[3]
    heads_b = (even_w_out.shape[1] - heads_a * val_a) // hd_b
    heads_c, val_c = ret_gn_g.shape[1:]
    key_c = state_ret.shape[4]
    n_experts = moe_w_expert.shape[2]
    n_groups = moe_w_group.shape[2]
    tm = TOKEN_TILE
    n_ctx = batch * seq
    n_ctx_tiles = n_ctx // tm
    tiles_per_req = dec_seq // tm
    assert seq % tm == 0 and dec_seq % tm == 0 and n_ctx % dec_seq == 0
    assert key_a == 128 and val_a == 128 and hd_b == 128 and n_groups + n_experts <= 128

    x = (x_prompt.reshape(n_ctx, d), x_sample.reshape(dec_batch * dec_seq, d))
    t = n_ctx + dec_batch * dec_seq

    n_vec = 1 + dec_batch
    rows = -(-n_vec // 8) * 8
    cvecs = jnp.zeros((rows, d), F32).at[0].set(c_ctx).at[1:n_vec].set(c)
    mods = _ada_mod(cvecs, mod_w, mod_b).reshape(depth, rows, 6, d)

    lower_bounds = jnp.cumsum(jax.nn.softmax(hgrn_lb_logits.astype(F32), axis=0), axis=0)
    log_gamma = jax.nn.log_sigmoid(ret_decay_logit.astype(F32))

    ang_b = _axial_angles(dec_seq, hd_b)
    cos_b = jnp.concatenate([jnp.cos(ang_b), jnp.cos(ang_b)], axis=-1)
    sin_b = jnp.concatenate([-jnp.sin(ang_b), jnp.sin(ang_b)], axis=-1)
    ang_c = _axial_angles(dec_seq, key_c)
    cos_c, sin_c = jnp.cos(ang_c), jnp.sin(ang_c)

    pad = 128 - n_groups - n_experts
    new_k = new_v = new_sh = new_sr = None
    assert depth == 2, "one HGRN2/GQA layer followed by one retention layer"
    for l in range(depth):
        mod = mods[l]
        if l % 2 == 0:
            e = l // 2
            na, nva, nqb, nkv = heads_a * key_a, heads_a * val_a, heads_b * hd_b, kv_b * hd_b
            o_zf, o_kb = na + nva, 3 * na + 2 * nva + nqb
            w_in = even_w_in[e].astype(BF16)
            proj16, proj32 = _in_proj(x, norm1_g[l], mod, w_in, t, n_ctx_tiles, tiles_per_req, w_in.shape[1],
                                      f32_cols=((o_zf, o_zf + 2 * na), (o_kb, o_kb + 2 * nkv)))
            lb = lower_bounds[l]
            oa, s_ctx = _hgrn(proj16, proj32, lb, hgrn_norm_g[e], None, e, None, batch, seq, 0,
                              heads_a, key_a, val_a)
            oa, _ = _hgrn(proj16, proj32, lb, hgrn_norm_g[e], state_hgrn, e, oa, dec_batch, dec_seq,
                          n_ctx // dec_seq, heads_a, key_a, val_a)
            col_q, col_kv = (na + 2 * nva) // hd_b, (2 * na) // hd_b
            ob, kc, vc = _attn(proj16, proj32, attn_qn_g[e], attn_kn_g[e], None, e, None, batch, seq, 0,
                               heads_b, kv_b, hd_b, col_q, col_kv)
            (ob,) = _attn(proj16, proj32, attn_qn_g[e], attn_kn_g[e],
                          (cache_attn_k, cache_attn_v, cos_b, sin_b), e, ob,
                          dec_batch, dec_seq, n_ctx // dec_seq, heads_b, kv_b, hd_b, col_q, col_kv)
            new_k = kc.reshape(batch, 1, seq, kv_b, hd_b)
            new_v = vc.reshape(batch, 1, seq, kv_b, hd_b)
            new_sh = s_ctx[:, None]
            w_out = even_w_out[e].astype(BF16)
            mixes = [oa, ob]
            w_parts = [w_out[:heads_a * val_a], w_out[heads_a * val_a:]]
        else:
            o = l // 2
            w_in = odd_w_in[o].astype(BF16)
            proj = _in_proj(x, norm1_g[l], mod, w_in, t, n_ctx_tiles, tiles_per_req, w_in.shape[1] // 2)
            y, s_ctx = _retention(proj, log_gamma[o], ret_gn_g[o], None, o, None, batch, seq, 0,
                                  heads_c, key_c, val_c)
            y, _ = _retention(proj, log_gamma[o], ret_gn_g[o], (state_ret, cos_c, sin_c), o, y,
                              dec_batch, dec_seq, n_ctx // dec_seq, heads_c, key_c, val_c)
            new_sr = s_ctx[:, None]
            mixes = [y]
            w_parts = [odd_w_out[o].astype(BF16)]

        w_router = jnp.concatenate([moe_w_group[l], moe_w_expert[l], jnp.zeros((d, pad), F32)], axis=1)
        w_router_hi = w_router.astype(BF16)
        w_router = jnp.stack([w_router_hi, (w_router - w_router_hi.astype(F32)).astype(BF16)])
        b_router = jnp.concatenate([moe_b_group[l], moe_b_expert[l], jnp.zeros((pad,), F32)]).reshape(1, 128)
        xn, h2, meta, ew, cnt = _out_proj(mixes, w_parts, x, mod, norm2_g[l], w_router, b_router,
                                          n_ctx_tiles, tiles_per_req, n_groups, n_experts)
        counts = cnt[0, n_groups:n_groups + n_experts].astype(jnp.int32)
        tile_expert, n_used, ends, pos1, pos2, n_tiles = _dispatch_plan(meta, counts, n_experts, EXPERT_TILE)
        src_tok = _row_source(ends, pos1, pos2, n_tiles * EXPERT_TILE, EXPERT_TILE)
        y_sorted = _moe_experts(h2, src_tok, tile_expert, n_used, moe_w_gate, moe_w_up, moe_w_down, l, EXPERT_TILE)
        x = _moe_combine(y_sorted, pos1, pos2, xn, ew, mod, final_norm_g, n_ctx_tiles, tiles_per_req,
                         final_norm=(l == depth - 1))

    y_prompt = x[0].reshape(batch, seq, d)
    y_sample = x[1].reshape(dec_batch, dec_seq, d)
    return (y_prompt, y_sample, new_k, new_v, new_sh, new_sr)
```

```python
import functools

import numpy as np
import jax
import jax.numpy as jnp
from jax import lax
from jax.experimental import pallas as pl
from jax.experimental.pallas import tpu as pltpu

F32 = jnp.float32
BF16 = jnp.bfloat16
EPS = 1e-6
ROPE_BASE = 10000.0
GRID_W = 64
N_GROUPS = 4
TOP_K_INNER = 2
HGRN_BLOCK = 128
HGRN_DIAG = 8
RET_CHUNK = 256
TOKEN_TILE = 256
INPROJ_TILE = 512
EXPERT_TILE = 256
MOE_GATHER_AHEAD = 2
ROW_SOURCE_CHUNK = 512
V7X_VMEM_LIMIT_BYTES = 56 * 1024 * 1024
HIGHEST = lax.Precision.HIGHEST
LOG2_E = 1.4426950408889634

_NT = (((1,), (1,)), ((), ()))


def _cparams(*sem):
    return pltpu.CompilerParams(dimension_semantics=sem, vmem_limit_bytes=V7X_VMEM_LIMIT_BYTES)


def _rms(x, g):
    return x * lax.rsqrt(jnp.mean(x * x, axis=-1, keepdims=True) + EPS) * g


def _silu(x):
    return x * jax.nn.sigmoid(x)


def _bdot(a, b):
    return jnp.dot(a.astype(BF16), b.astype(BF16), preferred_element_type=F32)


def _bdot_nt(a, b):
    return lax.dot_general(a.astype(BF16), b.astype(BF16), _NT, preferred_element_type=F32)


def _mod_kernel(c_ref, w_ref, b_ref, o_ref):
    a = _silu(c_ref[...])
    o_ref[0] = jnp.dot(a, w_ref[0], precision=HIGHEST, preferred_element_type=F32) + b_ref[0]


def _ada_mod(cvecs, mod_w, mod_b):
    depth, d, n = mod_w.shape
    tn = n // 4
    rows = cvecs.shape[0]
    return pl.pallas_call(
        _mod_kernel,
        grid=(depth, n // tn),
        in_specs=[pl.BlockSpec((rows, d), lambda l, j: (0, 0)),
                  pl.BlockSpec((1, d, tn), lambda l, j: (l, 0, j)),
                  pl.BlockSpec((1, 1, tn), lambda l, j: (l, 0, j))],
        out_specs=pl.BlockSpec((1, rows, tn), lambda l, j: (l, 0, j)),
        out_shape=jax.ShapeDtypeStruct((depth, rows, n), F32),
        compiler_params=_cparams("arbitrary", "arbitrary"),
        name="ada_mod",
    )(cvecs, mod_w, mod_b.reshape(depth, 1, n))


def _tile_mod_row(i, n_ctx_tiles, tiles_per_req):
    return jnp.where(i < n_ctx_tiles, 0, 1 + jnp.maximum(i - n_ctx_tiles, 0) // tiles_per_req)


def _token_specs(x, tm, n_ctx_tiles, step_of):
    if not isinstance(x, tuple):
        return [pl.BlockSpec((tm, x.shape[1]), lambda *g: (step_of(*g), 0))], (x,)
    d = x[0].shape[1]
    return ([pl.BlockSpec((tm, d), lambda *g: (jnp.minimum(step_of(*g), n_ctx_tiles - 1), 0)),
             pl.BlockSpec((tm, d), lambda *g: (jnp.maximum(step_of(*g) - n_ctx_tiles, 0), 0))], x)


def _token_rows(x_refs, i, n_ctx_tiles):
    if len(x_refs) == 1:
        return x_refs[0][...]
    return jnp.where(i < n_ctx_tiles, x_refs[0][...], x_refs[1][...])


def _inproj_kernel(*refs, n_x, n_ctx_tiles, f32_cols):
    x_refs = refs[:n_x]
    g_ref, mod_ref, w_ref = refs[n_x:n_x + 3]
    out_refs = refs[n_x + 3:]
    x = _token_rows(x_refs, pl.program_id(1), n_ctx_tiles)
    h = _rms(x, g_ref[...]) * (1.0 + mod_ref[0, 1:2, :]) + mod_ref[0, 0:1, :]
    acc = jnp.dot(h.astype(BF16), w_ref[...], preferred_element_type=F32)
    if not f32_cols:
        out_refs[0][...] = acc.astype(BF16)
        return
    n = acc.shape[1]
    at16 = at32 = start = 0
    for lo, hi in tuple(f32_cols) + ((n, n),):
        if lo > start:
            out_refs[0][:, at16:at16 + lo - start] = acc[:, start:lo].astype(BF16)
            at16 += lo - start
        if hi > lo:
            out_refs[1][:, at32:at32 + hi - lo] = acc[:, lo:hi]
            at32 += hi - lo
        start = hi


def _in_proj(x, g, mod, w_bf, t, n_ctx_tiles, tiles_per_req, tn, f32_cols=()):
    d, n = w_bf.shape
    tm = INPROJ_TILE
    scale = INPROJ_TILE // TOKEN_TILE
    assert n_ctx_tiles % scale == 0 and tiles_per_req % scale == 0
    n_ctx_tiles, tiles_per_req = n_ctx_tiles // scale, tiles_per_req // scale
    mrow = functools.partial(_tile_mod_row, n_ctx_tiles=n_ctx_tiles, tiles_per_req=tiles_per_req)
    x_specs, x_args = _token_specs(x, tm, n_ctx_tiles, lambda j, i: i)
    if not f32_cols:
        out_specs = pl.BlockSpec((tm, tn), lambda j, i: (i, j))
        out_shape = jax.ShapeDtypeStruct((t, n), BF16)
    else:
        assert tn == n
        n32 = sum(hi - lo for lo, hi in f32_cols)
        out_specs = [pl.BlockSpec((tm, n - n32), lambda j, i: (i, 0)),
                     pl.BlockSpec((tm, n32), lambda j, i: (i, 0))]
        out_shape = [jax.ShapeDtypeStruct((t, n - n32), BF16), jax.ShapeDtypeStruct((t, n32), F32)]
    return pl.pallas_call(
        functools.partial(_inproj_kernel, n_x=len(x_args), n_ctx_tiles=n_ctx_tiles, f32_cols=tuple(f32_cols)),
        grid=(n // tn, t // tm),
        in_specs=x_specs + [pl.BlockSpec((1, d), lambda j, i: (0, 0)),
                            pl.BlockSpec((1, 6, d), lambda j, i: (mrow(i), 0, 0)),
                            pl.BlockSpec((d, tn), lambda j, i: (0, j))],
        out_specs=out_specs,
        out_shape=out_shape,
        compiler_params=_cparams("arbitrary", "arbitrary"),
        name="in_proj",
    )(*x_args, g.reshape(1, d), mod, w_bf)


def _hgrn_decay(z, lb, reverse):
    nb = HGRN_BLOCK
    row = lax.broadcasted_iota(jnp.int32, (nb, nb), 0)
    col = lax.broadcasted_iota(jnp.int32, (nb, nb), 1)
    tri = jnp.where((col >= row) if reverse else (col <= row), 1.0, 0.0).astype(BF16)
    logf = jnp.log(lb + (1.0 - lb) * jax.nn.sigmoid(z))
    k = (1.0 - lb) * jax.nn.sigmoid(-z)
    l2 = logf * LOG2_E
    hi = l2.astype(BF16)
    lo = (l2 - hi.astype(F32)).astype(BF16)
    c = jnp.dot(tri, hi, preferred_element_type=F32) + jnp.dot(tri, lo, preferred_element_type=F32)
    return c, k


def _hgrn_block(q, v, st_t, reverse, c_ref, k_ref, ecat):
    nb = HGRN_BLOCK
    row = lax.broadcasted_iota(jnp.int32, (nb, nb), 0)
    col = lax.broadcasted_iota(jnp.int32, (nb, nb), 1)
    c = c_ref[...]
    k = k_ref[...]
    edge = 0 if reverse else nb - 1
    c_edge = c_ref[edge:edge + 1, :]

    o = _bdot_nt(q * jnp.exp2(c), st_t)
    kd = k * jnp.exp2(c_edge - c)
    st_new = st_t * jnp.exp2(c_edge) + _bdot(v.T, kd)

    a = jnp.zeros((nb, nb), F32)
    m = nb // 2
    while m >= HGRN_DIAG:
        pieces = []
        for grp in range(nb // (2 * m)):
            idx = grp * 2 * m + (m if reverse else m - 1)
            pieces.append(jnp.broadcast_to(c_ref[idx:idx + 1, :], (2 * m, nb)))
        bnd = pieces[0] if len(pieces) == 1 else jnp.concatenate(pieces, axis=0)
        e = jnp.exp2(-jnp.abs(c - bnd))
        am = _bdot_nt(q * e, k * e)
        same = (row & ~(2 * m - 1)) == (col & ~(2 * m - 1))
        r_hi = (row & m) != 0
        c_hi = (col & m) != 0
        if reverse:
            mask = same & jnp.logical_not(r_hi) & c_hi
        else:
            mask = same & r_hi & jnp.logical_not(c_hi)
        a = jnp.where(mask, am, a)
        m //= 2

    dg = HGRN_DIAG
    rowi = lax.broadcasted_iota(jnp.int32, (dg, nb), 0)
    cols = []
    for s in range(dg):
        keep = (rowi <= s) if reverse else (rowi >= s)
        tiles = []
        for j in range(nb // dg):
            r = j * dg + s
            qj = q[j * dg:(j + 1) * dg, :]
            cj = c[j * dg:(j + 1) * dg, :]
            cs = jnp.broadcast_to(c_ref[r:r + 1, :], (dg, nb))
            ks = jnp.broadcast_to(k_ref[r:r + 1, :], (dg, nb))
            tiles.append(qj * ks * jnp.exp2(jnp.where(keep, cj - cs, -jnp.inf)))
        cols.append(jnp.concatenate(tiles, axis=0).astype(BF16))
    p = jnp.concatenate(cols, axis=1)
    red = jnp.dot(p, ecat, preferred_element_type=F32)
    a = jnp.where((row >> 3) == (col >> 3), red, a)
    o = o + _bdot(a, v)
    return o, st_new


def _hgrn_kernel(*refs, nblk, has_state, qscale):
    qa_ref, ia_ref, zf_ref, zb_ref, ga_ref, lb_ref, ng_ref, ecat_ref = refs[:8]
    s0_ref = refs[8] if has_state else None
    o_ref, sfin_ref, oacc_ref, c_scr, k_scr = refs[-5:]
    nb = HGRN_BLOCK

    def gates(direction, blk, slot):
        z_ref = zb_ref if direction == 1 else zf_ref
        r0 = blk * nb if isinstance(blk, int) else pl.multiple_of(blk * nb, nb)
        c, k = _hgrn_decay(z_ref[pl.ds(r0, nb), :], lb_ref[direction:direction + 1, :], direction == 1)
        c_scr[direction, slot] = c
        k_scr[direction, slot] = k

    def block(direction, blk, slot, st_t):
        r0 = blk * nb if isinstance(blk, int) else pl.multiple_of(blk * nb, nb)
        q = _silu(qa_ref[pl.ds(r0, nb), :].astype(F32)) * qscale
        v = ia_ref[pl.ds(r0, nb), :].astype(F32)
        o, st_new = _hgrn_block(q, v, st_t, direction == 1, c_scr.at[direction, slot],
                                k_scr.at[direction, slot], ecat_ref[...])
        oacc_ref[direction, pl.ds(r0, nb), :] = o
        return st_new

    gates(0, 0, 0)
    gates(1, nblk - 1, 0)

    def body(i, carry):
        st_f, st_b = carry
        slot = lax.rem(i, 2)
        st_f = block(0, i, slot, st_f)
        st_b = block(1, nblk - 1 - i, slot, st_b)
        gates(0, i + 1, 1 - slot)
        gates(1, nblk - 2 - i, 1 - slot)
        return st_f, st_b

    if has_state:
        st0 = (s0_ref[0].T, s0_ref[1].T)
    else:
        zero = jnp.zeros((ia_ref.shape[1], qa_ref.shape[1]), F32)
        st0 = (zero, zero)
    st_f, st_b = lax.fori_loop(0, nblk - 1, body, st0)
    last_slot = (nblk - 1) % 2
    st_f = block(0, nblk - 1, last_slot, st_f)
    st_b = block(1, 0, last_slot, st_b)
    sfin_ref[0] = st_f.T
    sfin_ref[1] = st_b.T

    o = oacc_ref[0] + oacc_ref[1]
    o_ref[...] = (_rms(o, ng_ref[0]) * _silu(ga_ref[...].astype(F32))).astype(o_ref.dtype)


def _hgrn(proj16, proj32, lb, norm_g, s0, layer_idx, out_prev, n_seq, seq, row_blk0, heads, key_d, val_d):
    t = proj16.shape[0]
    has_state = s0 is not None
    hk = heads * key_d // 128
    off_qa, off_ia, off_ga = 0, hk, 2 * hk
    off_zf, off_zb = 0, hk

    def col(off):
        return pl.BlockSpec((seq, 128), lambda b, h: (row_blk0 + b, off + h))

    lane = np.arange(key_d)[None, :]
    srow = np.repeat(np.arange(HGRN_DIAG), key_d)[:, None]
    ecat = jnp.asarray((lane % HGRN_DIAG == srow).astype(np.float32), dtype=BF16)
    in_specs = [col(off_qa), col(off_ia), col(off_zf), col(off_zb), col(off_ga),
                pl.BlockSpec((2, key_d), lambda b, h: (0, h)),
                pl.BlockSpec((1, 1, val_d), lambda b, h: (h, 0, 0)),
                pl.BlockSpec((HGRN_DIAG * key_d, key_d), lambda b, h: (0, 0))]
    args = [proj16, proj16, proj32, proj32, proj16, lb, norm_g.reshape(heads, 1, val_d), ecat]
    if has_state:
        in_specs.append(pl.BlockSpec((None, None, 2, None, key_d, val_d),
                                     lambda b, h: (b, layer_idx, 0, h, 0, 0)))
        args.append(s0)
    aliases = {}
    if out_prev is not None:
        in_specs.append(pl.BlockSpec(memory_space=pl.ANY))
        args.append(out_prev)
        aliases = {len(args) - 1: 0}

    kern = functools.partial(_hgrn_kernel, nblk=seq // HGRN_BLOCK, has_state=has_state,
                             qscale=float(key_d) ** -0.5)
    return pl.pallas_call(
        kern,
        grid=(n_seq, heads),
        in_specs=in_specs,
        out_specs=[pl.BlockSpec((seq, val_d), lambda b, h: (row_blk0 + b, h)),
                   pl.BlockSpec((None, 2, None, key_d, val_d), lambda b, h: (b, 0, h, 0, 0))],
        out_shape=[jax.ShapeDtypeStruct((t, heads * val_d), BF16),
                   jax.ShapeDtypeStruct((n_seq, 2, heads, key_d, val_d), F32)],
        scratch_shapes=[pltpu.VMEM((2, seq, val_d), F32),
                        pltpu.VMEM((2, 2, HGRN_BLOCK, key_d), F32),
                        pltpu.VMEM((2, 2, HGRN_BLOCK, key_d), F32)],
        input_output_aliases=aliases,
        compiler_params=_cparams("arbitrary", "arbitrary"),
        name="hgrn2",
    )(*args)


def _rope(x, cos, sin_signed):
    return x * cos + pltpu.roll(x, x.shape[-1] // 2, 1) * sin_signed


def _attn_kernel(*refs, has_ctx, grp, qblk, scale):
    if has_ctx:
        (q_ref, k_ref, v_ref, qn_ref, kn_ref, ck_ref, cv_ref, cos_ref, sin_ref, _prev, o_ref) = refs
    else:
        (q_ref, k_ref, v_ref, qn_ref, kn_ref, o_ref, ko_ref, vo_ref) = refs
    seq, hd = k_ref.shape
    k = _rms(k_ref[...], kn_ref[...])
    v = v_ref[...]
    if has_ctx:
        k = _rope(k, cos_ref[...], sin_ref[...])
        ck = ck_ref[...].astype(BF16)
        cv = cv_ref[...].astype(BF16)
    else:
        ko_ref[...] = k
        vo_ref[...] = v
    kb = k.astype(BF16)
    vb = v.astype(BF16)

    def body(i, carry):
        r0 = pl.multiple_of(i * qblk, qblk)
        for g in range(grp):
            q = _rms(q_ref[pl.ds(r0, qblk), g * hd:(g + 1) * hd].astype(F32), qn_ref[...])
            if has_ctx:
                q = _rope(q, cos_ref[pl.ds(r0, qblk), :], sin_ref[pl.ds(r0, qblk), :])
            qb = q.astype(BF16)
            s2 = lax.dot_general(qb, kb, _NT, preferred_element_type=F32) * scale
            m = jnp.max(s2, axis=-1, keepdims=True)
            if has_ctx:
                s1 = lax.dot_general(qb, ck, _NT, preferred_element_type=F32) * scale
                m = jnp.maximum(m, jnp.max(s1, axis=-1, keepdims=True))
                p1 = jnp.exp(s1 - m)
            p2 = jnp.exp(s2 - m)
            den = jnp.sum(p2, axis=-1, keepdims=True)
            acc = jnp.dot(p2.astype(BF16), vb, preferred_element_type=F32)
            if has_ctx:
                den = den + jnp.sum(p1, axis=-1, keepdims=True)
                acc = acc + jnp.dot(p1.astype(BF16), cv, preferred_element_type=F32)
            o_ref[pl.ds(r0, qblk), g * hd:(g + 1) * hd] = (acc / den).astype(o_ref.dtype)
        return carry

    lax.fori_loop(0, seq // qblk, body, 0)


def _attn(proj16, proj32, qn_g, kn_g, ctx, layer_idx, out_prev, n_seq, seq, row_blk0, heads, kv_heads, hd,
          col_q, col_kv):
    t = proj16.shape[0]
    grp = heads // kv_heads
    has_ctx = ctx is not None
    cq = col_q // grp
    ck0 = col_kv
    cv0 = ck0 + kv_heads
    in_specs = [pl.BlockSpec((seq, grp * hd), lambda b, h: (row_blk0 + b, cq + h)),
                pl.BlockSpec((seq, hd), lambda b, h: (row_blk0 + b, ck0 + h)),
                pl.BlockSpec((seq, hd), lambda b, h: (row_blk0 + b, cv0 + h)),
                pl.BlockSpec((1, hd), lambda b, h: (0, 0)),
                pl.BlockSpec((1, hd), lambda b, h: (0, 0))]
    args = [proj16, proj32, proj32, qn_g.reshape(1, hd), kn_g.reshape(1, hd)]
    out_specs = [pl.BlockSpec((seq, grp * hd), lambda b, h: (row_blk0 + b, h))]
    out_shape = [jax.ShapeDtypeStruct((t, heads * hd), BF16)]
    aliases = {}
    if has_ctx:
        cache_k, cache_v, cos, sin_signed = ctx
        nreq, nlay, past = cache_k.shape[:3]
        cache_k = cache_k.reshape(nreq, nlay, past, kv_heads * hd)
        cache_v = cache_v.reshape(nreq, nlay, past, kv_heads * hd)
        spec_c = pl.BlockSpec((None, None, past, hd), lambda b, h: (b, layer_idx, 0, h))
        in_specs += [spec_c, spec_c,
                     pl.BlockSpec((seq, hd), lambda b, h: (0, 0)),
                     pl.BlockSpec((seq, hd), lambda b, h: (0, 0)),
                     pl.BlockSpec(memory_space=pl.ANY)]
        args += [cache_k, cache_v, cos, sin_signed, out_prev]
        aliases = {len(args) - 1: 0}
    else:
        out_specs += [pl.BlockSpec((seq, hd), lambda b, h: (b, h)),
                      pl.BlockSpec((seq, hd), lambda b, h: (b, h))]
        out_shape += [jax.ShapeDtypeStruct((n_seq * seq, kv_heads * hd), F32)] * 2
    return pl.pallas_call(
        functools.partial(_attn_kernel, has_ctx=has_ctx, grp=grp, qblk=min(seq, 256), scale=float(hd) ** -0.5),
        grid=(n_seq, kv_heads),
        in_specs=in_specs,
        out_specs=out_specs,
        out_shape=out_shape,
        input_output_aliases=aliases,
        compiler_params=_cparams("arbitrary", "arbitrary"),
        name="gqa",
    )(*args)


def _ret_kernel(*refs, nchunk, has_ctx, kscale):
    if has_ctx:
        (lg_ref, q_ref, k_ref, v_ref, g_ref, gn_ref, s0_ref, cos_ref, sin_ref, _prev,
         o_ref, sfin_ref, sb_ref) = refs
    else:
        (lg_ref, q_ref, k_ref, v_ref, g_ref, gn_ref, o_ref, sfin_ref, sb_ref) = refs
    cs = RET_CHUNK
    kd_dim = q_ref.shape[1]
    h = pl.program_id(1)
    lgf = lg_ref[0, h]
    lgb = lg_ref[1, h]
    row = lax.broadcasted_iota(jnp.int32, (cs, cs), 0)
    col = lax.broadcasted_iota(jnp.int32, (cs, cs), 1)
    rel = (row - col).astype(F32)
    dmat = (jnp.exp(jnp.where(rel >= 0, rel * lgf, -jnp.inf))
            + jnp.exp(jnp.where(rel <= 0, -rel * lgb, -jnp.inf)))
    j = lax.broadcasted_iota(jnp.int32, (cs, 1), 0).astype(F32)
    one = jnp.ones((1, 1), F32)
    dec_f = jnp.exp(one * (cs * lgf))
    dec_b = jnp.exp(one * (cs * lgb))

    def load_qk(ref, r0, mul):
        x = ref[pl.ds(r0, cs), :].astype(F32) * mul
        if has_ctx:
            half = kd_dim // 2
            x1, x2 = x[:, :half], x[:, half:]
            cos = cos_ref[pl.ds(r0, cs), :]
            sin = sin_ref[pl.ds(r0, cs), :]
            x = jnp.concatenate([x1 * cos - x2 * sin, x2 * cos + x1 * sin], axis=1)
        return x

    def bwd_body(i, st):
        n = nchunk - 1 - i
        r0 = pl.multiple_of(n * cs, cs)
        sb_ref[n] = st
        k = load_qk(k_ref, r0, kscale)
        v = v_ref[pl.ds(r0, cs), :]
        return st * dec_b + _bdot((k * jnp.exp(j * lgb)).T, v)

    st_b0 = s0_ref[1] if has_ctx else jnp.zeros(sfin_ref.shape[1:], F32)
    st_b = lax.fori_loop(0, nchunk, bwd_body, st_b0)
    sfin_ref[1] = st_b

    def fwd_body(n, st):
        r0 = pl.multiple_of(n * cs, cs)
        q = load_qk(q_ref, r0, 1.0)
        k = load_qk(k_ref, r0, kscale)
        v = v_ref[pl.ds(r0, cs), :]
        a = _bdot_nt(q, k) * dmat
        o = _bdot(a, v)
        if has_ctx or nchunk > 1:
            o = o + _bdot(q * jnp.exp((j + 1.0) * lgf), st)
            o = o + _bdot(q * jnp.exp((cs - j) * lgb), sb_ref[n])
        mu = jnp.mean(o, axis=-1, keepdims=True)
        var = jnp.mean(jnp.square(o - mu), axis=-1, keepdims=True)
        y = (o - mu) * lax.rsqrt(var + EPS) * gn_ref[0]
        o_ref[pl.ds(r0, cs), :] = (_silu(g_ref[pl.ds(r0, cs), :].astype(F32)) * y).astype(o_ref.dtype)
        return st * dec_f + _bdot((k * jnp.exp((cs - 1.0 - j) * lgf)).T, v)

    st_f0 = s0_ref[0] if has_ctx else jnp.zeros(sfin_ref.shape[1:], F32)
    sfin_ref[0] = lax.fori_loop(0, nchunk, fwd_body, st_f0)


def _retention(proj, log_gamma, gn_g, ctx, layer_idx, out_prev, n_seq, seq, row_blk0, heads, key_d, val_d):
    t = proj.shape[0]
    has_ctx = ctx is not None
    nq = heads * key_d
    cq, ck, cv, cg = 0, nq // key_d, 2 * nq // val_d, (2 * nq + heads * val_d) // val_d
    in_specs = [pl.BlockSpec(memory_space=pltpu.SMEM),
                pl.BlockSpec((seq, key_d), lambda b, h: (row_blk0 + b, cq + h)),
                pl.BlockSpec((seq, key_d), lambda b, h: (row_blk0 + b, ck + h)),
                pl.BlockSpec((seq, val_d), lambda b, h: (row_blk0 + b, cv + h)),
                pl.BlockSpec((seq, val_d), lambda b, h: (row_blk0 + b, cg + h)),
                pl.BlockSpec((1, 1, val_d), lambda b, h: (h, 0, 0))]
    args = [log_gamma, proj, proj, proj, proj, gn_g.reshape(heads, 1, val_d)]
    aliases = {}
    if has_ctx:
        s0, cos, sin = ctx
        in_specs += [pl.BlockSpec((None, None, 2, None, key_d, val_d), lambda b, h: (b, layer_idx, 0, h, 0, 0)),
                     pl.BlockSpec((seq, key_d // 2), lambda b, h: (0, 0)),
                     pl.BlockSpec((seq, key_d // 2), lambda b, h: (0, 0)),
                     pl.BlockSpec(memory_space=pl.ANY)]
        args += [s0, cos, sin, out_prev]
        aliases = {len(args) - 1: 0}
    nchunk = seq // RET_CHUNK
    return pl.pallas_call(
        functools.partial(_ret_kernel, nchunk=nchunk, has_ctx=has_ctx, kscale=float(key_d) ** -0.5),
        grid=(n_seq, heads),
        in_specs=in_specs,
        out_specs=[pl.BlockSpec((seq, val_d), lambda b, h: (row_blk0 + b, h)),
                   pl.BlockSpec((None, 2, None, key_d, val_d), lambda b, h: (b, 0, h, 0, 0))],
        out_shape=[jax.ShapeDtypeStruct((t, heads * val_d), BF16),
                   jax.ShapeDtypeStruct((n_seq, 2, heads, key_d, val_d), F32)],
        scratch_shapes=[pltpu.VMEM((nchunk, key_d, val_d), F32)],
        input_output_aliases=aliases,
        compiler_params=_cparams("arbitrary", "arbitrary"),
        name="retention",
    )(*args)


def _outproj_kernel(*refs, n_in, n_x, n_ctx_tiles, n_groups, n_experts):
    mix_refs = refs[:n_in]
    w_refs = refs[n_in:2 * n_in]
    x_refs = refs[2 * n_in:2 * n_in + n_x]
    (mod_ref, g2_ref, wr_ref, br_ref, xn_ref, h2_ref, meta_ref, ew_ref, cnt_out_ref,
     cnt_ref) = refs[2 * n_in + n_x:]
    acc = _bdot(mix_refs[0][...], w_refs[0][...])
    for mref, wref in zip(mix_refs[1:], w_refs[1:]):
        acc = acc + _bdot(mref[...], wref[...])
    xn = _token_rows(x_refs, pl.program_id(0), n_ctx_tiles) + mod_ref[0, 2:3, :] * acc
    xn_ref[...] = xn
    h2 = _rms(xn, g2_ref[...]) * (1.0 + mod_ref[0, 4:5, :]) + mod_ref[0, 3:4, :]
    h2_ref[...] = h2.astype(BF16).reshape(h2_ref.shape)
    h_hi = h2.astype(BF16)
    h_lo = (h2 - h_hi.astype(F32)).astype(BF16)
    nl = br_ref.shape[1]
    hh = jnp.dot(h_hi, wr_ref[...], preferred_element_type=F32)
    lg = hh[:, :nl] + hh[:, nl:] + jnp.dot(h_lo, wr_ref[:, :nl], preferred_element_type=F32) + br_ref[...]

    per_grp = n_experts // n_groups
    lane = lax.broadcasted_iota(jnp.int32, lg.shape, 1).astype(F32)
    big = float(lg.shape[1])
    ninf = -jnp.inf
    gl = jnp.where(lane < n_groups, lg, ninf)
    gmax = jnp.max(gl, axis=-1, keepdims=True)
    p_grp = 1.0 / jnp.sum(jnp.exp(gl - gmax), axis=-1, keepdims=True)
    gidx = jnp.min(jnp.where(gl == gmax, lane, big), axis=-1, keepdims=True)
    in_grp = ((lane >= n_groups) & (lane < n_groups + n_experts)
              & (jnp.floor((lane - n_groups) / per_grp) == gidx))
    el = jnp.where(in_grp, lg, ninf)
    emax = jnp.max(el, axis=-1, keepdims=True)
    esum = jnp.sum(jnp.exp(el - emax), axis=-1, keepdims=True)
    i1 = jnp.min(jnp.where(el == emax, lane, big), axis=-1, keepdims=True)
    el2 = jnp.where(lane == i1, ninf, el)
    emax2 = jnp.max(el2, axis=-1, keepdims=True)
    i2 = jnp.min(jnp.where(el2 == emax2, lane, big), axis=-1, keepdims=True)
    p1 = 1.0 / esum
    p2 = jnp.exp(emax2 - emax) / esum
    w1 = p1 / (p1 + p2) * p_grp
    w2 = p2 / (p1 + p2) * p_grp
    ew_ref[...] = jnp.where(lane == 0.0, w1, jnp.where(lane == 1.0, w2, 0.0))

    step = pl.program_id(0)

    @pl.when(step == 0)
    def _():
        cnt_ref[...] = jnp.zeros_like(cnt_ref)

    tm = lg.shape[0]
    hit1 = lane == i1
    hit2 = lane == i2
    sel = jnp.where(hit1 | hit2, 1.0, 0.0)
    rr = lax.broadcasted_iota(jnp.int32, (tm, tm), 0)
    cc = lax.broadcasted_iota(jnp.int32, (tm, tm), 1)
    before = jnp.where(cc < rr, 1.0, 0.0).astype(BF16)
    prior = jnp.dot(before, sel.astype(BF16), preferred_element_type=F32) + cnt_ref[...]
    rank1 = jnp.sum(jnp.where(hit1, prior, 0.0), axis=-1, keepdims=True)
    rank2 = jnp.sum(jnp.where(hit2, prior, 0.0), axis=-1, keepdims=True)
    cnt_ref[...] += jnp.sum(sel, axis=0, keepdims=True)
    cnt_out_ref[...] = cnt_ref[...]
    meta = jnp.where(lane == 0.0, i1 - n_groups,
                     jnp.where(lane == 1.0, i2 - n_groups,
                               jnp.where(lane == 2.0, rank1, jnp.where(lane == 3.0, rank2, 0.0))))
    meta_ref[0] = meta.T[0:8, :].astype(jnp.int32)


def _out_proj(mixes, w_parts, x, mod, g2, w_router, b_router, n_ctx_tiles, tiles_per_req, n_groups, n_experts):
    t = mixes[0].shape[0]
    d = w_parts[0].shape[1]
    tm = TOKEN_TILE
    n_in = len(mixes)
    mrow = functools.partial(_tile_mod_row, n_ctx_tiles=n_ctx_tiles, tiles_per_req=tiles_per_req)
    x_specs, x_args = _token_specs(x, tm, n_ctx_tiles, lambda i: i)
    in_specs = ([pl.BlockSpec((tm, m.shape[1]), lambda i: (i, 0)) for m in mixes]
                + [pl.BlockSpec(w.shape, lambda i: (0, 0)) for w in w_parts]
                + x_specs
                + [pl.BlockSpec((1, 6, d), lambda i: (mrow(i), 0, 0)),
                   pl.BlockSpec((1, d), lambda i: (0, 0)),
                   pl.BlockSpec(w_router.shape, lambda i: (0, 0)),
                   pl.BlockSpec((1, 128), lambda i: (0, 0))])
    return pl.pallas_call(
        functools.partial(_outproj_kernel, n_in=n_in, n_x=len(x_args), n_ctx_tiles=n_ctx_tiles,
                          n_groups=n_groups, n_experts=n_experts),
        grid=(t // tm,),
        in_specs=in_specs,
        out_specs=[pl.BlockSpec((tm, d), lambda i: (i, 0)),
                   pl.BlockSpec((tm, d // 128, 128), lambda i: (i, 0, 0)),
                   pl.BlockSpec((1, 8, tm), lambda i: (i, 0, 0)),
                   pl.BlockSpec((tm, 128), lambda i: (i, 0)),
                   pl.BlockSpec((1, 128), lambda i: (0, 0))],
        out_shape=[jax.ShapeDtypeStruct((t, d), F32),
                   jax.ShapeDtypeStruct((t, d // 128, 128), BF16),
                   jax.ShapeDtypeStruct((t // tm, 8, tm), jnp.int32),
                   jax.ShapeDtypeStruct((t, 128), F32),
                   jax.ShapeDtypeStruct((1, 128), F32)],
        scratch_shapes=[pltpu.VMEM((1, 128), F32)],
        compiler_params=_cparams("arbitrary"),
        name="out_proj_router",
    )(*mixes, *w_parts, *x_args, mod, g2.reshape(1, d), w_router, b_router)


def _dispatch_plan(meta, counts, n_experts, tile):
    t = meta.shape[0] * meta.shape[2]
    e1, e2, rank1, rank2 = (meta[:, k, :].reshape(t) for k in range(4))
    padded = ((counts + tile - 1) // tile) * tile
    ends = jnp.cumsum(padded)
    starts = ends - padded
    experts = jnp.arange(n_experts, dtype=jnp.int32)[None, :]
    pos = [jnp.sum(jnp.where(e[:, None] == experts, starts[None, :], 0), axis=1) + r
           for e, r in ((e1, rank1), (e2, rank2))]
    n_tiles = (2 * t) // tile + n_experts
    n_used = (ends[-1] // tile).astype(jnp.int32)
    tile_start = jnp.arange(n_tiles, dtype=jnp.int32) * tile
    probe = jnp.minimum(tile_start, ends[-1] - 1)
    tile_expert = jnp.sum((ends[None, :] <= probe[:, None]).astype(jnp.int32), axis=1)
    return tile_expert, n_used.reshape(1), ends.astype(jnp.int32), pos[0], pos[1], n_tiles


def _row_source_kernel(ends_ref, p1_ref, p2_ref, o_ref, *, n_tok, n_rows, tile, n_experts):
    def clear8(j, c):
        base = lax.rem(j * 8, n_tok)
        for u in range(8):
            o_ref[j * 8 + u] = base + u
        return c

    def per_expert(e, c):
        first = lax.shift_right_logical(jnp.maximum(ends_ref[e] - tile, 0), 3)
        lax.fori_loop(first, first + tile // 8, clear8, 0)
        return c

    g = pl.program_id(0)

    @pl.when(g == 0)
    def _():
        lax.fori_loop(0, n_experts, per_expert, 0)
        lax.fori_loop(lax.shift_right_logical(ends_ref[n_experts - 1], 3), n_rows // 8, clear8, 0)

    chunk = p1_ref.shape[2]

    def place(r, c):
        t = g * chunk + r
        o_ref[p1_ref[0, 0, r]] = t
        o_ref[p2_ref[0, 0, r]] = t
        return c

    lax.fori_loop(0, chunk, place, 0, unroll=8)


def _row_source(ends, pos1, pos2, n_rows, tile):
    n_tok = pos1.shape[0]
    chunk = ROW_SOURCE_CHUNK
    smem = pl.BlockSpec(memory_space=pltpu.SMEM)
    pos_spec = pl.BlockSpec((1, 1, chunk), lambda g: (g, 0, 0), memory_space=pltpu.SMEM)
    return pl.pallas_call(
        functools.partial(_row_source_kernel, n_tok=n_tok, n_rows=n_rows, tile=tile, n_experts=ends.shape[0]),
        grid=(n_tok // chunk,),
        in_specs=[smem, pos_spec, pos_spec],
        out_specs=smem,
        out_shape=jax.ShapeDtypeStruct((n_rows,), jnp.int32),
        compiler_params=_cparams("arbitrary"),
        name="moe_row_source",
    )(ends, pos1.reshape(n_tok // chunk, 1, chunk), pos2.reshape(n_tok // chunk, 1, chunk))


def _moe_kernel(te_ref, nu_ref, *refs, tm, ahead):
    tok_refs = refs[:ahead + 1]
    h_hbm, wg_ref, wu_ref, wd_ref, o_ref, wg_bf, wu_bf, wd_bf, buf, sem = refs[ahead + 1:]
    i = pl.program_id(0)
    nslot = ahead + 1
    slot = lax.rem(i, nslot)
    n_used = nu_ref[0]

    def issue(tile, tok_ref):
        s = lax.rem(tile, nslot)

        def body(j, c):
            for u in range(2):
                r = 2 * j + u
                pltpu.make_async_copy(h_hbm.at[tok_ref[0, 0, r]], buf.at[s, r], sem.at[s]).start(priority=u)
            return c

        lax.fori_loop(0, tm // 2, body, 0, unroll=4)

    @pl.when(i == 0)
    def _():
        for j in range(ahead):
            @pl.when(j < n_used)
            def _():
                issue(jnp.int32(j), tok_refs[j])

    new_expert = jnp.logical_or(i == 0, te_ref[i] != te_ref[jnp.maximum(i - 1, 0)])

    @pl.when(new_expert)
    def _():
        wg_bf[...] = wg_ref[...].astype(BF16)
        wu_bf[...] = wu_ref[...].astype(BF16)
        wd_bf[...] = wd_ref[...].astype(BF16)

    def compute(issue_next):
        s_next = lax.rem(i + ahead, nslot)
        quarter = tm // 4

        def start_rows(part):
            if issue_next:
                for r in range(part * quarter, (part + 1) * quarter):
                    pltpu.make_async_copy(h_hbm.at[tok_refs[ahead][0, 0, r]], buf.at[s_next, r],
                                          sem.at[s_next]).start(priority=r % 2)

        pltpu.make_async_copy(h_hbm.at[pl.ds(0, tm)], buf.at[slot], sem.at[slot]).wait()
        start_rows(0)
        x = buf[slot].reshape(tm, -1)
        g = jnp.dot(x, wg_bf[...], preferred_element_type=F32)
        start_rows(1)
        u = jnp.dot(x, wu_bf[...], preferred_element_type=F32)
        start_rows(2)
        act = _silu(g) * u
        y = jnp.dot(act.astype(BF16), wd_bf[...], preferred_element_type=F32)
        start_rows(3)
        o_ref[...] = y.astype(BF16).reshape(o_ref.shape)

    @pl.when(i + ahead < n_used)
    def _():
        compute(True)

    @pl.when(jnp.logical_and(i + ahead >= n_used, i < n_used))
    def _():
        compute(False)

    @pl.when(i >= n_used)
    def _():
        o_ref[...] = jnp.zeros_like(o_ref)


def _moe_experts(h3, src_tok, tile_expert, n_used, w_gate, w_up, w_down, layer_idx, tm):
    t, sub, lanes = h3.shape
    d = sub * lanes
    n_tiles = src_tok.shape[0] // tm
    f = w_gate.shape[-1]
    ahead = MOE_GATHER_AHEAD
    tok = src_tok.reshape(n_tiles, 1, tm)

    def tok_spec(k):
        return pl.BlockSpec((1, 1, tm), lambda i, te, nu: (jnp.minimum(i + k, n_tiles - 1), 0, 0),
                            memory_space=pltpu.SMEM)

    return pl.pallas_call(
        functools.partial(_moe_kernel, tm=tm, ahead=ahead),
        grid_spec=pltpu.PrefetchScalarGridSpec(
            num_scalar_prefetch=2,
            grid=(n_tiles,),
            in_specs=[tok_spec(k) for k in range(ahead + 1)] + [
                      pl.BlockSpec(memory_space=pl.ANY),
                      pl.BlockSpec((None, None, d, f), lambda i, te, nu: (layer_idx, te[i], 0, 0)),
                      pl.BlockSpec((None, None, d, f), lambda i, te, nu: (layer_idx, te[i], 0, 0)),
                      pl.BlockSpec((None, None, f, d), lambda i, te, nu: (layer_idx, te[i], 0, 0))],
            out_specs=pl.BlockSpec((tm, sub, lanes), lambda i, te, nu: (i, 0, 0)),
            scratch_shapes=[pltpu.VMEM((d, f), BF16), pltpu.VMEM((d, f), BF16), pltpu.VMEM((f, d), BF16),
                            pltpu.VMEM((ahead + 1, tm, sub, lanes), BF16),
                            pltpu.SemaphoreType.DMA((ahead + 1,))]),
        out_shape=jax.ShapeDtypeStruct((n_tiles * tm, sub, lanes), BF16),
        compiler_params=_cparams("arbitrary"),
        name="moe_experts",
    )(tile_expert, n_used, *([tok] * (ahead + 1)), h3, w_gate, w_up, w_down)


def _combine_kernel(p1_ref, p2_ref, p1n_ref, p2n_ref, y_hbm, x_ref, ew_ref, mod_ref, gf_ref, *rest,
                    tm, n_steps, n_ctx_tiles, final_norm):
    out_refs, (buf, sem) = rest[:-2], rest[-2:]
    i = pl.program_id(0)
    slot = lax.rem(i, 2)

    def issue(s, pa_ref, pb_ref):
        def body(r, c):
            pltpu.make_async_copy(y_hbm.at[pa_ref[0, 0, r]], buf.at[s, 0, r], sem.at[s, 0]).start(priority=0)
            pltpu.make_async_copy(y_hbm.at[pb_ref[0, 0, r]], buf.at[s, 1, r], sem.at[s, 1]).start(priority=1)
            return c

        lax.fori_loop(0, tm, body, 0, unroll=8)

    @pl.when(i == 0)
    def _():
        issue(0, p1_ref, p2_ref)

    @pl.when(i + 1 < n_steps)
    def _():
        issue(1 - slot, p1n_ref, p2n_ref)

    for k in range(2):
        pltpu.make_async_copy(y_hbm.at[pl.ds(0, tm)], buf.at[slot, k], sem.at[slot, k]).wait()
    ew = ew_ref[...]
    y = (ew[:, 0:1] * buf[slot, 0].reshape(tm, -1).astype(F32)
         + ew[:, 1:2] * buf[slot, 1].reshape(tm, -1).astype(F32))
    x = x_ref[...] + mod_ref[0, 5:6, :] * y
    if not final_norm:
        out_refs[0][...] = x
    else:
        x = _rms(x, gf_ref[...])
        ctx_ref, lat_ref = out_refs

        @pl.when(i < n_ctx_tiles)
        def _():
            ctx_ref[...] = x

        @pl.when(i >= n_ctx_tiles)
        def _():
            lat_ref[...] = x


def _moe_combine(y_sorted, pos1, pos2, x, ew, mod, final_g, n_ctx_tiles, tiles_per_req, final_norm):
    t, d = x.shape
    tm = TOKEN_TILE
    n_steps = t // tm
    mrow = functools.partial(_tile_mod_row, n_ctx_tiles=n_ctx_tiles, tiles_per_req=tiles_per_req)
    smem_rows = pl.BlockSpec((1, 1, tm), lambda i: (i, 0, 0), memory_space=pltpu.SMEM)
    smem_next = pl.BlockSpec((1, 1, tm), lambda i: (jnp.minimum(i + 1, n_steps - 1), 0, 0),
                             memory_space=pltpu.SMEM)
    p1 = pos1.reshape(n_steps, 1, tm)
    p2 = pos2.reshape(n_steps, 1, tm)
    if final_norm:
        out_specs = [pl.BlockSpec((tm, d), lambda i: (jnp.minimum(i, n_ctx_tiles - 1), 0)),
                     pl.BlockSpec((tm, d), lambda i: (jnp.maximum(i - n_ctx_tiles, 0), 0))]
        out_shape = [jax.ShapeDtypeStruct((n_ctx_tiles * tm, d), F32),
                     jax.ShapeDtypeStruct((t - n_ctx_tiles * tm, d), F32)]
    else:
        out_specs = pl.BlockSpec((tm, d), lambda i: (i, 0))
        out_shape = jax.ShapeDtypeStruct((t, d), F32)
    return pl.pallas_call(
        functools.partial(_combine_kernel, tm=tm, n_steps=n_steps, n_ctx_tiles=n_ctx_tiles,
                          final_norm=final_norm),
        grid=(n_steps,),
        in_specs=[smem_rows, smem_rows, smem_next, smem_next,
                  pl.BlockSpec(memory_space=pl.ANY),
                  pl.BlockSpec((tm, d), lambda i: (i, 0)),
                  pl.BlockSpec((tm, 128), lambda i: (i, 0)),
                  pl.BlockSpec((1, 6, d), lambda i: (mrow(i), 0, 0)),
                  pl.BlockSpec((1, d), lambda i: (0, 0))],
        out_specs=out_specs,
        out_shape=out_shape,
        scratch_shapes=[pltpu.VMEM((2, 2, tm) + y_sorted.shape[1:], y_sorted.dtype),
                        pltpu.SemaphoreType.DMA((2, 2))],
        compiler_params=_cparams("arbitrary"),
        name="moe_combine",
    )(p1, p2, p1, p2, y_sorted, x, ew, mod, final_g.reshape(1, d))


def _axial_angles(seq, head_dim):
    rows = seq // GRID_W
    row = jnp.repeat(jnp.arange(rows), GRID_W).astype(F32)
    col = jnp.tile(jnp.arange(GRID_W), rows).astype(F32)
    n_freq = head_dim // 4
    inv = ROPE_BASE ** (-jnp.arange(n_freq, dtype=F32) / n_freq)
    return jnp.concatenate([row[:, None] * inv, col[:, None] * inv], axis=-1)


def kernel(x_prompt, x_sample, cache_attn_k, cache_attn_v, state_hgrn, state_ret, c, c_ctx, mod_w, mod_b, norm1_g, norm2_g, even_w_in, even_w_out, hgrn_lb_logits, hgrn_norm_g, attn_qn_g, attn_kn_g, odd_w_in, odd_w_out, ret_decay_logit, ret_gn_g, moe_w_group, moe_b_group, moe_w_expert, moe_b_expert, moe_w_gate, moe_w_up, moe_w_down, final_norm_g):
    batch, seq, d = x_prompt.shape
    dec_batch, dec_seq, _ = x_sample.shape
    depth = mod_w.shape[0]
    heads_a, val_a = hgrn_norm_g.shape[1:]
    key_a = hgrn_lb_logits.shape[2] // heads_a
    hd_b = attn_qn_g.shape[1]
    kv_b = cache_attn_k.shape[3]
    heads_b = (even_w_out.shape[1] - heads_a * val_a) // hd_b
    heads_c, val_c = ret_gn_g.shape[1:]
    key_c = state_ret.shape[4]
    n_experts = moe_w_expert.shape[2]
    n_groups = moe_w_group.shape[2]
    tm = TOKEN_TILE
    n_ctx = batch * seq
    n_ctx_tiles = n_ctx // tm
    tiles_per_req = dec_seq // tm
    assert seq % tm == 0 and dec_seq % tm == 0 and n_ctx % dec_seq == 0
    assert key_a == 128 and val_a == 128 and hd_b == 128 and n_groups + n_experts <= 128

    x = (x_prompt.reshape(n_ctx, d), x_sample.reshape(dec_batch * dec_seq, d))
    t = n_ctx + dec_batch * dec_seq

    n_vec = 1 + dec_batch
    rows = -(-n_vec // 8) * 8
    cvecs = jnp.zeros((rows, d), F32).at[0].set(c_ctx).at[1:n_vec].set(c)
    mods = _ada_mod(cvecs, mod_w, mod_b).reshape(depth, rows, 6, d)

    lower_bounds = jnp.cumsum(jax.nn.softmax(hgrn_lb_logits.astype(F32), axis=0), axis=0)
    log_gamma = jax.nn.log_sigmoid(ret_decay_logit.astype(F32))

    ang_b = _axial_angles(dec_seq, hd_b)
    cos_b = jnp.concatenate([jnp.cos(ang_b), jnp.cos(ang_b)], axis=-1)
    sin_b = jnp.concatenate([-jnp.sin(ang_b), jnp.sin(ang_b)], axis=-1)
    ang_c = _axial_angles(dec_seq, key_c)
    cos_c, sin_c = jnp.cos(ang_c), jnp.sin(ang_c)

    pad = 128 - n_groups - n_experts
    new_k = new_v = new_sh = new_sr = None
    assert depth == 2, "one HGRN2/GQA layer followed by one retention layer"
    for l in range(depth):
        mod = mods[l]
        if l % 2 == 0:
            e = l // 2
            na, nva, nqb, nkv = heads_a * key_a, heads_a * val_a, heads_b * hd_b, kv_b * hd_b
            o_zf, o_kb = na + nva, 3 * na + 2 * nva + nqb
            w_in = even_w_in[e].astype(BF16)
            proj16, proj32 = _in_proj(x, norm1_g[l], mod, w_in, t, n_ctx_tiles, tiles_per_req, w_in.shape[1],
                                      f32_cols=((o_zf, o_zf + 2 * na), (o_kb, o_kb + 2 * nkv)))
            lb = lower_bounds[l]
            oa, s_ctx = _hgrn(proj16, proj32, lb, hgrn_norm_g[e], None, e, None, batch, seq, 0,
                              heads_a, key_a, val_a)
            oa, _ = _hgrn(proj16, proj32, lb, hgrn_norm_g[e], state_hgrn, e, oa, dec_batch, dec_seq,
                          n_ctx // dec_seq, heads_a, key_a, val_a)
            col_q, col_kv = (na + 2 * nva) // hd_b, (2 * na) // hd_b
            ob, kc, vc = _attn(proj16, proj32, attn_qn_g[e], attn_kn_g[e], None, e, None, batch, seq, 0,
                               heads_b, kv_b, hd_b, col_q, col_kv)
            (ob,) = _attn(proj16, proj32, attn_qn_g[e], attn_kn_g[e],
                          (cache_attn_k, cache_attn_v, cos_b, sin_b), e, ob,
                          dec_batch, dec_seq, n_ctx // dec_seq, heads_b, kv_b, hd_b, col_q, col_kv)
            new_k = kc.reshape(batch, 1, seq, kv_b, hd_b)
            new_v = vc.reshape(batch, 1, seq, kv_b, hd_b)
            new_sh = s_ctx[:, None]
            w_out = even_w_out[e].astype(BF16)
            mixes = [oa, ob]
            w_parts = [w_out[:heads_a * val_a], w_out[heads_a * val_a:]]
        else:
            o = l // 2
            w_in = odd_w_in[o].astype(BF16)
            proj = _in_proj(x, norm1_g[l], mod, w_in, t, n_ctx_tiles, tiles_per_req, w_in.shape[1] // 2)
            y, s_ctx = _retention(proj, log_gamma[o], ret_gn_g[o], None, o, None, batch, seq, 0,
                                  heads_c, key_c, val_c)
            y, _ = _retention(proj, log_gamma[o], ret_gn_g[o], (state_ret, cos_c, sin_c), o, y,
                              dec_batch, dec_seq, n_ctx // dec_seq, heads_c, key_c, val_c)
            new_sr = s_ctx[:, None]
            mixes = [y]
            w_parts = [odd_w_out[o].astype(BF16)]

        w_router = jnp.concatenate([moe_w_group[l], moe_w_expert[l], jnp.zeros((d, pad), F32)], axis=1)
        w_router_hi = w_router.astype(BF16)
        w_router = jnp.concatenate([w_router_hi, (w_router - w_router_hi.astype(F32)).astype(BF16)], axis=1)
        b_router = jnp.concatenate([moe_b_group[l], moe_b_expert[l], jnp.zeros((pad,), F32)]).reshape(1, 128)
        xn, h2, meta, ew, cnt = _out_proj(mixes, w_parts, x, mod, norm2_g[l], w_router, b_router,
                                          n_ctx_tiles, tiles_per_req, n_groups, n_experts)
        counts = cnt[0, n_groups:n_groups + n_experts].astype(jnp.int32)
        tile_expert, n_used, ends, pos1, pos2, n_tiles = _dispatch_plan(meta, counts, n_experts, EXPERT_TILE)
        src_tok = _row_source(ends, pos1, pos2, n_tiles * EXPERT_TILE, EXPERT_TILE)
        y_sorted = _moe_experts(h2, src_tok, tile_expert, n_used, moe_w_gate, moe_w_up, moe_w_down, l, EXPERT_TILE)
        x = _moe_combine(y_sorted, pos1, pos2, xn, ew, mod, final_norm_g, n_ctx_tiles, tiles_per_req,
                         final_norm=(l == depth - 1))

    y_prompt = x[0].reshape(batch, seq, d)
    y_sample = x[1].reshape(dec_batch, dec_seq, d)
    return (y_prompt, y_sample, new_k, new_v, new_sh, new_sr)
```

```python
import functools

import numpy as np
import jax
import jax.numpy as jnp
from jax import lax
from jax.experimental import pallas as pl
from jax.experimental.pallas import tpu as pltpu

F32 = jnp.float32
BF16 = jnp.bfloat16
EPS = 1e-6
ROPE_BASE = 10000.0
GRID_W = 64
N_GROUPS = 4
TOP_K_INNER = 2
HGRN_BLOCK = 128
HGRN_DIAG = 8
RET_CHUNK = 256
TOKEN_TILE = 256
INPROJ_TILE = 512
EXPERT_TILE = 256
MOE_GATHER_AHEAD = 2
ROW_SOURCE_CHUNK = 512
V7X_VMEM_LIMIT_BYTES = 56 * 1024 * 1024
HIGHEST = lax.Precision.HIGHEST
LOG2_E = 1.4426950408889634

_NT = (((1,), (1,)), ((), ()))


def _cparams(*sem):
    return pltpu.CompilerParams(dimension_semantics=sem, vmem_limit_bytes=V7X_VMEM_LIMIT_BYTES)


def _rms(x, g):
    return x * lax.rsqrt(jnp.mean(x * x, axis=-1, keepdims=True) + EPS) * g


def _silu(x):
    return x * jax.nn.sigmoid(x)


def _bdot(a, b):
    return jnp.dot(a.astype(BF16), b.astype(BF16), preferred_element_type=F32)


def _bdot_nt(a, b):
    return lax.dot_general(a.astype(BF16), b.astype(BF16), _NT, preferred_element_type=F32)


def _mod_kernel(c_ref, w_ref, b_ref, o_ref):
    a = _silu(c_ref[...])
    o_ref[0] = jnp.dot(a, w_ref[0], precision=HIGHEST, preferred_element_type=F32) + b_ref[0]


def _ada_mod(cvecs, mod_w, mod_b):
    depth, d, n = mod_w.shape
    tn = n // 4
    rows = cvecs.shape[0]
    return pl.pallas_call(
        _mod_kernel,
        grid=(depth, n // tn),
        in_specs=[pl.BlockSpec((rows, d), lambda l, j: (0, 0)),
                  pl.BlockSpec((1, d, tn), lambda l, j: (l, 0, j)),
                  pl.BlockSpec((1, 1, tn), lambda l, j: (l, 0, j))],
        out_specs=pl.BlockSpec((1, rows, tn), lambda l, j: (l, 0, j)),
        out_shape=jax.ShapeDtypeStruct((depth, rows, n), F32),
        compiler_params=_cparams("arbitrary", "arbitrary"),
        name="ada_mod",
    )(cvecs, mod_w, mod_b.reshape(depth, 1, n))


def _tile_mod_row(i, n_ctx_tiles, tiles_per_req):
    return jnp.where(i < n_ctx_tiles, 0, 1 + jnp.maximum(i - n_ctx_tiles, 0) // tiles_per_req)


def _token_specs(x, tm, n_ctx_tiles, step_of):
    if not isinstance(x, tuple):
        return [pl.BlockSpec((tm, x.shape[1]), lambda *g: (step_of(*g), 0))], (x,)
    d = x[0].shape[1]
    return ([pl.BlockSpec((tm, d), lambda *g: (jnp.minimum(step_of(*g), n_ctx_tiles - 1), 0)),
             pl.BlockSpec((tm, d), lambda *g: (jnp.maximum(step_of(*g) - n_ctx_tiles, 0), 0))], x)


def _token_rows(x_refs, i, n_ctx_tiles):
    if len(x_refs) == 1:
        return x_refs[0][...]
    return jnp.where(i < n_ctx_tiles, x_refs[0][...], x_refs[1][...])


def _inproj_kernel(*refs, n_x, n_ctx_tiles, f32_cols):
    x_refs = refs[:n_x]
    g_ref, mod_ref, w_ref = refs[n_x:n_x + 3]
    out_refs = refs[n_x + 3:]
    x = _token_rows(x_refs, pl.program_id(1), n_ctx_tiles)
    h = _rms(x, g_ref[...]) * (1.0 + mod_ref[0, 1:2, :]) + mod_ref[0, 0:1, :]
    acc = jnp.dot(h.astype(BF16), w_ref[...], preferred_element_type=F32)
    if not f32_cols:
        out_refs[0][...] = acc.astype(BF16)
        return
    n = acc.shape[1]
    at16 = at32 = start = 0
    for lo, hi in tuple(f32_cols) + ((n, n),):
        if lo > start:
            out_refs[0][:, at16:at16 + lo - start] = acc[:, start:lo].astype(BF16)
            at16 += lo - start
        if hi > lo:
            out_refs[1][:, at32:at32 + hi - lo] = acc[:, lo:hi]
            at32 += hi - lo
        start = hi


def _in_proj(x, g, mod, w_bf, t, n_ctx_tiles, tiles_per_req, tn, f32_cols=()):
    d, n = w_bf.shape
    tm = INPROJ_TILE
    scale = INPROJ_TILE // TOKEN_TILE
    assert n_ctx_tiles % scale == 0 and tiles_per_req % scale == 0
    n_ctx_tiles, tiles_per_req = n_ctx_tiles // scale, tiles_per_req // scale
    mrow = functools.partial(_tile_mod_row, n_ctx_tiles=n_ctx_tiles, tiles_per_req=tiles_per_req)
    x_specs, x_args = _token_specs(x, tm, n_ctx_tiles, lambda j, i: i)
    if not f32_cols:
        out_specs = pl.BlockSpec((tm, tn), lambda j, i: (i, j))
        out_shape = jax.ShapeDtypeStruct((t, n), BF16)
    else:
        assert tn == n
        n32 = sum(hi - lo for lo, hi in f32_cols)
        out_specs = [pl.BlockSpec((tm, n - n32), lambda j, i: (i, 0)),
                     pl.BlockSpec((tm, n32), lambda j, i: (i, 0))]
        out_shape = [jax.ShapeDtypeStruct((t, n - n32), BF16), jax.ShapeDtypeStruct((t, n32), F32)]
    return pl.pallas_call(
        functools.partial(_inproj_kernel, n_x=len(x_args), n_ctx_tiles=n_ctx_tiles, f32_cols=tuple(f32_cols)),
        grid=(n // tn, t // tm),
        in_specs=x_specs + [pl.BlockSpec((1, d), lambda j, i: (0, 0)),
                            pl.BlockSpec((1, 6, d), lambda j, i: (mrow(i), 0, 0)),
                            pl.BlockSpec((d, tn), lambda j, i: (0, j))],
        out_specs=out_specs,
        out_shape=out_shape,
        compiler_params=_cparams("arbitrary", "arbitrary"),
        name="in_proj",
    )(*x_args, g.reshape(1, d), mod, w_bf)


def _hgrn_decay(z, lb, reverse):
    nb = HGRN_BLOCK
    row = lax.broadcasted_iota(jnp.int32, (nb, nb), 0)
    col = lax.broadcasted_iota(jnp.int32, (nb, nb), 1)
    tri = jnp.where((col >= row) if reverse else (col <= row), 1.0, 0.0).astype(BF16)
    logf = jnp.log(lb + (1.0 - lb) * jax.nn.sigmoid(z))
    k = (1.0 - lb) * jax.nn.sigmoid(-z)
    l2 = logf * LOG2_E
    hi = l2.astype(BF16)
    lo = (l2 - hi.astype(F32)).astype(BF16)
    c = jnp.dot(tri, hi, preferred_element_type=F32) + jnp.dot(tri, lo, preferred_element_type=F32)
    return c, k


def _hgrn_block(q, v, st_t, reverse, c_ref, k_ref, ecat):
    nb = HGRN_BLOCK
    row = lax.broadcasted_iota(jnp.int32, (nb, nb), 0)
    col = lax.broadcasted_iota(jnp.int32, (nb, nb), 1)
    c = c_ref[...]
    k = k_ref[...]
    edge = 0 if reverse else nb - 1
    c_edge = c_ref[edge:edge + 1, :]

    o = _bdot_nt(q * jnp.exp2(c), st_t)
    kd = k * jnp.exp2(c_edge - c)
    st_new = st_t * jnp.exp2(c_edge) + _bdot(v.T, kd)

    a = jnp.zeros((nb, nb), F32)
    m = nb // 2
    while m >= HGRN_DIAG:
        pieces = []
        for grp in range(nb // (2 * m)):
            idx = grp * 2 * m + (m if reverse else m - 1)
            pieces.append(jnp.broadcast_to(c_ref[idx:idx + 1, :], (2 * m, nb)))
        bnd = pieces[0] if len(pieces) == 1 else jnp.concatenate(pieces, axis=0)
        e = jnp.exp2(-jnp.abs(c - bnd))
        same = (row & ~(2 * m - 1)) == (col & ~(2 * m - 1))
        r_hi = (row & m) != 0
        c_hi = (col & m) != 0
        z = (jnp.where(r_hi != reverse, q, k) * e).astype(BF16)
        am = lax.dot_general(z, z, _NT, preferred_element_type=F32)
        if reverse:
            mask = same & jnp.logical_not(r_hi) & c_hi
        else:
            mask = same & r_hi & jnp.logical_not(c_hi)
        a = jnp.where(mask, am, a)
        m //= 2

    dg = HGRN_DIAG
    rowi = lax.broadcasted_iota(jnp.int32, (dg, nb), 0)
    cols = []
    for s in range(dg):
        keep = (rowi <= s) if reverse else (rowi >= s)
        tiles = []
        for j in range(nb // dg):
            r = j * dg + s
            qj = q[j * dg:(j + 1) * dg, :]
            cj = c[j * dg:(j + 1) * dg, :]
            cs = jnp.broadcast_to(c_ref[r:r + 1, :], (dg, nb))
            ks = jnp.broadcast_to(k_ref[r:r + 1, :], (dg, nb))
            tiles.append(qj * ks * jnp.exp2(jnp.where(keep, cj - cs, -jnp.inf)))
        cols.append(jnp.concatenate(tiles, axis=0).astype(BF16))
    p = jnp.concatenate(cols, axis=1)
    red = jnp.dot(p, ecat, preferred_element_type=F32)
    a = jnp.where((row >> 3) == (col >> 3), red, a)
    o = o + _bdot(a, v)
    return o, st_new


def _hgrn_kernel(*refs, nblk, has_state, qscale):
    qa_ref, ia_ref, zf_ref, zb_ref, ga_ref, lb_ref, ng_ref, ecat_ref = refs[:8]
    s0_ref = refs[8] if has_state else None
    o_ref, sfin_ref, oacc_ref, c_scr, k_scr = refs[-5:]
    nb = HGRN_BLOCK

    def gates(direction, blk, slot):
        z_ref = zb_ref if direction == 1 else zf_ref
        r0 = blk * nb if isinstance(blk, int) else pl.multiple_of(blk * nb, nb)
        c, k = _hgrn_decay(z_ref[pl.ds(r0, nb), :], lb_ref[direction:direction + 1, :], direction == 1)
        c_scr[direction, slot] = c
        k_scr[direction, slot] = k

    def block(direction, blk, slot, st_t):
        r0 = blk * nb if isinstance(blk, int) else pl.multiple_of(blk * nb, nb)
        q = _silu(qa_ref[pl.ds(r0, nb), :].astype(F32)) * qscale
        v = ia_ref[pl.ds(r0, nb), :].astype(F32)
        o, st_new = _hgrn_block(q, v, st_t, direction == 1, c_scr.at[direction, slot],
                                k_scr.at[direction, slot], ecat_ref[...])
        oacc_ref[direction, pl.ds(r0, nb), :] = o
        return st_new

    gates(0, 0, 0)
    gates(1, nblk - 1, 0)

    def body(i, carry):
        st_f, st_b = carry
        slot = lax.rem(i, 2)
        st_f = block(0, i, slot, st_f)
        st_b = block(1, nblk - 1 - i, slot, st_b)
        gates(0, i + 1, 1 - slot)
        gates(1, nblk - 2 - i, 1 - slot)
        return st_f, st_b

    if has_state:
        st0 = (s0_ref[0].T, s0_ref[1].T)
    else:
        zero = jnp.zeros((ia_ref.shape[1], qa_ref.shape[1]), F32)
        st0 = (zero, zero)
    st_f, st_b = lax.fori_loop(0, nblk - 1, body, st0)
    last_slot = (nblk - 1) % 2
    st_f = block(0, nblk - 1, last_slot, st_f)
    st_b = block(1, 0, last_slot, st_b)
    sfin_ref[0] = st_f.T
    sfin_ref[1] = st_b.T

    o = oacc_ref[0] + oacc_ref[1]
    o_ref[...] = (_rms(o, ng_ref[0]) * _silu(ga_ref[...].astype(F32))).astype(o_ref.dtype)


def _hgrn(proj16, proj32, lb, norm_g, s0, layer_idx, out_prev, n_seq, seq, row_blk0, heads, key_d, val_d):
    t = proj16.shape[0]
    has_state = s0 is not None
    hk = heads * key_d // 128
    off_qa, off_ia, off_ga = 0, hk, 2 * hk
    off_zf, off_zb = 0, hk

    def col(off):
        return pl.BlockSpec((seq, 128), lambda b, h: (row_blk0 + b, off + h))

    lane = np.arange(key_d)[None, :]
    srow = np.repeat(np.arange(HGRN_DIAG), key_d)[:, None]
    ecat = jnp.asarray((lane % HGRN_DIAG == srow).astype(np.float32), dtype=BF16)
    in_specs = [col(off_qa), col(off_ia), col(off_zf), col(off_zb), col(off_ga),
                pl.BlockSpec((2, key_d), lambda b, h: (0, h)),
                pl.BlockSpec((1, 1, val_d), lambda b, h: (h, 0, 0)),
                pl.BlockSpec((HGRN_DIAG * key_d, key_d), lambda b, h: (0, 0))]
    args = [proj16, proj16, proj32, proj32, proj16, lb, norm_g.reshape(heads, 1, val_d), ecat]
    if has_state:
        in_specs.append(pl.BlockSpec((None, None, 2, None, key_d, val_d),
                                     lambda b, h: (b, layer_idx, 0, h, 0, 0)))
        args.append(s0)
    aliases = {}
    if out_prev is not None:
        in_specs.append(pl.BlockSpec(memory_space=pl.ANY))
        args.append(out_prev)
        aliases = {len(args) - 1: 0}

    kern = functools.partial(_hgrn_kernel, nblk=seq // HGRN_BLOCK, has_state=has_state,
                             qscale=float(key_d) ** -0.5)
    return pl.pallas_call(
        kern,
        grid=(n_seq, heads),
        in_specs=in_specs,
        out_specs=[pl.BlockSpec((seq, val_d), lambda b, h: (row_blk0 + b, h)),
                   pl.BlockSpec((None, 2, None, key_d, val_d), lambda b, h: (b, 0, h, 0, 0))],
        out_shape=[jax.ShapeDtypeStruct((t, heads * val_d), BF16),
                   jax.ShapeDtypeStruct((n_seq, 2, heads, key_d, val_d), F32)],
        scratch_shapes=[pltpu.VMEM((2, seq, val_d), F32),
                        pltpu.VMEM((2, 2, HGRN_BLOCK, key_d), F32),
                        pltpu.VMEM((2, 2, HGRN_BLOCK, key_d), F32)],
        input_output_aliases=aliases,
        compiler_params=_cparams("arbitrary", "arbitrary"),
        name="hgrn2",
    )(*args)


def _rope(x, cos, sin_signed):
    return x * cos + pltpu.roll(x, x.shape[-1] // 2, 1) * sin_signed


def _attn_kernel(*refs, has_ctx, grp, qblk, scale):
    if has_ctx:
        (q_ref, k_ref, v_ref, qn_ref, kn_ref, ck_ref, cv_ref, cos_ref, sin_ref, _prev, o_ref) = refs
    else:
        (q_ref, k_ref, v_ref, qn_ref, kn_ref, o_ref, ko_ref, vo_ref) = refs
    seq, hd = k_ref.shape
    k = _rms(k_ref[...], kn_ref[...])
    v = v_ref[...]
    if has_ctx:
        k = _rope(k, cos_ref[...], sin_ref[...])
        ck = ck_ref[...].astype(BF16)
        cv = cv_ref[...].astype(BF16)
    else:
        ko_ref[...] = k
        vo_ref[...] = v
    kb = k.astype(BF16)
    vb = v.astype(BF16)

    def body(i, carry):
        r0 = pl.multiple_of(i * qblk, qblk)
        for g in range(grp):
            q = _rms(q_ref[pl.ds(r0, qblk), g * hd:(g + 1) * hd].astype(F32), qn_ref[...])
            if has_ctx:
                q = _rope(q, cos_ref[pl.ds(r0, qblk), :], sin_ref[pl.ds(r0, qblk), :])
            qb = q.astype(BF16)
            s2 = lax.dot_general(qb, kb, _NT, preferred_element_type=F32) * scale
            m = jnp.max(s2, axis=-1, keepdims=True)
            if has_ctx:
                s1 = lax.dot_general(qb, ck, _NT, preferred_element_type=F32) * scale
                m = jnp.maximum(m, jnp.max(s1, axis=-1, keepdims=True))
                p1 = jnp.exp(s1 - m)
            p2 = jnp.exp(s2 - m)
            den = jnp.sum(p2, axis=-1, keepdims=True)
            acc = jnp.dot(p2.astype(BF16), vb, preferred_element_type=F32)
            if has_ctx:
                den = den + jnp.sum(p1, axis=-1, keepdims=True)
                acc = acc + jnp.dot(p1.astype(BF16), cv, preferred_element_type=F32)
            o_ref[pl.ds(r0, qblk), g * hd:(g + 1) * hd] = (acc / den).astype(o_ref.dtype)
        return carry

    lax.fori_loop(0, seq // qblk, body, 0)


def _attn(proj16, proj32, qn_g, kn_g, ctx, layer_idx, out_prev, n_seq, seq, row_blk0, heads, kv_heads, hd,
          col_q, col_kv):
    t = proj16.shape[0]
    grp = heads // kv_heads
    has_ctx = ctx is not None
    cq = col_q // grp
    ck0 = col_kv
    cv0 = ck0 + kv_heads
    in_specs = [pl.BlockSpec((seq, grp * hd), lambda b, h: (row_blk0 + b, cq + h)),
                pl.BlockSpec((seq, hd), lambda b, h: (row_blk0 + b, ck0 + h)),
                pl.BlockSpec((seq, hd), lambda b, h: (row_blk0 + b, cv0 + h)),
                pl.BlockSpec((1, hd), lambda b, h: (0, 0)),
                pl.BlockSpec((1, hd), lambda b, h: (0, 0))]
    args = [proj16, proj32, proj32, qn_g.reshape(1, hd), kn_g.reshape(1, hd)]
    out_specs = [pl.BlockSpec((seq, grp * hd), lambda b, h: (row_blk0 + b, h))]
    out_shape = [jax.ShapeDtypeStruct((t, heads * hd), BF16)]
    aliases = {}
    if has_ctx:
        cache_k, cache_v, cos, sin_signed = ctx
        nreq, nlay, past = cache_k.shape[:3]
        cache_k = cache_k.reshape(nreq, nlay, past, kv_heads * hd)
        cache_v = cache_v.reshape(nreq, nlay, past, kv_heads * hd)
        spec_c = pl.BlockSpec((None, None, past, hd), lambda b, h: (b, layer_idx, 0, h))
        in_specs += [spec_c, spec_c,
                     pl.BlockSpec((seq, hd), lambda b, h: (0, 0)),
                     pl.BlockSpec((seq, hd), lambda b, h: (0, 0)),
                     pl.BlockSpec(memory_space=pl.ANY)]
        args += [cache_k, cache_v, cos, sin_signed, out_prev]
        aliases = {len(args) - 1: 0}
    else:
        out_specs += [pl.BlockSpec((seq, hd), lambda b, h: (b, h)),
                      pl.BlockSpec((seq, hd), lambda b, h: (b, h))]
        out_shape += [jax.ShapeDtypeStruct((n_seq * seq, kv_heads * hd), F32)] * 2
    return pl.pallas_call(
        functools.partial(_attn_kernel, has_ctx=has_ctx, grp=grp, qblk=min(seq, 256), scale=float(hd) ** -0.5),
        grid=(n_seq, kv_heads),
        in_specs=in_specs,
        out_specs=out_specs,
        out_shape=out_shape,
        input_output_aliases=aliases,
        compiler_params=_cparams("arbitrary", "arbitrary"),
        name="gqa",
    )(*args)


def _ret_kernel(*refs, nchunk, has_ctx, kscale):
    if has_ctx:
        (lg_ref, q_ref, k_ref, v_ref, g_ref, gn_ref, s0_ref, cos_ref, sin_ref, _prev,
         o_ref, sfin_ref, sb_ref) = refs
    else:
        (lg_ref, q_ref, k_ref, v_ref, g_ref, gn_ref, o_ref, sfin_ref, sb_ref) = refs
    cs = RET_CHUNK
    kd_dim = q_ref.shape[1]
    h = pl.program_id(1)
    lgf = lg_ref[0, h]
    lgb = lg_ref[1, h]
    row = lax.broadcasted_iota(jnp.int32, (cs, cs), 0)
    col = lax.broadcasted_iota(jnp.int32, (cs, cs), 1)
    rel = (row - col).astype(F32)
    dmat = (jnp.exp(jnp.where(rel >= 0, rel * lgf, -jnp.inf))
            + jnp.exp(jnp.where(rel <= 0, -rel * lgb, -jnp.inf)))
    j = lax.broadcasted_iota(jnp.int32, (cs, 1), 0).astype(F32)
    one = jnp.ones((1, 1), F32)
    dec_f = jnp.exp(one * (cs * lgf))
    dec_b = jnp.exp(one * (cs * lgb))

    def load_qk(ref, r0, mul):
        x = ref[pl.ds(r0, cs), :].astype(F32) * mul
        if has_ctx:
            half = kd_dim // 2
            x1, x2 = x[:, :half], x[:, half:]
            cos = cos_ref[pl.ds(r0, cs), :]
            sin = sin_ref[pl.ds(r0, cs), :]
            x = jnp.concatenate([x1 * cos - x2 * sin, x2 * cos + x1 * sin], axis=1)
        return x

    def bwd_body(i, st):
        n = nchunk - 1 - i
        r0 = pl.multiple_of(n * cs, cs)
        sb_ref[n] = st
        k = load_qk(k_ref, r0, kscale)
        v = v_ref[pl.ds(r0, cs), :]
        return st * dec_b + _bdot((k * jnp.exp(j * lgb)).T, v)

    st_b0 = s0_ref[1] if has_ctx else jnp.zeros(sfin_ref.shape[1:], F32)
    st_b = lax.fori_loop(0, nchunk, bwd_body, st_b0)
    sfin_ref[1] = st_b

    def fwd_body(n, st):
        r0 = pl.multiple_of(n * cs, cs)
        q = load_qk(q_ref, r0, 1.0)
        k = load_qk(k_ref, r0, kscale)
        v = v_ref[pl.ds(r0, cs), :]
        a = _bdot_nt(q, k) * dmat
        o = _bdot(a, v)
        if has_ctx or nchunk > 1:
            o = o + _bdot(q * jnp.exp((j + 1.0) * lgf), st)
            o = o + _bdot(q * jnp.exp((cs - j) * lgb), sb_ref[n])
        mu = jnp.mean(o, axis=-1, keepdims=True)
        var = jnp.mean(jnp.square(o - mu), axis=-1, keepdims=True)
        y = (o - mu) * lax.rsqrt(var + EPS) * gn_ref[0]
        o_ref[pl.ds(r0, cs), :] = (_silu(g_ref[pl.ds(r0, cs), :].astype(F32)) * y).astype(o_ref.dtype)
        return st * dec_f + _bdot((k * jnp.exp((cs - 1.0 - j) * lgf)).T, v)

    st_f0 = s0_ref[0] if has_ctx else jnp.zeros(sfin_ref.shape[1:], F32)
    sfin_ref[0] = lax.fori_loop(0, nchunk, fwd_body, st_f0)


def _retention(proj, log_gamma, gn_g, ctx, layer_idx, out_prev, n_seq, seq, row_blk0, heads, key_d, val_d):
    t = proj.shape[0]
    has_ctx = ctx is not None
    nq = heads * key_d
    cq, ck, cv, cg = 0, nq // key_d, 2 * nq // val_d, (2 * nq + heads * val_d) // val_d
    in_specs = [pl.BlockSpec(memory_space=pltpu.SMEM),
                pl.BlockSpec((seq, key_d), lambda b, h: (row_blk0 + b, cq + h)),
                pl.BlockSpec((seq, key_d), lambda b, h: (row_blk0 + b, ck + h)),
                pl.BlockSpec((seq, val_d), lambda b, h: (row_blk0 + b, cv + h)),
                pl.BlockSpec((seq, val_d), lambda b, h: (row_blk0 + b, cg + h)),
                pl.BlockSpec((1, 1, val_d), lambda b, h: (h, 0, 0))]
    args = [log_gamma, proj, proj, proj, proj, gn_g.reshape(heads, 1, val_d)]
    aliases = {}
    if has_ctx:
        s0, cos, sin = ctx
        in_specs += [pl.BlockSpec((None, None, 2, None, key_d, val_d), lambda b, h: (b, layer_idx, 0, h, 0, 0)),
                     pl.BlockSpec((seq, key_d // 2), lambda b, h: (0, 0)),
                     pl.BlockSpec((seq, key_d // 2), lambda b, h: (0, 0)),
                     pl.BlockSpec(memory_space=pl.ANY)]
        args += [s0, cos, sin, out_prev]
        aliases = {len(args) - 1: 0}
    nchunk = seq // RET_CHUNK
    return pl.pallas_call(
        functools.partial(_ret_kernel, nchunk=nchunk, has_ctx=has_ctx, kscale=float(key_d) ** -0.5),
        grid=(n_seq, heads),
        in_specs=in_specs,
        out_specs=[pl.BlockSpec((seq, val_d), lambda b, h: (row_blk0 + b, h)),
                   pl.BlockSpec((None, 2, None, key_d, val_d), lambda b, h: (b, 0, h, 0, 0))],
        out_shape=[jax.ShapeDtypeStruct((t, heads * val_d), BF16),
                   jax.ShapeDtypeStruct((n_seq, 2, heads, key_d, val_d), F32)],
        scratch_shapes=[pltpu.VMEM((nchunk, key_d, val_d), F32)],
        input_output_aliases=aliases,
        compiler_params=_cparams("arbitrary", "arbitrary"),
        name="retention",
    )(*args)


def _outproj_kernel(*refs, n_in, n_x, n_ctx_tiles, n_groups, n_experts):
    mix_refs = refs[:n_in]
    w_refs = refs[n_in:2 * n_in]
    x_refs = refs[2 * n_in:2 * n_in + n_x]
    (mod_ref, g2_ref, wr_ref, br_ref, xn_ref, h2_ref, meta_ref, ew_ref, cnt_out_ref,
     cnt_ref) = refs[2 * n_in + n_x:]
    acc = _bdot(mix_refs[0][...], w_refs[0][...])
    for mref, wref in zip(mix_refs[1:], w_refs[1:]):
        acc = acc + _bdot(mref[...], wref[...])
    xn = _token_rows(x_refs, pl.program_id(0), n_ctx_tiles) + mod_ref[0, 2:3, :] * acc
    xn_ref[...] = xn
    h2 = _rms(xn, g2_ref[...]) * (1.0 + mod_ref[0, 4:5, :]) + mod_ref[0, 3:4, :]
    h2_ref[...] = h2.astype(BF16).reshape(h2_ref.shape)
    h_hi = h2.astype(BF16)
    h_lo = (h2 - h_hi.astype(F32)).astype(BF16)
    nl = br_ref.shape[1]
    hh = jnp.dot(h_hi, wr_ref[...], preferred_element_type=F32)
    lg = hh[:, :nl] + hh[:, nl:] + jnp.dot(h_lo, wr_ref[:, :nl], preferred_element_type=F32) + br_ref[...]

    per_grp = n_experts // n_groups
    lane = lax.broadcasted_iota(jnp.int32, lg.shape, 1).astype(F32)
    big = float(lg.shape[1])
    ninf = -jnp.inf
    gl = jnp.where(lane < n_groups, lg, ninf)
    gmax = jnp.max(gl, axis=-1, keepdims=True)
    p_grp = 1.0 / jnp.sum(jnp.exp(gl - gmax), axis=-1, keepdims=True)
    gidx = jnp.min(jnp.where(gl == gmax, lane, big), axis=-1, keepdims=True)
    in_grp = ((lane >= n_groups) & (lane < n_groups + n_experts)
              & (jnp.floor((lane - n_groups) / per_grp) == gidx))
    el = jnp.where(in_grp, lg, ninf)
    emax = jnp.max(el, axis=-1, keepdims=True)
    esum = jnp.sum(jnp.exp(el - emax), axis=-1, keepdims=True)
    i1 = jnp.min(jnp.where(el == emax, lane, big), axis=-1, keepdims=True)
    el2 = jnp.where(lane == i1, ninf, el)
    emax2 = jnp.max(el2, axis=-1, keepdims=True)
    i2 = jnp.min(jnp.where(el2 == emax2, lane, big), axis=-1, keepdims=True)
    p1 = 1.0 / esum
    p2 = jnp.exp(emax2 - emax) / esum
    w1 = p1 / (p1 + p2) * p_grp
    w2 = p2 / (p1 + p2) * p_grp
    ew_ref[...] = jnp.where(lane == 0.0, w1, jnp.where(lane == 1.0, w2, 0.0))

    step = pl.program_id(0)

    @pl.when(step == 0)
    def _():
        cnt_ref[...] = jnp.zeros_like(cnt_ref)

    tm = lg.shape[0]
    hit1 = lane == i1
    hit2 = lane == i2
    sel = jnp.where(hit1 | hit2, 1.0, 0.0)
    rr = lax.broadcasted_iota(jnp.int32, (tm, tm), 0)
    cc = lax.broadcasted_iota(jnp.int32, (tm, tm), 1)
    before = jnp.where(cc < rr, 1.0, 0.0).astype(BF16)
    prior = jnp.dot(before, sel.astype(BF16), preferred_element_type=F32) + cnt_ref[...]
    rank1 = jnp.sum(jnp.where(hit1, prior, 0.0), axis=-1, keepdims=True)
    rank2 = jnp.sum(jnp.where(hit2, prior, 0.0), axis=-1, keepdims=True)
    cnt_ref[...] += jnp.sum(sel, axis=0, keepdims=True)
    cnt_out_ref[...] = cnt_ref[...]
    meta = jnp.where(lane == 0.0, i1 - n_groups,
                     jnp.where(lane == 1.0, i2 - n_groups,
                               jnp.where(lane == 2.0, rank1, jnp.where(lane == 3.0, rank2, 0.0))))
    meta_ref[0] = meta.T[0:8, :].astype(jnp.int32)


def _out_proj(mixes, w_parts, x, mod, g2, w_router, b_router, n_ctx_tiles, tiles_per_req, n_groups, n_experts):
    t = mixes[0].shape[0]
    d = w_parts[0].shape[1]
    tm = TOKEN_TILE
    n_in = len(mixes)
    mrow = functools.partial(_tile_mod_row, n_ctx_tiles=n_ctx_tiles, tiles_per_req=tiles_per_req)
    x_specs, x_args = _token_specs(x, tm, n_ctx_tiles, lambda i: i)
    in_specs = ([pl.BlockSpec((tm, m.shape[1]), lambda i: (i, 0)) for m in mixes]
                + [pl.BlockSpec(w.shape, lambda i: (0, 0)) for w in w_parts]
                + x_specs
                + [pl.BlockSpec((1, 6, d), lambda i: (mrow(i), 0, 0)),
                   pl.BlockSpec((1, d), lambda i: (0, 0)),
                   pl.BlockSpec(w_router.shape, lambda i: (0, 0)),
                   pl.BlockSpec((1, 128), lambda i: (0, 0))])
    return pl.pallas_call(
        functools.partial(_outproj_kernel, n_in=n_in, n_x=len(x_args), n_ctx_tiles=n_ctx_tiles,
                          n_groups=n_groups, n_experts=n_experts),
        grid=(t // tm,),
        in_specs=in_specs,
        out_specs=[pl.BlockSpec((tm, d), lambda i: (i, 0)),
                   pl.BlockSpec((tm, d // 128, 128), lambda i: (i, 0, 0)),
                   pl.BlockSpec((1, 8, tm), lambda i: (i, 0, 0)),
                   pl.BlockSpec((tm, 128), lambda i: (i, 0)),
                   pl.BlockSpec((1, 128), lambda i: (0, 0))],
        out_shape=[jax.ShapeDtypeStruct((t, d), F32),
                   jax.ShapeDtypeStruct((t, d // 128, 128), BF16),
                   jax.ShapeDtypeStruct((t // tm, 8, tm), jnp.int32),
                   jax.ShapeDtypeStruct((t, 128), F32),
                   jax.ShapeDtypeStruct((1, 128), F32)],
        scratch_shapes=[pltpu.VMEM((1, 128), F32)],
        compiler_params=_cparams("arbitrary"),
        name="out_proj_router",
    )(*mixes, *w_parts, *x_args, mod, g2.reshape(1, d), w_router, b_router)


def _dispatch_plan(meta, counts, n_experts, tile):
    t = meta.shape[0] * meta.shape[2]
    e1, e2, rank1, rank2 = (meta[:, k, :].reshape(t) for k in range(4))
    padded = ((counts + tile - 1) // tile) * tile
    ends = jnp.cumsum(padded)
    starts = ends - padded
    experts = jnp.arange(n_experts, dtype=jnp.int32)[None, :]
    pos = [jnp.sum(jnp.where(e[:, None] == experts, starts[None, :], 0), axis=1) + r
           for e, r in ((e1, rank1), (e2, rank2))]
    n_tiles = (2 * t) // tile + n_experts
    n_used = (ends[-1] // tile).astype(jnp.int32)
    tile_start = jnp.arange(n_tiles, dtype=jnp.int32) * tile
    probe = jnp.minimum(tile_start, ends[-1] - 1)
    tile_expert = jnp.sum((ends[None, :] <= probe[:, None]).astype(jnp.int32), axis=1)
    return tile_expert, n_used.reshape(1), ends.astype(jnp.int32), pos[0], pos[1], n_tiles


def _row_source_kernel(ends_ref, p1_ref, p2_ref, o_ref, *, n_tok, n_rows, tile, n_experts):
    def clear8(j, c):
        base = j * 8
        if n_rows <= 3 * n_tok:
            base = jnp.where(base >= 2 * n_tok, base - 2 * n_tok, jnp.where(base >= n_tok, base - n_tok, base))
        else:
            base = lax.rem(base, n_tok)
        for u in range(8):
            o_ref[j * 8 + u] = base + u
        return c

    def per_expert(e, c):
        first = lax.shift_right_logical(jnp.maximum(ends_ref[e] - tile, 0), 3)
        lax.fori_loop(first, first + tile // 8, clear8, 0)
        return c

    g = pl.program_id(0)

    @pl.when(g == 0)
    def _():
        lax.fori_loop(0, n_experts, per_expert, 0)
        lax.fori_loop(lax.shift_right_logical(ends_ref[n_experts - 1], 3), n_rows // 8, clear8, 0)

    chunk = p1_ref.shape[2]

    def place(r, c):
        t = g * chunk + r
        o_ref[p1_ref[0, 0, r]] = t
        o_ref[p2_ref[0, 0, r]] = t
        return c

    lax.fori_loop(0, chunk, place, 0, unroll=8)


def _row_source(ends, pos1, pos2, n_rows, tile):
    n_tok = pos1.shape[0]
    chunk = ROW_SOURCE_CHUNK
    smem = pl.BlockSpec(memory_space=pltpu.SMEM)
    pos_spec = pl.BlockSpec((1, 1, chunk), lambda g: (g, 0, 0), memory_space=pltpu.SMEM)
    return pl.pallas_call(
        functools.partial(_row_source_kernel, n_tok=n_tok, n_rows=n_rows, tile=tile, n_experts=ends.shape[0]),
        grid=(n_tok // chunk,),
        in_specs=[smem, pos_spec, pos_spec],
        out_specs=smem,
        out_shape=jax.ShapeDtypeStruct((n_rows,), jnp.int32),
        compiler_params=_cparams("arbitrary"),
        name="moe_row_source",
    )(ends, pos1.reshape(n_tok // chunk, 1, chunk), pos2.reshape(n_tok // chunk, 1, chunk))


def _moe_kernel(te_ref, nu_ref, *refs, tm, ahead):
    tok_refs = refs[:ahead + 1]
    h_hbm, wg_ref, wu_ref, wd_ref, o_ref, wg_bf, wu_bf, wd_bf, buf, sem = refs[ahead + 1:]
    i = pl.program_id(0)
    nslot = ahead + 1
    slot = lax.rem(i, nslot)
    n_used = nu_ref[0]

    def issue(tile, tok_ref):
        s = lax.rem(tile, nslot)

        def body(j, c):
            for u in range(2):
                r = 2 * j + u
                pltpu.make_async_copy(h_hbm.at[tok_ref[0, 0, r]], buf.at[s, r], sem.at[s]).start(priority=u)
            return c

        lax.fori_loop(0, tm // 2, body, 0, unroll=4)

    @pl.when(i == 0)
    def _():
        for j in range(ahead):
            @pl.when(j < n_used)
            def _():
                issue(jnp.int32(j), tok_refs[j])

    new_expert = jnp.logical_or(i == 0, te_ref[i] != te_ref[jnp.maximum(i - 1, 0)])

    @pl.when(new_expert)
    def _():
        wg_bf[...] = wg_ref[...].astype(BF16)
        wu_bf[...] = wu_ref[...].astype(BF16)
        wd_bf[...] = wd_ref[...].astype(BF16)

    def compute(issue_next):
        s_next = lax.rem(i + ahead, nslot)
        quarter = tm // 4

        def start_rows(part):
            if issue_next:
                for r in range(part * quarter, (part + 1) * quarter):
                    pltpu.make_async_copy(h_hbm.at[tok_refs[ahead][0, 0, r]], buf.at[s_next, r],
                                          sem.at[s_next]).start(priority=r % 2)

        pltpu.make_async_copy(h_hbm.at[pl.ds(0, tm)], buf.at[slot], sem.at[slot]).wait()
        start_rows(0)
        x = buf[slot].reshape(tm, -1)
        g = jnp.dot(x, wg_bf[...], preferred_element_type=F32)
        start_rows(1)
        u = jnp.dot(x, wu_bf[...], preferred_element_type=F32)
        start_rows(2)
        act = _silu(g) * u
        y = jnp.dot(act.astype(BF16), wd_bf[...], preferred_element_type=F32)
        start_rows(3)
        o_ref[...] = y.astype(BF16).reshape(o_ref.shape)

    @pl.when(i + ahead < n_used)
    def _():
        compute(True)

    @pl.when(jnp.logical_and(i + ahead >= n_used, i < n_used))
    def _():
        compute(False)

    @pl.when(i >= n_used)
    def _():
        o_ref[...] = jnp.zeros_like(o_ref)


def _moe_experts(h3, src_tok, tile_expert, n_used, w_gate, w_up, w_down, layer_idx, tm):
    t, sub, lanes = h3.shape
    d = sub * lanes
    n_tiles = src_tok.shape[0] // tm
    f = w_gate.shape[-1]
    ahead = MOE_GATHER_AHEAD
    tok = src_tok.reshape(n_tiles, 1, tm)

    def tok_spec(k):
        return pl.BlockSpec((1, 1, tm), lambda i, te, nu: (jnp.minimum(i + k, n_tiles - 1), 0, 0),
                            memory_space=pltpu.SMEM)

    return pl.pallas_call(
        functools.partial(_moe_kernel, tm=tm, ahead=ahead),
        grid_spec=pltpu.PrefetchScalarGridSpec(
            num_scalar_prefetch=2,
            grid=(n_tiles,),
            in_specs=[tok_spec(k) for k in range(ahead + 1)] + [
                      pl.BlockSpec(memory_space=pl.ANY),
                      pl.BlockSpec((None, None, d, f), lambda i, te, nu: (layer_idx, te[i], 0, 0)),
                      pl.BlockSpec((None, None, d, f), lambda i, te, nu: (layer_idx, te[i], 0, 0)),
                      pl.BlockSpec((None, None, f, d), lambda i, te, nu: (layer_idx, te[i], 0, 0))],
            out_specs=pl.BlockSpec((tm, sub, lanes), lambda i, te, nu: (i, 0, 0)),
            scratch_shapes=[pltpu.VMEM((d, f), BF16), pltpu.VMEM((d, f), BF16), pltpu.VMEM((f, d), BF16),
                            pltpu.VMEM((ahead + 1, tm, sub, lanes), BF16),
                            pltpu.SemaphoreType.DMA((ahead + 1,))]),
        out_shape=jax.ShapeDtypeStruct((n_tiles * tm, sub, lanes), BF16),
        compiler_params=_cparams("arbitrary"),
        name="moe_experts",
    )(tile_expert, n_used, *([tok] * (ahead + 1)), h3, w_gate, w_up, w_down)


def _combine_kernel(p1_ref, p2_ref, p1n_ref, p2n_ref, y_hbm, x_ref, ew_ref, mod_ref, gf_ref, *rest,
                    tm, n_steps, n_ctx_tiles, final_norm):
    out_refs, (buf, sem) = rest[:-2], rest[-2:]
    i = pl.program_id(0)
    slot = lax.rem(i, 2)

    def issue(s, pa_ref, pb_ref):
        def body(r, c):
            pltpu.make_async_copy(y_hbm.at[pa_ref[0, 0, r]], buf.at[s, 0, r], sem.at[s, 0]).start(priority=0)
            pltpu.make_async_copy(y_hbm.at[pb_ref[0, 0, r]], buf.at[s, 1, r], sem.at[s, 1]).start(priority=1)
            return c

        lax.fori_loop(0, tm, body, 0, unroll=8)

    @pl.when(i == 0)
    def _():
        issue(0, p1_ref, p2_ref)

    @pl.when(i + 1 < n_steps)
    def _():
        issue(1 - slot, p1n_ref, p2n_ref)

    for k in range(2):
        pltpu.make_async_copy(y_hbm.at[pl.ds(0, tm)], buf.at[slot, k], sem.at[slot, k]).wait()
    ew = ew_ref[...]
    y = (ew[:, 0:1] * buf[slot, 0].reshape(tm, -1).astype(F32)
         + ew[:, 1:2] * buf[slot, 1].reshape(tm, -1).astype(F32))
    x = x_ref[...] + mod_ref[0, 5:6, :] * y
    if not final_norm:
        out_refs[0][...] = x
    else:
        x = _rms(x, gf_ref[...])
        ctx_ref, lat_ref = out_refs

        @pl.when(i < n_ctx_tiles)
        def _():
            ctx_ref[...] = x

        @pl.when(i >= n_ctx_tiles)
        def _():
            lat_ref[...] = x


def _moe_combine(y_sorted, pos1, pos2, x, ew, mod, final_g, n_ctx_tiles, tiles_per_req, final_norm):
    t, d = x.shape
    tm = TOKEN_TILE
    n_steps = t // tm
    mrow = functools.partial(_tile_mod_row, n_ctx_tiles=n_ctx_tiles, tiles_per_req=tiles_per_req)
    smem_rows = pl.BlockSpec((1, 1, tm), lambda i: (i, 0, 0), memory_space=pltpu.SMEM)
    smem_next = pl.BlockSpec((1, 1, tm), lambda i: (jnp.minimum(i + 1, n_steps - 1), 0, 0),
                             memory_space=pltpu.SMEM)
    p1 = pos1.reshape(n_steps, 1, tm)
    p2 = pos2.reshape(n_steps, 1, tm)
    if final_norm:
        out_specs = [pl.BlockSpec((tm, d), lambda i: (jnp.minimum(i, n_ctx_tiles - 1), 0)),
                     pl.BlockSpec((tm, d), lambda i: (jnp.maximum(i - n_ctx_tiles, 0), 0))]
        out_shape = [jax.ShapeDtypeStruct((n_ctx_tiles * tm, d), F32),
                     jax.ShapeDtypeStruct((t - n_ctx_tiles * tm, d), F32)]
    else:
        out_specs = pl.BlockSpec((tm, d), lambda i: (i, 0))
        out_shape = jax.ShapeDtypeStruct((t, d), F32)
    return pl.pallas_call(
        functools.partial(_combine_kernel, tm=tm, n_steps=n_steps, n_ctx_tiles=n_ctx_tiles,
                          final_norm=final_norm),
        grid=(n_steps,),
        in_specs=[smem_rows, smem_rows, smem_next, smem_next,
                  pl.BlockSpec(memory_space=pl.ANY),
                  pl.BlockSpec((tm, d), lambda i: (i, 0)),
                  pl.BlockSpec((tm, 128), lambda i: (i, 0)),
                  pl.BlockSpec((1, 6, d), lambda i: (mrow(i), 0, 0)),
                  pl.BlockSpec((1, d), lambda i: (0, 0))],
        out_specs=out_specs,
        out_shape=out_shape,
        scratch_shapes=[pltpu.VMEM((2, 2, tm) + y_sorted.shape[1:], y_sorted.dtype),
                        pltpu.SemaphoreType.DMA((2, 2))],
        compiler_params=_cparams("arbitrary"),
        name="moe_combine",
    )(p1, p2, p1, p2, y_sorted, x, ew, mod, final_g.reshape(1, d))


def _axial_angles(seq, head_dim):
    rows = seq // GRID_W
    row = jnp.repeat(jnp.arange(rows), GRID_W).astype(F32)
    col = jnp.tile(jnp.arange(GRID_W), rows).astype(F32)
    n_freq = head_dim // 4
    inv = ROPE_BASE ** (-jnp.arange(n_freq, dtype=F32) / n_freq)
    return jnp.concatenate([row[:, None] * inv, col[:, None] * inv], axis=-1)


def kernel(x_prompt, x_sample, cache_attn_k, cache_attn_v, state_hgrn, state_ret, c, c_ctx, mod_w, mod_b, norm1_g, norm2_g, even_w_in, even_w_out, hgrn_lb_logits, hgrn_norm_g, attn_qn_g, attn_kn_g, odd_w_in, odd_w_out, ret_decay_logit, ret_gn_g, moe_w_group, moe_b_group, moe_w_expert, moe_b_expert, moe_w_gate, moe_w_up, moe_w_down, final_norm_g):
    batch, seq, d = x_prompt.shape
    dec_batch, dec_seq, _ = x_sample.shape
    depth = mod_w.shape[0]
    heads_a, val_a = hgrn_norm_g.shape[1:]
    key_a = hgrn_lb_logits.shape[2] // heads_a
    hd_b = attn_qn_g.shape[1]
    kv_b = cache_attn_k.shape[3]
    heads_b = (even_w_out.shape[1] - heads_a * val_a) // hd_b
    heads_c, val_c = ret_gn_g.shape[1:]
    key_c = state_ret.shape[4]
    n_experts = moe_w_expert.shape[2]
    n_groups = moe_w_group.shape[2]
    tm = TOKEN_TILE
    n_ctx = batch * seq
    n_ctx_tiles = n_ctx // tm
    tiles_per_req = dec_seq // tm
    assert seq % tm == 0 and dec_seq % tm == 0 and n_ctx % dec_seq == 0
    assert key_a == 128 and val_a == 128 and hd_b == 128 and n_groups + n_experts <= 128

    x = (x_prompt.reshape(n_ctx, d), x_sample.reshape(dec_batch * dec_seq, d))
    t = n_ctx + dec_batch * dec_seq

    n_vec = 1 + dec_batch
    rows = -(-n_vec // 8) * 8
    cvecs = jnp.zeros((rows, d), F32).at[0].set(c_ctx).at[1:n_vec].set(c)
    mods = _ada_mod(cvecs, mod_w, mod_b).reshape(depth, rows, 6, d)

    lower_bounds = jnp.cumsum(jax.nn.softmax(hgrn_lb_logits.astype(F32), axis=0), axis=0)
    log_gamma = jax.nn.log_sigmoid(ret_decay_logit.astype(F32))

    ang_b = _axial_angles(dec_seq, hd_b)
    cos_b = jnp.concatenate([jnp.cos(ang_b), jnp.cos(ang_b)], axis=-1)
    sin_b = jnp.concatenate([-jnp.sin(ang_b), jnp.sin(ang_b)], axis=-1)
    ang_c = _axial_angles(dec_seq, key_c)
    cos_c, sin_c = jnp.cos(ang_c), jnp.sin(ang_c)

    pad = 128 - n_groups - n_experts
    new_k = new_v = new_sh = new_sr = None
    assert depth == 2, "one HGRN2/GQA layer followed by one retention layer"
    for l in range(depth):
        mod = mods[l]
        if l % 2 == 0:
            e = l // 2
            na, nva, nqb, nkv = heads_a * key_a, heads_a * val_a, heads_b * hd_b, kv_b * hd_b
            o_zf, o_kb = na + nva, 3 * na + 2 * nva + nqb
            w_in = even_w_in[e].astype(BF16)
            proj16, proj32 = _in_proj(x, norm1_g[l], mod, w_in, t, n_ctx_tiles, tiles_per_req, w_in.shape[1],
                                      f32_cols=((o_zf, o_zf + 2 * na), (o_kb, o_kb + 2 * nkv)))
            lb = lower_bounds[l]
            oa, s_ctx = _hgrn(proj16, proj32, lb, hgrn_norm_g[e], None, e, None, batch, seq, 0,
                              heads_a, key_a, val_a)
            oa, _ = _hgrn(proj16, proj32, lb, hgrn_norm_g[e], state_hgrn, e, oa, dec_batch, dec_seq,
                          n_ctx // dec_seq, heads_a, key_a, val_a)
            col_q, col_kv = (na + 2 * nva) // hd_b, (2 * na) // hd_b
            ob, kc, vc = _attn(proj16, proj32, attn_qn_g[e], attn_kn_g[e], None, e, None, batch, seq, 0,
                               heads_b, kv_b, hd_b, col_q, col_kv)
            (ob,) = _attn(proj16, proj32, attn_qn_g[e], attn_kn_g[e],
                          (cache_attn_k, cache_attn_v, cos_b, sin_b), e, ob,
                          dec_batch, dec_seq, n_ctx // dec_seq, heads_b, kv_b, hd_b, col_q, col_kv)
            new_k = kc.reshape(batch, 1, seq, kv_b, hd_b)
            new_v = vc.reshape(batch, 1, seq, kv_b, hd_b)
            new_sh = s_ctx[:, None]
            w_out = even_w_out[e].astype(BF16)
            mixes = [oa, ob]
            w_parts = [w_out[:heads_a * val_a], w_out[heads_a * val_a:]]
        else:
            o = l // 2
            w_in = odd_w_in[o].astype(BF16)
            proj = _in_proj(x, norm1_g[l], mod, w_in, t, n_ctx_tiles, tiles_per_req, w_in.shape[1] // 2)
            y, s_ctx = _retention(proj, log_gamma[o], ret_gn_g[o], None, o, None, batch, seq, 0,
                                  heads_c, key_c, val_c)
            y, _ = _retention(proj, log_gamma[o], ret_gn_g[o], (state_ret, cos_c, sin_c), o, y,
                              dec_batch, dec_seq, n_ctx // dec_seq, heads_c, key_c, val_c)
            new_sr = s_ctx[:, None]
            mixes = [y]
            w_parts = [odd_w_out[o].astype(BF16)]

        w_router = jnp.concatenate([moe_w_group[l], moe_w_expert[l], jnp.zeros((d, pad), F32)], axis=1)
        w_router_hi = w_router.astype(BF16)
        w_router = jnp.concatenate([w_router_hi, (w_router - w_router_hi.astype(F32)).astype(BF16)], axis=1)
        b_router = jnp.concatenate([moe_b_group[l], moe_b_expert[l], jnp.zeros((pad,), F32)]).reshape(1, 128)
        xn, h2, meta, ew, cnt = _out_proj(mixes, w_parts, x, mod, norm2_g[l], w_router, b_router,
                                          n_ctx_tiles, tiles_per_req, n_groups, n_experts)
        counts = cnt[0, n_groups:n_groups + n_experts].astype(jnp.int32)
        tile_expert, n_used, ends, pos1, pos2, n_tiles = _dispatch_plan(meta, counts, n_experts, EXPERT_TILE)
        src_tok = _row_source(ends, pos1, pos2, n_tiles * EXPERT_TILE, EXPERT_TILE)
        y_sorted = _moe_experts(h2, src_tok, tile_expert, n_used, moe_w_gate, moe_w_up, moe_w_down, l, EXPERT_TILE)
        x = _moe_combine(y_sorted, pos1, pos2, xn, ew, mod, final_norm_g, n_ctx_tiles, tiles_per_req,
                         final_norm=(l == depth - 1))

    y_prompt = x[0].reshape(batch, seq, d)
    y_sample = x[1].reshape(dec_batch, dec_seq, d)
    return (y_prompt, y_sample, new_k, new_v, new_sh, new_sr)
```

```python
import functools

import numpy as np
import jax
import jax.numpy as jnp
from jax import lax
from jax.experimental import pallas as pl
from jax.experimental.pallas import tpu as pltpu

F32 = jnp.float32
BF16 = jnp.bfloat16
EPS = 1e-6
ROPE_BASE = 10000.0
GRID_W = 64
HGRN_BLOCK = 128
HGRN_DIAG = 8
RET_CHUNK = 256
TOKEN_TILE = 256
INPROJ_TILE = 512
EXPERT_TILE = 256
MOE_GATHER_AHEAD = 2
ROW_SOURCE_CHUNK = 512
V7X_VMEM_LIMIT_BYTES = 56 * 1024 * 1024
HIGHEST = lax.Precision.HIGHEST
LOG2_E = 1.4426950408889634

_NT = (((1,), (1,)), ((), ()))


def _cparams(*sem):
    return pltpu.CompilerParams(dimension_semantics=sem, vmem_limit_bytes=V7X_VMEM_LIMIT_BYTES)


def _rms(x, g):
    return x * lax.rsqrt(jnp.mean(x * x, axis=-1, keepdims=True) + EPS) * g


def _silu(x):
    return x * jax.nn.sigmoid(x)


def _bdot(a, b):
    return jnp.dot(a.astype(BF16), b.astype(BF16), preferred_element_type=F32)


def _bdot_nt(a, b):
    return lax.dot_general(a.astype(BF16), b.astype(BF16), _NT, preferred_element_type=F32)


def _mod_kernel(c_ref, w_ref, b_ref, o_ref):
    a = _silu(c_ref[...])
    o_ref[0] = jnp.dot(a, w_ref[0], precision=HIGHEST, preferred_element_type=F32) + b_ref[0]


def _ada_mod(cvecs, mod_w, mod_b):
    depth, d, n = mod_w.shape
    tn = n // 4
    rows = cvecs.shape[0]
    return pl.pallas_call(
        _mod_kernel,
        grid=(depth, n // tn),
        in_specs=[pl.BlockSpec((rows, d), lambda l, j: (0, 0)),
                  pl.BlockSpec((1, d, tn), lambda l, j: (l, 0, j)),
                  pl.BlockSpec((1, 1, tn), lambda l, j: (l, 0, j))],
        out_specs=pl.BlockSpec((1, rows, tn), lambda l, j: (l, 0, j)),
        out_shape=jax.ShapeDtypeStruct((depth, rows, n), F32),
        compiler_params=_cparams("arbitrary", "arbitrary"),
        name="ada_mod",
    )(cvecs, mod_w, mod_b.reshape(depth, 1, n))


def _tile_mod_row(i, n_ctx_tiles, tiles_per_req):
    return jnp.where(i < n_ctx_tiles, 0, 1 + jnp.maximum(i - n_ctx_tiles, 0) // tiles_per_req)


def _token_specs(x, tm, n_ctx_tiles, step_of):
    if not isinstance(x, tuple):
        return [pl.BlockSpec((tm, x.shape[1]), lambda *g: (step_of(*g), 0))], (x,)
    d = x[0].shape[1]
    return ([pl.BlockSpec((tm, d), lambda *g: (jnp.minimum(step_of(*g), n_ctx_tiles - 1), 0)),
             pl.BlockSpec((tm, d), lambda *g: (jnp.maximum(step_of(*g) - n_ctx_tiles, 0), 0))], x)


def _token_rows(x_refs, i, n_ctx_tiles):
    if len(x_refs) == 1:
        return x_refs[0][...]
    return jnp.where(i < n_ctx_tiles, x_refs[0][...], x_refs[1][...])


def _inproj_kernel(*refs, n_x, n_ctx_tiles, f32_cols):
    x_refs = refs[:n_x]
    g_ref, mod_ref, w_ref = refs[n_x:n_x + 3]
    out_refs = refs[n_x + 3:]
    x = _token_rows(x_refs, pl.program_id(1), n_ctx_tiles)
    h = _rms(x, g_ref[...]) * (1.0 + mod_ref[0, 1:2, :]) + mod_ref[0, 0:1, :]
    acc = jnp.dot(h.astype(BF16), w_ref[...], preferred_element_type=F32)
    if not f32_cols:
        out_refs[0][...] = acc.astype(BF16)
        return
    n = acc.shape[1]
    at16 = at32 = start = 0
    for lo, hi in tuple(f32_cols) + ((n, n),):
        if lo > start:
            out_refs[0][:, at16:at16 + lo - start] = acc[:, start:lo].astype(BF16)
            at16 += lo - start
        if hi > lo:
            out_refs[1][:, at32:at32 + hi - lo] = acc[:, lo:hi]
            at32 += hi - lo
        start = hi


def _in_proj(x, g, mod, w_bf, t, n_ctx_tiles, tiles_per_req, tn, f32_cols=()):
    d, n = w_bf.shape
    tm = INPROJ_TILE
    scale = INPROJ_TILE // TOKEN_TILE
    assert n_ctx_tiles % scale == 0 and tiles_per_req % scale == 0
    n_ctx_tiles, tiles_per_req = n_ctx_tiles // scale, tiles_per_req // scale
    mrow = functools.partial(_tile_mod_row, n_ctx_tiles=n_ctx_tiles, tiles_per_req=tiles_per_req)
    x_specs, x_args = _token_specs(x, tm, n_ctx_tiles, lambda j, i: i)
    if not f32_cols:
        out_specs = pl.BlockSpec((tm, tn), lambda j, i: (i, j))
        out_shape = jax.ShapeDtypeStruct((t, n), BF16)
    else:
        assert tn == n
        n32 = sum(hi - lo for lo, hi in f32_cols)
        out_specs = [pl.BlockSpec((tm, n - n32), lambda j, i: (i, 0)),
                     pl.BlockSpec((tm, n32), lambda j, i: (i, 0))]
        out_shape = [jax.ShapeDtypeStruct((t, n - n32), BF16), jax.ShapeDtypeStruct((t, n32), F32)]
    return pl.pallas_call(
        functools.partial(_inproj_kernel, n_x=len(x_args), n_ctx_tiles=n_ctx_tiles, f32_cols=tuple(f32_cols)),
        grid=(n // tn, t // tm),
        in_specs=x_specs + [pl.BlockSpec((1, d), lambda j, i: (0, 0)),
                            pl.BlockSpec((1, 6, d), lambda j, i: (mrow(i), 0, 0)),
                            pl.BlockSpec((d, tn), lambda j, i: (0, j))],
        out_specs=out_specs,
        out_shape=out_shape,
        compiler_params=_cparams("arbitrary", "arbitrary"),
        name="in_proj",
    )(*x_args, g.reshape(1, d), mod, w_bf)


def _hgrn_decay(z, lb, reverse):
    nb = HGRN_BLOCK
    row = lax.broadcasted_iota(jnp.int32, (nb, nb), 0)
    col = lax.broadcasted_iota(jnp.int32, (nb, nb), 1)
    tri = jnp.where((col >= row) if reverse else (col <= row), 1.0, 0.0).astype(BF16)
    logf = jnp.log(lb + (1.0 - lb) * jax.nn.sigmoid(z))
    k = (1.0 - lb) * jax.nn.sigmoid(-z)
    l2 = logf * LOG2_E
    hi = l2.astype(BF16)
    lo = (l2 - hi.astype(F32)).astype(BF16)
    c = jnp.dot(tri, hi, preferred_element_type=F32) + jnp.dot(tri, lo, preferred_element_type=F32)
    return c, k


def _hgrn_block(q, v, st_t, reverse, c_ref, k_ref, ecat):
    nb = HGRN_BLOCK
    row = lax.broadcasted_iota(jnp.int32, (nb, nb), 0)
    col = lax.broadcasted_iota(jnp.int32, (nb, nb), 1)
    c = c_ref[...]
    k = k_ref[...]
    edge = 0 if reverse else nb - 1
    c_edge = c_ref[edge:edge + 1, :]

    o = _bdot_nt(q * jnp.exp2(c), st_t)
    kd = k * jnp.exp2(c_edge - c)
    st_new = st_t * jnp.exp2(c_edge) + _bdot(v.T, kd)

    a = jnp.zeros((nb, nb), F32)
    m = nb // 2
    while m >= HGRN_DIAG:
        pieces = []
        for grp in range(nb // (2 * m)):
            idx = grp * 2 * m + (m if reverse else m - 1)
            pieces.append(jnp.broadcast_to(c_ref[idx:idx + 1, :], (2 * m, nb)))
        bnd = pieces[0] if len(pieces) == 1 else jnp.concatenate(pieces, axis=0)
        e = jnp.exp2(-jnp.abs(c - bnd))
        same = (row & ~(2 * m - 1)) == (col & ~(2 * m - 1))
        r_hi = (row & m) != 0
        c_hi = (col & m) != 0
        z = (jnp.where(r_hi != reverse, q, k) * e).astype(BF16)
        am = lax.dot_general(z, z, _NT, preferred_element_type=F32)
        if reverse:
            mask = same & jnp.logical_not(r_hi) & c_hi
        else:
            mask = same & r_hi & jnp.logical_not(c_hi)
        a = jnp.where(mask, am, a)
        m //= 2

    dg = HGRN_DIAG
    rowi = lax.broadcasted_iota(jnp.int32, (dg, nb), 0)
    cols = []
    for s in range(dg):
        keep = (rowi <= s) if reverse else (rowi >= s)
        tiles = []
        for j in range(nb // dg):
            r = j * dg + s
            qj = q[j * dg:(j + 1) * dg, :]
            cj = c[j * dg:(j + 1) * dg, :]
            cs = jnp.broadcast_to(c_ref[r:r + 1, :], (dg, nb))
            ks = jnp.broadcast_to(k_ref[r:r + 1, :], (dg, nb))
            tiles.append(qj * ks * jnp.exp2(jnp.where(keep, cj - cs, -jnp.inf)))
        cols.append(jnp.concatenate(tiles, axis=0).astype(BF16))
    p = jnp.concatenate(cols, axis=1)
    red = jnp.dot(p, ecat, preferred_element_type=F32)
    a = jnp.where((row >> 3) == (col >> 3), red, a)
    o = o + _bdot(a, v)
    return o, st_new


def _hgrn_kernel(*refs, nblk, has_state, qscale):
    qa_ref, ia_ref, zf_ref, zb_ref, ga_ref, lb_ref, ng_ref, ecat_ref = refs[:8]
    s0_ref = refs[8] if has_state else None
    o_ref, sfin_ref, oacc_ref, c_scr, k_scr = refs[-5:]
    nb = HGRN_BLOCK

    def gates(direction, blk, slot):
        z_ref = zb_ref if direction == 1 else zf_ref
        r0 = blk * nb if isinstance(blk, int) else pl.multiple_of(blk * nb, nb)
        c, k = _hgrn_decay(z_ref[pl.ds(r0, nb), :], lb_ref[direction:direction + 1, :], direction == 1)
        c_scr[direction, slot] = c
        k_scr[direction, slot] = k

    def block(direction, blk, slot, st_t):
        r0 = blk * nb if isinstance(blk, int) else pl.multiple_of(blk * nb, nb)
        q = _silu(qa_ref[pl.ds(r0, nb), :].astype(F32)) * qscale
        v = ia_ref[pl.ds(r0, nb), :].astype(F32)
        o, st_new = _hgrn_block(q, v, st_t, direction == 1, c_scr.at[direction, slot],
                                k_scr.at[direction, slot], ecat_ref[...])
        oacc_ref[direction, pl.ds(r0, nb), :] = o
        return st_new

    gates(0, 0, 0)
    gates(1, nblk - 1, 0)

    def body(i, carry):
        st_f, st_b = carry
        slot = lax.rem(i, 2)
        st_f = block(0, i, slot, st_f)
        st_b = block(1, nblk - 1 - i, slot, st_b)
        gates(0, i + 1, 1 - slot)
        gates(1, nblk - 2 - i, 1 - slot)
        return st_f, st_b

    if has_state:
        st0 = (s0_ref[0].T, s0_ref[1].T)
    else:
        zero = jnp.zeros((ia_ref.shape[1], qa_ref.shape[1]), F32)
        st0 = (zero, zero)
    st_f, st_b = lax.fori_loop(0, nblk - 1, body, st0)
    last_slot = (nblk - 1) % 2
    st_f = block(0, nblk - 1, last_slot, st_f)
    st_b = block(1, 0, last_slot, st_b)
    sfin_ref[0] = st_f.T
    sfin_ref[1] = st_b.T

    o = oacc_ref[0] + oacc_ref[1]
    o_ref[...] = (_rms(o, ng_ref[0]) * _silu(ga_ref[...].astype(F32))).astype(o_ref.dtype)


def _hgrn(proj16, proj32, lb, norm_g, s0, layer_idx, out_prev, n_seq, seq, row_blk0, heads, key_d, val_d):
    t = proj16.shape[0]
    has_state = s0 is not None
    hk = heads * key_d // 128
    off_qa, off_ia, off_ga = 0, hk, 2 * hk
    off_zf, off_zb = 0, hk

    def col(off):
        return pl.BlockSpec((seq, 128), lambda b, h: (row_blk0 + b, off + h))

    lane = np.arange(key_d)[None, :]
    srow = np.repeat(np.arange(HGRN_DIAG), key_d)[:, None]
    ecat = jnp.asarray((lane % HGRN_DIAG == srow).astype(np.float32), dtype=BF16)
    in_specs = [col(off_qa), col(off_ia), col(off_zf), col(off_zb), col(off_ga),
                pl.BlockSpec((2, key_d), lambda b, h: (0, h)),
                pl.BlockSpec((1, 1, val_d), lambda b, h: (h, 0, 0)),
                pl.BlockSpec((HGRN_DIAG * key_d, key_d), lambda b, h: (0, 0))]
    args = [proj16, proj16, proj32, proj32, proj16, lb, norm_g.reshape(heads, 1, val_d), ecat]
    if has_state:
        in_specs.append(pl.BlockSpec((None, None, 2, None, key_d, val_d),
                                     lambda b, h: (b, layer_idx, 0, h, 0, 0)))
        args.append(s0)
    aliases = {}
    if out_prev is not None:
        in_specs.append(pl.BlockSpec(memory_space=pl.ANY))
        args.append(out_prev)
        aliases = {len(args) - 1: 0}

    kern = functools.partial(_hgrn_kernel, nblk=seq // HGRN_BLOCK, has_state=has_state,
                             qscale=float(key_d) ** -0.5)
    return pl.pallas_call(
        kern,
        grid=(n_seq, heads),
        in_specs=in_specs,
        out_specs=[pl.BlockSpec((seq, val_d), lambda b, h: (row_blk0 + b, h)),
                   pl.BlockSpec((None, 2, None, key_d, val_d), lambda b, h: (b, 0, h, 0, 0))],
        out_shape=[jax.ShapeDtypeStruct((t, heads * val_d), BF16),
                   jax.ShapeDtypeStruct((n_seq, 2, heads, key_d, val_d), F32)],
        scratch_shapes=[pltpu.VMEM((2, seq, val_d), F32),
                        pltpu.VMEM((2, 2, HGRN_BLOCK, key_d), F32),
                        pltpu.VMEM((2, 2, HGRN_BLOCK, key_d), F32)],
        input_output_aliases=aliases,
        compiler_params=_cparams("arbitrary", "arbitrary"),
        name="hgrn2",
    )(*args)


def _rope(x, cos, sin_signed):
    return x * cos + pltpu.roll(x, x.shape[-1] // 2, 1) * sin_signed


def _attn_kernel(*refs, has_ctx, grp, qblk, scale):
    if has_ctx:
        (q_ref, k_ref, v_ref, qn_ref, kn_ref, ck_ref, cv_ref, cos_ref, sin_ref, _prev, o_ref) = refs
    else:
        (q_ref, k_ref, v_ref, qn_ref, kn_ref, o_ref, ko_ref, vo_ref) = refs
    seq, hd = k_ref.shape
    k = _rms(k_ref[...], kn_ref[...])
    v = v_ref[...]
    if has_ctx:
        k = _rope(k, cos_ref[...], sin_ref[...])
        ck = ck_ref[...].astype(BF16)
        cv = cv_ref[...].astype(BF16)
    else:
        head = pl.program_id(1)
        ko_ref[:, pl.ds(head, 1), :] = k[:, None, :]
        vo_ref[:, pl.ds(head, 1), :] = v[:, None, :]
    kb = k.astype(BF16)
    vb = v.astype(BF16)

    def body(i, carry):
        r0 = pl.multiple_of(i * qblk, qblk)
        for g in range(grp):
            q = _rms(q_ref[pl.ds(r0, qblk), g * hd:(g + 1) * hd].astype(F32), qn_ref[...])
            if has_ctx:
                q = _rope(q, cos_ref[pl.ds(r0, qblk), :], sin_ref[pl.ds(r0, qblk), :])
            qb = q.astype(BF16)
            s2 = lax.dot_general(qb, kb, _NT, preferred_element_type=F32) * scale
            m = jnp.max(s2, axis=-1, keepdims=True)
            if has_ctx:
                s1 = lax.dot_general(qb, ck, _NT, preferred_element_type=F32) * scale
                m = jnp.maximum(m, jnp.max(s1, axis=-1, keepdims=True))
                p1 = jnp.exp(s1 - m)
            p2 = jnp.exp(s2 - m)
            den = jnp.sum(p2, axis=-1, keepdims=True)
            acc = jnp.dot(p2.astype(BF16), vb, preferred_element_type=F32)
            if has_ctx:
                den = den + jnp.sum(p1, axis=-1, keepdims=True)
                acc = acc + jnp.dot(p1.astype(BF16), cv, preferred_element_type=F32)
            o_ref[pl.ds(r0, qblk), g * hd:(g + 1) * hd] = (acc / den).astype(o_ref.dtype)
        return carry

    lax.fori_loop(0, seq // qblk, body, 0)


def _attn(proj16, proj32, qn_g, kn_g, ctx, layer_idx, out_prev, n_seq, seq, row_blk0, heads, kv_heads, hd,
          col_q, col_kv):
    t = proj16.shape[0]
    grp = heads // kv_heads
    has_ctx = ctx is not None
    cq = col_q // grp
    ck0 = col_kv
    cv0 = ck0 + kv_heads
    in_specs = [pl.BlockSpec((seq, grp * hd), lambda b, h: (row_blk0 + b, cq + h)),
                pl.BlockSpec((seq, hd), lambda b, h: (row_blk0 + b, ck0 + h)),
                pl.BlockSpec((seq, hd), lambda b, h: (row_blk0 + b, cv0 + h)),
                pl.BlockSpec((1, hd), lambda b, h: (0, 0)),
                pl.BlockSpec((1, hd), lambda b, h: (0, 0))]
    args = [proj16, proj32, proj32, qn_g.reshape(1, hd), kn_g.reshape(1, hd)]
    out_specs = [pl.BlockSpec((seq, grp * hd), lambda b, h: (row_blk0 + b, h))]
    out_shape = [jax.ShapeDtypeStruct((t, heads * hd), BF16)]
    aliases = {}
    if has_ctx:
        cache_k, cache_v, cos, sin_signed = ctx
        nreq, nlay, past = cache_k.shape[:3]
        cache_k = cache_k.reshape(nreq, nlay, past, kv_heads * hd)
        cache_v = cache_v.reshape(nreq, nlay, past, kv_heads * hd)
        spec_c = pl.BlockSpec((None, None, past, hd), lambda b, h: (b, layer_idx, 0, h))
        in_specs += [spec_c, spec_c,
                     pl.BlockSpec((seq, hd), lambda b, h: (0, 0)),
                     pl.BlockSpec((seq, hd), lambda b, h: (0, 0)),
                     pl.BlockSpec(memory_space=pl.ANY)]
        args += [cache_k, cache_v, cos, sin_signed, out_prev]
        aliases = {len(args) - 1: 0}
    else:
        out_specs += [pl.BlockSpec((None, seq, kv_heads, hd), lambda b, h: (b, 0, 0, 0))] * 2
        out_shape += [jax.ShapeDtypeStruct((n_seq, seq, kv_heads, hd), F32)] * 2
    return pl.pallas_call(
        functools.partial(_attn_kernel, has_ctx=has_ctx, grp=grp, qblk=min(seq, 256), scale=float(hd) ** -0.5),
        grid=(n_seq, kv_heads),
        in_specs=in_specs,
        out_specs=out_specs,
        out_shape=out_shape,
        input_output_aliases=aliases,
        compiler_params=_cparams("arbitrary", "arbitrary"),
        name="gqa",
    )(*args)


def _ret_kernel(*refs, nchunk, has_ctx, kscale):
    if has_ctx:
        (lg_ref, q_ref, k_ref, v_ref, g_ref, gn_ref, s0_ref, cos_ref, sin_ref, _prev,
         o_ref, sfin_ref, sb_ref) = refs
    else:
        (lg_ref, q_ref, k_ref, v_ref, g_ref, gn_ref, o_ref, sfin_ref, sb_ref) = refs
    cs = RET_CHUNK
    kd_dim = q_ref.shape[1]
    h = pl.program_id(1)
    lgf = lg_ref[0, h]
    lgb = lg_ref[1, h]
    row = lax.broadcasted_iota(jnp.int32, (cs, cs), 0)
    col = lax.broadcasted_iota(jnp.int32, (cs, cs), 1)
    rel = (row - col).astype(F32)
    dmat = (jnp.exp(jnp.where(rel >= 0, rel * lgf, -jnp.inf))
            + jnp.exp(jnp.where(rel <= 0, -rel * lgb, -jnp.inf)))
    j = lax.broadcasted_iota(jnp.int32, (cs, 1), 0).astype(F32)
    one = jnp.ones((1, 1), F32)
    dec_f = jnp.exp(one * (cs * lgf))
    dec_b = jnp.exp(one * (cs * lgb))

    def load_qk(ref, r0, mul):
        x = ref[pl.ds(r0, cs), :].astype(F32) * mul
        if has_ctx:
            half = kd_dim // 2
            x1, x2 = x[:, :half], x[:, half:]
            cos = cos_ref[pl.ds(r0, cs), :]
            sin = sin_ref[pl.ds(r0, cs), :]
            x = jnp.concatenate([x1 * cos - x2 * sin, x2 * cos + x1 * sin], axis=1)
        return x

    def bwd_body(i, st):
        n = nchunk - 1 - i
        r0 = pl.multiple_of(n * cs, cs)
        sb_ref[n] = st
        k = load_qk(k_ref, r0, kscale)
        v = v_ref[pl.ds(r0, cs), :]
        return st * dec_b + _bdot((k * jnp.exp(j * lgb)).T, v)

    st_b0 = s0_ref[1] if has_ctx else jnp.zeros(sfin_ref.shape[1:], F32)
    st_b = lax.fori_loop(0, nchunk, bwd_body, st_b0)
    sfin_ref[1] = st_b

    def fwd_body(n, st):
        r0 = pl.multiple_of(n * cs, cs)
        q = load_qk(q_ref, r0, 1.0)
        k = load_qk(k_ref, r0, kscale)
        v = v_ref[pl.ds(r0, cs), :]
        a = _bdot_nt(q, k) * dmat
        o = _bdot(a, v)
        if has_ctx or nchunk > 1:
            o = o + _bdot(q * jnp.exp((j + 1.0) * lgf), st)
            o = o + _bdot(q * jnp.exp((cs - j) * lgb), sb_ref[n])
        mu = jnp.mean(o, axis=-1, keepdims=True)
        var = jnp.mean(jnp.square(o - mu), axis=-1, keepdims=True)
        y = (o - mu) * lax.rsqrt(var + EPS) * gn_ref[0]
        o_ref[pl.ds(r0, cs), :] = (_silu(g_ref[pl.ds(r0, cs), :].astype(F32)) * y).astype(o_ref.dtype)
        return st * dec_f + _bdot((k * jnp.exp((cs - 1.0 - j) * lgf)).T, v)

    st_f0 = s0_ref[0] if has_ctx else jnp.zeros(sfin_ref.shape[1:], F32)
    sfin_ref[0] = lax.fori_loop(0, nchunk, fwd_body, st_f0)


def _retention(proj, log_gamma, gn_g, ctx, layer_idx, out_prev, n_seq, seq, row_blk0, heads, key_d, val_d):
    t = proj.shape[0]
    has_ctx = ctx is not None
    nq = heads * key_d
    cq, ck, cv, cg = 0, nq // key_d, 2 * nq // val_d, (2 * nq + heads * val_d) // val_d
    in_specs = [pl.BlockSpec(memory_space=pltpu.SMEM),
                pl.BlockSpec((seq, key_d), lambda b, h: (row_blk0 + b, cq + h)),
                pl.BlockSpec((seq, key_d), lambda b, h: (row_blk0 + b, ck + h)),
                pl.BlockSpec((seq, val_d), lambda b, h: (row_blk0 + b, cv + h)),
                pl.BlockSpec((seq, val_d), lambda b, h: (row_blk0 + b, cg + h)),
                pl.BlockSpec((1, 1, val_d), lambda b, h: (h, 0, 0))]
    args = [log_gamma, proj, proj, proj, proj, gn_g.reshape(heads, 1, val_d)]
    aliases = {}
    if has_ctx:
        s0, cos, sin = ctx
        in_specs += [pl.BlockSpec((None, None, 2, None, key_d, val_d), lambda b, h: (b, layer_idx, 0, h, 0, 0)),
                     pl.BlockSpec((seq, key_d // 2), lambda b, h: (0, 0)),
                     pl.BlockSpec((seq, key_d // 2), lambda b, h: (0, 0)),
                     pl.BlockSpec(memory_space=pl.ANY)]
        args += [s0, cos, sin, out_prev]
        aliases = {len(args) - 1: 0}
    nchunk = seq // RET_CHUNK
    return pl.pallas_call(
        functools.partial(_ret_kernel, nchunk=nchunk, has_ctx=has_ctx, kscale=float(key_d) ** -0.5),
        grid=(n_seq, heads),
        in_specs=in_specs,
        out_specs=[pl.BlockSpec((seq, val_d), lambda b, h: (row_blk0 + b, h)),
                   pl.BlockSpec((None, 2, None, key_d, val_d), lambda b, h: (b, 0, h, 0, 0))],
        out_shape=[jax.ShapeDtypeStruct((t, heads * val_d), BF16),
                   jax.ShapeDtypeStruct((n_seq, 2, heads, key_d, val_d), F32)],
        scratch_shapes=[pltpu.VMEM((nchunk, key_d, val_d), F32)],
        input_output_aliases=aliases,
        compiler_params=_cparams("arbitrary", "arbitrary"),
        name="retention",
    )(*args)


def _outproj_kernel(*refs, n_in, n_x, n_ctx_tiles, n_groups, n_experts):
    mix_refs = refs[:n_in]
    w_refs = refs[n_in:2 * n_in]
    x_refs = refs[2 * n_in:2 * n_in + n_x]
    (mod_ref, g2_ref, wr_ref, br_ref, xn_ref, h2_ref, meta_ref, ew_ref, cnt_out_ref,
     cnt_ref) = refs[2 * n_in + n_x:]
    acc = _bdot(mix_refs[0][...], w_refs[0][...])
    for mref, wref in zip(mix_refs[1:], w_refs[1:]):
        acc = acc + _bdot(mref[...], wref[...])
    xn = _token_rows(x_refs, pl.program_id(0), n_ctx_tiles) + mod_ref[0, 2:3, :] * acc
    xn_ref[...] = xn
    h2 = _rms(xn, g2_ref[...]) * (1.0 + mod_ref[0, 4:5, :]) + mod_ref[0, 3:4, :]
    h2_ref[...] = h2.astype(BF16).reshape(h2_ref.shape)
    h_hi = h2.astype(BF16)
    h_lo = (h2 - h_hi.astype(F32)).astype(BF16)
    nl = br_ref.shape[1]
    hh = jnp.dot(h_hi, wr_ref[...], preferred_element_type=F32)
    lg = hh[:, :nl] + hh[:, nl:] + jnp.dot(h_lo, wr_ref[:, :nl], preferred_element_type=F32) + br_ref[...]

    per_grp = n_experts // n_groups
    lane = lax.broadcasted_iota(jnp.int32, lg.shape, 1).astype(F32)
    big = float(lg.shape[1])
    ninf = -jnp.inf
    gl = jnp.where(lane < n_groups, lg, ninf)
    gmax = jnp.max(gl, axis=-1, keepdims=True)
    p_grp = 1.0 / jnp.sum(jnp.exp(gl - gmax), axis=-1, keepdims=True)
    gidx = jnp.min(jnp.where(gl == gmax, lane, big), axis=-1, keepdims=True)
    in_grp = ((lane >= n_groups) & (lane < n_groups + n_experts)
              & (jnp.floor((lane - n_groups) / per_grp) == gidx))
    el = jnp.where(in_grp, lg, ninf)
    emax = jnp.max(el, axis=-1, keepdims=True)
    esum = jnp.sum(jnp.exp(el - emax), axis=-1, keepdims=True)
    i1 = jnp.min(jnp.where(el == emax, lane, big), axis=-1, keepdims=True)
    el2 = jnp.where(lane == i1, ninf, el)
    emax2 = jnp.max(el2, axis=-1, keepdims=True)
    i2 = jnp.min(jnp.where(el2 == emax2, lane, big), axis=-1, keepdims=True)
    p1 = 1.0 / esum
    p2 = jnp.exp(emax2 - emax) / esum
    w1 = p1 / (p1 + p2) * p_grp
    w2 = p2 / (p1 + p2) * p_grp
    ew_ref[...] = jnp.where(lane == 0.0, w1, jnp.where(lane == 1.0, w2, 0.0))

    step = pl.program_id(0)

    @pl.when(step == 0)
    def _():
        cnt_ref[...] = jnp.zeros_like(cnt_ref)

    tm = lg.shape[0]
    hit1 = lane == i1
    hit2 = lane == i2
    sel = jnp.where(hit1 | hit2, 1.0, 0.0)
    rr = lax.broadcasted_iota(jnp.int32, (tm, tm), 0)
    cc = lax.broadcasted_iota(jnp.int32, (tm, tm), 1)
    before = jnp.where(cc < rr, 1.0, 0.0).astype(BF16)
    prior = jnp.dot(before, sel.astype(BF16), preferred_element_type=F32) + cnt_ref[...]
    rank1 = jnp.sum(jnp.where(hit1, prior, 0.0), axis=-1, keepdims=True)
    rank2 = jnp.sum(jnp.where(hit2, prior, 0.0), axis=-1, keepdims=True)
    cnt_ref[...] += jnp.sum(sel, axis=0, keepdims=True)
    cnt_out_ref[...] = cnt_ref[...]
    meta = jnp.where(lane == 0.0, i1 - n_groups,
                     jnp.where(lane == 1.0, i2 - n_groups,
                               jnp.where(lane == 2.0, rank1, jnp.where(lane == 3.0, rank2, 0.0))))
    meta_ref[0] = meta.T[0:8, :].astype(jnp.int32)


def _out_proj(mixes, w_parts, x, mod, g2, w_router, b_router, n_ctx_tiles, tiles_per_req, n_groups, n_experts):
    t = mixes[0].shape[0]
    d = w_parts[0].shape[1]
    tm = TOKEN_TILE
    n_in = len(mixes)
    mrow = functools.partial(_tile_mod_row, n_ctx_tiles=n_ctx_tiles, tiles_per_req=tiles_per_req)
    x_specs, x_args = _token_specs(x, tm, n_ctx_tiles, lambda i: i)
    in_specs = ([pl.BlockSpec((tm, m.shape[1]), lambda i: (i, 0)) for m in mixes]
                + [pl.BlockSpec(w.shape, lambda i: (0, 0)) for w in w_parts]
                + x_specs
                + [pl.BlockSpec((1, 6, d), lambda i: (mrow(i), 0, 0)),
                   pl.BlockSpec((1, d), lambda i: (0, 0)),
                   pl.BlockSpec(w_router.shape, lambda i: (0, 0)),
                   pl.BlockSpec((1, 128), lambda i: (0, 0))])
    return pl.pallas_call(
        functools.partial(_outproj_kernel, n_in=n_in, n_x=len(x_args), n_ctx_tiles=n_ctx_tiles,
                          n_groups=n_groups, n_experts=n_experts),
        grid=(t // tm,),
        in_specs=in_specs,
        out_specs=[pl.BlockSpec((tm, d), lambda i: (i, 0)),
                   pl.BlockSpec((tm, d // 128, 128), lambda i: (i, 0, 0)),
                   pl.BlockSpec((1, 8, tm), lambda i: (i, 0, 0)),
                   pl.BlockSpec((tm, 128), lambda i: (i, 0)),
                   pl.BlockSpec((1, 128), lambda i: (0, 0))],
        out_shape=[jax.ShapeDtypeStruct((t, d), F32),
                   jax.ShapeDtypeStruct((t, d // 128, 128), BF16),
                   jax.ShapeDtypeStruct((t // tm, 8, tm), jnp.int32),
                   jax.ShapeDtypeStruct((t, 128), F32),
                   jax.ShapeDtypeStruct((1, 128), F32)],
        scratch_shapes=[pltpu.VMEM((1, 128), F32)],
        compiler_params=_cparams("arbitrary"),
        name="out_proj_router",
    )(*mixes, *w_parts, *x_args, mod, g2.reshape(1, d), w_router, b_router)


def _dispatch_plan(meta, counts, n_experts, tile):
    t = meta.shape[0] * meta.shape[2]
    e1, e2, rank1, rank2 = (meta[:, k, :].reshape(t) for k in range(4))
    padded = ((counts + tile - 1) // tile) * tile
    ends = jnp.cumsum(padded)
    starts = ends - padded
    experts = jnp.arange(n_experts, dtype=jnp.int32)[None, :]
    pos = [jnp.sum(jnp.where(e[:, None] == experts, starts[None, :], 0), axis=1) + r
           for e, r in ((e1, rank1), (e2, rank2))]
    n_tiles = (2 * t) // tile + n_experts
    n_used = (ends[-1] // tile).astype(jnp.int32)
    tile_start = jnp.arange(n_tiles, dtype=jnp.int32) * tile
    probe = jnp.minimum(tile_start, ends[-1] - 1)
    tile_expert = jnp.sum((ends[None, :] <= probe[:, None]).astype(jnp.int32), axis=1)
    return tile_expert, n_used.reshape(1), ends.astype(jnp.int32), pos[0], pos[1], n_tiles


def _row_source_kernel(ends_ref, p1_ref, p2_ref, o_ref, *, n_tok, n_rows, tile, n_experts):
    def clear8(j, c):
        base = j * 8
        if n_rows <= 3 * n_tok:
            base = jnp.where(base >= 2 * n_tok, base - 2 * n_tok, jnp.where(base >= n_tok, base - n_tok, base))
        else:
            base = lax.rem(base, n_tok)
        for u in range(8):
            o_ref[j * 8 + u] = base + u
        return c

    def per_expert(e, c):
        first = lax.shift_right_logical(jnp.maximum(ends_ref[e] - tile, 0), 3)
        lax.fori_loop(first, first + tile // 8, clear8, 0)
        return c

    g = pl.program_id(0)

    @pl.when(g == 0)
    def _():
        lax.fori_loop(0, n_experts, per_expert, 0)
        lax.fori_loop(lax.shift_right_logical(ends_ref[n_experts - 1], 3), n_rows // 8, clear8, 0)

    chunk = p1_ref.shape[2]

    def place(r, c):
        t = g * chunk + r
        o_ref[p1_ref[0, 0, r]] = t
        o_ref[p2_ref[0, 0, r]] = t
        return c

    lax.fori_loop(0, chunk, place, 0, unroll=8)


def _row_source(ends, pos1, pos2, n_rows, tile):
    n_tok = pos1.shape[0]
    chunk = ROW_SOURCE_CHUNK
    smem = pl.BlockSpec(memory_space=pltpu.SMEM)
    pos_spec = pl.BlockSpec((1, 1, chunk), lambda g: (g, 0, 0), memory_space=pltpu.SMEM)
    return pl.pallas_call(
        functools.partial(_row_source_kernel, n_tok=n_tok, n_rows=n_rows, tile=tile, n_experts=ends.shape[0]),
        grid=(n_tok // chunk,),
        in_specs=[smem, pos_spec, pos_spec],
        out_specs=smem,
        out_shape=jax.ShapeDtypeStruct((n_rows,), jnp.int32),
        compiler_params=_cparams("arbitrary"),
        name="moe_row_source",
    )(ends, pos1.reshape(n_tok // chunk, 1, chunk), pos2.reshape(n_tok // chunk, 1, chunk))


def _moe_kernel(te_ref, nu_ref, *refs, tm, ahead):
    tok_refs = refs[:ahead + 1]
    h_hbm, wg_ref, wu_ref, wd_ref, o_ref, wg_bf, wu_bf, wd_bf, buf, sem = refs[ahead + 1:]
    i = pl.program_id(0)
    nslot = ahead + 1
    slot = lax.rem(i, nslot)
    n_used = nu_ref[0]

    def issue(tile, tok_ref):
        s = lax.rem(tile, nslot)

        def body(j, c):
            for u in range(2):
                r = 2 * j + u
                pltpu.make_async_copy(h_hbm.at[tok_ref[0, 0, r]], buf.at[s, r], sem.at[s]).start(priority=u)
            return c

        lax.fori_loop(0, tm // 2, body, 0, unroll=4)

    @pl.when(i == 0)
    def _():
        for j in range(ahead):
            @pl.when(j < n_used)
            def _():
                issue(jnp.int32(j), tok_refs[j])

    new_expert = jnp.logical_or(i == 0, te_ref[i] != te_ref[jnp.maximum(i - 1, 0)])

    @pl.when(new_expert)
    def _():
        wg_bf[...] = wg_ref[...].astype(BF16)
        wu_bf[...] = wu_ref[...].astype(BF16)
        wd_bf[...] = wd_ref[...].astype(BF16)

    def compute(issue_next):
        s_next = lax.rem(i + ahead, nslot)
        quarter = tm // 4

        def start_rows(part):
            if issue_next:
                for r in range(part * quarter, (part + 1) * quarter):
                    pltpu.make_async_copy(h_hbm.at[tok_refs[ahead][0, 0, r]], buf.at[s_next, r],
                                          sem.at[s_next]).start(priority=r % 2)

        pltpu.make_async_copy(h_hbm.at[pl.ds(0, tm)], buf.at[slot], sem.at[slot]).wait()
        start_rows(0)
        x = buf[slot].reshape(tm, -1)
        g = jnp.dot(x, wg_bf[...], preferred_element_type=F32)
        start_rows(1)
        u = jnp.dot(x, wu_bf[...], preferred_element_type=F32)
        start_rows(2)
        act = _silu(g) * u
        y = jnp.dot(act.astype(BF16), wd_bf[...], preferred_element_type=F32)
        start_rows(3)
        o_ref[...] = y.astype(BF16).reshape(o_ref.shape)

    @pl.when(i + ahead < n_used)
    def _():
        compute(True)

    @pl.when(jnp.logical_and(i + ahead >= n_used, i < n_used))
    def _():
        compute(False)

    @pl.when(i >= n_used)
    def _():
        o_ref[...] = jnp.zeros_like(o_ref)


def _moe_experts(h3, src_tok, tile_expert, n_used, w_gate, w_up, w_down, layer_idx, tm):
    t, sub, lanes = h3.shape
    d = sub * lanes
    n_tiles = src_tok.shape[0] // tm
    f = w_gate.shape[-1]
    ahead = MOE_GATHER_AHEAD
    tok = src_tok.reshape(n_tiles, 1, tm)

    def tok_spec(k):
        return pl.BlockSpec((1, 1, tm), lambda i, te, nu: (jnp.minimum(i + k, n_tiles - 1), 0, 0),
                            memory_space=pltpu.SMEM)

    return pl.pallas_call(
        functools.partial(_moe_kernel, tm=tm, ahead=ahead),
        grid_spec=pltpu.PrefetchScalarGridSpec(
            num_scalar_prefetch=2,
            grid=(n_tiles,),
            in_specs=[tok_spec(k) for k in range(ahead + 1)] + [
                      pl.BlockSpec(memory_space=pl.ANY),
                      pl.BlockSpec((None, None, d, f), lambda i, te, nu: (layer_idx, te[i], 0, 0)),
                      pl.BlockSpec((None, None, d, f), lambda i, te, nu: (layer_idx, te[i], 0, 0)),
                      pl.BlockSpec((None, None, f, d), lambda i, te, nu: (layer_idx, te[i], 0, 0))],
            out_specs=pl.BlockSpec((tm, sub, lanes), lambda i, te, nu: (i, 0, 0)),
            scratch_shapes=[pltpu.VMEM((d, f), BF16), pltpu.VMEM((d, f), BF16), pltpu.VMEM((f, d), BF16),
                            pltpu.VMEM((ahead + 1, tm, sub, lanes), BF16),
                            pltpu.SemaphoreType.DMA((ahead + 1,))]),
        out_shape=jax.ShapeDtypeStruct((n_tiles * tm, sub, lanes), BF16),
        compiler_params=_cparams("arbitrary"),
        name="moe_experts",
    )(tile_expert, n_used, *([tok] * (ahead + 1)), h3, w_gate, w_up, w_down)


def _combine_kernel(p1_ref, p2_ref, p1n_ref, p2n_ref, y_hbm, x_ref, ew_ref, mod_ref, gf_ref, *rest,
                    tm, n_steps, n_ctx_tiles, final_norm):
    out_refs, (buf, sem) = rest[:-2], rest[-2:]
    i = pl.program_id(0)
    slot = lax.rem(i, 2)

    def issue(s, pa_ref, pb_ref):
        def body(r, c):
            pltpu.make_async_copy(y_hbm.at[pa_ref[0, 0, r]], buf.at[s, 0, r], sem.at[s, 0]).start(priority=0)
            pltpu.make_async_copy(y_hbm.at[pb_ref[0, 0, r]], buf.at[s, 1, r], sem.at[s, 1]).start(priority=1)
            return c

        lax.fori_loop(0, tm, body, 0, unroll=8)

    @pl.when(i == 0)
    def _():
        issue(0, p1_ref, p2_ref)

    @pl.when(i + 1 < n_steps)
    def _():
        issue(1 - slot, p1n_ref, p2n_ref)

    for k in range(2):
        pltpu.make_async_copy(y_hbm.at[pl.ds(0, tm)], buf.at[slot, k], sem.at[slot, k]).wait()
    ew = ew_ref[...]
    y = (ew[:, 0:1] * buf[slot, 0].reshape(tm, -1).astype(F32)
         + ew[:, 1:2] * buf[slot, 1].reshape(tm, -1).astype(F32))
    x = x_ref[...] + mod_ref[0, 5:6, :] * y
    if not final_norm:
        out_refs[0][...] = x
    else:
        x = _rms(x, gf_ref[...])
        ctx_ref, lat_ref = out_refs

        @pl.when(i < n_ctx_tiles)
        def _():
            ctx_ref[...] = x

        @pl.when(i >= n_ctx_tiles)
        def _():
            lat_ref[...] = x


def _moe_combine(y_sorted, pos1, pos2, x, ew, mod, final_g, n_ctx_tiles, tiles_per_req, final_norm):
    t, d = x.shape
    tm = TOKEN_TILE
    n_steps = t // tm
    mrow = functools.partial(_tile_mod_row, n_ctx_tiles=n_ctx_tiles, tiles_per_req=tiles_per_req)
    smem_rows = pl.BlockSpec((1, 1, tm), lambda i: (i, 0, 0), memory_space=pltpu.SMEM)
    smem_next = pl.BlockSpec((1, 1, tm), lambda i: (jnp.minimum(i + 1, n_steps - 1), 0, 0),
                             memory_space=pltpu.SMEM)
    p1 = pos1.reshape(n_steps, 1, tm)
    p2 = pos2.reshape(n_steps, 1, tm)
    if final_norm:
        out_specs = [pl.BlockSpec((tm, d), lambda i: (jnp.minimum(i, n_ctx_tiles - 1), 0)),
                     pl.BlockSpec((tm, d), lambda i: (jnp.maximum(i - n_ctx_tiles, 0), 0))]
        out_shape = [jax.ShapeDtypeStruct((n_ctx_tiles * tm, d), F32),
                     jax.ShapeDtypeStruct((t - n_ctx_tiles * tm, d), F32)]
    else:
        out_specs = pl.BlockSpec((tm, d), lambda i: (i, 0))
        out_shape = jax.ShapeDtypeStruct((t, d), F32)
    return pl.pallas_call(
        functools.partial(_combine_kernel, tm=tm, n_steps=n_steps, n_ctx_tiles=n_ctx_tiles,
                          final_norm=final_norm),
        grid=(n_steps,),
        in_specs=[smem_rows, smem_rows, smem_next, smem_next,
                  pl.BlockSpec(memory_space=pl.ANY),
                  pl.BlockSpec((tm, d), lambda i: (i, 0)),
                  pl.BlockSpec((tm, 128), lambda i: (i, 0)),
                  pl.BlockSpec((1, 6, d), lambda i: (mrow(i), 0, 0)),
                  pl.BlockSpec((1, d), lambda i: (0, 0))],
        out_specs=out_specs,
        out_shape=out_shape,
        scratch_shapes=[pltpu.VMEM((2, 2, tm) + y_sorted.shape[1:], y_sorted.dtype),
                        pltpu.SemaphoreType.DMA((2, 2))],
        compiler_params=_cparams("arbitrary"),
        name="moe_combine",
    )(p1, p2, p1, p2, y_sorted, x, ew, mod, final_g.reshape(1, d))


def _axial_angles(seq, head_dim):
    rows = seq // GRID_W
    row = jnp.repeat(jnp.arange(rows), GRID_W).astype(F32)
    col = jnp.tile(jnp.arange(GRID_W), rows).astype(F32)
    n_freq = head_dim // 4
    inv = ROPE_BASE ** (-jnp.arange(n_freq, dtype=F32) / n_freq)
    return jnp.concatenate([row[:, None] * inv, col[:, None] * inv], axis=-1)


def kernel(x_prompt, x_sample, cache_attn_k, cache_attn_v, state_hgrn, state_ret, c, c_ctx, mod_w, mod_b, norm1_g, norm2_g, even_w_in, even_w_out, hgrn_lb_logits, hgrn_norm_g, attn_qn_g, attn_kn_g, odd_w_in, odd_w_out, ret_decay_logit, ret_gn_g, moe_w_group, moe_b_group, moe_w_expert, moe_b_expert, moe_w_gate, moe_w_up, moe_w_down, final_norm_g):
    batch, seq, d = x_prompt.shape
    dec_batch, dec_seq, _ = x_sample.shape
    depth = mod_w.shape[0]
    heads_a, val_a = hgrn_norm_g.shape[1:]
    key_a = hgrn_lb_logits.shape[2] // heads_a
    hd_b = attn_qn_g.shape[1]
    kv_b = cache_attn_k.shape[3]
    heads_b = (even_w_out.shape[1] - heads_a * val_a) // hd_b
    heads_c, val_c = ret_gn_g.shape[1:]
    key_c = state_ret.shape[4]
    n_experts = moe_w_expert.shape[2]
    n_groups = moe_w_group.shape[2]
    tm = TOKEN_TILE
    n_ctx = batch * seq
    n_ctx_tiles = n_ctx // tm
    tiles_per_req = dec_seq // tm
    assert seq % tm == 0 and dec_seq % tm == 0 and n_ctx % dec_seq == 0
    assert key_a == 128 and val_a == 128 and hd_b == 128 and n_groups + n_experts <= 128
    assert d % 128 == 0 and seq % RET_CHUNK == 0 and dec_seq % RET_CHUNK == 0 and dec_seq % GRID_W == 0
    n_tok = n_ctx + dec_batch * dec_seq
    assert (2 * n_tok) % EXPERT_TILE == 0 and n_tok % ROW_SOURCE_CHUNK == 0

    x = (x_prompt.reshape(n_ctx, d), x_sample.reshape(dec_batch * dec_seq, d))
    t = n_ctx + dec_batch * dec_seq

    n_vec = 1 + dec_batch
    rows = -(-n_vec // 8) * 8
    cvecs = jnp.zeros((rows, d), F32).at[0].set(c_ctx).at[1:n_vec].set(c)
    mods = _ada_mod(cvecs, mod_w, mod_b).reshape(depth, rows, 6, d)

    lower_bounds = jnp.cumsum(jax.nn.softmax(hgrn_lb_logits.astype(F32), axis=0), axis=0)
    log_gamma = jax.nn.log_sigmoid(ret_decay_logit.astype(F32))

    ang_b = _axial_angles(dec_seq, hd_b)
    cos_b = jnp.concatenate([jnp.cos(ang_b), jnp.cos(ang_b)], axis=-1)
    sin_b = jnp.concatenate([-jnp.sin(ang_b), jnp.sin(ang_b)], axis=-1)
    ang_c = _axial_angles(dec_seq, key_c)
    cos_c, sin_c = jnp.cos(ang_c), jnp.sin(ang_c)

    pad = 128 - n_groups - n_experts
    new_k = new_v = new_sh = new_sr = None
    assert depth == 2, "one HGRN2/GQA layer followed by one retention layer"
    for l in range(depth):
        mod = mods[l]
        if l % 2 == 0:
            e = l // 2
            na, nva, nqb, nkv = heads_a * key_a, heads_a * val_a, heads_b * hd_b, kv_b * hd_b
            o_zf, o_kb = na + nva, 3 * na + 2 * nva + nqb
            w_in = even_w_in[e].astype(BF16)
            proj16, proj32 = _in_proj(x, norm1_g[l], mod, w_in, t, n_ctx_tiles, tiles_per_req, w_in.shape[1],
                                      f32_cols=((o_zf, o_zf + 2 * na), (o_kb, o_kb + 2 * nkv)))
            lb = lower_bounds[l]
            oa, s_ctx = _hgrn(proj16, proj32, lb, hgrn_norm_g[e], None, e, None, batch, seq, 0,
                              heads_a, key_a, val_a)
            oa, _ = _hgrn(proj16, proj32, lb, hgrn_norm_g[e], state_hgrn, e, oa, dec_batch, dec_seq,
                          n_ctx // dec_seq, heads_a, key_a, val_a)
            col_q, col_kv = (na + 2 * nva) // hd_b, (2 * na) // hd_b
            ob, kc, vc = _attn(proj16, proj32, attn_qn_g[e], attn_kn_g[e], None, e, None, batch, seq, 0,
                               heads_b, kv_b, hd_b, col_q, col_kv)
            (ob,) = _attn(proj16, proj32, attn_qn_g[e], attn_kn_g[e],
                          (cache_attn_k, cache_attn_v, cos_b, sin_b), e, ob,
                          dec_batch, dec_seq, n_ctx // dec_seq, heads_b, kv_b, hd_b, col_q, col_kv)
            new_k = kc.reshape(batch, 1, seq, kv_b, hd_b)
            new_v = vc.reshape(batch, 1, seq, kv_b, hd_b)
            new_sh = s_ctx[:, None]
            w_out = even_w_out[e].astype(BF16)
            mixes = [oa, ob]
            w_parts = [w_out[:heads_a * val_a], w_out[heads_a * val_a:]]
        else:
            o = l // 2
            w_in = odd_w_in[o].astype(BF16)
            proj = _in_proj(x, norm1_g[l], mod, w_in, t, n_ctx_tiles, tiles_per_req, w_in.shape[1] // 2)
            y, s_ctx = _retention(proj, log_gamma[o], ret_gn_g[o], None, o, None, batch, seq, 0,
                                  heads_c, key_c, val_c)
            y, _ = _retention(proj, log_gamma[o], ret_gn_g[o], (state_ret, cos_c, sin_c), o, y,
                              dec_batch, dec_seq, n_ctx // dec_seq, heads_c, key_c, val_c)
            new_sr = s_ctx[:, None]
            mixes = [y]
            w_parts = [odd_w_out[o].astype(BF16)]

        w_router = jnp.concatenate([moe_w_group[l], moe_w_expert[l], jnp.zeros((d, pad), F32)], axis=1)
        w_router_hi = w_router.astype(BF16)
        w_router = jnp.concatenate([w_router_hi, (w_router - w_router_hi.astype(F32)).astype(BF16)], axis=1)
        b_router = jnp.concatenate([moe_b_group[l], moe_b_expert[l], jnp.zeros((pad,), F32)]).reshape(1, 128)
        xn, h2, meta, ew, cnt = _out_proj(mixes, w_parts, x, mod, norm2_g[l], w_router, b_router,
                                          n_ctx_tiles, tiles_per_req, n_groups, n_experts)
        counts = cnt[0, n_groups:n_groups + n_experts].astype(jnp.int32)
        tile_expert, n_used, ends, pos1, pos2, n_tiles = _dispatch_plan(meta, counts, n_experts, EXPERT_TILE)
        src_tok = _row_source(ends, pos1, pos2, n_tiles * EXPERT_TILE, EXPERT_TILE)
        y_sorted = _moe_experts(h2, src_tok, tile_expert, n_used, moe_w_gate, moe_w_up, moe_w_down, l, EXPERT_TILE)
        x = _moe_combine(y_sorted, pos1, pos2, xn, ew, mod, final_norm_g, n_ctx_tiles, tiles_per_req,
                         final_norm=(l == depth - 1))

    y_prompt = x[0].reshape(batch, seq, d)
    y_sample = x[1].reshape(dec_batch, dec_seq, d)
    return (y_prompt, y_sample, new_k, new_v, new_sh, new_sr)
```

```python
import functools

import numpy as np
import jax
import jax.numpy as jnp
from jax import lax
from jax.experimental import pallas as pl
from jax.experimental.pallas import tpu as pltpu

F32 = jnp.float32
BF16 = jnp.bfloat16
EPS = 1e-6
ROPE_BASE = 10000.0
GRID_W = 64
HGRN_BLOCK = 128
HGRN_DIAG = 8
RET_CHUNK = 256
TOKEN_TILE = 256
INPROJ_TILE = 512
EXPERT_TILE = 256
MOE_GATHER_AHEAD = 2
ROW_SOURCE_CHUNK = 512
V7X_VMEM_LIMIT_BYTES = 56 * 1024 * 1024
HIGHEST = lax.Precision.HIGHEST
LOG2_E = 1.4426950408889634

_NT = (((1,), (1,)), ((), ()))


def _cparams(*sem):
    return pltpu.CompilerParams(dimension_semantics=sem, vmem_limit_bytes=V7X_VMEM_LIMIT_BYTES)


def _rms(x, g):
    return x * lax.rsqrt(jnp.mean(x * x, axis=-1, keepdims=True) + EPS) * g


def _silu(x):
    return x * jax.nn.sigmoid(x)


def _bdot(a, b):
    return jnp.dot(a.astype(BF16), b.astype(BF16), preferred_element_type=F32)


def _bdot_nt(a, b):
    return lax.dot_general(a.astype(BF16), b.astype(BF16), _NT, preferred_element_type=F32)


def _mod_kernel(c_ref, w_ref, b_ref, o_ref):
    a = _silu(c_ref[...])
    o_ref[0] = jnp.dot(a, w_ref[0], precision=HIGHEST, preferred_element_type=F32) + b_ref[0]


def _ada_mod(cvecs, mod_w, mod_b):
    depth, d, n = mod_w.shape
    tn = n // 4
    rows = cvecs.shape[0]
    return pl.pallas_call(
        _mod_kernel,
        grid=(depth, n // tn),
        in_specs=[pl.BlockSpec((rows, d), lambda l, j: (0, 0)),
                  pl.BlockSpec((1, d, tn), lambda l, j: (l, 0, j)),
                  pl.BlockSpec((1, 1, tn), lambda l, j: (l, 0, j))],
        out_specs=pl.BlockSpec((1, rows, tn), lambda l, j: (l, 0, j)),
        out_shape=jax.ShapeDtypeStruct((depth, rows, n), F32),
        compiler_params=_cparams("arbitrary", "arbitrary"),
        name="ada_mod",
    )(cvecs, mod_w, mod_b.reshape(depth, 1, n))


def _tile_mod_row(i, n_ctx_tiles, tiles_per_req):
    return jnp.where(i < n_ctx_tiles, 0, 1 + jnp.maximum(i - n_ctx_tiles, 0) // tiles_per_req)


def _token_specs(x, tm, n_ctx_tiles, step_of):
    if not isinstance(x, tuple):
        return [pl.BlockSpec((tm, x.shape[1]), lambda *g: (step_of(*g), 0))], (x,)
    d = x[0].shape[1]
    return ([pl.BlockSpec((tm, d), lambda *g: (jnp.minimum(step_of(*g), n_ctx_tiles - 1), 0)),
             pl.BlockSpec((tm, d), lambda *g: (jnp.maximum(step_of(*g) - n_ctx_tiles, 0), 0))], x)


def _token_rows(x_refs, i, n_ctx_tiles):
    if len(x_refs) == 1:
        return x_refs[0][...]
    return jnp.where(i < n_ctx_tiles, x_refs[0][...], x_refs[1][...])


def _inproj_kernel(*refs, n_x, n_ctx_tiles, f32_cols, cast_w):
    x_refs = refs[:n_x]
    g_ref, mod_ref, w_ref = refs[n_x:n_x + 3]
    out_refs = refs[n_x + 3:]
    if cast_w:
        out_refs, w_bf = out_refs[:-1], out_refs[-1]

        @pl.when(pl.program_id(1) == 0)
        def _():
            w_bf[...] = w_ref[...].astype(BF16)

        w_ref = w_bf
    x = _token_rows(x_refs, pl.program_id(1), n_ctx_tiles)
    h = _rms(x, g_ref[...]) * (1.0 + mod_ref[0, 1:2, :]) + mod_ref[0, 0:1, :]
    acc = jnp.dot(h.astype(BF16), w_ref[...], preferred_element_type=F32)
    if not f32_cols:
        out_refs[0][...] = acc.astype(BF16)
        return
    n = acc.shape[1]
    at16 = at32 = start = 0
    for lo, hi in tuple(f32_cols) + ((n, n),):
        if lo > start:
            out_refs[0][:, at16:at16 + lo - start] = acc[:, start:lo].astype(BF16)
            at16 += lo - start
        if hi > lo:
            out_refs[1][:, at32:at32 + hi - lo] = acc[:, lo:hi]
            at32 += hi - lo
        start = hi


def _in_proj(x, g, mod, w_bf, t, n_ctx_tiles, tiles_per_req, tn, f32_cols=()):
    d, n = w_bf.shape
    cast_w = w_bf.dtype != BF16
    tm = INPROJ_TILE
    scale = INPROJ_TILE // TOKEN_TILE
    assert n_ctx_tiles % scale == 0 and tiles_per_req % scale == 0
    n_ctx_tiles, tiles_per_req = n_ctx_tiles // scale, tiles_per_req // scale
    mrow = functools.partial(_tile_mod_row, n_ctx_tiles=n_ctx_tiles, tiles_per_req=tiles_per_req)
    x_specs, x_args = _token_specs(x, tm, n_ctx_tiles, lambda j, i: i)
    if not f32_cols:
        out_specs = pl.BlockSpec((tm, tn), lambda j, i: (i, j))
        out_shape = jax.ShapeDtypeStruct((t, n), BF16)
    else:
        assert tn == n
        n32 = sum(hi - lo for lo, hi in f32_cols)
        out_specs = [pl.BlockSpec((tm, n - n32), lambda j, i: (i, 0)),
                     pl.BlockSpec((tm, n32), lambda j, i: (i, 0))]
        out_shape = [jax.ShapeDtypeStruct((t, n - n32), BF16), jax.ShapeDtypeStruct((t, n32), F32)]
    return pl.pallas_call(
        functools.partial(_inproj_kernel, n_x=len(x_args), n_ctx_tiles=n_ctx_tiles, f32_cols=tuple(f32_cols),
                          cast_w=cast_w),
        grid=(n // tn, t // tm),
        in_specs=x_specs + [pl.BlockSpec((1, d), lambda j, i: (0, 0)),
                            pl.BlockSpec((1, 6, d), lambda j, i: (mrow(i), 0, 0)),
                            pl.BlockSpec((d, tn), lambda j, i: (0, j))],
        out_specs=out_specs,
        out_shape=out_shape,
        scratch_shapes=[pltpu.VMEM((d, tn), BF16)] if cast_w else [],
        compiler_params=_cparams("arbitrary", "arbitrary"),
        name="in_proj",
    )(*x_args, g.reshape(1, d), mod, w_bf)


def _hgrn_decay(z, lb, reverse):
    nb = HGRN_BLOCK
    row = lax.broadcasted_iota(jnp.int32, (nb, nb), 0)
    col = lax.broadcasted_iota(jnp.int32, (nb, nb), 1)
    tri = jnp.where((col >= row) if reverse else (col <= row), 1.0, 0.0).astype(BF16)
    logf = jnp.log(lb + (1.0 - lb) * jax.nn.sigmoid(z))
    k = (1.0 - lb) * jax.nn.sigmoid(-z)
    l2 = logf * LOG2_E
    hi = l2.astype(BF16)
    lo = (l2 - hi.astype(F32)).astype(BF16)
    c = jnp.dot(tri, hi, preferred_element_type=F32) + jnp.dot(tri, lo, preferred_element_type=F32)
    return c, k


def _hgrn_block(q, v, st_t, reverse, c_ref, k_ref, ecat):
    nb = HGRN_BLOCK
    row = lax.broadcasted_iota(jnp.int32, (nb, nb), 0)
    col = lax.broadcasted_iota(jnp.int32, (nb, nb), 1)
    c = c_ref[...]
    k = k_ref[...]
    edge = 0 if reverse else nb - 1
    c_edge = c_ref[edge:edge + 1, :]

    o = _bdot_nt(q * jnp.exp2(c), st_t)
    kd = k * jnp.exp2(c_edge - c)
    st_new = st_t * jnp.exp2(c_edge) + _bdot(v.T, kd)

    a = jnp.zeros((nb, nb), F32)
    m = nb // 2
    while m >= HGRN_DIAG:
        pieces = []
        for grp in range(nb // (2 * m)):
            idx = grp * 2 * m + (m if reverse else m - 1)
            pieces.append(jnp.broadcast_to(c_ref[idx:idx + 1, :], (2 * m, nb)))
        bnd = pieces[0] if len(pieces) == 1 else jnp.concatenate(pieces, axis=0)
        e = jnp.exp2(-jnp.abs(c - bnd))
        same = (row & ~(2 * m - 1)) == (col & ~(2 * m - 1))
        r_hi = (row & m) != 0
        c_hi = (col & m) != 0
        z = (jnp.where(r_hi != reverse, q, k) * e).astype(BF16)
        am = lax.dot_general(z, z, _NT, preferred_element_type=F32)
        if reverse:
            mask = same & jnp.logical_not(r_hi) & c_hi
        else:
            mask = same & r_hi & jnp.logical_not(c_hi)
        a = jnp.where(mask, am, a)
        m //= 2

    dg = HGRN_DIAG
    rowi = lax.broadcasted_iota(jnp.int32, (dg, nb), 0)
    cols = []
    for s in range(dg):
        keep = (rowi <= s) if reverse else (rowi >= s)
        tiles = []
        for j in range(nb // dg):
            r = j * dg + s
            qj = q[j * dg:(j + 1) * dg, :]
            cj = c[j * dg:(j + 1) * dg, :]
            cs = jnp.broadcast_to(c_ref[r:r + 1, :], (dg, nb))
            ks = jnp.broadcast_to(k_ref[r:r + 1, :], (dg, nb))
            tiles.append(qj * ks * jnp.exp2(jnp.where(keep, cj - cs, -jnp.inf)))
        cols.append(jnp.concatenate(tiles, axis=0).astype(BF16))
    p = jnp.concatenate(cols, axis=1)
    red = jnp.dot(p, ecat, preferred_element_type=F32)
    a = jnp.where((row >> 3) == (col >> 3), red, a)
    o = o + _bdot(a, v)
    return o, st_new


def _hgrn_kernel(*refs, nblk, has_state, qscale):
    qa_ref, ia_ref, zf_ref, zb_ref, ga_ref, lb_ref, ng_ref, ecat_ref = refs[:8]
    s0_ref = refs[8] if has_state else None
    o_ref, sfin_ref, oacc_ref, c_scr, k_scr = refs[-5:]
    nb = HGRN_BLOCK

    def gates(direction, blk, slot):
        z_ref = zb_ref if direction == 1 else zf_ref
        r0 = blk * nb if isinstance(blk, int) else pl.multiple_of(blk * nb, nb)
        c, k = _hgrn_decay(z_ref[pl.ds(r0, nb), :], lb_ref[direction:direction + 1, :], direction == 1)
        c_scr[direction, slot] = c
        k_scr[direction, slot] = k

    def block(direction, blk, slot, st_t):
        r0 = blk * nb if isinstance(blk, int) else pl.multiple_of(blk * nb, nb)
        q = _silu(qa_ref[pl.ds(r0, nb), :].astype(F32)) * qscale
        v = ia_ref[pl.ds(r0, nb), :].astype(F32)
        o, st_new = _hgrn_block(q, v, st_t, direction == 1, c_scr.at[direction, slot],
                                k_scr.at[direction, slot], ecat_ref[...])
        oacc_ref[direction, pl.ds(r0, nb), :] = o
        return st_new

    gates(0, 0, 0)
    gates(1, nblk - 1, 0)

    def body(i, carry):
        st_f, st_b = carry
        slot = lax.rem(i, 2)
        st_f = block(0, i, slot, st_f)
        st_b = block(1, nblk - 1 - i, slot, st_b)
        gates(0, i + 1, 1 - slot)
        gates(1, nblk - 2 - i, 1 - slot)
        return st_f, st_b

    if has_state:
        st0 = (s0_ref[0].T, s0_ref[1].T)
    else:
        zero = jnp.zeros((ia_ref.shape[1], qa_ref.shape[1]), F32)
        st0 = (zero, zero)
    st_f, st_b = lax.fori_loop(0, nblk - 1, body, st0)
    last_slot = (nblk - 1) % 2
    st_f = block(0, nblk - 1, last_slot, st_f)
    st_b = block(1, 0, last_slot, st_b)
    sfin_ref[0] = st_f.T
    sfin_ref[1] = st_b.T

    o = oacc_ref[0] + oacc_ref[1]
    o_ref[...] = (_rms(o, ng_ref[0]) * _silu(ga_ref[...].astype(F32))).astype(o_ref.dtype)


def _hgrn(proj16, proj32, lb, norm_g, s0, layer_idx, out_prev, n_seq, seq, row_blk0, heads, key_d, val_d):
    t = proj16.shape[0]
    has_state = s0 is not None
    hk = heads * key_d // 128
    off_qa, off_ia, off_ga = 0, hk, 2 * hk
    off_zf, off_zb = 0, hk

    def col(off):
        return pl.BlockSpec((seq, 128), lambda b, h: (row_blk0 + b, off + h))

    lane = np.arange(key_d)[None, :]
    srow = np.repeat(np.arange(HGRN_DIAG), key_d)[:, None]
    ecat = jnp.asarray((lane % HGRN_DIAG == srow).astype(np.float32), dtype=BF16)
    in_specs = [col(off_qa), col(off_ia), col(off_zf), col(off_zb), col(off_ga),
                pl.BlockSpec((2, key_d), lambda b, h: (0, h)),
                pl.BlockSpec((1, 1, val_d), lambda b, h: (h, 0, 0)),
                pl.BlockSpec((HGRN_DIAG * key_d, key_d), lambda b, h: (0, 0))]
    args = [proj16, proj16, proj32, proj32, proj16, lb, norm_g.reshape(heads, 1, val_d), ecat]
    if has_state:
        in_specs.append(pl.BlockSpec((None, None, 2, None, key_d, val_d),
                                     lambda b, h: (b, layer_idx, 0, h, 0, 0)))
        args.append(s0)
    aliases = {}
    if out_prev is not None:
        in_specs.append(pl.BlockSpec(memory_space=pl.ANY))
        args.append(out_prev)
        aliases = {len(args) - 1: 0}

    kern = functools.partial(_hgrn_kernel, nblk=seq // HGRN_BLOCK, has_state=has_state,
                             qscale=float(key_d) ** -0.5)
    return pl.pallas_call(
        kern,
        grid=(n_seq, heads),
        in_specs=in_specs,
        out_specs=[pl.BlockSpec((seq, val_d), lambda b, h: (row_blk0 + b, h)),
                   pl.BlockSpec((None, 2, None, key_d, val_d), lambda b, h: (b, 0, h, 0, 0))],
        out_shape=[jax.ShapeDtypeStruct((t, heads * val_d), BF16),
                   jax.ShapeDtypeStruct((n_seq, 2, heads, key_d, val_d), F32)],
        scratch_shapes=[pltpu.VMEM((2, seq, val_d), F32),
                        pltpu.VMEM((2, 2, HGRN_BLOCK, key_d), F32),
                        pltpu.VMEM((2, 2, HGRN_BLOCK, key_d), F32)],
        input_output_aliases=aliases,
        compiler_params=_cparams("arbitrary", "arbitrary"),
        name="hgrn2",
    )(*args)


def _rope(x, cos, sin_signed):
    return x * cos + pltpu.roll(x, x.shape[-1] // 2, 1) * sin_signed


def _attn_kernel(*refs, has_ctx, grp, qblk, scale):
    if has_ctx:
        (q_ref, k_ref, v_ref, qn_ref, kn_ref, ck_ref, cv_ref, cos_ref, sin_ref, _prev, o_ref) = refs
    else:
        (q_ref, k_ref, v_ref, qn_ref, kn_ref, o_ref, ko_ref, vo_ref) = refs
    seq, hd = k_ref.shape
    k = _rms(k_ref[...], kn_ref[...])
    v = v_ref[...]
    if has_ctx:
        k = _rope(k, cos_ref[...], sin_ref[...])
        ck = ck_ref[...].astype(BF16)
        cv = cv_ref[...].astype(BF16)
    else:
        head = pl.program_id(1)
        ko_ref[:, pl.ds(head, 1), :] = k[:, None, :]
        vo_ref[:, pl.ds(head, 1), :] = v[:, None, :]
    kb = k.astype(BF16)
    vb = v.astype(BF16)

    def body(i, carry):
        r0 = pl.multiple_of(i * qblk, qblk)
        for g in range(grp):
            q = _rms(q_ref[pl.ds(r0, qblk), g * hd:(g + 1) * hd].astype(F32), qn_ref[...])
            if has_ctx:
                q = _rope(q, cos_ref[pl.ds(r0, qblk), :], sin_ref[pl.ds(r0, qblk), :])
            qb = q.astype(BF16)
            s2 = lax.dot_general(qb, kb, _NT, preferred_element_type=F32) * scale
            m = jnp.max(s2, axis=-1, keepdims=True)
            if has_ctx:
                s1 = lax.dot_general(qb, ck, _NT, preferred_element_type=F32) * scale
                m = jnp.maximum(m, jnp.max(s1, axis=-1, keepdims=True))
                p1 = jnp.exp(s1 - m)
            p2 = jnp.exp(s2 - m)
            den = jnp.sum(p2, axis=-1, keepdims=True)
            acc = jnp.dot(p2.astype(BF16), vb, preferred_element_type=F32)
            if has_ctx:
                den = den + jnp.sum(p1, axis=-1, keepdims=True)
                acc = acc + jnp.dot(p1.astype(BF16), cv, preferred_element_type=F32)
            o_ref[pl.ds(r0, qblk), g * hd:(g + 1) * hd] = (acc / den).astype(o_ref.dtype)
        return carry

    lax.fori_loop(0, seq // qblk, body, 0)


def _attn(proj16, proj32, qn_g, kn_g, ctx, layer_idx, out_prev, n_seq, seq, row_blk0, heads, kv_heads, hd,
          col_q, col_kv):
    t = proj16.shape[0]
    grp = heads // kv_heads
    has_ctx = ctx is not None
    cq = col_q // grp
    ck0 = col_kv
    cv0 = ck0 + kv_heads
    in_specs = [pl.BlockSpec((seq, grp * hd), lambda b, h: (row_blk0 + b, cq + h)),
                pl.BlockSpec((seq, hd), lambda b, h: (row_blk0 + b, ck0 + h)),
                pl.BlockSpec((seq, hd), lambda b, h: (row_blk0 + b, cv0 + h)),
                pl.BlockSpec((1, hd), lambda b, h: (0, 0)),
                pl.BlockSpec((1, hd), lambda b, h: (0, 0))]
    args = [proj16, proj32, proj32, qn_g.reshape(1, hd), kn_g.reshape(1, hd)]
    out_specs = [pl.BlockSpec((seq, grp * hd), lambda b, h: (row_blk0 + b, h))]
    out_shape = [jax.ShapeDtypeStruct((t, heads * hd), BF16)]
    aliases = {}
    if has_ctx:
        cache_k, cache_v, cos, sin_signed = ctx
        nreq, nlay, past = cache_k.shape[:3]
        cache_k = cache_k.reshape(nreq, nlay, past, kv_heads * hd)
        cache_v = cache_v.reshape(nreq, nlay, past, kv_heads * hd)
        spec_c = pl.BlockSpec((None, None, past, hd), lambda b, h: (b, layer_idx, 0, h))
        in_specs += [spec_c, spec_c,
                     pl.BlockSpec((seq, hd), lambda b, h: (0, 0)),
                     pl.BlockSpec((seq, hd), lambda b, h: (0, 0)),
                     pl.BlockSpec(memory_space=pl.ANY)]
        args += [cache_k, cache_v, cos, sin_signed, out_prev]
        aliases = {len(args) - 1: 0}
    else:
        out_specs += [pl.BlockSpec((None, seq, kv_heads, hd), lambda b, h: (b, 0, 0, 0))] * 2
        out_shape += [jax.ShapeDtypeStruct((n_seq, seq, kv_heads, hd), F32)] * 2
    return pl.pallas_call(
        functools.partial(_attn_kernel, has_ctx=has_ctx, grp=grp, qblk=min(seq, 256), scale=float(hd) ** -0.5),
        grid=(n_seq, kv_heads),
        in_specs=in_specs,
        out_specs=out_specs,
        out_shape=out_shape,
        input_output_aliases=aliases,
        compiler_params=_cparams("arbitrary", "arbitrary"),
        name="gqa",
    )(*args)


def _ret_kernel(*refs, nchunk, has_ctx, kscale):
    if has_ctx:
        (lg_ref, q_ref, k_ref, v_ref, g_ref, gn_ref, s0_ref, cos_ref, sin_ref, _prev,
         o_ref, sfin_ref, sb_ref) = refs
    else:
        (lg_ref, q_ref, k_ref, v_ref, g_ref, gn_ref, o_ref, sfin_ref, sb_ref) = refs
    cs = RET_CHUNK
    kd_dim = q_ref.shape[1]
    h = pl.program_id(1)
    lgf = lg_ref[0, h]
    lgb = lg_ref[1, h]
    row = lax.broadcasted_iota(jnp.int32, (cs, cs), 0)
    col = lax.broadcasted_iota(jnp.int32, (cs, cs), 1)
    rel = (row - col).astype(F32)
    dmat = (jnp.exp(jnp.where(rel >= 0, rel * lgf, -jnp.inf))
            + jnp.exp(jnp.where(rel <= 0, -rel * lgb, -jnp.inf)))
    j = lax.broadcasted_iota(jnp.int32, (cs, 1), 0).astype(F32)
    one = jnp.ones((1, 1), F32)
    dec_f = jnp.exp(one * (cs * lgf))
    dec_b = jnp.exp(one * (cs * lgb))

    def load_qk(ref, r0, mul):
        x = ref[pl.ds(r0, cs), :].astype(F32) * mul
        if has_ctx:
            half = kd_dim // 2
            x1, x2 = x[:, :half], x[:, half:]
            cos = cos_ref[pl.ds(r0, cs), :]
            sin = sin_ref[pl.ds(r0, cs), :]
            x = jnp.concatenate([x1 * cos - x2 * sin, x2 * cos + x1 * sin], axis=1)
        return x

    def bwd_body(i, st):
        n = nchunk - 1 - i
        r0 = pl.multiple_of(n * cs, cs)
        sb_ref[n] = st
        k = load_qk(k_ref, r0, kscale)
        v = v_ref[pl.ds(r0, cs), :]
        return st * dec_b + _bdot((k * jnp.exp(j * lgb)).T, v)

    st_b0 = s0_ref[1] if has_ctx else jnp.zeros(sfin_ref.shape[1:], F32)
    st_b = lax.fori_loop(0, nchunk, bwd_body, st_b0)
    sfin_ref[1] = st_b

    def fwd_body(n, st):
        r0 = pl.multiple_of(n * cs, cs)
        q = load_qk(q_ref, r0, 1.0)
        k = load_qk(k_ref, r0, kscale)
        v = v_ref[pl.ds(r0, cs), :]
        a = _bdot_nt(q, k) * dmat
        o = _bdot(a, v)
        if has_ctx or nchunk > 1:
            o = o + _bdot(q * jnp.exp((j + 1.0) * lgf), st)
            o = o + _bdot(q * jnp.exp((cs - j) * lgb), sb_ref[n])
        mu = jnp.mean(o, axis=-1, keepdims=True)
        var = jnp.mean(jnp.square(o - mu), axis=-1, keepdims=True)
        y = (o - mu) * lax.rsqrt(var + EPS) * gn_ref[0]
        o_ref[pl.ds(r0, cs), :] = (_silu(g_ref[pl.ds(r0, cs), :].astype(F32)) * y).astype(o_ref.dtype)
        return st * dec_f + _bdot((k * jnp.exp((cs - 1.0 - j) * lgf)).T, v)

    st_f0 = s0_ref[0] if has_ctx else jnp.zeros(sfin_ref.shape[1:], F32)
    sfin_ref[0] = lax.fori_loop(0, nchunk, fwd_body, st_f0)


def _retention(proj, log_gamma, gn_g, ctx, layer_idx, out_prev, n_seq, seq, row_blk0, heads, key_d, val_d):
    t = proj.shape[0]
    has_ctx = ctx is not None
    nq = heads * key_d
    cq, ck, cv, cg = 0, nq // key_d, 2 * nq // val_d, (2 * nq + heads * val_d) // val_d
    in_specs = [pl.BlockSpec(memory_space=pltpu.SMEM),
                pl.BlockSpec((seq, key_d), lambda b, h: (row_blk0 + b, cq + h)),
                pl.BlockSpec((seq, key_d), lambda b, h: (row_blk0 + b, ck + h)),
                pl.BlockSpec((seq, val_d), lambda b, h: (row_blk0 + b, cv + h)),
                pl.BlockSpec((seq, val_d), lambda b, h: (row_blk0 + b, cg + h)),
                pl.BlockSpec((1, 1, val_d), lambda b, h: (h, 0, 0))]
    args = [log_gamma, proj, proj, proj, proj, gn_g.reshape(heads, 1, val_d)]
    aliases = {}
    if has_ctx:
        s0, cos, sin = ctx
        in_specs += [pl.BlockSpec((None, None, 2, None, key_d, val_d), lambda b, h: (b, layer_idx, 0, h, 0, 0)),
                     pl.BlockSpec((seq, key_d // 2), lambda b, h: (0, 0)),
                     pl.BlockSpec((seq, key_d // 2), lambda b, h: (0, 0)),
                     pl.BlockSpec(memory_space=pl.ANY)]
        args += [s0, cos, sin, out_prev]
        aliases = {len(args) - 1: 0}
    nchunk = seq // RET_CHUNK
    return pl.pallas_call(
        functools.partial(_ret_kernel, nchunk=nchunk, has_ctx=has_ctx, kscale=float(key_d) ** -0.5),
        grid=(n_seq, heads),
        in_specs=in_specs,
        out_specs=[pl.BlockSpec((seq, val_d), lambda b, h: (row_blk0 + b, h)),
                   pl.BlockSpec((None, 2, None, key_d, val_d), lambda b, h: (b, 0, h, 0, 0))],
        out_shape=[jax.ShapeDtypeStruct((t, heads * val_d), BF16),
                   jax.ShapeDtypeStruct((n_seq, 2, heads, key_d, val_d), F32)],
        scratch_shapes=[pltpu.VMEM((nchunk, key_d, val_d), F32)],
        input_output_aliases=aliases,
        compiler_params=_cparams("arbitrary", "arbitrary"),
        name="retention",
    )(*args)


def _outproj_kernel(*refs, n_in, n_x, n_ctx_tiles, n_groups, n_experts):
    mix_refs = refs[:n_in]
    w_refs = refs[n_in:2 * n_in]
    x_refs = refs[2 * n_in:2 * n_in + n_x]
    base = 2 * n_in + n_x
    (mod_ref, g2_ref, wr_ref, br_ref, xn_ref, h2_ref, meta_ref, ew_ref, cnt_out_ref,
     cnt_ref) = refs[base:base + 10]
    wbf_refs = refs[base + 10:]

    @pl.when(pl.program_id(0) == 0)
    def _():
        for wref, wbf in zip(w_refs, wbf_refs):
            wbf[...] = wref[...].astype(BF16)

    acc = jnp.dot(mix_refs[0][...], wbf_refs[0][...], preferred_element_type=F32)
    for mref, wbf in zip(mix_refs[1:], wbf_refs[1:]):
        acc = acc + jnp.dot(mref[...], wbf[...], preferred_element_type=F32)
    xn = _token_rows(x_refs, pl.program_id(0), n_ctx_tiles) + mod_ref[0, 2:3, :] * acc
    xn_ref[...] = xn
    h2 = _rms(xn, g2_ref[...]) * (1.0 + mod_ref[0, 4:5, :]) + mod_ref[0, 3:4, :]
    h2_ref[...] = h2.astype(BF16).reshape(h2_ref.shape)
    h_hi = h2.astype(BF16)
    h_lo = (h2 - h_hi.astype(F32)).astype(BF16)
    nl = br_ref.shape[1]
    hh = jnp.dot(h_hi, wr_ref[...], preferred_element_type=F32)
    lg = hh[:, :nl] + hh[:, nl:] + jnp.dot(h_lo, wr_ref[:, :nl], preferred_element_type=F32) + br_ref[...]

    per_grp = n_experts // n_groups
    lane = lax.broadcasted_iota(jnp.int32, lg.shape, 1).astype(F32)
    big = float(lg.shape[1])
    ninf = -jnp.inf
    gl = jnp.where(lane < n_groups, lg, ninf)
    gmax = jnp.max(gl, axis=-1, keepdims=True)
    p_grp = 1.0 / jnp.sum(jnp.exp(gl - gmax), axis=-1, keepdims=True)
    gidx = jnp.min(jnp.where(gl == gmax, lane, big), axis=-1, keepdims=True)
    in_grp = ((lane >= n_groups) & (lane < n_groups + n_experts)
              & (jnp.floor((lane - n_groups) / per_grp) == gidx))
    el = jnp.where(in_grp, lg, ninf)
    emax = jnp.max(el, axis=-1, keepdims=True)
    esum = jnp.sum(jnp.exp(el - emax), axis=-1, keepdims=True)
    i1 = jnp.min(jnp.where(el == emax, lane, big), axis=-1, keepdims=True)
    el2 = jnp.where(lane == i1, ninf, el)
    emax2 = jnp.max(el2, axis=-1, keepdims=True)
    i2 = jnp.min(jnp.where(el2 == emax2, lane, big), axis=-1, keepdims=True)
    p1 = 1.0 / esum
    p2 = jnp.exp(emax2 - emax) / esum
    w1 = p1 / (p1 + p2) * p_grp
    w2 = p2 / (p1 + p2) * p_grp
    ew_ref[...] = jnp.where(lane == 0.0, w1, jnp.where(lane == 1.0, w2, 0.0))

    step = pl.program_id(0)

    @pl.when(step == 0)
    def _():
        cnt_ref[...] = jnp.zeros_like(cnt_ref)

    tm = lg.shape[0]
    hit1 = lane == i1
    hit2 = lane == i2
    sel = jnp.where(hit1 | hit2, 1.0, 0.0)
    rr = lax.broadcasted_iota(jnp.int32, (tm, tm), 0)
    cc = lax.broadcasted_iota(jnp.int32, (tm, tm), 1)
    before = jnp.where(cc < rr, 1.0, 0.0).astype(BF16)
    prior = jnp.dot(before, sel.astype(BF16), preferred_element_type=F32) + cnt_ref[...]
    rank1 = jnp.sum(jnp.where(hit1, prior, 0.0), axis=-1, keepdims=True)
    rank2 = jnp.sum(jnp.where(hit2, prior, 0.0), axis=-1, keepdims=True)
    cnt_ref[...] += jnp.sum(sel, axis=0, keepdims=True)
    cnt_out_ref[...] = cnt_ref[...]
    meta = jnp.where(lane == 0.0, i1 - n_groups,
                     jnp.where(lane == 1.0, i2 - n_groups,
                               jnp.where(lane == 2.0, rank1, jnp.where(lane == 3.0, rank2, 0.0))))
    meta_ref[0] = meta.T[0:8, :].astype(jnp.int32)


def _out_proj(mixes, w_out, x, mod, g2, w_router, b_router, n_ctx_tiles, tiles_per_req, n_groups, n_experts):
    t = mixes[0].shape[0]
    d = w_out.shape[1]
    tm = TOKEN_TILE
    n_in = len(mixes)
    width = mixes[0].shape[1]
    assert all(m.shape[1] == width and m.dtype == BF16 for m in mixes) and w_out.shape[0] == n_in * width
    mrow = functools.partial(_tile_mod_row, n_ctx_tiles=n_ctx_tiles, tiles_per_req=tiles_per_req)
    x_specs, x_args = _token_specs(x, tm, n_ctx_tiles, lambda i: i)
    in_specs = ([pl.BlockSpec((tm, width), lambda i: (i, 0)) for _ in mixes]
                + [pl.BlockSpec((width, d), lambda i, k=k: (k, 0)) for k in range(n_in)]
                + x_specs
                + [pl.BlockSpec((1, 6, d), lambda i: (mrow(i), 0, 0)),
                   pl.BlockSpec((1, d), lambda i: (0, 0)),
                   pl.BlockSpec(w_router.shape, lambda i: (0, 0)),
                   pl.BlockSpec((1, 128), lambda i: (0, 0))])
    return pl.pallas_call(
        functools.partial(_outproj_kernel, n_in=n_in, n_x=len(x_args), n_ctx_tiles=n_ctx_tiles,
                          n_groups=n_groups, n_experts=n_experts),
        grid=(t // tm,),
        in_specs=in_specs,
        out_specs=[pl.BlockSpec((tm, d), lambda i: (i, 0)),
                   pl.BlockSpec((tm, d // 128, 128), lambda i: (i, 0, 0)),
                   pl.BlockSpec((1, 8, tm), lambda i: (i, 0, 0)),
                   pl.BlockSpec((tm, 128), lambda i: (i, 0)),
                   pl.BlockSpec((1, 128), lambda i: (0, 0))],
        out_shape=[jax.ShapeDtypeStruct((t, d), F32),
                   jax.ShapeDtypeStruct((t, d // 128, 128), BF16),
                   jax.ShapeDtypeStruct((t // tm, 8, tm), jnp.int32),
                   jax.ShapeDtypeStruct((t, 128), F32),
                   jax.ShapeDtypeStruct((1, 128), F32)],
        scratch_shapes=[pltpu.VMEM((1, 128), F32)] + [pltpu.VMEM((width, d), BF16)] * n_in,
        compiler_params=_cparams("arbitrary"),
        name="out_proj_router",
    )(*mixes, *([w_out] * n_in), *x_args, mod, g2.reshape(1, d), w_router, b_router)


def _dispatch_plan(meta, counts, n_experts, tile):
    t = meta.shape[0] * meta.shape[2]
    e1, e2, rank1, rank2 = (meta[:, k, :].reshape(t) for k in range(4))
    padded = ((counts + tile - 1) // tile) * tile
    ends = jnp.cumsum(padded)
    starts = ends - padded
    experts = jnp.arange(n_experts, dtype=jnp.int32)[None, :]
    pos = [jnp.sum(jnp.where(e[:, None] == experts, starts[None, :], 0), axis=1) + r
           for e, r in ((e1, rank1), (e2, rank2))]
    n_tiles = (2 * t) // tile + n_experts
    n_used = (ends[-1] // tile).astype(jnp.int32)
    tile_start = jnp.arange(n_tiles, dtype=jnp.int32) * tile
    probe = jnp.minimum(tile_start, ends[-1] - 1)
    tile_expert = jnp.sum((ends[None, :] <= probe[:, None]).astype(jnp.int32), axis=1)
    return tile_expert, n_used.reshape(1), ends.astype(jnp.int32), pos[0], pos[1], n_tiles


def _row_source_kernel(ends_ref, p1_ref, p2_ref, o_ref, *, n_tok, n_rows, tile, n_experts):
    def clear8(j, c):
        base = j * 8
        if n_rows <= 3 * n_tok:
            base = jnp.where(base >= 2 * n_tok, base - 2 * n_tok, jnp.where(base >= n_tok, base - n_tok, base))
        else:
            base = lax.rem(base, n_tok)
        for u in range(8):
            o_ref[j * 8 + u] = base + u
        return c

    def per_expert(e, c):
        first = lax.shift_right_logical(jnp.maximum(ends_ref[e] - tile, 0), 3)
        lax.fori_loop(first, first + tile // 8, clear8, 0)
        return c

    g = pl.program_id(0)

    @pl.when(g == 0)
    def _():
        lax.fori_loop(0, n_experts, per_expert, 0)
        lax.fori_loop(lax.shift_right_logical(ends_ref[n_experts - 1], 3), n_rows // 8, clear8, 0)

    chunk = p1_ref.shape[2]

    def place(r, c):
        t = g * chunk + r
        o_ref[p1_ref[0, 0, r]] = t
        o_ref[p2_ref[0, 0, r]] = t
        return c

    lax.fori_loop(0, chunk, place, 0, unroll=8)


def _row_source(ends, pos1, pos2, n_rows, tile):
    n_tok = pos1.shape[0]
    chunk = ROW_SOURCE_CHUNK
    smem = pl.BlockSpec(memory_space=pltpu.SMEM)
    pos_spec = pl.BlockSpec((1, 1, chunk), lambda g: (g, 0, 0), memory_space=pltpu.SMEM)
    return pl.pallas_call(
        functools.partial(_row_source_kernel, n_tok=n_tok, n_rows=n_rows, tile=tile, n_experts=ends.shape[0]),
        grid=(n_tok // chunk,),
        in_specs=[smem, pos_spec, pos_spec],
        out_specs=smem,
        out_shape=jax.ShapeDtypeStruct((n_rows,), jnp.int32),
        compiler_params=_cparams("arbitrary"),
        name="moe_row_source",
    )(ends, pos1.reshape(n_tok // chunk, 1, chunk), pos2.reshape(n_tok // chunk, 1, chunk))


def _moe_kernel(te_ref, nu_ref, *refs, tm, ahead):
    tok_refs = refs[:ahead + 1]
    h_hbm, wg_ref, wu_ref, wd_ref, o_ref, wg_bf, wu_bf, wd_bf, buf, sem = refs[ahead + 1:]
    i = pl.program_id(0)
    nslot = ahead + 1
    slot = lax.rem(i, nslot)
    n_used = nu_ref[0]

    def issue(tile, tok_ref):
        s = lax.rem(tile, nslot)

        def body(j, c):
            for u in range(2):
                r = 2 * j + u
                pltpu.make_async_copy(h_hbm.at[tok_ref[0, 0, r]], buf.at[s, r], sem.at[s]).start(priority=u)
            return c

        lax.fori_loop(0, tm // 2, body, 0, unroll=4)

    @pl.when(i == 0)
    def _():
        for j in range(ahead):
            @pl.when(j < n_used)
            def _():
                issue(jnp.int32(j), tok_refs[j])

    new_expert = jnp.logical_or(i == 0, te_ref[i] != te_ref[jnp.maximum(i - 1, 0)])

    @pl.when(new_expert)
    def _():
        wg_bf[...] = wg_ref[...].astype(BF16)
        wu_bf[...] = wu_ref[...].astype(BF16)
        wd_bf[...] = wd_ref[...].astype(BF16)

    def compute(issue_next):
        s_next = lax.rem(i + ahead, nslot)
        quarter = tm // 4

        def start_rows(part):
            if issue_next:
                for r in range(part * quarter, (part + 1) * quarter):
                    pltpu.make_async_copy(h_hbm.at[tok_refs[ahead][0, 0, r]], buf.at[s_next, r],
                                          sem.at[s_next]).start(priority=r % 2)

        pltpu.make_async_copy(h_hbm.at[pl.ds(0, tm)], buf.at[slot], sem.at[slot]).wait()
        start_rows(0)
        x = buf[slot].reshape(tm, -1)
        g = jnp.dot(x, wg_bf[...], preferred_element_type=F32)
        start_rows(1)
        u = jnp.dot(x, wu_bf[...], preferred_element_type=F32)
        start_rows(2)
        act = _silu(g) * u
        y = jnp.dot(act.astype(BF16), wd_bf[...], preferred_element_type=F32)
        start_rows(3)
        o_ref[...] = y.astype(BF16).reshape(o_ref.shape)

    @pl.when(i + ahead < n_used)
    def _():
        compute(True)

    @pl.when(jnp.logical_and(i + ahead >= n_used, i < n_used))
    def _():
        compute(False)

    @pl.when(i >= n_used)
    def _():
        o_ref[...] = jnp.zeros_like(o_ref)


def _moe_experts(h3, src_tok, tile_expert, n_used, w_gate, w_up, w_down, layer_idx, tm):
    t, sub, lanes = h3.shape
    d = sub * lanes
    n_tiles = src_tok.shape[0] // tm
    f = w_gate.shape[-1]
    ahead = MOE_GATHER_AHEAD
    tok = src_tok.reshape(n_tiles, 1, tm)

    def tok_spec(k):
        return pl.BlockSpec((1, 1, tm), lambda i, te, nu: (jnp.minimum(i + k, n_tiles - 1), 0, 0),
                            memory_space=pltpu.SMEM)

    return pl.pallas_call(
        functools.partial(_moe_kernel, tm=tm, ahead=ahead),
        grid_spec=pltpu.PrefetchScalarGridSpec(
            num_scalar_prefetch=2,
            grid=(n_tiles,),
            in_specs=[tok_spec(k) for k in range(ahead + 1)] + [
                      pl.BlockSpec(memory_space=pl.ANY),
                      pl.BlockSpec((None, None, d, f), lambda i, te, nu: (layer_idx, te[i], 0, 0)),
                      pl.BlockSpec((None, None, d, f), lambda i, te, nu: (layer_idx, te[i], 0, 0)),
                      pl.BlockSpec((None, None, f, d), lambda i, te, nu: (layer_idx, te[i], 0, 0))],
            out_specs=pl.BlockSpec((tm, sub, lanes), lambda i, te, nu: (i, 0, 0)),
            scratch_shapes=[pltpu.VMEM((d, f), BF16), pltpu.VMEM((d, f), BF16), pltpu.VMEM((f, d), BF16),
                            pltpu.VMEM((ahead + 1, tm, sub, lanes), BF16),
                            pltpu.SemaphoreType.DMA((ahead + 1,))]),
        out_shape=jax.ShapeDtypeStruct((n_tiles * tm, sub, lanes), BF16),
        compiler_params=_cparams("arbitrary"),
        name="moe_experts",
    )(tile_expert, n_used, *([tok] * (ahead + 1)), h3, w_gate, w_up, w_down)


def _combine_kernel(p1_ref, p2_ref, p1n_ref, p2n_ref, y_hbm, x_ref, ew_ref, mod_ref, gf_ref, *rest,
                    tm, n_steps, n_ctx_tiles, final_norm):
    out_refs, (buf, sem) = rest[:-2], rest[-2:]
    i = pl.program_id(0)
    slot = lax.rem(i, 2)

    def issue(s, pa_ref, pb_ref):
        def body(r, c):
            pltpu.make_async_copy(y_hbm.at[pa_ref[0, 0, r]], buf.at[s, 0, r], sem.at[s, 0]).start(priority=0)
            pltpu.make_async_copy(y_hbm.at[pb_ref[0, 0, r]], buf.at[s, 1, r], sem.at[s, 1]).start(priority=1)
            return c

        lax.fori_loop(0, tm, body, 0, unroll=8)

    @pl.when(i == 0)
    def _():
        issue(0, p1_ref, p2_ref)

    @pl.when(i + 1 < n_steps)
    def _():
        issue(1 - slot, p1n_ref, p2n_ref)

    for k in range(2):
        pltpu.make_async_copy(y_hbm.at[pl.ds(0, tm)], buf.at[slot, k], sem.at[slot, k]).wait()
    ew = ew_ref[...]
    y = (ew[:, 0:1] * buf[slot, 0].reshape(tm, -1).astype(F32)
         + ew[:, 1:2] * buf[slot, 1].reshape(tm, -1).astype(F32))
    x = x_ref[...] + mod_ref[0, 5:6, :] * y
    if not final_norm:
        out_refs[0][...] = x
    else:
        x = _rms(x, gf_ref[...])
        ctx_ref, lat_ref = out_refs

        @pl.when(i < n_ctx_tiles)
        def _():
            ctx_ref[...] = x

        @pl.when(i >= n_ctx_tiles)
        def _():
            lat_ref[...] = x


def _moe_combine(y_sorted, pos1, pos2, x, ew, mod, final_g, n_ctx_tiles, tiles_per_req, final_norm):
    t, d = x.shape
    tm = TOKEN_TILE
    n_steps = t // tm
    mrow = functools.partial(_tile_mod_row, n_ctx_tiles=n_ctx_tiles, tiles_per_req=tiles_per_req)
    smem_rows = pl.BlockSpec((1, 1, tm), lambda i: (i, 0, 0), memory_space=pltpu.SMEM)
    smem_next = pl.BlockSpec((1, 1, tm), lambda i: (jnp.minimum(i + 1, n_steps - 1), 0, 0),
                             memory_space=pltpu.SMEM)
    p1 = pos1.reshape(n_steps, 1, tm)
    p2 = pos2.reshape(n_steps, 1, tm)
    if final_norm:
        out_specs = [pl.BlockSpec((tm, d), lambda i: (jnp.minimum(i, n_ctx_tiles - 1), 0)),
                     pl.BlockSpec((tm, d), lambda i: (jnp.maximum(i - n_ctx_tiles, 0), 0))]
        out_shape = [jax.ShapeDtypeStruct((n_ctx_tiles * tm, d), F32),
                     jax.ShapeDtypeStruct((t - n_ctx_tiles * tm, d), F32)]
    else:
        out_specs = pl.BlockSpec((tm, d), lambda i: (i, 0))
        out_shape = jax.ShapeDtypeStruct((t, d), F32)
    return pl.pallas_call(
        functools.partial(_combine_kernel, tm=tm, n_steps=n_steps, n_ctx_tiles=n_ctx_tiles,
                          final_norm=final_norm),
        grid=(n_steps,),
        in_specs=[smem_rows, smem_rows, smem_next, smem_next,
                  pl.BlockSpec(memory_space=pl.ANY),
                  pl.BlockSpec((tm, d), lambda i: (i, 0)),
                  pl.BlockSpec((tm, 128), lambda i: (i, 0)),
                  pl.BlockSpec((1, 6, d), lambda i: (mrow(i), 0, 0)),
                  pl.BlockSpec((1, d), lambda i: (0, 0))],
        out_specs=out_specs,
        out_shape=out_shape,
        scratch_shapes=[pltpu.VMEM((2, 2, tm) + y_sorted.shape[1:], y_sorted.dtype),
                        pltpu.SemaphoreType.DMA((2, 2))],
        compiler_params=_cparams("arbitrary"),
        name="moe_combine",
    )(p1, p2, p1, p2, y_sorted, x, ew, mod, final_g.reshape(1, d))


def _axial_angles(seq, head_dim):
    rows = seq // GRID_W
    row = jnp.repeat(jnp.arange(rows), GRID_W).astype(F32)
    col = jnp.tile(jnp.arange(GRID_W), rows).astype(F32)
    n_freq = head_dim // 4
    inv = ROPE_BASE ** (-jnp.arange(n_freq, dtype=F32) / n_freq)
    return jnp.concatenate([row[:, None] * inv, col[:, None] * inv], axis=-1)


def kernel(x_prompt, x_sample, cache_attn_k, cache_attn_v, state_hgrn, state_ret, c, c_ctx, mod_w, mod_b, norm1_g, norm2_g, even_w_in, even_w_out, hgrn_lb_logits, hgrn_norm_g, attn_qn_g, attn_kn_g, odd_w_in, odd_w_out, ret_decay_logit, ret_gn_g, moe_w_group, moe_b_group, moe_w_expert, moe_b_expert, moe_w_gate, moe_w_up, moe_w_down, final_norm_g):
    batch, seq, d = x_prompt.shape
    dec_batch, dec_seq, _ = x_sample.shape
    depth = mod_w.shape[0]
    heads_a, val_a = hgrn_norm_g.shape[1:]
    key_a = hgrn_lb_logits.shape[2] // heads_a
    hd_b = attn_qn_g.shape[1]
    kv_b = cache_attn_k.shape[3]
    heads_b = (even_w_out.shape[1] - heads_a * val_a) // hd_b
    heads_c, val_c = ret_gn_g.shape[1:]
    key_c = state_ret.shape[4]
    n_experts = moe_w_expert.shape[2]
    n_groups = moe_w_group.shape[2]
    tm = TOKEN_TILE
    n_ctx = batch * seq
    n_ctx_tiles = n_ctx // tm
    tiles_per_req = dec_seq // tm
    assert seq % tm == 0 and dec_seq % tm == 0 and n_ctx % dec_seq == 0
    assert key_a == 128 and val_a == 128 and hd_b == 128 and n_groups + n_experts <= 128
    assert d % 128 == 0 and seq % RET_CHUNK == 0 and dec_seq % RET_CHUNK == 0 and dec_seq % GRID_W == 0
    n_tok = n_ctx + dec_batch * dec_seq
    assert (2 * n_tok) % EXPERT_TILE == 0 and n_tok % ROW_SOURCE_CHUNK == 0

    x = (x_prompt.reshape(n_ctx, d), x_sample.reshape(dec_batch * dec_seq, d))
    t = n_ctx + dec_batch * dec_seq

    n_vec = 1 + dec_batch
    rows = -(-n_vec // 8) * 8
    cvecs = jnp.zeros((rows, d), F32).at[0].set(c_ctx).at[1:n_vec].set(c)
    mods = _ada_mod(cvecs, mod_w, mod_b).reshape(depth, rows, 6, d)

    lower_bounds = jnp.cumsum(jax.nn.softmax(hgrn_lb_logits.astype(F32), axis=0), axis=0)
    log_gamma = jax.nn.log_sigmoid(ret_decay_logit.astype(F32))

    ang_b = _axial_angles(dec_seq, hd_b)
    cos_b = jnp.concatenate([jnp.cos(ang_b), jnp.cos(ang_b)], axis=-1)
    sin_b = jnp.concatenate([-jnp.sin(ang_b), jnp.sin(ang_b)], axis=-1)
    ang_c = _axial_angles(dec_seq, key_c)
    cos_c, sin_c = jnp.cos(ang_c), jnp.sin(ang_c)

    pad = 128 - n_groups - n_experts
    new_k = new_v = new_sh = new_sr = None
    assert depth == 2, "one HGRN2/GQA layer followed by one retention layer"
    for l in range(depth):
        mod = mods[l]
        if l % 2 == 0:
            e = l // 2
            na, nva, nqb, nkv = heads_a * key_a, heads_a * val_a, heads_b * hd_b, kv_b * hd_b
            o_zf, o_kb = na + nva, 3 * na + 2 * nva + nqb
            w_in = even_w_in[e].astype(BF16)
            proj16, proj32 = _in_proj(x, norm1_g[l], mod, w_in, t, n_ctx_tiles, tiles_per_req, w_in.shape[1],
                                      f32_cols=((o_zf, o_zf + 2 * na), (o_kb, o_kb + 2 * nkv)))
            lb = lower_bounds[l]
            oa, s_ctx = _hgrn(proj16, proj32, lb, hgrn_norm_g[e], None, e, None, batch, seq, 0,
                              heads_a, key_a, val_a)
            oa, _ = _hgrn(proj16, proj32, lb, hgrn_norm_g[e], state_hgrn, e, oa, dec_batch, dec_seq,
                          n_ctx // dec_seq, heads_a, key_a, val_a)
            col_q, col_kv = (na + 2 * nva) // hd_b, (2 * na) // hd_b
            ob, kc, vc = _attn(proj16, proj32, attn_qn_g[e], attn_kn_g[e], None, e, None, batch, seq, 0,
                               heads_b, kv_b, hd_b, col_q, col_kv)
            (ob,) = _attn(proj16, proj32, attn_qn_g[e], attn_kn_g[e],
                          (cache_attn_k, cache_attn_v, cos_b, sin_b), e, ob,
                          dec_batch, dec_seq, n_ctx // dec_seq, heads_b, kv_b, hd_b, col_q, col_kv)
            new_k = kc.reshape(batch, 1, seq, kv_b, hd_b)
            new_v = vc.reshape(batch, 1, seq, kv_b, hd_b)
            new_sh = s_ctx[:, None]
            w_out = even_w_out[e]
            mixes = [oa, ob]
        else:
            o = l // 2
            w_in = odd_w_in[o]
            proj = _in_proj(x, norm1_g[l], mod, w_in, t, n_ctx_tiles, tiles_per_req, w_in.shape[1] // 2)
            y, s_ctx = _retention(proj, log_gamma[o], ret_gn_g[o], None, o, None, batch, seq, 0,
                                  heads_c, key_c, val_c)
            y, _ = _retention(proj, log_gamma[o], ret_gn_g[o], (state_ret, cos_c, sin_c), o, y,
                              dec_batch, dec_seq, n_ctx // dec_seq, heads_c, key_c, val_c)
            new_sr = s_ctx[:, None]
            w_out = odd_w_out[o]
            mixes = [y]

        w_router = jnp.concatenate([moe_w_group[l], moe_w_expert[l], jnp.zeros((d, pad), F32)], axis=1)
        w_router_hi = w_router.astype(BF16)
        w_router = jnp.concatenate([w_router_hi, (w_router - w_router_hi.astype(F32)).astype(BF16)], axis=1)
        b_router = jnp.concatenate([moe_b_group[l], moe_b_expert[l], jnp.zeros((pad,), F32)]).reshape(1, 128)
        xn, h2, meta, ew, cnt = _out_proj(mixes, w_out, x, mod, norm2_g[l], w_router, b_router,
                                          n_ctx_tiles, tiles_per_req, n_groups, n_experts)
        counts = cnt[0, n_groups:n_groups + n_experts].astype(jnp.int32)
        tile_expert, n_used, ends, pos1, pos2, n_tiles = _dispatch_plan(meta, counts, n_experts, EXPERT_TILE)
        src_tok = _row_source(ends, pos1, pos2, n_tiles * EXPERT_TILE, EXPERT_TILE)
        y_sorted = _moe_experts(h2, src_tok, tile_expert, n_used, moe_w_gate, moe_w_up, moe_w_down, l, EXPERT_TILE)
        x = _moe_combine(y_sorted, pos1, pos2, xn, ew, mod, final_norm_g, n_ctx_tiles, tiles_per_req,
                         final_norm=(l == depth - 1))

    y_prompt = x[0].reshape(batch, seq, d)
    y_sample = x[1].reshape(dec_batch, dec_seq, d)
    return (y_prompt, y_sample, new_k, new_v, new_sh, new_sr)
```

```python
import functools

import numpy as np
import jax
import jax.numpy as jnp
from jax import lax
from jax.experimental import pallas as pl
from jax.experimental.pallas import tpu as pltpu

F32 = jnp.float32
BF16 = jnp.bfloat16
EPS = 1e-6
ROPE_BASE = 10000.0
GRID_W = 64
HGRN_BLOCK = 128
HGRN_DIAG = 8
RET_CHUNK = 256
TOKEN_TILE = 256
INPROJ_TILE = 512
EXPERT_TILE = 256
MOE_GATHER_AHEAD = 2
ROW_SOURCE_CHUNK = 512
V7X_VMEM_LIMIT_BYTES = 56 * 1024 * 1024
HIGHEST = lax.Precision.HIGHEST
LOG2_E = 1.4426950408889634

_NT = (((1,), (1,)), ((), ()))


def _cparams(*sem):
    return pltpu.CompilerParams(dimension_semantics=sem, vmem_limit_bytes=V7X_VMEM_LIMIT_BYTES)


def _rms(x, g):
    return x * lax.rsqrt(jnp.mean(x * x, axis=-1, keepdims=True) + EPS) * g


def _silu(x):
    return x * jax.nn.sigmoid(x)


def _bdot(a, b):
    return jnp.dot(a.astype(BF16), b.astype(BF16), preferred_element_type=F32)


def _bdot_nt(a, b):
    return lax.dot_general(a.astype(BF16), b.astype(BF16), _NT, preferred_element_type=F32)


def _mod_kernel(c_ref, w_ref, b_ref, o_ref):
    a = _silu(c_ref[...])
    o_ref[0] = jnp.dot(a, w_ref[0], precision=HIGHEST, preferred_element_type=F32) + b_ref[0]


def _ada_mod(cvecs, mod_w, mod_b):
    depth, d, n = mod_w.shape
    tn = n // 4
    rows = cvecs.shape[0]
    return pl.pallas_call(
        _mod_kernel,
        grid=(depth, n // tn),
        in_specs=[pl.BlockSpec((rows, d), lambda l, j: (0, 0)),
                  pl.BlockSpec((1, d, tn), lambda l, j: (l, 0, j)),
                  pl.BlockSpec((1, 1, tn), lambda l, j: (l, 0, j))],
        out_specs=pl.BlockSpec((1, rows, tn), lambda l, j: (l, 0, j)),
        out_shape=jax.ShapeDtypeStruct((depth, rows, n), F32),
        compiler_params=_cparams("arbitrary", "arbitrary"),
        name="ada_mod",
    )(cvecs, mod_w, mod_b.reshape(depth, 1, n))


def _tile_mod_row(i, n_ctx_tiles, tiles_per_req):
    return jnp.where(i < n_ctx_tiles, 0, 1 + jnp.maximum(i - n_ctx_tiles, 0) // tiles_per_req)


def _token_specs(x, tm, n_ctx_tiles, step_of):
    if not isinstance(x, tuple):
        return [pl.BlockSpec((tm, x.shape[1]), lambda *g: (step_of(*g), 0))], (x,)
    d = x[0].shape[1]
    return ([pl.BlockSpec((tm, d), lambda *g: (jnp.minimum(step_of(*g), n_ctx_tiles - 1), 0)),
             pl.BlockSpec((tm, d), lambda *g: (jnp.maximum(step_of(*g) - n_ctx_tiles, 0), 0))], x)


def _token_rows(x_refs, i, n_ctx_tiles):
    if len(x_refs) == 1:
        return x_refs[0][...]
    return jnp.where(i < n_ctx_tiles, x_refs[0][...], x_refs[1][...])


def _inproj_kernel(*refs, n_x, n_ctx_tiles, f32_cols, cast_w):
    x_refs = refs[:n_x]
    g_ref, mod_ref, w_ref = refs[n_x:n_x + 3]
    out_refs = refs[n_x + 3:]
    if cast_w:
        out_refs, w_bf = out_refs[:-1], out_refs[-1]

        @pl.when(pl.program_id(1) == 0)
        def _():
            w_bf[...] = w_ref[...].astype(BF16)

        w_ref = w_bf
    x = _token_rows(x_refs, pl.program_id(1), n_ctx_tiles)
    h = _rms(x, g_ref[...]) * (1.0 + mod_ref[0, 1:2, :]) + mod_ref[0, 0:1, :]
    acc = jnp.dot(h.astype(BF16), w_ref[...], preferred_element_type=F32)
    if not f32_cols:
        out_refs[0][...] = acc.astype(BF16)
        return
    n = acc.shape[1]
    at16 = at32 = start = 0
    for lo, hi in tuple(f32_cols) + ((n, n),):
        if lo > start:
            out_refs[0][:, at16:at16 + lo - start] = acc[:, start:lo].astype(BF16)
            at16 += lo - start
        if hi > lo:
            out_refs[1][:, at32:at32 + hi - lo] = acc[:, lo:hi]
            at32 += hi - lo
        start = hi


def _in_proj(x, g, mod, w_bf, t, n_ctx_tiles, tiles_per_req, tn, f32_cols=()):
    d, n = w_bf.shape
    cast_w = w_bf.dtype != BF16
    tm = INPROJ_TILE
    scale = INPROJ_TILE // TOKEN_TILE
    assert n_ctx_tiles % scale == 0 and tiles_per_req % scale == 0
    n_ctx_tiles, tiles_per_req = n_ctx_tiles // scale, tiles_per_req // scale
    mrow = functools.partial(_tile_mod_row, n_ctx_tiles=n_ctx_tiles, tiles_per_req=tiles_per_req)
    x_specs, x_args = _token_specs(x, tm, n_ctx_tiles, lambda j, i: i)
    if not f32_cols:
        out_specs = pl.BlockSpec((tm, tn), lambda j, i: (i, j))
        out_shape = jax.ShapeDtypeStruct((t, n), BF16)
    else:
        assert tn == n
        n32 = sum(hi - lo for lo, hi in f32_cols)
        out_specs = [pl.BlockSpec((tm, n - n32), lambda j, i: (i, 0)),
                     pl.BlockSpec((tm, n32), lambda j, i: (i, 0))]
        out_shape = [jax.ShapeDtypeStruct((t, n - n32), BF16), jax.ShapeDtypeStruct((t, n32), F32)]
    return pl.pallas_call(
        functools.partial(_inproj_kernel, n_x=len(x_args), n_ctx_tiles=n_ctx_tiles, f32_cols=tuple(f32_cols),
                          cast_w=cast_w),
        grid=(n // tn, t // tm),
        in_specs=x_specs + [pl.BlockSpec((1, d), lambda j, i: (0, 0)),
                            pl.BlockSpec((1, 6, d), lambda j, i: (mrow(i), 0, 0)),
                            pl.BlockSpec((d, tn), lambda j, i: (0, j))],
        out_specs=out_specs,
        out_shape=out_shape,
        scratch_shapes=[pltpu.VMEM((d, tn), BF16)] if cast_w else [],
        compiler_params=_cparams("arbitrary", "arbitrary"),
        name="in_proj",
    )(*x_args, g.reshape(1, d), mod, w_bf)


def _hgrn_constants(nb, dg):
    r = np.arange(nb)[:, None]
    c = np.arange(nb)[None, :]
    levels = []
    m = nb // 2
    while m >= dg:
        levels.append(m)
        m //= 2
    owner = np.full((2, nb, nb), len(levels) + 1, np.int32)
    qrows = np.zeros((2, len(levels), nb, nb), np.int32)
    for d in (0, 1):
        for idx, m in enumerate(levels):
            same = (r // (2 * m)) == (c // (2 * m))
            r_hi = (r % (2 * m)) >= m
            c_hi = (c % (2 * m)) >= m
            later_q = r_hi if d == 0 else ~r_hi
            owner[d][same & later_q & (~c_hi if d == 0 else c_hi)] = idx
            qrows[d, idx] = np.broadcast_to(later_q, (nb, nb))
        owner[d][(r // dg) == (c // dg)] = len(levels)
    tri = np.stack([c <= r, c >= r]).astype(np.float32)
    return levels, jnp.asarray(tri, dtype=BF16), jnp.asarray(owner), jnp.asarray(qrows)


def _hgrn_decay(z, lb, tri):
    logf = jnp.log(lb + (1.0 - lb) * jax.nn.sigmoid(z))
    k = (1.0 - lb) * jax.nn.sigmoid(-z)
    l2 = logf * LOG2_E
    hi = l2.astype(BF16)
    lo = (l2 - hi.astype(F32)).astype(BF16)
    c = jnp.dot(tri, hi, preferred_element_type=F32) + jnp.dot(tri, lo, preferred_element_type=F32)
    return c, k


def _hgrn_block(q, v, st_t, reverse, c_ref, k_ref, ecat, levels, owner, qrows_ref):
    nb = HGRN_BLOCK
    c = c_ref[...]
    k = k_ref[...]
    edge = 0 if reverse else nb - 1
    c_edge = c_ref[edge:edge + 1, :]

    o = _bdot_nt(q * jnp.exp2(c), st_t)
    kd = k * jnp.exp2(c_edge - c)
    st_new = st_t * jnp.exp2(c_edge) + _bdot(v.T, kd)

    a = jnp.zeros((nb, nb), F32)
    for idx, m in enumerate(levels):
        pieces = []
        for grp in range(nb // (2 * m)):
            edge_row = grp * 2 * m + (m if reverse else m - 1)
            pieces.append(jnp.broadcast_to(c_ref[edge_row:edge_row + 1, :], (2 * m, nb)))
        bnd = pieces[0] if len(pieces) == 1 else jnp.concatenate(pieces, axis=0)
        e = jnp.exp2(-jnp.abs(c - bnd))
        z = (jnp.where(qrows_ref[idx] != 0, q, k) * e).astype(BF16)
        am = lax.dot_general(z, z, _NT, preferred_element_type=F32)
        a = jnp.where(owner == idx, am, a)

    dg = HGRN_DIAG
    rowi = lax.broadcasted_iota(jnp.int32, (dg, nb), 0)
    cols = []
    for s in range(dg):
        keep = (rowi <= s) if reverse else (rowi >= s)
        tiles = []
        for j in range(nb // dg):
            r = j * dg + s
            qj = q[j * dg:(j + 1) * dg, :]
            cj = c[j * dg:(j + 1) * dg, :]
            cs = jnp.broadcast_to(c_ref[r:r + 1, :], (dg, nb))
            ks = jnp.broadcast_to(k_ref[r:r + 1, :], (dg, nb))
            tiles.append(qj * ks * jnp.exp2(jnp.where(keep, cj - cs, -jnp.inf)))
        cols.append(jnp.concatenate(tiles, axis=0).astype(BF16))
    p = jnp.concatenate(cols, axis=1)
    red = jnp.dot(p, ecat, preferred_element_type=F32)
    a = jnp.where(owner == len(levels), red, a)
    o = o + _bdot(a, v)
    return o, st_new


def _hgrn_kernel(*refs, nblk, has_state, qscale, levels):
    (qa_ref, ia_ref, zf_ref, zb_ref, ga_ref, lb_ref, ng_ref, ecat_ref, tri_ref, owner_ref,
     qrows_ref) = refs[:11]
    s0_ref = refs[11] if has_state else None
    o_ref, sfin_ref, oacc_ref, c_scr, k_scr = refs[-5:]
    nb = HGRN_BLOCK

    def gates(direction, blk, slot):
        z_ref = zb_ref if direction == 1 else zf_ref
        r0 = blk * nb if isinstance(blk, int) else pl.multiple_of(blk * nb, nb)
        c, k = _hgrn_decay(z_ref[pl.ds(r0, nb), :], lb_ref[direction:direction + 1, :], tri_ref[direction])
        c_scr[direction, slot] = c
        k_scr[direction, slot] = k

    def block(direction, blk, slot, st_t):
        r0 = blk * nb if isinstance(blk, int) else pl.multiple_of(blk * nb, nb)
        q = _silu(qa_ref[pl.ds(r0, nb), :].astype(F32)) * qscale
        v = ia_ref[pl.ds(r0, nb), :].astype(F32)
        o, st_new = _hgrn_block(q, v, st_t, direction == 1, c_scr.at[direction, slot],
                                k_scr.at[direction, slot], ecat_ref[...], levels, owner_ref[direction],
                                qrows_ref.at[direction])
        oacc_ref[direction, pl.ds(r0, nb), :] = o
        return st_new

    gates(0, 0, 0)
    gates(1, nblk - 1, 0)

    def body(i, carry):
        st_f, st_b = carry
        slot = lax.rem(i, 2)
        st_f = block(0, i, slot, st_f)
        st_b = block(1, nblk - 1 - i, slot, st_b)
        gates(0, i + 1, 1 - slot)
        gates(1, nblk - 2 - i, 1 - slot)
        return st_f, st_b

    if has_state:
        st0 = (s0_ref[0].T, s0_ref[1].T)
    else:
        zero = jnp.zeros((ia_ref.shape[1], qa_ref.shape[1]), F32)
        st0 = (zero, zero)
    st_f, st_b = lax.fori_loop(0, nblk - 1, body, st0)
    last_slot = (nblk - 1) % 2
    st_f = block(0, nblk - 1, last_slot, st_f)
    st_b = block(1, 0, last_slot, st_b)
    sfin_ref[0] = st_f.T
    sfin_ref[1] = st_b.T

    o = oacc_ref[0] + oacc_ref[1]
    o_ref[...] = (_rms(o, ng_ref[0]) * _silu(ga_ref[...].astype(F32))).astype(o_ref.dtype)


def _hgrn(proj16, proj32, lb, norm_g, s0, layer_idx, out_prev, n_seq, seq, row_blk0, heads, key_d, val_d):
    t = proj16.shape[0]
    has_state = s0 is not None
    hk = heads * key_d // 128
    off_qa, off_ia, off_ga = 0, hk, 2 * hk
    off_zf, off_zb = 0, hk

    def col(off):
        return pl.BlockSpec((seq, 128), lambda b, h: (row_blk0 + b, off + h))

    lane = np.arange(key_d)[None, :]
    srow = np.repeat(np.arange(HGRN_DIAG), key_d)[:, None]
    ecat = jnp.asarray((lane % HGRN_DIAG == srow).astype(np.float32), dtype=BF16)
    levels, tri, owner, qrows = _hgrn_constants(HGRN_BLOCK, HGRN_DIAG)
    in_specs = [col(off_qa), col(off_ia), col(off_zf), col(off_zb), col(off_ga),
                pl.BlockSpec((2, key_d), lambda b, h: (0, h)),
                pl.BlockSpec((1, 1, val_d), lambda b, h: (h, 0, 0)),
                pl.BlockSpec((HGRN_DIAG * key_d, key_d), lambda b, h: (0, 0)),
                pl.BlockSpec(tri.shape, lambda b, h: (0, 0, 0)),
                pl.BlockSpec(owner.shape, lambda b, h: (0, 0, 0)),
                pl.BlockSpec(qrows.shape, lambda b, h: (0, 0, 0, 0))]
    args = [proj16, proj16, proj32, proj32, proj16, lb, norm_g.reshape(heads, 1, val_d), ecat, tri, owner, qrows]
    if has_state:
        in_specs.append(pl.BlockSpec((None, None, 2, None, key_d, val_d),
                                     lambda b, h: (b, layer_idx, 0, h, 0, 0)))
        args.append(s0)
    aliases = {}
    if out_prev is not None:
        in_specs.append(pl.BlockSpec(memory_space=pl.ANY))
        args.append(out_prev)
        aliases = {len(args) - 1: 0}

    kern = functools.partial(_hgrn_kernel, nblk=seq // HGRN_BLOCK, has_state=has_state,
                             qscale=float(key_d) ** -0.5, levels=tuple(levels))
    return pl.pallas_call(
        kern,
        grid=(n_seq, heads),
        in_specs=in_specs,
        out_specs=[pl.BlockSpec((seq, val_d), lambda b, h: (row_blk0 + b, h)),
                   pl.BlockSpec((None, 2, None, key_d, val_d), lambda b, h: (b, 0, h, 0, 0))],
        out_shape=[jax.ShapeDtypeStruct((t, heads * val_d), BF16),
                   jax.ShapeDtypeStruct((n_seq, 2, heads, key_d, val_d), F32)],
        scratch_shapes=[pltpu.VMEM((2, seq, val_d), F32),
                        pltpu.VMEM((2, 2, HGRN_BLOCK, key_d), F32),
                        pltpu.VMEM((2, 2, HGRN_BLOCK, key_d), F32)],
        input_output_aliases=aliases,
        compiler_params=_cparams("arbitrary", "arbitrary"),
        name="hgrn2",
    )(*args)


def _rope(x, cos, sin_signed):
    return x * cos + pltpu.roll(x, x.shape[-1] // 2, 1) * sin_signed


def _attn_kernel(*refs, has_ctx, grp, qblk, scale):
    if has_ctx:
        (q_ref, k_ref, v_ref, qn_ref, kn_ref, ck_ref, cv_ref, cos_ref, sin_ref, _prev, o_ref) = refs
    else:
        (q_ref, k_ref, v_ref, qn_ref, kn_ref, o_ref, ko_ref, vo_ref) = refs
    seq, hd = k_ref.shape
    k = _rms(k_ref[...], kn_ref[...])
    v = v_ref[...]
    if has_ctx:
        k = _rope(k, cos_ref[...], sin_ref[...])
        ck = ck_ref[...].astype(BF16)
        cv = cv_ref[...].astype(BF16)
    else:
        head = pl.program_id(1)
        ko_ref[:, pl.ds(head, 1), :] = k[:, None, :]
        vo_ref[:, pl.ds(head, 1), :] = v[:, None, :]
    kb = k.astype(BF16)
    vb = v.astype(BF16)

    def body(i, carry):
        r0 = pl.multiple_of(i * qblk, qblk)
        for g in range(grp):
            q = _rms(q_ref[pl.ds(r0, qblk), g * hd:(g + 1) * hd].astype(F32), qn_ref[...])
            if has_ctx:
                q = _rope(q, cos_ref[pl.ds(r0, qblk), :], sin_ref[pl.ds(r0, qblk), :])
            qb = q.astype(BF16)
            s2 = lax.dot_general(qb, kb, _NT, preferred_element_type=F32) * scale
            m = jnp.max(s2, axis=-1, keepdims=True)
            if has_ctx:
                s1 = lax.dot_general(qb, ck, _NT, preferred_element_type=F32) * scale
                m = jnp.maximum(m, jnp.max(s1, axis=-1, keepdims=True))
                p1 = jnp.exp(s1 - m)
            p2 = jnp.exp(s2 - m)
            den = jnp.sum(p2, axis=-1, keepdims=True)
            acc = jnp.dot(p2.astype(BF16), vb, preferred_element_type=F32)
            if has_ctx:
                den = den + jnp.sum(p1, axis=-1, keepdims=True)
                acc = acc + jnp.dot(p1.astype(BF16), cv, preferred_element_type=F32)
            o_ref[pl.ds(r0, qblk), g * hd:(g + 1) * hd] = (acc / den).astype(o_ref.dtype)
        return carry

    lax.fori_loop(0, seq // qblk, body, 0)


def _attn(proj16, proj32, qn_g, kn_g, ctx, layer_idx, out_prev, n_seq, seq, row_blk0, heads, kv_heads, hd,
          col_q, col_kv):
    t = proj16.shape[0]
    grp = heads // kv_heads
    has_ctx = ctx is not None
    cq = col_q // grp
    ck0 = col_kv
    cv0 = ck0 + kv_heads
    in_specs = [pl.BlockSpec((seq, grp * hd), lambda b, h: (row_blk0 + b, cq + h)),
                pl.BlockSpec((seq, hd), lambda b, h: (row_blk0 + b, ck0 + h)),
                pl.BlockSpec((seq, hd), lambda b, h: (row_blk0 + b, cv0 + h)),
                pl.BlockSpec((1, hd), lambda b, h: (0, 0)),
                pl.BlockSpec((1, hd), lambda b, h: (0, 0))]
    args = [proj16, proj32, proj32, qn_g.reshape(1, hd), kn_g.reshape(1, hd)]
    out_specs = [pl.BlockSpec((seq, grp * hd), lambda b, h: (row_blk0 + b, h))]
    out_shape = [jax.ShapeDtypeStruct((t, heads * hd), BF16)]
    aliases = {}
    if has_ctx:
        cache_k, cache_v, cos, sin_signed = ctx
        nreq, nlay, past = cache_k.shape[:3]
        cache_k = cache_k.reshape(nreq, nlay, past, kv_heads * hd)
        cache_v = cache_v.reshape(nreq, nlay, past, kv_heads * hd)
        spec_c = pl.BlockSpec((None, None, past, hd), lambda b, h: (b, layer_idx, 0, h))
        in_specs += [spec_c, spec_c,
                     pl.BlockSpec((seq, hd), lambda b, h: (0, 0)),
                     pl.BlockSpec((seq, hd), lambda b, h: (0, 0)),
                     pl.BlockSpec(memory_space=pl.ANY)]
        args += [cache_k, cache_v, cos, sin_signed, out_prev]
        aliases = {len(args) - 1: 0}
    else:
        out_specs += [pl.BlockSpec((None, seq, kv_heads, hd), lambda b, h: (b, 0, 0, 0))] * 2
        out_shape += [jax.ShapeDtypeStruct((n_seq, seq, kv_heads, hd), F32)] * 2
    return pl.pallas_call(
        functools.partial(_attn_kernel, has_ctx=has_ctx, grp=grp, qblk=min(seq, 256), scale=float(hd) ** -0.5),
        grid=(n_seq, kv_heads),
        in_specs=in_specs,
        out_specs=out_specs,
        out_shape=out_shape,
        input_output_aliases=aliases,
        compiler_params=_cparams("arbitrary", "arbitrary"),
        name="gqa",
    )(*args)


def _ret_kernel(*refs, nchunk, has_ctx, kscale):
    if has_ctx:
        (lg_ref, q_ref, k_ref, v_ref, g_ref, gn_ref, s0_ref, cos_ref, sin_ref, _prev,
         o_ref, sfin_ref, sb_ref) = refs
    else:
        (lg_ref, q_ref, k_ref, v_ref, g_ref, gn_ref, o_ref, sfin_ref, sb_ref) = refs
    cs = RET_CHUNK
    kd_dim = q_ref.shape[1]
    h = pl.program_id(1)
    lgf = lg_ref[0, h]
    lgb = lg_ref[1, h]
    row = lax.broadcasted_iota(jnp.int32, (cs, cs), 0)
    col = lax.broadcasted_iota(jnp.int32, (cs, cs), 1)
    rel = (row - col).astype(F32)
    dmat = (jnp.exp(jnp.where(rel >= 0, rel * lgf, -jnp.inf))
            + jnp.exp(jnp.where(rel <= 0, -rel * lgb, -jnp.inf)))
    j = lax.broadcasted_iota(jnp.int32, (cs, 1), 0).astype(F32)
    one = jnp.ones((1, 1), F32)
    dec_f = jnp.exp(one * (cs * lgf))
    dec_b = jnp.exp(one * (cs * lgb))

    def load_qk(ref, r0, mul):
        x = ref[pl.ds(r0, cs), :].astype(F32) * mul
        if has_ctx:
            half = kd_dim // 2
            x1, x2 = x[:, :half], x[:, half:]
            cos = cos_ref[pl.ds(r0, cs), :]
            sin = sin_ref[pl.ds(r0, cs), :]
            x = jnp.concatenate([x1 * cos - x2 * sin, x2 * cos + x1 * sin], axis=1)
        return x

    def bwd_body(i, st):
        n = nchunk - 1 - i
        r0 = pl.multiple_of(n * cs, cs)
        sb_ref[n] = st
        k = load_qk(k_ref, r0, kscale)
        v = v_ref[pl.ds(r0, cs), :]
        return st * dec_b + _bdot((k * jnp.exp(j * lgb)).T, v)

    st_b0 = s0_ref[1] if has_ctx else jnp.zeros(sfin_ref.shape[1:], F32)
    st_b = lax.fori_loop(0, nchunk, bwd_body, st_b0)
    sfin_ref[1] = st_b

    def fwd_body(n, st):
        r0 = pl.multiple_of(n * cs, cs)
        q = load_qk(q_ref, r0, 1.0)
        k = load_qk(k_ref, r0, kscale)
        v = v_ref[pl.ds(r0, cs), :]
        a = _bdot_nt(q, k) * dmat
        o = _bdot(a, v)
        if has_ctx or nchunk > 1:
            o = o + _bdot(q * jnp.exp((j + 1.0) * lgf), st)
            o = o + _bdot(q * jnp.exp((cs - j) * lgb), sb_ref[n])
        mu = jnp.mean(o, axis=-1, keepdims=True)
        var = jnp.mean(jnp.square(o - mu), axis=-1, keepdims=True)
        y = (o - mu) * lax.rsqrt(var + EPS) * gn_ref[0]
        o_ref[pl.ds(r0, cs), :] = (_silu(g_ref[pl.ds(r0, cs), :].astype(F32)) * y).astype(o_ref.dtype)
        return st * dec_f + _bdot((k * jnp.exp((cs - 1.0 - j) * lgf)).T, v)

    st_f0 = s0_ref[0] if has_ctx else jnp.zeros(sfin_ref.shape[1:], F32)
    sfin_ref[0] = lax.fori_loop(0, nchunk, fwd_body, st_f0)


def _retention(proj, log_gamma, gn_g, ctx, layer_idx, out_prev, n_seq, seq, row_blk0, heads, key_d, val_d):
    t = proj.shape[0]
    has_ctx = ctx is not None
    nq = heads * key_d
    cq, ck, cv, cg = 0, nq // key_d, 2 * nq // val_d, (2 * nq + heads * val_d) // val_d
    in_specs = [pl.BlockSpec(memory_space=pltpu.SMEM),
                pl.BlockSpec((seq, key_d), lambda b, h: (row_blk0 + b, cq + h)),
                pl.BlockSpec((seq, key_d), lambda b, h: (row_blk0 + b, ck + h)),
                pl.BlockSpec((seq, val_d), lambda b, h: (row_blk0 + b, cv + h)),
                pl.BlockSpec((seq, val_d), lambda b, h: (row_blk0 + b, cg + h)),
                pl.BlockSpec((1, 1, val_d), lambda b, h: (h, 0, 0))]
    args = [log_gamma, proj, proj, proj, proj, gn_g.reshape(heads, 1, val_d)]
    aliases = {}
    if has_ctx:
        s0, cos, sin = ctx
        in_specs += [pl.BlockSpec((None, None, 2, None, key_d, val_d), lambda b, h: (b, layer_idx, 0, h, 0, 0)),
                     pl.BlockSpec((seq, key_d // 2), lambda b, h: (0, 0)),
                     pl.BlockSpec((seq, key_d // 2), lambda b, h: (0, 0)),
                     pl.BlockSpec(memory_space=pl.ANY)]
        args += [s0, cos, sin, out_prev]
        aliases = {len(args) - 1: 0}
    nchunk = seq // RET_CHUNK
    return pl.pallas_call(
        functools.partial(_ret_kernel, nchunk=nchunk, has_ctx=has_ctx, kscale=float(key_d) ** -0.5),
        grid=(n_seq, heads),
        in_specs=in_specs,
        out_specs=[pl.BlockSpec((seq, val_d), lambda b, h: (row_blk0 + b, h)),
                   pl.BlockSpec((None, 2, None, key_d, val_d), lambda b, h: (b, 0, h, 0, 0))],
        out_shape=[jax.ShapeDtypeStruct((t, heads * val_d), BF16),
                   jax.ShapeDtypeStruct((n_seq, 2, heads, key_d, val_d), F32)],
        scratch_shapes=[pltpu.VMEM((nchunk, key_d, val_d), F32)],
        input_output_aliases=aliases,
        compiler_params=_cparams("arbitrary", "arbitrary"),
        name="retention",
    )(*args)


def _outproj_kernel(*refs, n_in, n_x, n_ctx_tiles, n_groups, n_experts):
    mix_refs = refs[:n_in]
    w_refs = refs[n_in:2 * n_in]
    x_refs = refs[2 * n_in:2 * n_in + n_x]
    base = 2 * n_in + n_x
    (mod_ref, g2_ref, wr_ref, br_ref, xn_ref, h2_ref, meta_ref, ew_ref, cnt_out_ref,
     cnt_ref) = refs[base:base + 10]
    wbf_refs = refs[base + 10:]

    @pl.when(pl.program_id(0) == 0)
    def _():
        for wref, wbf in zip(w_refs, wbf_refs):
            wbf[...] = wref[...].astype(BF16)

    acc = jnp.dot(mix_refs[0][...], wbf_refs[0][...], preferred_element_type=F32)
    for mref, wbf in zip(mix_refs[1:], wbf_refs[1:]):
        acc = acc + jnp.dot(mref[...], wbf[...], preferred_element_type=F32)
    xn = _token_rows(x_refs, pl.program_id(0), n_ctx_tiles) + mod_ref[0, 2:3, :] * acc
    xn_ref[...] = xn
    h2 = _rms(xn, g2_ref[...]) * (1.0 + mod_ref[0, 4:5, :]) + mod_ref[0, 3:4, :]
    h2_ref[...] = h2.astype(BF16).reshape(h2_ref.shape)
    h_hi = h2.astype(BF16)
    h_lo = (h2 - h_hi.astype(F32)).astype(BF16)
    nl = br_ref.shape[1]
    hh = jnp.dot(h_hi, wr_ref[...], preferred_element_type=F32)
    lg = hh[:, :nl] + hh[:, nl:] + jnp.dot(h_lo, wr_ref[:, :nl], preferred_element_type=F32) + br_ref[...]

    per_grp = n_experts // n_groups
    lane = lax.broadcasted_iota(jnp.int32, lg.shape, 1).astype(F32)
    big = float(lg.shape[1])
    ninf = -jnp.inf
    gl = jnp.where(lane < n_groups, lg, ninf)
    gmax = jnp.max(gl, axis=-1, keepdims=True)
    p_grp = 1.0 / jnp.sum(jnp.exp(gl - gmax), axis=-1, keepdims=True)
    gidx = jnp.min(jnp.where(gl == gmax, lane, big), axis=-1, keepdims=True)
    in_grp = ((lane >= n_groups) & (lane < n_groups + n_experts)
              & (jnp.floor((lane - n_groups) / per_grp) == gidx))
    el = jnp.where(in_grp, lg, ninf)
    emax = jnp.max(el, axis=-1, keepdims=True)
    esum = jnp.sum(jnp.exp(el - emax), axis=-1, keepdims=True)
    i1 = jnp.min(jnp.where(el == emax, lane, big), axis=-1, keepdims=True)
    el2 = jnp.where(lane == i1, ninf, el)
    emax2 = jnp.max(el2, axis=-1, keepdims=True)
    i2 = jnp.min(jnp.where(el2 == emax2, lane, big), axis=-1, keepdims=True)
    p1 = 1.0 / esum
    p2 = jnp.exp(emax2 - emax) / esum
    w1 = p1 / (p1 + p2) * p_grp
    w2 = p2 / (p1 + p2) * p_grp
    ew_ref[...] = jnp.where(lane == 0.0, w1, jnp.where(lane == 1.0, w2, 0.0))

    step = pl.program_id(0)

    @pl.when(step == 0)
    def _():
        cnt_ref[...] = jnp.zeros_like(cnt_ref)

    tm = lg.shape[0]
    hit1 = lane == i1
    hit2 = lane == i2
    sel = jnp.where(hit1 | hit2, 1.0, 0.0)
    rr = lax.broadcasted_iota(jnp.int32, (tm, tm), 0)
    cc = lax.broadcasted_iota(jnp.int32, (tm, tm), 1)
    before = jnp.where(cc < rr, 1.0, 0.0).astype(BF16)
    prior = jnp.dot(before, sel.astype(BF16), preferred_element_type=F32) + cnt_ref[...]
    rank1 = jnp.sum(jnp.where(hit1, prior, 0.0), axis=-1, keepdims=True)
    rank2 = jnp.sum(jnp.where(hit2, prior, 0.0), axis=-1, keepdims=True)
    cnt_ref[...] += jnp.sum(sel, axis=0, keepdims=True)
    cnt_out_ref[...] = cnt_ref[...]
    meta = jnp.where(lane == 0.0, i1 - n_groups,
                     jnp.where(lane == 1.0, i2 - n_groups,
                               jnp.where(lane == 2.0, rank1, jnp.where(lane == 3.0, rank2, 0.0))))
    meta_ref[0] = meta.T[0:8, :].astype(jnp.int32)


def _out_proj(mixes, w_out, x, mod, g2, w_router, b_router, n_ctx_tiles, tiles_per_req, n_groups, n_experts):
    t = mixes[0].shape[0]
    d = w_out.shape[1]
    tm = TOKEN_TILE
    n_in = len(mixes)
    width = mixes[0].shape[1]
    assert all(m.shape[1] == width and m.dtype == BF16 for m in mixes) and w_out.shape[0] == n_in * width
    mrow = functools.partial(_tile_mod_row, n_ctx_tiles=n_ctx_tiles, tiles_per_req=tiles_per_req)
    x_specs, x_args = _token_specs(x, tm, n_ctx_tiles, lambda i: i)
    in_specs = ([pl.BlockSpec((tm, width), lambda i: (i, 0)) for _ in mixes]
                + [pl.BlockSpec((width, d), lambda i, k=k: (k, 0)) for k in range(n_in)]
                + x_specs
                + [pl.BlockSpec((1, 6, d), lambda i: (mrow(i), 0, 0)),
                   pl.BlockSpec((1, d), lambda i: (0, 0)),
                   pl.BlockSpec(w_router.shape, lambda i: (0, 0)),
                   pl.BlockSpec((1, 128), lambda i: (0, 0))])
    return pl.pallas_call(
        functools.partial(_outproj_kernel, n_in=n_in, n_x=len(x_args), n_ctx_tiles=n_ctx_tiles,
                          n_groups=n_groups, n_experts=n_experts),
        grid=(t // tm,),
        in_specs=in_specs,
        out_specs=[pl.BlockSpec((tm, d), lambda i: (i, 0)),
                   pl.BlockSpec((tm, d // 128, 128), lambda i: (i, 0, 0)),
                   pl.BlockSpec((1, 8, tm), lambda i: (i, 0, 0)),
                   pl.BlockSpec((tm, 128), lambda i: (i, 0)),
                   pl.BlockSpec((1, 128), lambda i: (0, 0))],
        out_shape=[jax.ShapeDtypeStruct((t, d), F32),
                   jax.ShapeDtypeStruct((t, d // 128, 128), BF16),
                   jax.ShapeDtypeStruct((t // tm, 8, tm), jnp.int32),
                   jax.ShapeDtypeStruct((t, 128), F32),
                   jax.ShapeDtypeStruct((1, 128), F32)],
        scratch_shapes=[pltpu.VMEM((1, 128), F32)] + [pltpu.VMEM((width, d), BF16)] * n_in,
        compiler_params=_cparams("arbitrary"),
        name="out_proj_router",
    )(*mixes, *([w_out] * n_in), *x_args, mod, g2.reshape(1, d), w_router, b_router)


def _dispatch_plan(meta, counts, n_experts, tile):
    t = meta.shape[0] * meta.shape[2]
    e1, e2, rank1, rank2 = (meta[:, k, :].reshape(t) for k in range(4))
    padded = ((counts + tile - 1) // tile) * tile
    ends = jnp.cumsum(padded)
    starts = ends - padded
    experts = jnp.arange(n_experts, dtype=jnp.int32)[None, :]
    pos = [jnp.sum(jnp.where(e[:, None] == experts, starts[None, :], 0), axis=1) + r
           for e, r in ((e1, rank1), (e2, rank2))]
    n_tiles = (2 * t) // tile + n_experts
    n_used = (ends[-1] // tile).astype(jnp.int32)
    tile_start = jnp.arange(n_tiles, dtype=jnp.int32) * tile
    probe = jnp.minimum(tile_start, ends[-1] - 1)
    tile_expert = jnp.sum((ends[None, :] <= probe[:, None]).astype(jnp.int32), axis=1)
    return tile_expert, n_used.reshape(1), ends.astype(jnp.int32), pos[0], pos[1], n_tiles


def _row_source_kernel(ends_ref, p1_ref, p2_ref, o_ref, *, n_tok, n_rows, tile, n_experts):
    def clear8(j, c):
        base = j * 8
        if n_rows <= 3 * n_tok:
            base = jnp.where(base >= 2 * n_tok, base - 2 * n_tok, jnp.where(base >= n_tok, base - n_tok, base))
        else:
            base = lax.rem(base, n_tok)
        for u in range(8):
            o_ref[j * 8 + u] = base + u
        return c

    def per_expert(e, c):
        first = lax.shift_right_logical(jnp.maximum(ends_ref[e] - tile, 0), 3)
        lax.fori_loop(first, first + tile // 8, clear8, 0)
        return c

    g = pl.program_id(0)

    @pl.when(g == 0)
    def _():
        lax.fori_loop(0, n_experts, per_expert, 0)
        lax.fori_loop(lax.shift_right_logical(ends_ref[n_experts - 1], 3), n_rows // 8, clear8, 0)

    chunk = p1_ref.shape[2]

    def place(r, c):
        t = g * chunk + r
        o_ref[p1_ref[0, 0, r]] = t
        o_ref[p2_ref[0, 0, r]] = t
        return c

    lax.fori_loop(0, chunk, place, 0, unroll=8)


def _row_source(ends, pos1, pos2, n_rows, tile):
    n_tok = pos1.shape[0]
    chunk = ROW_SOURCE_CHUNK
    smem = pl.BlockSpec(memory_space=pltpu.SMEM)
    pos_spec = pl.BlockSpec((1, 1, chunk), lambda g: (g, 0, 0), memory_space=pltpu.SMEM)
    return pl.pallas_call(
        functools.partial(_row_source_kernel, n_tok=n_tok, n_rows=n_rows, tile=tile, n_experts=ends.shape[0]),
        grid=(n_tok // chunk,),
        in_specs=[smem, pos_spec, pos_spec],
        out_specs=smem,
        out_shape=jax.ShapeDtypeStruct((n_rows,), jnp.int32),
        compiler_params=_cparams("arbitrary"),
        name="moe_row_source",
    )(ends, pos1.reshape(n_tok // chunk, 1, chunk), pos2.reshape(n_tok // chunk, 1, chunk))


def _moe_kernel(te_ref, nu_ref, *refs, tm, ahead):
    tok_refs = refs[:ahead + 1]
    h_hbm, wg_ref, wu_ref, wd_ref, o_ref, wg_bf, wu_bf, wd_bf, buf, sem = refs[ahead + 1:]
    i = pl.program_id(0)
    nslot = ahead + 1
    slot = lax.rem(i, nslot)
    n_used = nu_ref[0]

    def issue(tile, tok_ref):
        s = lax.rem(tile, nslot)

        def body(j, c):
            for u in range(2):
                r = 2 * j + u
                pltpu.make_async_copy(h_hbm.at[tok_ref[0, 0, r]], buf.at[s, r], sem.at[s]).start(priority=u)
            return c

        lax.fori_loop(0, tm // 2, body, 0, unroll=4)

    @pl.when(i == 0)
    def _():
        for j in range(ahead):
            @pl.when(j < n_used)
            def _():
                issue(jnp.int32(j), tok_refs[j])

    new_expert = jnp.logical_or(i == 0, te_ref[i] != te_ref[jnp.maximum(i - 1, 0)])

    @pl.when(new_expert)
    def _():
        wg_bf[...] = wg_ref[...].astype(BF16)
        wu_bf[...] = wu_ref[...].astype(BF16)
        wd_bf[...] = wd_ref[...].astype(BF16)

    def compute(issue_next):
        s_next = lax.rem(i + ahead, nslot)
        quarter = tm // 4

        def start_rows(part):
            if issue_next:
                for r in range(part * quarter, (part + 1) * quarter):
                    pltpu.make_async_copy(h_hbm.at[tok_refs[ahead][0, 0, r]], buf.at[s_next, r],
                                          sem.at[s_next]).start(priority=r % 2)

        pltpu.make_async_copy(h_hbm.at[pl.ds(0, tm)], buf.at[slot], sem.at[slot]).wait()
        start_rows(0)
        x = buf[slot].reshape(tm, -1)
        g = jnp.dot(x, wg_bf[...], preferred_element_type=F32)
        start_rows(1)
        u = jnp.dot(x, wu_bf[...], preferred_element_type=F32)
        start_rows(2)
        act = _silu(g) * u
        y = jnp.dot(act.astype(BF16), wd_bf[...], preferred_element_type=F32)
        start_rows(3)
        o_ref[...] = y.astype(BF16).reshape(o_ref.shape)

    @pl.when(i + ahead < n_used)
    def _():
        compute(True)

    @pl.when(jnp.logical_and(i + ahead >= n_used, i < n_used))
    def _():
        compute(False)

    @pl.when(i >= n_used)
    def _():
        o_ref[...] = jnp.zeros_like(o_ref)


def _moe_experts(h3, src_tok, tile_expert, n_used, w_gate, w_up, w_down, layer_idx, tm):
    t, sub, lanes = h3.shape
    d = sub * lanes
    n_tiles = src_tok.shape[0] // tm
    f = w_gate.shape[-1]
    ahead = MOE_GATHER_AHEAD
    tok = src_tok.reshape(n_tiles, 1, tm)

    def tok_spec(k):
        return pl.BlockSpec((1, 1, tm), lambda i, te, nu: (jnp.minimum(i + k, n_tiles - 1), 0, 0),
                            memory_space=pltpu.SMEM)

    return pl.pallas_call(
        functools.partial(_moe_kernel, tm=tm, ahead=ahead),
        grid_spec=pltpu.PrefetchScalarGridSpec(
            num_scalar_prefetch=2,
            grid=(n_tiles,),
            in_specs=[tok_spec(k) for k in range(ahead + 1)] + [
                      pl.BlockSpec(memory_space=pl.ANY),
                      pl.BlockSpec((None, None, d, f), lambda i, te, nu: (layer_idx, te[i], 0, 0)),
                      pl.BlockSpec((None, None, d, f), lambda i, te, nu: (layer_idx, te[i], 0, 0)),
                      pl.BlockSpec((None, None, f, d), lambda i, te, nu: (layer_idx, te[i], 0, 0))],
            out_specs=pl.BlockSpec((tm, sub, lanes), lambda i, te, nu: (i, 0, 0)),
            scratch_shapes=[pltpu.VMEM((d, f), BF16), pltpu.VMEM((d, f), BF16), pltpu.VMEM((f, d), BF16),
                            pltpu.VMEM((ahead + 1, tm, sub, lanes), BF16),
                            pltpu.SemaphoreType.DMA((ahead + 1,))]),
        out_shape=jax.ShapeDtypeStruct((n_tiles * tm, sub, lanes), BF16),
        compiler_params=_cparams("arbitrary"),
        name="moe_experts",
    )(tile_expert, n_used, *([tok] * (ahead + 1)), h3, w_gate, w_up, w_down)


def _combine_kernel(p1_ref, p2_ref, p1n_ref, p2n_ref, y_hbm, x_ref, ew_ref, mod_ref, gf_ref, *rest,
                    tm, n_steps, n_ctx_tiles, final_norm):
    out_refs, (buf, sem) = rest[:-2], rest[-2:]
    i = pl.program_id(0)
    slot = lax.rem(i, 2)

    def issue(s, pa_ref, pb_ref):
        def body(r, c):
            pltpu.make_async_copy(y_hbm.at[pa_ref[0, 0, r]], buf.at[s, 0, r], sem.at[s, 0]).start(priority=0)
            pltpu.make_async_copy(y_hbm.at[pb_ref[0, 0, r]], buf.at[s, 1, r], sem.at[s, 1]).start(priority=1)
            return c

        lax.fori_loop(0, tm, body, 0, unroll=8)

    @pl.when(i == 0)
    def _():
        issue(0, p1_ref, p2_ref)

    @pl.when(i + 1 < n_steps)
    def _():
        issue(1 - slot, p1n_ref, p2n_ref)

    for k in range(2):
        pltpu.make_async_copy(y_hbm.at[pl.ds(0, tm)], buf.at[slot, k], sem.at[slot, k]).wait()
    ew = ew_ref[...]
    y = (ew[:, 0:1] * buf[slot, 0].reshape(tm, -1).astype(F32)
         + ew[:, 1:2] * buf[slot, 1].reshape(tm, -1).astype(F32))
    x = x_ref[...] + mod_ref[0, 5:6, :] * y
    if not final_norm:
        out_refs[0][...] = x
    else:
        x = _rms(x, gf_ref[...])
        ctx_ref, lat_ref = out_refs

        @pl.when(i < n_ctx_tiles)
        def _():
            ctx_ref[...] = x

        @pl.when(i >= n_ctx_tiles)
        def _():
            lat_ref[...] = x


def _moe_combine(y_sorted, pos1, pos2, x, ew, mod, final_g, n_ctx_tiles, tiles_per_req, final_norm):
    t, d = x.shape
    tm = TOKEN_TILE
    n_steps = t // tm
    mrow = functools.partial(_tile_mod_row, n_ctx_tiles=n_ctx_tiles, tiles_per_req=tiles_per_req)
    smem_rows = pl.BlockSpec((1, 1, tm), lambda i: (i, 0, 0), memory_space=pltpu.SMEM)
    smem_next = pl.BlockSpec((1, 1, tm), lambda i: (jnp.minimum(i + 1, n_steps - 1), 0, 0),
                             memory_space=pltpu.SMEM)
    p1 = pos1.reshape(n_steps, 1, tm)
    p2 = pos2.reshape(n_steps, 1, tm)
    if final_norm:
        out_specs = [pl.BlockSpec((tm, d), lambda i: (jnp.minimum(i, n_ctx_tiles - 1), 0)),
                     pl.BlockSpec((tm, d), lambda i: (jnp.maximum(i - n_ctx_tiles, 0), 0))]
        out_shape = [jax.ShapeDtypeStruct((n_ctx_tiles * tm, d), F32),
                     jax.ShapeDtypeStruct((t - n_ctx_tiles * tm, d), F32)]
    else:
        out_specs = pl.BlockSpec((tm, d), lambda i: (i, 0))
        out_shape = jax.ShapeDtypeStruct((t, d), F32)
    return pl.pallas_call(
        functools.partial(_combine_kernel, tm=tm, n_steps=n_steps, n_ctx_tiles=n_ctx_tiles,
                          final_norm=final_norm),
        grid=(n_steps,),
        in_specs=[smem_rows, smem_rows, smem_next, smem_next,
                  pl.BlockSpec(memory_space=pl.ANY),
                  pl.BlockSpec((tm, d), lambda i: (i, 0)),
                  pl.BlockSpec((tm, 128), lambda i: (i, 0)),
                  pl.BlockSpec((1, 6, d), lambda i: (mrow(i), 0, 0)),
                  pl.BlockSpec((1, d), lambda i: (0, 0))],
        out_specs=out_specs,
        out_shape=out_shape,
        scratch_shapes=[pltpu.VMEM((2, 2, tm) + y_sorted.shape[1:], y_sorted.dtype),
                        pltpu.SemaphoreType.DMA((2, 2))],
        compiler_params=_cparams("arbitrary"),
        name="moe_combine",
    )(p1, p2, p1, p2, y_sorted, x, ew, mod, final_g.reshape(1, d))


def _axial_angles(seq, head_dim):
    rows = seq // GRID_W
    row = jnp.repeat(jnp.arange(rows), GRID_W).astype(F32)
    col = jnp.tile(jnp.arange(GRID_W), rows).astype(F32)
    n_freq = head_dim // 4
    inv = ROPE_BASE ** (-jnp.arange(n_freq, dtype=F32) / n_freq)
    return jnp.concatenate([row[:, None] * inv, col[:, None] * inv], axis=-1)


def kernel(x_prompt, x_sample, cache_attn_k, cache_attn_v, state_hgrn, state_ret, c, c_ctx, mod_w, mod_b, norm1_g, norm2_g, even_w_in, even_w_out, hgrn_lb_logits, hgrn_norm_g, attn_qn_g, attn_kn_g, odd_w_in, odd_w_out, ret_decay_logit, ret_gn_g, moe_w_group, moe_b_group, moe_w_expert, moe_b_expert, moe_w_gate, moe_w_up, moe_w_down, final_norm_g):
    batch, seq, d = x_prompt.shape
    dec_batch, dec_seq, _ = x_sample.shape
    depth = mod_w.shape[0]
    heads_a, val_a = hgrn_norm_g.shape[1:]
    key_a = hgrn_lb_logits.shape[2] // heads_a
    hd_b = attn_qn_g.shape[1]
    kv_b = cache_attn_k.shape[3]
    heads_b = (even_w_out.shape[1] - heads_a * val_a) // hd_b
    heads_c, val_c = ret_gn_g.shape[1:]
    key_c = state_ret.shape[4]
    n_experts = moe_w_expert.shape[2]
    n_groups = moe_w_group.shape[2]
    tm = TOKEN_TILE
    n_ctx = batch * seq
    n_ctx_tiles = n_ctx // tm
    tiles_per_req = dec_seq // tm
    assert seq % tm == 0 and dec_seq % tm == 0 and n_ctx % dec_seq == 0
    assert key_a == 128 and val_a == 128 and hd_b == 128 and n_groups + n_experts <= 128
    assert d % 128 == 0 and seq % RET_CHUNK == 0 and dec_seq % RET_CHUNK == 0 and dec_seq % GRID_W == 0
    n_tok = n_ctx + dec_batch * dec_seq
    assert (2 * n_tok) % EXPERT_TILE == 0 and n_tok % ROW_SOURCE_CHUNK == 0

    x = (x_prompt.reshape(n_ctx, d), x_sample.reshape(dec_batch * dec_seq, d))
    t = n_ctx + dec_batch * dec_seq

    n_vec = 1 + dec_batch
    rows = -(-n_vec // 8) * 8
    cvecs = jnp.zeros((rows, d), F32).at[0].set(c_ctx).at[1:n_vec].set(c)
    mods = _ada_mod(cvecs, mod_w, mod_b).reshape(depth, rows, 6, d)

    lower_bounds = jnp.cumsum(jax.nn.softmax(hgrn_lb_logits.astype(F32), axis=0), axis=0)
    log_gamma = jax.nn.log_sigmoid(ret_decay_logit.astype(F32))

    ang_b = _axial_angles(dec_seq, hd_b)
    cos_b = jnp.concatenate([jnp.cos(ang_b), jnp.cos(ang_b)], axis=-1)
    sin_b = jnp.concatenate([-jnp.sin(ang_b), jnp.sin(ang_b)], axis=-1)
    ang_c = _axial_angles(dec_seq, key_c)
    cos_c, sin_c = jnp.cos(ang_c), jnp.sin(ang_c)

    pad = 128 - n_groups - n_experts
    new_k = new_v = new_sh = new_sr = None
    assert depth == 2, "one HGRN2/GQA layer followed by one retention layer"
    for l in range(depth):
        mod = mods[l]
        if l % 2 == 0:
            e = l // 2
            na, nva, nqb, nkv = heads_a * key_a, heads_a * val_a, heads_b * hd_b, kv_b * hd_b
            o_zf, o_kb = na + nva, 3 * na + 2 * nva + nqb
            w_in = even_w_in[e].astype(BF16)
            proj16, proj32 = _in_proj(x, norm1_g[l], mod, w_in, t, n_ctx_tiles, tiles_per_req, w_in.shape[1],
                                      f32_cols=((o_zf, o_zf + 2 * na), (o_kb, o_kb + 2 * nkv)))
            lb = lower_bounds[l]
            oa, s_ctx = _hgrn(proj16, proj32, lb, hgrn_norm_g[e], None, e, None, batch, seq, 0,
                              heads_a, key_a, val_a)
            oa, _ = _hgrn(proj16, proj32, lb, hgrn_norm_g[e], state_hgrn, e, oa, dec_batch, dec_seq,
                          n_ctx // dec_seq, heads_a, key_a, val_a)
            col_q, col_kv = (na + 2 * nva) // hd_b, (2 * na) // hd_b
            ob, kc, vc = _attn(proj16, proj32, attn_qn_g[e], attn_kn_g[e], None, e, None, batch, seq, 0,
                               heads_b, kv_b, hd_b, col_q, col_kv)
            (ob,) = _attn(proj16, proj32, attn_qn_g[e], attn_kn_g[e],
                          (cache_attn_k, cache_attn_v, cos_b, sin_b), e, ob,
                          dec_batch, dec_seq, n_ctx // dec_seq, heads_b, kv_b, hd_b, col_q, col_kv)
            new_k = kc.reshape(batch, 1, seq, kv_b, hd_b)
            new_v = vc.reshape(batch, 1, seq, kv_b, hd_b)
            new_sh = s_ctx[:, None]
            w_out = even_w_out[e]
            mixes = [oa, ob]
        else:
            o = l // 2
            w_in = odd_w_in[o]
            proj = _in_proj(x, norm1_g[l], mod, w_in, t, n_ctx_tiles, tiles_per_req, w_in.shape[1] // 2)
            y, s_ctx = _retention(proj, log_gamma[o], ret_gn_g[o], None, o, None, batch, seq, 0,
                                  heads_c, key_c, val_c)
            y, _ = _retention(proj, log_gamma[o], ret_gn_g[o], (state_ret, cos_c, sin_c), o, y,
                              dec_batch, dec_seq, n_ctx // dec_seq, heads_c, key_c, val_c)
            new_sr = s_ctx[:, None]
            w_out = odd_w_out[o]
            mixes = [y]

        w_router = jnp.concatenate([moe_w_group[l], moe_w_expert[l], jnp.zeros((d, pad), F32)], axis=1)
        w_router_hi = w_router.astype(BF16)
        w_router = jnp.concatenate([w_router_hi, (w_router - w_router_hi.astype(F32)).astype(BF16)], axis=1)
        b_router = jnp.concatenate([moe_b_group[l], moe_b_expert[l], jnp.zeros((pad,), F32)]).reshape(1, 128)
        xn, h2, meta, ew, cnt = _out_proj(mixes, w_out, x, mod, norm2_g[l], w_router, b_router,
                                          n_ctx_tiles, tiles_per_req, n_groups, n_experts)
        counts = cnt[0, n_groups:n_groups + n_experts].astype(jnp.int32)
        tile_expert, n_used, ends, pos1, pos2, n_tiles = _dispatch_plan(meta, counts, n_experts, EXPERT_TILE)
        src_tok = _row_source(ends, pos1, pos2, n_tiles * EXPERT_TILE, EXPERT_TILE)
        y_sorted = _moe_experts(h2, src_tok, tile_expert, n_used, moe_w_gate, moe_w_up, moe_w_down, l, EXPERT_TILE)
        x = _moe_combine(y_sorted, pos1, pos2, xn, ew, mod, final_norm_g, n_ctx_tiles, tiles_per_req,
                         final_norm=(l == depth - 1))

    y_prompt = x[0].reshape(batch, seq, d)
    y_sample = x[1].reshape(dec_batch, dec_seq, d)
    return (y_prompt, y_sample, new_k, new_v, new_sh, new_sr)
```

```python
import functools

import numpy as np
import jax
import jax.numpy as jnp
from jax import lax
from jax.experimental import pallas as pl
from jax.experimental.pallas import tpu as pltpu

F32 = jnp.float32
BF16 = jnp.bfloat16
EPS = 1e-6
ROPE_BASE = 10000.0
GRID_W = 64
HGRN_BLOCK = 128
HGRN_DIAG = 8
RET_CHUNK = 256
TOKEN_TILE = 256
INPROJ_TILE = 512
EXPERT_TILE = 256
MOE_GATHER_AHEAD = 2
ROW_SOURCE_CHUNK = 512
V7X_VMEM_LIMIT_BYTES = 56 * 1024 * 1024
HIGHEST = lax.Precision.HIGHEST
LOG2_E = 1.4426950408889634

_NT = (((1,), (1,)), ((), ()))


def _cparams(*sem):
    return pltpu.CompilerParams(dimension_semantics=sem, vmem_limit_bytes=V7X_VMEM_LIMIT_BYTES)


def _rms(x, g):
    return x * lax.rsqrt(jnp.mean(x * x, axis=-1, keepdims=True) + EPS) * g


def _silu(x):
    return x * jax.nn.sigmoid(x)


def _bdot(a, b):
    return jnp.dot(a.astype(BF16), b.astype(BF16), preferred_element_type=F32)


def _bdot_nt(a, b):
    return lax.dot_general(a.astype(BF16), b.astype(BF16), _NT, preferred_element_type=F32)


def _mod_kernel(c_ref, w_ref, b_ref, o_ref):
    a = _silu(c_ref[...])
    o_ref[0] = jnp.dot(a, w_ref[0], precision=HIGHEST, preferred_element_type=F32) + b_ref[0]


def _ada_mod(cvecs, mod_w, mod_b):
    depth, d, n = mod_w.shape
    tn = n // 4
    rows = cvecs.shape[0]
    return pl.pallas_call(
        _mod_kernel,
        grid=(depth, n // tn),
        in_specs=[pl.BlockSpec((rows, d), lambda l, j: (0, 0)),
                  pl.BlockSpec((1, d, tn), lambda l, j: (l, 0, j)),
                  pl.BlockSpec((1, 1, tn), lambda l, j: (l, 0, j))],
        out_specs=pl.BlockSpec((1, rows, tn), lambda l, j: (l, 0, j)),
        out_shape=jax.ShapeDtypeStruct((depth, rows, n), F32),
        compiler_params=_cparams("arbitrary", "arbitrary"),
        name="ada_mod",
    )(cvecs, mod_w, mod_b.reshape(depth, 1, n))


def _tile_mod_row(i, n_ctx_tiles, tiles_per_req):
    return jnp.where(i < n_ctx_tiles, 0, 1 + jnp.maximum(i - n_ctx_tiles, 0) // tiles_per_req)


def _token_specs(x, tm, n_ctx_tiles, step_of):
    if not isinstance(x, tuple):
        return [pl.BlockSpec((tm, x.shape[1]), lambda *g: (step_of(*g), 0))], (x,)
    d = x[0].shape[1]
    return ([pl.BlockSpec((tm, d), lambda *g: (jnp.minimum(step_of(*g), n_ctx_tiles - 1), 0)),
             pl.BlockSpec((tm, d), lambda *g: (jnp.maximum(step_of(*g) - n_ctx_tiles, 0), 0))], x)


def _token_rows(x_refs, i, n_ctx_tiles):
    if len(x_refs) == 1:
        return x_refs[0][...]
    return jnp.where(i < n_ctx_tiles, x_refs[0][...], x_refs[1][...])


def _inproj_kernel(*refs, n_x, n_ctx_tiles, f32_cols, cast_w):
    x_refs = refs[:n_x]
    g_ref, mod_ref, w_ref = refs[n_x:n_x + 3]
    out_refs = refs[n_x + 3:]
    if cast_w:
        out_refs, w_bf = out_refs[:-1], out_refs[-1]

        @pl.when(pl.program_id(1) == 0)
        def _():
            w_bf[...] = w_ref[...].astype(BF16)

        w_ref = w_bf
    x = _token_rows(x_refs, pl.program_id(1), n_ctx_tiles)
    h = _rms(x, g_ref[...]) * (1.0 + mod_ref[0, 1:2, :]) + mod_ref[0, 0:1, :]
    acc = jnp.dot(h.astype(BF16), w_ref[...], preferred_element_type=F32)
    if not f32_cols:
        out_refs[0][...] = acc.astype(BF16)
        return
    n = acc.shape[1]
    at16 = at32 = start = 0
    for lo, hi in tuple(f32_cols) + ((n, n),):
        if lo > start:
            out_refs[0][:, at16:at16 + lo - start] = acc[:, start:lo].astype(BF16)
            at16 += lo - start
        if hi > lo:
            out_refs[1][:, at32:at32 + hi - lo] = acc[:, lo:hi]
            at32 += hi - lo
        start = hi


def _in_proj(x, g, mod, w_bf, t, n_ctx_tiles, tiles_per_req, tn, f32_cols=()):
    d, n = w_bf.shape
    cast_w = w_bf.dtype != BF16
    tm = INPROJ_TILE
    scale = INPROJ_TILE // TOKEN_TILE
    assert n_ctx_tiles % scale == 0 and tiles_per_req % scale == 0
    n_ctx_tiles, tiles_per_req = n_ctx_tiles // scale, tiles_per_req // scale
    mrow = functools.partial(_tile_mod_row, n_ctx_tiles=n_ctx_tiles, tiles_per_req=tiles_per_req)
    x_specs, x_args = _token_specs(x, tm, n_ctx_tiles, lambda j, i: i)
    if not f32_cols:
        out_specs = pl.BlockSpec((tm, tn), lambda j, i: (i, j))
        out_shape = jax.ShapeDtypeStruct((t, n), BF16)
    else:
        assert tn == n
        n32 = sum(hi - lo for lo, hi in f32_cols)
        out_specs = [pl.BlockSpec((tm, n - n32), lambda j, i: (i, 0)),
                     pl.BlockSpec((tm, n32), lambda j, i: (i, 0))]
        out_shape = [jax.ShapeDtypeStruct((t, n - n32), BF16), jax.ShapeDtypeStruct((t, n32), F32)]
    return pl.pallas_call(
        functools.partial(_inproj_kernel, n_x=len(x_args), n_ctx_tiles=n_ctx_tiles, f32_cols=tuple(f32_cols),
                          cast_w=cast_w),
        grid=(n // tn, t // tm),
        in_specs=x_specs + [pl.BlockSpec((1, d), lambda j, i: (0, 0)),
                            pl.BlockSpec((1, 6, d), lambda j, i: (mrow(i), 0, 0)),
                            pl.BlockSpec((d, tn), lambda j, i: (0, j))],
        out_specs=out_specs,
        out_shape=out_shape,
        scratch_shapes=[pltpu.VMEM((d, tn), BF16)] if cast_w else [],
        compiler_params=_cparams("arbitrary", "arbitrary"),
        name="in_proj",
    )(*x_args, g.reshape(1, d), mod, w_bf)


def _hgrn_constants(nb, dg):
    r = np.arange(nb)[:, None]
    c = np.arange(nb)[None, :]
    levels = []
    m = nb // 2
    while m >= dg:
        levels.append(m)
        m //= 2
    owner = np.full((2, nb, nb), len(levels) + 1, np.int32)
    qrows = np.zeros((2, len(levels), nb, nb), np.int32)
    for d in (0, 1):
        for idx, m in enumerate(levels):
            same = (r // (2 * m)) == (c // (2 * m))
            r_hi = (r % (2 * m)) >= m
            c_hi = (c % (2 * m)) >= m
            later_q = r_hi if d == 0 else ~r_hi
            owner[d][same & later_q & (~c_hi if d == 0 else c_hi)] = idx
            qrows[d, idx] = np.broadcast_to(later_q, (nb, nb))
        owner[d][(r // dg) == (c // dg)] = len(levels)
    tri = np.stack([c <= r, c >= r]).astype(np.float32)
    return levels, jnp.asarray(tri, dtype=BF16), jnp.asarray(owner), jnp.asarray(qrows)


def _hgrn_decay(z, lb, tri):
    logf = jnp.log(lb + (1.0 - lb) * jax.nn.sigmoid(z))
    k = (1.0 - lb) * jax.nn.sigmoid(-z)
    l2 = logf * LOG2_E
    hi = l2.astype(BF16)
    lo = (l2 - hi.astype(F32)).astype(BF16)
    c = jnp.dot(tri, hi, preferred_element_type=F32) + jnp.dot(tri, lo, preferred_element_type=F32)
    return c, k


def _hgrn_block(q, v, st_t, reverse, c_ref, k_ref, ecat, levels, owner, qrows_ref):
    nb = HGRN_BLOCK
    c = c_ref[...]
    k = k_ref[...]
    edge = 0 if reverse else nb - 1
    c_edge = c_ref[edge:edge + 1, :]

    a = jnp.zeros((nb, nb), F32)
    for idx, m in enumerate(levels):
        pieces = []
        for grp in range(nb // (2 * m)):
            edge_row = grp * 2 * m + (m if reverse else m - 1)
            pieces.append(jnp.broadcast_to(c_ref[edge_row:edge_row + 1, :], (2 * m, nb)))
        bnd = pieces[0] if len(pieces) == 1 else jnp.concatenate(pieces, axis=0)
        e = jnp.exp2(-jnp.abs(c - bnd))
        z = (jnp.where(qrows_ref[idx] != 0, q, k) * e).astype(BF16)
        am = lax.dot_general(z, z, _NT, preferred_element_type=F32)
        a = jnp.where(owner == idx, am, a)

    dg = HGRN_DIAG
    rowi = lax.broadcasted_iota(jnp.int32, (dg, nb), 0)
    cols = []
    for s in range(dg):
        keep = (rowi <= s) if reverse else (rowi >= s)
        tiles = []
        for j in range(nb // dg):
            r = j * dg + s
            qj = q[j * dg:(j + 1) * dg, :]
            cj = c[j * dg:(j + 1) * dg, :]
            cs = jnp.broadcast_to(c_ref[r:r + 1, :], (dg, nb))
            ks = jnp.broadcast_to(k_ref[r:r + 1, :], (dg, nb))
            tiles.append(qj * ks * jnp.exp2(jnp.where(keep, cj - cs, -jnp.inf)))
        cols.append(jnp.concatenate(tiles, axis=0).astype(BF16))
    p = jnp.concatenate(cols, axis=1)
    red = jnp.dot(p, ecat, preferred_element_type=F32)
    a = jnp.where(owner == len(levels), red, a)
    o = _bdot_nt(q * jnp.exp2(c), st_t) + _bdot(a, v)
    kd = k * jnp.exp2(c_edge - c)
    st_new = st_t * jnp.exp2(c_edge) + _bdot(v.T, kd)
    return o, st_new


def _hgrn_kernel(*refs, nblk, has_state, qscale, levels):
    (qa_ref, ia_ref, zf_ref, zb_ref, ga_ref, lb_ref, ng_ref, ecat_ref, tri_ref, owner_ref,
     qrows_ref) = refs[:11]
    s0_ref = refs[11] if has_state else None
    o_ref, sfin_ref, oacc_ref, c_scr, k_scr = refs[-5:]
    nb = HGRN_BLOCK

    def gates(direction, blk, slot):
        z_ref = zb_ref if direction == 1 else zf_ref
        r0 = blk * nb if isinstance(blk, int) else pl.multiple_of(blk * nb, nb)
        c, k = _hgrn_decay(z_ref[pl.ds(r0, nb), :], lb_ref[direction:direction + 1, :], tri_ref[direction])
        c_scr[direction, slot] = c
        k_scr[direction, slot] = k

    def block(direction, blk, slot, st_t):
        r0 = blk * nb if isinstance(blk, int) else pl.multiple_of(blk * nb, nb)
        q = _silu(qa_ref[pl.ds(r0, nb), :].astype(F32)) * qscale
        v = ia_ref[pl.ds(r0, nb), :].astype(F32)
        o, st_new = _hgrn_block(q, v, st_t, direction == 1, c_scr.at[direction, slot],
                                k_scr.at[direction, slot], ecat_ref[...], levels, owner_ref[direction],
                                qrows_ref.at[direction])
        oacc_ref[direction, pl.ds(r0, nb), :] = o
        return st_new

    gates(0, 0, 0)
    gates(1, nblk - 1, 0)

    def body(i, carry):
        st_f, st_b = carry
        slot = lax.rem(i, 2)
        st_f = block(0, i, slot, st_f)
        st_b = block(1, nblk - 1 - i, slot, st_b)
        gates(0, i + 1, 1 - slot)
        gates(1, nblk - 2 - i, 1 - slot)
        return st_f, st_b

    if has_state:
        st0 = (s0_ref[0].T, s0_ref[1].T)
    else:
        zero = jnp.zeros((ia_ref.shape[1], qa_ref.shape[1]), F32)
        st0 = (zero, zero)
    st_f, st_b = lax.fori_loop(0, nblk - 1, body, st0)
    last_slot = (nblk - 1) % 2
    st_f = block(0, nblk - 1, last_slot, st_f)
    st_b = block(1, 0, last_slot, st_b)
    sfin_ref[0] = st_f.T
    sfin_ref[1] = st_b.T

    o = oacc_ref[0] + oacc_ref[1]
    o_ref[...] = (_rms(o, ng_ref[0]) * _silu(ga_ref[...].astype(F32))).astype(o_ref.dtype)


def _hgrn(proj16, proj32, lb, norm_g, s0, layer_idx, out_prev, n_seq, seq, row_blk0, heads, key_d, val_d):
    t = proj16.shape[0]
    has_state = s0 is not None
    hk = heads * key_d // 128
    off_qa, off_ia, off_ga = 0, hk, 2 * hk
    off_zf, off_zb = 0, hk

    def col(off):
        return pl.BlockSpec((seq, 128), lambda b, h: (row_blk0 + b, off + h))

    lane = np.arange(key_d)[None, :]
    srow = np.repeat(np.arange(HGRN_DIAG), key_d)[:, None]
    ecat = jnp.asarray((lane % HGRN_DIAG == srow).astype(np.float32), dtype=BF16)
    levels, tri, owner, qrows = _hgrn_constants(HGRN_BLOCK, HGRN_DIAG)
    in_specs = [col(off_qa), col(off_ia), col(off_zf), col(off_zb), col(off_ga),
                pl.BlockSpec((2, key_d), lambda b, h: (0, h)),
                pl.BlockSpec((1, 1, val_d), lambda b, h: (h, 0, 0)),
                pl.BlockSpec((HGRN_DIAG * key_d, key_d), lambda b, h: (0, 0)),
                pl.BlockSpec(tri.shape, lambda b, h: (0, 0, 0)),
                pl.BlockSpec(owner.shape, lambda b, h: (0, 0, 0)),
                pl.BlockSpec(qrows.shape, lambda b, h: (0, 0, 0, 0))]
    args = [proj16, proj16, proj32, proj32, proj16, lb, norm_g.reshape(heads, 1, val_d), ecat, tri, owner, qrows]
    if has_state:
        in_specs.append(pl.BlockSpec((None, None, 2, None, key_d, val_d),
                                     lambda b, h: (b, layer_idx, 0, h, 0, 0)))
        args.append(s0)
    aliases = {}
    if out_prev is not None:
        in_specs.append(pl.BlockSpec(memory_space=pl.ANY))
        args.append(out_prev)
        aliases = {len(args) - 1: 0}

    kern = functools.partial(_hgrn_kernel, nblk=seq // HGRN_BLOCK, has_state=has_state,
                             qscale=float(key_d) ** -0.5, levels=tuple(levels))
    return pl.pallas_call(
        kern,
        grid=(n_seq, heads),
        in_specs=in_specs,
        out_specs=[pl.BlockSpec((seq, val_d), lambda b, h: (row_blk0 + b, h)),
                   pl.BlockSpec((None, 2, None, key_d, val_d), lambda b, h: (b, 0, h, 0, 0))],
        out_shape=[jax.ShapeDtypeStruct((t, heads * val_d), BF16),
                   jax.ShapeDtypeStruct((n_seq, 2, heads, key_d, val_d), F32)],
        scratch_shapes=[pltpu.VMEM((2, seq, val_d), F32),
                        pltpu.VMEM((2, 2, HGRN_BLOCK, key_d), F32),
                        pltpu.VMEM((2, 2, HGRN_BLOCK, key_d), F32)],
        input_output_aliases=aliases,
        compiler_params=_cparams("arbitrary", "arbitrary"),
        name="hgrn2",
    )(*args)


def _rope(x, cos, sin_signed):
    return x * cos + pltpu.roll(x, x.shape[-1] // 2, 1) * sin_signed


def _attn_kernel(*refs, has_ctx, grp, qblk, scale):
    if has_ctx:
        (q_ref, k_ref, v_ref, qn_ref, kn_ref, ck_ref, cv_ref, cos_ref, sin_ref, _prev, o_ref) = refs
    else:
        (q_ref, k_ref, v_ref, qn_ref, kn_ref, o_ref, ko_ref, vo_ref) = refs
    seq, hd = k_ref.shape
    k = _rms(k_ref[...], kn_ref[...])
    v = v_ref[...]
    if has_ctx:
        k = _rope(k, cos_ref[...], sin_ref[...])
        ck = ck_ref[...].astype(BF16)
        cv = cv_ref[...].astype(BF16)
    else:
        head = pl.program_id(1)
        ko_ref[:, pl.ds(head, 1), :] = k[:, None, :]
        vo_ref[:, pl.ds(head, 1), :] = v[:, None, :]
    kb = k.astype(BF16)
    vb = v.astype(BF16)

    def body(i, carry):
        r0 = pl.multiple_of(i * qblk, qblk)
        for g in range(grp):
            q = _rms(q_ref[pl.ds(r0, qblk), g * hd:(g + 1) * hd].astype(F32), qn_ref[...])
            if has_ctx:
                q = _rope(q, cos_ref[pl.ds(r0, qblk), :], sin_ref[pl.ds(r0, qblk), :])
            qb = q.astype(BF16)
            s2 = lax.dot_general(qb, kb, _NT, preferred_element_type=F32) * scale
            m = jnp.max(s2, axis=-1, keepdims=True)
            if has_ctx:
                s1 = lax.dot_general(qb, ck, _NT, preferred_element_type=F32) * scale
                m = jnp.maximum(m, jnp.max(s1, axis=-1, keepdims=True))
                p1 = jnp.exp(s1 - m)
            p2 = jnp.exp(s2 - m)
            den = jnp.sum(p2, axis=-1, keepdims=True)
            acc = jnp.dot(p2.astype(BF16), vb, preferred_element_type=F32)
            if has_ctx:
                den = den + jnp.sum(p1, axis=-1, keepdims=True)
                acc = acc + jnp.dot(p1.astype(BF16), cv, preferred_element_type=F32)
            o_ref[pl.ds(r0, qblk), g * hd:(g + 1) * hd] = (acc / den).astype(o_ref.dtype)
        return carry

    lax.fori_loop(0, seq // qblk, body, 0)


def _attn(proj16, proj32, qn_g, kn_g, ctx, layer_idx, out_prev, n_seq, seq, row_blk0, heads, kv_heads, hd,
          col_q, col_kv):
    t = proj16.shape[0]
    grp = heads // kv_heads
    has_ctx = ctx is not None
    cq = col_q // grp
    ck0 = col_kv
    cv0 = ck0 + kv_heads
    in_specs = [pl.BlockSpec((seq, grp * hd), lambda b, h: (row_blk0 + b, cq + h)),
                pl.BlockSpec((seq, hd), lambda b, h: (row_blk0 + b, ck0 + h)),
                pl.BlockSpec((seq, hd), lambda b, h: (row_blk0 + b, cv0 + h)),
                pl.BlockSpec((1, hd), lambda b, h: (0, 0)),
                pl.BlockSpec((1, hd), lambda b, h: (0, 0))]
    args = [proj16, proj32, proj32, qn_g.reshape(1, hd), kn_g.reshape(1, hd)]
    out_specs = [pl.BlockSpec((seq, grp * hd), lambda b, h: (row_blk0 + b, h))]
    out_shape = [jax.ShapeDtypeStruct((t, heads * hd), BF16)]
    aliases = {}
    if has_ctx:
        cache_k, cache_v, cos, sin_signed = ctx
        nreq, nlay, past = cache_k.shape[:3]
        cache_k = cache_k.reshape(nreq, nlay, past, kv_heads * hd)
        cache_v = cache_v.reshape(nreq, nlay, past, kv_heads * hd)
        spec_c = pl.BlockSpec((None, None, past, hd), lambda b, h: (b, layer_idx, 0, h))
        in_specs += [spec_c, spec_c,
                     pl.BlockSpec((seq, hd), lambda b, h: (0, 0)),
                     pl.BlockSpec((seq, hd), lambda b, h: (0, 0)),
                     pl.BlockSpec(memory_space=pl.ANY)]
        args += [cache_k, cache_v, cos, sin_signed, out_prev]
        aliases = {len(args) - 1: 0}
    else:
        out_specs += [pl.BlockSpec((None, seq, kv_heads, hd), lambda b, h: (b, 0, 0, 0))] * 2
        out_shape += [jax.ShapeDtypeStruct((n_seq, seq, kv_heads, hd), F32)] * 2
    return pl.pallas_call(
        functools.partial(_attn_kernel, has_ctx=has_ctx, grp=grp, qblk=min(seq, 256), scale=float(hd) ** -0.5),
        grid=(n_seq, kv_heads),
        in_specs=in_specs,
        out_specs=out_specs,
        out_shape=out_shape,
        input_output_aliases=aliases,
        compiler_params=_cparams("arbitrary", "arbitrary"),
        name="gqa",
    )(*args)


def _ret_kernel(*refs, nchunk, has_ctx, kscale):
    if has_ctx:
        (lg_ref, q_ref, k_ref, v_ref, g_ref, gn_ref, s0_ref, cos_ref, sin_ref, _prev,
         o_ref, sfin_ref, sb_ref) = refs
    else:
        (lg_ref, q_ref, k_ref, v_ref, g_ref, gn_ref, o_ref, sfin_ref, sb_ref) = refs
    cs = RET_CHUNK
    kd_dim = q_ref.shape[1]
    h = pl.program_id(1)
    lgf = lg_ref[0, h]
    lgb = lg_ref[1, h]
    row = lax.broadcasted_iota(jnp.int32, (cs, cs), 0)
    col = lax.broadcasted_iota(jnp.int32, (cs, cs), 1)
    rel = (row - col).astype(F32)
    dmat = (jnp.exp(jnp.where(rel >= 0, rel * lgf, -jnp.inf))
            + jnp.exp(jnp.where(rel <= 0, -rel * lgb, -jnp.inf)))
    j = lax.broadcasted_iota(jnp.int32, (cs, 1), 0).astype(F32)
    one = jnp.ones((1, 1), F32)
    dec_f = jnp.exp(one * (cs * lgf))
    dec_b = jnp.exp(one * (cs * lgb))

    def load_qk(ref, r0, mul):
        x = ref[pl.ds(r0, cs), :].astype(F32) * mul
        if has_ctx:
            half = kd_dim // 2
            x1, x2 = x[:, :half], x[:, half:]
            cos = cos_ref[pl.ds(r0, cs), :]
            sin = sin_ref[pl.ds(r0, cs), :]
            x = jnp.concatenate([x1 * cos - x2 * sin, x2 * cos + x1 * sin], axis=1)
        return x

    def bwd_body(i, st):
        n = nchunk - 1 - i
        r0 = pl.multiple_of(n * cs, cs)
        sb_ref[n] = st
        k = load_qk(k_ref, r0, kscale)
        v = v_ref[pl.ds(r0, cs), :]
        return st * dec_b + _bdot((k * jnp.exp(j * lgb)).T, v)

    st_b0 = s0_ref[1] if has_ctx else jnp.zeros(sfin_ref.shape[1:], F32)
    st_b = lax.fori_loop(0, nchunk, bwd_body, st_b0)
    sfin_ref[1] = st_b

    def fwd_body(n, st):
        r0 = pl.multiple_of(n * cs, cs)
        q = load_qk(q_ref, r0, 1.0)
        k = load_qk(k_ref, r0, kscale)
        v = v_ref[pl.ds(r0, cs), :]
        a = _bdot_nt(q, k) * dmat
        o = _bdot(a, v)
        if has_ctx or nchunk > 1:
            o = o + _bdot(q * jnp.exp((j + 1.0) * lgf), st)
            o = o + _bdot(q * jnp.exp((cs - j) * lgb), sb_ref[n])
        mu = jnp.mean(o, axis=-1, keepdims=True)
        var = jnp.mean(jnp.square(o - mu), axis=-1, keepdims=True)
        y = (o - mu) * lax.rsqrt(var + EPS) * gn_ref[0]
        o_ref[pl.ds(r0, cs), :] = (_silu(g_ref[pl.ds(r0, cs), :].astype(F32)) * y).astype(o_ref.dtype)
        return st * dec_f + _bdot((k * jnp.exp((cs - 1.0 - j) * lgf)).T, v)

    st_f0 = s0_ref[0] if has_ctx else jnp.zeros(sfin_ref.shape[1:], F32)
    sfin_ref[0] = lax.fori_loop(0, nchunk, fwd_body, st_f0)


def _retention(proj, log_gamma, gn_g, ctx, layer_idx, out_prev, n_seq, seq, row_blk0, heads, key_d, val_d):
    t = proj.shape[0]
    has_ctx = ctx is not None
    nq = heads * key_d
    cq, ck, cv, cg = 0, nq // key_d, 2 * nq // val_d, (2 * nq + heads * val_d) // val_d
    in_specs = [pl.BlockSpec(memory_space=pltpu.SMEM),
                pl.BlockSpec((seq, key_d), lambda b, h: (row_blk0 + b, cq + h)),
                pl.BlockSpec((seq, key_d), lambda b, h: (row_blk0 + b, ck + h)),
                pl.BlockSpec((seq, val_d), lambda b, h: (row_blk0 + b, cv + h)),
                pl.BlockSpec((seq, val_d), lambda b, h: (row_blk0 + b, cg + h)),
                pl.BlockSpec((1, 1, val_d), lambda b, h: (h, 0, 0))]
    args = [log_gamma, proj, proj, proj, proj, gn_g.reshape(heads, 1, val_d)]
    aliases = {}
    if has_ctx:
        s0, cos, sin = ctx
        in_specs += [pl.BlockSpec((None, None, 2, None, key_d, val_d), lambda b, h: (b, layer_idx, 0, h, 0, 0)),
                     pl.BlockSpec((seq, key_d // 2), lambda b, h: (0, 0)),
                     pl.BlockSpec((seq, key_d // 2), lambda b, h: (0, 0)),
                     pl.BlockSpec(memory_space=pl.ANY)]
        args += [s0, cos, sin, out_prev]
        aliases = {len(args) - 1: 0}
    nchunk = seq // RET_CHUNK
    return pl.pallas_call(
        functools.partial(_ret_kernel, nchunk=nchunk, has_ctx=has_ctx, kscale=float(key_d) ** -0.5),
        grid=(n_seq, heads),
        in_specs=in_specs,
        out_specs=[pl.BlockSpec((seq, val_d), lambda b, h: (row_blk0 + b, h)),
                   pl.BlockSpec((None, 2, None, key_d, val_d), lambda b, h: (b, 0, h, 0, 0))],
        out_shape=[jax.ShapeDtypeStruct((t, heads * val_d), BF16),
                   jax.ShapeDtypeStruct((n_seq, 2, heads, key_d, val_d), F32)],
        scratch_shapes=[pltpu.VMEM((nchunk, key_d, val_d), F32)],
        input_output_aliases=aliases,
        compiler_params=_cparams("arbitrary", "arbitrary"),
        name="retention",
    )(*args)


def _outproj_kernel(*refs, n_in, n_x, n_ctx_tiles, n_groups, n_experts):
    mix_refs = refs[:n_in]
    w_refs = refs[n_in:2 * n_in]
    x_refs = refs[2 * n_in:2 * n_in + n_x]
    base = 2 * n_in + n_x
    (mod_ref, g2_ref, wr_ref, br_ref, xn_ref, h2_ref, meta_ref, ew_ref, cnt_out_ref,
     cnt_ref) = refs[base:base + 10]
    wbf_refs = refs[base + 10:]

    @pl.when(pl.program_id(0) == 0)
    def _():
        for wref, wbf in zip(w_refs, wbf_refs):
            wbf[...] = wref[...].astype(BF16)

    acc = jnp.dot(mix_refs[0][...], wbf_refs[0][...], preferred_element_type=F32)
    for mref, wbf in zip(mix_refs[1:], wbf_refs[1:]):
        acc = acc + jnp.dot(mref[...], wbf[...], preferred_element_type=F32)
    xn = _token_rows(x_refs, pl.program_id(0), n_ctx_tiles) + mod_ref[0, 2:3, :] * acc
    xn_ref[...] = xn
    h2 = _rms(xn, g2_ref[...]) * (1.0 + mod_ref[0, 4:5, :]) + mod_ref[0, 3:4, :]
    h2_ref[...] = h2.astype(BF16).reshape(h2_ref.shape)
    h_hi = h2.astype(BF16)
    h_lo = (h2 - h_hi.astype(F32)).astype(BF16)
    nl = br_ref.shape[1]
    hh = jnp.dot(h_hi, wr_ref[...], preferred_element_type=F32)
    lg = hh[:, :nl] + hh[:, nl:] + jnp.dot(h_lo, wr_ref[:, :nl], preferred_element_type=F32) + br_ref[...]

    per_grp = n_experts // n_groups
    lane = lax.broadcasted_iota(jnp.int32, lg.shape, 1).astype(F32)
    big = float(lg.shape[1])
    ninf = -jnp.inf
    gl = jnp.where(lane < n_groups, lg, ninf)
    gmax = jnp.max(gl, axis=-1, keepdims=True)
    p_grp = 1.0 / jnp.sum(jnp.exp(gl - gmax), axis=-1, keepdims=True)
    gidx = jnp.min(jnp.where(gl == gmax, lane, big), axis=-1, keepdims=True)
    in_grp = ((lane >= n_groups) & (lane < n_groups + n_experts)
              & (jnp.floor((lane - n_groups) / per_grp) == gidx))
    el = jnp.where(in_grp, lg, ninf)
    emax = jnp.max(el, axis=-1, keepdims=True)
    esum = jnp.sum(jnp.exp(el - emax), axis=-1, keepdims=True)
    i1 = jnp.min(jnp.where(el == emax, lane, big), axis=-1, keepdims=True)
    el2 = jnp.where(lane == i1, ninf, el)
    emax2 = jnp.max(el2, axis=-1, keepdims=True)
    i2 = jnp.min(jnp.where(el2 == emax2, lane, big), axis=-1, keepdims=True)
    p1 = 1.0 / esum
    p2 = jnp.exp(emax2 - emax) / esum
    w1 = p1 / (p1 + p2) * p_grp
    w2 = p2 / (p1 + p2) * p_grp
    ew_ref[...] = jnp.where(lane == 0.0, w1, jnp.where(lane == 1.0, w2, 0.0))

    step = pl.program_id(0)

    @pl.when(step == 0)
    def _():
        cnt_ref[...] = jnp.zeros_like(cnt_ref)

    tm = lg.shape[0]
    hit1 = lane == i1
    hit2 = lane == i2
    sel = jnp.where(hit1 | hit2, 1.0, 0.0)
    rr = lax.broadcasted_iota(jnp.int32, (tm, tm), 0)
    cc = lax.broadcasted_iota(jnp.int32, (tm, tm), 1)
    before = jnp.where(cc < rr, 1.0, 0.0).astype(BF16)
    prior = jnp.dot(before, sel.astype(BF16), preferred_element_type=F32) + cnt_ref[...]
    rank1 = jnp.sum(jnp.where(hit1, prior, 0.0), axis=-1, keepdims=True)
    rank2 = jnp.sum(jnp.where(hit2, prior, 0.0), axis=-1, keepdims=True)
    cnt_ref[...] += jnp.sum(sel, axis=0, keepdims=True)
    cnt_out_ref[...] = cnt_ref[...]
    meta = jnp.where(lane == 0.0, i1 - n_groups,
                     jnp.where(lane == 1.0, i2 - n_groups,
                               jnp.where(lane == 2.0, rank1, jnp.where(lane == 3.0, rank2, 0.0))))
    meta_ref[0] = meta.T[0:8, :].astype(jnp.int32)


def _out_proj(mixes, w_out, x, mod, g2, w_router, b_router, n_ctx_tiles, tiles_per_req, n_groups, n_experts):
    t = mixes[0].shape[0]
    d = w_out.shape[1]
    tm = TOKEN_TILE
    n_in = len(mixes)
    width = mixes[0].shape[1]
    assert all(m.shape[1] == width and m.dtype == BF16 for m in mixes) and w_out.shape[0] == n_in * width
    mrow = functools.partial(_tile_mod_row, n_ctx_tiles=n_ctx_tiles, tiles_per_req=tiles_per_req)
    x_specs, x_args = _token_specs(x, tm, n_ctx_tiles, lambda i: i)
    in_specs = ([pl.BlockSpec((tm, width), lambda i: (i, 0)) for _ in mixes]
                + [pl.BlockSpec((width, d), lambda i, k=k: (k, 0)) for k in range(n_in)]
                + x_specs
                + [pl.BlockSpec((1, 6, d), lambda i: (mrow(i), 0, 0)),
                   pl.BlockSpec((1, d), lambda i: (0, 0)),
                   pl.BlockSpec(w_router.shape, lambda i: (0, 0)),
                   pl.BlockSpec((1, 128), lambda i: (0, 0))])
    return pl.pallas_call(
        functools.partial(_outproj_kernel, n_in=n_in, n_x=len(x_args), n_ctx_tiles=n_ctx_tiles,
                          n_groups=n_groups, n_experts=n_experts),
        grid=(t // tm,),
        in_specs=in_specs,
        out_specs=[pl.BlockSpec((tm, d), lambda i: (i, 0)),
                   pl.BlockSpec((tm, d // 128, 128), lambda i: (i, 0, 0)),
                   pl.BlockSpec((1, 8, tm), lambda i: (i, 0, 0)),
                   pl.BlockSpec((tm, 128), lambda i: (i, 0)),
                   pl.BlockSpec((1, 128), lambda i: (0, 0))],
        out_shape=[jax.ShapeDtypeStruct((t, d), F32),
                   jax.ShapeDtypeStruct((t, d // 128, 128), BF16),
                   jax.ShapeDtypeStruct((t // tm, 8, tm), jnp.int32),
                   jax.ShapeDtypeStruct((t, 128), F32),
                   jax.ShapeDtypeStruct((1, 128), F32)],
        scratch_shapes=[pltpu.VMEM((1, 128), F32)] + [pltpu.VMEM((width, d), BF16)] * n_in,
        compiler_params=_cparams("arbitrary"),
        name="out_proj_router",
    )(*mixes, *([w_out] * n_in), *x_args, mod, g2.reshape(1, d), w_router, b_router)


def _dispatch_plan(meta, counts, n_experts, tile):
    t = meta.shape[0] * meta.shape[2]
    e1, e2, rank1, rank2 = (meta[:, k, :].reshape(t) for k in range(4))
    padded = ((counts + tile - 1) // tile) * tile
    ends = jnp.cumsum(padded)
    starts = ends - padded
    experts = jnp.arange(n_experts, dtype=jnp.int32)[None, :]
    pos = [jnp.sum(jnp.where(e[:, None] == experts, starts[None, :], 0), axis=1) + r
           for e, r in ((e1, rank1), (e2, rank2))]
    n_tiles = (2 * t) // tile + n_experts
    n_used = (ends[-1] // tile).astype(jnp.int32)
    tile_start = jnp.arange(n_tiles, dtype=jnp.int32) * tile
    probe = jnp.minimum(tile_start, ends[-1] - 1)
    tile_expert = jnp.sum((ends[None, :] <= probe[:, None]).astype(jnp.int32), axis=1)
    return tile_expert, n_used.reshape(1), ends.astype(jnp.int32), pos[0], pos[1], n_tiles


def _row_source_kernel(ends_ref, p1_ref, p2_ref, o_ref, *, n_tok, n_rows, tile, n_experts):
    def clear8(j, c):
        base = j * 8
        if n_rows <= 3 * n_tok:
            base = jnp.where(base >= 2 * n_tok, base - 2 * n_tok, jnp.where(base >= n_tok, base - n_tok, base))
        else:
            base = lax.rem(base, n_tok)
        for u in range(8):
            o_ref[j * 8 + u] = base + u
        return c

    def per_expert(e, c):
        first = lax.shift_right_logical(jnp.maximum(ends_ref[e] - tile, 0), 3)
        lax.fori_loop(first, first + tile // 8, clear8, 0)
        return c

    g = pl.program_id(0)

    @pl.when(g == 0)
    def _():
        lax.fori_loop(0, n_experts, per_expert, 0)
        lax.fori_loop(lax.shift_right_logical(ends_ref[n_experts - 1], 3), n_rows // 8, clear8, 0)

    chunk = p1_ref.shape[2]

    def place(r, c):
        t = g * chunk + r
        o_ref[p1_ref[0, 0, r]] = t
        o_ref[p2_ref[0, 0, r]] = t
        return c

    lax.fori_loop(0, chunk, place, 0, unroll=8)


def _row_source(ends, pos1, pos2, n_rows, tile):
    n_tok = pos1.shape[0]
    chunk = ROW_SOURCE_CHUNK
    smem = pl.BlockSpec(memory_space=pltpu.SMEM)
    pos_spec = pl.BlockSpec((1, 1, chunk), lambda g: (g, 0, 0), memory_space=pltpu.SMEM)
    return pl.pallas_call(
        functools.partial(_row_source_kernel, n_tok=n_tok, n_rows=n_rows, tile=tile, n_experts=ends.shape[0]),
        grid=(n_tok // chunk,),
        in_specs=[smem, pos_spec, pos_spec],
        out_specs=smem,
        out_shape=jax.ShapeDtypeStruct((n_rows,), jnp.int32),
        compiler_params=_cparams("arbitrary"),
        name="moe_row_source",
    )(ends, pos1.reshape(n_tok // chunk, 1, chunk), pos2.reshape(n_tok // chunk, 1, chunk))


def _moe_kernel(te_ref, nu_ref, *refs, tm, ahead):
    tok_refs = refs[:ahead + 1]
    h_hbm, wg_ref, wu_ref, wd_ref, o_ref, wg_bf, wu_bf, wd_bf, buf, sem = refs[ahead + 1:]
    i = pl.program_id(0)
    nslot = ahead + 1
    slot = lax.rem(i, nslot)
    n_used = nu_ref[0]

    def issue(tile, tok_ref):
        s = lax.rem(tile, nslot)

        def body(j, c):
            for u in range(2):
                r = 2 * j + u
                pltpu.make_async_copy(h_hbm.at[tok_ref[0, 0, r]], buf.at[s, r], sem.at[s]).start(priority=u)
            return c

        lax.fori_loop(0, tm // 2, body, 0, unroll=4)

    @pl.when(i == 0)
    def _():
        for j in range(ahead):
            @pl.when(j < n_used)
            def _():
                issue(jnp.int32(j), tok_refs[j])

    new_expert = jnp.logical_or(i == 0, te_ref[i] != te_ref[jnp.maximum(i - 1, 0)])

    @pl.when(new_expert)
    def _():
        wg_bf[...] = wg_ref[...].astype(BF16)
        wu_bf[...] = wu_ref[...].astype(BF16)
        wd_bf[...] = wd_ref[...].astype(BF16)

    def compute(issue_next):
        s_next = lax.rem(i + ahead, nslot)
        quarter = tm // 4

        def start_rows(part):
            if issue_next:
                for r in range(part * quarter, (part + 1) * quarter):
                    pltpu.make_async_copy(h_hbm.at[tok_refs[ahead][0, 0, r]], buf.at[s_next, r],
                                          sem.at[s_next]).start(priority=r % 2)

        pltpu.make_async_copy(h_hbm.at[pl.ds(0, tm)], buf.at[slot], sem.at[slot]).wait()
        start_rows(0)
        x = buf[slot].reshape(tm, -1)
        g = jnp.dot(x, wg_bf[...], preferred_element_type=F32)
        start_rows(1)
        u = jnp.dot(x, wu_bf[...], preferred_element_type=F32)
        start_rows(2)
        act = _silu(g) * u
        y = jnp.dot(act.astype(BF16), wd_bf[...], preferred_element_type=F32)
        start_rows(3)
        o_ref[...] = y.astype(BF16).reshape(o_ref.shape)

    @pl.when(i + ahead < n_used)
    def _():
        compute(True)

    @pl.when(jnp.logical_and(i + ahead >= n_used, i < n_used))
    def _():
        compute(False)

    @pl.when(i >= n_used)
    def _():
        o_ref[...] = jnp.zeros_like(o_ref)


def _moe_experts(h3, src_tok, tile_expert, n_used, w_gate, w_up, w_down, layer_idx, tm):
    t, sub, lanes = h3.shape
    d = sub * lanes
    n_tiles = src_tok.shape[0] // tm
    f = w_gate.shape[-1]
    ahead = MOE_GATHER_AHEAD
    tok = src_tok.reshape(n_tiles, 1, tm)

    def tok_spec(k):
        return pl.BlockSpec((1, 1, tm), lambda i, te, nu: (jnp.minimum(i + k, n_tiles - 1), 0, 0),
                            memory_space=pltpu.SMEM)

    return pl.pallas_call(
        functools.partial(_moe_kernel, tm=tm, ahead=ahead),
        grid_spec=pltpu.PrefetchScalarGridSpec(
            num_scalar_prefetch=2,
            grid=(n_tiles,),
            in_specs=[tok_spec(k) for k in range(ahead + 1)] + [
                      pl.BlockSpec(memory_space=pl.ANY),
                      pl.BlockSpec((None, None, d, f), lambda i, te, nu: (layer_idx, te[i], 0, 0)),
                      pl.BlockSpec((None, None, d, f), lambda i, te, nu: (layer_idx, te[i], 0, 0)),
                      pl.BlockSpec((None, None, f, d), lambda i, te, nu: (layer_idx, te[i], 0, 0))],
            out_specs=pl.BlockSpec((tm, sub, lanes), lambda i, te, nu: (i, 0, 0)),
            scratch_shapes=[pltpu.VMEM((d, f), BF16), pltpu.VMEM((d, f), BF16), pltpu.VMEM((f, d), BF16),
                            pltpu.VMEM((ahead + 1, tm, sub, lanes), BF16),
                            pltpu.SemaphoreType.DMA((ahead + 1,))]),
        out_shape=jax.ShapeDtypeStruct((n_tiles * tm, sub, lanes), BF16),
        compiler_params=_cparams("arbitrary"),
        name="moe_experts",
    )(tile_expert, n_used, *([tok] * (ahead + 1)), h3, w_gate, w_up, w_down)


def _combine_kernel(p1_ref, p2_ref, p1n_ref, p2n_ref, y_hbm, x_ref, ew_ref, mod_ref, gf_ref, *rest,
                    tm, n_steps, n_ctx_tiles, final_norm):
    out_refs, (buf, sem) = rest[:-2], rest[-2:]
    i = pl.program_id(0)
    slot = lax.rem(i, 2)

    def issue(s, pa_ref, pb_ref):
        def body(r, c):
            pltpu.make_async_copy(y_hbm.at[pa_ref[0, 0, r]], buf.at[s, 0, r], sem.at[s, 0]).start(priority=0)
            pltpu.make_async_copy(y_hbm.at[pb_ref[0, 0, r]], buf.at[s, 1, r], sem.at[s, 1]).start(priority=1)
            return c

        lax.fori_loop(0, tm, body, 0, unroll=8)

    @pl.when(i == 0)
    def _():
        issue(0, p1_ref, p2_ref)

    @pl.when(i + 1 < n_steps)
    def _():
        issue(1 - slot, p1n_ref, p2n_ref)

    for k in range(2):
        pltpu.make_async_copy(y_hbm.at[pl.ds(0, tm)], buf.at[slot, k], sem.at[slot, k]).wait()
    ew = ew_ref[...]
    y = (ew[:, 0:1] * buf[slot, 0].reshape(tm, -1).astype(F32)
         + ew[:, 1:2] * buf[slot, 1].reshape(tm, -1).astype(F32))
    x = x_ref[...] + mod_ref[0, 5:6, :] * y
    if not final_norm:
        out_refs[0][...] = x
    else:
        x = _rms(x, gf_ref[...])
        ctx_ref, lat_ref = out_refs

        @pl.when(i < n_ctx_tiles)
        def _():
            ctx_ref[...] = x

        @pl.when(i >= n_ctx_tiles)
        def _():
            lat_ref[...] = x


def _moe_combine(y_sorted, pos1, pos2, x, ew, mod, final_g, n_ctx_tiles, tiles_per_req, final_norm):
    t, d = x.shape
    tm = TOKEN_TILE
    n_steps = t // tm
    mrow = functools.partial(_tile_mod_row, n_ctx_tiles=n_ctx_tiles, tiles_per_req=tiles_per_req)
    smem_rows = pl.BlockSpec((1, 1, tm), lambda i: (i, 0, 0), memory_space=pltpu.SMEM)
    smem_next = pl.BlockSpec((1, 1, tm), lambda i: (jnp.minimum(i + 1, n_steps - 1), 0, 0),
                             memory_space=pltpu.SMEM)
    p1 = pos1.reshape(n_steps, 1, tm)
    p2 = pos2.reshape(n_steps, 1, tm)
    if final_norm:
        out_specs = [pl.BlockSpec((tm, d), lambda i: (jnp.minimum(i, n_ctx_tiles - 1), 0)),
                     pl.BlockSpec((tm, d), lambda i: (jnp.maximum(i - n_ctx_tiles, 0), 0))]
        out_shape = [jax.ShapeDtypeStruct((n_ctx_tiles * tm, d), F32),
                     jax.ShapeDtypeStruct((t - n_ctx_tiles * tm, d), F32)]
    else:
        out_specs = pl.BlockSpec((tm, d), lambda i: (i, 0))
        out_shape = jax.ShapeDtypeStruct((t, d), F32)
    return pl.pallas_call(
        functools.partial(_combine_kernel, tm=tm, n_steps=n_steps, n_ctx_tiles=n_ctx_tiles,
                          final_norm=final_norm),
        grid=(n_steps,),
        in_specs=[smem_rows, smem_rows, smem_next, smem_next,
                  pl.BlockSpec(memory_space=pl.ANY),
                  pl.BlockSpec((tm, d), lambda i: (i, 0)),
                  pl.BlockSpec((tm, 128), lambda i: (i, 0)),
                  pl.BlockSpec((1, 6, d), lambda i: (mrow(i), 0, 0)),
                  pl.BlockSpec((1, d), lambda i: (0, 0))],
        out_specs=out_specs,
        out_shape=out_shape,
        scratch_shapes=[pltpu.VMEM((2, 2, tm) + y_sorted.shape[1:], y_sorted.dtype),
                        pltpu.SemaphoreType.DMA((2, 2))],
        compiler_params=_cparams("arbitrary"),
        name="moe_combine",
    )(p1, p2, p1, p2, y_sorted, x, ew, mod, final_g.reshape(1, d))


def _axial_angles(seq, head_dim):
    rows = seq // GRID_W
    row = jnp.repeat(jnp.arange(rows), GRID_W).astype(F32)
    col = jnp.tile(jnp.arange(GRID_W), rows).astype(F32)
    n_freq = head_dim // 4
    inv = ROPE_BASE ** (-jnp.arange(n_freq, dtype=F32) / n_freq)
    return jnp.concatenate([row[:, None] * inv, col[:, None] * inv], axis=-1)


def kernel(x_prompt, x_sample, cache_attn_k, cache_attn_v, state_hgrn, state_ret, c, c_ctx, mod_w, mod_b, norm1_g, norm2_g, even_w_in, even_w_out, hgrn_lb_logits, hgrn_norm_g, attn_qn_g, attn_kn_g, odd_w_in, odd_w_out, ret_decay_logit, ret_gn_g, moe_w_group, moe_b_group, moe_w_expert, moe_b_expert, moe_w_gate, moe_w_up, moe_w_down, final_norm_g):
    batch, seq, d = x_prompt.shape
    dec_batch, dec_seq, _ = x_sample.shape
    depth = mod_w.shape[0]
    heads_a, val_a = hgrn_norm_g.shape[1:]
    key_a = hgrn_lb_logits.shape[2] // heads_a
    hd_b = attn_qn_g.shape[1]
    kv_b = cache_attn_k.shape[3]
    heads_b = (even_w_out.shape[1] - heads_a * val_a) // hd_b
    heads_c, val_c = ret_gn_g.shape[1:]
    key_c = state_ret.shape[4]
    n_experts = moe_w_expert.shape[2]
    n_groups = moe_w_group.shape[2]
    tm = TOKEN_TILE
    n_ctx = batch * seq
    n_ctx_tiles = n_ctx // tm
    tiles_per_req = dec_seq // tm
    assert seq % tm == 0 and dec_seq % tm == 0 and n_ctx % dec_seq == 0
    assert key_a == 128 and val_a == 128 and hd_b == 128 and n_groups + n_experts <= 128
    assert d % 128 == 0 and seq % RET_CHUNK == 0 and dec_seq % RET_CHUNK == 0 and dec_seq % GRID_W == 0
    n_tok = n_ctx + dec_batch * dec_seq
    assert (2 * n_tok) % EXPERT_TILE == 0 and n_tok % ROW_SOURCE_CHUNK == 0

    x = (x_prompt.reshape(n_ctx, d), x_sample.reshape(dec_batch * dec_seq, d))
    t = n_ctx + dec_batch * dec_seq

    n_vec = 1 + dec_batch
    rows = -(-n_vec // 8) * 8
    cvecs = jnp.zeros((rows, d), F32).at[0].set(c_ctx).at[1:n_vec].set(c)
    mods = _ada_mod(cvecs, mod_w, mod_b).reshape(depth, rows, 6, d)

    lower_bounds = jnp.cumsum(jax.nn.softmax(hgrn_lb_logits.astype(F32), axis=0), axis=0)
    log_gamma = jax.nn.log_sigmoid(ret_decay_logit.astype(F32))

    ang_b = _axial_angles(dec_seq, hd_b)
    cos_b = jnp.concatenate([jnp.cos(ang_b), jnp.cos(ang_b)], axis=-1)
    sin_b = jnp.concatenate([-jnp.sin(ang_b), jnp.sin(ang_b)], axis=-1)
    ang_c = _axial_angles(dec_seq, key_c)
    cos_c, sin_c = jnp.cos(ang_c), jnp.sin(ang_c)

    pad = 128 - n_groups - n_experts
    new_k = new_v = new_sh = new_sr = None
    assert depth == 2, "one HGRN2/GQA layer followed by one retention layer"
    for l in range(depth):
        mod = mods[l]
        if l % 2 == 0:
            e = l // 2
            na, nva, nqb, nkv = heads_a * key_a, heads_a * val_a, heads_b * hd_b, kv_b * hd_b
            o_zf, o_kb = na + nva, 3 * na + 2 * nva + nqb
            w_in = even_w_in[e].astype(BF16)
            proj16, proj32 = _in_proj(x, norm1_g[l], mod, w_in, t, n_ctx_tiles, tiles_per_req, w_in.shape[1],
                                      f32_cols=((o_zf, o_zf + 2 * na), (o_kb, o_kb + 2 * nkv)))
            lb = lower_bounds[l]
            oa, s_ctx = _hgrn(proj16, proj32, lb, hgrn_norm_g[e], None, e, None, batch, seq, 0,
                              heads_a, key_a, val_a)
            oa, _ = _hgrn(proj16, proj32, lb, hgrn_norm_g[e], state_hgrn, e, oa, dec_batch, dec_seq,
                          n_ctx // dec_seq, heads_a, key_a, val_a)
            col_q, col_kv = (na + 2 * nva) // hd_b, (2 * na) // hd_b
            ob, kc, vc = _attn(proj16, proj32, attn_qn_g[e], attn_kn_g[e], None, e, None, batch, seq, 0,
                               heads_b, kv_b, hd_b, col_q, col_kv)
            (ob,) = _attn(proj16, proj32, attn_qn_g[e], attn_kn_g[e],
                          (cache_attn_k, cache_attn_v, cos_b, sin_b), e, ob,
                          dec_batch, dec_seq, n_ctx // dec_seq, heads_b, kv_b, hd_b, col_q, col_kv)
            new_k = kc.reshape(batch, 1, seq, kv_b, hd_b)
            new_v = vc.reshape(batch, 1, seq, kv_b, hd_b)
            new_sh = s_ctx[:, None]
            w_out = even_w_out[e]
            mixes = [oa, ob]
        else:
            o = l // 2
            w_in = odd_w_in[o]
            proj = _in_proj(x, norm1_g[l], mod, w_in, t, n_ctx_tiles, tiles_per_req, w_in.shape[1] // 2)
            y, s_ctx = _retention(proj, log_gamma[o], ret_gn_g[o], None, o, None, batch, seq, 0,
                                  heads_c, key_c, val_c)
            y, _ = _retention(proj, log_gamma[o], ret_gn_g[o], (state_ret, cos_c, sin_c), o, y,
                              dec_batch, dec_seq, n_ctx // dec_seq, heads_c, key_c, val_c)
            new_sr = s_ctx[:, None]
            w_out = odd_w_out[o]
            mixes = [y]

        w_router = jnp.concatenate([moe_w_group[l], moe_w_expert[l], jnp.zeros((d, pad), F32)], axis=1)
        w_router_hi = w_router.astype(BF16)
        w_router = jnp.concatenate([w_router_hi, (w_router - w_router_hi.astype(F32)).astype(BF16)], axis=1)
        b_router = jnp.concatenate([moe_b_group[l], moe_b_expert[l], jnp.zeros((pad,), F32)]).reshape(1, 128)
        xn, h2, meta, ew, cnt = _out_proj(mixes, w_out, x, mod, norm2_g[l], w_router, b_router,
                                          n_ctx_tiles, tiles_per_req, n_groups, n_experts)
        counts = cnt[0, n_groups:n_groups + n_experts].astype(jnp.int32)
        tile_expert, n_used, ends, pos1, pos2, n_tiles = _dispatch_plan(meta, counts, n_experts, EXPERT_TILE)
        src_tok = _row_source(ends, pos1, pos2, n_tiles * EXPERT_TILE, EXPERT_TILE)
        y_sorted = _moe_experts(h2, src_tok, tile_expert, n_used, moe_w_gate, moe_w_up, moe_w_down, l, EXPERT_TILE)
        x = _moe_combine(y_sorted, pos1, pos2, xn, ew, mod, final_norm_g, n_ctx_tiles, tiles_per_req,
                         final_norm=(l == depth - 1))

    y_prompt = x[0].reshape(batch, seq, d)
    y_sample = x[1].reshape(dec_batch, dec_seq, d)
    return (y_prompt, y_sample, new_k, new_v, new_sh, new_sr)
```

```python
import functools

import numpy as np
import jax
import jax.numpy as jnp
from jax import lax
from jax.experimental import pallas as pl
from jax.experimental.pallas import tpu as pltpu

F32 = jnp.float32
BF16 = jnp.bfloat16
EPS = 1e-6
ROPE_BASE = 10000.0
GRID_W = 64
HGRN_BLOCK = 128
HGRN_DIAG = 8
RET_CHUNK = 256
TOKEN_TILE = 256
INPROJ_TILE = 512
EXPERT_TILE = 256
MOE_GATHER_AHEAD = 2
ROW_SOURCE_CHUNK = 512
V7X_VMEM_LIMIT_BYTES = 56 * 1024 * 1024
HIGHEST = lax.Precision.HIGHEST
LOG2_E = 1.4426950408889634

_NT = (((1,), (1,)), ((), ()))


def _cparams(*sem):
    return pltpu.CompilerParams(dimension_semantics=sem, vmem_limit_bytes=V7X_VMEM_LIMIT_BYTES)


def _rms(x, g):
    return x * lax.rsqrt(jnp.mean(x * x, axis=-1, keepdims=True) + EPS) * g


def _silu(x):
    return x * jax.nn.sigmoid(x)


def _bdot(a, b):
    return jnp.dot(a.astype(BF16), b.astype(BF16), preferred_element_type=F32)


def _bdot_nt(a, b):
    return lax.dot_general(a.astype(BF16), b.astype(BF16), _NT, preferred_element_type=F32)


def _mod_kernel(c_ref, w_ref, b_ref, o_ref):
    a = _silu(c_ref[...])
    o_ref[0] = jnp.dot(a, w_ref[0], precision=HIGHEST, preferred_element_type=F32) + b_ref[0]


def _ada_mod(cvecs, mod_w, mod_b):
    depth, d, n = mod_w.shape
    tn = n // 4
    rows = cvecs.shape[0]
    return pl.pallas_call(
        _mod_kernel,
        grid=(depth, n // tn),
        in_specs=[pl.BlockSpec((rows, d), lambda l, j: (0, 0)),
                  pl.BlockSpec((1, d, tn), lambda l, j: (l, 0, j)),
                  pl.BlockSpec((1, 1, tn), lambda l, j: (l, 0, j))],
        out_specs=pl.BlockSpec((1, rows, tn), lambda l, j: (l, 0, j)),
        out_shape=jax.ShapeDtypeStruct((depth, rows, n), F32),
        compiler_params=_cparams("arbitrary", "arbitrary"),
        name="ada_mod",
    )(cvecs, mod_w, mod_b.reshape(depth, 1, n))


def _tile_mod_row(i, n_ctx_tiles, tiles_per_req):
    return jnp.where(i < n_ctx_tiles, 0, 1 + jnp.maximum(i - n_ctx_tiles, 0) // tiles_per_req)


def _token_specs(x, tm, n_ctx_tiles, step_of):
    if not isinstance(x, tuple):
        return [pl.BlockSpec((tm, x.shape[1]), lambda *g: (step_of(*g), 0))], (x,)
    d = x[0].shape[1]
    return ([pl.BlockSpec((tm, d), lambda *g: (jnp.minimum(step_of(*g), n_ctx_tiles - 1), 0)),
             pl.BlockSpec((tm, d), lambda *g: (jnp.maximum(step_of(*g) - n_ctx_tiles, 0), 0))], x)


def _token_rows(x_refs, i, n_ctx_tiles):
    if len(x_refs) == 1:
        return x_refs[0][...]
    return jnp.where(i < n_ctx_tiles, x_refs[0][...], x_refs[1][...])


def _inproj_kernel(*refs, n_x, n_ctx_tiles, f32_cols, cast_w):
    x_refs = refs[:n_x]
    g_ref, mod_ref, w_ref = refs[n_x:n_x + 3]
    out_refs = refs[n_x + 3:]
    if cast_w:
        out_refs, w_bf = out_refs[:-1], out_refs[-1]

        @pl.when(pl.program_id(1) == 0)
        def _():
            w_bf[...] = w_ref[...].astype(BF16)

        w_ref = w_bf
    x = _token_rows(x_refs, pl.program_id(1), n_ctx_tiles)
    h = _rms(x, g_ref[...]) * (1.0 + mod_ref[0, 1:2, :]) + mod_ref[0, 0:1, :]
    acc = jnp.dot(h.astype(BF16), w_ref[...], preferred_element_type=F32)
    if not f32_cols:
        out_refs[0][...] = acc.astype(BF16)
        return
    n = acc.shape[1]
    at16 = at32 = start = 0
    for lo, hi in tuple(f32_cols) + ((n, n),):
        if lo > start:
            out_refs[0][:, at16:at16 + lo - start] = acc[:, start:lo].astype(BF16)
            at16 += lo - start
        if hi > lo:
            out_refs[1][:, at32:at32 + hi - lo] = acc[:, lo:hi]
            at32 += hi - lo
        start = hi


def _in_proj(x, g, mod, w_bf, t, n_ctx_tiles, tiles_per_req, tn, f32_cols=()):
    d, n = w_bf.shape
    cast_w = w_bf.dtype != BF16
    tm = INPROJ_TILE
    scale = INPROJ_TILE // TOKEN_TILE
    assert n_ctx_tiles % scale == 0 and tiles_per_req % scale == 0
    n_ctx_tiles, tiles_per_req = n_ctx_tiles // scale, tiles_per_req // scale
    mrow = functools.partial(_tile_mod_row, n_ctx_tiles=n_ctx_tiles, tiles_per_req=tiles_per_req)
    x_specs, x_args = _token_specs(x, tm, n_ctx_tiles, lambda j, i: i)
    if not f32_cols:
        out_specs = pl.BlockSpec((tm, tn), lambda j, i: (i, j))
        out_shape = jax.ShapeDtypeStruct((t, n), BF16)
    else:
        assert tn == n
        n32 = sum(hi - lo for lo, hi in f32_cols)
        out_specs = [pl.BlockSpec((tm, n - n32), lambda j, i: (i, 0)),
                     pl.BlockSpec((tm, n32), lambda j, i: (i, 0))]
        out_shape = [jax.ShapeDtypeStruct((t, n - n32), BF16), jax.ShapeDtypeStruct((t, n32), F32)]
    return pl.pallas_call(
        functools.partial(_inproj_kernel, n_x=len(x_args), n_ctx_tiles=n_ctx_tiles, f32_cols=tuple(f32_cols),
                          cast_w=cast_w),
        grid=(n // tn, t // tm),
        in_specs=x_specs + [pl.BlockSpec((1, d), lambda j, i: (0, 0)),
                            pl.BlockSpec((1, 6, d), lambda j, i: (mrow(i), 0, 0)),
                            pl.BlockSpec((d, tn), lambda j, i: (0, j))],
        out_specs=out_specs,
        out_shape=out_shape,
        scratch_shapes=[pltpu.VMEM((d, tn), BF16)] if cast_w else [],
        compiler_params=_cparams("arbitrary", "arbitrary"),
        name="in_proj",
    )(*x_args, g.reshape(1, d), mod, w_bf)


def _hgrn_constants(nb, dg):
    r = np.arange(nb)[:, None]
    c = np.arange(nb)[None, :]
    levels = []
    m = nb // 2
    while m >= dg:
        levels.append(m)
        m //= 2
    owner = np.full((2, nb, nb), len(levels) + 1, np.int32)
    qrows = np.zeros((2, len(levels), nb, nb), np.int32)
    for d in (0, 1):
        for idx, m in enumerate(levels):
            same = (r // (2 * m)) == (c // (2 * m))
            r_hi = (r % (2 * m)) >= m
            c_hi = (c % (2 * m)) >= m
            later_q = r_hi if d == 0 else ~r_hi
            owner[d][same & later_q & (~c_hi if d == 0 else c_hi)] = idx
            qrows[d, idx] = np.broadcast_to(later_q, (nb, nb))
        owner[d][(r // dg) == (c // dg)] = len(levels)
    tri = np.stack([c <= r, c >= r]).astype(np.float32)
    return levels, jnp.asarray(tri, dtype=BF16), jnp.asarray(owner), jnp.asarray(qrows)


def _hgrn_decay(z, lb, tri):
    logf = jnp.log(lb + (1.0 - lb) * jax.nn.sigmoid(z))
    k = (1.0 - lb) * jax.nn.sigmoid(-z)
    l2 = logf * LOG2_E
    hi = l2.astype(BF16)
    lo = (l2 - hi.astype(F32)).astype(BF16)
    c = jnp.dot(tri, hi, preferred_element_type=F32) + jnp.dot(tri, lo, preferred_element_type=F32)
    return c, k


def _hgrn_block(q, v, st_t, reverse, c_ref, k_ref, ecat, levels, owner, qrows_ref):
    nb = HGRN_BLOCK
    c = c_ref[...]
    k = k_ref[...]
    edge = 0 if reverse else nb - 1
    c_edge = c_ref[edge:edge + 1, :]

    a = jnp.zeros((nb, nb), F32)
    for idx, m in enumerate(levels):
        pieces = []
        for grp in range(nb // (2 * m)):
            edge_row = grp * 2 * m + (m if reverse else m - 1)
            pieces.append(jnp.broadcast_to(c_ref[edge_row:edge_row + 1, :], (2 * m, nb)))
        bnd = pieces[0] if len(pieces) == 1 else jnp.concatenate(pieces, axis=0)
        e = jnp.exp2(-jnp.abs(c - bnd))
        z = (jnp.where(qrows_ref[idx] != 0, q, k) * e).astype(BF16)
        am = lax.dot_general(z, z, _NT, preferred_element_type=F32)
        a = jnp.where(owner == idx, am, a)

    dg = HGRN_DIAG
    rowi = lax.broadcasted_iota(jnp.int32, (dg, nb), 0)
    cols = []
    for s in range(dg):
        keep = (rowi <= s) if reverse else (rowi >= s)
        tiles = []
        for j in range(nb // dg):
            r = j * dg + s
            qj = q[j * dg:(j + 1) * dg, :]
            cj = c[j * dg:(j + 1) * dg, :]
            cs = jnp.broadcast_to(c_ref[r:r + 1, :], (dg, nb))
            ks = jnp.broadcast_to(k_ref[r:r + 1, :], (dg, nb))
            tiles.append(qj * ks * jnp.exp2(jnp.where(keep, cj - cs, -jnp.inf)))
        cols.append(jnp.concatenate(tiles, axis=0).astype(BF16))
    p = jnp.concatenate(cols, axis=1)
    red = jnp.dot(p, ecat, preferred_element_type=F32)
    a = jnp.where(owner == len(levels), red, a)
    o = _bdot_nt(q * jnp.exp2(c), st_t) + _bdot(a, v)
    kd = k * jnp.exp2(c_edge - c)
    st_new = st_t * jnp.exp2(c_edge) + _bdot(v.T, kd)
    return o, st_new


def _hgrn_kernel(*refs, nblk, has_state, qscale, levels):
    (qa_ref, ia_ref, zf_ref, zb_ref, ga_ref, lb_ref, ng_ref, ecat_ref, tri_ref, owner_ref,
     qrows_ref) = refs[:11]
    s0_ref = refs[11] if has_state else None
    o_ref, sfin_ref, oacc_ref, c_scr, k_scr = refs[-5:]
    nb = HGRN_BLOCK

    def gates(direction, blk, slot):
        z_ref = zb_ref if direction == 1 else zf_ref
        r0 = blk * nb if isinstance(blk, int) else pl.multiple_of(blk * nb, nb)
        c, k = _hgrn_decay(z_ref[pl.ds(r0, nb), :], lb_ref[direction:direction + 1, :], tri_ref[direction])
        c_scr[direction, slot] = c
        k_scr[direction, slot] = k

    def block(direction, blk, slot, st_t):
        r0 = blk * nb if isinstance(blk, int) else pl.multiple_of(blk * nb, nb)
        q = _silu(qa_ref[pl.ds(r0, nb), :].astype(F32)) * qscale
        v = ia_ref[pl.ds(r0, nb), :].astype(F32)
        o, st_new = _hgrn_block(q, v, st_t, direction == 1, c_scr.at[direction, slot],
                                k_scr.at[direction, slot], ecat_ref[...], levels, owner_ref[direction],
                                qrows_ref.at[direction])
        oacc_ref[direction, pl.ds(r0, nb), :] = o
        return st_new

    gates(0, 0, 0)
    gates(1, nblk - 1, 0)

    def body(i, carry):
        st_f, st_b = carry
        slot = lax.rem(i, 2)
        st_f = block(0, i, slot, st_f)
        st_b = block(1, nblk - 1 - i, slot, st_b)
        gates(0, i + 1, 1 - slot)
        gates(1, nblk - 2 - i, 1 - slot)
        return st_f, st_b

    if has_state:
        st0 = (s0_ref[0].T, s0_ref[1].T)
    else:
        zero = jnp.zeros((ia_ref.shape[1], qa_ref.shape[1]), F32)
        st0 = (zero, zero)
    st_f, st_b = lax.fori_loop(0, nblk - 1, body, st0)
    last_slot = (nblk - 1) % 2
    st_f = block(0, nblk - 1, last_slot, st_f)
    st_b = block(1, 0, last_slot, st_b)
    sfin_ref[0] = st_f.T
    sfin_ref[1] = st_b.T

    o = oacc_ref[0] + oacc_ref[1]
    o_ref[...] = (_rms(o, ng_ref[0]) * _silu(ga_ref[...].astype(F32))).astype(o_ref.dtype)


def _hgrn(proj16, proj32, lb, norm_g, s0, layer_idx, out_prev, n_seq, seq, row_blk0, heads, key_d, val_d):
    t = proj16.shape[0]
    has_state = s0 is not None
    hk = heads * key_d // 128
    off_qa, off_ia, off_ga = 0, hk, 2 * hk
    off_zf, off_zb = 0, hk

    def col(off):
        return pl.BlockSpec((seq, 128), lambda b, h: (row_blk0 + b, off + h))

    lane = np.arange(key_d)[None, :]
    srow = np.repeat(np.arange(HGRN_DIAG), key_d)[:, None]
    ecat = jnp.asarray((lane % HGRN_DIAG == srow).astype(np.float32), dtype=BF16)
    levels, tri, owner, qrows = _hgrn_constants(HGRN_BLOCK, HGRN_DIAG)
    in_specs = [col(off_qa), col(off_ia), col(off_zf), col(off_zb), col(off_ga),
                pl.BlockSpec((2, key_d), lambda b, h: (0, h)),
                pl.BlockSpec((1, 1, val_d), lambda b, h: (h, 0, 0)),
                pl.BlockSpec((HGRN_DIAG * key_d, key_d), lambda b, h: (0, 0)),
                pl.BlockSpec(tri.shape, lambda b, h: (0, 0, 0)),
                pl.BlockSpec(owner.shape, lambda b, h: (0, 0, 0)),
                pl.BlockSpec(qrows.shape, lambda b, h: (0, 0, 0, 0))]
    args = [proj16, proj16, proj32, proj32, proj16, lb, norm_g.reshape(heads, 1, val_d), ecat, tri, owner, qrows]
    if has_state:
        in_specs.append(pl.BlockSpec((None, None, 2, None, key_d, val_d),
                                     lambda b, h: (b, layer_idx, 0, h, 0, 0)))
        args.append(s0)
    aliases = {}
    if out_prev is not None:
        in_specs.append(pl.BlockSpec(memory_space=pl.ANY))
        args.append(out_prev)
        aliases = {len(args) - 1: 0}

    kern = functools.partial(_hgrn_kernel, nblk=seq // HGRN_BLOCK, has_state=has_state,
                             qscale=float(key_d) ** -0.5, levels=tuple(levels))
    return pl.pallas_call(
        kern,
        grid=(n_seq, heads),
        in_specs=in_specs,
        out_specs=[pl.BlockSpec((seq, val_d), lambda b, h: (row_blk0 + b, h)),
                   pl.BlockSpec((None, 2, None, key_d, val_d), lambda b, h: (b, 0, h, 0, 0))],
        out_shape=[jax.ShapeDtypeStruct((t, heads * val_d), BF16),
                   jax.ShapeDtypeStruct((n_seq, 2, heads, key_d, val_d), F32)],
        scratch_shapes=[pltpu.VMEM((2, seq, val_d), F32),
                        pltpu.VMEM((2, 2, HGRN_BLOCK, key_d), F32),
                        pltpu.VMEM((2, 2, HGRN_BLOCK, key_d), F32)],
        input_output_aliases=aliases,
        compiler_params=_cparams("arbitrary", "arbitrary"),
        name="hgrn2",
    )(*args)


def _rope(x, cos, sin_signed):
    return x * cos + pltpu.roll(x, x.shape[-1] // 2, 1) * sin_signed


def _attn_kernel(*refs, has_ctx, grp, qblk, scale):
    if has_ctx:
        (q_ref, k_ref, v_ref, qn_ref, kn_ref, ck_ref, cv_ref, cos_ref, sin_ref, _prev, o_ref) = refs
    else:
        (q_ref, k_ref, v_ref, qn_ref, kn_ref, o_ref, ko_ref, vo_ref) = refs
    seq, hd = k_ref.shape
    k = _rms(k_ref[...], kn_ref[...])
    v = v_ref[...]
    if has_ctx:
        k = _rope(k, cos_ref[...], sin_ref[...])
        ck = ck_ref[...].astype(BF16)
        cv = cv_ref[...].astype(BF16)
    else:
        head = pl.program_id(1)
        ko_ref[:, pl.ds(head, 1), :] = k[:, None, :]
        vo_ref[:, pl.ds(head, 1), :] = v[:, None, :]
    kb = k.astype(BF16)
    vb = v.astype(BF16)

    def body(i, carry):
        r0 = pl.multiple_of(i * qblk, qblk)
        for g in range(grp):
            q = _rms(q_ref[pl.ds(r0, qblk), g * hd:(g + 1) * hd].astype(F32), qn_ref[...])
            if has_ctx:
                q = _rope(q, cos_ref[pl.ds(r0, qblk), :], sin_ref[pl.ds(r0, qblk), :])
            qb = q.astype(BF16)
            s2 = lax.dot_general(qb, kb, _NT, preferred_element_type=F32) * scale
            m = jnp.max(s2, axis=-1, keepdims=True)
            if has_ctx:
                s1 = lax.dot_general(qb, ck, _NT, preferred_element_type=F32) * scale
                m = jnp.maximum(m, jnp.max(s1, axis=-1, keepdims=True))
                p1 = jnp.exp(s1 - m)
            p2 = jnp.exp(s2 - m)
            den = jnp.sum(p2, axis=-1, keepdims=True)
            acc = jnp.dot(p2.astype(BF16), vb, preferred_element_type=F32)
            if has_ctx:
                den = den + jnp.sum(p1, axis=-1, keepdims=True)
                acc = acc + jnp.dot(p1.astype(BF16), cv, preferred_element_type=F32)
            o_ref[pl.ds(r0, qblk), g * hd:(g + 1) * hd] = (acc / den).astype(o_ref.dtype)
        return carry

    lax.fori_loop(0, seq // qblk, body, 0)


def _attn(proj16, proj32, qn_g, kn_g, ctx, layer_idx, out_prev, n_seq, seq, row_blk0, heads, kv_heads, hd,
          col_q, col_kv):
    t = proj16.shape[0]
    grp = heads // kv_heads
    has_ctx = ctx is not None
    cq = col_q // grp
    ck0 = col_kv
    cv0 = ck0 + kv_heads
    in_specs = [pl.BlockSpec((seq, grp * hd), lambda b, h: (row_blk0 + b, cq + h)),
                pl.BlockSpec((seq, hd), lambda b, h: (row_blk0 + b, ck0 + h)),
                pl.BlockSpec((seq, hd), lambda b, h: (row_blk0 + b, cv0 + h)),
                pl.BlockSpec((1, hd), lambda b, h: (0, 0)),
                pl.BlockSpec((1, hd), lambda b, h: (0, 0))]
    args = [proj16, proj32, proj32, qn_g.reshape(1, hd), kn_g.reshape(1, hd)]
    out_specs = [pl.BlockSpec((seq, grp * hd), lambda b, h: (row_blk0 + b, h))]
    out_shape = [jax.ShapeDtypeStruct((t, heads * hd), BF16)]
    aliases = {}
    if has_ctx:
        cache_k, cache_v, cos, sin_signed = ctx
        nreq, nlay, past = cache_k.shape[:3]
        cache_k = cache_k.reshape(nreq, nlay, past, kv_heads * hd)
        cache_v = cache_v.reshape(nreq, nlay, past, kv_heads * hd)
        spec_c = pl.BlockSpec((None, None, past, hd), lambda b, h: (b, layer_idx, 0, h))
        in_specs += [spec_c, spec_c,
                     pl.BlockSpec((seq, hd), lambda b, h: (0, 0)),
                     pl.BlockSpec((seq, hd), lambda b, h: (0, 0)),
                     pl.BlockSpec(memory_space=pl.ANY)]
        args += [cache_k, cache_v, cos, sin_signed, out_prev]
        aliases = {len(args) - 1: 0}
    else:
        out_specs += [pl.BlockSpec((None, seq, kv_heads, hd), lambda b, h: (b, 0, 0, 0))] * 2
        out_shape += [jax.ShapeDtypeStruct((n_seq, seq, kv_heads, hd), F32)] * 2
    return pl.pallas_call(
        functools.partial(_attn_kernel, has_ctx=has_ctx, grp=grp, qblk=min(seq, 256), scale=float(hd) ** -0.5),
        grid=(n_seq, kv_heads),
        in_specs=in_specs,
        out_specs=out_specs,
        out_shape=out_shape,
        input_output_aliases=aliases,
        compiler_params=_cparams("arbitrary", "arbitrary"),
        name="gqa",
    )(*args)


def _ret_kernel(*refs, nchunk, has_ctx, kscale):
    if has_ctx:
        (lg_ref, q_ref, k_ref, v_ref, g_ref, gn_ref, s0_ref, cos_ref, sin_ref, _prev,
         o_ref, sfin_ref, sb_ref, krot_ref) = refs
    else:
        (lg_ref, q_ref, k_ref, v_ref, g_ref, gn_ref, o_ref, sfin_ref, sb_ref) = refs
        krot_ref = None
    cs = RET_CHUNK
    kd_dim = q_ref.shape[1]
    h = pl.program_id(1)
    lgf = lg_ref[0, h]
    lgb = lg_ref[1, h]
    row = lax.broadcasted_iota(jnp.int32, (cs, cs), 0)
    col = lax.broadcasted_iota(jnp.int32, (cs, cs), 1)
    rel = (row - col).astype(F32)
    dmat = (jnp.exp(jnp.where(rel >= 0, rel * lgf, -jnp.inf))
            + jnp.exp(jnp.where(rel <= 0, -rel * lgb, -jnp.inf)))
    j = lax.broadcasted_iota(jnp.int32, (cs, 1), 0).astype(F32)
    one = jnp.ones((1, 1), F32)
    dec_f = jnp.exp(one * (cs * lgf))
    dec_b = jnp.exp(one * (cs * lgb))

    def load_qk(ref, r0, mul):
        x = ref[pl.ds(r0, cs), :].astype(F32) * mul
        if has_ctx:
            half = kd_dim // 2
            x1, x2 = x[:, :half], x[:, half:]
            cos = cos_ref[pl.ds(r0, cs), :]
            sin = sin_ref[pl.ds(r0, cs), :]
            x = jnp.concatenate([x1 * cos - x2 * sin, x2 * cos + x1 * sin], axis=1)
        return x

    def bwd_body(i, st):
        n = nchunk - 1 - i
        r0 = pl.multiple_of(n * cs, cs)
        sb_ref[n] = st
        k = load_qk(k_ref, r0, kscale)
        if has_ctx:
            krot_ref[pl.ds(r0, cs), :] = k
        v = v_ref[pl.ds(r0, cs), :]
        return st * dec_b + _bdot((k * jnp.exp(j * lgb)).T, v)

    st_b0 = s0_ref[1] if has_ctx else jnp.zeros(sfin_ref.shape[1:], F32)
    st_b = lax.fori_loop(0, nchunk, bwd_body, st_b0)
    sfin_ref[1] = st_b

    def fwd_body(n, st):
        r0 = pl.multiple_of(n * cs, cs)
        q = load_qk(q_ref, r0, 1.0)
        k = krot_ref[pl.ds(r0, cs), :] if has_ctx else load_qk(k_ref, r0, kscale)
        v = v_ref[pl.ds(r0, cs), :]
        a = _bdot_nt(q, k) * dmat
        o = _bdot(a, v)
        if has_ctx or nchunk > 1:
            o = o + _bdot(q * jnp.exp((j + 1.0) * lgf), st)
            o = o + _bdot(q * jnp.exp((cs - j) * lgb), sb_ref[n])
        mu = jnp.mean(o, axis=-1, keepdims=True)
        var = jnp.mean(jnp.square(o - mu), axis=-1, keepdims=True)
        y = (o - mu) * lax.rsqrt(var + EPS) * gn_ref[0]
        o_ref[pl.ds(r0, cs), :] = (_silu(g_ref[pl.ds(r0, cs), :].astype(F32)) * y).astype(o_ref.dtype)
        return st * dec_f + _bdot((k * jnp.exp((cs - 1.0 - j) * lgf)).T, v)

    st_f0 = s0_ref[0] if has_ctx else jnp.zeros(sfin_ref.shape[1:], F32)
    sfin_ref[0] = lax.fori_loop(0, nchunk, fwd_body, st_f0)


def _retention(proj, log_gamma, gn_g, ctx, layer_idx, out_prev, n_seq, seq, row_blk0, heads, key_d, val_d):
    t = proj.shape[0]
    has_ctx = ctx is not None
    nq = heads * key_d
    cq, ck, cv, cg = 0, nq // key_d, 2 * nq // val_d, (2 * nq + heads * val_d) // val_d
    in_specs = [pl.BlockSpec(memory_space=pltpu.SMEM),
                pl.BlockSpec((seq, key_d), lambda b, h: (row_blk0 + b, cq + h)),
                pl.BlockSpec((seq, key_d), lambda b, h: (row_blk0 + b, ck + h)),
                pl.BlockSpec((seq, val_d), lambda b, h: (row_blk0 + b, cv + h)),
                pl.BlockSpec((seq, val_d), lambda b, h: (row_blk0 + b, cg + h)),
                pl.BlockSpec((1, 1, val_d), lambda b, h: (h, 0, 0))]
    args = [log_gamma, proj, proj, proj, proj, gn_g.reshape(heads, 1, val_d)]
    aliases = {}
    if has_ctx:
        s0, cos, sin = ctx
        in_specs += [pl.BlockSpec((None, None, 2, None, key_d, val_d), lambda b, h: (b, layer_idx, 0, h, 0, 0)),
                     pl.BlockSpec((seq, key_d // 2), lambda b, h: (0, 0)),
                     pl.BlockSpec((seq, key_d // 2), lambda b, h: (0, 0)),
                     pl.BlockSpec(memory_space=pl.ANY)]
        args += [s0, cos, sin, out_prev]
        aliases = {len(args) - 1: 0}
    nchunk = seq // RET_CHUNK
    return pl.pallas_call(
        functools.partial(_ret_kernel, nchunk=nchunk, has_ctx=has_ctx, kscale=float(key_d) ** -0.5),
        grid=(n_seq, heads),
        in_specs=in_specs,
        out_specs=[pl.BlockSpec((seq, val_d), lambda b, h: (row_blk0 + b, h)),
                   pl.BlockSpec((None, 2, None, key_d, val_d), lambda b, h: (b, 0, h, 0, 0))],
        out_shape=[jax.ShapeDtypeStruct((t, heads * val_d), BF16),
                   jax.ShapeDtypeStruct((n_seq, 2, heads, key_d, val_d), F32)],
        scratch_shapes=([pltpu.VMEM((nchunk, key_d, val_d), F32)]
                        + ([pltpu.VMEM((seq, key_d), F32)] if has_ctx else [])),
        input_output_aliases=aliases,
        compiler_params=_cparams("arbitrary", "arbitrary"),
        name="retention",
    )(*args)


def _outproj_kernel(*refs, n_in, n_x, n_ctx_tiles, n_groups, n_experts):
    mix_refs = refs[:n_in]
    w_refs = refs[n_in:2 * n_in]
    x_refs = refs[2 * n_in:2 * n_in + n_x]
    base = 2 * n_in + n_x
    (mod_ref, g2_ref, wr_ref, br_ref, xn_ref, h2_ref, meta_ref, ew_ref, cnt_out_ref,
     cnt_ref) = refs[base:base + 10]
    wbf_refs = refs[base + 10:]

    @pl.when(pl.program_id(0) == 0)
    def _():
        for wref, wbf in zip(w_refs, wbf_refs):
            wbf[...] = wref[...].astype(BF16)

    acc = jnp.dot(mix_refs[0][...], wbf_refs[0][...], preferred_element_type=F32)
    for mref, wbf in zip(mix_refs[1:], wbf_refs[1:]):
        acc = acc + jnp.dot(mref[...], wbf[...], preferred_element_type=F32)
    xn = _token_rows(x_refs, pl.program_id(0), n_ctx_tiles) + mod_ref[0, 2:3, :] * acc
    xn_ref[...] = xn
    h2 = _rms(xn, g2_ref[...]) * (1.0 + mod_ref[0, 4:5, :]) + mod_ref[0, 3:4, :]
    h2_ref[...] = h2.astype(BF16).reshape(h2_ref.shape)
    h_hi = h2.astype(BF16)
    h_lo = (h2 - h_hi.astype(F32)).astype(BF16)
    nl = br_ref.shape[1]
    hh = jnp.dot(h_hi, wr_ref[...], preferred_element_type=F32)
    lg = hh[:, :nl] + hh[:, nl:] + jnp.dot(h_lo, wr_ref[:, :nl], preferred_element_type=F32) + br_ref[...]

    per_grp = n_experts // n_groups
    lane = lax.broadcasted_iota(jnp.int32, lg.shape, 1).astype(F32)
    big = float(lg.shape[1])
    ninf = -jnp.inf
    gl = jnp.where(lane < n_groups, lg, ninf)
    gmax = jnp.max(gl, axis=-1, keepdims=True)
    p_grp = 1.0 / jnp.sum(jnp.exp(gl - gmax), axis=-1, keepdims=True)
    gidx = jnp.min(jnp.where(gl == gmax, lane, big), axis=-1, keepdims=True)
    in_grp = ((lane >= n_groups) & (lane < n_groups + n_experts)
              & (jnp.floor((lane - n_groups) / per_grp) == gidx))
    el = jnp.where(in_grp, lg, ninf)
    emax = jnp.max(el, axis=-1, keepdims=True)
    esum = jnp.sum(jnp.exp(el - emax), axis=-1, keepdims=True)
    i1 = jnp.min(jnp.where(el == emax, lane, big), axis=-1, keepdims=True)
    el2 = jnp.where(lane == i1, ninf, el)
    emax2 = jnp.max(el2, axis=-1, keepdims=True)
    i2 = jnp.min(jnp.where(el2 == emax2, lane, big), axis=-1, keepdims=True)
    p1 = 1.0 / esum
    p2 = jnp.exp(emax2 - emax) / esum
    w1 = p1 / (p1 + p2) * p_grp
    w2 = p2 / (p1 + p2) * p_grp
    ew_ref[...] = jnp.where(lane == 0.0, w1, jnp.where(lane == 1.0, w2, 0.0))

    step = pl.program_id(0)

    @pl.when(step == 0)
    def _():
        cnt_ref[...] = jnp.zeros_like(cnt_ref)

    tm = lg.shape[0]
    hit1 = lane == i1
    hit2 = lane == i2
    sel = jnp.where(hit1 | hit2, 1.0, 0.0)
    rr = lax.broadcasted_iota(jnp.int32, (tm, tm), 0)
    cc = lax.broadcasted_iota(jnp.int32, (tm, tm), 1)
    before = jnp.where(cc < rr, 1.0, 0.0).astype(BF16)
    prior = jnp.dot(before, sel.astype(BF16), preferred_element_type=F32) + cnt_ref[...]
    rank1 = jnp.sum(jnp.where(hit1, prior, 0.0), axis=-1, keepdims=True)
    rank2 = jnp.sum(jnp.where(hit2, prior, 0.0), axis=-1, keepdims=True)
    cnt_ref[...] += jnp.sum(sel, axis=0, keepdims=True)
    cnt_out_ref[...] = cnt_ref[...]
    meta = jnp.where(lane == 0.0, i1 - n_groups,
                     jnp.where(lane == 1.0, i2 - n_groups,
                               jnp.where(lane == 2.0, rank1, jnp.where(lane == 3.0, rank2, 0.0))))
    meta_ref[0] = meta.T[0:8, :].astype(jnp.int32)


def _out_proj(mixes, w_out, x, mod, g2, w_router, b_router, n_ctx_tiles, tiles_per_req, n_groups, n_experts):
    t = mixes[0].shape[0]
    d = w_out.shape[1]
    tm = TOKEN_TILE
    n_in = len(mixes)
    width = mixes[0].shape[1]
    assert all(m.shape[1] == width and m.dtype == BF16 for m in mixes) and w_out.shape[0] == n_in * width
    mrow = functools.partial(_tile_mod_row, n_ctx_tiles=n_ctx_tiles, tiles_per_req=tiles_per_req)
    x_specs, x_args = _token_specs(x, tm, n_ctx_tiles, lambda i: i)
    in_specs = ([pl.BlockSpec((tm, width), lambda i: (i, 0)) for _ in mixes]
                + [pl.BlockSpec((width, d), lambda i, k=k: (k, 0)) for k in range(n_in)]
                + x_specs
                + [pl.BlockSpec((1, 6, d), lambda i: (mrow(i), 0, 0)),
                   pl.BlockSpec((1, d), lambda i: (0, 0)),
                   pl.BlockSpec(w_router.shape, lambda i: (0, 0)),
                   pl.BlockSpec((1, 128), lambda i: (0, 0))])
    return pl.pallas_call(
        functools.partial(_outproj_kernel, n_in=n_in, n_x=len(x_args), n_ctx_tiles=n_ctx_tiles,
                          n_groups=n_groups, n_experts=n_experts),
        grid=(t // tm,),
        in_specs=in_specs,
        out_specs=[pl.BlockSpec((tm, d), lambda i: (i, 0)),
                   pl.BlockSpec((tm, d // 128, 128), lambda i: (i, 0, 0)),
                   pl.BlockSpec((1, 8, tm), lambda i: (i, 0, 0)),
                   pl.BlockSpec((tm, 128), lambda i: (i, 0)),
                   pl.BlockSpec((1, 128), lambda i: (0, 0))],
        out_shape=[jax.ShapeDtypeStruct((t, d), F32),
                   jax.ShapeDtypeStruct((t, d // 128, 128), BF16),
                   jax.ShapeDtypeStruct((t // tm, 8, tm), jnp.int32),
                   jax.ShapeDtypeStruct((t, 128), F32),
                   jax.ShapeDtypeStruct((1, 128), F32)],
        scratch_shapes=[pltpu.VMEM((1, 128), F32)] + [pltpu.VMEM((width, d), BF16)] * n_in,
        compiler_params=_cparams("arbitrary"),
        name="out_proj_router",
    )(*mixes, *([w_out] * n_in), *x_args, mod, g2.reshape(1, d), w_router, b_router)


def _dispatch_plan(meta, counts, n_experts, tile):
    t = meta.shape[0] * meta.shape[2]
    e1, e2, rank1, rank2 = (meta[:, k, :].reshape(t) for k in range(4))
    padded = ((counts + tile - 1) // tile) * tile
    ends = jnp.cumsum(padded)
    starts = ends - padded
    experts = jnp.arange(n_experts, dtype=jnp.int32)[None, :]
    pos = [jnp.sum(jnp.where(e[:, None] == experts, starts[None, :], 0), axis=1) + r
           for e, r in ((e1, rank1), (e2, rank2))]
    n_tiles = (2 * t) // tile + n_experts
    n_used = (ends[-1] // tile).astype(jnp.int32)
    tile_start = jnp.arange(n_tiles, dtype=jnp.int32) * tile
    probe = jnp.minimum(tile_start, ends[-1] - 1)
    tile_expert = jnp.sum((ends[None, :] <= probe[:, None]).astype(jnp.int32), axis=1)
    return tile_expert, n_used.reshape(1), ends.astype(jnp.int32), pos[0], pos[1], n_tiles


def _row_source_kernel(ends_ref, p1_ref, p2_ref, o_ref, *, n_tok, n_rows, tile, n_experts):
    def clear8(j, c):
        base = j * 8
        if n_rows <= 3 * n_tok:
            base = jnp.where(base >= 2 * n_tok, base - 2 * n_tok, jnp.where(base >= n_tok, base - n_tok, base))
        else:
            base = lax.rem(base, n_tok)
        for u in range(8):
            o_ref[j * 8 + u] = base + u
        return c

    def per_expert(e, c):
        first = lax.shift_right_logical(jnp.maximum(ends_ref[e] - tile, 0), 3)
        lax.fori_loop(first, first + tile // 8, clear8, 0)
        return c

    g = pl.program_id(0)

    @pl.when(g == 0)
    def _():
        lax.fori_loop(0, n_experts, per_expert, 0)
        lax.fori_loop(lax.shift_right_logical(ends_ref[n_experts - 1], 3), n_rows // 8, clear8, 0)

    chunk = p1_ref.shape[2]

    def place(r, c):
        t = g * chunk + r
        o_ref[p1_ref[0, 0, r]] = t
        o_ref[p2_ref[0, 0, r]] = t
        return c

    lax.fori_loop(0, chunk, place, 0, unroll=8)


def _row_source(ends, pos1, pos2, n_rows, tile):
    n_tok = pos1.shape[0]
    chunk = ROW_SOURCE_CHUNK
    smem = pl.BlockSpec(memory_space=pltpu.SMEM)
    pos_spec = pl.BlockSpec((1, 1, chunk), lambda g: (g, 0, 0), memory_space=pltpu.SMEM)
    return pl.pallas_call(
        functools.partial(_row_source_kernel, n_tok=n_tok, n_rows=n_rows, tile=tile, n_experts=ends.shape[0]),
        grid=(n_tok // chunk,),
        in_specs=[smem, pos_spec, pos_spec],
        out_specs=smem,
        out_shape=jax.ShapeDtypeStruct((n_rows,), jnp.int32),
        compiler_params=_cparams("arbitrary"),
        name="moe_row_source",
    )(ends, pos1.reshape(n_tok // chunk, 1, chunk), pos2.reshape(n_tok // chunk, 1, chunk))


def _moe_kernel(te_ref, nu_ref, *refs, tm, ahead):
    tok_refs = refs[:ahead + 1]
    h_hbm, wg_ref, wu_ref, wd_ref, o_ref, wg_bf, wu_bf, wd_bf, buf, sem = refs[ahead + 1:]
    i = pl.program_id(0)
    nslot = ahead + 1
    slot = lax.rem(i, nslot)
    n_used = nu_ref[0]

    def issue(tile, tok_ref):
        s = lax.rem(tile, nslot)

        def body(j, c):
            for u in range(2):
                r = 2 * j + u
                pltpu.make_async_copy(h_hbm.at[tok_ref[0, 0, r]], buf.at[s, r], sem.at[s]).start(priority=u)
            return c

        lax.fori_loop(0, tm // 2, body, 0, unroll=4)

    @pl.when(i == 0)
    def _():
        for j in range(ahead):
            @pl.when(j < n_used)
            def _():
                issue(jnp.int32(j), tok_refs[j])

    new_expert = jnp.logical_or(i == 0, te_ref[i] != te_ref[jnp.maximum(i - 1, 0)])

    @pl.when(new_expert)
    def _():
        wg_bf[...] = wg_ref[...].astype(BF16)
        wu_bf[...] = wu_ref[...].astype(BF16)
        wd_bf[...] = wd_ref[...].astype(BF16)

    def compute(issue_next):
        s_next = lax.rem(i + ahead, nslot)
        quarter = tm // 4

        def start_rows(part):
            if issue_next:
                for r in range(part * quarter, (part + 1) * quarter):
                    pltpu.make_async_copy(h_hbm.at[tok_refs[ahead][0, 0, r]], buf.at[s_next, r],
                                          sem.at[s_next]).start(priority=r % 2)

        pltpu.make_async_copy(h_hbm.at[pl.ds(0, tm)], buf.at[slot], sem.at[slot]).wait()
        start_rows(0)
        x = buf[slot].reshape(tm, -1)
        g = jnp.dot(x, wg_bf[...], preferred_element_type=F32)
        start_rows(1)
        u = jnp.dot(x, wu_bf[...], preferred_element_type=F32)
        start_rows(2)
        act = _silu(g) * u
        y = jnp.dot(act.astype(BF16), wd_bf[...], preferred_element_type=F32)
        start_rows(3)
        o_ref[...] = y.astype(BF16).reshape(o_ref.shape)

    @pl.when(i + ahead < n_used)
    def _():
        compute(True)

    @pl.when(jnp.logical_and(i + ahead >= n_used, i < n_used))
    def _():
        compute(False)

    @pl.when(i >= n_used)
    def _():
        o_ref[...] = jnp.zeros_like(o_ref)


def _moe_experts(h3, src_tok, tile_expert, n_used, w_gate, w_up, w_down, layer_idx, tm):
    t, sub, lanes = h3.shape
    d = sub * lanes
    n_tiles = src_tok.shape[0] // tm
    f = w_gate.shape[-1]
    ahead = MOE_GATHER_AHEAD
    tok = src_tok.reshape(n_tiles, 1, tm)

    def tok_spec(k):
        return pl.BlockSpec((1, 1, tm), lambda i, te, nu: (jnp.minimum(i + k, n_tiles - 1), 0, 0),
                            memory_space=pltpu.SMEM)

    return pl.pallas_call(
        functools.partial(_moe_kernel, tm=tm, ahead=ahead),
        grid_spec=pltpu.PrefetchScalarGridSpec(
            num_scalar_prefetch=2,
            grid=(n_tiles,),
            in_specs=[tok_spec(k) for k in range(ahead + 1)] + [
                      pl.BlockSpec(memory_space=pl.ANY),
                      pl.BlockSpec((None, None, d, f), lambda i, te, nu: (layer_idx, te[i], 0, 0)),
                      pl.BlockSpec((None, None, d, f), lambda i, te, nu: (layer_idx, te[i], 0, 0)),
                      pl.BlockSpec((None, None, f, d), lambda i, te, nu: (layer_idx, te[i], 0, 0))],
            out_specs=pl.BlockSpec((tm, sub, lanes), lambda i, te, nu: (i, 0, 0)),
            scratch_shapes=[pltpu.VMEM((d, f), BF16), pltpu.VMEM((d, f), BF16), pltpu.VMEM((f, d), BF16),
                            pltpu.VMEM((ahead + 1, tm, sub, lanes), BF16),
                            pltpu.SemaphoreType.DMA((ahead + 1,))]),
        out_shape=jax.ShapeDtypeStruct((n_tiles * tm, sub, lanes), BF16),
        compiler_params=_cparams("arbitrary"),
        name="moe_experts",
    )(tile_expert, n_used, *([tok] * (ahead + 1)), h3, w_gate, w_up, w_down)


def _combine_kernel(p1_ref, p2_ref, p1n_ref, p2n_ref, y_hbm, x_ref, ew_ref, mod_ref, gf_ref, *rest,
                    tm, n_steps, n_ctx_tiles, final_norm):
    out_refs, (buf, sem) = rest[:-2], rest[-2:]
    i = pl.program_id(0)
    slot = lax.rem(i, 2)

    def issue(s, pa_ref, pb_ref):
        def body(r, c):
            pltpu.make_async_copy(y_hbm.at[pa_ref[0, 0, r]], buf.at[s, 0, r], sem.at[s, 0]).start(priority=0)
            pltpu.make_async_copy(y_hbm.at[pb_ref[0, 0, r]], buf.at[s, 1, r], sem.at[s, 1]).start(priority=1)
            return c

        lax.fori_loop(0, tm, body, 0, unroll=8)

    @pl.when(i == 0)
    def _():
        issue(0, p1_ref, p2_ref)

    @pl.when(i + 1 < n_steps)
    def _():
        issue(1 - slot, p1n_ref, p2n_ref)

    for k in range(2):
        pltpu.make_async_copy(y_hbm.at[pl.ds(0, tm)], buf.at[slot, k], sem.at[slot, k]).wait()
    ew = ew_ref[...]
    y = (ew[:, 0:1] * buf[slot, 0].reshape(tm, -1).astype(F32)
         + ew[:, 1:2] * buf[slot, 1].reshape(tm, -1).astype(F32))
    x = x_ref[...] + mod_ref[0, 5:6, :] * y
    if not final_norm:
        out_refs[0][...] = x
    else:
        x = _rms(x, gf_ref[...])
        ctx_ref, lat_ref = out_refs

        @pl.when(i < n_ctx_tiles)
        def _():
            ctx_ref[...] = x

        @pl.when(i >= n_ctx_tiles)
        def _():
            lat_ref[...] = x


def _moe_combine(y_sorted, pos1, pos2, x, ew, mod, final_g, n_ctx_tiles, tiles_per_req, final_norm):
    t, d = x.shape
    tm = TOKEN_TILE
    n_steps = t // tm
    mrow = functools.partial(_tile_mod_row, n_ctx_tiles=n_ctx_tiles, tiles_per_req=tiles_per_req)
    smem_rows = pl.BlockSpec((1, 1, tm), lambda i: (i, 0, 0), memory_space=pltpu.SMEM)
    smem_next = pl.BlockSpec((1, 1, tm), lambda i: (jnp.minimum(i + 1, n_steps - 1), 0, 0),
                             memory_space=pltpu.SMEM)
    p1 = pos1.reshape(n_steps, 1, tm)
    p2 = pos2.reshape(n_steps, 1, tm)
    if final_norm:
        out_specs = [pl.BlockSpec((tm, d), lambda i: (jnp.minimum(i, n_ctx_tiles - 1), 0)),
                     pl.BlockSpec((tm, d), lambda i: (jnp.maximum(i - n_ctx_tiles, 0), 0))]
        out_shape = [jax.ShapeDtypeStruct((n_ctx_tiles * tm, d), F32),
                     jax.ShapeDtypeStruct((t - n_ctx_tiles * tm, d), F32)]
    else:
        out_specs = pl.BlockSpec((tm, d), lambda i: (i, 0))
        out_shape = jax.ShapeDtypeStruct((t, d), F32)
    return pl.pallas_call(
        functools.partial(_combine_kernel, tm=tm, n_steps=n_steps, n_ctx_tiles=n_ctx_tiles,
                          final_norm=final_norm),
        grid=(n_steps,),
        in_specs=[smem_rows, smem_rows, smem_next, smem_next,
                  pl.BlockSpec(memory_space=pl.ANY),
                  pl.BlockSpec((tm, d), lambda i: (i, 0)),
                  pl.BlockSpec((tm, 128), lambda i: (i, 0)),
                  pl.BlockSpec((1, 6, d), lambda i: (mrow(i), 0, 0)),
                  pl.BlockSpec((1, d), lambda i: (0, 0))],
        out_specs=out_specs,
        out_shape=out_shape,
        scratch_shapes=[pltpu.VMEM((2, 2, tm) + y_sorted.shape[1:], y_sorted.dtype),
                        pltpu.SemaphoreType.DMA((2, 2))],
        compiler_params=_cparams("arbitrary"),
        name="moe_combine",
    )(p1, p2, p1, p2, y_sorted, x, ew, mod, final_g.reshape(1, d))


def _axial_angles(seq, head_dim):
    rows = seq // GRID_W
    row = jnp.repeat(jnp.arange(rows), GRID_W).astype(F32)
    col = jnp.tile(jnp.arange(GRID_W), rows).astype(F32)
    n_freq = head_dim // 4
    inv = ROPE_BASE ** (-jnp.arange(n_freq, dtype=F32) / n_freq)
    return jnp.concatenate([row[:, None] * inv, col[:, None] * inv], axis=-1)


def kernel(x_prompt, x_sample, cache_attn_k, cache_attn_v, state_hgrn, state_ret, c, c_ctx, mod_w, mod_b, norm1_g, norm2_g, even_w_in, even_w_out, hgrn_lb_logits, hgrn_norm_g, attn_qn_g, attn_kn_g, odd_w_in, odd_w_out, ret_decay_logit, ret_gn_g, moe_w_group, moe_b_group, moe_w_expert, moe_b_expert, moe_w_gate, moe_w_up, moe_w_down, final_norm_g):
    batch, seq, d = x_prompt.shape
    dec_batch, dec_seq, _ = x_sample.shape
    depth = mod_w.shape[0]
    heads_a, val_a = hgrn_norm_g.shape[1:]
    key_a = hgrn_lb_logits.shape[2] // heads_a
    hd_b = attn_qn_g.shape[1]
    kv_b = cache_attn_k.shape[3]
    heads_b = (even_w_out.shape[1] - heads_a * val_a) // hd_b
    heads_c, val_c = ret_gn_g.shape[1:]
    key_c = state_ret.shape[4]
    n_experts = moe_w_expert.shape[2]
    n_groups = moe_w_group.shape[2]
    tm = TOKEN_TILE
    n_ctx = batch * seq
    n_ctx_tiles = n_ctx // tm
    tiles_per_req = dec_seq // tm
    assert seq % tm == 0 and dec_seq % tm == 0 and n_ctx % dec_seq == 0
    assert key_a == 128 and val_a == 128 and hd_b == 128 and n_groups + n_experts <= 128
    assert d % 128 == 0 and seq % RET_CHUNK == 0 and dec_seq % RET_CHUNK == 0 and dec_seq % GRID_W == 0
    n_tok = n_ctx + dec_batch * dec_seq
    assert (2 * n_tok) % EXPERT_TILE == 0 and n_tok % ROW_SOURCE_CHUNK == 0

    x = (x_prompt.reshape(n_ctx, d), x_sample.reshape(dec_batch * dec_seq, d))
    t = n_ctx + dec_batch * dec_seq

    n_vec = 1 + dec_batch
    rows = -(-n_vec // 8) * 8
    cvecs = jnp.zeros((rows, d), F32).at[0].set(c_ctx).at[1:n_vec].set(c)
    mods = _ada_mod(cvecs, mod_w, mod_b).reshape(depth, rows, 6, d)

    lower_bounds = jnp.cumsum(jax.nn.softmax(hgrn_lb_logits.astype(F32), axis=0), axis=0)
    log_gamma = jax.nn.log_sigmoid(ret_decay_logit.astype(F32))

    ang_b = _axial_angles(dec_seq, hd_b)
    cos_b = jnp.concatenate([jnp.cos(ang_b), jnp.cos(ang_b)], axis=-1)
    sin_b = jnp.concatenate([-jnp.sin(ang_b), jnp.sin(ang_b)], axis=-1)
    ang_c = _axial_angles(dec_seq, key_c)
    cos_c, sin_c = jnp.cos(ang_c), jnp.sin(ang_c)

    pad = 128 - n_groups - n_experts
    new_k = new_v = new_sh = new_sr = None
    assert depth == 2, "one HGRN2/GQA layer followed by one retention layer"
    for l in range(depth):
        mod = mods[l]
        if l % 2 == 0:
            e = l // 2
            na, nva, nqb, nkv = heads_a * key_a, heads_a * val_a, heads_b * hd_b, kv_b * hd_b
            o_zf, o_kb = na + nva, 3 * na + 2 * nva + nqb
            w_in = even_w_in[e].astype(BF16)
            proj16, proj32 = _in_proj(x, norm1_g[l], mod, w_in, t, n_ctx_tiles, tiles_per_req, w_in.shape[1],
                                      f32_cols=((o_zf, o_zf + 2 * na), (o_kb, o_kb + 2 * nkv)))
            lb = lower_bounds[l]
            oa, s_ctx = _hgrn(proj16, proj32, lb, hgrn_norm_g[e], None, e, None, batch, seq, 0,
                              heads_a, key_a, val_a)
            oa, _ = _hgrn(proj16, proj32, lb, hgrn_norm_g[e], state_hgrn, e, oa, dec_batch, dec_seq,
                          n_ctx // dec_seq, heads_a, key_a, val_a)
            col_q, col_kv = (na + 2 * nva) // hd_b, (2 * na) // hd_b
            ob, kc, vc = _attn(proj16, proj32, attn_qn_g[e], attn_kn_g[e], None, e, None, batch, seq, 0,
                               heads_b, kv_b, hd_b, col_q, col_kv)
            (ob,) = _attn(proj16, proj32, attn_qn_g[e], attn_kn_g[e],
                          (cache_attn_k, cache_attn_v, cos_b, sin_b), e, ob,
                          dec_batch, dec_seq, n_ctx // dec_seq, heads_b, kv_b, hd_b, col_q, col_kv)
            new_k = kc.reshape(batch, 1, seq, kv_b, hd_b)
            new_v = vc.reshape(batch, 1, seq, kv_b, hd_b)
            new_sh = s_ctx[:, None]
            w_out = even_w_out[e]
            mixes = [oa, ob]
        else:
            o = l // 2
            w_in = odd_w_in[o]
            proj = _in_proj(x, norm1_g[l], mod, w_in, t, n_ctx_tiles, tiles_per_req, w_in.shape[1] // 2)
            y, s_ctx = _retention(proj, log_gamma[o], ret_gn_g[o], None, o, None, batch, seq, 0,
                                  heads_c, key_c, val_c)
            y, _ = _retention(proj, log_gamma[o], ret_gn_g[o], (state_ret, cos_c, sin_c), o, y,
                              dec_batch, dec_seq, n_ctx // dec_seq, heads_c, key_c, val_c)
            new_sr = s_ctx[:, None]
            w_out = odd_w_out[o]
            mixes = [y]

        w_router = jnp.concatenate([moe_w_group[l], moe_w_expert[l], jnp.zeros((d, pad), F32)], axis=1)
        w_router_hi = w_router.astype(BF16)
        w_router = jnp.concatenate([w_router_hi, (w_router - w_router_hi.astype(F32)).astype(BF16)], axis=1)
        b_router = jnp.concatenate([moe_b_group[l], moe_b_expert[l], jnp.zeros((pad,), F32)]).reshape(1, 128)
        xn, h2, meta, ew, cnt = _out_proj(mixes, w_out, x, mod, norm2_g[l], w_router, b_router,
                                          n_ctx_tiles, tiles_per_req, n_groups, n_experts)
        counts = cnt[0, n_groups:n_groups + n_experts].astype(jnp.int32)
        tile_expert, n_used, ends, pos1, pos2, n_tiles = _dispatch_plan(meta, counts, n_experts, EXPERT_TILE)
        src_tok = _row_source(ends, pos1, pos2, n_tiles * EXPERT_TILE, EXPERT_TILE)
        y_sorted = _moe_experts(h2, src_tok, tile_expert, n_used, moe_w_gate, moe_w_up, moe_w_down, l, EXPERT_TILE)
        x = _moe_combine(y_sorted, pos1, pos2, xn, ew, mod, final_norm_g, n_ctx_tiles, tiles_per_req,
                         final_norm=(l == depth - 1))

    y_prompt = x[0].reshape(batch, seq, d)
    y_sample = x[1].reshape(dec_batch, dec_seq, d)
    return (y_prompt, y_sample, new_k, new_v, new_sh, new_sr)
```
